```python
import math
import jax, jax.numpy as jnp
from jax import lax
import numpy as np

D_MODEL = 1024
BATCH = 4
SEQ = 4096
DEPTH = 2

D_MIX = D_MODEL
EPS = 1e-6
NEG = -1e30
FORCE = 1e6
F_FLOOR = 1e-30
NSA_HEADS = 8
NSA_KV_GROUPS = 2
NSA_HD = 64
CMP_LEN = 32
CMP_STRIDE = 16
CMP_HID = 256
SEL_BLOCK = 64
SEL_TOPK = 16
SEL_QBLOCK = 32
WINDOW = 512
WIN_QBLOCK = 128
HG_HEADS = 4
HG_DK = 128
HG_DV = 128
HG_CHUNK = 64
D_FF = 2752

NSA_WIDTH = NSA_HEADS * NSA_HD
HG_WIDTH = HG_HEADS * HG_DV
KV_W = NSA_KV_GROUPS * NSA_HD
IN_SIZES = (NSA_WIDTH, KV_W, KV_W, KV_W, KV_W, KV_W, KV_W, NSA_HEADS * 3,
            HG_HEADS * HG_DK, HG_HEADS * HG_DK, HG_WIDTH, HG_WIDTH)
IN_COLS = sum(IN_SIZES)
IN_SPLITS = tuple(int(v) for v in np.cumsum(IN_SIZES)[:-1])

kernel_name = "hymba_nsa_hgrn2_macaron_alibi"


def rms_norm(x, g):
    xf = x.astype(jnp.float32)
    y = xf * lax.rsqrt(jnp.mean(xf * xf, axis=-1, keepdims=True) + EPS)
    return (y * g.astype(jnp.float32)).astype(x.dtype)


def swiglu(x, w_gu, w_down):
    gate, up = jnp.split(x @ w_gu, 2, axis=-1)
    return (jax.nn.silu(gate) * up) @ w_down


def alibi_slopes(n):
    return jnp.asarray([2.0 ** (-8.0 * (i + 1) / n) for i in range(n)], dtype=jnp.float32)


def compress(kv, pos_emb, w1, w2):
    B, S, G, dk = kv.shape
    chunks = kv.reshape(B, S // CMP_STRIDE, CMP_STRIDE, G, dk)
    blocks = jnp.concatenate([chunks[:, :-1], chunks[:, 1:]], axis=2) + pos_emb[:, None, :]
    nb = blocks.shape[1]
    flat = blocks.transpose(0, 1, 3, 2, 4).reshape(B, nb, G, CMP_LEN * dk)
    return jax.nn.gelu(flat @ w1) @ w2


def nsa_group(q, k_c, v_c, k_s, v_s, k_w, v_w, gate_logits,
              cmp_pos_k, cmp_pos_v, cmp_k_w1, cmp_k_w2, cmp_v_w1, cmp_v_w2):
    B, S = q.shape[:2]
    G, Hg, dk = NSA_KV_GROUPS, NSA_HEADS // NSA_KV_GROUPS, NSA_HD
    scale = dk ** -0.5
    slopes = alibi_slopes(NSA_HEADS).reshape(G, Hg)
    q5 = q.reshape(B, S, G, Hg, dk)
    kv4 = lambda t: t.reshape(B, S, G, dk)
    pos = jnp.arange(S, dtype=jnp.int32)

    kc = compress(kv4(k_c), cmp_pos_k, cmp_k_w1, cmp_k_w2)
    vc = compress(kv4(v_c), cmp_pos_v, cmp_v_w1, cmp_v_w2)
    nb = kc.shape[1]
    blk_end = jnp.arange(nb, dtype=jnp.int32) * CMP_STRIDE + (CMP_LEN - 1)
    dist_c = pos[:, None] - blk_end[None, :]
    s_c = jnp.einsum('bsgnd,bcgd->bsgnc', q5, kc).astype(jnp.float32) * scale
    s_c = s_c - slopes[:, :, None] * dist_c.astype(jnp.float32)[:, None, None, :]
    s_c = jnp.where((dist_c >= 0)[:, None, None, :], s_c, NEG)
    p_c = jax.nn.softmax(s_c, axis=-1)
    p_c = jnp.where((pos >= CMP_LEN - 1)[:, None, None, None], p_c, 0.0)
    o_cmp = jnp.einsum('bsgnc,bcgd->bsgnd', p_c.astype(vc.dtype), vc)

    imp = p_c.sum(axis=3)
    padded = jnp.pad(imp, ((0, 0), (0, 0), (0, 0), (1, 1)))
    chunk_imp = padded[..., :-1] + padded[..., 1:]
    ns = S // SEL_BLOCK
    p_slc = chunk_imp.reshape(B, S, G, ns, SEL_BLOCK // CMP_STRIDE).sum(-1)
    blk = jnp.arange(ns, dtype=jnp.int32)
    cur = pos // SEL_BLOCK
    forced = (blk[None] == 0) | (blk[None] == cur[:, None]) | (blk[None] == cur[:, None] - 1)
    causal = blk[None] * SEL_BLOCK <= pos[:, None]
    score = jnp.where(forced[:, None], FORCE, jnp.where(causal[:, None], p_slc, NEG))
    n_sel = min(SEL_TOPK, ns)
    _, sel_idx = lax.top_k(score, n_sel)

    k_blocks = kv4(k_s).reshape(B, ns, SEL_BLOCK, G, dk).transpose(0, 3, 1, 2, 4)
    v_blocks = kv4(v_s).reshape(B, ns, SEL_BLOCK, G, dk).transpose(0, 3, 1, 2, 4)
    nq = S // SEL_QBLOCK
    b_ix = jnp.arange(B)[:, None, None, None]
    g_ix = jnp.arange(G)[None, None, :, None]
    l_ar = jnp.arange(SEL_BLOCK, dtype=jnp.int32)

    def sel_step(args):
        qb, idx, qi = args
        kg = k_blocks[b_ix, g_ix, idx]
        vg = v_blocks[b_ix, g_ix, idx]
        s = jnp.einsum('btgnd,btgkld->btgnkl', qb, kg).astype(jnp.float32) * scale
        qpos = qi * SEL_QBLOCK + jnp.arange(SEL_QBLOCK, dtype=jnp.int32)
        kpos = idx[..., None] * SEL_BLOCK + l_ar
        dist = qpos[None, :, None, None, None] - kpos
        s = s - slopes[None, None, :, :, None, None] * dist.astype(jnp.float32)[:, :, :, None]
        s = jnp.where((dist >= 0)[:, :, :, None], s, NEG)
        p = jax.nn.softmax(s.reshape(s.shape[:4] + (-1,)), axis=-1).reshape(s.shape)
        return jnp.einsum('btgnkl,btgkld->btgnd', p.astype(vg.dtype), vg)

    o_sel = lax.map(sel_step, (q5.reshape(B, nq, SEL_QBLOCK, G, Hg, dk).swapaxes(0, 1),
                               sel_idx.reshape(B, nq, SEL_QBLOCK, G, n_sel).swapaxes(0, 1),
                               jnp.arange(nq, dtype=jnp.int32)))
    o_sel = o_sel.swapaxes(0, 1).reshape(B, S, G, Hg, dk)

    nwq = S // WIN_QBLOCK
    n_kb = WINDOW // WIN_QBLOCK + 1
    kw_len = n_kb * WIN_QBLOCK

    def band(t):
        tp = jnp.pad(kv4(t), ((0, 0), (WINDOW, 0), (0, 0), (0, 0)))
        tp = tp.reshape(B, nwq + WINDOW // WIN_QBLOCK, WIN_QBLOCK, G, dk)
        return jnp.concatenate([tp[:, j:j + nwq] for j in range(n_kb)], axis=2).swapaxes(0, 1)

    def win_step(args):
        qb, kb, vb, qi = args
        s = jnp.einsum('btgnd,bkgd->btgnk', qb, kb).astype(jnp.float32) * scale
        qpos = qi * WIN_QBLOCK + jnp.arange(WIN_QBLOCK, dtype=jnp.int32)
        kpos = qi * WIN_QBLOCK - WINDOW + jnp.arange(kw_len, dtype=jnp.int32)
        dist = qpos[:, None] - kpos[None, :]
        valid = (dist >= 0) & (dist < WINDOW) & (kpos >= 0)[None, :]
        s = s - slopes[None, None, :, :, None] * dist.astype(jnp.float32)[None, :, None, None, :]
        s = jnp.where(valid[None, :, None, None, :], s, NEG)
        p = jax.nn.softmax(s, axis=-1)
        return jnp.einsum('btgnk,bkgd->btgnd', p.astype(vb.dtype), vb)

    o_win = lax.map(win_step, (q5.reshape(B, nwq, WIN_QBLOCK, G, Hg, dk).swapaxes(0, 1),
                               band(k_w), band(v_w), jnp.arange(nwq, dtype=jnp.int32)))
    o_win = o_win.swapaxes(0, 1).reshape(B, S, G, Hg, dk)

    g = jax.nn.sigmoid(gate_logits.astype(jnp.float32)).reshape(B, S, G, Hg, 3)
    o = (g[..., 0:1] * o_cmp.astype(jnp.float32) + g[..., 1:2] * o_sel.astype(jnp.float32)
         + g[..., 2:3] * o_win.astype(jnp.float32))
    return o.reshape(B, S, NSA_WIDTH).astype(q.dtype)


def hgrn2_group(q, f_logit, i_in, g_in, lower_bound, out_norm):
    B, S = q.shape[:2]
    H, dk, dv, C = HG_HEADS, HG_DK, HG_DV, HG_CHUNK
    qf = jax.nn.silu(q.astype(jnp.float32)).reshape(B, S, H, dk)
    lb = lower_bound.astype(jnp.float32)
    z = f_logit.astype(jnp.float32)
    f = lb + (1.0 - lb) * jax.nn.sigmoid(z)
    logf = jnp.log(jnp.maximum(f, F_FLOOR)).reshape(B, S, H, dk)
    kf = ((1.0 - lb) * jax.nn.sigmoid(-z)).reshape(B, S, H, dk)
    vf = i_in.astype(jnp.float32).reshape(B, S, H, dv)
    nc = S // C
    to_chunks = lambda t: t.reshape(B, nc, C, H, t.shape[-1]).transpose(1, 0, 3, 2, 4)
    tril = jnp.tril(jnp.ones((C, C), dtype=bool))

    def step(state, inp):
        qc, kc, vc, lf = inp
        b = jnp.cumsum(lf, axis=2)
        decay = b[:, :, :, None, :] - b[:, :, None, :, :]
        decay = jnp.exp(jnp.where(tril[:, :, None], decay, NEG))
        a = jnp.einsum('bhtd,bhtsd,bhsd->bhts', qc, decay, kc)
        o_intra = jnp.einsum('bhts,bhsv->bhtv', a, vc)
        o_inter = jnp.einsum('bhtd,bhdv->bhtv', qc * jnp.exp(b), state)
        b_last = b[:, :, -1]
        new_state = (jnp.exp(b_last)[..., None] * state
                     + jnp.einsum('bhsd,bhsv->bhdv', kc * jnp.exp(b_last[:, :, None] - b), vc))
        return new_state, o_intra + o_inter

    state0 = jnp.zeros((B, H, dk, dv), jnp.float32)
    _, o = lax.scan(step, state0, (to_chunks(qf), to_chunks(kf), to_chunks(vf), to_chunks(logf)))
    o = o.transpose(1, 0, 3, 2, 4).reshape(B, S, H, dv)
    o = o * lax.rsqrt(jnp.mean(o * o, axis=-1, keepdims=True) + EPS) * out_norm.astype(jnp.float32)
    o = o.reshape(B, S, HG_WIDTH) * jax.nn.silu(g_in.astype(jnp.float32))
    return o.astype(q.dtype)


def hybrid_mixer(h, w_in, cmp_pos_k, cmp_pos_v, cmp_k_w1, cmp_k_w2, cmp_v_w1, cmp_v_w2,
                 lower_bound, hg_norm, w_out):
    parts = jnp.split(h @ w_in, IN_SPLITS, axis=-1)
    nq, k_c, v_c, k_s, v_s, k_w, v_w, gl, hq, hf, hi, hg = parts
    o_nsa = nsa_group(nq, k_c, v_c, k_s, v_s, k_w, v_w, gl,
                      cmp_pos_k, cmp_pos_v, cmp_k_w1, cmp_k_w2, cmp_v_w1, cmp_v_w2)
    o_hg = hgrn2_group(hq, hf, hi, hg, lower_bound, hg_norm)
    return jnp.concatenate([o_nsa, o_hg], axis=-1) @ w_out


def setup_inputs(seed: int = 0) -> dict:
    key = jax.random.key(seed)
    ks = jax.random.split(key, 20)
    nrm = lambda k, shape, s: jax.random.normal(k, shape, jnp.float32) * s
    gain = lambda k, shape: 1.0 + 0.02 * jax.random.normal(k, shape, jnp.float32)
    L = DEPTH
    return {
        "x": jax.random.normal(ks[0], (BATCH, SEQ, D_MODEL), jnp.float32),
        "ffn1_norm": gain(ks[1], (L, D_MODEL)),
        "ffn1_w_gu": nrm(ks[2], (L, D_MODEL, 2 * D_FF), D_MODEL ** -0.5),
        "ffn1_w_down": nrm(ks[3], (L, D_FF, D_MODEL), D_FF ** -0.5),
        "mix_norm": gain(ks[4], (L, D_MODEL)),
        "w_in": nrm(ks[5], (L, D_MODEL, IN_COLS), D_MODEL ** -0.5),
        "cmp_pos_k": nrm(ks[6], (L, CMP_LEN, NSA_HD), 0.02),
        "cmp_pos_v": nrm(ks[7], (L, CMP_LEN, NSA_HD), 0.02),
        "cmp_k_w1": nrm(ks[8], (L, CMP_LEN * NSA_HD, CMP_HID), (CMP_LEN * NSA_HD) ** -0.5),
        "cmp_k_w2": nrm(ks[9], (L, CMP_HID, NSA_HD), CMP_HID ** -0.5),
        "cmp_v_w1": nrm(ks[10], (L, CMP_LEN * NSA_HD, CMP_HID), (CMP_LEN * NSA_HD) ** -0.5),
        "cmp_v_w2": nrm(ks[11], (L, CMP_HID, NSA_HD), CMP_HID ** -0.5),
        "hgrn_lower_bound": nrm(ks[12], (L, HG_HEADS * HG_DK), 0.1),
        "hgrn_out_norm": gain(ks[13], (L, HG_DV)),
        "w_out": nrm(ks[14], (L, D_MIX, D_MODEL), D_MIX ** -0.5),
        "ffn2_norm": gain(ks[15], (L, D_MODEL)),
        "ffn2_w_gu": nrm(ks[16], (L, D_MODEL, 2 * D_FF), D_MODEL ** -0.5),
        "ffn2_w_down": nrm(ks[17], (L, D_FF, D_MODEL), D_FF ** -0.5),
        "final_norm": gain(ks[18], (D_MODEL,)),
    }


def reference(x, ffn1_norm, ffn1_w_gu, ffn1_w_down, mix_norm, w_in, cmp_pos_k, cmp_pos_v,
              cmp_k_w1, cmp_k_w2, cmp_v_w1, cmp_v_w2, hgrn_lower_bound, hgrn_out_norm, w_out,
              ffn2_norm, ffn2_w_gu, ffn2_w_down, final_norm):
    lb_sm = jax.nn.softmax(hgrn_lower_bound.astype(jnp.float32), axis=0)
    lower_bounds = jnp.cumsum(lb_sm, axis=0) - lb_sm[0]
    h = x
    for l in range(DEPTH):
        h = h + 0.5 * swiglu(rms_norm(h, ffn1_norm[l]), ffn1_w_gu[l], ffn1_w_down[l])
        h = h + hybrid_mixer(rms_norm(h, mix_norm[l]), w_in[l], cmp_pos_k[l], cmp_pos_v[l],
                             cmp_k_w1[l], cmp_k_w2[l], cmp_v_w1[l], cmp_v_w2[l],
                             lower_bounds[l], hgrn_out_norm[l], w_out[l])
        h = h + 0.5 * swiglu(rms_norm(h, ffn2_norm[l]), ffn2_w_gu[l], ffn2_w_down[l])
    return rms_norm(h, final_norm)
```

```python
import functools

import jax
import jax.numpy as jnp
import numpy as np
from jax import lax
from jax.experimental import pallas as pl
from jax.experimental.pallas import tpu as pltpu

F32 = jnp.float32
BF16 = jnp.bfloat16

D_MODEL = 1024
EPS = 1e-6
NEG = -1e30
FORCE = 1e6
F_FLOOR = 1e-30
NSA_HEADS = 8
NSA_KV_GROUPS = 2
HEADS_PER_GROUP = NSA_HEADS // NSA_KV_GROUPS
NSA_HD = 64
CMP_LEN = 32
CMP_STRIDE = 16
CMP_HID = 256
SEL_BLOCK = 64
SEL_TOPK = 16
WINDOW = 512
HG_HEADS = 4
HG_DK = 128
HG_DV = 128
HG_CHUNK = 64
D_FF = 2752
NSA_WIDTH = NSA_HEADS * NSA_HD
HG_WIDTH = HG_HEADS * HG_DV
KV_W = NSA_KV_GROUPS * NSA_HD

LANES = 128
D_FF_PAD = 2816
FF_CHUNK = 256
ROW_TILE = 512
ATT_TQ = 256
ATT_TK = 256
HG_TOKENS = 512
VMEM_LIMIT = 56 * 1024 * 1024

ALIBI_SLOPES = tuple(2.0 ** (-8.0 * (i + 1) / NSA_HEADS) for i in range(NSA_HEADS))

SEGMENTS = (
    ("q", NSA_HEADS * LANES, BF16),
    ("ks", KV_W, BF16),
    ("v2s", 2 * KV_W, BF16),
    ("kw", KV_W, BF16),
    ("v2w", 2 * KV_W, BF16),
    ("kc", KV_W, F32),
    ("vc", KV_W, F32),
    ("gl", LANES, F32),
    ("hg", 4 * HG_WIDTH, F32),
)
SEG_OFFSETS = tuple(int(v) for v in np.cumsum([0] + [s[1] for s in SEGMENTS]))
IN_COLS_EXT = SEG_OFFSETS[-1]


def _nn(a, b):
    return jnp.dot(a, b, preferred_element_type=F32)


def _nt(a, b):
    return lax.dot_general(a, b, (((1,), (1,)), ((), ())), preferred_element_type=F32)


def _tn(a, b):
    return lax.dot_general(a, b, (((0,), (0,)), ((), ())), preferred_element_type=F32)


def _split2(x):
    hi = x.astype(BF16)
    lo = (x - hi.astype(F32)).astype(BF16)
    return hi, lo


def _split3(x):
    hi = x.astype(BF16)
    r = x - hi.astype(F32)
    mid = r.astype(BF16)
    lo = (r - mid.astype(F32)).astype(BF16)
    return hi, mid, lo


def _dot3(a, b):
    ah, al = _split2(a)
    bh, bl = _split2(b)
    return _nn(ah, bh) + _nn(ah, bl) + _nn(al, bh)


def _sigmoid(x):
    return 1.0 / (1.0 + jnp.exp(-x))


def _rms(x, g):
    return x * lax.rsqrt(jnp.mean(x * x, axis=-1, keepdims=True) + EPS) * g


def _resident(shape):
    nd = len(shape)
    return pl.BlockSpec(shape, lambda *_: (0,) * nd, pipeline_mode=pl.Buffered(1))


def _params(sem):
    return pltpu.CompilerParams(dimension_semantics=sem, vmem_limit_bytes=VMEM_LIMIT)


def _ffn_kernel(x_ref, g_ref, wg_ref, wu_ref, wd_ref, gf_ref, o_ref, *, final):
    x = x_ref[...]
    xn = _rms(x, g_ref[...]).astype(BF16)
    acc = jnp.zeros(x.shape, F32)
    for k in range(D_FF_PAD // FF_CHUNK):
        sl = slice(k * FF_CHUNK, (k + 1) * FF_CHUNK)
        gate = _nn(xn, wg_ref[:, sl])
        up = _nn(xn, wu_ref[:, sl])
        h = (gate * _sigmoid(gate) * up).astype(BF16)
        acc = acc + _nn(h, wd_ref[sl, :])
    y = x + 0.5 * acc
    if final:
        y = _rms(y, gf_ref[...])
    o_ref[...] = y


def _ffn(h, norm_g, w_gu, w_down, final_g, final):
    t = h.shape[0]
    pad = D_FF_PAD - D_FF
    wg = jnp.pad(w_gu[:, :D_FF], ((0, 0), (0, pad))).astype(BF16)
    wu = jnp.pad(w_gu[:, D_FF:], ((0, 0), (0, pad))).astype(BF16)
    wd = jnp.pad(w_down, ((0, pad), (0, 0))).astype(BF16)
    row = pl.BlockSpec((ROW_TILE, D_MODEL), lambda i: (i, 0))
    return pl.pallas_call(
        functools.partial(_ffn_kernel, final=final),
        grid=(t // ROW_TILE,),
        in_specs=[row, _resident((1, D_MODEL)), _resident(wg.shape), _resident(wu.shape),
                  _resident(wd.shape), _resident((1, D_MODEL))],
        out_specs=row,
        out_shape=jax.ShapeDtypeStruct((t, D_MODEL), F32),
        compiler_params=_params(("parallel",)),
        name="ffn",
    )(h, norm_g.reshape(1, -1), wg, wu, wd, final_g.reshape(1, -1))


def _inproj_kernel(x_ref, g_ref, w_ref, *o_refs):
    xn = _rms(x_ref[...], g_ref[...]).astype(BF16)
    for o_ref, (_, width, dtype), start in zip(o_refs, SEGMENTS, SEG_OFFSETS[:-1], strict=True):
        o_ref[...] = _nn(xn, w_ref[:, start:start + width]).astype(dtype)


def _build_w_in(w_in):
    sizes = (NSA_WIDTH, KV_W, KV_W, KV_W, KV_W, KV_W, KV_W, NSA_HEADS * 3,
             HG_WIDTH, HG_WIDTH, HG_WIDTH, HG_WIDTH)
    splits = [int(v) for v in np.cumsum(sizes)[:-1]]
    wq, wkc, wvc, wks, wvs, wkw, wvw, wgl, whq, whf, whi, whg = jnp.split(w_in, splits, axis=1)
    d = w_in.shape[0]
    scale = NSA_HD ** -0.5
    zero = jnp.zeros((d, NSA_HD), w_in.dtype)
    q_cols = []
    for h in range(NSA_HEADS):
        w = wq[:, h * NSA_HD:(h + 1) * NSA_HD] * scale
        q_cols += [w, zero] if h // HEADS_PER_GROUP == 0 else [zero, w]
    swap = lambda w: jnp.concatenate([w, w[:, NSA_HD:], w[:, :NSA_HD]], axis=1)
    gl = wgl.reshape(d, NSA_HEADS, 3).transpose(0, 2, 1).reshape(d, 3 * NSA_HEADS)
    gl = jnp.pad(gl, ((0, 0), (0, LANES - 3 * NSA_HEADS)))
    cols = q_cols + [wks, swap(wvs), wkw, swap(wvw), wkc, wvc, gl, whq, whf, whi, whg]
    return jnp.concatenate(cols, axis=1).astype(BF16)


def _inproj(h, norm_g, w_ext):
    t = h.shape[0]
    row = lambda w: pl.BlockSpec((ROW_TILE, w), lambda i: (i, 0))
    return pl.pallas_call(
        _inproj_kernel,
        grid=(t // ROW_TILE,),
        in_specs=[row(D_MODEL), _resident((1, D_MODEL)), _resident(w_ext.shape)],
        out_specs=[row(w) for _, w, _ in SEGMENTS],
        out_shape=[jax.ShapeDtypeStruct((t, w), dt) for _, w, dt in SEGMENTS],
        compiler_params=_params(("parallel",)),
        name="inproj",
    )(h, norm_g.reshape(1, -1), w_ext)


def _outproj_kernel(a_ref, b_ref, w_ref, res_ref, o_ref):
    o_ref[...] = (res_ref[...] + _nn(a_ref[...], w_ref[:NSA_WIDTH, :])
                  + _nn(b_ref[...], w_ref[NSA_WIDTH:, :]))


def _outproj(o_nsa, o_hg, w_out, res):
    t = res.shape[0]
    row = lambda w: pl.BlockSpec((ROW_TILE, w), lambda i: (i, 0))
    return pl.pallas_call(
        _outproj_kernel,
        grid=(t // ROW_TILE,),
        in_specs=[row(NSA_WIDTH), row(HG_WIDTH), _resident(w_out.shape), row(D_MODEL)],
        out_specs=row(D_MODEL),
        out_shape=jax.ShapeDtypeStruct((t, D_MODEL), F32),
        compiler_params=_params(("parallel",)),
        name="outproj",
    )(o_nsa, o_hg, w_out.astype(BF16), res)


def _gelu_tanh(x):
    return 0.5 * x * (1.0 + jnp.tanh(0.7978845608028654 * (x + 0.044715 * (x * x * x))))


def _compress_kernel(x_ref, pos_ref, w1_ref, w2_ref, o_ref):
    half = CMP_STRIDE * NSA_HD
    nbp = x_ref.shape[2]
    out = jnp.zeros(o_ref.shape[1:], F32)
    for g in range(NSA_KV_GROUPS):
        x = x_ref[0, g]
        ha = _dot3(x + pos_ref[:, :half], w1_ref[:half, :])
        hb = _dot3(x + pos_ref[:, half:], w1_ref[half:, :])
        act = _gelu_tanh(ha + pltpu.roll(hb, nbp - 1, axis=0))
        out = out + _dot3(act, w2_ref[g])
    o_ref[0] = out


def _compress(kv, pos, w1, w2, reps):
    b, s, _ = kv.shape
    nbp = s // CMP_STRIDE
    x = kv.reshape(b, nbp, CMP_STRIDE, NSA_KV_GROUPS, NSA_HD).transpose(0, 3, 1, 2, 4)
    x = x.reshape(b, NSA_KV_GROUPS, nbp, CMP_STRIDE * NSA_HD)
    zero = jnp.zeros_like(w2)
    w2p = jnp.stack([jnp.concatenate([w2 if r == g else zero for r in reps], axis=1)
                     for g in range(NSA_KV_GROUPS)])
    width = len(reps) * NSA_HD
    return pl.pallas_call(
        _compress_kernel,
        grid=(b,),
        in_specs=[pl.BlockSpec((1,) + x.shape[1:], lambda i: (i, 0, 0, 0)),
                  _resident((1, CMP_LEN * NSA_HD)), _resident(w1.shape), _resident(w2p.shape)],
        out_specs=pl.BlockSpec((1, nbp, width), lambda i: (i, 0, 0)),
        out_shape=jax.ShapeDtypeStruct((b, nbp, width), F32),
        compiler_params=_params(("parallel",)),
        name="compress",
    )(x, pos.reshape(1, -1), w1, w2p)


def _pair_blocks(g, jj):
    lower = slice(0, LANES) if g == 0 else slice(LANES, 2 * LANES)
    upper = slice(LANES, 2 * LANES) if g == 0 else slice(0, LANES)
    return lower, upper


def _cmp_kernel(q_ref, kc_ref, vc2_ref, mt_ref, ocmp_ref, sel_ref, sc_scr, *, tq, ns, n_sel):
    nbp = kc_ref.shape[1]
    q0 = pl.program_id(1) * tq
    pos = q0 + lax.broadcasted_iota(jnp.int32, (tq, nbp), 0)
    blk_end = lax.broadcasted_iota(jnp.int32, (tq, nbp), 1) * CMP_STRIDE + (CMP_LEN - 1)
    dist = pos - blk_end
    valid = dist >= 0
    distf = dist.astype(F32)
    row_ok = pos >= CMP_LEN - 1
    kc_hi, kc_lo = _split2(kc_ref[0])
    vc2 = vc2_ref[0].astype(BF16)
    lane_lo = lax.broadcasted_iota(jnp.int32, (tq, LANES), 1) < NSA_HD
    blk = lax.broadcasted_iota(jnp.int32, (ns, tq), 0)
    pos_t = q0 + lax.broadcasted_iota(jnp.int32, (ns, tq), 1)
    cur = pos_t // SEL_BLOCK
    forced = (blk == 0) | (blk == cur) | (blk == cur - 1)
    causal = blk * SEL_BLOCK <= pos_t
    sel_t = []
    for g in range(NSA_KV_GROUPS):
        imp = jnp.zeros((tq, nbp), F32)
        acc = []
        for hh in range(HEADS_PER_GROUP):
            h = g * HEADS_PER_GROUP + hh
            qh = q_ref[0, :, h * LANES:(h + 1) * LANES]
            s = _nt(qh, kc_hi) + _nt(qh, kc_lo)
            s = jnp.where(valid, s - ALIBI_SLOPES[h] * distf, NEG)
            e = jnp.exp(s - jnp.max(s, axis=-1, keepdims=True))
            p = e / jnp.sum(e, axis=-1, keepdims=True)
            p = jnp.where(row_ok, p, 0.0)
            imp = imp + p
            acc.append(_nn(p.astype(BF16), vc2))
        for jj in range(HEADS_PER_GROUP // 2):
            lower, upper = _pair_blocks(g, jj)
            blk_out = jnp.where(lane_lo, acc[2 * jj][:, lower], acc[2 * jj + 1][:, upper])
            c0 = (g * HEADS_PER_GROUP + 2 * jj) * NSA_HD
            ocmp_ref[0, :, c0:c0 + LANES] = blk_out
        p_slc = sum(_nt(mt_ref[...], part) for part in _split3(imp))
        sc_scr[...] = jnp.where(forced, FORCE, jnp.where(causal, p_slc, NEG))
        score = sc_scr[...]
        rank = jnp.zeros((ns, tq), F32)
        for j in range(ns):
            row = sc_scr[pl.ds(j, 1), :]
            ge = jnp.where(row >= score, 1.0, 0.0)
            gt = jnp.where(row > score, 1.0, 0.0)
            rank = rank + jnp.where(blk > j, ge, gt)
        sel_t.append(jnp.where(rank < n_sel, 1.0, 0.0))
    sel_ref[0] = jnp.concatenate(sel_t, axis=0).T.astype(BF16)


def _importance_matrix(nbp, ns):
    per = SEL_BLOCK // CMP_STRIDE
    m = np.zeros((ns, nbp), np.float32)
    for n in range(ns):
        for c in range(per * n, per * (n + 1)):
            for cc in (c - 1, c):
                if 0 <= cc < nbp - 1:
                    m[n, cc] += 1.0
    return jnp.asarray(m, BF16)


def _cmp_attention(q, kc, vc2, b, s):
    nbp = s // CMP_STRIDE
    ns = s // SEL_BLOCK
    tq = ATT_TQ
    q3 = q.reshape(b, s, NSA_HEADS * LANES)
    kern = functools.partial(_cmp_kernel, tq=tq, ns=ns, n_sel=min(SEL_TOPK, ns))
    return pl.pallas_call(
        kern,
        grid=(b, s // tq),
        in_specs=[pl.BlockSpec((1, tq, NSA_HEADS * LANES), lambda i, j: (i, j, 0)),
                  pl.BlockSpec((1, nbp, KV_W), lambda i, j: (i, 0, 0)),
                  pl.BlockSpec((1, nbp, 2 * KV_W), lambda i, j: (i, 0, 0)),
                  _resident((ns, nbp))],
        out_specs=[pl.BlockSpec((1, tq, NSA_WIDTH), lambda i, j: (i, j, 0)),
                   pl.BlockSpec((1, tq, NSA_KV_GROUPS * ns), lambda i, j: (i, j, 0))],
        out_shape=[jax.ShapeDtypeStruct((b, s, NSA_WIDTH), F32),
                   jax.ShapeDtypeStruct((b, s, NSA_KV_GROUPS * ns), BF16)],
        scratch_shapes=[pltpu.VMEM((ns, tq), F32)],
        compiler_params=_params(("parallel", "parallel")),
        name="cmp_topk",
    )(q3, kc, vc2, _importance_matrix(nbp, ns))


def _attn_kernel(q_ref, ks_ref, v2s_ref, kw_ref, v2w_ref, sel_ref, gl_ref, ocmp_ref, o_ref,
                 m_scr, l_scr, acc_scr, *, tq, tk, ns):
    i = pl.program_id(1)
    gates = _sigmoid(gl_ref[0])
    rel = (lax.broadcasted_iota(jnp.int32, (tq, tk), 1)
           - lax.broadcasted_iota(jnp.int32, (tq, tk), 0)).astype(F32)
    sel = sel_ref[0]
    n_sel_lanes = NSA_KV_GROUPS * ns
    e_row = lax.broadcasted_iota(jnp.int32, (n_sel_lanes, tk), 0)
    e_col = lax.broadcasted_iota(jnp.int32, (n_sel_lanes, tk), 1) // SEL_BLOCK
    lane_lo = lax.broadcasted_iota(jnp.int32, (tq, LANES), 1) < NSA_HD
    rows_of = lambda hh: slice(hh * tq, (hh + 1) * tq)

    for g in range(NSA_KV_GROUPS):
        heads = [g * HEADS_PER_GROUP + hh for hh in range(HEADS_PER_GROUP)]
        qs = [q_ref[0, :, h * LANES:(h + 1) * LANES] for h in heads]
        for br, (k_ref, v_ref) in enumerate(((ks_ref, v2s_ref), (kw_ref, v2w_ref))):
            m_scr[br] = jnp.full(m_scr.shape[1:], NEG, F32)
            l_scr[br] = jnp.zeros(l_scr.shape[1:], F32)
            acc_scr[br] = jnp.zeros(acc_scr.shape[1:], F32)

            def body(j, carry, br=br, k_ref=k_ref, v_ref=v_ref):
                k = k_ref[0, pl.ds(pl.multiple_of(j * tk, tk), tk), :]
                v = v_ref[0, pl.ds(pl.multiple_of(j * tk, tk), tk), :]
                relj = rel + (j * tk - i * tq).astype(F32)
                if br == 0:
                    expand = (e_row == g * ns + j * (tk // SEL_BLOCK) + e_col)
                    picked = _nn(sel, jnp.where(expand, 1.0, 0.0).astype(BF16))
                    valid = (picked > 0.5) & (relj <= 0.0)
                else:
                    valid = (relj <= 0.0) & (relj > -float(WINDOW))
                for hh in range(HEADS_PER_GROUP):
                    r = rows_of(hh)
                    s = _nt(qs[hh], k) + ALIBI_SLOPES[heads[hh]] * relj
                    s = jnp.where(valid, s, NEG)
                    m_prev = m_scr[br, r]
                    m_new = jnp.maximum(m_prev, jnp.max(s, axis=-1, keepdims=True))
                    alpha = jnp.exp(m_prev - m_new)
                    p = jnp.where(valid, jnp.exp(s - m_new), 0.0)
                    l_scr[br, r] = alpha * l_scr[br, r] + jnp.sum(p, axis=-1, keepdims=True)
                    acc_scr[br, r] = alpha * acc_scr[br, r] + _nn(p.astype(BF16), v)
                    m_scr[br, r] = m_new
                return carry

            lo = 0 if br == 0 else jnp.maximum(i - WINDOW // tk, 0)
            lax.fori_loop(lo, i + 1, body, 0)

        for jj in range(HEADS_PER_GROUP // 2):
            lower, upper = _pair_blocks(g, jj)
            h0 = heads[2 * jj]
            c0 = h0 * NSA_HD
            gate = lambda br, h: gates[:, br * NSA_HEADS + h:br * NSA_HEADS + h + 1]
            blk = jnp.where(lane_lo, gate(0, h0), gate(0, h0 + 1)) * ocmp_ref[0, :, c0:c0 + LANES]
            for br in range(2):
                ra, rb = rows_of(2 * jj), rows_of(2 * jj + 1)
                oa = acc_scr[br, ra][:, lower] / l_scr[br, ra]
                ob = acc_scr[br, rb][:, upper] / l_scr[br, rb]
                blk = blk + jnp.where(lane_lo, gate(br + 1, h0) * oa, gate(br + 1, h0 + 1) * ob)
            o_ref[0, :, c0:c0 + LANES] = blk.astype(o_ref.dtype)


def _attention(q, ks, v2s, kw, v2w, sel, gl, ocmp, b, s):
    ns = s // SEL_BLOCK
    tq, tk = ATT_TQ, ATT_TK
    r3 = lambda a: a.reshape(b, s, a.shape[-1])
    tile = lambda w: pl.BlockSpec((1, tq, w), lambda i, j: (i, j, 0))
    full = lambda w: pl.BlockSpec((1, s, w), lambda i, j: (i, 0, 0))
    rows = HEADS_PER_GROUP * tq
    kern = functools.partial(_attn_kernel, tq=tq, tk=tk, ns=ns)
    return pl.pallas_call(
        kern,
        grid=(b, s // tq),
        in_specs=[tile(NSA_HEADS * LANES), full(KV_W), full(2 * KV_W), full(KV_W), full(2 * KV_W),
                  tile(NSA_KV_GROUPS * ns), tile(LANES), tile(NSA_WIDTH)],
        out_specs=tile(NSA_WIDTH),
        out_shape=jax.ShapeDtypeStruct((b, s, NSA_WIDTH), BF16),
        scratch_shapes=[pltpu.VMEM((2, rows, 1), F32), pltpu.VMEM((2, rows, 1), F32),
                        pltpu.VMEM((2, rows, 2 * KV_W), F32)],
        compiler_params=_params(("parallel", "parallel")),
        name="sel_win_attention",
    )(r3(q), r3(ks), r3(v2s), r3(kw), r3(v2w), sel, r3(gl), ocmp)


def _hgrn_kernel(hq_ref, hf_ref, hi_ref, hg_ref, lbraw_ref, onorm_ref, o_ref, st_scr, b_scr,
                 *, layer, n_chunks):
    c = HG_CHUNK

    @pl.when(pl.program_id(1) == 0)
    def _():
        st_scr[...] = jnp.zeros(st_scr.shape, F32)

    raw = lbraw_ref[...]
    ex = jnp.exp(raw - jnp.max(raw, axis=0, keepdims=True))
    sm = ex / jnp.sum(ex, axis=0, keepdims=True)
    lb_all = jnp.zeros((1, raw.shape[1]), F32)
    for l in range(1, layer + 1):
        lb_all = lb_all + sm[l:l + 1, :]

    t_idx = lax.broadcasted_iota(jnp.int32, (c, HG_DK), 0)
    sub = lax.broadcasted_iota(jnp.int32, (8, HG_DK), 0)
    ti = lax.broadcasted_iota(jnp.int32, (c, c), 0)
    si = lax.broadcasted_iota(jnp.int32, (c, c), 1)
    tril = jnp.where(si <= ti, 1.0, 0.0).astype(BF16)
    onorm = onorm_ref[...]

    def ref_rows(m):
        if m >= 4:
            return jnp.concatenate(
                [jnp.broadcast_to(b_scr[pl.ds(s0 + m - 1, 1), :], (2 * m, HG_DK))
                 for s0 in range(0, c, 2 * m)], axis=0)
        pieces = []
        for s0 in range(0, c, 8):
            r1 = jnp.broadcast_to(b_scr[pl.ds(s0 + 1, 1), :], (8, HG_DK))
            r5 = jnp.broadcast_to(b_scr[pl.ds(s0 + 5, 1), :], (8, HG_DK))
            pieces.append(jnp.where(sub < 4, r1, r5))
        return jnp.concatenate(pieces, axis=0)

    def chunk(ci, carry):
        rows = pl.ds(pl.multiple_of(ci * c, c), c)
        for h in range(HG_HEADS):
            lanes = slice(h * HG_DK, (h + 1) * HG_DK)
            lb = lb_all[:, lanes]
            z = hf_ref[rows, lanes]
            hq = hq_ref[rows, lanes]
            q = hq * _sigmoid(hq)
            f = jnp.maximum(lb + (1.0 - lb) * _sigmoid(z), F_FLOOR)
            logf = jnp.log(f)
            k = (1.0 - lb) * _sigmoid(-z)
            v = hi_ref[rows, lanes].astype(BF16)
            bcum = sum(_nn(tril, part) for part in _split3(logf))
            b_scr[...] = bcum

            a = jnp.where(ti == si, jnp.sum(q * k, axis=-1, keepdims=True), 0.0)
            for m in (32, 16, 8, 4, 2, 1):
                second = (t_idx & m) != 0
                if m == 1:
                    w = jnp.where(second, f, 1.0)
                else:
                    ref = ref_rows(m)
                    w = jnp.exp(jnp.where(second, bcum - ref, ref - bcum))
                qs = jnp.where(second, q * w, 0.0).astype(BF16)
                ks = jnp.where(second, 0.0, k * w).astype(BF16)
                same = (ti // (2 * m)) == (si // (2 * m))
                a = a + jnp.where(same, _nt(qs, ks), 0.0)

            st = st_scr[h]
            o = _nn(a.astype(BF16), v) + _nt((q * jnp.exp(bcum)).astype(BF16), st.astype(BF16))
            b_last = b_scr[pl.ds(c - 1, 1), :]
            kd = (k * jnp.exp(b_last - bcum)).astype(BF16)
            st_scr[h] = jnp.exp(b_last) * st + _tn(v, kd)

            o = o * lax.rsqrt(jnp.mean(o * o, axis=-1, keepdims=True) + EPS) * onorm
            gate = hg_ref[rows, lanes]
            o_ref[rows, lanes] = (o * (gate * _sigmoid(gate))).astype(o_ref.dtype)
        return carry

    lax.fori_loop(0, n_chunks, chunk, 0)


def _hgrn(hg4, lb_raw, out_norm, layer, b, s):
    tt = HG_TOKENS
    per = s // tt
    col = lambda k: pl.BlockSpec((tt, HG_WIDTH), lambda i, j, k=k: (i * per + j, k))
    kern = functools.partial(_hgrn_kernel, layer=layer, n_chunks=tt // HG_CHUNK)
    return pl.pallas_call(
        kern,
        grid=(b, per),
        in_specs=[col(0), col(1), col(2), col(3), _resident(lb_raw.shape), _resident((1, HG_DV))],
        out_specs=pl.BlockSpec((tt, HG_WIDTH), lambda i, j: (i * per + j, 0)),
        out_shape=jax.ShapeDtypeStruct((b * s, HG_WIDTH), BF16),
        scratch_shapes=[pltpu.VMEM((HG_HEADS, HG_DV, HG_DK), F32), pltpu.VMEM((HG_CHUNK, HG_DK), F32)],
        compiler_params=_params(("parallel", "arbitrary")),
        name="hgrn2",
    )(hg4, hg4, hg4, hg4, lb_raw, out_norm.reshape(1, -1))


def _mixer(h, l, b, s, mix_norm, w_in, cmp_pos_k, cmp_pos_v, cmp_k_w1, cmp_k_w2, cmp_v_w1, cmp_v_w2,
           hgrn_lower_bound, hgrn_out_norm, w_out):
    q, ks, v2s, kw, v2w, kc_in, vc_in, gl, hg4 = _inproj(h, mix_norm[l], _build_w_in(w_in[l]))
    kc = _compress(kc_in.reshape(b, s, KV_W), cmp_pos_k[l], cmp_k_w1[l], cmp_k_w2[l], (0, 1))
    vc2 = _compress(vc_in.reshape(b, s, KV_W), cmp_pos_v[l], cmp_v_w1[l], cmp_v_w2[l], (0, 1, 1, 0))
    ocmp, sel = _cmp_attention(q, kc, vc2, b, s)
    o_nsa = _attention(q, ks, v2s, kw, v2w, sel, gl, ocmp, b, s)
    o_hg = _hgrn(hg4, hgrn_lower_bound, hgrn_out_norm[l], l, b, s)
    return _outproj(o_nsa.reshape(b * s, NSA_WIDTH), o_hg, w_out[l], h)


def kernel(x, ffn1_norm, ffn1_w_gu, ffn1_w_down, mix_norm, w_in, cmp_pos_k, cmp_pos_v, cmp_k_w1, cmp_k_w2, cmp_v_w1, cmp_v_w2, hgrn_lower_bound, hgrn_out_norm, w_out, ffn2_norm, ffn2_w_gu, ffn2_w_down, final_norm):
    b, s, d = x.shape
    depth = ffn1_norm.shape[0]
    h = x.reshape(b * s, d)
    for l in range(depth):
        h = _ffn(h, ffn1_norm[l], ffn1_w_gu[l], ffn1_w_down[l], final_norm, False)
        h = _mixer(h, l, b, s, mix_norm, w_in, cmp_pos_k, cmp_pos_v, cmp_k_w1, cmp_k_w2, cmp_v_w1,
                   cmp_v_w2, hgrn_lower_bound, hgrn_out_norm, w_out)
        h = _ffn(h, ffn2_norm[l], ffn2_w_gu[l], ffn2_w_down[l], final_norm, l == depth - 1)
    return h.reshape(b, s, d)
```

```python
import functools

import jax
import jax.numpy as jnp
import numpy as np
from jax import lax
from jax.experimental import pallas as pl
from jax.experimental.pallas import tpu as pltpu

F32 = jnp.float32
BF16 = jnp.bfloat16

D_MODEL = 1024
EPS = 1e-6
NEG = -1e30
FORCE = 1e6
F_FLOOR = 1e-30
NSA_HEADS = 8
NSA_KV_GROUPS = 2
HEADS_PER_GROUP = NSA_HEADS // NSA_KV_GROUPS
NSA_HD = 64
CMP_LEN = 32
CMP_STRIDE = 16
CMP_HID = 256
SEL_BLOCK = 64
SEL_TOPK = 16
WINDOW = 512
HG_HEADS = 4
HG_DK = 128
HG_DV = 128
HG_CHUNK = 64
D_FF = 2752
NSA_WIDTH = NSA_HEADS * NSA_HD
HG_WIDTH = HG_HEADS * HG_DV
KV_W = NSA_KV_GROUPS * NSA_HD

LANES = 128
D_FF_PAD = 2816
FF_CHUNK = 256
ROW_TILE = 512
ATT_TQ = 256
ATT_TK = 256
HG_TOKENS = 512
VMEM_LIMIT = 56 * 1024 * 1024

SEL_PENALTY = 2.0 ** 50
ALIBI_SLOPES = tuple(2.0 ** (-8.0 * (i + 1) / NSA_HEADS) for i in range(NSA_HEADS))

SEGMENTS = (
    ("q", NSA_HEADS * LANES, BF16),
    ("ks", KV_W, BF16),
    ("v2s", 2 * KV_W, BF16),
    ("kw", KV_W, BF16),
    ("v2w", 2 * KV_W, BF16),
    ("kc", KV_W, F32),
    ("vc", KV_W, F32),
    ("gl", LANES, F32),
    ("hg", 4 * HG_WIDTH, F32),
)
SEG_OFFSETS = tuple(int(v) for v in np.cumsum([0] + [s[1] for s in SEGMENTS]))
IN_COLS_EXT = SEG_OFFSETS[-1]


def _nn(a, b):
    return jnp.dot(a, b, preferred_element_type=F32)


def _nt(a, b):
    return lax.dot_general(a, b, (((1,), (1,)), ((), ())), preferred_element_type=F32)


def _tn(a, b):
    return lax.dot_general(a, b, (((0,), (0,)), ((), ())), preferred_element_type=F32)


def _split2(x):
    hi = x.astype(BF16)
    lo = (x - hi.astype(F32)).astype(BF16)
    return hi, lo


def _split3(x):
    hi = x.astype(BF16)
    r = x - hi.astype(F32)
    mid = r.astype(BF16)
    lo = (r - mid.astype(F32)).astype(BF16)
    return hi, mid, lo


def _dot3(a, b):
    ah, al = _split2(a)
    bh, bl = _split2(b)
    return _nn(ah, bh) + _nn(ah, bl) + _nn(al, bh)


def _sigmoid(x):
    return 1.0 / (1.0 + jnp.exp(-x))


def _rms(x, g):
    return x * lax.rsqrt(jnp.mean(x * x, axis=-1, keepdims=True) + EPS) * g


def _resident(shape):
    nd = len(shape)
    return pl.BlockSpec(shape, lambda *_: (0,) * nd, pipeline_mode=pl.Buffered(1))


def _params(sem):
    return pltpu.CompilerParams(dimension_semantics=sem, vmem_limit_bytes=VMEM_LIMIT)


def _ffn_kernel(x_ref, g_ref, wg_ref, wu_ref, wd_ref, gf_ref, o_ref, *, final):
    x = x_ref[...]
    xn = _rms(x, g_ref[...]).astype(BF16)
    acc = jnp.zeros(x.shape, F32)
    for k in range(D_FF_PAD // FF_CHUNK):
        sl = slice(k * FF_CHUNK, (k + 1) * FF_CHUNK)
        gate = _nn(xn, wg_ref[:, sl])
        up = _nn(xn, wu_ref[:, sl])
        h = (gate * _sigmoid(gate) * up).astype(BF16)
        acc = acc + _nn(h, wd_ref[sl, :])
    y = x + 0.5 * acc
    if final:
        y = _rms(y, gf_ref[...])
    o_ref[...] = y


def _ffn(h, norm_g, w_gu, w_down, final_g, final):
    t = h.shape[0]
    pad = D_FF_PAD - D_FF
    wg = jnp.pad(w_gu[:, :D_FF], ((0, 0), (0, pad))).astype(BF16)
    wu = jnp.pad(w_gu[:, D_FF:], ((0, 0), (0, pad))).astype(BF16)
    wd = jnp.pad(w_down, ((0, pad), (0, 0))).astype(BF16)
    row = pl.BlockSpec((ROW_TILE, D_MODEL), lambda i: (i, 0))
    return pl.pallas_call(
        functools.partial(_ffn_kernel, final=final),
        grid=(t // ROW_TILE,),
        in_specs=[row, _resident((1, D_MODEL)), _resident(wg.shape), _resident(wu.shape),
                  _resident(wd.shape), _resident((1, D_MODEL))],
        out_specs=row,
        out_shape=jax.ShapeDtypeStruct((t, D_MODEL), F32),
        compiler_params=_params(("parallel",)),
        name="ffn",
    )(h, norm_g.reshape(1, -1), wg, wu, wd, final_g.reshape(1, -1))


def _inproj_kernel(x_ref, g_ref, w_ref, *o_refs):
    xn = _rms(x_ref[...], g_ref[...]).astype(BF16)
    for o_ref, (_, width, dtype), start in zip(o_refs, SEGMENTS, SEG_OFFSETS[:-1], strict=True):
        o_ref[...] = _nn(xn, w_ref[:, start:start + width]).astype(dtype)


def _build_w_in(w_in):
    sizes = (NSA_WIDTH, KV_W, KV_W, KV_W, KV_W, KV_W, KV_W, NSA_HEADS * 3,
             HG_WIDTH, HG_WIDTH, HG_WIDTH, HG_WIDTH)
    splits = [int(v) for v in np.cumsum(sizes)[:-1]]
    wq, wkc, wvc, wks, wvs, wkw, wvw, wgl, whq, whf, whi, whg = jnp.split(w_in, splits, axis=1)
    d = w_in.shape[0]
    scale = NSA_HD ** -0.5
    zero = jnp.zeros((d, NSA_HD), w_in.dtype)
    q_cols = []
    for h in range(NSA_HEADS):
        w = wq[:, h * NSA_HD:(h + 1) * NSA_HD] * scale
        q_cols += [w, zero] if h // HEADS_PER_GROUP == 0 else [zero, w]
    swap = lambda w: jnp.concatenate([w, w[:, NSA_HD:], w[:, :NSA_HD]], axis=1)
    gl = wgl.reshape(d, NSA_HEADS, 3).transpose(0, 2, 1).reshape(d, 3 * NSA_HEADS)
    gl = jnp.pad(gl, ((0, 0), (0, LANES - 3 * NSA_HEADS)))
    cols = q_cols + [wks, swap(wvs), wkw, swap(wvw), wkc, wvc, gl, whq, whf, whi, whg]
    return jnp.concatenate(cols, axis=1).astype(BF16)


def _inproj(h, norm_g, w_ext):
    t = h.shape[0]
    row = lambda w: pl.BlockSpec((ROW_TILE, w), lambda i: (i, 0))
    return pl.pallas_call(
        _inproj_kernel,
        grid=(t // ROW_TILE,),
        in_specs=[row(D_MODEL), _resident((1, D_MODEL)), _resident(w_ext.shape)],
        out_specs=[row(w) for _, w, _ in SEGMENTS],
        out_shape=[jax.ShapeDtypeStruct((t, w), dt) for _, w, dt in SEGMENTS],
        compiler_params=_params(("parallel",)),
        name="inproj",
    )(h, norm_g.reshape(1, -1), w_ext)


def _outproj_kernel(a_ref, b_ref, w_ref, res_ref, o_ref):
    o_ref[...] = (res_ref[...] + _nn(a_ref[...], w_ref[:NSA_WIDTH, :])
                  + _nn(b_ref[...], w_ref[NSA_WIDTH:, :]))


def _outproj(o_nsa, o_hg, w_out, res):
    t = res.shape[0]
    row = lambda w: pl.BlockSpec((ROW_TILE, w), lambda i: (i, 0))
    return pl.pallas_call(
        _outproj_kernel,
        grid=(t // ROW_TILE,),
        in_specs=[row(NSA_WIDTH), row(HG_WIDTH), _resident(w_out.shape), row(D_MODEL)],
        out_specs=row(D_MODEL),
        out_shape=jax.ShapeDtypeStruct((t, D_MODEL), F32),
        compiler_params=_params(("parallel",)),
        name="outproj",
    )(o_nsa, o_hg, w_out.astype(BF16), res)


def _gelu_tanh(x):
    return 0.5 * x * (1.0 + jnp.tanh(0.7978845608028654 * (x + 0.044715 * (x * x * x))))


def _compress_kernel(x_ref, pos_ref, w1_ref, w2_ref, o_ref):
    half = CMP_STRIDE * NSA_HD
    nbp = x_ref.shape[2]
    out = jnp.zeros(o_ref.shape[1:], F32)
    for g in range(NSA_KV_GROUPS):
        x = x_ref[0, g]
        ha = _dot3(x + pos_ref[:, :half], w1_ref[:half, :])
        hb = _dot3(x + pos_ref[:, half:], w1_ref[half:, :])
        act = _gelu_tanh(ha + pltpu.roll(hb, nbp - 1, axis=0))
        out = out + _dot3(act, w2_ref[g])
    o_ref[0] = out


def _compress(kv, pos, w1, w2, reps):
    b, s, _ = kv.shape
    nbp = s // CMP_STRIDE
    x = kv.reshape(b, nbp, CMP_STRIDE, NSA_KV_GROUPS, NSA_HD).transpose(0, 3, 1, 2, 4)
    x = x.reshape(b, NSA_KV_GROUPS, nbp, CMP_STRIDE * NSA_HD)
    zero = jnp.zeros_like(w2)
    w2p = jnp.stack([jnp.concatenate([w2 if r == g else zero for r in reps], axis=1)
                     for g in range(NSA_KV_GROUPS)])
    width = len(reps) * NSA_HD
    return pl.pallas_call(
        _compress_kernel,
        grid=(b,),
        in_specs=[pl.BlockSpec((1,) + x.shape[1:], lambda i: (i, 0, 0, 0)),
                  _resident((1, CMP_LEN * NSA_HD)), _resident(w1.shape), _resident(w2p.shape)],
        out_specs=pl.BlockSpec((1, nbp, width), lambda i: (i, 0, 0)),
        out_shape=jax.ShapeDtypeStruct((b, nbp, width), F32),
        compiler_params=_params(("parallel",)),
        name="compress",
    )(x, pos.reshape(1, -1), w1, w2p)


def _pair_blocks(g, jj):
    lower = slice(0, LANES) if g == 0 else slice(LANES, 2 * LANES)
    upper = slice(LANES, 2 * LANES) if g == 0 else slice(0, LANES)
    return lower, upper


def _cmp_kernel(q_ref, kc_ref, vc2_ref, mt_ref, ocmp_ref, sel_ref, sc_scr, *, tq, ns, n_sel):
    nbp = kc_ref.shape[1]
    q0 = pl.program_id(1) * tq
    pos = q0 + lax.broadcasted_iota(jnp.int32, (tq, nbp), 0)
    blk_end = lax.broadcasted_iota(jnp.int32, (tq, nbp), 1) * CMP_STRIDE + (CMP_LEN - 1)
    dist = pos - blk_end
    valid = dist >= 0
    distf = dist.astype(F32)
    row_ok = pos >= CMP_LEN - 1
    kc_hi, kc_lo = _split2(kc_ref[0])
    vc2 = vc2_ref[0].astype(BF16)
    lane_lo = lax.broadcasted_iota(jnp.int32, (tq, LANES), 1) < NSA_HD
    blk = lax.broadcasted_iota(jnp.int32, (ns, tq), 0)
    pos_t = q0 + lax.broadcasted_iota(jnp.int32, (ns, tq), 1)
    cur = pos_t // SEL_BLOCK
    forced = (blk == 0) | (blk == cur) | (blk == cur - 1)
    causal = blk * SEL_BLOCK <= pos_t
    sel_t = []
    for g in range(NSA_KV_GROUPS):
        imp = jnp.zeros((tq, nbp), F32)
        acc = []
        for hh in range(HEADS_PER_GROUP):
            h = g * HEADS_PER_GROUP + hh
            qh = q_ref[0, :, h * LANES:(h + 1) * LANES]
            s = _nt(qh, kc_hi) + _nt(qh, kc_lo)
            s = jnp.where(valid, s - ALIBI_SLOPES[h] * distf, NEG)
            e = jnp.exp(s - jnp.max(s, axis=-1, keepdims=True))
            p = e / jnp.sum(e, axis=-1, keepdims=True)
            p = jnp.where(row_ok, p, 0.0)
            imp = imp + p
            acc.append(_nn(p.astype(BF16), vc2))
        for jj in range(HEADS_PER_GROUP // 2):
            lower, upper = _pair_blocks(g, jj)
            blk_out = jnp.where(lane_lo, acc[2 * jj][:, lower], acc[2 * jj + 1][:, upper])
            c0 = (g * HEADS_PER_GROUP + 2 * jj) * NSA_HD
            ocmp_ref[0, :, c0:c0 + LANES] = blk_out
        p_slc = sum(_nt(mt_ref[...], part) for part in _split3(imp))
        sc_scr[...] = jnp.where(forced, FORCE, jnp.where(causal, p_slc, NEG))
        score = sc_scr[...]
        rank = jnp.zeros((ns, tq), F32)
        for j in range(ns):
            row = sc_scr[pl.ds(j, 1), :]
            ge = jnp.where(row >= score, 1.0, 0.0)
            gt = jnp.where(row > score, 1.0, 0.0)
            rank = rank + jnp.where(blk > j, ge, gt)
        sel_t.append(jnp.where(rank < n_sel, 0.0, SEL_PENALTY))
    sel_ref[0] = jnp.concatenate(sel_t, axis=0).T.astype(BF16)


def _importance_matrix(nbp, ns):
    per = SEL_BLOCK // CMP_STRIDE
    m = np.zeros((ns, nbp), np.float32)
    for n in range(ns):
        for c in range(per * n, per * (n + 1)):
            for cc in (c - 1, c):
                if 0 <= cc < nbp - 1:
                    m[n, cc] += 1.0
    return jnp.asarray(m, BF16)


def _cmp_attention(q, kc, vc2, b, s):
    nbp = s // CMP_STRIDE
    ns = s // SEL_BLOCK
    tq = ATT_TQ
    q3 = q.reshape(b, s, NSA_HEADS * LANES)
    kern = functools.partial(_cmp_kernel, tq=tq, ns=ns, n_sel=min(SEL_TOPK, ns))
    return pl.pallas_call(
        kern,
        grid=(b, s // tq),
        in_specs=[pl.BlockSpec((1, tq, NSA_HEADS * LANES), lambda i, j: (i, j, 0)),
                  pl.BlockSpec((1, nbp, KV_W), lambda i, j: (i, 0, 0)),
                  pl.BlockSpec((1, nbp, 2 * KV_W), lambda i, j: (i, 0, 0)),
                  _resident((ns, nbp))],
        out_specs=[pl.BlockSpec((1, tq, NSA_WIDTH), lambda i, j: (i, j, 0)),
                   pl.BlockSpec((1, tq, NSA_KV_GROUPS * ns), lambda i, j: (i, j, 0))],
        out_shape=[jax.ShapeDtypeStruct((b, s, NSA_WIDTH), F32),
                   jax.ShapeDtypeStruct((b, s, NSA_KV_GROUPS * ns), BF16)],
        scratch_shapes=[pltpu.VMEM((ns, tq), F32)],
        compiler_params=_params(("parallel", "parallel")),
        name="cmp_topk",
    )(q3, kc, vc2, _importance_matrix(nbp, ns))


def _attn_kernel(q_ref, ks_ref, v2s_ref, kw_ref, v2w_ref, pen_ref, gl_ref, ocmp_ref, o_ref,
                 qa_scr, p_scr, m_scr, acc_scr, *, tq, tk, ns):
    i = pl.program_id(1)
    g = pl.program_id(2)
    npen = NSA_KV_GROUPS * ns
    half = NSA_HD
    f0 = half * (1 - g)
    rows_of = lambda hh: slice(hh * tq, (hh + 1) * tq)
    lane_q = lax.broadcasted_iota(jnp.int32, (tq, LANES), 1)
    lane_k = lax.broadcasted_iota(jnp.int32, (tk, LANES), 1)
    key_c = lax.broadcasted_iota(jnp.int32, (tk, LANES), 0)
    data_q = (lane_q // half) == g
    data_k = (lane_k // half) == g
    lane_lo = lane_q < half
    lane_lo_k = lane_k < half
    rel = (lax.broadcasted_iota(jnp.int32, (tq, tk), 1)
           - lax.broadcasted_iota(jnp.int32, (tq, tk), 0)).astype(F32)
    group_scale = jnp.where(g == 0, 1.0, ALIBI_SLOPES[HEADS_PER_GROUP] / ALIBI_SLOPES[0])

    for hh in range(HEADS_PER_GROUP):
        slope = ALIBI_SLOPES[hh] * group_scale
        feat = jnp.where(lane_q == f0, SEL_BLOCK * slope, jnp.where(lane_q == f0 + 1, slope, 0.0))
        q_blk = q_ref[0, :, hh * LANES:(hh + 1) * LANES]
        qa_scr[rows_of(hh), :LANES] = jnp.where(data_q, q_blk, feat.astype(BF16))
        qa_scr[rows_of(hh), LANES:] = pen_ref[0]

    def k_aug(k_tile, j, with_pen):
        base = j * (tk // SEL_BLOCK) - i * (tq // SEL_BLOCK)
        feat = jnp.where(lane_k == f0, (base + key_c // SEL_BLOCK).astype(F32),
                         jnp.where(lane_k == f0 + 1, (key_c % SEL_BLOCK).astype(F32), 0.0))
        k0 = jnp.where(data_k, k_tile, feat.astype(BF16))
        if not with_pen:
            return k0
        n = lax.broadcasted_iota(jnp.int32, (tk, npen), 1)
        blk = g * ns + j * (tk // SEL_BLOCK) + lax.broadcasted_iota(jnp.int32, (tk, npen), 0) // SEL_BLOCK
        return jnp.concatenate([k0, jnp.where(n == blk, -SEL_PENALTY, 0.0).astype(BF16)], axis=1)

    def v_aug(v_tile):
        b0, b1 = v_tile[:, :LANES], v_tile[:, LANES:]
        even = jnp.where(g == 0, b0, b1)
        odd = jnp.where(g == 0, b1, b0)
        one = jnp.ones((), BF16)
        return jnp.concatenate([jnp.where(lane_lo_k, even, one), jnp.where(lane_lo_k, one, odd)], axis=1)

    def update(k_ref, v_ref, j, br, mode, first, with_pen):
        start = pl.multiple_of(j * tk, tk)
        ka = k_aug(k_ref[0, pl.ds(start, tk), :], j, with_pen)
        va = v_aug(v_ref[0, pl.ds(start, tk), :])
        qa = qa_scr[...] if with_pen else qa_scr[:, :LANES]
        s = _nt(qa, ka)
        alphas = []
        for hh in range(HEADS_PER_GROUP):
            r = rows_of(hh)
            sh = s[r]
            if mode == "causal":
                sh = jnp.where(rel <= 0.0, sh, NEG)
            elif mode == "lower":
                sh = jnp.where(rel + (j * tk - i * tq).astype(F32) > -float(WINDOW), sh, NEG)
            m_cur = jnp.max(sh, axis=-1, keepdims=True)
            if first:
                m_new = jnp.broadcast_to(m_cur, (tq, LANES))
            else:
                m_prev = m_scr[br, r]
                m_new = jnp.maximum(m_prev, m_cur)
                alphas.append(jnp.exp(m_prev - m_new))
            m_scr[br, r] = m_new
            p_scr[r] = jnp.exp(sh - jnp.concatenate([m_new] * (tk // LANES), axis=1)).astype(BF16)
        pv = _nn(p_scr[...], va)
        for hh in range(HEADS_PER_GROUP):
            r = rows_of(hh)
            if first:
                acc_scr[br, r] = pv[r]
            else:
                acc_scr[br, r] = jnp.concatenate([alphas[hh]] * 2, axis=1) * acc_scr[br, r] + pv[r]

    update(ks_ref, v2s_ref, i, 0, "causal", True, True)

    def sel_body(j, carry):
        update(ks_ref, v2s_ref, j, 0, None, False, True)
        return carry

    lax.fori_loop(0, i, sel_body, 0)

    update(kw_ref, v2w_ref, i, 1, "causal", True, False)
    n_back = WINDOW // tk
    for d in range(1, n_back + 1):
        @pl.when(i >= d)
        def _(d=d):
            update(kw_ref, v2w_ref, i - d, 1, "lower" if d == n_back else None, False, False)

    gates = _sigmoid(gl_ref[0])
    for jj in range(HEADS_PER_GROUP // 2):
        def gate(br, odd, jj=jj):
            c = br * NSA_HEADS + 2 * jj + odd
            return jnp.where(g == 0, gates[:, c:c + 1], gates[:, c + HEADS_PER_GROUP:c + HEADS_PER_GROUP + 1])
        cols = slice(jj * LANES, (jj + 1) * LANES)
        blk = jnp.where(lane_lo, gate(0, 0), gate(0, 1)) * ocmp_ref[0, :, cols]
        for br in range(2):
            a0 = acc_scr[br, rows_of(2 * jj)]
            a1 = acc_scr[br, rows_of(2 * jj + 1)]
            low = a0[:, :LANES] / a0[:, LANES:]
            up = a1[:, LANES:] / a1[:, :LANES]
            blk = blk + jnp.where(lane_lo, gate(br + 1, 0) * low, gate(br + 1, 1) * up)
        o_ref[0, :, cols] = blk.astype(o_ref.dtype)


def _attention(q, ks, v2s, kw, v2w, pen, gl, ocmp, b, s):
    ns = s // SEL_BLOCK
    tq, tk = ATT_TQ, ATT_TK
    assert tq == tk and WINDOW % tk == 0 and tq % SEL_BLOCK == 0
    npen = NSA_KV_GROUPS * ns
    gw = HEADS_PER_GROUP * LANES
    ow = HEADS_PER_GROUP * NSA_HD
    r3 = lambda a: a.reshape(b, s, a.shape[-1])
    tile = lambda w: pl.BlockSpec((1, tq, w), lambda i, j, g: (i, j, 0))
    gtile = lambda w: pl.BlockSpec((1, tq, w), lambda i, j, g: (i, j, g))
    full = lambda w: pl.BlockSpec((1, s, w), lambda i, j, g: (i, 0, 0))
    rows = HEADS_PER_GROUP * tq
    kern = functools.partial(_attn_kernel, tq=tq, tk=tk, ns=ns)
    return pl.pallas_call(
        kern,
        grid=(b, s // tq, NSA_KV_GROUPS),
        in_specs=[gtile(gw), full(KV_W), full(2 * KV_W), full(KV_W), full(2 * KV_W),
                  tile(npen), tile(LANES), gtile(ow)],
        out_specs=gtile(ow),
        out_shape=jax.ShapeDtypeStruct((b, s, NSA_WIDTH), BF16),
        scratch_shapes=[pltpu.VMEM((rows, LANES + npen), BF16), pltpu.VMEM((rows, tk), BF16),
                        pltpu.VMEM((2, rows, LANES), F32), pltpu.VMEM((2, rows, 2 * LANES), F32)],
        compiler_params=_params(("parallel", "parallel", "parallel")),
        name="sel_win_attention",
    )(r3(q), r3(ks), r3(v2s), r3(kw), r3(v2w), pen, r3(gl), ocmp)


def _hgrn_kernel(hq_ref, hf_ref, hi_ref, hg_ref, lbraw_ref, onorm_ref, o_ref, st_scr, b_scr,
                 *, layer, n_chunks):
    c = HG_CHUNK

    @pl.when(pl.program_id(1) == 0)
    def _():
        st_scr[...] = jnp.zeros(st_scr.shape, F32)

    raw = lbraw_ref[...]
    ex = jnp.exp(raw - jnp.max(raw, axis=0, keepdims=True))
    sm = ex / jnp.sum(ex, axis=0, keepdims=True)
    lb_all = jnp.zeros((1, raw.shape[1]), F32)
    for l in range(1, layer + 1):
        lb_all = lb_all + sm[l:l + 1, :]

    t_idx = lax.broadcasted_iota(jnp.int32, (c, HG_DK), 0)
    sub = lax.broadcasted_iota(jnp.int32, (8, HG_DK), 0)
    ti = lax.broadcasted_iota(jnp.int32, (c, c), 0)
    si = lax.broadcasted_iota(jnp.int32, (c, c), 1)
    tril = jnp.where(si <= ti, 1.0, 0.0).astype(BF16)
    onorm = onorm_ref[...]

    def ref_rows(m):
        if m >= 4:
            return jnp.concatenate(
                [jnp.broadcast_to(b_scr[pl.ds(s0 + m - 1, 1), :], (2 * m, HG_DK))
                 for s0 in range(0, c, 2 * m)], axis=0)
        pieces = []
        for s0 in range(0, c, 8):
            r1 = jnp.broadcast_to(b_scr[pl.ds(s0 + 1, 1), :], (8, HG_DK))
            r5 = jnp.broadcast_to(b_scr[pl.ds(s0 + 5, 1), :], (8, HG_DK))
            pieces.append(jnp.where(sub < 4, r1, r5))
        return jnp.concatenate(pieces, axis=0)

    def chunk(ci, carry):
        rows = pl.ds(pl.multiple_of(ci * c, c), c)
        for h in range(HG_HEADS):
            lanes = slice(h * HG_DK, (h + 1) * HG_DK)
            lb = lb_all[:, lanes]
            z = hf_ref[rows, lanes]
            hq = hq_ref[rows, lanes]
            q = hq * _sigmoid(hq)
            f = jnp.maximum(lb + (1.0 - lb) * _sigmoid(z), F_FLOOR)
            logf = jnp.log(f)
            k = (1.0 - lb) * _sigmoid(-z)
            v = hi_ref[rows, lanes].astype(BF16)
            bcum = sum(_nn(tril, part) for part in _split3(logf))
            b_scr[...] = bcum

            a = jnp.where(ti == si, jnp.sum(q * k, axis=-1, keepdims=True), 0.0)
            for m in (32, 16, 8, 4, 2, 1):
                second = (t_idx & m) != 0
                if m == 1:
                    w = jnp.where(second, f, 1.0)
                else:
                    ref = ref_rows(m)
                    w = jnp.exp(jnp.where(second, bcum - ref, ref - bcum))
                qs = jnp.where(second, q * w, 0.0).astype(BF16)
                ks = jnp.where(second, 0.0, k * w).astype(BF16)
                same = (ti // (2 * m)) == (si // (2 * m))
                a = a + jnp.where(same, _nt(qs, ks), 0.0)

            st = st_scr[h]
            o = _nn(a.astype(BF16), v) + _nt((q * jnp.exp(bcum)).astype(BF16), st.astype(BF16))
            b_last = b_scr[pl.ds(c - 1, 1), :]
            kd = (k * jnp.exp(b_last - bcum)).astype(BF16)
            st_scr[h] = jnp.exp(b_last) * st + _tn(v, kd)

            o = o * lax.rsqrt(jnp.mean(o * o, axis=-1, keepdims=True) + EPS) * onorm
            gate = hg_ref[rows, lanes]
            o_ref[rows, lanes] = (o * (gate * _sigmoid(gate))).astype(o_ref.dtype)
        return carry

    lax.fori_loop(0, n_chunks, chunk, 0)


def _hgrn(hg4, lb_raw, out_norm, layer, b, s):
    tt = HG_TOKENS
    per = s // tt
    col = lambda k: pl.BlockSpec((tt, HG_WIDTH), lambda i, j, k=k: (i * per + j, k))
    kern = functools.partial(_hgrn_kernel, layer=layer, n_chunks=tt // HG_CHUNK)
    return pl.pallas_call(
        kern,
        grid=(b, per),
        in_specs=[col(0), col(1), col(2), col(3), _resident(lb_raw.shape), _resident((1, HG_DV))],
        out_specs=pl.BlockSpec((tt, HG_WIDTH), lambda i, j: (i * per + j, 0)),
        out_shape=jax.ShapeDtypeStruct((b * s, HG_WIDTH), BF16),
        scratch_shapes=[pltpu.VMEM((HG_HEADS, HG_DV, HG_DK), F32), pltpu.VMEM((HG_CHUNK, HG_DK), F32)],
        compiler_params=_params(("parallel", "arbitrary")),
        name="hgrn2",
    )(hg4, hg4, hg4, hg4, lb_raw, out_norm.reshape(1, -1))


def _mixer(h, l, b, s, mix_norm, w_in, cmp_pos_k, cmp_pos_v, cmp_k_w1, cmp_k_w2, cmp_v_w1, cmp_v_w2,
           hgrn_lower_bound, hgrn_out_norm, w_out):
    q, ks, v2s, kw, v2w, kc_in, vc_in, gl, hg4 = _inproj(h, mix_norm[l], _build_w_in(w_in[l]))
    kc = _compress(kc_in.reshape(b, s, KV_W), cmp_pos_k[l], cmp_k_w1[l], cmp_k_w2[l], (0, 1))
    vc2 = _compress(vc_in.reshape(b, s, KV_W), cmp_pos_v[l], cmp_v_w1[l], cmp_v_w2[l], (0, 1, 1, 0))
    ocmp, sel = _cmp_attention(q, kc, vc2, b, s)
    o_nsa = _attention(q, ks, v2s, kw, v2w, sel, gl, ocmp, b, s)
    o_hg = _hgrn(hg4, hgrn_lower_bound, hgrn_out_norm[l], l, b, s)
    return _outproj(o_nsa.reshape(b * s, NSA_WIDTH), o_hg, w_out[l], h)


def kernel(x, ffn1_norm, ffn1_w_gu, ffn1_w_down, mix_norm, w_in, cmp_pos_k, cmp_pos_v, cmp_k_w1, cmp_k_w2, cmp_v_w1, cmp_v_w2, hgrn_lower_bound, hgrn_out_norm, w_out, ffn2_norm, ffn2_w_gu, ffn2_w_down, final_norm):
    b, s, d = x.shape
    depth = ffn1_norm.shape[0]
    h = x.reshape(b * s, d)
    for l in range(depth):
        h = _ffn(h, ffn1_norm[l], ffn1_w_gu[l], ffn1_w_down[l], final_norm, False)
        h = _mixer(h, l, b, s, mix_norm, w_in, cmp_pos_k, cmp_pos_v, cmp_k_w1, cmp_k_w2, cmp_v_w1,
                   cmp_v_w2, hgrn_lower_bound, hgrn_out_norm, w_out)
        h = _ffn(h, ffn2_norm[l], ffn2_w_gu[l], ffn2_w_down[l], final_norm, l == depth - 1)
    return h.reshape(b, s, d)
```

```python
import functools

import jax
import jax.numpy as jnp
import numpy as np
from jax import lax
from jax.experimental import pallas as pl
from jax.experimental.pallas import tpu as pltpu

F32 = jnp.float32
BF16 = jnp.bfloat16

D_MODEL = 1024
EPS = 1e-6
NEG = -1e30
FORCE = 1e6
F_FLOOR = 1e-30
NSA_HEADS = 8
NSA_KV_GROUPS = 2
HEADS_PER_GROUP = NSA_HEADS // NSA_KV_GROUPS
NSA_HD = 64
CMP_LEN = 32
CMP_STRIDE = 16
CMP_HID = 256
SEL_BLOCK = 64
SEL_TOPK = 16
WINDOW = 512
HG_HEADS = 4
HG_DK = 128
HG_DV = 128
HG_CHUNK = 64
D_FF = 2752
NSA_WIDTH = NSA_HEADS * NSA_HD
HG_WIDTH = HG_HEADS * HG_DV
KV_W = NSA_KV_GROUPS * NSA_HD

LANES = 128
D_FF_PAD = 2816
FF_CHUNK = 256
ROW_TILE = 512
ATT_TQ = 256
ATT_TK = 256
HG_TOKENS = 512
VMEM_LIMIT = 56 * 1024 * 1024

SEL_PENALTY = 2.0 ** 50
ALIBI_SLOPES = tuple(2.0 ** (-8.0 * (i + 1) / NSA_HEADS) for i in range(NSA_HEADS))

SEGMENTS = (
    ("q", NSA_HEADS * LANES, BF16),
    ("ks", KV_W, BF16),
    ("v2s", 2 * KV_W, BF16),
    ("kw", KV_W, BF16),
    ("v2w", 2 * KV_W, BF16),
    ("kc", KV_W, F32),
    ("vc", KV_W, F32),
    ("gl", LANES, F32),
    ("hg", 4 * HG_WIDTH, F32),
)
SEG_OFFSETS = tuple(int(v) for v in np.cumsum([0] + [s[1] for s in SEGMENTS]))
IN_COLS_EXT = SEG_OFFSETS[-1]


def _nn(a, b):
    return jnp.dot(a, b, preferred_element_type=F32)


def _nt(a, b):
    return lax.dot_general(a, b, (((1,), (1,)), ((), ())), preferred_element_type=F32)


def _tn(a, b):
    return lax.dot_general(a, b, (((0,), (0,)), ((), ())), preferred_element_type=F32)


def _split2(x):
    hi = x.astype(BF16)
    lo = (x - hi.astype(F32)).astype(BF16)
    return hi, lo


def _split3(x):
    hi = x.astype(BF16)
    r = x - hi.astype(F32)
    mid = r.astype(BF16)
    lo = (r - mid.astype(F32)).astype(BF16)
    return hi, mid, lo


def _dot3(a, b):
    ah, al = _split2(a)
    bh, bl = _split2(b)
    return _nn(ah, bh) + _nn(ah, bl) + _nn(al, bh)


def _sigmoid(x):
    return 1.0 / (1.0 + jnp.exp(-x))


def _rms(x, g):
    return x * lax.rsqrt(jnp.mean(x * x, axis=-1, keepdims=True) + EPS) * g


def _resident(shape):
    nd = len(shape)
    return pl.BlockSpec(shape, lambda *_: (0,) * nd, pipeline_mode=pl.Buffered(1))


def _params(sem):
    return pltpu.CompilerParams(dimension_semantics=sem, vmem_limit_bytes=VMEM_LIMIT)


def _ffn_kernel(x_ref, g_ref, wg_ref, wu_ref, wd_ref, gf_ref, o_ref, *, final):
    x = x_ref[...]
    xn = _rms(x, g_ref[...]).astype(BF16)
    acc = jnp.zeros(x.shape, F32)
    for k in range(D_FF_PAD // FF_CHUNK):
        sl = slice(k * FF_CHUNK, (k + 1) * FF_CHUNK)
        gate = _nn(xn, wg_ref[:, sl])
        up = _nn(xn, wu_ref[:, sl])
        h = (gate * _sigmoid(gate) * up).astype(BF16)
        acc = acc + _nn(h, wd_ref[sl, :])
    y = x + 0.5 * acc
    if final:
        y = _rms(y, gf_ref[...])
    o_ref[...] = y


def _ffn(h, norm_g, w_gu, w_down, final_g, final):
    t = h.shape[0]
    pad = D_FF_PAD - D_FF
    wg = jnp.pad(w_gu[:, :D_FF], ((0, 0), (0, pad))).astype(BF16)
    wu = jnp.pad(w_gu[:, D_FF:], ((0, 0), (0, pad))).astype(BF16)
    wd = jnp.pad(w_down, ((0, pad), (0, 0))).astype(BF16)
    row = pl.BlockSpec((ROW_TILE, D_MODEL), lambda i: (i, 0))
    return pl.pallas_call(
        functools.partial(_ffn_kernel, final=final),
        grid=(t // ROW_TILE,),
        in_specs=[row, _resident((1, D_MODEL)), _resident(wg.shape), _resident(wu.shape),
                  _resident(wd.shape), _resident((1, D_MODEL))],
        out_specs=row,
        out_shape=jax.ShapeDtypeStruct((t, D_MODEL), F32),
        compiler_params=_params(("parallel",)),
        name="ffn",
    )(h, norm_g.reshape(1, -1), wg, wu, wd, final_g.reshape(1, -1))


def _inproj_kernel(x_ref, g_ref, w_ref, qfeat_ref, *o_refs):
    xn = _rms(x_ref[...], g_ref[...]).astype(BF16)
    for o_ref, (name, width, dtype), start in zip(o_refs, SEGMENTS, SEG_OFFSETS[:-1], strict=True):
        y = _nn(xn, w_ref[:, start:start + width])
        if name == "q":
            y = y + qfeat_ref[...]
        o_ref[...] = y.astype(dtype)


def _q_features():
    feat = np.zeros((1, NSA_HEADS * LANES), np.float32)
    for h in range(NSA_HEADS):
        f0 = h * LANES + NSA_HD * (1 - h // HEADS_PER_GROUP)
        feat[0, f0] = SEL_BLOCK * ALIBI_SLOPES[h]
        feat[0, f0 + 1] = ALIBI_SLOPES[h]
    return jnp.asarray(feat)


def _build_w_in(w_in):
    sizes = (NSA_WIDTH, KV_W, KV_W, KV_W, KV_W, KV_W, KV_W, NSA_HEADS * 3,
             HG_WIDTH, HG_WIDTH, HG_WIDTH, HG_WIDTH)
    splits = [int(v) for v in np.cumsum(sizes)[:-1]]
    wq, wkc, wvc, wks, wvs, wkw, wvw, wgl, whq, whf, whi, whg = jnp.split(w_in, splits, axis=1)
    d = w_in.shape[0]
    scale = NSA_HD ** -0.5
    zero = jnp.zeros((d, NSA_HD), w_in.dtype)
    q_cols = []
    for h in range(NSA_HEADS):
        w = wq[:, h * NSA_HD:(h + 1) * NSA_HD] * scale
        q_cols += [w, zero] if h // HEADS_PER_GROUP == 0 else [zero, w]
    swap = lambda w: jnp.concatenate([w, w[:, NSA_HD:], w[:, :NSA_HD]], axis=1)
    gl = wgl.reshape(d, NSA_HEADS, 3).transpose(0, 2, 1).reshape(d, 3 * NSA_HEADS)
    gl = jnp.pad(gl, ((0, 0), (0, LANES - 3 * NSA_HEADS)))
    cols = q_cols + [wks, swap(wvs), wkw, swap(wvw), wkc, wvc, gl, whq, whf, whi, whg]
    return jnp.concatenate(cols, axis=1).astype(BF16)


def _inproj(h, norm_g, w_ext):
    t = h.shape[0]
    row = lambda w: pl.BlockSpec((ROW_TILE, w), lambda i: (i, 0))
    return pl.pallas_call(
        _inproj_kernel,
        grid=(t // ROW_TILE,),
        in_specs=[row(D_MODEL), _resident((1, D_MODEL)), _resident(w_ext.shape),
                  _resident((1, NSA_HEADS * LANES))],
        out_specs=[row(w) for _, w, _ in SEGMENTS],
        out_shape=[jax.ShapeDtypeStruct((t, w), dt) for _, w, dt in SEGMENTS],
        compiler_params=_params(("parallel",)),
        name="inproj",
    )(h, norm_g.reshape(1, -1), w_ext, _q_features())


def _outproj_kernel(a_ref, b_ref, w_ref, res_ref, o_ref):
    o_ref[...] = (res_ref[...] + _nn(a_ref[...], w_ref[:NSA_WIDTH, :])
                  + _nn(b_ref[...], w_ref[NSA_WIDTH:, :]))


def _outproj(o_nsa, o_hg, w_out, res):
    t = res.shape[0]
    row = lambda w: pl.BlockSpec((ROW_TILE, w), lambda i: (i, 0))
    return pl.pallas_call(
        _outproj_kernel,
        grid=(t // ROW_TILE,),
        in_specs=[row(NSA_WIDTH), row(HG_WIDTH), _resident(w_out.shape), row(D_MODEL)],
        out_specs=row(D_MODEL),
        out_shape=jax.ShapeDtypeStruct((t, D_MODEL), F32),
        compiler_params=_params(("parallel",)),
        name="outproj",
    )(o_nsa, o_hg, w_out.astype(BF16), res)


def _gelu_tanh(x):
    return 0.5 * x * (1.0 + jnp.tanh(0.7978845608028654 * (x + 0.044715 * (x * x * x))))


def _compress_kernel(x_ref, pos_ref, w1_ref, w2_ref, o_ref):
    half = CMP_STRIDE * NSA_HD
    nbp = x_ref.shape[2]
    out = jnp.zeros(o_ref.shape[1:], F32)
    for g in range(NSA_KV_GROUPS):
        x = x_ref[0, g]
        ha = _dot3(x + pos_ref[:, :half], w1_ref[:half, :])
        hb = _dot3(x + pos_ref[:, half:], w1_ref[half:, :])
        act = _gelu_tanh(ha + pltpu.roll(hb, nbp - 1, axis=0))
        out = out + _dot3(act, w2_ref[g])
    o_ref[0] = out


def _compress(kv, pos, w1, w2, reps):
    b, s, _ = kv.shape
    nbp = s // CMP_STRIDE
    x = kv.reshape(b, nbp, CMP_STRIDE, NSA_KV_GROUPS, NSA_HD).transpose(0, 3, 1, 2, 4)
    x = x.reshape(b, NSA_KV_GROUPS, nbp, CMP_STRIDE * NSA_HD)
    zero = jnp.zeros_like(w2)
    w2p = jnp.stack([jnp.concatenate([w2 if r == g else zero for r in reps], axis=1)
                     for g in range(NSA_KV_GROUPS)])
    width = len(reps) * NSA_HD
    return pl.pallas_call(
        _compress_kernel,
        grid=(b,),
        in_specs=[pl.BlockSpec((1,) + x.shape[1:], lambda i: (i, 0, 0, 0)),
                  _resident((1, CMP_LEN * NSA_HD)), _resident(w1.shape), _resident(w2p.shape)],
        out_specs=pl.BlockSpec((1, nbp, width), lambda i: (i, 0, 0)),
        out_shape=jax.ShapeDtypeStruct((b, nbp, width), F32),
        compiler_params=_params(("parallel",)),
        name="compress",
    )(x, pos.reshape(1, -1), w1, w2p)


def _pair_blocks(g, jj):
    lower = slice(0, LANES) if g == 0 else slice(LANES, 2 * LANES)
    upper = slice(LANES, 2 * LANES) if g == 0 else slice(0, LANES)
    return lower, upper


def _cmp_kernel(q_ref, kc_ref, vc2_ref, mt_ref, ocmp_ref, sel_ref, *, tq, ns, n_sel):
    nbp = kc_ref.shape[1]
    q0 = pl.program_id(1) * tq
    pos = q0 + lax.broadcasted_iota(jnp.int32, (tq, nbp), 0)
    blk_end = lax.broadcasted_iota(jnp.int32, (tq, nbp), 1) * CMP_STRIDE + (CMP_LEN - 1)
    valid = blk_end <= pos
    row_ok = (q0 + lax.broadcasted_iota(jnp.int32, (tq, 1), 0)) >= CMP_LEN - 1
    kc = kc_ref[0].astype(BF16)
    vc2 = vc2_ref[0].astype(BF16)
    lane_c = lax.broadcasted_iota(jnp.int32, (nbp, LANES), 1)
    c_idx = lax.broadcasted_iota(jnp.int32, (nbp, LANES), 0)
    per = SEL_BLOCK // CMP_STRIDE
    feat_a = (c_idx // per - q0 // SEL_BLOCK).astype(F32)
    feat_b = ((c_idx % per) * CMP_STRIDE + (CMP_LEN - 1)).astype(F32)
    lane_lo = lax.broadcasted_iota(jnp.int32, (tq, LANES), 1) < NSA_HD
    blk_f = lax.broadcasted_iota(jnp.int32, (ns, tq), 0).astype(F32)
    blk = lax.broadcasted_iota(jnp.int32, (ns, tq), 0)
    pos_t = q0 + lax.broadcasted_iota(jnp.int32, (ns, tq), 1)
    cur = pos_t // SEL_BLOCK
    forced = (blk == 0) | (blk == cur) | (blk == cur - 1)
    causal = blk * SEL_BLOCK <= pos_t
    sel_t = []
    for g in range(NSA_KV_GROUPS):
        f0 = NSA_HD * (1 - g)
        feat = jnp.where(lane_c == f0, feat_a, jnp.where(lane_c == f0 + 1, feat_b, 0.0))
        kc_g = jnp.where((lane_c // NSA_HD) == g, kc, feat.astype(BF16))
        imp = jnp.zeros((tq, nbp), F32)
        acc = []
        for hh in range(HEADS_PER_GROUP):
            h = g * HEADS_PER_GROUP + hh
            s = _nt(q_ref[0, :, h * LANES:(h + 1) * LANES], kc_g)
            s = jnp.where(valid, s, NEG)
            e = jnp.exp(s - jnp.max(s, axis=-1, keepdims=True))
            inv = jnp.where(row_ok, 1.0 / jnp.sum(e, axis=-1, keepdims=True), 0.0)
            p = e * inv
            imp = imp + p
            acc.append(_nn(p.astype(BF16), vc2))
        for jj in range(HEADS_PER_GROUP // 2):
            lower, upper = _pair_blocks(g, jj)
            blk_out = jnp.where(lane_lo, acc[2 * jj][:, lower], acc[2 * jj + 1][:, upper])
            c0 = (g * HEADS_PER_GROUP + 2 * jj) * NSA_HD
            ocmp_ref[0, :, c0:c0 + LANES] = blk_out
        p_slc = sum(_nt(mt_ref[...], part) for part in _split3(imp))
        score = jnp.where(forced, FORCE, jnp.where(causal, p_slc, NEG))
        taken = -jnp.inf
        for _ in range(n_sel):
            top = jnp.max(score, axis=0, keepdims=True)
            first = jnp.min(jnp.where(score == top, blk_f, float(ns)), axis=0, keepdims=True)
            score = jnp.where(blk_f == first, taken, score)
        sel_t.append(jnp.where(score == taken, 0.0, SEL_PENALTY))
    sel_ref[0] = jnp.concatenate(sel_t, axis=0).T.astype(BF16)


def _importance_matrix(nbp, ns):
    per = SEL_BLOCK // CMP_STRIDE
    m = np.zeros((ns, nbp), np.float32)
    for n in range(ns):
        for c in range(per * n, per * (n + 1)):
            for cc in (c - 1, c):
                if 0 <= cc < nbp - 1:
                    m[n, cc] += 1.0
    return jnp.asarray(m, BF16)


def _cmp_attention(q, kc, vc2, b, s):
    nbp = s // CMP_STRIDE
    ns = s // SEL_BLOCK
    tq = ATT_TQ
    q3 = q.reshape(b, s, NSA_HEADS * LANES)
    kern = functools.partial(_cmp_kernel, tq=tq, ns=ns, n_sel=min(SEL_TOPK, ns))
    return pl.pallas_call(
        kern,
        grid=(b, s // tq),
        in_specs=[pl.BlockSpec((1, tq, NSA_HEADS * LANES), lambda i, j: (i, j, 0)),
                  pl.BlockSpec((1, nbp, KV_W), lambda i, j: (i, 0, 0)),
                  pl.BlockSpec((1, nbp, 2 * KV_W), lambda i, j: (i, 0, 0)),
                  _resident((ns, nbp))],
        out_specs=[pl.BlockSpec((1, tq, NSA_WIDTH), lambda i, j: (i, j, 0)),
                   pl.BlockSpec((1, tq, NSA_KV_GROUPS * ns), lambda i, j: (i, j, 0))],
        out_shape=[jax.ShapeDtypeStruct((b, s, NSA_WIDTH), F32),
                   jax.ShapeDtypeStruct((b, s, NSA_KV_GROUPS * ns), BF16)],
        compiler_params=_params(("parallel", "parallel")),
        name="cmp_topk",
    )(q3, kc, vc2, _importance_matrix(nbp, ns))


def _attn_kernel(q_ref, ks_ref, v2s_ref, kw_ref, v2w_ref, pen_ref, gl_ref, ocmp_ref, o_ref,
                 qa_scr, s_scr, p_scr, al_scr, m_scr, acc_scr, *, tq, tk, ns):
    i = pl.program_id(1)
    g = pl.program_id(2)
    npen = NSA_KV_GROUPS * ns
    half = NSA_HD
    f0 = half * (1 - g)
    rows_of = lambda hh: slice(hh * tq, (hh + 1) * tq)
    lane_q = lax.broadcasted_iota(jnp.int32, (tq, LANES), 1)
    lane_k = lax.broadcasted_iota(jnp.int32, (tk, LANES), 1)
    key_c = lax.broadcasted_iota(jnp.int32, (tk, LANES), 0)
    data_k = (lane_k // half) == g
    lane_lo = lane_q < half
    lane_lo_k = lane_k < half
    rel = (lax.broadcasted_iota(jnp.int32, (tq, tk), 1)
           - lax.broadcasted_iota(jnp.int32, (tq, tk), 0)).astype(F32)

    for hh in range(HEADS_PER_GROUP):
        qa_scr[rows_of(hh), :LANES] = q_ref[0, :, hh * LANES:(hh + 1) * LANES]
        qa_scr[rows_of(hh), LANES:] = pen_ref[0]

    def k_aug(k_tile, j, with_pen):
        base = j * (tk // SEL_BLOCK) - i * (tq // SEL_BLOCK)
        feat = jnp.where(lane_k == f0, (base + key_c // SEL_BLOCK).astype(F32),
                         jnp.where(lane_k == f0 + 1, (key_c % SEL_BLOCK).astype(F32), 0.0))
        k0 = jnp.where(data_k, k_tile, feat.astype(BF16))
        if not with_pen:
            return k0
        n = lax.broadcasted_iota(jnp.int32, (tk, npen), 1)
        blk = g * ns + j * (tk // SEL_BLOCK) + lax.broadcasted_iota(jnp.int32, (tk, npen), 0) // SEL_BLOCK
        return jnp.concatenate([k0, jnp.where(n == blk, -SEL_PENALTY, 0.0).astype(BF16)], axis=1)

    def v_aug(v_tile):
        b0, b1 = v_tile[:, :LANES], v_tile[:, LANES:]
        even = jnp.where(g == 0, b0, b1)
        odd = jnp.where(g == 0, b1, b0)
        one = jnp.ones((), BF16)
        return jnp.concatenate([jnp.where(lane_lo_k, even, one), jnp.where(lane_lo_k, one, odd)], axis=1)

    def scores(k_ref, j, with_pen, slot):
        qa = qa_scr[...] if with_pen else qa_scr[:, :LANES]
        start = pl.multiple_of(j * tk, tk)
        s_scr[slot] = _nt(qa, k_aug(k_ref[0, pl.ds(start, tk), :], j, with_pen))

    def probs(br, j, mode, slot, first, exists=None):
        for hh in range(HEADS_PER_GROUP):
            r = rows_of(hh)
            s = s_scr[slot, r]
            if mode == "causal":
                s = jnp.where(rel <= 0.0, s, NEG)
            elif mode == "lower":
                s = jnp.where(rel + (j * tk - i * tq).astype(F32) > -float(WINDOW), s, NEG)
            if exists is not None:
                s = jnp.where(exists, s, NEG)
            m_cur = jnp.max(s, axis=-1, keepdims=True)
            if first:
                m_new = jnp.broadcast_to(m_cur, (tq, LANES))
            else:
                m_prev = m_scr[br, r]
                m_new = jnp.maximum(m_prev, m_cur)
                al_scr[slot, r] = jnp.exp(m_prev - m_new)
            m_scr[br, r] = m_new
            p_scr[slot, r] = jnp.exp(s - jnp.concatenate([m_new] * (tk // LANES), axis=1)).astype(BF16)

    def accumulate(br, v_ref, j, slot, first):
        start = pl.multiple_of(j * tk, tk)
        pv = _nn(p_scr[slot], v_aug(v_ref[0, pl.ds(start, tk), :]))
        for hh in range(HEADS_PER_GROUP):
            r = rows_of(hh)
            if first:
                acc_scr[br, r] = pv[r]
            else:
                acc_scr[br, r] = jnp.concatenate([al_scr[slot, r]] * 2, axis=1) * acc_scr[br, r] + pv[r]

    last = jnp.maximum(i - 1, 0)
    tile_of = lambda n: jnp.minimum(n - 1, last)
    scores(ks_ref, i, True, 0)
    scores(ks_ref, tile_of(1), True, 1)
    probs(0, i, "causal", 0, True)
    scores(ks_ref, tile_of(2), True, 0)
    probs(0, tile_of(1), None, 1, False)
    accumulate(0, v2s_ref, i, 0, True)

    def sel_step(n, parity):
        scores(ks_ref, tile_of(n + 2), True, parity)
        probs(0, tile_of(n + 1), None, 1 - parity, False)
        accumulate(0, v2s_ref, n - 1, parity, False)

    def sel_two_steps(t, carry):
        sel_step(2 * t + 1, 1)
        sel_step(2 * t + 2, 0)
        return carry

    lax.fori_loop(0, i // 2, sel_two_steps, 0)

    @pl.when(i % 2 == 1)
    def _():
        sel_step(i, 1)

    n_back = WINDOW // tk
    back = [(jnp.maximum(i - d, 0), "lower" if d == n_back else None, i >= d) for d in range(1, n_back + 1)]
    scores(kw_ref, i, False, 0)
    scores(kw_ref, back[0][0], False, 1)
    probs(1, i, "causal", 0, True)
    accumulate(1, v2w_ref, i, 0, True)
    for d, (j, mode, exists) in enumerate(back, start=1):
        if d < n_back:
            scores(kw_ref, back[d][0], False, (d + 1) % 2)
        probs(1, j, mode, d % 2, False, exists)
        accumulate(1, v2w_ref, j, d % 2, False)

    gates = _sigmoid(gl_ref[0])
    for jj in range(HEADS_PER_GROUP // 2):
        def gate(br, odd, jj=jj):
            c = br * NSA_HEADS + 2 * jj + odd
            return jnp.where(g == 0, gates[:, c:c + 1], gates[:, c + HEADS_PER_GROUP:c + HEADS_PER_GROUP + 1])
        cols = slice(jj * LANES, (jj + 1) * LANES)
        blk = jnp.where(lane_lo, gate(0, 0), gate(0, 1)) * ocmp_ref[0, :, cols]
        for br in range(2):
            a0 = acc_scr[br, rows_of(2 * jj)]
            a1 = acc_scr[br, rows_of(2 * jj + 1)]
            low = a0[:, :LANES] / a0[:, LANES:]
            up = a1[:, LANES:] / a1[:, :LANES]
            blk = blk + jnp.where(lane_lo, gate(br + 1, 0) * low, gate(br + 1, 1) * up)
        o_ref[0, :, cols] = blk.astype(o_ref.dtype)


def _attention(q, ks, v2s, kw, v2w, pen, gl, ocmp, b, s):
    ns = s // SEL_BLOCK
    tq, tk = ATT_TQ, ATT_TK
    assert tq == tk and WINDOW % tk == 0 and WINDOW // tk <= 2 and tq % SEL_BLOCK == 0
    npen = NSA_KV_GROUPS * ns
    gw = HEADS_PER_GROUP * LANES
    ow = HEADS_PER_GROUP * NSA_HD
    r3 = lambda a: a.reshape(b, s, a.shape[-1])
    tile = lambda w: pl.BlockSpec((1, tq, w), lambda i, j, g: (i, j, 0))
    gtile = lambda w: pl.BlockSpec((1, tq, w), lambda i, j, g: (i, j, g))
    full = lambda w: pl.BlockSpec((1, s, w), lambda i, j, g: (i, 0, 0))
    rows = HEADS_PER_GROUP * tq
    kern = functools.partial(_attn_kernel, tq=tq, tk=tk, ns=ns)
    return pl.pallas_call(
        kern,
        grid=(b, s // tq, NSA_KV_GROUPS),
        in_specs=[gtile(gw), full(KV_W), full(2 * KV_W), full(KV_W), full(2 * KV_W),
                  tile(npen), tile(LANES), gtile(ow)],
        out_specs=gtile(ow),
        out_shape=jax.ShapeDtypeStruct((b, s, NSA_WIDTH), BF16),
        scratch_shapes=[pltpu.VMEM((rows, LANES + npen), BF16), pltpu.VMEM((2, rows, tk), F32),
                        pltpu.VMEM((2, rows, tk), BF16), pltpu.VMEM((2, rows, LANES), F32),
                        pltpu.VMEM((2, rows, LANES), F32), pltpu.VMEM((2, rows, 2 * LANES), F32)],
        compiler_params=_params(("parallel", "parallel", "parallel")),
        name="sel_win_attention",
    )(r3(q), r3(ks), r3(v2s), r3(kw), r3(v2w), pen, r3(gl), ocmp)


def _hgrn_kernel(hq_ref, hf_ref, hi_ref, hg_ref, lbraw_ref, onorm_ref, o_ref, st_scr, b_scr,
                 *, layer, n_chunks):
    c = HG_CHUNK

    @pl.when(pl.program_id(1) == 0)
    def _():
        st_scr[...] = jnp.zeros(st_scr.shape, F32)

    raw = lbraw_ref[...]
    ex = jnp.exp(raw - jnp.max(raw, axis=0, keepdims=True))
    sm = ex / jnp.sum(ex, axis=0, keepdims=True)
    lb_all = jnp.zeros((1, raw.shape[1]), F32)
    for l in range(1, layer + 1):
        lb_all = lb_all + sm[l:l + 1, :]

    t_idx = lax.broadcasted_iota(jnp.int32, (c, HG_DK), 0)
    sub = lax.broadcasted_iota(jnp.int32, (8, HG_DK), 0)
    ti = lax.broadcasted_iota(jnp.int32, (c, c), 0)
    si = lax.broadcasted_iota(jnp.int32, (c, c), 1)
    tril = jnp.where(si <= ti, 1.0, 0.0).astype(BF16)
    onorm = onorm_ref[...]
    levels = (32, 16, 8, 4, 2, 1)
    split_bit = ti ^ si
    pair_mask = {m: jnp.where((ti > si) & (split_bit >= m) & (split_bit < 2 * m), 1.0, 0.0) for m in levels}
    second_half = {m: (t_idx & m) != 0 for m in levels}
    sign = {m: jnp.where(second_half[m], 1.0, -1.0) for m in levels}

    def ref_rows(h, m):
        if m >= 4:
            return jnp.concatenate(
                [jnp.broadcast_to(b_scr[h, pl.ds(s0 + m - 1, 1), :], (2 * m, HG_DK))
                 for s0 in range(0, c, 2 * m)], axis=0)
        pieces = []
        for s0 in range(0, c, 8):
            r1 = jnp.broadcast_to(b_scr[h, pl.ds(s0 + 1, 1), :], (8, HG_DK))
            r5 = jnp.broadcast_to(b_scr[h, pl.ds(s0 + 5, 1), :], (8, HG_DK))
            pieces.append(jnp.where(sub < 4, r1, r5))
        return jnp.concatenate(pieces, axis=0)

    def chunk(ci, carry):
        rows = pl.ds(pl.multiple_of(ci * c, c), c)
        for h in range(HG_HEADS):
            lanes = slice(h * HG_DK, (h + 1) * HG_DK)
            lb = lb_all[:, lanes]
            z = hf_ref[rows, lanes]
            hq = hq_ref[rows, lanes]
            q = hq * _sigmoid(hq)
            ez = jnp.exp(-jnp.abs(z))
            big = 1.0 / (1.0 + ez)
            small = ez * big
            f = jnp.maximum(lb + (1.0 - lb) * jnp.where(z >= 0.0, big, small), F_FLOOR)
            logf = jnp.log(f)
            k = (1.0 - lb) * jnp.where(z >= 0.0, small, big)
            v = hi_ref[rows, lanes].astype(BF16)
            bcum = sum(_nn(tril, part) for part in _split3(logf))
            b_scr[h] = bcum

            a = jnp.where(ti == si, jnp.sum(q * k, axis=-1, keepdims=True), 0.0)
            for m in levels:
                if m == 1:
                    r = jnp.where(second_half[m], q * f, k)
                else:
                    w = jnp.exp((bcum - ref_rows(h, m)) * sign[m])
                    r = jnp.where(second_half[m], q, k) * w
                r = r.astype(BF16)
                a = a + _nt(r, r) * pair_mask[m]

            st = st_scr[h]
            o = _nn(a.astype(BF16), v) + _nt((q * jnp.exp(bcum)).astype(BF16), st.astype(BF16))
            b_last = b_scr[h, pl.ds(c - 1, 1), :]
            kd = (k * jnp.exp(b_last - bcum)).astype(BF16)
            st_scr[h] = jnp.exp(b_last) * st + _tn(v, kd)

            o = o * lax.rsqrt(jnp.mean(o * o, axis=-1, keepdims=True) + EPS) * onorm
            gate = hg_ref[rows, lanes]
            o_ref[rows, lanes] = (o * (gate * _sigmoid(gate))).astype(o_ref.dtype)
        return carry

    lax.fori_loop(0, n_chunks, chunk, 0, unroll=2)


def _hgrn(hg4, lb_raw, out_norm, layer, b, s):
    tt = HG_TOKENS
    per = s // tt
    col = lambda k: pl.BlockSpec((tt, HG_WIDTH), lambda i, j, k=k: (i * per + j, k))
    kern = functools.partial(_hgrn_kernel, layer=layer, n_chunks=tt // HG_CHUNK)
    return pl.pallas_call(
        kern,
        grid=(b, per),
        in_specs=[col(0), col(1), col(2), col(3), _resident(lb_raw.shape), _resident((1, HG_DV))],
        out_specs=pl.BlockSpec((tt, HG_WIDTH), lambda i, j: (i * per + j, 0)),
        out_shape=jax.ShapeDtypeStruct((b * s, HG_WIDTH), BF16),
        scratch_shapes=[pltpu.VMEM((HG_HEADS, HG_DV, HG_DK), F32),
                        pltpu.VMEM((HG_HEADS, HG_CHUNK, HG_DK), F32)],
        compiler_params=_params(("parallel", "arbitrary")),
        name="hgrn2",
    )(hg4, hg4, hg4, hg4, lb_raw, out_norm.reshape(1, -1))


def _mixer(h, l, b, s, mix_norm, w_in, cmp_pos_k, cmp_pos_v, cmp_k_w1, cmp_k_w2, cmp_v_w1, cmp_v_w2,
           hgrn_lower_bound, hgrn_out_norm, w_out):
    q, ks, v2s, kw, v2w, kc_in, vc_in, gl, hg4 = _inproj(h, mix_norm[l], _build_w_in(w_in[l]))
    kc = _compress(kc_in.reshape(b, s, KV_W), cmp_pos_k[l], cmp_k_w1[l], cmp_k_w2[l], (0, 1))
    vc2 = _compress(vc_in.reshape(b, s, KV_W), cmp_pos_v[l], cmp_v_w1[l], cmp_v_w2[l], (0, 1, 1, 0))
    ocmp, sel = _cmp_attention(q, kc, vc2, b, s)
    o_nsa = _attention(q, ks, v2s, kw, v2w, sel, gl, ocmp, b, s)
    o_hg = _hgrn(hg4, hgrn_lower_bound, hgrn_out_norm[l], l, b, s)
    return _outproj(o_nsa.reshape(b * s, NSA_WIDTH), o_hg, w_out[l], h)


def kernel(x, ffn1_norm, ffn1_w_gu, ffn1_w_down, mix_norm, w_in, cmp_pos_k, cmp_pos_v, cmp_k_w1, cmp_k_w2, cmp_v_w1, cmp_v_w2, hgrn_lower_bound, hgrn_out_norm, w_out, ffn2_norm, ffn2_w_gu, ffn2_w_down, final_norm):
    b, s, d = x.shape
    depth = ffn1_norm.shape[0]
    h = x.reshape(b * s, d)
    for l in range(depth):
        h = _ffn(h, ffn1_norm[l], ffn1_w_gu[l], ffn1_w_down[l], final_norm, False)
        h = _mixer(h, l, b, s, mix_norm, w_in, cmp_pos_k, cmp_pos_v, cmp_k_w1, cmp_k_w2, cmp_v_w1,
                   cmp_v_w2, hgrn_lower_bound, hgrn_out_norm, w_out)
        h = _ffn(h, ffn2_norm[l], ffn2_w_gu[l], ffn2_w_down[l], final_norm, l == depth - 1)
    return h.reshape(b, s, d)
```

```python
import functools

import jax
import jax.numpy as jnp
import numpy as np
from jax import lax
from jax.experimental import pallas as pl
from jax.experimental.pallas import tpu as pltpu

F32 = jnp.float32
BF16 = jnp.bfloat16

D_MODEL = 1024
EPS = 1e-6
NEG = -1e30
FORCE = 1e6
F_FLOOR = 1e-30
NSA_HEADS = 8
NSA_KV_GROUPS = 2
HEADS_PER_GROUP = NSA_HEADS // NSA_KV_GROUPS
NSA_HD = 64
CMP_LEN = 32
CMP_STRIDE = 16
CMP_HID = 256
SEL_BLOCK = 64
SEL_TOPK = 16
WINDOW = 512
HG_HEADS = 4
HG_DK = 128
HG_DV = 128
HG_CHUNK = 64
D_FF = 2752
NSA_WIDTH = NSA_HEADS * NSA_HD
HG_WIDTH = HG_HEADS * HG_DV
KV_W = NSA_KV_GROUPS * NSA_HD

LANES = 128
D_FF_PAD = 2816
FF_CHUNK = 256
ROW_TILE = 512
ATT_TQ = 256
ATT_TK = 256
HG_TOKENS = 512
VMEM_LIMIT = 56 * 1024 * 1024

SEL_PENALTY = 2.0 ** 50
ALIBI_SLOPES = tuple(2.0 ** (-8.0 * (i + 1) / NSA_HEADS) for i in range(NSA_HEADS))

SEGMENTS = (
    ("q", NSA_HEADS * LANES, BF16),
    ("ks", NSA_KV_GROUPS * LANES, BF16),
    ("vs", NSA_KV_GROUPS * LANES, BF16),
    ("kw", NSA_KV_GROUPS * LANES, BF16),
    ("vw", NSA_KV_GROUPS * LANES, BF16),
    ("kc", KV_W, F32),
    ("vc", KV_W, F32),
    ("gl", LANES, F32),
    ("hg", 4 * HG_WIDTH, F32),
)
SEG_OFFSETS = tuple(int(v) for v in np.cumsum([0] + [s[1] for s in SEGMENTS]))
IN_COLS_EXT = SEG_OFFSETS[-1]


def _nn(a, b):
    return jnp.dot(a, b, preferred_element_type=F32)


def _nt(a, b):
    return lax.dot_general(a, b, (((1,), (1,)), ((), ())), preferred_element_type=F32)


def _tn(a, b):
    return lax.dot_general(a, b, (((0,), (0,)), ((), ())), preferred_element_type=F32)


def _split2(x):
    hi = x.astype(BF16)
    lo = (x - hi.astype(F32)).astype(BF16)
    return hi, lo


def _split3(x):
    hi = x.astype(BF16)
    r = x - hi.astype(F32)
    mid = r.astype(BF16)
    lo = (r - mid.astype(F32)).astype(BF16)
    return hi, mid, lo


def _dot3(a, b):
    ah, al = _split2(a)
    bh, bl = _split2(b)
    return _nn(ah, bh) + _nn(ah, bl) + _nn(al, bh)


def _sigmoid(x):
    return 1.0 / (1.0 + jnp.exp(-x))


def _rms(x, g):
    return x * lax.rsqrt(jnp.mean(x * x, axis=-1, keepdims=True) + EPS) * g


def _resident(shape):
    nd = len(shape)
    return pl.BlockSpec(shape, lambda *_: (0,) * nd, pipeline_mode=pl.Buffered(1))


def _params(sem):
    return pltpu.CompilerParams(dimension_semantics=sem, vmem_limit_bytes=VMEM_LIMIT)


def _ffn_kernel(x_ref, g_ref, wg_ref, wu_ref, wd_ref, gf_ref, o_ref, *, final):
    x = x_ref[...]
    xn = _rms(x, g_ref[...]).astype(BF16)
    acc = jnp.zeros(x.shape, F32)
    for k in range(D_FF_PAD // FF_CHUNK):
        sl = slice(k * FF_CHUNK, (k + 1) * FF_CHUNK)
        gate = _nn(xn, wg_ref[:, sl])
        up = _nn(xn, wu_ref[:, sl])
        h = (gate * _sigmoid(gate) * up).astype(BF16)
        acc = acc + _nn(h, wd_ref[sl, :])
    y = x + 0.5 * acc
    if final:
        y = _rms(y, gf_ref[...])
    o_ref[...] = y


def _ffn(h, norm_g, w_gu, w_down, final_g, final):
    t = h.shape[0]
    pad = D_FF_PAD - D_FF
    wg = jnp.pad(w_gu[:, :D_FF], ((0, 0), (0, pad))).astype(BF16)
    wu = jnp.pad(w_gu[:, D_FF:], ((0, 0), (0, pad))).astype(BF16)
    wd = jnp.pad(w_down, ((0, pad), (0, 0))).astype(BF16)
    row = pl.BlockSpec((ROW_TILE, D_MODEL), lambda i: (i, 0))
    return pl.pallas_call(
        functools.partial(_ffn_kernel, final=final),
        grid=(t // ROW_TILE,),
        in_specs=[row, _resident((1, D_MODEL)), _resident(wg.shape), _resident(wu.shape),
                  _resident(wd.shape), _resident((1, D_MODEL))],
        out_specs=row,
        out_shape=jax.ShapeDtypeStruct((t, D_MODEL), F32),
        compiler_params=_params(("parallel",)),
        name="ffn",
    )(h, norm_g.reshape(1, -1), wg, wu, wd, final_g.reshape(1, -1))


def _inproj_kernel(x_ref, g_ref, w_ref, qfeat_ref, *o_refs, seq):
    xn = _rms(x_ref[...], g_ref[...]).astype(BF16)
    rows = x_ref.shape[0]
    kpos = (pl.program_id(0) % (seq // rows)) * rows + lax.broadcasted_iota(jnp.int32, (rows, LANES), 0)
    lane = lax.broadcasted_iota(jnp.int32, (rows, LANES), 1)
    blk = kpos // SEL_BLOCK
    digit = lambda f0: jnp.where(lane == f0, blk.astype(F32),
                                 jnp.where(lane == f0 + 1, (kpos % SEL_BLOCK).astype(F32), 0.0))
    halves = [NSA_HD * (1 - g) for g in range(NSA_KV_GROUPS)]
    kfeat = jnp.concatenate([digit(f0) for f0 in halves], axis=1)
    is_pen = lambda f0: (lane == f0 + 1 + blk) & (blk >= 1) & (blk <= seq // SEL_BLOCK - 2)
    kpen = jnp.concatenate([jnp.where(is_pen(f0), -SEL_PENALTY, 0.0) for f0 in halves], axis=1)
    ones_half = jnp.where(lane >= NSA_HD, 1.0, 0.0)
    vfeat = jnp.concatenate([ones_half] * NSA_KV_GROUPS, axis=1)
    for o_ref, (name, width, dtype), start in zip(o_refs, SEGMENTS, SEG_OFFSETS[:-1], strict=True):
        y = _nn(xn, w_ref[:, start:start + width])
        if name == "q":
            y = y + qfeat_ref[...]
        elif name == "ks":
            y = y + (kfeat + kpen)
        elif name == "kw":
            y = y + kfeat
        elif name in ("vs", "vw"):
            y = y + vfeat
        o_ref[...] = y.astype(dtype)


def _q_features():
    feat = np.zeros((1, NSA_HEADS * LANES), np.float32)
    for h in range(NSA_HEADS):
        f0 = h * LANES + NSA_HD * (1 - h // HEADS_PER_GROUP)
        feat[0, f0] = SEL_BLOCK * ALIBI_SLOPES[h]
        feat[0, f0 + 1] = ALIBI_SLOPES[h]
    return jnp.asarray(feat)


def _build_w_in(w_in):
    sizes = (NSA_WIDTH, KV_W, KV_W, KV_W, KV_W, KV_W, KV_W, NSA_HEADS * 3,
             HG_WIDTH, HG_WIDTH, HG_WIDTH, HG_WIDTH)
    splits = [int(v) for v in np.cumsum(sizes)[:-1]]
    wq, wkc, wvc, wks, wvs, wkw, wvw, wgl, whq, whf, whi, whg = jnp.split(w_in, splits, axis=1)
    d = w_in.shape[0]
    scale = NSA_HD ** -0.5
    zero = jnp.zeros((d, NSA_HD), w_in.dtype)
    q_cols = []
    for h in range(NSA_HEADS):
        w = wq[:, h * NSA_HD:(h + 1) * NSA_HD] * scale
        q_cols += [w, zero] if h // HEADS_PER_GROUP == 0 else [zero, w]
    g0, g1 = (lambda w: w[:, :NSA_HD]), (lambda w: w[:, NSA_HD:])
    k_blocks = lambda w: [g0(w), zero, zero, g1(w)]
    v_blocks = lambda w: [g0(w), zero, g1(w), zero]
    gl = wgl.reshape(d, NSA_HEADS, 3).transpose(0, 2, 1).reshape(d, 3 * NSA_HEADS)
    gl = jnp.pad(gl, ((0, 0), (0, LANES - 3 * NSA_HEADS)))
    cols = (q_cols + k_blocks(wks) + v_blocks(wvs) + k_blocks(wkw) + v_blocks(wvw)
            + [wkc, wvc, gl, whq, whf, whi, whg])
    return jnp.concatenate(cols, axis=1).astype(BF16)


def _inproj(h, norm_g, w_ext, seq):
    t = h.shape[0]
    assert seq % ROW_TILE == 0
    row = lambda w: pl.BlockSpec((ROW_TILE, w), lambda i: (i, 0))
    return pl.pallas_call(
        functools.partial(_inproj_kernel, seq=seq),
        grid=(t // ROW_TILE,),
        in_specs=[row(D_MODEL), _resident((1, D_MODEL)), _resident(w_ext.shape),
                  _resident((1, NSA_HEADS * LANES))],
        out_specs=[row(w) for _, w, _ in SEGMENTS],
        out_shape=[jax.ShapeDtypeStruct((t, w), dt) for _, w, dt in SEGMENTS],
        compiler_params=_params(("parallel",)),
        name="inproj",
    )(h, norm_g.reshape(1, -1), w_ext, _q_features())


def _outproj_kernel(a_ref, b_ref, w_ref, res_ref, o_ref):
    o_ref[...] = (res_ref[...] + _nn(a_ref[...], w_ref[:NSA_WIDTH, :])
                  + _nn(b_ref[...], w_ref[NSA_WIDTH:, :]))


def _outproj(o_nsa, o_hg, w_out, res):
    t = res.shape[0]
    row = lambda w: pl.BlockSpec((ROW_TILE, w), lambda i: (i, 0))
    return pl.pallas_call(
        _outproj_kernel,
        grid=(t // ROW_TILE,),
        in_specs=[row(NSA_WIDTH), row(HG_WIDTH), _resident(w_out.shape), row(D_MODEL)],
        out_specs=row(D_MODEL),
        out_shape=jax.ShapeDtypeStruct((t, D_MODEL), F32),
        compiler_params=_params(("parallel",)),
        name="outproj",
    )(o_nsa, o_hg, w_out.astype(BF16), res)


def _gelu_tanh(x):
    return 0.5 * x * (1.0 + jnp.tanh(0.7978845608028654 * (x + 0.044715 * (x * x * x))))


def _compress_kernel(kv_ref, pos_ref, w1_ref, w2_ref, o_ref):
    nbp = o_ref.shape[1]
    hid = NSA_KV_GROUPS * CMP_HID
    ha = jnp.zeros((nbp, hid), F32)
    hb = jnp.zeros((nbp, hid), F32)
    for l in range(CMP_STRIDE):
        x = kv_ref[0, pl.ds(l, nbp, stride=CMP_STRIDE), :]
        ha = ha + _dot3(x + pos_ref[l:l + 1, :], w1_ref[l])
        hb = hb + _dot3(x + pos_ref[CMP_STRIDE + l:CMP_STRIDE + l + 1, :], w1_ref[CMP_STRIDE + l])
    act = _gelu_tanh(ha + pltpu.roll(hb, nbp - 1, axis=0))
    out = jnp.zeros(o_ref.shape[1:], F32)
    for g in range(NSA_KV_GROUPS):
        out = out + _dot3(act[:, g * CMP_HID:(g + 1) * CMP_HID], w2_ref[g])
    o_ref[0] = out


def _compress(kv, pos, w1, w2, reps):
    b, s, _ = kv.shape
    nbp = s // CMP_STRIDE
    w1l = w1.reshape(CMP_LEN, NSA_HD, CMP_HID)
    z1 = jnp.zeros_like(w1l)
    w1p = jnp.concatenate([jnp.concatenate([w1l, z1], axis=2), jnp.concatenate([z1, w1l], axis=2)], axis=1)
    pos2 = jnp.concatenate([pos] * NSA_KV_GROUPS, axis=1)
    zero = jnp.zeros_like(w2)
    w2p = jnp.stack([jnp.concatenate([w2 if r == g else zero for r in reps], axis=1)
                     for g in range(NSA_KV_GROUPS)])
    width = len(reps) * NSA_HD
    return pl.pallas_call(
        _compress_kernel,
        grid=(b,),
        in_specs=[pl.BlockSpec((1, s, KV_W), lambda i: (i, 0, 0)),
                  _resident(pos2.shape), _resident(w1p.shape), _resident(w2p.shape)],
        out_specs=pl.BlockSpec((1, nbp, width), lambda i: (i, 0, 0)),
        out_shape=jax.ShapeDtypeStruct((b, nbp, width), F32),
        compiler_params=_params(("parallel",)),
        name="compress",
    )(kv, pos2, w1p, w2p)


def _pair_blocks(g, jj):
    lower = slice(0, LANES) if g == 0 else slice(LANES, 2 * LANES)
    upper = slice(LANES, 2 * LANES) if g == 0 else slice(0, LANES)
    return lower, upper


def _cmp_kernel(q_ref, kc_ref, vc2_ref, mt_ref, ocmp_ref, sel_ref, *, tq, ns, n_sel):
    nbp = kc_ref.shape[1]
    q0 = pl.program_id(1) * tq
    pos = q0 + lax.broadcasted_iota(jnp.int32, (tq, nbp), 0)
    blk_end = lax.broadcasted_iota(jnp.int32, (tq, nbp), 1) * CMP_STRIDE + (CMP_LEN - 1)
    valid = blk_end <= pos
    row_ok = (q0 + lax.broadcasted_iota(jnp.int32, (tq, 1), 0)) >= CMP_LEN - 1
    kc = kc_ref[0].astype(BF16)
    vc2 = vc2_ref[0].astype(BF16)
    lane_c = lax.broadcasted_iota(jnp.int32, (nbp, LANES), 1)
    c_idx = lax.broadcasted_iota(jnp.int32, (nbp, LANES), 0)
    per = SEL_BLOCK // CMP_STRIDE
    feat_a = (c_idx // per - q0 // SEL_BLOCK).astype(F32)
    feat_b = ((c_idx % per) * CMP_STRIDE + (CMP_LEN - 1)).astype(F32)
    lane_lo = lax.broadcasted_iota(jnp.int32, (tq, LANES), 1) < NSA_HD
    blk_f = lax.broadcasted_iota(jnp.int32, (ns, tq), 0).astype(F32)
    blk = lax.broadcasted_iota(jnp.int32, (ns, tq), 0)
    pos_t = q0 + lax.broadcasted_iota(jnp.int32, (ns, tq), 1)
    cur = pos_t // SEL_BLOCK
    forced = (blk == 0) | (blk == cur) | (blk == cur - 1)
    causal = blk * SEL_BLOCK <= pos_t
    sel_t = []
    for g in range(NSA_KV_GROUPS):
        f0 = NSA_HD * (1 - g)
        feat = jnp.where(lane_c == f0, feat_a, jnp.where(lane_c == f0 + 1, feat_b, 0.0))
        kc_g = jnp.where((lane_c // NSA_HD) == g, kc, feat.astype(BF16))
        imp = jnp.zeros((tq, nbp), F32)
        acc = []
        for hh in range(HEADS_PER_GROUP):
            h = g * HEADS_PER_GROUP + hh
            s = _nt(q_ref[0, :, h * LANES:(h + 1) * LANES], kc_g)
            s = jnp.where(valid, s, NEG)
            e = jnp.exp(s - jnp.max(s, axis=-1, keepdims=True))
            inv = jnp.where(row_ok, 1.0 / jnp.sum(e, axis=-1, keepdims=True), 0.0)
            p = e * inv
            imp = imp + p
            acc.append(_nn(p.astype(BF16), vc2))
        for jj in range(HEADS_PER_GROUP // 2):
            lower, upper = _pair_blocks(g, jj)
            blk_out = jnp.where(lane_lo, acc[2 * jj][:, lower], acc[2 * jj + 1][:, upper])
            c0 = (g * HEADS_PER_GROUP + 2 * jj) * NSA_HD
            ocmp_ref[0, :, c0:c0 + LANES] = blk_out
        p_slc = sum(_nt(mt_ref[...], part) for part in _split3(imp))
        score = jnp.where(forced, FORCE, jnp.where(causal, p_slc, NEG))
        taken = -jnp.inf
        for _ in range(n_sel):
            top = jnp.max(score, axis=0, keepdims=True)
            first = jnp.min(jnp.where(score == top, blk_f, float(ns)), axis=0, keepdims=True)
            score = jnp.where(blk_f == first, taken, score)
        pen = jnp.where(score == taken, 0.0, SEL_PENALTY)
        pen = jnp.where((blk >= 2) & (blk <= ns - 1), pltpu.roll(pen, 1, axis=0), 0.0)
        if ns < NSA_HD:
            pen = jnp.concatenate([pen, jnp.zeros((NSA_HD - ns, tq), F32)], axis=0)
        zero = jnp.zeros((NSA_HD, tq), F32)
        sel_t += [zero, pen] if g == 0 else [pen, zero]
    sel_ref[0] = jnp.concatenate(sel_t, axis=0).T.astype(BF16)


def _importance_matrix(nbp, ns):
    per = SEL_BLOCK // CMP_STRIDE
    m = np.zeros((ns, nbp), np.float32)
    for n in range(ns):
        for c in range(per * n, per * (n + 1)):
            for cc in (c - 1, c):
                if 0 <= cc < nbp - 1:
                    m[n, cc] += 1.0
    return jnp.asarray(m, BF16)


def _cmp_attention(q, kc, vc2, b, s):
    nbp = s // CMP_STRIDE
    ns = s // SEL_BLOCK
    tq = ATT_TQ
    assert ns <= NSA_HD
    q3 = q.reshape(b, s, NSA_HEADS * LANES)
    kern = functools.partial(_cmp_kernel, tq=tq, ns=ns, n_sel=min(SEL_TOPK, ns))
    return pl.pallas_call(
        kern,
        grid=(b, s // tq),
        in_specs=[pl.BlockSpec((1, tq, NSA_HEADS * LANES), lambda i, j: (i, j, 0)),
                  pl.BlockSpec((1, nbp, KV_W), lambda i, j: (i, 0, 0)),
                  pl.BlockSpec((1, nbp, 2 * KV_W), lambda i, j: (i, 0, 0)),
                  _resident((ns, nbp))],
        out_specs=[pl.BlockSpec((1, tq, NSA_WIDTH), lambda i, j: (i, j, 0)),
                   pl.BlockSpec((1, tq, NSA_KV_GROUPS * LANES), lambda i, j: (i, j, 0))],
        out_shape=[jax.ShapeDtypeStruct((b, s, NSA_WIDTH), F32),
                   jax.ShapeDtypeStruct((b, s, NSA_KV_GROUPS * LANES), BF16)],
        compiler_params=_params(("parallel", "parallel")),
        name="cmp_topk",
    )(q3, kc, vc2, _importance_matrix(nbp, ns))


def _attn_kernel(q_ref, ks_ref, vs_ref, kw_ref, vw_ref, penq_ref, gl_ref, gexp_ref, ocmp_ref,
                 o_ref, qa_scr, s_scr, p_scr, al_scr, m_scr, acc_scr, *, tq, tk):
    i = pl.program_id(1)
    rows_of = lambda hh: slice(hh * tq, (hh + 1) * tq)
    lane_lo = lax.broadcasted_iota(jnp.int32, (tq, LANES), 1) < NSA_HD
    rel = (lax.broadcasted_iota(jnp.int32, (tq, tk), 1)
           - lax.broadcasted_iota(jnp.int32, (tq, tk), 0)).astype(F32)

    for hh in range(HEADS_PER_GROUP):
        qa_scr[rows_of(hh)] = q_ref[0, :, hh * LANES:(hh + 1) * LANES] + penq_ref[0]

    def scores(k_ref, j, slot):
        s_scr[slot] = _nt(qa_scr[...], k_ref[0, pl.ds(pl.multiple_of(j * tk, tk), tk), :])

    def probs(br, j, mode, slot, first, exists=None):
        for hh in range(HEADS_PER_GROUP):
            r = rows_of(hh)
            s = s_scr[slot, r]
            if mode == "causal":
                s = jnp.where(rel <= 0.0, s, NEG)
            elif mode == "lower":
                s = jnp.where(rel + (j * tk - i * tq).astype(F32) > -float(WINDOW), s, NEG)
            if exists is not None:
                s = jnp.where(exists, s, NEG)
            m_cur = jnp.max(s, axis=-1, keepdims=True)
            if first:
                m_new = jnp.broadcast_to(m_cur, (tq, LANES))
            else:
                m_prev = m_scr[br, r]
                m_new = jnp.maximum(m_prev, m_cur)
                al_scr[slot, r] = jnp.exp(m_prev - m_new)
            m_scr[br, r] = m_new
            p_scr[slot, r] = jnp.exp(s - jnp.concatenate([m_new] * (tk // LANES), axis=1)).astype(BF16)

    def accumulate(br, v_ref, j, slot, first):
        pv = _nn(p_scr[slot], v_ref[0, pl.ds(pl.multiple_of(j * tk, tk), tk), :])
        for hh in range(HEADS_PER_GROUP):
            r = rows_of(hh)
            if first:
                acc_scr[br, r] = pv[r]
            else:
                acc_scr[br, r] = al_scr[slot, r] * acc_scr[br, r] + pv[r]

    last = jnp.maximum(i - 1, 0)
    tile_of = lambda n: jnp.minimum(n - 1, last)
    scores(ks_ref, i, 0)
    scores(ks_ref, tile_of(1), 1)
    probs(0, i, "causal", 0, True)
    scores(ks_ref, tile_of(2), 0)
    probs(0, tile_of(1), None, 1, False)
    accumulate(0, vs_ref, i, 0, True)

    def sel_step(n, parity):
        scores(ks_ref, tile_of(n + 2), parity)
        probs(0, tile_of(n + 1), None, 1 - parity, False)
        accumulate(0, vs_ref, n - 1, parity, False)

    def sel_two_steps(t, carry):
        sel_step(2 * t + 1, 1)
        sel_step(2 * t + 2, 0)
        return carry

    lax.fori_loop(0, i // 2, sel_two_steps, 0)

    @pl.when(i % 2 == 1)
    def _():
        sel_step(i, 1)

    n_back = WINDOW // tk
    back = [(jnp.maximum(i - d, 0), "lower" if d == n_back else None, i >= d) for d in range(1, n_back + 1)]
    scores(kw_ref, i, 0)
    scores(kw_ref, back[0][0], 1)
    probs(1, i, "causal", 0, True)
    accumulate(1, vw_ref, i, 0, True)
    for d, (j, mode, exists) in enumerate(back, start=1):
        if d < n_back:
            scores(kw_ref, back[d][0], (d + 1) % 2)
        probs(1, j, mode, d % 2, False, exists)
        accumulate(1, vw_ref, j, d % 2, False)

    gexp = sum(_nn(part, gexp_ref[0]) for part in _split2(_sigmoid(gl_ref[0])))
    n_pairs = HEADS_PER_GROUP // 2
    gate_blk = lambda br, jj: gexp[:, (br * n_pairs + jj) * LANES:(br * n_pairs + jj + 1) * LANES]
    for jj in range(n_pairs):
        cols = slice(jj * LANES, (jj + 1) * LANES)
        blk = gate_blk(0, jj) * ocmp_ref[0, :, cols]
        for br in range(2):
            even = acc_scr[br, rows_of(2 * jj)]
            odd = acc_scr[br, rows_of(2 * jj + 1)]
            low = even / pltpu.roll(even, NSA_HD, axis=1)
            up = pltpu.roll(odd, NSA_HD, axis=1) / odd
            blk = blk + gate_blk(br + 1, jj) * jnp.where(lane_lo, low, up)
        o_ref[0, :, cols] = blk.astype(o_ref.dtype)


def _gate_expansion():
    n_pairs = HEADS_PER_GROUP // 2
    r = np.zeros((NSA_KV_GROUPS, LANES, 3 * n_pairs * LANES), np.float32)
    for g in range(NSA_KV_GROUPS):
        for br in range(3):
            for jj in range(n_pairs):
                for odd in range(2):
                    src = br * NSA_HEADS + g * HEADS_PER_GROUP + 2 * jj + odd
                    dst = (br * n_pairs + jj) * LANES + odd * NSA_HD
                    r[g, src, dst:dst + NSA_HD] = 1.0
    return jnp.asarray(r, BF16)


def _attention(q, ks, vs, kw, vw, pen, gl, ocmp, b, s):
    tq, tk = ATT_TQ, ATT_TK
    assert tq == tk and WINDOW % tk == 0 and WINDOW // tk <= 2 and tq % SEL_BLOCK == 0
    gw = HEADS_PER_GROUP * LANES
    ow = HEADS_PER_GROUP * NSA_HD
    r3 = lambda a: a.reshape(b, s, a.shape[-1])
    tile = lambda w: pl.BlockSpec((1, tq, w), lambda i, j, g: (i, j, 0))
    gtile = lambda w: pl.BlockSpec((1, tq, w), lambda i, j, g: (i, j, g))
    gfull = pl.BlockSpec((1, s, LANES), lambda i, j, g: (i, 0, g))
    gexp = _gate_expansion()
    rows = HEADS_PER_GROUP * tq
    kern = functools.partial(_attn_kernel, tq=tq, tk=tk)
    return pl.pallas_call(
        kern,
        grid=(b, s // tq, NSA_KV_GROUPS),
        in_specs=[gtile(gw), gfull, gfull, gfull, gfull, gtile(LANES), tile(LANES),
                  pl.BlockSpec((1,) + gexp.shape[1:], lambda i, j, g: (g, 0, 0)), gtile(ow)],
        out_specs=gtile(ow),
        out_shape=jax.ShapeDtypeStruct((b, s, NSA_WIDTH), BF16),
        scratch_shapes=[pltpu.VMEM((rows, LANES), BF16), pltpu.VMEM((2, rows, tk), F32),
                        pltpu.VMEM((2, rows, tk), BF16), pltpu.VMEM((2, rows, LANES), F32),
                        pltpu.VMEM((2, rows, LANES), F32), pltpu.VMEM((2, rows, LANES), F32)],
        compiler_params=_params(("parallel", "parallel", "parallel")),
        name="sel_win_attention",
    )(r3(q), r3(ks), r3(vs), r3(kw), r3(vw), pen, r3(gl), gexp, ocmp)


def _hgrn_kernel(hq_ref, hf_ref, hi_ref, hg_ref, lbraw_ref, onorm_ref, o_ref, st_scr, b_scr,
                 *, layer, n_chunks):
    c = HG_CHUNK

    @pl.when(pl.program_id(1) == 0)
    def _():
        st_scr[...] = jnp.zeros(st_scr.shape, F32)

    raw = lbraw_ref[...]
    ex = jnp.exp(raw - jnp.max(raw, axis=0, keepdims=True))
    sm = ex / jnp.sum(ex, axis=0, keepdims=True)
    lb_all = jnp.zeros((1, raw.shape[1]), F32)
    for l in range(1, layer + 1):
        lb_all = lb_all + sm[l:l + 1, :]

    t_idx = lax.broadcasted_iota(jnp.int32, (c, HG_DK), 0)
    sub = lax.broadcasted_iota(jnp.int32, (8, HG_DK), 0)
    ti = lax.broadcasted_iota(jnp.int32, (c, c), 0)
    si = lax.broadcasted_iota(jnp.int32, (c, c), 1)
    tril = jnp.where(si <= ti, 1.0, 0.0).astype(BF16)
    onorm = onorm_ref[...]
    levels = (32, 16, 8, 4, 2, 1)
    split_bit = ti ^ si
    pair_mask = {m: jnp.where((ti > si) & (split_bit >= m) & (split_bit < 2 * m), 1.0, 0.0) for m in levels}
    second_half = {m: (t_idx & m) != 0 for m in levels}
    sign = {m: jnp.where(second_half[m], 1.0, -1.0) for m in levels}

    def ref_rows(h, m):
        if m >= 4:
            return jnp.concatenate(
                [jnp.broadcast_to(b_scr[h, pl.ds(s0 + m - 1, 1), :], (2 * m, HG_DK))
                 for s0 in range(0, c, 2 * m)], axis=0)
        pieces = []
        for s0 in range(0, c, 8):
            r1 = jnp.broadcast_to(b_scr[h, pl.ds(s0 + 1, 1), :], (8, HG_DK))
            r5 = jnp.broadcast_to(b_scr[h, pl.ds(s0 + 5, 1), :], (8, HG_DK))
            pieces.append(jnp.where(sub < 4, r1, r5))
        return jnp.concatenate(pieces, axis=0)

    def chunk(ci, carry):
        rows = pl.ds(pl.multiple_of(ci * c, c), c)
        for h in range(HG_HEADS):
            lanes = slice(h * HG_DK, (h + 1) * HG_DK)
            lb = lb_all[:, lanes]
            z = hf_ref[rows, lanes]
            hq = hq_ref[rows, lanes]
            q = hq * _sigmoid(hq)
            ez = jnp.exp(-jnp.abs(z))
            big = 1.0 / (1.0 + ez)
            small = ez * big
            f = jnp.maximum(lb + (1.0 - lb) * jnp.where(z >= 0.0, big, small), F_FLOOR)
            logf = jnp.log(f)
            k = (1.0 - lb) * jnp.where(z >= 0.0, small, big)
            v = hi_ref[rows, lanes].astype(BF16)
            bcum = sum(_nn(tril, part) for part in _split3(logf))
            b_scr[h] = bcum

            a = jnp.where(ti == si, jnp.sum(q * k, axis=-1, keepdims=True), 0.0)
            for m in levels:
                if m == 1:
                    r = jnp.where(second_half[m], q * f, k)
                else:
                    w = jnp.exp((bcum - ref_rows(h, m)) * sign[m])
                    r = jnp.where(second_half[m], q, k) * w
                r = r.astype(BF16)
                a = a + _nt(r, r) * pair_mask[m]

            st = st_scr[h]
            o = _nn(a.astype(BF16), v) + _nt((q * jnp.exp(bcum)).astype(BF16), st.astype(BF16))
            b_last = b_scr[h, pl.ds(c - 1, 1), :]
            kd = (k * jnp.exp(b_last - bcum)).astype(BF16)
            st_scr[h] = jnp.exp(b_last) * st + _tn(v, kd)

            o = o * lax.rsqrt(jnp.mean(o * o, axis=-1, keepdims=True) + EPS) * onorm
            gate = hg_ref[rows, lanes]
            o_ref[rows, lanes] = (o * (gate * _sigmoid(gate))).astype(o_ref.dtype)
        return carry

    lax.fori_loop(0, n_chunks, chunk, 0, unroll=4)


def _hgrn(hg4, lb_raw, out_norm, layer, b, s):
    tt = HG_TOKENS
    per = s // tt
    col = lambda k: pl.BlockSpec((tt, HG_WIDTH), lambda i, j, k=k: (i * per + j, k))
    kern = functools.partial(_hgrn_kernel, layer=layer, n_chunks=tt // HG_CHUNK)
    return pl.pallas_call(
        kern,
        grid=(b, per),
        in_specs=[col(0), col(1), col(2), col(3), _resident(lb_raw.shape), _resident((1, HG_DV))],
        out_specs=pl.BlockSpec((tt, HG_WIDTH), lambda i, j: (i * per + j, 0)),
        out_shape=jax.ShapeDtypeStruct((b * s, HG_WIDTH), BF16),
        scratch_shapes=[pltpu.VMEM((HG_HEADS, HG_DV, HG_DK), F32),
                        pltpu.VMEM((HG_HEADS, HG_CHUNK, HG_DK), F32)],
        compiler_params=_params(("parallel", "arbitrary")),
        name="hgrn2",
    )(hg4, hg4, hg4, hg4, lb_raw, out_norm.reshape(1, -1))


def _mixer(h, l, b, s, mix_norm, w_in, cmp_pos_k, cmp_pos_v, cmp_k_w1, cmp_k_w2, cmp_v_w1, cmp_v_w2,
           hgrn_lower_bound, hgrn_out_norm, w_out):
    q, ks, vs, kw, vw, kc_in, vc_in, gl, hg4 = _inproj(h, mix_norm[l], _build_w_in(w_in[l]), s)
    kc = _compress(kc_in.reshape(b, s, KV_W), cmp_pos_k[l], cmp_k_w1[l], cmp_k_w2[l], (0, 1))
    vc2 = _compress(vc_in.reshape(b, s, KV_W), cmp_pos_v[l], cmp_v_w1[l], cmp_v_w2[l], (0, 1, 1, 0))
    ocmp, sel = _cmp_attention(q, kc, vc2, b, s)
    o_nsa = _attention(q, ks, vs, kw, vw, sel, gl, ocmp, b, s)
    o_hg = _hgrn(hg4, hgrn_lower_bound, hgrn_out_norm[l], l, b, s)
    return _outproj(o_nsa.reshape(b * s, NSA_WIDTH), o_hg, w_out[l], h)


def kernel(x, ffn1_norm, ffn1_w_gu, ffn1_w_down, mix_norm, w_in, cmp_pos_k, cmp_pos_v, cmp_k_w1, cmp_k_w2, cmp_v_w1, cmp_v_w2, hgrn_lower_bound, hgrn_out_norm, w_out, ffn2_norm, ffn2_w_gu, ffn2_w_down, final_norm):
    b, s, d = x.shape
    depth = ffn1_norm.shape[0]
    h = x.reshape(b * s, d)
    for l in range(depth):
        h = _ffn(h, ffn1_norm[l], ffn1_w_gu[l], ffn1_w_down[l], final_norm, False)
        h = _mixer(h, l, b, s, mix_norm, w_in, cmp_pos_k, cmp_pos_v, cmp_k_w1, cmp_k_w2, cmp_v_w1,
                   cmp_v_w2, hgrn_lower_bound, hgrn_out_norm, w_out)
        h = _ffn(h, ffn2_norm[l], ffn2_w_gu[l], ffn2_w_down[l], final_norm, l == depth - 1)
    return h.reshape(b, s, d)
```

```python
import functools

import jax
import jax.numpy as jnp
import numpy as np
from jax import lax
from jax.experimental import pallas as pl
from jax.experimental.pallas import tpu as pltpu

F32 = jnp.float32
BF16 = jnp.bfloat16

D_MODEL = 1024
EPS = 1e-6
NEG = -1e30
FORCE = 1e6
F_FLOOR = 1e-30
NSA_HEADS = 8
NSA_KV_GROUPS = 2
HEADS_PER_GROUP = NSA_HEADS // NSA_KV_GROUPS
NSA_HD = 64
CMP_LEN = 32
CMP_STRIDE = 16
CMP_HID = 256
SEL_BLOCK = 64
SEL_TOPK = 16
WINDOW = 512
HG_HEADS = 4
HG_DK = 128
HG_DV = 128
HG_CHUNK = 64
D_FF = 2752
NSA_WIDTH = NSA_HEADS * NSA_HD
HG_WIDTH = HG_HEADS * HG_DV
KV_W = NSA_KV_GROUPS * NSA_HD

LANES = 128
D_FF_PAD = 2816
FF_CHUNK = 256
ROW_TILE = 512
ATT_TQ = 256
ATT_TK = 256
HG_TOKENS = 512
VMEM_LIMIT = 56 * 1024 * 1024

SEL_PENALTY = 2.0 ** 50
ALIBI_SLOPES = tuple(2.0 ** (-8.0 * (i + 1) / NSA_HEADS) for i in range(NSA_HEADS))

SEGMENTS = (
    ("q", NSA_HEADS * LANES, BF16, True),
    ("ks", NSA_KV_GROUPS * LANES, BF16, False),
    ("vs", NSA_KV_GROUPS * LANES, BF16, True),
    ("kw", NSA_KV_GROUPS * LANES, BF16, False),
    ("vw", NSA_KV_GROUPS * LANES, BF16, True),
    ("kc", KV_W, F32, False),
    ("vc", KV_W, F32, False),
    ("gl", LANES, F32, False),
    ("hg", 4 * HG_WIDTH, F32, False),
)


def _segment_offsets():
    offs, nxt = [], {False: 0, True: 0}
    for _, width, _, transposed in SEGMENTS:
        offs.append(nxt[transposed])
        nxt[transposed] += width
    return tuple(offs)


SEG_OFFSETS = _segment_offsets()


def _nn(a, b):
    return jnp.dot(a, b, preferred_element_type=F32)


def _nt(a, b):
    return lax.dot_general(a, b, (((1,), (1,)), ((), ())), preferred_element_type=F32)


def _tn(a, b):
    return lax.dot_general(a, b, (((0,), (0,)), ((), ())), preferred_element_type=F32)


def _split2(x):
    hi = x.astype(BF16)
    lo = (x - hi.astype(F32)).astype(BF16)
    return hi, lo


def _split3(x):
    hi = x.astype(BF16)
    r = x - hi.astype(F32)
    mid = r.astype(BF16)
    lo = (r - mid.astype(F32)).astype(BF16)
    return hi, mid, lo


def _dot3(a, b):
    ah, al = _split2(a)
    bh, bl = _split2(b)
    return _nn(ah, bh) + _nn(ah, bl) + _nn(al, bh)


def _sigmoid(x):
    return 1.0 / (1.0 + jnp.exp(-x))


def _rms(x, g):
    return x * lax.rsqrt(jnp.mean(x * x, axis=-1, keepdims=True) + EPS) * g


def _resident(shape):
    nd = len(shape)
    return pl.BlockSpec(shape, lambda *_: (0,) * nd, pipeline_mode=pl.Buffered(1))


def _params(sem, flags=None):
    return pltpu.CompilerParams(dimension_semantics=sem, vmem_limit_bytes=VMEM_LIMIT, flags=flags)


def _ffn_kernel(x_ref, g_ref, wg_ref, wu_ref, wd_ref, gf_ref, o_ref, *, final):
    x = x_ref[...]
    xn = _rms(x, g_ref[...]).astype(BF16)
    acc = jnp.zeros(x.shape, F32)
    for k in range(D_FF_PAD // FF_CHUNK):
        sl = slice(k * FF_CHUNK, (k + 1) * FF_CHUNK)
        gate = _nn(xn, wg_ref[:, sl])
        up = _nn(xn, wu_ref[:, sl])
        h = (gate * _sigmoid(gate) * up).astype(BF16)
        acc = acc + _nn(h, wd_ref[sl, :])
    y = x + 0.5 * acc
    if final:
        y = _rms(y, gf_ref[...])
    o_ref[...] = y


def _ffn(h, norm_g, w_gu, w_down, final_g, final):
    t = h.shape[0]
    pad = D_FF_PAD - D_FF
    wg = jnp.pad(w_gu[:, :D_FF], ((0, 0), (0, pad))).astype(BF16)
    wu = jnp.pad(w_gu[:, D_FF:], ((0, 0), (0, pad))).astype(BF16)
    wd = jnp.pad(w_down, ((0, pad), (0, 0))).astype(BF16)
    row = pl.BlockSpec((ROW_TILE, D_MODEL), lambda i: (i, 0))
    return pl.pallas_call(
        functools.partial(_ffn_kernel, final=final),
        grid=(t // ROW_TILE,),
        in_specs=[row, _resident((1, D_MODEL)), _resident(wg.shape), _resident(wu.shape),
                  _resident(wd.shape), _resident((1, D_MODEL))],
        out_specs=row,
        out_shape=jax.ShapeDtypeStruct((t, D_MODEL), F32),
        compiler_params=_params(("parallel",)),
        name="ffn",
    )(h, norm_g.reshape(1, -1), wg, wu, wd, final_g.reshape(1, -1))


def _inproj_kernel(x_ref, g_ref, w_ref, wt_ref, qfeat_ref, *o_refs, seq):
    xn = _rms(x_ref[...], g_ref[...]).astype(BF16)
    rows = x_ref.shape[0]
    feat_row = lax.broadcasted_iota(jnp.int32, (NSA_KV_GROUPS * LANES, rows), 0)
    vfeat_t = jnp.where(feat_row % LANES >= NSA_HD, 1.0, 0.0)
    kpos = (pl.program_id(0) % (seq // rows)) * rows + lax.broadcasted_iota(jnp.int32, (rows, LANES), 0)
    lane = lax.broadcasted_iota(jnp.int32, (rows, LANES), 1)
    blk = kpos // SEL_BLOCK
    digit = lambda f0: jnp.where(lane == f0, blk.astype(F32),
                                 jnp.where(lane == f0 + 1, (kpos % SEL_BLOCK).astype(F32), 0.0))
    halves = [NSA_HD * (1 - g) for g in range(NSA_KV_GROUPS)]
    kfeat = jnp.concatenate([digit(f0) for f0 in halves], axis=1)
    is_pen = lambda f0: (lane == f0 + 1 + blk) & (blk >= 1) & (blk <= seq // SEL_BLOCK - 2)
    kpen = jnp.concatenate([jnp.where(is_pen(f0), -SEL_PENALTY, 0.0) for f0 in halves], axis=1)
    for o_ref, (name, width, dtype, transposed), start in zip(o_refs, SEGMENTS, SEG_OFFSETS, strict=True):
        if transposed:
            y = _nt(wt_ref[start:start + width, :], xn)
            if name == "q":
                o_ref[...] = (y + jnp.concatenate([qfeat_ref[...]] * (rows // LANES), axis=1)).astype(dtype)
            else:
                y = (y + vfeat_t).astype(dtype)
                for t in range(rows // ATT_TK):
                    o_ref[t] = y[:, t * ATT_TK:(t + 1) * ATT_TK]
        else:
            y = _nn(xn, w_ref[:, start:start + width])
            if name == "ks":
                y = y + (kfeat + kpen)
            elif name == "kw":
                y = y + kfeat
            o_ref[...] = y.astype(dtype)


def _q_features():
    feat = np.zeros((NSA_HEADS * LANES, LANES), np.float32)
    for h in range(NSA_HEADS):
        f0 = h * LANES + NSA_HD * (1 - h // HEADS_PER_GROUP)
        feat[f0, :] = SEL_BLOCK * ALIBI_SLOPES[h]
        feat[f0 + 1, :] = ALIBI_SLOPES[h]
    return jnp.asarray(feat)


def _build_w_in(w_in):
    sizes = (NSA_WIDTH, KV_W, KV_W, KV_W, KV_W, KV_W, KV_W, NSA_HEADS * 3,
             HG_WIDTH, HG_WIDTH, HG_WIDTH, HG_WIDTH)
    splits = [int(v) for v in np.cumsum(sizes)[:-1]]
    wq, wkc, wvc, wks, wvs, wkw, wvw, wgl, whq, whf, whi, whg = jnp.split(w_in, splits, axis=1)
    d = w_in.shape[0]
    scale = NSA_HD ** -0.5
    zero = jnp.zeros((d, NSA_HD), w_in.dtype)
    q_cols = []
    for h in range(NSA_HEADS):
        w = wq[:, h * NSA_HD:(h + 1) * NSA_HD] * scale
        q_cols += [w, zero] if h // HEADS_PER_GROUP == 0 else [zero, w]
    g0, g1 = (lambda w: w[:, :NSA_HD]), (lambda w: w[:, NSA_HD:])
    k_blocks = lambda w: [g0(w), zero, zero, g1(w)]
    v_blocks = lambda w: [g0(w), zero, g1(w), zero]
    gl = wgl.reshape(d, NSA_HEADS, 3).transpose(0, 2, 1).reshape(d, 3 * NSA_HEADS)
    gl = jnp.pad(gl, ((0, 0), (0, LANES - 3 * NSA_HEADS)))
    by_name = {"q": q_cols, "ks": k_blocks(wks), "vs": v_blocks(wvs), "kw": k_blocks(wkw), "vw": v_blocks(wvw),
               "kc": [wkc], "vc": [wvc], "gl": [gl], "hg": [whq, whf, whi, whg]}
    pick = lambda transposed: jnp.concatenate(
        [c for name, _, _, tr in SEGMENTS if tr == transposed for c in by_name[name]], axis=1).astype(BF16)
    return pick(False), pick(True).T


def _inproj(h, norm_g, w_ext, w_t, seq):
    t = h.shape[0]
    assert seq % ROW_TILE == 0 and ROW_TILE % ATT_TK == 0
    row = lambda w: pl.BlockSpec((ROW_TILE, w), lambda i: (i, 0))
    out_specs, out_shape = [], []
    for name, w, dt, transposed in SEGMENTS:
        if not transposed:
            out_specs.append(row(w))
            out_shape.append(jax.ShapeDtypeStruct((t, w), dt))
        elif name == "q":
            out_specs.append(pl.BlockSpec((w, ROW_TILE), lambda i: (0, i)))
            out_shape.append(jax.ShapeDtypeStruct((w, t), dt))
        else:
            out_specs.append(pl.BlockSpec((ROW_TILE // ATT_TK, w, ATT_TK), lambda i: (i, 0, 0)))
            out_shape.append(jax.ShapeDtypeStruct((t // ATT_TK, w, ATT_TK), dt))
    qfeat = _q_features()
    return pl.pallas_call(
        functools.partial(_inproj_kernel, seq=seq),
        grid=(t // ROW_TILE,),
        in_specs=[row(D_MODEL), _resident((1, D_MODEL)), _resident(w_ext.shape), _resident(w_t.shape),
                  _resident(qfeat.shape)],
        out_specs=out_specs,
        out_shape=out_shape,
        compiler_params=_params(("parallel",)),
        name="inproj",
    )(h, norm_g.reshape(1, -1), w_ext, w_t, qfeat)


def _outproj_kernel(a_ref, b_ref, w_ref, res_ref, o_ref):
    o_ref[...] = (res_ref[...] + _nn(a_ref[...], w_ref[:NSA_WIDTH, :])
                  + _nn(b_ref[...], w_ref[NSA_WIDTH:, :]))


def _outproj(o_nsa, o_hg, w_out, res):
    t = res.shape[0]
    row = lambda w: pl.BlockSpec((ROW_TILE, w), lambda i: (i, 0))
    return pl.pallas_call(
        _outproj_kernel,
        grid=(t // ROW_TILE,),
        in_specs=[row(NSA_WIDTH), row(HG_WIDTH), _resident(w_out.shape), row(D_MODEL)],
        out_specs=row(D_MODEL),
        out_shape=jax.ShapeDtypeStruct((t, D_MODEL), F32),
        compiler_params=_params(("parallel",)),
        name="outproj",
    )(o_nsa, o_hg, w_out.astype(BF16), res)


def _gelu_tanh(x):
    return 0.5 * x * (1.0 + jnp.tanh(0.7978845608028654 * (x + 0.044715 * (x * x * x))))


def _compress_kernel(kv_ref, pos_ref, w1_ref, w2_ref, o_ref):
    nbp = o_ref.shape[1]
    hid = NSA_KV_GROUPS * CMP_HID
    ha = jnp.zeros((nbp, hid), F32)
    hb = jnp.zeros((nbp, hid), F32)
    for l in range(CMP_STRIDE):
        x = kv_ref[0, pl.ds(l, nbp, stride=CMP_STRIDE), :]
        ha = ha + _dot3(x + pos_ref[l:l + 1, :], w1_ref[l])
        hb = hb + _dot3(x + pos_ref[CMP_STRIDE + l:CMP_STRIDE + l + 1, :], w1_ref[CMP_STRIDE + l])
    act = _gelu_tanh(ha + pltpu.roll(hb, nbp - 1, axis=0))
    out = jnp.zeros(o_ref.shape[1:], F32)
    for g in range(NSA_KV_GROUPS):
        out = out + _dot3(act[:, g * CMP_HID:(g + 1) * CMP_HID], w2_ref[g])
    o_ref[0] = out


def _compress(kv, pos, w1, w2):
    b, s, _ = kv.shape
    nbp = s // CMP_STRIDE
    w1l = w1.reshape(CMP_LEN, NSA_HD, CMP_HID)
    z1 = jnp.zeros_like(w1l)
    w1p = jnp.concatenate([jnp.concatenate([w1l, z1], axis=2), jnp.concatenate([z1, w1l], axis=2)], axis=1)
    pos2 = jnp.concatenate([pos] * NSA_KV_GROUPS, axis=1)
    zero = jnp.zeros_like(w2)
    w2p = jnp.stack([jnp.concatenate([w2 if r == g else zero for r in range(NSA_KV_GROUPS)], axis=1)
                     for g in range(NSA_KV_GROUPS)])
    width = KV_W
    return pl.pallas_call(
        _compress_kernel,
        grid=(b,),
        in_specs=[pl.BlockSpec((1, s, KV_W), lambda i: (i, 0, 0)),
                  _resident(pos2.shape), _resident(w1p.shape), _resident(w2p.shape)],
        out_specs=pl.BlockSpec((1, nbp, width), lambda i: (i, 0, 0)),
        out_shape=jax.ShapeDtypeStruct((b, nbp, width), F32),
        compiler_params=_params(("parallel",)),
        name="compress",
    )(kv, pos2, w1p, w2p)


def _cmp_kernel(q_ref, kc_ref, vc_ref, mt_ref, ocmp_ref, sel_ref, *, tq, ns, n_sel):
    nbp = kc_ref.shape[1]
    q0 = pl.program_id(1) * tq
    pos = q0 + lax.broadcasted_iota(jnp.int32, (nbp, tq), 1)
    blk_end = lax.broadcasted_iota(jnp.int32, (nbp, tq), 0) * CMP_STRIDE + (CMP_LEN - 1)
    valid = blk_end <= pos
    row_ok = (q0 + lax.broadcasted_iota(jnp.int32, (1, tq), 1)) >= CMP_LEN - 1
    kc = kc_ref[0].astype(BF16)
    vc_t = vc_ref[0].T.astype(BF16)
    lane_c = lax.broadcasted_iota(jnp.int32, (nbp, LANES), 1)
    c_idx = lax.broadcasted_iota(jnp.int32, (nbp, LANES), 0)
    per = SEL_BLOCK // CMP_STRIDE
    feat_a = (c_idx // per - q0 // SEL_BLOCK).astype(F32)
    feat_b = ((c_idx % per) * CMP_STRIDE + (CMP_LEN - 1)).astype(F32)
    blk_f = lax.broadcasted_iota(jnp.int32, (ns, tq), 0).astype(F32)
    blk = lax.broadcasted_iota(jnp.int32, (ns, tq), 0)
    pos_t = q0 + lax.broadcasted_iota(jnp.int32, (ns, tq), 1)
    cur = pos_t // SEL_BLOCK
    forced = (blk == 0) | (blk == cur) | (blk == cur - 1)
    causal = blk * SEL_BLOCK <= pos_t
    sel_t, out_t = [], []
    for g in range(NSA_KV_GROUPS):
        f0 = NSA_HD * (1 - g)
        feat = jnp.where(lane_c == f0, feat_a, jnp.where(lane_c == f0 + 1, feat_b, 0.0))
        kc_g = jnp.where((lane_c // NSA_HD) == g, kc, feat.astype(BF16))
        vc_g = vc_t[g * NSA_HD:(g + 1) * NSA_HD, :]
        imp = jnp.zeros((nbp, tq), F32)
        for hh in range(HEADS_PER_GROUP):
            h = g * HEADS_PER_GROUP + hh
            s = _nn(kc_g, q_ref[h * LANES:(h + 1) * LANES, :])
            s = jnp.where(valid, s, NEG)
            e = jnp.exp(s - jnp.max(s, axis=0, keepdims=True))
            inv = jnp.where(row_ok, 1.0 / jnp.sum(e, axis=0, keepdims=True), 0.0)
            p = e * inv
            imp = imp + p
            out_t.append(_nn(vc_g, p.astype(BF16)))
        p_slc = sum(_nn(mt_ref[...], part) for part in _split3(imp))
        score = jnp.where(forced, FORCE, jnp.where(causal, p_slc, NEG))
        taken = -jnp.inf
        for _ in range(n_sel):
            top = jnp.max(score, axis=0, keepdims=True)
            first = jnp.min(jnp.where(score == top, blk_f, float(ns)), axis=0, keepdims=True)
            score = jnp.where(blk_f == first, taken, score)
        pen = jnp.where(score == taken, 0.0, SEL_PENALTY)
        pen = jnp.where((blk >= 2) & (blk <= ns - 1), pltpu.roll(pen, 1, axis=0), 0.0)
        if ns < NSA_HD:
            pen = jnp.concatenate([pen, jnp.zeros((NSA_HD - ns, tq), F32)], axis=0)
        zero = jnp.zeros((NSA_HD, tq), F32)
        sel_t += [zero, pen] if g == 0 else [pen, zero]
    sel_ref[0] = jnp.concatenate(sel_t, axis=0).astype(BF16)
    ocmp_ref[0] = jnp.concatenate(out_t, axis=0).T


def _importance_matrix(nbp, ns):
    per = SEL_BLOCK // CMP_STRIDE
    m = np.zeros((ns, nbp), np.float32)
    for n in range(ns):
        for c in range(per * n, per * (n + 1)):
            for cc in (c - 1, c):
                if 0 <= cc < nbp - 1:
                    m[n, cc] += 1.0
    return jnp.asarray(m, BF16)


def _cmp_attention(q_t, kc, vc, b, s):
    nbp = s // CMP_STRIDE
    ns = s // SEL_BLOCK
    tq = ATT_TQ
    nq = s // tq
    assert ns <= NSA_HD
    kern = functools.partial(_cmp_kernel, tq=tq, ns=ns, n_sel=min(SEL_TOPK, ns))
    return pl.pallas_call(
        kern,
        grid=(b, nq),
        in_specs=[pl.BlockSpec((NSA_HEADS * LANES, tq), lambda i, j: (0, i * nq + j)),
                  pl.BlockSpec((1, nbp, KV_W), lambda i, j: (i, 0, 0)),
                  pl.BlockSpec((1, nbp, KV_W), lambda i, j: (i, 0, 0)),
                  _resident((ns, nbp))],
        out_specs=[pl.BlockSpec((1, tq, NSA_WIDTH), lambda i, j: (i, j, 0)),
                   pl.BlockSpec((1, NSA_KV_GROUPS * LANES, tq), lambda i, j: (i, 0, j))],
        out_shape=[jax.ShapeDtypeStruct((b, s, NSA_WIDTH), F32),
                   jax.ShapeDtypeStruct((b, NSA_KV_GROUPS * LANES, s), BF16)],
        compiler_params=_params(("parallel", "parallel")),
        name="cmp_topk",
    )(q_t, kc, vc, _importance_matrix(nbp, ns))


def _attn_kernel(q_ref, ks_ref, vs_ref, kw_ref, vw_ref, penq_ref, gl_ref, gexp_ref, ocmp_ref,
                 o_ref, qa_scr, s_scr, p_scr, al_scr, m_scr, acc_scr, *, tq, tk):
    i = pl.program_id(1)
    cols_of = lambda hh: slice(hh * tq, (hh + 1) * tq)
    lane_lo = lax.broadcasted_iota(jnp.int32, (tq, LANES), 1) < NSA_HD
    rel = (lax.broadcasted_iota(jnp.int32, (tk, tq), 0)
           - lax.broadcasted_iota(jnp.int32, (tk, tq), 1)).astype(F32)

    for hh in range(HEADS_PER_GROUP):
        qa_scr[:, cols_of(hh)] = q_ref[hh * LANES:(hh + 1) * LANES, :] + penq_ref[0]

    def scores(k_ref, j, slot):
        s_scr[slot] = _nn(k_ref[0, pl.ds(pl.multiple_of(j * tk, tk), tk), :], qa_scr[...])

    def probs(br, j, mode, slot, first, exists=None):
        for hh in range(HEADS_PER_GROUP):
            c = cols_of(hh)
            s = s_scr[slot, :, c]
            if mode == "causal":
                s = jnp.where(rel <= 0.0, s, NEG)
            elif mode == "lower":
                s = jnp.where(rel + (j * tk - i * tq).astype(F32) > -float(WINDOW), s, NEG)
            if exists is not None:
                s = jnp.where(exists, s, NEG)
            m_new = jnp.max(s, axis=0, keepdims=True)
            if not first:
                m_prev = m_scr[br, :, c]
                m_new = jnp.maximum(m_prev, m_new)
                al_scr[slot, :, c] = jnp.exp(m_prev - m_new)
            m_scr[br, :, c] = m_new
            p_scr[slot, :, c] = jnp.exp(s - m_new).astype(BF16)

    def accumulate(br, v_ref, j, slot, first):
        pv = _nn(v_ref[j], p_scr[slot])
        for hh in range(HEADS_PER_GROUP):
            c = cols_of(hh)
            acc_scr[br, :, c] = pv[:, c] if first else al_scr[slot, :, c] * acc_scr[br, :, c] + pv[:, c]

    last = jnp.maximum(i - 1, 0)
    tile_of = lambda n: jnp.minimum(n - 1, last)
    scores(ks_ref, i, 0)
    scores(ks_ref, tile_of(1), 1)
    probs(0, i, "causal", 0, True)
    scores(ks_ref, tile_of(2), 0)
    probs(0, tile_of(1), None, 1, False)
    accumulate(0, vs_ref, i, 0, True)

    def sel_step(n, parity):
        scores(ks_ref, tile_of(n + 2), parity)
        probs(0, tile_of(n + 1), None, 1 - parity, False)
        accumulate(0, vs_ref, n - 1, parity, False)

    def sel_two_steps(t, carry):
        sel_step(2 * t + 1, 1)
        sel_step(2 * t + 2, 0)
        return carry

    lax.fori_loop(0, i // 2, sel_two_steps, 0)

    @pl.when(i % 2 == 1)
    def _():
        sel_step(i, 1)

    n_back = WINDOW // tk
    back = [(jnp.maximum(i - d, 0), "lower" if d == n_back else None, i >= d) for d in range(1, n_back + 1)]
    scores(kw_ref, i, 0)
    scores(kw_ref, back[0][0], 1)
    probs(1, i, "causal", 0, True)
    accumulate(1, vw_ref, i, 0, True)
    for d, (j, mode, exists) in enumerate(back, start=1):
        if d < n_back:
            scores(kw_ref, back[d][0], (d + 1) % 2)
        probs(1, j, mode, d % 2, False, exists)
        accumulate(1, vw_ref, j, d % 2, False)

    gexp = sum(_nn(part, gexp_ref[0]) for part in _split2(_sigmoid(gl_ref[0])))
    n_pairs = HEADS_PER_GROUP // 2
    gate_blk = lambda br, jj: gexp[:, (br * n_pairs + jj) * LANES:(br * n_pairs + jj + 1) * LANES]
    for jj in range(n_pairs):
        cols = slice(jj * LANES, (jj + 1) * LANES)
        blk = gate_blk(0, jj) * ocmp_ref[0, :, cols]
        for br in range(2):
            even = acc_scr[br, :, cols_of(2 * jj)].T
            odd = acc_scr[br, :, cols_of(2 * jj + 1)].T
            low = even / pltpu.roll(even, NSA_HD, axis=1)
            up = pltpu.roll(odd, NSA_HD, axis=1) / odd
            blk = blk + gate_blk(br + 1, jj) * jnp.where(lane_lo, low, up)
        o_ref[0, :, cols] = blk.astype(o_ref.dtype)


def _gate_expansion():
    n_pairs = HEADS_PER_GROUP // 2
    r = np.zeros((NSA_KV_GROUPS, LANES, 3 * n_pairs * LANES), np.float32)
    for g in range(NSA_KV_GROUPS):
        for br in range(3):
            for jj in range(n_pairs):
                for odd in range(2):
                    src = br * NSA_HEADS + g * HEADS_PER_GROUP + 2 * jj + odd
                    dst = (br * n_pairs + jj) * LANES + odd * NSA_HD
                    r[g, src, dst:dst + NSA_HD] = 1.0
    return jnp.asarray(r, BF16)


def _attention(q_t, ks, vs_t, kw, vw_t, pen_t, gl, ocmp, b, s):
    tq, tk = ATT_TQ, ATT_TK
    assert tq == tk and WINDOW % tk == 0 and WINDOW // tk <= 2 and tq % SEL_BLOCK == 0
    nq = s // tq
    gw = HEADS_PER_GROUP * LANES
    ow = HEADS_PER_GROUP * NSA_HD
    r3 = lambda a: a.reshape(b, s, a.shape[-1])
    tile = lambda w: pl.BlockSpec((1, tq, w), lambda i, j, g: (i, j, 0))
    gtile = lambda w: pl.BlockSpec((1, tq, w), lambda i, j, g: (i, j, g))
    k_full = pl.BlockSpec((1, s, LANES), lambda i, j, g: (i, 0, g))
    v_full = pl.BlockSpec((s // tk, LANES, tk), lambda i, j, g: (i, g, 0))
    gexp = _gate_expansion()
    cols = HEADS_PER_GROUP * tq
    kern = functools.partial(_attn_kernel, tq=tq, tk=tk)
    return pl.pallas_call(
        kern,
        grid=(b, nq, NSA_KV_GROUPS),
        in_specs=[pl.BlockSpec((gw, tq), lambda i, j, g: (g, i * nq + j)), k_full, v_full, k_full, v_full,
                  pl.BlockSpec((1, LANES, tq), lambda i, j, g: (i, g, j)), tile(LANES),
                  pl.BlockSpec((1,) + gexp.shape[1:], lambda i, j, g: (g, 0, 0)), gtile(ow)],
        out_specs=gtile(ow),
        out_shape=jax.ShapeDtypeStruct((b, s, NSA_WIDTH), BF16),
        scratch_shapes=[pltpu.VMEM((LANES, cols), BF16), pltpu.VMEM((2, tk, cols), F32),
                        pltpu.VMEM((2, tk, cols), BF16), pltpu.VMEM((2, 1, cols), F32),
                        pltpu.VMEM((2, 1, cols), F32), pltpu.VMEM((2, LANES, cols), F32)],
        compiler_params=_params(("parallel", "parallel", "parallel")),
        name="sel_win_attention",
    )(q_t, r3(ks), vs_t, r3(kw), vw_t, pen_t, r3(gl), gexp, ocmp)


def _hgrn_kernel(hq_ref, hf_ref, hi_ref, hg_ref, lbraw_ref, onorm_ref, o_ref, st_scr, b_scr,
                 *, layer, n_chunks):
    c = HG_CHUNK

    @pl.when(pl.program_id(1) == 0)
    def _():
        st_scr[...] = jnp.zeros(st_scr.shape, F32)

    raw = lbraw_ref[...]
    ex = jnp.exp(raw - jnp.max(raw, axis=0, keepdims=True))
    sm = ex / jnp.sum(ex, axis=0, keepdims=True)
    lb_all = jnp.zeros((1, raw.shape[1]), F32)
    for l in range(1, layer + 1):
        lb_all = lb_all + sm[l:l + 1, :]

    t_idx = lax.broadcasted_iota(jnp.int32, (c, HG_DK), 0)
    sub = lax.broadcasted_iota(jnp.int32, (8, HG_DK), 0)
    ti = lax.broadcasted_iota(jnp.int32, (c, c), 0)
    si = lax.broadcasted_iota(jnp.int32, (c, c), 1)
    tril = jnp.where(si <= ti, 1.0, 0.0).astype(BF16)
    onorm = onorm_ref[...]
    levels = (32, 16, 8, 4, 2, 1)
    split_bit = ti ^ si
    pair_mask = {m: jnp.where((ti > si) & (split_bit >= m) & (split_bit < 2 * m), 1.0, 0.0) for m in levels}
    second_half = {m: (t_idx & m) != 0 for m in levels}
    sign = {m: jnp.where(second_half[m], 1.0, -1.0) for m in levels}

    def ref_rows(h, m):
        if m >= 4:
            return jnp.concatenate(
                [jnp.broadcast_to(b_scr[h, pl.ds(s0 + m - 1, 1), :], (2 * m, HG_DK))
                 for s0 in range(0, c, 2 * m)], axis=0)
        pieces = []
        for s0 in range(0, c, 8):
            r1 = jnp.broadcast_to(b_scr[h, pl.ds(s0 + 1, 1), :], (8, HG_DK))
            r5 = jnp.broadcast_to(b_scr[h, pl.ds(s0 + 5, 1), :], (8, HG_DK))
            pieces.append(jnp.where(sub < 4, r1, r5))
        return jnp.concatenate(pieces, axis=0)

    def chunk(ci, carry):
        rows = pl.ds(pl.multiple_of(ci * c, c), c)
        for h in range(HG_HEADS):
            lanes = slice(h * HG_DK, (h + 1) * HG_DK)
            lb = lb_all[:, lanes]
            z = hf_ref[rows, lanes]
            hq = hq_ref[rows, lanes]
            q = hq * _sigmoid(hq)
            ez = jnp.exp(-jnp.abs(z))
            big = 1.0 / (1.0 + ez)
            small = ez * big
            f = jnp.maximum(lb + (1.0 - lb) * jnp.where(z >= 0.0, big, small), F_FLOOR)
            logf = jnp.log(f)
            k = (1.0 - lb) * jnp.where(z >= 0.0, small, big)
            v = hi_ref[rows, lanes].astype(BF16)
            bcum = sum(_nn(tril, part) for part in _split3(logf))
            b_scr[h] = bcum

            a = jnp.where(ti == si, jnp.sum(q * k, axis=-1, keepdims=True), 0.0)
            for m in levels:
                if m == 1:
                    r = jnp.where(second_half[m], q * f, k)
                else:
                    w = jnp.exp((bcum - ref_rows(h, m)) * sign[m])
                    r = jnp.where(second_half[m], q, k) * w
                r = r.astype(BF16)
                a = a + _nt(r, r) * pair_mask[m]

            st = st_scr[h]
            o = _nn(a.astype(BF16), v) + _nt((q * jnp.exp(bcum)).astype(BF16), st.astype(BF16))
            b_last = b_scr[h, pl.ds(c - 1, 1), :]
            kd = (k * jnp.exp(b_last - bcum)).astype(BF16)
            st_scr[h] = jnp.exp(b_last) * st + _tn(v, kd)

            o = o * lax.rsqrt(jnp.mean(o * o, axis=-1, keepdims=True) + EPS) * onorm
            gate = hg_ref[rows, lanes]
            o_ref[rows, lanes] = (o * (gate * _sigmoid(gate))).astype(o_ref.dtype)
        return carry

    lax.fori_loop(0, n_chunks, chunk, 0, unroll=4)


def _hgrn(hg4, lb_raw, out_norm, layer, b, s):
    tt = HG_TOKENS
    per = s // tt
    col = lambda k: pl.BlockSpec((tt, HG_WIDTH), lambda i, j, k=k: (i * per + j, k))
    kern = functools.partial(_hgrn_kernel, layer=layer, n_chunks=tt // HG_CHUNK)
    return pl.pallas_call(
        kern,
        grid=(b, per),
        in_specs=[col(0), col(1), col(2), col(3), _resident(lb_raw.shape), _resident((1, HG_DV))],
        out_specs=pl.BlockSpec((tt, HG_WIDTH), lambda i, j: (i * per + j, 0)),
        out_shape=jax.ShapeDtypeStruct((b * s, HG_WIDTH), BF16),
        scratch_shapes=[pltpu.VMEM((HG_HEADS, HG_DV, HG_DK), F32),
                        pltpu.VMEM((HG_HEADS, HG_CHUNK, HG_DK), F32)],
        compiler_params=_params(("parallel", "arbitrary")),
        name="hgrn2",
    )(hg4, hg4, hg4, hg4, lb_raw, out_norm.reshape(1, -1))


def _mixer(h, l, b, s, mix_norm, w_in, cmp_pos_k, cmp_pos_v, cmp_k_w1, cmp_k_w2, cmp_v_w1, cmp_v_w2,
           hgrn_lower_bound, hgrn_out_norm, w_out):
    q_t, ks, vs_t, kw, vw_t, kc_in, vc_in, gl, hg4 = _inproj(h, mix_norm[l], *_build_w_in(w_in[l]), s)
    kc = _compress(kc_in.reshape(b, s, KV_W), cmp_pos_k[l], cmp_k_w1[l], cmp_k_w2[l])
    vc = _compress(vc_in.reshape(b, s, KV_W), cmp_pos_v[l], cmp_v_w1[l], cmp_v_w2[l])
    ocmp, pen_t = _cmp_attention(q_t, kc, vc, b, s)
    o_nsa = _attention(q_t, ks, vs_t, kw, vw_t, pen_t, gl, ocmp, b, s)
    o_hg = _hgrn(hg4, hgrn_lower_bound, hgrn_out_norm[l], l, b, s)
    return _outproj(o_nsa.reshape(b * s, NSA_WIDTH), o_hg, w_out[l], h)


def kernel(x, ffn1_norm, ffn1_w_gu, ffn1_w_down, mix_norm, w_in, cmp_pos_k, cmp_pos_v, cmp_k_w1, cmp_k_w2, cmp_v_w1, cmp_v_w2, hgrn_lower_bound, hgrn_out_norm, w_out, ffn2_norm, ffn2_w_gu, ffn2_w_down, final_norm):
    b, s, d = x.shape
    depth = ffn1_norm.shape[0]
    h = x.reshape(b * s, d)
    for l in range(depth):
        h = _ffn(h, ffn1_norm[l], ffn1_w_gu[l], ffn1_w_down[l], final_norm, False)
        h = _mixer(h, l, b, s, mix_norm, w_in, cmp_pos_k, cmp_pos_v, cmp_k_w1, cmp_k_w2, cmp_v_w1,
                   cmp_v_w2, hgrn_lower_bound, hgrn_out_norm, w_out)
        h = _ffn(h, ffn2_norm[l], ffn2_w_gu[l], ffn2_w_down[l], final_norm, l == depth - 1)
    return h.reshape(b, s, d)
```

```python
import functools

import jax
import jax.numpy as jnp
import numpy as np
from jax import lax
from jax.experimental import pallas as pl
from jax.experimental.pallas import tpu as pltpu

F32 = jnp.float32
BF16 = jnp.bfloat16

D_MODEL = 1024
EPS = 1e-6
NEG = -1e30
FORCE = 1e6
F_FLOOR = 1e-30
NSA_HEADS = 8
NSA_KV_GROUPS = 2
HEADS_PER_GROUP = NSA_HEADS // NSA_KV_GROUPS
NSA_HD = 64
CMP_LEN = 32
CMP_STRIDE = 16
CMP_HID = 256
SEL_BLOCK = 64
SEL_TOPK = 16
WINDOW = 512
HG_HEADS = 4
HG_DK = 128
HG_DV = 128
HG_CHUNK = 64
D_FF = 2752
NSA_WIDTH = NSA_HEADS * NSA_HD
HG_WIDTH = HG_HEADS * HG_DV
KV_W = NSA_KV_GROUPS * NSA_HD

LANES = 128
D_FF_PAD = 2816
FF_CHUNK = 256
ROW_TILE = 512
ATT_TQ = 256
ATT_TK = 256
HG_TOKENS = 512
VMEM_LIMIT = 56 * 1024 * 1024

SEL_PENALTY = 2.0 ** 50
ALIBI_SLOPES = tuple(2.0 ** (-8.0 * (i + 1) / NSA_HEADS) for i in range(NSA_HEADS))

SEGMENTS = (
    ("q", NSA_HEADS * LANES, BF16),
    ("ks", NSA_KV_GROUPS * LANES, BF16),
    ("vs", NSA_KV_GROUPS * LANES, BF16),
    ("kw", NSA_KV_GROUPS * LANES, BF16),
    ("vw", NSA_KV_GROUPS * LANES, BF16),
    ("kc", KV_W, F32),
    ("vc", KV_W, F32),
    ("gl", LANES, F32),
    ("hg", 4 * HG_WIDTH, F32),
)
SEG_OFFSETS = tuple(int(v) for v in np.cumsum([0] + [s[1] for s in SEGMENTS]))
IN_COLS_EXT = SEG_OFFSETS[-1]


def _nn(a, b):
    return jnp.dot(a, b, preferred_element_type=F32)


def _nt(a, b):
    return lax.dot_general(a, b, (((1,), (1,)), ((), ())), preferred_element_type=F32)


def _tn(a, b):
    return lax.dot_general(a, b, (((0,), (0,)), ((), ())), preferred_element_type=F32)


def _split2(x):
    hi = x.astype(BF16)
    lo = (x - hi.astype(F32)).astype(BF16)
    return hi, lo


def _split3(x):
    hi = x.astype(BF16)
    r = x - hi.astype(F32)
    mid = r.astype(BF16)
    lo = (r - mid.astype(F32)).astype(BF16)
    return hi, mid, lo


def _dot3(a, b):
    ah, al = _split2(a)
    bh, bl = _split2(b)
    return _nn(ah, bh) + _nn(ah, bl) + _nn(al, bh)


def _sigmoid(x):
    return 1.0 / (1.0 + jnp.exp(-x))


def _rms(x, g):
    return x * lax.rsqrt(jnp.mean(x * x, axis=-1, keepdims=True) + EPS) * g


def _resident(shape):
    nd = len(shape)
    return pl.BlockSpec(shape, lambda *_: (0,) * nd, pipeline_mode=pl.Buffered(1))


def _params(sem):
    return pltpu.CompilerParams(dimension_semantics=sem, vmem_limit_bytes=VMEM_LIMIT)


def _ffn_kernel(*refs, final, proj):
    if proj:
        x_ref, a_ref, b_ref, wo_ref, g_ref, wg_ref, wu_ref, wd_ref, gf_ref, o_ref = refs
        x = x_ref[...] + _nn(a_ref[...], wo_ref[:NSA_WIDTH, :]) + _nn(b_ref[...], wo_ref[NSA_WIDTH:, :])
    else:
        x_ref, g_ref, wg_ref, wu_ref, wd_ref, gf_ref, o_ref = refs
        x = x_ref[...]
    xn = _rms(x, g_ref[...]).astype(BF16)
    acc = jnp.zeros(x.shape, F32)
    for k in range(D_FF_PAD // FF_CHUNK):
        sl = slice(k * FF_CHUNK, (k + 1) * FF_CHUNK)
        gate = _nn(xn, wg_ref[:, sl])
        up = _nn(xn, wu_ref[:, sl])
        h = (gate * _sigmoid(gate) * up).astype(BF16)
        acc = acc + _nn(h, wd_ref[sl, :])
    y = x + 0.5 * acc
    if final:
        y = _rms(y, gf_ref[...])
    o_ref[...] = y


def _ffn_weights(w_gu, w_down):
    pad = D_FF_PAD - D_FF
    w_gu = w_gu.astype(BF16)
    wg = jnp.pad(w_gu[..., :D_FF], ((0, 0), (0, 0), (0, pad)))
    wu = jnp.pad(w_gu[..., D_FF:], ((0, 0), (0, 0), (0, pad)))
    wd = jnp.pad(w_down.astype(BF16), ((0, 0), (0, pad), (0, 0)))
    return wg, wu, wd


def _ffn(h, norm_g, wg, wu, wd, final_g, final, proj=None):
    t = h.shape[0]
    row = lambda w: pl.BlockSpec((ROW_TILE, w), lambda i: (i, 0))
    weights = [_resident((1, D_MODEL)), _resident(wg.shape), _resident(wu.shape), _resident(wd.shape),
               _resident((1, D_MODEL))]
    operands = [norm_g.reshape(1, -1), wg, wu, wd, final_g.reshape(1, -1)]
    if proj is None:
        in_specs, args = [row(D_MODEL)] + weights, [h] + operands
    else:
        o_nsa, o_hg, w_out = proj
        in_specs = [row(D_MODEL), row(NSA_WIDTH), row(HG_WIDTH), _resident(w_out.shape)] + weights
        args = [h, o_nsa, o_hg, w_out] + operands
    return pl.pallas_call(
        functools.partial(_ffn_kernel, final=final, proj=proj is not None),
        grid=(t // ROW_TILE,),
        in_specs=in_specs,
        out_specs=row(D_MODEL),
        out_shape=jax.ShapeDtypeStruct((t, D_MODEL), F32),
        compiler_params=_params(("parallel",)),
        name="ffn",
    )(*args)


def _inproj_kernel(x_ref, g_ref, w_ref, qfeat_ref, *o_refs, seq):
    xn = _rms(x_ref[...], g_ref[...]).astype(BF16)
    rows = x_ref.shape[0]
    kpos = (pl.program_id(0) % (seq // rows)) * rows + lax.broadcasted_iota(jnp.int32, (rows, LANES), 0)
    lane = lax.broadcasted_iota(jnp.int32, (rows, LANES), 1)
    blk = kpos // SEL_BLOCK
    digit = lambda f0: jnp.where(lane == f0, blk.astype(F32),
                                 jnp.where(lane == f0 + 1, (kpos % SEL_BLOCK).astype(F32), 0.0))
    halves = [NSA_HD * (1 - g) for g in range(NSA_KV_GROUPS)]
    kfeat = jnp.concatenate([digit(f0) for f0 in halves], axis=1)
    is_pen = lambda f0: (lane == f0 + 1 + blk) & (blk >= 1) & (blk <= seq // SEL_BLOCK - 2)
    kpen = jnp.concatenate([jnp.where(is_pen(f0), -SEL_PENALTY, 0.0) for f0 in halves], axis=1)
    ones_half = jnp.where(lane >= NSA_HD, 1.0, 0.0)
    vfeat = jnp.concatenate([ones_half] * NSA_KV_GROUPS, axis=1)
    for o_ref, (name, width, dtype), start in zip(o_refs, SEGMENTS, SEG_OFFSETS[:-1], strict=True):
        y = _nn(xn, w_ref[:, start:start + width])
        if name == "q":
            y = y + qfeat_ref[...]
        elif name == "ks":
            y = y + (kfeat + kpen)
        elif name == "kw":
            y = y + kfeat
        elif name in ("vs", "vw"):
            y = y + vfeat
        o_ref[...] = y.astype(dtype)


def _q_features():
    feat = np.zeros((1, NSA_HEADS * LANES), np.float32)
    for h in range(NSA_HEADS):
        f0 = h * LANES + NSA_HD * (1 - h // HEADS_PER_GROUP)
        feat[0, f0] = SEL_BLOCK * ALIBI_SLOPES[h]
        feat[0, f0 + 1] = ALIBI_SLOPES[h]
    return jnp.asarray(feat)


def _build_w_in(w_in):
    sizes = (NSA_WIDTH, KV_W, KV_W, KV_W, KV_W, KV_W, KV_W, NSA_HEADS * 3,
             HG_WIDTH, HG_WIDTH, HG_WIDTH, HG_WIDTH)
    splits = [int(v) for v in np.cumsum(sizes)[:-1]]
    w_in = w_in.astype(BF16)
    wq, wkc, wvc, wks, wvs, wkw, wvw, wgl, whq, whf, whi, whg = jnp.split(w_in, splits, axis=1)
    d = w_in.shape[0]
    scale = NSA_HD ** -0.5
    zero = jnp.zeros((d, NSA_HD), w_in.dtype)
    q_cols = []
    for h in range(NSA_HEADS):
        w = wq[:, h * NSA_HD:(h + 1) * NSA_HD] * scale
        q_cols += [w, zero] if h // HEADS_PER_GROUP == 0 else [zero, w]
    g0, g1 = (lambda w: w[:, :NSA_HD]), (lambda w: w[:, NSA_HD:])
    k_blocks = lambda w: [g0(w), zero, zero, g1(w)]
    v_blocks = lambda w: [g0(w), zero, g1(w), zero]
    gl = wgl.reshape(d, NSA_HEADS, 3).transpose(0, 2, 1).reshape(d, 3 * NSA_HEADS)
    gl = jnp.pad(gl, ((0, 0), (0, LANES - 3 * NSA_HEADS)))
    cols = (q_cols + k_blocks(wks) + v_blocks(wvs) + k_blocks(wkw) + v_blocks(wvw)
            + [wkc, wvc, gl, whq, whf, whi, whg])
    return jnp.concatenate(cols, axis=1)


def _inproj(h, norm_g, w_ext, seq):
    t = h.shape[0]
    assert seq % ROW_TILE == 0
    row = lambda w: pl.BlockSpec((ROW_TILE, w), lambda i: (i, 0))
    return pl.pallas_call(
        functools.partial(_inproj_kernel, seq=seq),
        grid=(t // ROW_TILE,),
        in_specs=[row(D_MODEL), _resident((1, D_MODEL)), _resident(w_ext.shape),
                  _resident((1, NSA_HEADS * LANES))],
        out_specs=[row(w) for _, w, _ in SEGMENTS],
        out_shape=[jax.ShapeDtypeStruct((t, w), dt) for _, w, dt in SEGMENTS],
        compiler_params=_params(("parallel",)),
        name="inproj",
    )(h, norm_g.reshape(1, -1), w_ext, _q_features())


def _gelu_tanh(x):
    return 0.5 * x * (1.0 + jnp.tanh(0.7978845608028654 * (x + 0.044715 * (x * x * x))))


def _compress_kernel(kv_ref, pos_ref, w1_ref, w2_ref, o_ref):
    nbp = o_ref.shape[1]
    hid = NSA_KV_GROUPS * CMP_HID
    ha = jnp.zeros((nbp, hid), F32)
    hb = jnp.zeros((nbp, hid), F32)
    for l in range(CMP_STRIDE):
        x = kv_ref[0, pl.ds(l, nbp, stride=CMP_STRIDE), :]
        ha = ha + _dot3(x + pos_ref[l:l + 1, :], w1_ref[l])
        hb = hb + _dot3(x + pos_ref[CMP_STRIDE + l:CMP_STRIDE + l + 1, :], w1_ref[CMP_STRIDE + l])
    act = _gelu_tanh(ha + pltpu.roll(hb, nbp - 1, axis=0))
    out = jnp.zeros(o_ref.shape[1:], F32)
    for g in range(NSA_KV_GROUPS):
        out = out + _dot3(act[:, g * CMP_HID:(g + 1) * CMP_HID], w2_ref[g])
    o_ref[0] = out


def _compress(kv, pos, w1, w2, reps):
    b, s, _ = kv.shape
    nbp = s // CMP_STRIDE
    w1l = w1.reshape(CMP_LEN, NSA_HD, CMP_HID)
    z1 = jnp.zeros_like(w1l)
    w1p = jnp.concatenate([jnp.concatenate([w1l, z1], axis=2), jnp.concatenate([z1, w1l], axis=2)], axis=1)
    pos2 = jnp.concatenate([pos] * NSA_KV_GROUPS, axis=1)
    zero = jnp.zeros_like(w2)
    w2p = jnp.stack([jnp.concatenate([w2 if r == g else zero for r in reps], axis=1)
                     for g in range(NSA_KV_GROUPS)])
    width = len(reps) * NSA_HD
    return pl.pallas_call(
        _compress_kernel,
        grid=(b,),
        in_specs=[pl.BlockSpec((1, s, KV_W), lambda i: (i, 0, 0)),
                  _resident(pos2.shape), _resident(w1p.shape), _resident(w2p.shape)],
        out_specs=pl.BlockSpec((1, nbp, width), lambda i: (i, 0, 0)),
        out_shape=jax.ShapeDtypeStruct((b, nbp, width), F32),
        compiler_params=_params(("parallel",)),
        name="compress",
    )(kv, pos2, w1p, w2p)


def _pair_blocks(g, jj):
    lower = slice(0, LANES) if g == 0 else slice(LANES, 2 * LANES)
    upper = slice(LANES, 2 * LANES) if g == 0 else slice(0, LANES)
    return lower, upper


def _cmp_kernel(q_ref, kc_ref, vc2_ref, mt_ref, ocmp_ref, sel_ref, *, tq, ns, n_sel):
    nbp = kc_ref.shape[1]
    q0 = pl.program_id(1) * tq
    pos = q0 + lax.broadcasted_iota(jnp.int32, (tq, nbp), 0)
    blk_end = lax.broadcasted_iota(jnp.int32, (tq, nbp), 1) * CMP_STRIDE + (CMP_LEN - 1)
    valid = blk_end <= pos
    row_ok = (q0 + lax.broadcasted_iota(jnp.int32, (tq, 1), 0)) >= CMP_LEN - 1
    kc = kc_ref[0].astype(BF16)
    vc2 = vc2_ref[0].astype(BF16)
    lane_c = lax.broadcasted_iota(jnp.int32, (nbp, LANES), 1)
    c_idx = lax.broadcasted_iota(jnp.int32, (nbp, LANES), 0)
    per = SEL_BLOCK // CMP_STRIDE
    feat_a = (c_idx // per - q0 // SEL_BLOCK).astype(F32)
    feat_b = ((c_idx % per) * CMP_STRIDE + (CMP_LEN - 1)).astype(F32)
    lane_lo = lax.broadcasted_iota(jnp.int32, (tq, LANES), 1) < NSA_HD
    blk_f = lax.broadcasted_iota(jnp.int32, (ns, tq), 0).astype(F32)
    blk = lax.broadcasted_iota(jnp.int32, (ns, tq), 0)
    pos_t = q0 + lax.broadcasted_iota(jnp.int32, (ns, tq), 1)
    cur = pos_t // SEL_BLOCK
    forced = (blk == 0) | (blk == cur) | (blk == cur - 1)
    causal = blk * SEL_BLOCK <= pos_t
    sel_t = []
    for g in range(NSA_KV_GROUPS):
        f0 = NSA_HD * (1 - g)
        feat = jnp.where(lane_c == f0, feat_a, jnp.where(lane_c == f0 + 1, feat_b, 0.0))
        kc_g = jnp.where((lane_c // NSA_HD) == g, kc, feat.astype(BF16))
        imp = jnp.zeros((tq, nbp), F32)
        acc = []
        for hh in range(HEADS_PER_GROUP):
            h = g * HEADS_PER_GROUP + hh
            s = _nt(q_ref[0, :, h * LANES:(h + 1) * LANES], kc_g)
            s = jnp.where(valid, s, NEG)
            e = jnp.exp(s - jnp.max(s, axis=-1, keepdims=True))
            inv = jnp.where(row_ok, 1.0 / jnp.sum(e, axis=-1, keepdims=True), 0.0)
            p = e * inv
            imp = imp + p
            acc.append(_nn(p.astype(BF16), vc2))
        for jj in range(HEADS_PER_GROUP // 2):
            lower, upper = _pair_blocks(g, jj)
            blk_out = jnp.where(lane_lo, acc[2 * jj][:, lower], acc[2 * jj + 1][:, upper])
            c0 = (g * HEADS_PER_GROUP + 2 * jj) * NSA_HD
            ocmp_ref[0, :, c0:c0 + LANES] = blk_out
        p_slc = sum(_nt(mt_ref[...], part) for part in _split3(imp))
        taken = -jnp.inf
        score = jnp.where(forced, taken, jnp.where(causal, p_slc, NEG))
        for _ in range(n_sel - 3):
            top = jnp.max(score, axis=0, keepdims=True)
            first = jnp.min(jnp.where(score == top, blk_f, float(ns)), axis=0, keepdims=True)
            score = jnp.where(blk_f == first, taken, score)
        pen = jnp.where(score == taken, 0.0, SEL_PENALTY)
        pen = jnp.where((blk >= 2) & (blk <= ns - 1), pltpu.roll(pen, 1, axis=0), 0.0)
        if ns < NSA_HD:
            pen = jnp.concatenate([pen, jnp.zeros((NSA_HD - ns, tq), F32)], axis=0)
        zero = jnp.zeros((NSA_HD, tq), F32)
        sel_t += [zero, pen] if g == 0 else [pen, zero]
    sel_ref[0] = jnp.concatenate(sel_t, axis=0).T.astype(BF16)


def _importance_matrix(nbp, ns):
    per = SEL_BLOCK // CMP_STRIDE
    m = np.zeros((ns, nbp), np.float32)
    for n in range(ns):
        for c in range(per * n, per * (n + 1)):
            for cc in (c - 1, c):
                if 0 <= cc < nbp - 1:
                    m[n, cc] += 1.0
    return jnp.asarray(m, BF16)


def _cmp_attention(q, kc, vc2, b, s):
    nbp = s // CMP_STRIDE
    ns = s // SEL_BLOCK
    tq = ATT_TQ
    assert ns <= NSA_HD
    assert min(SEL_TOPK, ns) >= 3
    q3 = q.reshape(b, s, NSA_HEADS * LANES)
    kern = functools.partial(_cmp_kernel, tq=tq, ns=ns, n_sel=min(SEL_TOPK, ns))
    return pl.pallas_call(
        kern,
        grid=(b, s // tq),
        in_specs=[pl.BlockSpec((1, tq, NSA_HEADS * LANES), lambda i, j: (i, j, 0)),
                  pl.BlockSpec((1, nbp, KV_W), lambda i, j: (i, 0, 0)),
                  pl.BlockSpec((1, nbp, 2 * KV_W), lambda i, j: (i, 0, 0)),
                  _resident((ns, nbp))],
        out_specs=[pl.BlockSpec((1, tq, NSA_WIDTH), lambda i, j: (i, j, 0)),
                   pl.BlockSpec((1, tq, NSA_KV_GROUPS * LANES), lambda i, j: (i, j, 0))],
        out_shape=[jax.ShapeDtypeStruct((b, s, NSA_WIDTH), F32),
                   jax.ShapeDtypeStruct((b, s, NSA_KV_GROUPS * LANES), BF16)],
        compiler_params=_params(("parallel", "parallel")),
        name="cmp_topk",
    )(q3, kc, vc2, _importance_matrix(nbp, ns))


def _attn_kernel(q_ref, ks_ref, vs_ref, kw_ref, vw_ref, penq_ref, gl_ref, gexp_ref, ocmp_ref,
                 o_ref, qa_scr, s_scr, p_scr, al_scr, m_scr, acc_scr, *, tq, tk):
    i = pl.program_id(1)
    rows_of = lambda hh: slice(hh * tq, (hh + 1) * tq)
    lane_lo = lax.broadcasted_iota(jnp.int32, (tq, LANES), 1) < NSA_HD
    rel = (lax.broadcasted_iota(jnp.int32, (tq, tk), 1)
           - lax.broadcasted_iota(jnp.int32, (tq, tk), 0)).astype(F32)

    for hh in range(HEADS_PER_GROUP):
        qa_scr[rows_of(hh)] = q_ref[0, :, hh * LANES:(hh + 1) * LANES] + penq_ref[0]

    def scores(k_ref, j, slot):
        s_scr[slot] = _nt(qa_scr[...], k_ref[0, pl.ds(pl.multiple_of(j * tk, tk), tk), :])

    def probs(br, j, mode, slot, first, exists=None):
        for hh in range(HEADS_PER_GROUP):
            r = rows_of(hh)
            s = s_scr[slot, r]
            if mode == "causal":
                s = jnp.where(rel <= 0.0, s, NEG)
            elif mode == "lower":
                s = jnp.where(rel + (j * tk - i * tq).astype(F32) > -float(WINDOW), s, NEG)
            if exists is not None:
                s = jnp.where(exists, s, NEG)
            m_cur = jnp.max(s, axis=-1, keepdims=True)
            if first:
                m_new = jnp.broadcast_to(m_cur, (tq, LANES))
            else:
                m_prev = m_scr[br, r]
                m_new = jnp.maximum(m_prev, m_cur)
                al_scr[slot, r] = jnp.exp(m_prev - m_new)
            m_scr[br, r] = m_new
            p_scr[slot, r] = jnp.exp(s - jnp.concatenate([m_new] * (tk // LANES), axis=1)).astype(BF16)

    def accumulate(br, v_ref, j, slot, first):
        pv = _nn(p_scr[slot], v_ref[0, pl.ds(pl.multiple_of(j * tk, tk), tk), :])
        for hh in range(HEADS_PER_GROUP):
            r = rows_of(hh)
            if first:
                acc_scr[br, r] = pv[r]
            else:
                acc_scr[br, r] = al_scr[slot, r] * acc_scr[br, r] + pv[r]

    last = jnp.maximum(i - 1, 0)
    tile_of = lambda n: jnp.minimum(n - 1, last)
    scores(ks_ref, i, 0)
    scores(ks_ref, tile_of(1), 1)
    probs(0, i, "causal", 0, True)
    scores(ks_ref, tile_of(2), 0)
    probs(0, tile_of(1), None, 1, False)
    accumulate(0, vs_ref, i, 0, True)

    def sel_step(n, parity):
        scores(ks_ref, tile_of(n + 2), parity)
        probs(0, tile_of(n + 1), None, 1 - parity, False)
        accumulate(0, vs_ref, n - 1, parity, False)

    def sel_two_steps(t, carry):
        sel_step(2 * t + 1, 1)
        sel_step(2 * t + 2, 0)
        return carry

    lax.fori_loop(0, i // 2, sel_two_steps, 0)

    @pl.when(i % 2 == 1)
    def _():
        sel_step(i, 1)

    n_back = WINDOW // tk
    back = [(jnp.maximum(i - d, 0), "lower" if d == n_back else None, i >= d) for d in range(1, n_back + 1)]
    scores(kw_ref, i, 0)
    scores(kw_ref, back[0][0], 1)
    probs(1, i, "causal", 0, True)
    accumulate(1, vw_ref, i, 0, True)
    for d, (j, mode, exists) in enumerate(back, start=1):
        if d < n_back:
            scores(kw_ref, back[d][0], (d + 1) % 2)
        probs(1, j, mode, d % 2, False, exists)
        accumulate(1, vw_ref, j, d % 2, False)

    gexp = sum(_nn(part, gexp_ref[0]) for part in _split2(_sigmoid(gl_ref[0])))
    n_pairs = HEADS_PER_GROUP // 2
    gate_blk = lambda br, jj: gexp[:, (br * n_pairs + jj) * LANES:(br * n_pairs + jj + 1) * LANES]
    for jj in range(n_pairs):
        cols = slice(jj * LANES, (jj + 1) * LANES)
        blk = gate_blk(0, jj) * ocmp_ref[0, :, cols]
        for br in range(2):
            even = acc_scr[br, rows_of(2 * jj)]
            odd = acc_scr[br, rows_of(2 * jj + 1)]
            low = even / pltpu.roll(even, NSA_HD, axis=1)
            up = pltpu.roll(odd, NSA_HD, axis=1) / odd
            blk = blk + gate_blk(br + 1, jj) * jnp.where(lane_lo, low, up)
        o_ref[0, :, cols] = blk.astype(o_ref.dtype)


def _gate_expansion():
    n_pairs = HEADS_PER_GROUP // 2
    r = np.zeros((NSA_KV_GROUPS, LANES, 3 * n_pairs * LANES), np.float32)
    for g in range(NSA_KV_GROUPS):
        for br in range(3):
            for jj in range(n_pairs):
                for odd in range(2):
                    src = br * NSA_HEADS + g * HEADS_PER_GROUP + 2 * jj + odd
                    dst = (br * n_pairs + jj) * LANES + odd * NSA_HD
                    r[g, src, dst:dst + NSA_HD] = 1.0
    return jnp.asarray(r, BF16)


def _attention(q, ks, vs, kw, vw, pen, gl, ocmp, b, s):
    tq, tk = ATT_TQ, ATT_TK
    assert tq == tk and WINDOW % tk == 0 and WINDOW // tk <= 2 and tq % SEL_BLOCK == 0
    gw = HEADS_PER_GROUP * LANES
    ow = HEADS_PER_GROUP * NSA_HD
    r3 = lambda a: a.reshape(b, s, a.shape[-1])
    tile = lambda w: pl.BlockSpec((1, tq, w), lambda i, j, g: (i, j, 0))
    gtile = lambda w: pl.BlockSpec((1, tq, w), lambda i, j, g: (i, j, g))
    gfull = pl.BlockSpec((1, s, LANES), lambda i, j, g: (i, 0, g))
    gexp = _gate_expansion()
    rows = HEADS_PER_GROUP * tq
    kern = functools.partial(_attn_kernel, tq=tq, tk=tk)
    return pl.pallas_call(
        kern,
        grid=(b, s // tq, NSA_KV_GROUPS),
        in_specs=[gtile(gw), gfull, gfull, gfull, gfull, gtile(LANES), tile(LANES),
                  pl.BlockSpec((1,) + gexp.shape[1:], lambda i, j, g: (g, 0, 0)), gtile(ow)],
        out_specs=gtile(ow),
        out_shape=jax.ShapeDtypeStruct((b, s, NSA_WIDTH), BF16),
        scratch_shapes=[pltpu.VMEM((rows, LANES), BF16), pltpu.VMEM((2, rows, tk), F32),
                        pltpu.VMEM((2, rows, tk), BF16), pltpu.VMEM((2, rows, LANES), F32),
                        pltpu.VMEM((2, rows, LANES), F32), pltpu.VMEM((2, rows, LANES), F32)],
        compiler_params=_params(("parallel", "parallel", "parallel")),
        name="sel_win_attention",
    )(r3(q), r3(ks), r3(vs), r3(kw), r3(vw), pen, r3(gl), gexp, ocmp)


def _hgrn_kernel(hq_ref, hf_ref, hi_ref, hg_ref, lbraw_ref, onorm_ref, o_ref, st_scr, b_scr,
                 *, layer, n_chunks):
    c = HG_CHUNK

    @pl.when(pl.program_id(1) == 0)
    def _():
        st_scr[...] = jnp.zeros(st_scr.shape, F32)

    raw = lbraw_ref[...]
    ex = jnp.exp(raw - jnp.max(raw, axis=0, keepdims=True))
    sm = ex / jnp.sum(ex, axis=0, keepdims=True)
    lb_all = jnp.zeros((1, raw.shape[1]), F32)
    for l in range(1, layer + 1):
        lb_all = lb_all + sm[l:l + 1, :]

    t_idx = lax.broadcasted_iota(jnp.int32, (c, HG_DK), 0)
    sub = lax.broadcasted_iota(jnp.int32, (8, HG_DK), 0)
    ti = lax.broadcasted_iota(jnp.int32, (c, c), 0)
    si = lax.broadcasted_iota(jnp.int32, (c, c), 1)
    tril = jnp.where(si <= ti, 1.0, 0.0).astype(BF16)
    onorm = onorm_ref[...]
    levels = (32, 16, 8, 4, 2, 1)
    split_bit = ti ^ si
    pair_mask = {m: jnp.where((ti > si) & (split_bit >= m) & (split_bit < 2 * m), 1.0, 0.0) for m in levels}
    second_half = {m: (t_idx & m) != 0 for m in levels}
    sign = {m: jnp.where(second_half[m], 1.0, -1.0) for m in levels}

    def ref_rows(h, m):
        if m >= 4:
            return jnp.concatenate(
                [jnp.broadcast_to(b_scr[h, pl.ds(s0 + m - 1, 1), :], (2 * m, HG_DK))
                 for s0 in range(0, c, 2 * m)], axis=0)
        pieces = []
        for s0 in range(0, c, 8):
            r1 = jnp.broadcast_to(b_scr[h, pl.ds(s0 + 1, 1), :], (8, HG_DK))
            r5 = jnp.broadcast_to(b_scr[h, pl.ds(s0 + 5, 1), :], (8, HG_DK))
            pieces.append(jnp.where(sub < 4, r1, r5))
        return jnp.concatenate(pieces, axis=0)

    def chunk(ci, carry):
        rows = pl.ds(pl.multiple_of(ci * c, c), c)
        for h in range(HG_HEADS):
            lanes = slice(h * HG_DK, (h + 1) * HG_DK)
            lb = lb_all[:, lanes]
            z = hf_ref[rows, lanes]
            hq = hq_ref[rows, lanes]
            q = hq * _sigmoid(hq)
            ez = jnp.exp(-jnp.abs(z))
            big = 1.0 / (1.0 + ez)
            small = ez * big
            f = jnp.maximum(lb + (1.0 - lb) * jnp.where(z >= 0.0, big, small), F_FLOOR)
            logf = jnp.log(f)
            k = (1.0 - lb) * jnp.where(z >= 0.0, small, big)
            v = hi_ref[rows, lanes].astype(BF16)
            bcum = sum(_nn(tril, part) for part in _split3(logf))
            b_scr[h] = bcum

            a = jnp.where(ti == si, jnp.sum(q * k, axis=-1, keepdims=True), 0.0)
            for m in levels:
                if m == 1:
                    r = jnp.where(second_half[m], q * f, k)
                else:
                    w = jnp.exp((bcum - ref_rows(h, m)) * sign[m])
                    r = jnp.where(second_half[m], q, k) * w
                r = r.astype(BF16)
                a = a + _nt(r, r) * pair_mask[m]

            st = st_scr[h]
            o = _nn(a.astype(BF16), v) + _nt((q * jnp.exp(bcum)).astype(BF16), st.astype(BF16))
            b_last = b_scr[h, pl.ds(c - 1, 1), :]
            kd = (k * jnp.exp(b_last - bcum)).astype(BF16)
            st_scr[h] = jnp.exp(b_last) * st + _tn(v, kd)

            o = o * lax.rsqrt(jnp.mean(o * o, axis=-1, keepdims=True) + EPS) * onorm
            gate = hg_ref[rows, lanes]
            o_ref[rows, lanes] = (o * (gate * _sigmoid(gate))).astype(o_ref.dtype)
        return carry

    lax.fori_loop(0, n_chunks, chunk, 0, unroll=4)


def _hgrn(hg4, lb_raw, out_norm, layer, b, s):
    tt = HG_TOKENS
    per = s // tt
    col = lambda k: pl.BlockSpec((tt, HG_WIDTH), lambda i, j, k=k: (i * per + j, k))
    kern = functools.partial(_hgrn_kernel, layer=layer, n_chunks=tt // HG_CHUNK)
    return pl.pallas_call(
        kern,
        grid=(b, per),
        in_specs=[col(0), col(1), col(2), col(3), _resident(lb_raw.shape), _resident((1, HG_DV))],
        out_specs=pl.BlockSpec((tt, HG_WIDTH), lambda i, j: (i * per + j, 0)),
        out_shape=jax.ShapeDtypeStruct((b * s, HG_WIDTH), BF16),
        scratch_shapes=[pltpu.VMEM((HG_HEADS, HG_DV, HG_DK), F32),
                        pltpu.VMEM((HG_HEADS, HG_CHUNK, HG_DK), F32)],
        compiler_params=_params(("parallel", "arbitrary")),
        name="hgrn2",
    )(hg4, hg4, hg4, hg4, lb_raw, out_norm.reshape(1, -1))


def _mixer(h, l, b, s, mix_norm, w_in, cmp_pos_k, cmp_pos_v, cmp_k_w1, cmp_k_w2, cmp_v_w1, cmp_v_w2,
           hgrn_lower_bound, hgrn_out_norm):
    q, ks, vs, kw, vw, kc_in, vc_in, gl, hg4 = _inproj(h, mix_norm[l], _build_w_in(w_in[l]), s)
    kc = _compress(kc_in.reshape(b, s, KV_W), cmp_pos_k[l], cmp_k_w1[l], cmp_k_w2[l], (0, 1))
    vc2 = _compress(vc_in.reshape(b, s, KV_W), cmp_pos_v[l], cmp_v_w1[l], cmp_v_w2[l], (0, 1, 1, 0))
    ocmp, sel = _cmp_attention(q, kc, vc2, b, s)
    o_nsa = _attention(q, ks, vs, kw, vw, sel, gl, ocmp, b, s)
    o_hg = _hgrn(hg4, hgrn_lower_bound, hgrn_out_norm[l], l, b, s)
    return o_nsa.reshape(b * s, NSA_WIDTH), o_hg


def kernel(x, ffn1_norm, ffn1_w_gu, ffn1_w_down, mix_norm, w_in, cmp_pos_k, cmp_pos_v, cmp_k_w1, cmp_k_w2, cmp_v_w1, cmp_v_w2, hgrn_lower_bound, hgrn_out_norm, w_out, ffn2_norm, ffn2_w_gu, ffn2_w_down, final_norm):
    b, s, d = x.shape
    depth = ffn1_norm.shape[0]
    ffn1_w = _ffn_weights(ffn1_w_gu, ffn1_w_down)
    ffn2_w = _ffn_weights(ffn2_w_gu, ffn2_w_down)
    w_out = w_out.astype(BF16)
    h = x.reshape(b * s, d)
    for l in range(depth):
        h = _ffn(h, ffn1_norm[l], *(w[l] for w in ffn1_w), final_norm, False)
        o_nsa, o_hg = _mixer(h, l, b, s, mix_norm, w_in, cmp_pos_k, cmp_pos_v, cmp_k_w1, cmp_k_w2, cmp_v_w1,
                             cmp_v_w2, hgrn_lower_bound, hgrn_out_norm)
        h = _ffn(h, ffn2_norm[l], *(w[l] for w in ffn2_w), final_norm, l == depth - 1, (o_nsa, o_hg, w_out[l]))
    return h.reshape(b, s, d)
```

```python
import functools

import jax
import jax.numpy as jnp
import numpy as np
from jax import lax
from jax.experimental import pallas as pl
from jax.experimental.pallas import tpu as pltpu

F32 = jnp.float32
BF16 = jnp.bfloat16

D_MODEL = 1024
EPS = 1e-6
NEG = -1e30
FORCE = 1e6
F_FLOOR = 1e-30
NSA_HEADS = 8
NSA_KV_GROUPS = 2
HEADS_PER_GROUP = NSA_HEADS // NSA_KV_GROUPS
NSA_HD = 64
CMP_LEN = 32
CMP_STRIDE = 16
CMP_HID = 256
SEL_BLOCK = 64
SEL_TOPK = 16
WINDOW = 512
HG_HEADS = 4
HG_DK = 128
HG_DV = 128
HG_CHUNK = 64
D_FF = 2752
NSA_WIDTH = NSA_HEADS * NSA_HD
HG_WIDTH = HG_HEADS * HG_DV
KV_W = NSA_KV_GROUPS * NSA_HD

LANES = 128
D_FF_PAD = 2816
FF_CHUNK = 256
ROW_TILE = 512
ATT_TQ = 256
ATT_TK = 256
HG_TOKENS = 512
VMEM_LIMIT = 56 * 1024 * 1024

SEL_PENALTY = 2.0 ** 50
ALIBI_SLOPES = tuple(2.0 ** (-8.0 * (i + 1) / NSA_HEADS) for i in range(NSA_HEADS))

SEGMENTS = (
    ("q", NSA_HEADS * LANES, BF16, NSA_WIDTH),
    ("ks", NSA_KV_GROUPS * LANES, BF16, KV_W),
    ("vs", NSA_KV_GROUPS * LANES, BF16, KV_W),
    ("kw", NSA_KV_GROUPS * LANES, BF16, KV_W),
    ("vw", NSA_KV_GROUPS * LANES, BF16, KV_W),
    ("kc", KV_W, F32, KV_W),
    ("vc", KV_W, F32, KV_W),
    ("gl", LANES, F32, LANES),
    ("hg", 4 * HG_WIDTH, F32, 4 * HG_WIDTH),
)
SEG_OFFSETS = tuple(int(v) for v in np.cumsum([0] + [s[3] for s in SEGMENTS]))


def _nn(a, b):
    return jnp.dot(a, b, preferred_element_type=F32)


def _nt(a, b):
    return lax.dot_general(a, b, (((1,), (1,)), ((), ())), preferred_element_type=F32)


def _tn(a, b):
    return lax.dot_general(a, b, (((0,), (0,)), ((), ())), preferred_element_type=F32)


def _split2(x):
    hi = x.astype(BF16)
    lo = (x - hi.astype(F32)).astype(BF16)
    return hi, lo


def _split3(x):
    hi = x.astype(BF16)
    r = x - hi.astype(F32)
    mid = r.astype(BF16)
    lo = (r - mid.astype(F32)).astype(BF16)
    return hi, mid, lo


def _dot3(a, b):
    ah, al = _split2(a)
    bh, bl = _split2(b)
    return _nn(ah, bh) + _nn(ah, bl) + _nn(al, bh)


def _sigmoid(x):
    return 1.0 / (1.0 + jnp.exp(-x))


def _rms(x, g):
    return x * lax.rsqrt(jnp.mean(x * x, axis=-1, keepdims=True) + EPS) * g


def _resident(shape):
    nd = len(shape)
    return pl.BlockSpec(shape, lambda *_: (0,) * nd, pipeline_mode=pl.Buffered(1))


def _params(sem):
    return pltpu.CompilerParams(dimension_semantics=sem, vmem_limit_bytes=VMEM_LIMIT)


def _ffn_kernel(*refs, final, proj):
    if proj:
        x_ref, a_ref, b_ref, wo_ref, g_ref, wg_ref, wu_ref, wd_ref, gf_ref, o_ref = refs
        x = x_ref[...] + _nn(a_ref[...], wo_ref[:NSA_WIDTH, :]) + _nn(b_ref[...], wo_ref[NSA_WIDTH:, :])
    else:
        x_ref, g_ref, wg_ref, wu_ref, wd_ref, gf_ref, o_ref = refs
        x = x_ref[...]
    xn = _rms(x, g_ref[...]).astype(BF16)
    acc = jnp.zeros(x.shape, F32)
    for k in range(D_FF_PAD // FF_CHUNK):
        sl = slice(k * FF_CHUNK, (k + 1) * FF_CHUNK)
        gate = _nn(xn, wg_ref[:, sl])
        up = _nn(xn, wu_ref[:, sl])
        h = (gate * _sigmoid(gate) * up).astype(BF16)
        acc = acc + _nn(h, wd_ref[sl, :])
    y = x + 0.5 * acc
    if final:
        y = _rms(y, gf_ref[...])
    o_ref[...] = y


def _ffn_weights(w_gu, w_down):
    pad = D_FF_PAD - D_FF
    w_gu = w_gu.astype(BF16)
    wg = jnp.pad(w_gu[..., :D_FF], ((0, 0), (0, 0), (0, pad)))
    wu = jnp.pad(w_gu[..., D_FF:], ((0, 0), (0, 0), (0, pad)))
    wd = jnp.pad(w_down.astype(BF16), ((0, 0), (0, pad), (0, 0)))
    return wg, wu, wd


def _ffn(h, norm_g, wg, wu, wd, final_g, final, proj=None):
    t = h.shape[0]
    row = lambda w: pl.BlockSpec((ROW_TILE, w), lambda i: (i, 0))
    weights = [_resident((1, D_MODEL)), _resident(wg.shape), _resident(wu.shape), _resident(wd.shape),
               _resident((1, D_MODEL))]
    operands = [norm_g.reshape(1, -1), wg, wu, wd, final_g.reshape(1, -1)]
    if proj is None:
        in_specs, args = [row(D_MODEL)] + weights, [h] + operands
    else:
        o_nsa, o_hg, w_out = proj
        in_specs = [row(D_MODEL), row(NSA_WIDTH), row(HG_WIDTH), _resident(w_out.shape)] + weights
        args = [h, o_nsa, o_hg, w_out] + operands
    return pl.pallas_call(
        functools.partial(_ffn_kernel, final=final, proj=proj is not None),
        grid=(t // ROW_TILE,),
        in_specs=in_specs,
        out_specs=row(D_MODEL),
        out_shape=jax.ShapeDtypeStruct((t, D_MODEL), F32),
        compiler_params=_params(("parallel",)),
        name="ffn",
    )(*args)


def _inproj_kernel(x_ref, g_ref, w_ref, qfeat_ref, *o_refs, seq):
    xn = _rms(x_ref[...], g_ref[...]).astype(BF16)
    rows = x_ref.shape[0]
    kpos = (pl.program_id(0) % (seq // rows)) * rows + lax.broadcasted_iota(jnp.int32, (rows, LANES), 0)
    lane = lax.broadcasted_iota(jnp.int32, (rows, LANES), 1)
    blk = kpos // SEL_BLOCK
    digit = lambda f0: jnp.where(lane == f0, blk.astype(F32),
                                 jnp.where(lane == f0 + 1, (kpos % SEL_BLOCK).astype(F32), 0.0))
    halves = [NSA_HD * (1 - g) for g in range(NSA_KV_GROUPS)]
    kfeat = jnp.concatenate([digit(f0) for f0 in halves], axis=1)
    is_pen = lambda f0: (lane == f0 + 1 + blk) & (blk >= 1) & (blk <= seq // SEL_BLOCK - 2)
    kpen = jnp.concatenate([jnp.where(is_pen(f0), -SEL_PENALTY, 0.0) for f0 in halves], axis=1)
    lower = lane < NSA_HD
    swap = lambda a: pltpu.roll(a, NSA_HD, axis=1)
    small = [i for i, seg in enumerate(SEGMENTS) if seg[3] == LANES]
    assert small == list(range(small[0], small[-1] + 1))
    y_small = _nn(xn, w_ref[:, SEG_OFFSETS[small[0]]:SEG_OFFSETS[small[-1] + 1]])
    for i, (o_ref, (name, _, dtype, cols)) in enumerate(zip(o_refs, SEGMENTS, strict=True)):
        if i in small:
            y = y_small[:, (i - small[0]) * LANES:(i - small[0] + 1) * LANES]
        else:
            y = _nn(xn, w_ref[:, SEG_OFFSETS[i]:SEG_OFFSETS[i] + cols])
        if name == "q":
            blocks = []
            for h in range(NSA_HEADS):
                pair = y[:, (h // 2) * LANES:(h // 2 + 1) * LANES]
                g = h // HEADS_PER_GROUP
                data = pair if h % 2 == g else swap(pair)
                feat = qfeat_ref[:, h * LANES:(h + 1) * LANES]
                blocks.append(jnp.where(lower == (g == 0), data, feat))
            y = jnp.concatenate(blocks, axis=1)
        elif name in ("ks", "kw"):
            feat = kfeat + kpen if name == "ks" else kfeat
            y = jnp.concatenate([jnp.where(lower == (g == 0), y, feat[:, g * LANES:(g + 1) * LANES])
                                 for g in range(NSA_KV_GROUPS)], axis=1)
        elif name in ("vs", "vw"):
            y = jnp.concatenate([jnp.where(lower, y if g == 0 else swap(y), 1.0)
                                 for g in range(NSA_KV_GROUPS)], axis=1)
        o_ref[...] = y.astype(dtype)


def _q_features():
    feat = np.zeros((1, NSA_HEADS * LANES), np.float32)
    for h in range(NSA_HEADS):
        f0 = h * LANES + NSA_HD * (1 - h // HEADS_PER_GROUP)
        feat[0, f0] = SEL_BLOCK * ALIBI_SLOPES[h]
        feat[0, f0 + 1] = ALIBI_SLOPES[h]
    return jnp.asarray(feat)


def _build_w_in(w_in):
    sizes = (NSA_WIDTH, KV_W, KV_W, KV_W, KV_W, KV_W, KV_W, NSA_HEADS * 3,
             HG_WIDTH, HG_WIDTH, HG_WIDTH, HG_WIDTH)
    splits = [int(v) for v in np.cumsum(sizes)[:-1]]
    w_in = w_in.astype(BF16)
    wq, wkc, wvc, wks, wvs, wkw, wvw, wgl, whq, whf, whi, whg = jnp.split(w_in, splits, axis=1)
    d = w_in.shape[0]
    gl = wgl.reshape(d, NSA_HEADS, 3).transpose(0, 2, 1).reshape(d, 3 * NSA_HEADS)
    gl = jnp.pad(gl, ((0, 0), (0, LANES - 3 * NSA_HEADS)))
    cols = [wq * NSA_HD ** -0.5, wks, wvs, wkw, wvw, wkc, wvc, gl, whq, whf, whi, whg]
    return jnp.concatenate(cols, axis=1)


def _inproj(h, norm_g, w_ext, seq):
    t = h.shape[0]
    assert seq % ROW_TILE == 0
    row = lambda w: pl.BlockSpec((ROW_TILE, w), lambda i: (i, 0))
    return pl.pallas_call(
        functools.partial(_inproj_kernel, seq=seq),
        grid=(t // ROW_TILE,),
        in_specs=[row(D_MODEL), _resident((1, D_MODEL)), _resident(w_ext.shape),
                  _resident((1, NSA_HEADS * LANES))],
        out_specs=[row(w) for _, w, _, _ in SEGMENTS],
        out_shape=[jax.ShapeDtypeStruct((t, w), dt) for _, w, dt, _ in SEGMENTS],
        compiler_params=_params(("parallel",)),
        name="inproj",
    )(h, norm_g.reshape(1, -1), w_ext, _q_features())


def _gelu_tanh(x):
    return 0.5 * x * (1.0 + jnp.tanh(0.7978845608028654 * (x + 0.044715 * (x * x * x))))


def _compress_kernel(kv_ref, pos_ref, w1_ref, w2_ref, o_ref):
    nbp = o_ref.shape[1]
    hid = NSA_KV_GROUPS * CMP_HID
    ha = jnp.zeros((nbp, hid), F32)
    hb = jnp.zeros((nbp, hid), F32)
    for l in range(CMP_STRIDE):
        x = kv_ref[0, pl.ds(l, nbp, stride=CMP_STRIDE), :]
        ha = ha + _nn((x + pos_ref[l:l + 1, :]).astype(BF16), w1_ref[l])
        hb = hb + _nn((x + pos_ref[CMP_STRIDE + l:CMP_STRIDE + l + 1, :]).astype(BF16), w1_ref[CMP_STRIDE + l])
    act = _gelu_tanh(ha + pltpu.roll(hb, nbp - 1, axis=0)).astype(BF16)
    out = jnp.zeros(o_ref.shape[1:], F32)
    for g in range(NSA_KV_GROUPS):
        out = out + _nn(act[:, g * CMP_HID:(g + 1) * CMP_HID], w2_ref[g])
    o_ref[0] = out


def _compress(kv, pos, w1, w2, reps):
    b, s, _ = kv.shape
    nbp = s // CMP_STRIDE
    w1l = w1.astype(BF16).reshape(CMP_LEN, NSA_HD, CMP_HID)
    z1 = jnp.zeros_like(w1l)
    w1p = jnp.concatenate([jnp.concatenate([w1l, z1], axis=2), jnp.concatenate([z1, w1l], axis=2)], axis=1)
    pos2 = jnp.concatenate([pos] * NSA_KV_GROUPS, axis=1)
    w2 = w2.astype(BF16)
    zero = jnp.zeros_like(w2)
    w2p = jnp.stack([jnp.concatenate([w2 if r == g else zero for r in reps], axis=1)
                     for g in range(NSA_KV_GROUPS)])
    width = len(reps) * NSA_HD
    return pl.pallas_call(
        _compress_kernel,
        grid=(b,),
        in_specs=[pl.BlockSpec((1, s, KV_W), lambda i: (i, 0, 0)),
                  _resident(pos2.shape), _resident(w1p.shape), _resident(w2p.shape)],
        out_specs=pl.BlockSpec((1, nbp, width), lambda i: (i, 0, 0)),
        out_shape=jax.ShapeDtypeStruct((b, nbp, width), F32),
        compiler_params=_params(("parallel",)),
        name="compress",
    )(kv, pos2, w1p, w2p)


def _pair_blocks(g, jj):
    lower = slice(0, LANES) if g == 0 else slice(LANES, 2 * LANES)
    upper = slice(LANES, 2 * LANES) if g == 0 else slice(0, LANES)
    return lower, upper


def _cmp_kernel(q_ref, kc_ref, vc2_ref, mt_ref, ocmp_ref, sel_ref, *, tq, ns, n_sel):
    nbp = kc_ref.shape[1]
    q0 = pl.program_id(1) * tq
    pos = q0 + lax.broadcasted_iota(jnp.int32, (tq, nbp), 0)
    blk_end = lax.broadcasted_iota(jnp.int32, (tq, nbp), 1) * CMP_STRIDE + (CMP_LEN - 1)
    valid = blk_end <= pos
    row_ok = (q0 + lax.broadcasted_iota(jnp.int32, (tq, 1), 0)) >= CMP_LEN - 1
    kc = kc_ref[0].astype(BF16)
    vc2 = vc2_ref[0].astype(BF16)
    lane_c = lax.broadcasted_iota(jnp.int32, (nbp, LANES), 1)
    c_idx = lax.broadcasted_iota(jnp.int32, (nbp, LANES), 0)
    per = SEL_BLOCK // CMP_STRIDE
    feat_a = (c_idx // per - q0 // SEL_BLOCK).astype(F32)
    feat_b = ((c_idx % per) * CMP_STRIDE + (CMP_LEN - 1)).astype(F32)
    lane_lo = lax.broadcasted_iota(jnp.int32, (tq, LANES), 1) < NSA_HD
    blk_f = lax.broadcasted_iota(jnp.int32, (ns, tq), 0).astype(F32)
    blk = lax.broadcasted_iota(jnp.int32, (ns, tq), 0)
    pos_t = q0 + lax.broadcasted_iota(jnp.int32, (ns, tq), 1)
    cur = pos_t // SEL_BLOCK
    forced = (blk == 0) | (blk == cur) | (blk == cur - 1)
    causal = blk * SEL_BLOCK <= pos_t
    sel_t = []
    for g in range(NSA_KV_GROUPS):
        f0 = NSA_HD * (1 - g)
        feat = jnp.where(lane_c == f0, feat_a, jnp.where(lane_c == f0 + 1, feat_b, 0.0))
        kc_g = jnp.where((lane_c // NSA_HD) == g, kc, feat.astype(BF16))
        imp = jnp.zeros((tq, nbp), F32)
        acc = []
        for hh in range(HEADS_PER_GROUP):
            h = g * HEADS_PER_GROUP + hh
            s = _nt(q_ref[0, :, h * LANES:(h + 1) * LANES], kc_g)
            s = jnp.where(valid, s, NEG)
            e = jnp.exp(s - jnp.max(s, axis=-1, keepdims=True))
            inv = jnp.where(row_ok, 1.0 / jnp.sum(e, axis=-1, keepdims=True), 0.0)
            p = e * inv
            imp = imp + p
            acc.append(_nn(p.astype(BF16), vc2))
        for jj in range(HEADS_PER_GROUP // 2):
            lower, upper = _pair_blocks(g, jj)
            blk_out = jnp.where(lane_lo, acc[2 * jj][:, lower], acc[2 * jj + 1][:, upper])
            c0 = (g * HEADS_PER_GROUP + 2 * jj) * NSA_HD
            ocmp_ref[0, :, c0:c0 + LANES] = blk_out
        p_slc = sum(_nt(mt_ref[...], part) for part in _split3(imp))
        taken = -jnp.inf
        score = jnp.where(forced, taken, jnp.where(causal, p_slc, NEG))
        for _ in range(n_sel - 3):
            top = jnp.max(score, axis=0, keepdims=True)
            first = jnp.min(jnp.where(score == top, blk_f, float(ns)), axis=0, keepdims=True)
            score = jnp.where(blk_f == first, taken, score)
        pen = jnp.where(score == taken, 0.0, SEL_PENALTY)
        pen = jnp.where((blk >= 2) & (blk <= ns - 1), pltpu.roll(pen, 1, axis=0), 0.0)
        if ns < NSA_HD:
            pen = jnp.concatenate([pen, jnp.zeros((NSA_HD - ns, tq), F32)], axis=0)
        zero = jnp.zeros((NSA_HD, tq), F32)
        sel_t += [zero, pen] if g == 0 else [pen, zero]
    sel_ref[0] = jnp.concatenate(sel_t, axis=0).T.astype(BF16)


def _importance_matrix(nbp, ns):
    per = SEL_BLOCK // CMP_STRIDE
    m = np.zeros((ns, nbp), np.float32)
    for n in range(ns):
        for c in range(per * n, per * (n + 1)):
            for cc in (c - 1, c):
                if 0 <= cc < nbp - 1:
                    m[n, cc] += 1.0
    return jnp.asarray(m, BF16)


def _cmp_attention(q, kc, vc2, b, s):
    nbp = s // CMP_STRIDE
    ns = s // SEL_BLOCK
    tq = ATT_TQ
    assert ns <= NSA_HD
    assert min(SEL_TOPK, ns) >= 3
    q3 = q.reshape(b, s, NSA_HEADS * LANES)
    kern = functools.partial(_cmp_kernel, tq=tq, ns=ns, n_sel=min(SEL_TOPK, ns))
    return pl.pallas_call(
        kern,
        grid=(b, s // tq),
        in_specs=[pl.BlockSpec((1, tq, NSA_HEADS * LANES), lambda i, j: (i, j, 0)),
                  pl.BlockSpec((1, nbp, KV_W), lambda i, j: (i, 0, 0)),
                  pl.BlockSpec((1, nbp, 2 * KV_W), lambda i, j: (i, 0, 0)),
                  _resident((ns, nbp))],
        out_specs=[pl.BlockSpec((1, tq, NSA_WIDTH), lambda i, j: (i, j, 0)),
                   pl.BlockSpec((1, tq, NSA_KV_GROUPS * LANES), lambda i, j: (i, j, 0))],
        out_shape=[jax.ShapeDtypeStruct((b, s, NSA_WIDTH), F32),
                   jax.ShapeDtypeStruct((b, s, NSA_KV_GROUPS * LANES), BF16)],
        compiler_params=_params(("parallel", "parallel")),
        name="cmp_topk",
    )(q3, kc, vc2, _importance_matrix(nbp, ns))


def _attn_kernel(q_ref, ks_ref, vs_ref, kw_ref, vw_ref, penq_ref, gl_ref, gexp_ref, ocmp_ref,
                 o_ref, qa_scr, s_scr, p_scr, al_scr, m_scr, acc_scr, *, tq, tk):
    i = pl.program_id(1)
    rows_of = lambda hh: slice(hh * tq, (hh + 1) * tq)
    lane_lo = lax.broadcasted_iota(jnp.int32, (tq, LANES), 1) < NSA_HD
    rel = (lax.broadcasted_iota(jnp.int32, (tq, tk), 1)
           - lax.broadcasted_iota(jnp.int32, (tq, tk), 0)).astype(F32)

    for hh in range(HEADS_PER_GROUP):
        qa_scr[rows_of(hh)] = q_ref[0, :, hh * LANES:(hh + 1) * LANES] + penq_ref[0]

    def scores(k_ref, j, slot):
        s_scr[slot] = _nt(qa_scr[...], k_ref[0, pl.ds(pl.multiple_of(j * tk, tk), tk), :])

    def probs(br, j, mode, slot, first, exists=None):
        for hh in range(HEADS_PER_GROUP):
            r = rows_of(hh)
            s = s_scr[slot, r]
            if mode == "causal":
                s = jnp.where(rel <= 0.0, s, NEG)
            elif mode == "lower":
                s = jnp.where(rel + (j * tk - i * tq).astype(F32) > -float(WINDOW), s, NEG)
            if exists is not None:
                s = jnp.where(exists, s, NEG)
            m_cur = jnp.max(s, axis=-1, keepdims=True)
            if first:
                m_new = jnp.broadcast_to(m_cur, (tq, LANES))
            else:
                m_prev = m_scr[br, r]
                m_new = jnp.maximum(m_prev, m_cur)
                al_scr[slot, r] = jnp.exp(m_prev - m_new)
            m_scr[br, r] = m_new
            p_scr[slot, r] = jnp.exp(s - jnp.concatenate([m_new] * (tk // LANES), axis=1)).astype(BF16)

    def accumulate(br, v_ref, j, slot, first):
        pv = _nn(p_scr[slot], v_ref[0, pl.ds(pl.multiple_of(j * tk, tk), tk), :])
        for hh in range(HEADS_PER_GROUP):
            r = rows_of(hh)
            if first:
                acc_scr[br, r] = pv[r]
            else:
                acc_scr[br, r] = al_scr[slot, r] * acc_scr[br, r] + pv[r]

    last = jnp.maximum(i - 1, 0)
    tile_of = lambda n: jnp.minimum(n - 1, last)
    scores(ks_ref, i, 0)
    scores(ks_ref, tile_of(1), 1)
    probs(0, i, "causal", 0, True)
    scores(ks_ref, tile_of(2), 0)
    probs(0, tile_of(1), None, 1, False)
    accumulate(0, vs_ref, i, 0, True)

    def sel_step(n, parity):
        scores(ks_ref, tile_of(n + 2), parity)
        probs(0, tile_of(n + 1), None, 1 - parity, False)
        accumulate(0, vs_ref, n - 1, parity, False)

    def sel_two_steps(t, carry):
        sel_step(2 * t + 1, 1)
        sel_step(2 * t + 2, 0)
        return carry

    lax.fori_loop(0, i // 2, sel_two_steps, 0)

    @pl.when(i % 2 == 1)
    def _():
        sel_step(i, 1)

    n_back = WINDOW // tk
    back = [(jnp.maximum(i - d, 0), "lower" if d == n_back else None, i >= d) for d in range(1, n_back + 1)]
    scores(kw_ref, i, 0)
    scores(kw_ref, back[0][0], 1)
    probs(1, i, "causal", 0, True)
    accumulate(1, vw_ref, i, 0, True)
    for d, (j, mode, exists) in enumerate(back, start=1):
        if d < n_back:
            scores(kw_ref, back[d][0], (d + 1) % 2)
        probs(1, j, mode, d % 2, False, exists)
        accumulate(1, vw_ref, j, d % 2, False)

    gexp = sum(_nn(part, gexp_ref[0]) for part in _split2(_sigmoid(gl_ref[0])))
    n_pairs = HEADS_PER_GROUP // 2
    gate_blk = lambda br, jj: gexp[:, (br * n_pairs + jj) * LANES:(br * n_pairs + jj + 1) * LANES]
    for jj in range(n_pairs):
        cols = slice(jj * LANES, (jj + 1) * LANES)
        blk = gate_blk(0, jj) * ocmp_ref[0, :, cols]
        for br in range(2):
            even = acc_scr[br, rows_of(2 * jj)]
            odd = acc_scr[br, rows_of(2 * jj + 1)]
            low = even / pltpu.roll(even, NSA_HD, axis=1)
            up = pltpu.roll(odd, NSA_HD, axis=1) / odd
            blk = blk + gate_blk(br + 1, jj) * jnp.where(lane_lo, low, up)
        o_ref[0, :, cols] = blk.astype(o_ref.dtype)


def _gate_expansion():
    n_pairs = HEADS_PER_GROUP // 2
    r = np.zeros((NSA_KV_GROUPS, LANES, 3 * n_pairs * LANES), np.float32)
    for g in range(NSA_KV_GROUPS):
        for br in range(3):
            for jj in range(n_pairs):
                for odd in range(2):
                    src = br * NSA_HEADS + g * HEADS_PER_GROUP + 2 * jj + odd
                    dst = (br * n_pairs + jj) * LANES + odd * NSA_HD
                    r[g, src, dst:dst + NSA_HD] = 1.0
    return jnp.asarray(r, BF16)


def _attention(q, ks, vs, kw, vw, pen, gl, ocmp, b, s):
    tq, tk = ATT_TQ, ATT_TK
    assert tq == tk and WINDOW % tk == 0 and WINDOW // tk <= 2 and tq % SEL_BLOCK == 0
    gw = HEADS_PER_GROUP * LANES
    ow = HEADS_PER_GROUP * NSA_HD
    r3 = lambda a: a.reshape(b, s, a.shape[-1])
    tile = lambda w: pl.BlockSpec((1, tq, w), lambda i, j, g: (i, j, 0))
    gtile = lambda w: pl.BlockSpec((1, tq, w), lambda i, j, g: (i, j, g))
    gfull = pl.BlockSpec((1, s, LANES), lambda i, j, g: (i, 0, g))
    gexp = _gate_expansion()
    rows = HEADS_PER_GROUP * tq
    kern = functools.partial(_attn_kernel, tq=tq, tk=tk)
    return pl.pallas_call(
        kern,
        grid=(b, s // tq, NSA_KV_GROUPS),
        in_specs=[gtile(gw), gfull, gfull, gfull, gfull, gtile(LANES), tile(LANES),
                  pl.BlockSpec((1,) + gexp.shape[1:], lambda i, j, g: (g, 0, 0)), gtile(ow)],
        out_specs=gtile(ow),
        out_shape=jax.ShapeDtypeStruct((b, s, NSA_WIDTH), BF16),
        scratch_shapes=[pltpu.VMEM((rows, LANES), BF16), pltpu.VMEM((2, rows, tk), F32),
                        pltpu.VMEM((2, rows, tk), BF16), pltpu.VMEM((2, rows, LANES), F32),
                        pltpu.VMEM((2, rows, LANES), F32), pltpu.VMEM((2, rows, LANES), F32)],
        compiler_params=_params(("parallel", "parallel", "parallel")),
        name="sel_win_attention",
    )(r3(q), r3(ks), r3(vs), r3(kw), r3(vw), pen, r3(gl), gexp, ocmp)


def _hgrn_kernel(hq_ref, hf_ref, hi_ref, hg_ref, lbraw_ref, onorm_ref, o_ref, st_scr, b_scr,
                 *, layer, n_chunks):
    c = HG_CHUNK

    @pl.when(pl.program_id(1) == 0)
    def _():
        st_scr[...] = jnp.zeros(st_scr.shape, F32)

    raw = lbraw_ref[...]
    ex = jnp.exp(raw - jnp.max(raw, axis=0, keepdims=True))
    sm = ex / jnp.sum(ex, axis=0, keepdims=True)
    lb_all = jnp.zeros((1, raw.shape[1]), F32)
    for l in range(1, layer + 1):
        lb_all = lb_all + sm[l:l + 1, :]

    t_idx = lax.broadcasted_iota(jnp.int32, (c, HG_DK), 0)
    sub = lax.broadcasted_iota(jnp.int32, (8, HG_DK), 0)
    ti = lax.broadcasted_iota(jnp.int32, (c, c), 0)
    si = lax.broadcasted_iota(jnp.int32, (c, c), 1)
    tril = jnp.where(si <= ti, 1.0, 0.0).astype(BF16)
    onorm = onorm_ref[...]
    levels = (32, 16, 8, 4, 2, 1)
    split_bit = ti ^ si
    pair_mask = {m: jnp.where((ti > si) & (split_bit >= m) & (split_bit < 2 * m), 1.0, 0.0) for m in levels}
    second_half = {m: (t_idx & m) != 0 for m in levels}
    sign = {m: jnp.where(second_half[m], 1.0, -1.0) for m in levels}

    def ref_rows(h, m):
        if m >= 4:
            return jnp.concatenate(
                [jnp.broadcast_to(b_scr[h, pl.ds(s0 + m - 1, 1), :], (2 * m, HG_DK))
                 for s0 in range(0, c, 2 * m)], axis=0)
        pieces = []
        for s0 in range(0, c, 8):
            r1 = jnp.broadcast_to(b_scr[h, pl.ds(s0 + 1, 1), :], (8, HG_DK))
            r5 = jnp.broadcast_to(b_scr[h, pl.ds(s0 + 5, 1), :], (8, HG_DK))
            pieces.append(jnp.where(sub < 4, r1, r5))
        return jnp.concatenate(pieces, axis=0)

    def chunk(ci, carry):
        rows = pl.ds(pl.multiple_of(ci * c, c), c)
        for h in range(HG_HEADS):
            lanes = slice(h * HG_DK, (h + 1) * HG_DK)
            lb = lb_all[:, lanes]
            z = hf_ref[rows, lanes]
            hq = hq_ref[rows, lanes]
            q = hq * _sigmoid(hq)
            ez = jnp.exp(-jnp.abs(z))
            big = 1.0 / (1.0 + ez)
            small = ez * big
            f = jnp.maximum(lb + (1.0 - lb) * jnp.where(z >= 0.0, big, small), F_FLOOR)
            logf = jnp.log(f)
            k = (1.0 - lb) * jnp.where(z >= 0.0, small, big)
            v = hi_ref[rows, lanes].astype(BF16)
            bcum = sum(_nn(tril, part) for part in _split3(logf))
            b_scr[h] = bcum

            a = jnp.where(ti == si, jnp.sum(q * k, axis=-1, keepdims=True), 0.0)
            for m in levels:
                if m == 1:
                    r = jnp.where(second_half[m], q * f, k)
                else:
                    w = jnp.exp((bcum - ref_rows(h, m)) * sign[m])
                    r = jnp.where(second_half[m], q, k) * w
                r = r.astype(BF16)
                a = a + _nt(r, r) * pair_mask[m]

            st = st_scr[h]
            o = _nn(a.astype(BF16), v) + _nt((q * jnp.exp(bcum)).astype(BF16), st.astype(BF16))
            b_last = b_scr[h, pl.ds(c - 1, 1), :]
            kd = (k * jnp.exp(b_last - bcum)).astype(BF16)
            st_scr[h] = jnp.exp(b_last) * st + _tn(v, kd)

            o = o * lax.rsqrt(jnp.mean(o * o, axis=-1, keepdims=True) + EPS) * onorm
            gate = hg_ref[rows, lanes]
            o_ref[rows, lanes] = (o * (gate * _sigmoid(gate))).astype(o_ref.dtype)
        return carry

    lax.fori_loop(0, n_chunks, chunk, 0, unroll=4)


def _hgrn(hg4, lb_raw, out_norm, layer, b, s):
    tt = HG_TOKENS
    per = s // tt
    col = lambda k: pl.BlockSpec((tt, HG_WIDTH), lambda i, j, k=k: (i * per + j, k))
    kern = functools.partial(_hgrn_kernel, layer=layer, n_chunks=tt // HG_CHUNK)
    return pl.pallas_call(
        kern,
        grid=(b, per),
        in_specs=[col(0), col(1), col(2), col(3), _resident(lb_raw.shape), _resident((1, HG_DV))],
        out_specs=pl.BlockSpec((tt, HG_WIDTH), lambda i, j: (i * per + j, 0)),
        out_shape=jax.ShapeDtypeStruct((b * s, HG_WIDTH), BF16),
        scratch_shapes=[pltpu.VMEM((HG_HEADS, HG_DV, HG_DK), F32),
                        pltpu.VMEM((HG_HEADS, HG_CHUNK, HG_DK), F32)],
        compiler_params=_params(("parallel", "arbitrary")),
        name="hgrn2",
    )(hg4, hg4, hg4, hg4, lb_raw, out_norm.reshape(1, -1))


def _mixer(h, l, b, s, mix_norm, w_in, cmp_pos_k, cmp_pos_v, cmp_k_w1, cmp_k_w2, cmp_v_w1, cmp_v_w2,
           hgrn_lower_bound, hgrn_out_norm):
    q, ks, vs, kw, vw, kc_in, vc_in, gl, hg4 = _inproj(h, mix_norm[l], _build_w_in(w_in[l]), s)
    kc = _compress(kc_in.reshape(b, s, KV_W), cmp_pos_k[l], cmp_k_w1[l], cmp_k_w2[l], (0, 1))
    vc2 = _compress(vc_in.reshape(b, s, KV_W), cmp_pos_v[l], cmp_v_w1[l], cmp_v_w2[l], (0, 1, 1, 0))
    ocmp, sel = _cmp_attention(q, kc, vc2, b, s)
    o_nsa = _attention(q, ks, vs, kw, vw, sel, gl, ocmp, b, s)
    o_hg = _hgrn(hg4, hgrn_lower_bound, hgrn_out_norm[l], l, b, s)
    return o_nsa.reshape(b * s, NSA_WIDTH), o_hg


def kernel(x, ffn1_norm, ffn1_w_gu, ffn1_w_down, mix_norm, w_in, cmp_pos_k, cmp_pos_v, cmp_k_w1, cmp_k_w2, cmp_v_w1, cmp_v_w2, hgrn_lower_bound, hgrn_out_norm, w_out, ffn2_norm, ffn2_w_gu, ffn2_w_down, final_norm):
    b, s, d = x.shape
    depth = ffn1_norm.shape[0]
    ffn1_w = _ffn_weights(ffn1_w_gu, ffn1_w_down)
    ffn2_w = _ffn_weights(ffn2_w_gu, ffn2_w_down)
    w_out = w_out.astype(BF16)
    h = x.reshape(b * s, d)
    for l in range(depth):
        h = _ffn(h, ffn1_norm[l], *(w[l] for w in ffn1_w), final_norm, False)
        o_nsa, o_hg = _mixer(h, l, b, s, mix_norm, w_in, cmp_pos_k, cmp_pos_v, cmp_k_w1, cmp_k_w2, cmp_v_w1,
                             cmp_v_w2, hgrn_lower_bound, hgrn_out_norm)
        h = _ffn(h, ffn2_norm[l], *(w[l] for w in ffn2_w), final_norm, l == depth - 1, (o_nsa, o_hg, w_out[l]))
    return h.reshape(b, s, d)
```

```python
import functools

import jax
import jax.numpy as jnp
import numpy as np
from jax import lax
from jax.experimental import pallas as pl
from jax.experimental.pallas import tpu as pltpu

F32 = jnp.float32
BF16 = jnp.bfloat16

D_MODEL = 1024
EPS = 1e-6
NEG = -1e30
FORCE = 1e6
F_FLOOR = 1e-30
NSA_HEADS = 8
NSA_KV_GROUPS = 2
HEADS_PER_GROUP = NSA_HEADS // NSA_KV_GROUPS
NSA_HD = 64
CMP_LEN = 32
CMP_STRIDE = 16
CMP_HID = 256
SEL_BLOCK = 64
SEL_TOPK = 16
WINDOW = 512
HG_HEADS = 4
HG_DK = 128
HG_DV = 128
HG_CHUNK = 64
D_FF = 2752
NSA_WIDTH = NSA_HEADS * NSA_HD
HG_WIDTH = HG_HEADS * HG_DV
KV_W = NSA_KV_GROUPS * NSA_HD

LANES = 128
D_FF_PAD = 2816
FF_CHUNK = 256
ROW_TILE = 512
ATT_TQ = 256
ATT_TK = 256
HG_TOKENS = 512
VMEM_LIMIT = 56 * 1024 * 1024

SEL_PENALTY = 2.0 ** 50
ALIBI_SLOPES = tuple(2.0 ** (-8.0 * (i + 1) / NSA_HEADS) for i in range(NSA_HEADS))

SEGMENTS = (
    ("q", NSA_HEADS * LANES, BF16, NSA_WIDTH),
    ("ks", NSA_KV_GROUPS * LANES, BF16, KV_W),
    ("vs", NSA_KV_GROUPS * LANES, BF16, KV_W),
    ("kw", NSA_KV_GROUPS * LANES, BF16, KV_W),
    ("vw", NSA_KV_GROUPS * LANES, BF16, KV_W),
    ("kc", KV_W, F32, KV_W),
    ("vc", KV_W, F32, KV_W),
    ("gl", LANES, F32, LANES),
    ("hg", 4 * HG_WIDTH, F32, 4 * HG_WIDTH),
)
SEG_OFFSETS = tuple(int(v) for v in np.cumsum([0] + [s[3] for s in SEGMENTS]))


def _nn(a, b):
    return jnp.dot(a, b, preferred_element_type=F32)


def _nt(a, b):
    return lax.dot_general(a, b, (((1,), (1,)), ((), ())), preferred_element_type=F32)


def _tn(a, b):
    return lax.dot_general(a, b, (((0,), (0,)), ((), ())), preferred_element_type=F32)


def _split2(x):
    hi = x.astype(BF16)
    lo = (x - hi.astype(F32)).astype(BF16)
    return hi, lo


def _split3(x):
    hi = x.astype(BF16)
    r = x - hi.astype(F32)
    mid = r.astype(BF16)
    lo = (r - mid.astype(F32)).astype(BF16)
    return hi, mid, lo


def _dot3(a, b):
    ah, al = _split2(a)
    bh, bl = _split2(b)
    return _nn(ah, bh) + _nn(ah, bl) + _nn(al, bh)


def _sigmoid(x):
    return 1.0 / (1.0 + jnp.exp(-x))


def _rms(x, g):
    return x * lax.rsqrt(jnp.mean(x * x, axis=-1, keepdims=True) + EPS) * g


def _resident(shape):
    nd = len(shape)
    return pl.BlockSpec(shape, lambda *_: (0,) * nd, pipeline_mode=pl.Buffered(1))


def _params(sem):
    return pltpu.CompilerParams(dimension_semantics=sem, vmem_limit_bytes=VMEM_LIMIT)


def _ffn_kernel(*refs, final, proj):
    if proj:
        x_ref, a_ref, b_ref, wo_ref, g_ref, wg_ref, wu_ref, wd_ref, gf_ref, o_ref = refs
        x = x_ref[...] + _nn(a_ref[...], wo_ref[:NSA_WIDTH, :]) + _nn(b_ref[...], wo_ref[NSA_WIDTH:, :])
    else:
        x_ref, g_ref, wg_ref, wu_ref, wd_ref, gf_ref, o_ref = refs
        x = x_ref[...]
    xn = _rms(x, g_ref[...]).astype(BF16)
    acc = jnp.zeros(x.shape, F32)
    for k in range(D_FF_PAD // FF_CHUNK):
        sl = slice(k * FF_CHUNK, (k + 1) * FF_CHUNK)
        gate = _nn(xn, wg_ref[:, sl])
        up = _nn(xn, wu_ref[:, sl])
        h = (gate * _sigmoid(gate) * up).astype(BF16)
        acc = acc + _nn(h, wd_ref[sl, :])
    y = x + 0.5 * acc
    if final:
        y = _rms(y, gf_ref[...])
    o_ref[...] = y


def _ffn_weights(w_gu, w_down):
    pad = D_FF_PAD - D_FF
    w_gu = w_gu.astype(BF16)
    wg = jnp.pad(w_gu[..., :D_FF], ((0, 0), (0, 0), (0, pad)))
    wu = jnp.pad(w_gu[..., D_FF:], ((0, 0), (0, 0), (0, pad)))
    wd = jnp.pad(w_down.astype(BF16), ((0, 0), (0, pad), (0, 0)))
    return wg, wu, wd


def _ffn(h, norm_g, wg, wu, wd, final_g, final, proj=None):
    t = h.shape[0]
    row = lambda w: pl.BlockSpec((ROW_TILE, w), lambda i: (i, 0))
    weights = [_resident((1, D_MODEL)), _resident(wg.shape), _resident(wu.shape), _resident(wd.shape),
               _resident((1, D_MODEL))]
    operands = [norm_g.reshape(1, -1), wg, wu, wd, final_g.reshape(1, -1)]
    if proj is None:
        in_specs, args = [row(D_MODEL)] + weights, [h] + operands
    else:
        o_nsa, o_hg, w_out = proj
        in_specs = [row(D_MODEL), row(NSA_WIDTH), row(HG_WIDTH), _resident(w_out.shape)] + weights
        args = [h, o_nsa, o_hg, w_out] + operands
    return pl.pallas_call(
        functools.partial(_ffn_kernel, final=final, proj=proj is not None),
        grid=(t // ROW_TILE,),
        in_specs=in_specs,
        out_specs=row(D_MODEL),
        out_shape=jax.ShapeDtypeStruct((t, D_MODEL), F32),
        compiler_params=_params(("parallel",)),
        name="ffn",
    )(*args)


def _inproj_kernel(x_ref, g_ref, w_ref, qfeat_ref, *o_refs, seq):
    xn = _rms(x_ref[...], g_ref[...]).astype(BF16)
    rows = x_ref.shape[0]
    kpos = (pl.program_id(0) % (seq // rows)) * rows + lax.broadcasted_iota(jnp.int32, (rows, LANES), 0)
    lane = lax.broadcasted_iota(jnp.int32, (rows, LANES), 1)
    blk = kpos // SEL_BLOCK
    digit = lambda f0: jnp.where(lane == f0, blk.astype(F32),
                                 jnp.where(lane == f0 + 1, (kpos % SEL_BLOCK).astype(F32), 0.0))
    halves = [NSA_HD * (1 - g) for g in range(NSA_KV_GROUPS)]
    kfeat = jnp.concatenate([digit(f0) for f0 in halves], axis=1)
    is_pen = lambda f0: (lane == f0 + 1 + blk) & (blk >= 1) & (blk <= seq // SEL_BLOCK - 2)
    kpen = jnp.concatenate([jnp.where(is_pen(f0), -SEL_PENALTY, 0.0) for f0 in halves], axis=1)
    lower = lane < NSA_HD
    swap = lambda a: pltpu.roll(a, NSA_HD, axis=1)
    small = [i for i, seg in enumerate(SEGMENTS) if seg[3] == LANES]
    assert small == list(range(small[0], small[-1] + 1))
    y_small = _nn(xn, w_ref[:, SEG_OFFSETS[small[0]]:SEG_OFFSETS[small[-1] + 1]])
    for i, (o_ref, (name, _, dtype, cols)) in enumerate(zip(o_refs, SEGMENTS, strict=True)):
        if i in small:
            y = y_small[:, (i - small[0]) * LANES:(i - small[0] + 1) * LANES]
        else:
            y = _nn(xn, w_ref[:, SEG_OFFSETS[i]:SEG_OFFSETS[i] + cols])
        if name == "q":
            blocks = []
            for h in range(NSA_HEADS):
                pair = y[:, (h // 2) * LANES:(h // 2 + 1) * LANES]
                g = h // HEADS_PER_GROUP
                data = pair if h % 2 == g else swap(pair)
                feat = qfeat_ref[:, h * LANES:(h + 1) * LANES]
                blocks.append(jnp.where(lower == (g == 0), data, feat))
            y = jnp.concatenate(blocks, axis=1)
        elif name in ("ks", "kw"):
            feat = kfeat + kpen if name == "ks" else kfeat
            y = jnp.concatenate([jnp.where(lower == (g == 0), y, feat[:, g * LANES:(g + 1) * LANES])
                                 for g in range(NSA_KV_GROUPS)], axis=1)
        elif name in ("vs", "vw"):
            y = jnp.concatenate([jnp.where(lower, y if g == 0 else swap(y), 1.0)
                                 for g in range(NSA_KV_GROUPS)], axis=1)
        o_ref[...] = y.astype(dtype)


def _q_features():
    feat = np.zeros((1, NSA_HEADS * LANES), np.float32)
    for h in range(NSA_HEADS):
        f0 = h * LANES + NSA_HD * (1 - h // HEADS_PER_GROUP)
        feat[0, f0] = SEL_BLOCK * ALIBI_SLOPES[h]
        feat[0, f0 + 1] = ALIBI_SLOPES[h]
    return jnp.asarray(feat)


def _build_w_in(w_in):
    sizes = (NSA_WIDTH, KV_W, KV_W, KV_W, KV_W, KV_W, KV_W, NSA_HEADS * 3,
             HG_WIDTH, HG_WIDTH, HG_WIDTH, HG_WIDTH)
    splits = [int(v) for v in np.cumsum(sizes)[:-1]]
    w_in = w_in.astype(BF16)
    wq, wkc, wvc, wks, wvs, wkw, wvw, wgl, whq, whf, whi, whg = jnp.split(w_in, splits, axis=1)
    d = w_in.shape[0]
    gl = wgl.reshape(d, NSA_HEADS, 3).transpose(0, 2, 1).reshape(d, 3 * NSA_HEADS)
    gl = jnp.pad(gl, ((0, 0), (0, LANES - 3 * NSA_HEADS)))
    cols = [wq * NSA_HD ** -0.5, wks, wvs, wkw, wvw, wkc, wvc, gl, whq, whf, whi, whg]
    return jnp.concatenate(cols, axis=1)


def _inproj(h, norm_g, w_ext, seq):
    t = h.shape[0]
    assert seq % ROW_TILE == 0
    row = lambda w: pl.BlockSpec((ROW_TILE, w), lambda i: (i, 0))
    return pl.pallas_call(
        functools.partial(_inproj_kernel, seq=seq),
        grid=(t // ROW_TILE,),
        in_specs=[row(D_MODEL), _resident((1, D_MODEL)), _resident(w_ext.shape),
                  _resident((1, NSA_HEADS * LANES))],
        out_specs=[row(w) for _, w, _, _ in SEGMENTS],
        out_shape=[jax.ShapeDtypeStruct((t, w), dt) for _, w, dt, _ in SEGMENTS],
        compiler_params=_params(("parallel",)),
        name="inproj",
    )(h, norm_g.reshape(1, -1), w_ext, _q_features())


def _gelu_tanh(x):
    return 0.5 * x * (1.0 + jnp.tanh(0.7978845608028654 * (x + 0.044715 * (x * x * x))))


def _compress_kernel(kv_ref, pos_ref, w1_ref, w2_ref, o_ref):
    nbp = o_ref.shape[1]
    hid = NSA_KV_GROUPS * CMP_HID
    ha = jnp.zeros((nbp, hid), F32)
    hb = jnp.zeros((nbp, hid), F32)
    for l in range(CMP_STRIDE):
        x = kv_ref[0, pl.ds(l, nbp, stride=CMP_STRIDE), :]
        ha = ha + _nn((x + pos_ref[l:l + 1, :]).astype(BF16), w1_ref[l])
        hb = hb + _nn((x + pos_ref[CMP_STRIDE + l:CMP_STRIDE + l + 1, :]).astype(BF16), w1_ref[CMP_STRIDE + l])
    act = _gelu_tanh(ha + pltpu.roll(hb, nbp - 1, axis=0)).astype(BF16)
    out = jnp.zeros(o_ref.shape[1:], F32)
    for g in range(NSA_KV_GROUPS):
        out = out + _nn(act[:, g * CMP_HID:(g + 1) * CMP_HID], w2_ref[g])
    o_ref[0] = out


def _compress(kv, pos, w1, w2, reps):
    b, s, _ = kv.shape
    nbp = s // CMP_STRIDE
    w1l = w1.astype(BF16).reshape(CMP_LEN, NSA_HD, CMP_HID)
    z1 = jnp.zeros_like(w1l)
    w1p = jnp.concatenate([jnp.concatenate([w1l, z1], axis=2), jnp.concatenate([z1, w1l], axis=2)], axis=1)
    pos2 = jnp.concatenate([pos] * NSA_KV_GROUPS, axis=1)
    w2 = w2.astype(BF16)
    zero = jnp.zeros_like(w2)
    w2p = jnp.stack([jnp.concatenate([w2 if r == g else zero for r in reps], axis=1)
                     for g in range(NSA_KV_GROUPS)])
    width = len(reps) * NSA_HD
    return pl.pallas_call(
        _compress_kernel,
        grid=(b,),
        in_specs=[pl.BlockSpec((1, s, KV_W), lambda i: (i, 0, 0)),
                  _resident(pos2.shape), _resident(w1p.shape), _resident(w2p.shape)],
        out_specs=pl.BlockSpec((1, nbp, width), lambda i: (i, 0, 0)),
        out_shape=jax.ShapeDtypeStruct((b, nbp, width), F32),
        compiler_params=_params(("parallel",)),
        name="compress",
    )(kv, pos2, w1p, w2p)


def _pair_blocks(g, jj):
    lower = slice(0, LANES) if g == 0 else slice(LANES, 2 * LANES)
    upper = slice(LANES, 2 * LANES) if g == 0 else slice(0, LANES)
    return lower, upper


def _cmp_kernel(q_ref, kc_ref, vc2_ref, mt_ref, ocmp_ref, sel_ref, picks_ref, *, tq, ns, n_sel):
    nbp = kc_ref.shape[1]
    q0 = pl.program_id(1) * tq
    pos = q0 + lax.broadcasted_iota(jnp.int32, (tq, nbp), 0)
    blk_end = lax.broadcasted_iota(jnp.int32, (tq, nbp), 1) * CMP_STRIDE + (CMP_LEN - 1)
    valid = blk_end <= pos
    row_ok = (q0 + lax.broadcasted_iota(jnp.int32, (tq, 1), 0)) >= CMP_LEN - 1
    kc = kc_ref[0].astype(BF16)
    vc2 = vc2_ref[0].astype(BF16)
    lane_c = lax.broadcasted_iota(jnp.int32, (nbp, LANES), 1)
    c_idx = lax.broadcasted_iota(jnp.int32, (nbp, LANES), 0)
    per = SEL_BLOCK // CMP_STRIDE
    feat_a = (c_idx // per - q0 // SEL_BLOCK).astype(F32)
    feat_b = ((c_idx % per) * CMP_STRIDE + (CMP_LEN - 1)).astype(F32)
    lane_lo = lax.broadcasted_iota(jnp.int32, (tq, LANES), 1) < NSA_HD
    blk_f = lax.broadcasted_iota(jnp.int32, (ns, tq), 0).astype(F32)
    blk = lax.broadcasted_iota(jnp.int32, (ns, tq), 0)
    pos_t = q0 + lax.broadcasted_iota(jnp.int32, (ns, tq), 1)
    cur = pos_t // SEL_BLOCK
    forced = (blk == 0) | (blk == cur) | (blk == cur - 1)
    causal = blk * SEL_BLOCK <= pos_t
    sel_t, picks = [], []
    for g in range(NSA_KV_GROUPS):
        f0 = NSA_HD * (1 - g)
        feat = jnp.where(lane_c == f0, feat_a, jnp.where(lane_c == f0 + 1, feat_b, 0.0))
        kc_g = jnp.where((lane_c // NSA_HD) == g, kc, feat.astype(BF16))
        imp = jnp.zeros((tq, nbp), F32)
        acc = []
        for hh in range(HEADS_PER_GROUP):
            h = g * HEADS_PER_GROUP + hh
            s = _nt(q_ref[0, :, h * LANES:(h + 1) * LANES], kc_g)
            s = jnp.where(valid, s, NEG)
            e = jnp.exp(s - jnp.max(s, axis=-1, keepdims=True))
            inv = jnp.where(row_ok, 1.0 / jnp.sum(e, axis=-1, keepdims=True), 0.0)
            p = e * inv
            imp = imp + p
            acc.append(_nn(p.astype(BF16), vc2))
        for jj in range(HEADS_PER_GROUP // 2):
            lower, upper = _pair_blocks(g, jj)
            blk_out = jnp.where(lane_lo, acc[2 * jj][:, lower], acc[2 * jj + 1][:, upper])
            c0 = (g * HEADS_PER_GROUP + 2 * jj) * NSA_HD
            ocmp_ref[0, :, c0:c0 + LANES] = blk_out
        p_slc = sum(_nt(mt_ref[...], part) for part in _split3(imp))
        taken = -jnp.inf
        score = jnp.where(forced, taken, jnp.where(causal, p_slc, NEG))
        for _ in range(n_sel - 3):
            top = jnp.max(score, axis=0, keepdims=True)
            first = jnp.min(jnp.where(score == top, blk_f, float(ns)), axis=0, keepdims=True)
            score = jnp.where(blk_f == first, taken, score)
        picked = jnp.where(score == taken, 1.0, 0.0).astype(BF16)
        picks.append(_nt(jnp.ones((8, tq), BF16), picked))
        pen = jnp.where(score == taken, 0.0, SEL_PENALTY)
        pen = jnp.where((blk >= 2) & (blk <= ns - 1), pltpu.roll(pen, 1, axis=0), 0.0)
        if ns < NSA_HD:
            pen = jnp.concatenate([pen, jnp.zeros((NSA_HD - ns, tq), F32)], axis=0)
        zero = jnp.zeros((NSA_HD, tq), F32)
        sel_t += [zero, pen] if g == 0 else [pen, zero]
    sel_ref[0] = jnp.concatenate(sel_t, axis=0).T.astype(BF16)
    picks_ref[0, 0] = jnp.concatenate(picks, axis=1)


def _importance_matrix(nbp, ns):
    per = SEL_BLOCK // CMP_STRIDE
    m = np.zeros((ns, nbp), np.float32)
    for n in range(ns):
        for c in range(per * n, per * (n + 1)):
            for cc in (c - 1, c):
                if 0 <= cc < nbp - 1:
                    m[n, cc] += 1.0
    return jnp.asarray(m, BF16)


def _cmp_attention(q, kc, vc2, b, s):
    nbp = s // CMP_STRIDE
    ns = s // SEL_BLOCK
    tq = ATT_TQ
    assert ns <= NSA_HD
    assert min(SEL_TOPK, ns) >= 3
    q3 = q.reshape(b, s, NSA_HEADS * LANES)
    kern = functools.partial(_cmp_kernel, tq=tq, ns=ns, n_sel=min(SEL_TOPK, ns))
    return pl.pallas_call(
        kern,
        grid=(b, s // tq),
        in_specs=[pl.BlockSpec((1, tq, NSA_HEADS * LANES), lambda i, j: (i, j, 0)),
                  pl.BlockSpec((1, nbp, KV_W), lambda i, j: (i, 0, 0)),
                  pl.BlockSpec((1, nbp, 2 * KV_W), lambda i, j: (i, 0, 0)),
                  _resident((ns, nbp))],
        out_specs=[pl.BlockSpec((1, tq, NSA_WIDTH), lambda i, j: (i, j, 0)),
                   pl.BlockSpec((1, tq, NSA_KV_GROUPS * LANES), lambda i, j: (i, j, 0)),
                   pl.BlockSpec((1, 1, 8, NSA_KV_GROUPS * ns), lambda i, j: (i, j, 0, 0))],
        out_shape=[jax.ShapeDtypeStruct((b, s, NSA_WIDTH), F32),
                   jax.ShapeDtypeStruct((b, s, NSA_KV_GROUPS * LANES), BF16),
                   jax.ShapeDtypeStruct((b, s // tq, 8, NSA_KV_GROUPS * ns), F32)],
        compiler_params=_params(("parallel", "parallel")),
        name="cmp_topk",
    )(q3, kc, vc2, _importance_matrix(nbp, ns))


def _attn_kernel(tiles_ref, ntiles_ref, q_ref, ks_ref, vs_ref, kw_ref, vw_ref, penq_ref, gl_ref, gexp_ref,
                 ocmp_ref, o_ref, qa_scr, s_scr, p_scr, al_scr, m_scr, acc_scr, *, tq, tk):
    i = pl.program_id(1)
    rows_of = lambda hh: slice(hh * tq, (hh + 1) * tq)
    lane_lo = lax.broadcasted_iota(jnp.int32, (tq, LANES), 1) < NSA_HD
    rel = (lax.broadcasted_iota(jnp.int32, (tq, tk), 1)
           - lax.broadcasted_iota(jnp.int32, (tq, tk), 0)).astype(F32)

    for hh in range(HEADS_PER_GROUP):
        qa_scr[rows_of(hh)] = q_ref[0, :, hh * LANES:(hh + 1) * LANES] + penq_ref[0]

    def scores(k_ref, j, slot):
        s_scr[slot] = _nt(qa_scr[...], k_ref[0, pl.ds(pl.multiple_of(j * tk, tk), tk), :])

    def probs(br, j, mode, slot, first, exists=None):
        for hh in range(HEADS_PER_GROUP):
            r = rows_of(hh)
            s = s_scr[slot, r]
            if mode == "causal":
                s = jnp.where(rel <= 0.0, s, NEG)
            elif mode == "lower":
                s = jnp.where(rel + (j * tk - i * tq).astype(F32) > -float(WINDOW), s, NEG)
            if exists is not None:
                s = jnp.where(exists, s, NEG)
            m_cur = jnp.max(s, axis=-1, keepdims=True)
            if first:
                m_new = jnp.broadcast_to(m_cur, (tq, LANES))
            else:
                m_prev = m_scr[br, r]
                m_new = jnp.maximum(m_prev, m_cur)
                al_scr[slot, r] = jnp.exp(m_prev - m_new)
            m_scr[br, r] = m_new
            p_scr[slot, r] = jnp.exp(s - jnp.concatenate([m_new] * (tk // LANES), axis=1)).astype(BF16)

    def accumulate(br, v_ref, j, slot, first):
        pv = _nn(p_scr[slot], v_ref[0, pl.ds(pl.multiple_of(j * tk, tk), tk), :])
        for hh in range(HEADS_PER_GROUP):
            r = rows_of(hh)
            if first:
                acc_scr[br, r] = pv[r]
            else:
                acc_scr[br, r] = al_scr[slot, r] * acc_scr[br, r] + pv[r]

    step = (pl.program_id(0) * pl.num_programs(1) + i) * pl.num_programs(2) + pl.program_id(2)
    n_tiles = ntiles_ref[step]
    last = jnp.maximum(n_tiles - 1, 0)
    tile_of = lambda n: tiles_ref[step * pl.num_programs(1) + jnp.minimum(n - 1, last)]
    scores(ks_ref, i, 0)
    scores(ks_ref, tile_of(1), 1)
    probs(0, i, "causal", 0, True)
    scores(ks_ref, tile_of(2), 0)
    probs(0, tile_of(1), None, 1, False)
    accumulate(0, vs_ref, i, 0, True)

    def sel_step(n, parity):
        scores(ks_ref, tile_of(n + 2), parity)
        probs(0, tile_of(n + 1), None, 1 - parity, False)
        accumulate(0, vs_ref, tile_of(n), parity, False)

    def sel_two_steps(t, carry):
        sel_step(2 * t + 1, 1)
        sel_step(2 * t + 2, 0)
        return carry

    lax.fori_loop(0, n_tiles // 2, sel_two_steps, 0)

    @pl.when(n_tiles % 2 == 1)
    def _():
        sel_step(n_tiles, 1)

    n_back = WINDOW // tk
    back = [(jnp.maximum(i - d, 0), "lower" if d == n_back else None, i >= d) for d in range(1, n_back + 1)]
    scores(kw_ref, i, 0)
    scores(kw_ref, back[0][0], 1)
    probs(1, i, "causal", 0, True)
    accumulate(1, vw_ref, i, 0, True)
    for d, (j, mode, exists) in enumerate(back, start=1):
        if d < n_back:
            scores(kw_ref, back[d][0], (d + 1) % 2)
        probs(1, j, mode, d % 2, False, exists)
        accumulate(1, vw_ref, j, d % 2, False)

    gexp = sum(_nn(part, gexp_ref[0]) for part in _split2(_sigmoid(gl_ref[0])))
    n_pairs = HEADS_PER_GROUP // 2
    gate_blk = lambda br, jj: gexp[:, (br * n_pairs + jj) * LANES:(br * n_pairs + jj + 1) * LANES]
    for jj in range(n_pairs):
        cols = slice(jj * LANES, (jj + 1) * LANES)
        blk = gate_blk(0, jj) * ocmp_ref[0, :, cols]
        for br in range(2):
            even = acc_scr[br, rows_of(2 * jj)]
            odd = acc_scr[br, rows_of(2 * jj + 1)]
            low = even / pltpu.roll(even, NSA_HD, axis=1)
            up = pltpu.roll(odd, NSA_HD, axis=1) / odd
            blk = blk + gate_blk(br + 1, jj) * jnp.where(lane_lo, low, up)
        o_ref[0, :, cols] = blk.astype(o_ref.dtype)


def _gate_expansion():
    n_pairs = HEADS_PER_GROUP // 2
    r = np.zeros((NSA_KV_GROUPS, LANES, 3 * n_pairs * LANES), np.float32)
    for g in range(NSA_KV_GROUPS):
        for br in range(3):
            for jj in range(n_pairs):
                for odd in range(2):
                    src = br * NSA_HEADS + g * HEADS_PER_GROUP + 2 * jj + odd
                    dst = (br * n_pairs + jj) * LANES + odd * NSA_HD
                    r[g, src, dst:dst + NSA_HD] = 1.0
    return jnp.asarray(r, BF16)


def _picked_tiles(picks, b, s):
    ns, nq = s // SEL_BLOCK, s // ATT_TQ
    per_tile = ATT_TK // SEL_BLOCK
    per_block = picks[:, :, 0, :].reshape(b, nq, NSA_KV_GROUPS, ns // per_tile, per_tile)
    j = jnp.arange(ns // per_tile, dtype=jnp.int32)
    i = jnp.arange(nq, dtype=jnp.int32)[None, :, None, None]
    active = ((per_block.sum(-1) > 0) | (j == 0)) & (j < i)
    order = jnp.argsort(jnp.where(active, j, j + j.size), axis=-1).astype(jnp.int32)
    return order.reshape(-1), active.sum(-1).astype(jnp.int32).reshape(-1)


def _attention(q, ks, vs, kw, vw, pen, picks, gl, ocmp, b, s):
    tq, tk = ATT_TQ, ATT_TK
    assert tq == tk and WINDOW % tk == 0 and WINDOW // tk <= 2 and tq % SEL_BLOCK == 0
    gw = HEADS_PER_GROUP * LANES
    ow = HEADS_PER_GROUP * NSA_HD
    r3 = lambda a: a.reshape(b, s, a.shape[-1])
    tile = lambda w: pl.BlockSpec((1, tq, w), lambda i, j, g, *_: (i, j, 0))
    gtile = lambda w: pl.BlockSpec((1, tq, w), lambda i, j, g, *_: (i, j, g))
    gfull = pl.BlockSpec((1, s, LANES), lambda i, j, g, *_: (i, 0, g))
    gexp = _gate_expansion()
    rows = HEADS_PER_GROUP * tq
    kern = functools.partial(_attn_kernel, tq=tq, tk=tk)
    tiles, n_tiles = _picked_tiles(picks, b, s)
    grid_spec = pltpu.PrefetchScalarGridSpec(
        num_scalar_prefetch=2,
        grid=(b, s // tq, NSA_KV_GROUPS),
        in_specs=[gtile(gw), gfull, gfull, gfull, gfull, gtile(LANES), tile(LANES),
                  pl.BlockSpec((1,) + gexp.shape[1:], lambda i, j, g, *_: (g, 0, 0)), gtile(ow)],
        out_specs=gtile(ow),
        scratch_shapes=[pltpu.VMEM((rows, LANES), BF16), pltpu.VMEM((2, rows, tk), F32),
                        pltpu.VMEM((2, rows, tk), BF16), pltpu.VMEM((2, rows, LANES), F32),
                        pltpu.VMEM((2, rows, LANES), F32), pltpu.VMEM((2, rows, LANES), F32)])
    return pl.pallas_call(
        kern,
        grid_spec=grid_spec,
        out_shape=jax.ShapeDtypeStruct((b, s, NSA_WIDTH), BF16),
        compiler_params=_params(("parallel", "parallel", "parallel")),
        name="sel_win_attention",
    )(tiles, n_tiles, r3(q), r3(ks), r3(vs), r3(kw), r3(vw), pen, r3(gl), gexp, ocmp)


def _hgrn_kernel(hq_ref, hf_ref, hi_ref, hg_ref, lbraw_ref, onorm_ref, o_ref, st_scr, *, layer, n_chunks):
    c = HG_CHUNK

    @pl.when(pl.program_id(1) == 0)
    def _():
        st_scr[...] = jnp.zeros(st_scr.shape, F32)

    raw = lbraw_ref[...]
    ex = jnp.exp(raw - jnp.max(raw, axis=0, keepdims=True))
    sm = ex / jnp.sum(ex, axis=0, keepdims=True)
    lb_all = jnp.zeros((1, raw.shape[1]), F32)
    for l in range(1, layer + 1):
        lb_all = lb_all + sm[l:l + 1, :]

    t_idx = lax.broadcasted_iota(jnp.int32, (c, HG_DK), 0)
    sub = lax.broadcasted_iota(jnp.int32, (8, HG_DK), 0)
    ti = lax.broadcasted_iota(jnp.int32, (c, c), 0)
    si = lax.broadcasted_iota(jnp.int32, (c, c), 1)
    tril = jnp.where(si <= ti, 1.0, 0.0).astype(BF16)
    onorm = onorm_ref[...]
    levels = (32, 16, 8, 4, 2, 1)
    split_bit = ti ^ si
    pair_mask = {m: jnp.where((ti > si) & (split_bit >= m) & (split_bit < 2 * m), 1.0, 0.0) for m in levels}
    second_half = {m: (t_idx & m) != 0 for m in levels}
    sign = {m: jnp.where(second_half[m], 1.0, -1.0) for m in levels}

    def ref_rows(b, m):
        row = lambda r, n: jnp.broadcast_to(b[r:r + 1, :], (n, HG_DK))
        if m >= 4:
            return jnp.concatenate([row(s0 + m - 1, 2 * m) for s0 in range(0, c, 2 * m)], axis=0)
        return jnp.concatenate([jnp.where(sub < 4, row(s0 + 1, 8), row(s0 + 5, 8)) for s0 in range(0, c, 8)],
                               axis=0)

    def chunk(ci, carry):
        rows = pl.ds(pl.multiple_of(ci * c, c), c)
        for h in range(HG_HEADS):
            lanes = slice(h * HG_DK, (h + 1) * HG_DK)
            lb = lb_all[:, lanes]
            z = hf_ref[rows, lanes]
            hq = hq_ref[rows, lanes]
            q = hq * _sigmoid(hq)
            ez = jnp.exp(-jnp.abs(z))
            big = 1.0 / (1.0 + ez)
            small = ez * big
            f = jnp.maximum(lb + (1.0 - lb) * jnp.where(z >= 0.0, big, small), F_FLOOR)
            logf = jnp.log2(f)
            k = (1.0 - lb) * jnp.where(z >= 0.0, small, big)
            v = hi_ref[rows, lanes].astype(BF16)
            bcum = sum(_nn(tril, part) for part in _split3(logf))

            a = jnp.where(ti == si, jnp.sum(q * k, axis=-1, keepdims=True), 0.0)
            for m in levels:
                if m == 1:
                    r = jnp.where(second_half[m], q * f, k)
                else:
                    w = jnp.exp2((bcum - ref_rows(bcum, m)) * sign[m])
                    r = jnp.where(second_half[m], q, k) * w
                r = r.astype(BF16)
                a = a + _nt(r, r) * pair_mask[m]

            st = st_scr[h]
            o = _nn(a.astype(BF16), v) + _nt((q * jnp.exp2(bcum)).astype(BF16), st.astype(BF16))
            b_last = bcum[c - 1:c, :]
            kd = (k * jnp.exp2(b_last - bcum)).astype(BF16)
            st_scr[h] = jnp.exp2(b_last) * st + _tn(v, kd)

            o = o * lax.rsqrt(jnp.mean(o * o, axis=-1, keepdims=True) + EPS) * onorm
            gate = hg_ref[rows, lanes]
            o_ref[rows, lanes] = (o * (gate * _sigmoid(gate))).astype(o_ref.dtype)
        return carry

    lax.fori_loop(0, n_chunks, chunk, 0, unroll=4)


def _hgrn(hg4, lb_raw, out_norm, layer, b, s):
    tt = HG_TOKENS
    per = s // tt
    col = lambda k: pl.BlockSpec((tt, HG_WIDTH), lambda i, j, k=k: (i * per + j, k))
    kern = functools.partial(_hgrn_kernel, layer=layer, n_chunks=tt // HG_CHUNK)
    return pl.pallas_call(
        kern,
        grid=(b, per),
        in_specs=[col(0), col(1), col(2), col(3), _resident(lb_raw.shape), _resident((1, HG_DV))],
        out_specs=pl.BlockSpec((tt, HG_WIDTH), lambda i, j: (i * per + j, 0)),
        out_shape=jax.ShapeDtypeStruct((b * s, HG_WIDTH), BF16),
        scratch_shapes=[pltpu.VMEM((HG_HEADS, HG_DV, HG_DK), F32)],
        compiler_params=_params(("parallel", "arbitrary")),
        name="hgrn2",
    )(hg4, hg4, hg4, hg4, lb_raw, out_norm.reshape(1, -1))


def _mixer(h, l, b, s, mix_norm, w_in, cmp_pos_k, cmp_pos_v, cmp_k_w1, cmp_k_w2, cmp_v_w1, cmp_v_w2,
           hgrn_lower_bound, hgrn_out_norm):
    q, ks, vs, kw, vw, kc_in, vc_in, gl, hg4 = _inproj(h, mix_norm[l], _build_w_in(w_in[l]), s)
    kc = _compress(kc_in.reshape(b, s, KV_W), cmp_pos_k[l], cmp_k_w1[l], cmp_k_w2[l], (0, 1))
    vc2 = _compress(vc_in.reshape(b, s, KV_W), cmp_pos_v[l], cmp_v_w1[l], cmp_v_w2[l], (0, 1, 1, 0))
    ocmp, sel, picks = _cmp_attention(q, kc, vc2, b, s)
    o_nsa = _attention(q, ks, vs, kw, vw, sel, picks, gl, ocmp, b, s)
    o_hg = _hgrn(hg4, hgrn_lower_bound, hgrn_out_norm[l], l, b, s)
    return o_nsa.reshape(b * s, NSA_WIDTH), o_hg


def kernel(x, ffn1_norm, ffn1_w_gu, ffn1_w_down, mix_norm, w_in, cmp_pos_k, cmp_pos_v, cmp_k_w1, cmp_k_w2, cmp_v_w1, cmp_v_w2, hgrn_lower_bound, hgrn_out_norm, w_out, ffn2_norm, ffn2_w_gu, ffn2_w_down, final_norm):
    b, s, d = x.shape
    depth = ffn1_norm.shape[0]
    ffn1_w = _ffn_weights(ffn1_w_gu, ffn1_w_down)
    ffn2_w = _ffn_weights(ffn2_w_gu, ffn2_w_down)
    w_out = w_out.astype(BF16)
    h = x.reshape(b * s, d)
    for l in range(depth):
        h = _ffn(h, ffn1_norm[l], *(w[l] for w in ffn1_w), final_norm, False)
        o_nsa, o_hg = _mixer(h, l, b, s, mix_norm, w_in, cmp_pos_k, cmp_pos_v, cmp_k_w1, cmp_k_w2, cmp_v_w1,
                             cmp_v_w2, hgrn_lower_bound, hgrn_out_norm)
        h = _ffn(h, ffn2_norm[l], *(w[l] for w in ffn2_w), final_norm, l == depth - 1, (o_nsa, o_hg, w_out[l]))
    return h.reshape(b, s, d)
```

```python
import functools

import jax
import jax.numpy as jnp
import numpy as np
from jax import lax
from jax.experimental import pallas as pl
from jax.experimental.pallas import tpu as pltpu

F32 = jnp.float32
BF16 = jnp.bfloat16

D_MODEL = 1024
EPS = 1e-6
NEG = -1e30
FORCE = 1e6
F_FLOOR = 1e-30
NSA_HEADS = 8
NSA_KV_GROUPS = 2
HEADS_PER_GROUP = NSA_HEADS // NSA_KV_GROUPS
NSA_HD = 64
CMP_LEN = 32
CMP_STRIDE = 16
CMP_HID = 256
SEL_BLOCK = 64
SEL_TOPK = 16
WINDOW = 512
HG_HEADS = 4
HG_DK = 128
HG_DV = 128
HG_CHUNK = 64
D_FF = 2752
NSA_WIDTH = NSA_HEADS * NSA_HD
HG_WIDTH = HG_HEADS * HG_DV
KV_W = NSA_KV_GROUPS * NSA_HD

LANES = 128
D_FF_PAD = 2816
FF_CHUNK = 256
ROW_TILE = 512
ATT_TQ = 256
ATT_TK = 256
HG_TOKENS = 512
VMEM_LIMIT = 56 * 1024 * 1024

SEL_PENALTY = 2.0 ** 50
ALIBI_SLOPES = tuple(2.0 ** (-8.0 * (i + 1) / NSA_HEADS) for i in range(NSA_HEADS))

SEGMENTS = (
    ("q", NSA_HEADS * LANES, BF16, NSA_WIDTH),
    ("ks", NSA_KV_GROUPS * LANES, BF16, KV_W),
    ("vs", NSA_KV_GROUPS * LANES, BF16, KV_W),
    ("kw", NSA_KV_GROUPS * LANES, BF16, KV_W),
    ("vw", NSA_KV_GROUPS * LANES, BF16, KV_W),
    ("kc", KV_W, F32, KV_W),
    ("vc", KV_W, F32, KV_W),
    ("gl", LANES, F32, LANES),
    ("hg", 4 * HG_WIDTH, F32, 4 * HG_WIDTH),
)
SEG_OFFSETS = tuple(int(v) for v in np.cumsum([0] + [s[3] for s in SEGMENTS]))


def _nn(a, b):
    return jnp.dot(a, b, preferred_element_type=F32)


def _nt(a, b):
    return lax.dot_general(a, b, (((1,), (1,)), ((), ())), preferred_element_type=F32)


def _tn(a, b):
    return lax.dot_general(a, b, (((0,), (0,)), ((), ())), preferred_element_type=F32)


def _split2(x):
    hi = x.astype(BF16)
    lo = (x - hi.astype(F32)).astype(BF16)
    return hi, lo


def _split3(x):
    hi = x.astype(BF16)
    r = x - hi.astype(F32)
    mid = r.astype(BF16)
    lo = (r - mid.astype(F32)).astype(BF16)
    return hi, mid, lo


def _dot3(a, b):
    ah, al = _split2(a)
    bh, bl = _split2(b)
    return _nn(ah, bh) + _nn(ah, bl) + _nn(al, bh)


def _sigmoid(x):
    return 1.0 / (1.0 + jnp.exp(-x))


def _rms(x, g):
    return x * lax.rsqrt(jnp.mean(x * x, axis=-1, keepdims=True) + EPS) * g


def _resident(shape):
    nd = len(shape)
    return pl.BlockSpec(shape, lambda *_: (0,) * nd, pipeline_mode=pl.Buffered(1))


def _params(sem):
    return pltpu.CompilerParams(dimension_semantics=sem, vmem_limit_bytes=VMEM_LIMIT)


def _ffn_kernel(*refs, final, proj):
    if proj:
        x_ref, a_ref, b_ref, wo_ref, g_ref, wg_ref, wu_ref, wd_ref, gf_ref, o_ref = refs
        x = x_ref[...] + _nn(a_ref[...], wo_ref[:NSA_WIDTH, :]) + _nn(b_ref[...], wo_ref[NSA_WIDTH:, :])
    else:
        x_ref, g_ref, wg_ref, wu_ref, wd_ref, gf_ref, o_ref = refs
        x = x_ref[...]
    xn = _rms(x, g_ref[...]).astype(BF16)
    acc = jnp.zeros(x.shape, F32)
    for k in range(D_FF_PAD // FF_CHUNK):
        sl = slice(k * FF_CHUNK, (k + 1) * FF_CHUNK)
        gate = _nn(xn, wg_ref[:, sl])
        up = _nn(xn, wu_ref[:, sl])
        h = (gate * _sigmoid(gate) * up).astype(BF16)
        acc = acc + _nn(h, wd_ref[sl, :])
    y = x + 0.5 * acc
    if final:
        y = _rms(y, gf_ref[...])
    o_ref[...] = y


def _ffn_weights(w_gu, w_down):
    pad = D_FF_PAD - D_FF
    wg = jnp.pad(w_gu[..., :D_FF], ((0, 0), (0, 0), (0, pad))).astype(BF16)
    wu = jnp.pad(w_gu[..., D_FF:], ((0, 0), (0, 0), (0, pad))).astype(BF16)
    wd = jnp.pad(w_down, ((0, 0), (0, pad), (0, 0))).astype(BF16)
    return wg, wu, wd


def _ffn(h, norm_g, wg, wu, wd, final_g, final, proj=None):
    t = h.shape[0]
    row = lambda w: pl.BlockSpec((ROW_TILE, w), lambda i: (i, 0))
    weights = [_resident((1, D_MODEL)), _resident(wg.shape), _resident(wu.shape), _resident(wd.shape),
               _resident((1, D_MODEL))]
    operands = [norm_g.reshape(1, -1), wg, wu, wd, final_g.reshape(1, -1)]
    if proj is None:
        in_specs, args = [row(D_MODEL)] + weights, [h] + operands
    else:
        o_nsa, o_hg, w_out = proj
        in_specs = [row(D_MODEL), row(NSA_WIDTH), row(HG_WIDTH), _resident(w_out.shape)] + weights
        args = [h, o_nsa, o_hg, w_out] + operands
    return pl.pallas_call(
        functools.partial(_ffn_kernel, final=final, proj=proj is not None),
        grid=(t // ROW_TILE,),
        in_specs=in_specs,
        out_specs=row(D_MODEL),
        out_shape=jax.ShapeDtypeStruct((t, D_MODEL), F32),
        compiler_params=_params(("parallel",)),
        name="ffn",
    )(*args)


def _inproj_kernel(x_ref, g_ref, w_ref, qfeat_ref, *o_refs, seq):
    xn = _rms(x_ref[...], g_ref[...]).astype(BF16)
    rows = x_ref.shape[0]
    kpos = (pl.program_id(0) % (seq // rows)) * rows + lax.broadcasted_iota(jnp.int32, (rows, LANES), 0)
    lane = lax.broadcasted_iota(jnp.int32, (rows, LANES), 1)
    blk = kpos // SEL_BLOCK
    digit = lambda f0: jnp.where(lane == f0, blk.astype(F32),
                                 jnp.where(lane == f0 + 1, (kpos % SEL_BLOCK).astype(F32), 0.0))
    halves = [NSA_HD * (1 - g) for g in range(NSA_KV_GROUPS)]
    kfeat = jnp.concatenate([digit(f0) for f0 in halves], axis=1)
    is_pen = lambda f0: (lane == f0 + 1 + blk) & (blk >= 1) & (blk <= seq // SEL_BLOCK - 2)
    kpen = jnp.concatenate([jnp.where(is_pen(f0), -SEL_PENALTY, 0.0) for f0 in halves], axis=1)
    lower = lane < NSA_HD
    swap = lambda a: pltpu.roll(a, NSA_HD, axis=1)
    small = [i for i, seg in enumerate(SEGMENTS) if seg[3] == LANES]
    assert small == list(range(small[0], small[-1] + 1))
    y_small = _nn(xn, w_ref[:, SEG_OFFSETS[small[0]]:SEG_OFFSETS[small[-1] + 1]])
    for i, (o_ref, (name, _, dtype, cols)) in enumerate(zip(o_refs, SEGMENTS, strict=True)):
        if i in small:
            y = y_small[:, (i - small[0]) * LANES:(i - small[0] + 1) * LANES]
        else:
            y = _nn(xn, w_ref[:, SEG_OFFSETS[i]:SEG_OFFSETS[i] + cols])
        if name == "q":
            blocks = []
            for h in range(NSA_HEADS):
                pair = y[:, (h // 2) * LANES:(h // 2 + 1) * LANES]
                g = h // HEADS_PER_GROUP
                data = pair if h % 2 == g else swap(pair)
                feat = qfeat_ref[:, h * LANES:(h + 1) * LANES]
                blocks.append(jnp.where(lower == (g == 0), data, feat))
            y = jnp.concatenate(blocks, axis=1)
        elif name in ("ks", "kw"):
            feat = kfeat + kpen if name == "ks" else kfeat
            y = jnp.concatenate([jnp.where(lower == (g == 0), y, feat[:, g * LANES:(g + 1) * LANES])
                                 for g in range(NSA_KV_GROUPS)], axis=1)
        elif name in ("vs", "vw"):
            y = jnp.concatenate([jnp.where(lower, y if g == 0 else swap(y), 1.0)
                                 for g in range(NSA_KV_GROUPS)], axis=1)
        o_ref[...] = y.astype(dtype)


def _q_features():
    feat = np.zeros((1, NSA_HEADS * LANES), np.float32)
    for h in range(NSA_HEADS):
        f0 = h * LANES + NSA_HD * (1 - h // HEADS_PER_GROUP)
        feat[0, f0] = SEL_BLOCK * ALIBI_SLOPES[h]
        feat[0, f0 + 1] = ALIBI_SLOPES[h]
    return jnp.asarray(feat)


def _build_w_in(w_in):
    sizes = (NSA_WIDTH, KV_W, KV_W, KV_W, KV_W, KV_W, KV_W, NSA_HEADS * 3,
             HG_WIDTH, HG_WIDTH, HG_WIDTH, HG_WIDTH)
    splits = [int(v) for v in np.cumsum(sizes)[:-1]]
    w_in = w_in.astype(BF16)
    wq, wkc, wvc, wks, wvs, wkw, wvw, wgl, whq, whf, whi, whg = jnp.split(w_in, splits, axis=-1)
    lead = w_in.shape[:-1]
    gl = jnp.swapaxes(wgl.reshape(lead + (NSA_HEADS, 3)), -1, -2).reshape(lead + (3 * NSA_HEADS,))
    gl = jnp.pad(gl, [(0, 0)] * len(lead) + [(0, LANES - 3 * NSA_HEADS)])
    cols = [wq * NSA_HD ** -0.5, wks, wvs, wkw, wvw, wkc, wvc, gl, whq, whf, whi, whg]
    return jnp.concatenate(cols, axis=-1)


def _inproj(h, norm_g, w_ext, seq):
    t = h.shape[0]
    assert seq % ROW_TILE == 0
    row = lambda w: pl.BlockSpec((ROW_TILE, w), lambda i: (i, 0))
    return pl.pallas_call(
        functools.partial(_inproj_kernel, seq=seq),
        grid=(t // ROW_TILE,),
        in_specs=[row(D_MODEL), _resident((1, D_MODEL)), _resident(w_ext.shape),
                  _resident((1, NSA_HEADS * LANES))],
        out_specs=[row(w) for _, w, _, _ in SEGMENTS],
        out_shape=[jax.ShapeDtypeStruct((t, w), dt) for _, w, dt, _ in SEGMENTS],
        compiler_params=_params(("parallel",)),
        name="inproj",
    )(h, norm_g.reshape(1, -1), w_ext, _q_features())


def _gelu_tanh(x):
    return 0.5 * x * (1.0 + jnp.tanh(0.7978845608028654 * (x + 0.044715 * (x * x * x))))


def _compress_kernel(kv_ref, pos_ref, w1_ref, w2_ref, o_ref):
    nbp = o_ref.shape[1]
    hid = NSA_KV_GROUPS * CMP_HID
    ha = jnp.zeros((nbp, hid), F32)
    hb = jnp.zeros((nbp, hid), F32)
    for l in range(CMP_STRIDE):
        x = kv_ref[0, pl.ds(l, nbp, stride=CMP_STRIDE), :]
        ha = ha + _nn((x + pos_ref[l:l + 1, :]).astype(BF16), w1_ref[l])
        hb = hb + _nn((x + pos_ref[CMP_STRIDE + l:CMP_STRIDE + l + 1, :]).astype(BF16), w1_ref[CMP_STRIDE + l])
    act = _gelu_tanh(ha + pltpu.roll(hb, nbp - 1, axis=0)).astype(BF16)
    out = jnp.zeros(o_ref.shape[1:], F32)
    for g in range(NSA_KV_GROUPS):
        out = out + _nn(act[:, g * CMP_HID:(g + 1) * CMP_HID], w2_ref[g])
    o_ref[0] = out


def _compress_weights(pos, w1, w2, reps):
    layers = w1.shape[0]
    w1l = w1.astype(BF16).reshape(layers, CMP_LEN, NSA_HD, CMP_HID)
    z1 = jnp.zeros_like(w1l)
    w1p = jnp.concatenate([jnp.concatenate([w1l, z1], axis=3), jnp.concatenate([z1, w1l], axis=3)], axis=2)
    pos2 = jnp.concatenate([pos] * NSA_KV_GROUPS, axis=2)
    w2 = w2.astype(BF16)
    zero = jnp.zeros_like(w2)
    w2p = jnp.stack([jnp.concatenate([w2 if r == g else zero for r in reps], axis=2)
                     for g in range(NSA_KV_GROUPS)], axis=1)
    return pos2, w1p, w2p


def _compress(kv, pos2, w1p, w2p):
    b, s, _ = kv.shape
    nbp = s // CMP_STRIDE
    width = w2p.shape[-1]
    return pl.pallas_call(
        _compress_kernel,
        grid=(b,),
        in_specs=[pl.BlockSpec((1, s, KV_W), lambda i: (i, 0, 0)),
                  _resident(pos2.shape), _resident(w1p.shape), _resident(w2p.shape)],
        out_specs=pl.BlockSpec((1, nbp, width), lambda i: (i, 0, 0)),
        out_shape=jax.ShapeDtypeStruct((b, nbp, width), F32),
        compiler_params=_params(("parallel",)),
        name="compress",
    )(kv, pos2, w1p, w2p)


def _pair_blocks(g, jj):
    lower = slice(0, LANES) if g == 0 else slice(LANES, 2 * LANES)
    upper = slice(LANES, 2 * LANES) if g == 0 else slice(0, LANES)
    return lower, upper


def _cmp_kernel(q_ref, kc_ref, vc2_ref, mt_ref, ocmp_ref, sel_ref, picks_ref, *, tq, ns, n_sel):
    nbp = kc_ref.shape[1]
    q0 = pl.program_id(1) * tq
    pos = q0 + lax.broadcasted_iota(jnp.int32, (tq, nbp), 0)
    blk_end = lax.broadcasted_iota(jnp.int32, (tq, nbp), 1) * CMP_STRIDE + (CMP_LEN - 1)
    valid = blk_end <= pos
    row_ok = (q0 + lax.broadcasted_iota(jnp.int32, (tq, 1), 0)) >= CMP_LEN - 1
    kc = kc_ref[0].astype(BF16)
    vc2 = vc2_ref[0].astype(BF16)
    lane_c = lax.broadcasted_iota(jnp.int32, (nbp, LANES), 1)
    c_idx = lax.broadcasted_iota(jnp.int32, (nbp, LANES), 0)
    per = SEL_BLOCK // CMP_STRIDE
    feat_a = (c_idx // per - q0 // SEL_BLOCK).astype(F32)
    feat_b = ((c_idx % per) * CMP_STRIDE + (CMP_LEN - 1)).astype(F32)
    lane_lo = lax.broadcasted_iota(jnp.int32, (tq, LANES), 1) < NSA_HD
    blk_f = lax.broadcasted_iota(jnp.int32, (ns, tq), 0).astype(F32)
    blk = lax.broadcasted_iota(jnp.int32, (ns, tq), 0)
    pos_t = q0 + lax.broadcasted_iota(jnp.int32, (ns, tq), 1)
    cur = pos_t // SEL_BLOCK
    forced = (blk == 0) | (blk == cur) | (blk == cur - 1)
    causal = blk * SEL_BLOCK <= pos_t
    sel_t, picks = [], []
    for g in range(NSA_KV_GROUPS):
        f0 = NSA_HD * (1 - g)
        feat = jnp.where(lane_c == f0, feat_a, jnp.where(lane_c == f0 + 1, feat_b, 0.0))
        kc_g = jnp.where((lane_c // NSA_HD) == g, kc, feat.astype(BF16))
        imp = jnp.zeros((tq, nbp), F32)
        acc = []
        for hh in range(HEADS_PER_GROUP):
            h = g * HEADS_PER_GROUP + hh
            s = _nt(q_ref[0, :, h * LANES:(h + 1) * LANES], kc_g)
            s = jnp.where(valid, s, NEG)
            e = jnp.exp(s - jnp.max(s, axis=-1, keepdims=True))
            inv = jnp.where(row_ok, 1.0 / jnp.sum(e, axis=-1, keepdims=True), 0.0)
            p = e * inv
            imp = imp + p
            acc.append(_nn(p.astype(BF16), vc2))
        for jj in range(HEADS_PER_GROUP // 2):
            lower, upper = _pair_blocks(g, jj)
            blk_out = jnp.where(lane_lo, acc[2 * jj][:, lower], acc[2 * jj + 1][:, upper])
            c0 = (g * HEADS_PER_GROUP + 2 * jj) * NSA_HD
            ocmp_ref[0, :, c0:c0 + LANES] = blk_out
        p_slc = sum(_nt(mt_ref[...], part) for part in _split3(imp))
        taken = -jnp.inf
        score = jnp.where(forced, taken, jnp.where(causal, p_slc, NEG))
        for _ in range(n_sel - 3):
            top = jnp.max(score, axis=0, keepdims=True)
            first = jnp.min(jnp.where(score == top, blk_f, float(ns)), axis=0, keepdims=True)
            score = jnp.where(blk_f == first, taken, score)
        picked = jnp.where(score == taken, 1.0, 0.0).astype(BF16)
        picks.append(_nt(jnp.ones((8, tq), BF16), picked))
        pen = jnp.where(score == taken, 0.0, SEL_PENALTY)
        pen = jnp.where((blk >= 2) & (blk <= ns - 1), pltpu.roll(pen, 1, axis=0), 0.0)
        if ns < NSA_HD:
            pen = jnp.concatenate([pen, jnp.zeros((NSA_HD - ns, tq), F32)], axis=0)
        zero = jnp.zeros((NSA_HD, tq), F32)
        sel_t += [zero, pen] if g == 0 else [pen, zero]
    sel_ref[0] = jnp.concatenate(sel_t, axis=0).T.astype(BF16)
    picks_ref[0, 0] = jnp.concatenate(picks, axis=1)


def _importance_matrix(nbp, ns):
    per = SEL_BLOCK // CMP_STRIDE
    m = np.zeros((ns, nbp), np.float32)
    for n in range(ns):
        for c in range(per * n, per * (n + 1)):
            for cc in (c - 1, c):
                if 0 <= cc < nbp - 1:
                    m[n, cc] += 1.0
    return jnp.asarray(m, BF16)


def _cmp_attention(q, kc, vc2, b, s):
    nbp = s // CMP_STRIDE
    ns = s // SEL_BLOCK
    tq = ATT_TQ
    assert ns <= NSA_HD
    assert min(SEL_TOPK, ns) >= 3
    q3 = q.reshape(b, s, NSA_HEADS * LANES)
    kern = functools.partial(_cmp_kernel, tq=tq, ns=ns, n_sel=min(SEL_TOPK, ns))
    return pl.pallas_call(
        kern,
        grid=(b, s // tq),
        in_specs=[pl.BlockSpec((1, tq, NSA_HEADS * LANES), lambda i, j: (i, j, 0)),
                  pl.BlockSpec((1, nbp, KV_W), lambda i, j: (i, 0, 0)),
                  pl.BlockSpec((1, nbp, 2 * KV_W), lambda i, j: (i, 0, 0)),
                  _resident((ns, nbp))],
        out_specs=[pl.BlockSpec((1, tq, NSA_WIDTH), lambda i, j: (i, j, 0)),
                   pl.BlockSpec((1, tq, NSA_KV_GROUPS * LANES), lambda i, j: (i, j, 0)),
                   pl.BlockSpec((1, 1, 8, NSA_KV_GROUPS * ns), lambda i, j: (i, j, 0, 0))],
        out_shape=[jax.ShapeDtypeStruct((b, s, NSA_WIDTH), F32),
                   jax.ShapeDtypeStruct((b, s, NSA_KV_GROUPS * LANES), BF16),
                   jax.ShapeDtypeStruct((b, s // tq, 8, NSA_KV_GROUPS * ns), F32)],
        compiler_params=_params(("parallel", "parallel")),
        name="cmp_topk",
    )(q3, kc, vc2, _importance_matrix(nbp, ns))


def _attn_kernel(tiles_ref, ntiles_ref, q_ref, ks_ref, vs_ref, kw_ref, vw_ref, penq_ref, gl_ref, gexp_ref,
                 ocmp_ref, o_ref, qa_scr, s_scr, p_scr, al_scr, m_scr, acc_scr, *, tq, tk):
    i = pl.program_id(1)
    rows_of = lambda hh: slice(hh * tq, (hh + 1) * tq)
    lane_lo = lax.broadcasted_iota(jnp.int32, (tq, LANES), 1) < NSA_HD
    rel = (lax.broadcasted_iota(jnp.int32, (tq, tk), 1)
           - lax.broadcasted_iota(jnp.int32, (tq, tk), 0)).astype(F32)

    for hh in range(HEADS_PER_GROUP):
        qa_scr[rows_of(hh)] = q_ref[0, :, hh * LANES:(hh + 1) * LANES] + penq_ref[0]

    def scores(k_ref, j, slot):
        s_scr[slot] = _nt(qa_scr[...], k_ref[0, pl.ds(pl.multiple_of(j * tk, tk), tk), :])

    def probs(br, j, mode, slot, first, exists=None):
        for hh in range(HEADS_PER_GROUP):
            r = rows_of(hh)
            s = s_scr[slot, r]
            if mode == "causal":
                s = jnp.where(rel <= 0.0, s, NEG)
            elif mode == "lower":
                s = jnp.where(rel + (j * tk - i * tq).astype(F32) > -float(WINDOW), s, NEG)
            if exists is not None:
                s = jnp.where(exists, s, NEG)
            m_cur = jnp.max(s, axis=-1, keepdims=True)
            if first:
                m_new = jnp.broadcast_to(m_cur, (tq, LANES))
            else:
                m_prev = m_scr[br, r]
                m_new = jnp.maximum(m_prev, m_cur)
                al_scr[slot, r] = jnp.exp(m_prev - m_new)
            m_scr[br, r] = m_new
            p_scr[slot, r] = jnp.exp(s - jnp.concatenate([m_new] * (tk // LANES), axis=1)).astype(BF16)

    def accumulate(br, v_ref, j, slot, first):
        pv = _nn(p_scr[slot], v_ref[0, pl.ds(pl.multiple_of(j * tk, tk), tk), :])
        for hh in range(HEADS_PER_GROUP):
            r = rows_of(hh)
            if first:
                acc_scr[br, r] = pv[r]
            else:
                acc_scr[br, r] = al_scr[slot, r] * acc_scr[br, r] + pv[r]

    step = (pl.program_id(0) * pl.num_programs(1) + i) * pl.num_programs(2) + pl.program_id(2)
    n_tiles = ntiles_ref[step]
    last = jnp.maximum(n_tiles - 1, 0)
    tile_of = lambda n: tiles_ref[step * pl.num_programs(1) + jnp.minimum(n - 1, last)]
    scores(ks_ref, i, 0)
    scores(ks_ref, tile_of(1), 1)
    probs(0, i, "causal", 0, True)
    scores(ks_ref, tile_of(2), 0)
    probs(0, tile_of(1), None, 1, False)
    accumulate(0, vs_ref, i, 0, True)

    def sel_step(n, parity):
        scores(ks_ref, tile_of(n + 2), parity)
        probs(0, tile_of(n + 1), None, 1 - parity, False)
        accumulate(0, vs_ref, tile_of(n), parity, False)

    def sel_two_steps(t, carry):
        sel_step(2 * t + 1, 1)
        sel_step(2 * t + 2, 0)
        return carry

    lax.fori_loop(0, n_tiles // 2, sel_two_steps, 0)

    @pl.when(n_tiles % 2 == 1)
    def _():
        sel_step(n_tiles, 1)

    n_back = WINDOW // tk
    back = [(jnp.maximum(i - d, 0), "lower" if d == n_back else None, i >= d) for d in range(1, n_back + 1)]
    scores(kw_ref, i, 0)
    scores(kw_ref, back[0][0], 1)
    probs(1, i, "causal", 0, True)
    accumulate(1, vw_ref, i, 0, True)
    for d, (j, mode, exists) in enumerate(back, start=1):
        if d < n_back:
            scores(kw_ref, back[d][0], (d + 1) % 2)
        probs(1, j, mode, d % 2, False, exists)
        accumulate(1, vw_ref, j, d % 2, False)

    gexp = sum(_nn(part, gexp_ref[0]) for part in _split2(_sigmoid(gl_ref[0])))
    n_pairs = HEADS_PER_GROUP // 2
    gate_blk = lambda br, jj: gexp[:, (br * n_pairs + jj) * LANES:(br * n_pairs + jj + 1) * LANES]
    for jj in range(n_pairs):
        cols = slice(jj * LANES, (jj + 1) * LANES)
        blk = gate_blk(0, jj) * ocmp_ref[0, :, cols]
        for br in range(2):
            even = acc_scr[br, rows_of(2 * jj)]
            odd = acc_scr[br, rows_of(2 * jj + 1)]
            low = even / pltpu.roll(even, NSA_HD, axis=1)
            up = pltpu.roll(odd, NSA_HD, axis=1) / odd
            blk = blk + gate_blk(br + 1, jj) * jnp.where(lane_lo, low, up)
        o_ref[0, :, cols] = blk.astype(o_ref.dtype)


def _gate_expansion():
    n_pairs = HEADS_PER_GROUP // 2
    r = np.zeros((NSA_KV_GROUPS, LANES, 3 * n_pairs * LANES), np.float32)
    for g in range(NSA_KV_GROUPS):
        for br in range(3):
            for jj in range(n_pairs):
                for odd in range(2):
                    src = br * NSA_HEADS + g * HEADS_PER_GROUP + 2 * jj + odd
                    dst = (br * n_pairs + jj) * LANES + odd * NSA_HD
                    r[g, src, dst:dst + NSA_HD] = 1.0
    return jnp.asarray(r, BF16)


def _picked_tiles(picks, b, s):
    ns, nq = s // SEL_BLOCK, s // ATT_TQ
    per_tile = ATT_TK // SEL_BLOCK
    per_block = picks[:, :, 0, :].reshape(b, nq, NSA_KV_GROUPS, ns // per_tile, per_tile)
    j = jnp.arange(ns // per_tile, dtype=jnp.int32)
    i = jnp.arange(nq, dtype=jnp.int32)[None, :, None, None]
    active = ((per_block.sum(-1) > 0) | (j == 0)) & (j < i)
    slot = jnp.cumsum(active, axis=-1) - 1
    hit = active[..., None, :] & (slot[..., None, :] == j[:, None])
    tiles = jnp.sum(jnp.where(hit, j, 0), axis=-1).astype(jnp.int32)
    return tiles.reshape(-1), active.sum(-1).astype(jnp.int32).reshape(-1)


def _attention(q, ks, vs, kw, vw, pen, picks, gl, ocmp, b, s):
    tq, tk = ATT_TQ, ATT_TK
    assert tq == tk and WINDOW % tk == 0 and WINDOW // tk <= 2 and tq % SEL_BLOCK == 0
    gw = HEADS_PER_GROUP * LANES
    ow = HEADS_PER_GROUP * NSA_HD
    r3 = lambda a: a.reshape(b, s, a.shape[-1])
    tile = lambda w: pl.BlockSpec((1, tq, w), lambda i, j, g, *_: (i, j, 0))
    gtile = lambda w: pl.BlockSpec((1, tq, w), lambda i, j, g, *_: (i, j, g))
    gfull = pl.BlockSpec((1, s, LANES), lambda i, j, g, *_: (i, 0, g))
    gexp = _gate_expansion()
    rows = HEADS_PER_GROUP * tq
    kern = functools.partial(_attn_kernel, tq=tq, tk=tk)
    tiles, n_tiles = _picked_tiles(picks, b, s)
    grid_spec = pltpu.PrefetchScalarGridSpec(
        num_scalar_prefetch=2,
        grid=(b, s // tq, NSA_KV_GROUPS),
        in_specs=[gtile(gw), gfull, gfull, gfull, gfull, gtile(LANES), tile(LANES),
                  pl.BlockSpec((1,) + gexp.shape[1:], lambda i, j, g, *_: (g, 0, 0)), gtile(ow)],
        out_specs=gtile(ow),
        scratch_shapes=[pltpu.VMEM((rows, LANES), BF16), pltpu.VMEM((2, rows, tk), F32),
                        pltpu.VMEM((2, rows, tk), BF16), pltpu.VMEM((2, rows, LANES), F32),
                        pltpu.VMEM((2, rows, LANES), F32), pltpu.VMEM((2, rows, LANES), F32)])
    return pl.pallas_call(
        kern,
        grid_spec=grid_spec,
        out_shape=jax.ShapeDtypeStruct((b, s, NSA_WIDTH), BF16),
        compiler_params=_params(("parallel", "parallel", "parallel")),
        name="sel_win_attention",
    )(tiles, n_tiles, r3(q), r3(ks), r3(vs), r3(kw), r3(vw), pen, r3(gl), gexp, ocmp)


def _hgrn_kernel(hq_ref, hf_ref, hi_ref, hg_ref, lbraw_ref, onorm_ref, o_ref, st_scr, *, layer, n_chunks):
    c = HG_CHUNK

    @pl.when(pl.program_id(1) == 0)
    def _():
        st_scr[...] = jnp.zeros(st_scr.shape, F32)

    raw = lbraw_ref[...]
    ex = jnp.exp(raw - jnp.max(raw, axis=0, keepdims=True))
    sm = ex / jnp.sum(ex, axis=0, keepdims=True)
    lb_all = jnp.zeros((1, raw.shape[1]), F32)
    for l in range(1, layer + 1):
        lb_all = lb_all + sm[l:l + 1, :]

    t_idx = lax.broadcasted_iota(jnp.int32, (c, HG_DK), 0)
    sub = lax.broadcasted_iota(jnp.int32, (8, HG_DK), 0)
    ti = lax.broadcasted_iota(jnp.int32, (c, c), 0)
    si = lax.broadcasted_iota(jnp.int32, (c, c), 1)
    tril = jnp.where(si <= ti, 1.0, 0.0).astype(BF16)
    onorm = onorm_ref[...]
    levels = (32, 16, 8, 4, 2, 1)
    split_bit = ti ^ si
    pair_mask = {m: jnp.where((ti > si) & (split_bit >= m) & (split_bit < 2 * m), 1.0, 0.0) for m in levels}
    second_half = {m: (t_idx & m) != 0 for m in levels}
    sign = {m: jnp.where(second_half[m], 1.0, -1.0) for m in levels}

    def ref_rows(b, m):
        row = lambda r, n: jnp.broadcast_to(b[r:r + 1, :], (n, HG_DK))
        if m >= 4:
            return jnp.concatenate([row(s0 + m - 1, 2 * m) for s0 in range(0, c, 2 * m)], axis=0)
        return jnp.concatenate([jnp.where(sub < 4, row(s0 + 1, 8), row(s0 + 5, 8)) for s0 in range(0, c, 8)],
                               axis=0)

    def chunk(ci, carry):
        rows = pl.ds(pl.multiple_of(ci * c, c), c)
        for h in range(HG_HEADS):
            lanes = slice(h * HG_DK, (h + 1) * HG_DK)
            lb = lb_all[:, lanes]
            z = hf_ref[rows, lanes]
            hq = hq_ref[rows, lanes]
            q = hq * _sigmoid(hq)
            ez = jnp.exp(-jnp.abs(z))
            big = 1.0 / (1.0 + ez)
            small = ez * big
            f = jnp.maximum(lb + (1.0 - lb) * jnp.where(z >= 0.0, big, small), F_FLOOR)
            logf = jnp.log2(f)
            k = (1.0 - lb) * jnp.where(z >= 0.0, small, big)
            v = hi_ref[rows, lanes].astype(BF16)
            bcum = sum(_nn(tril, part) for part in _split3(logf))

            a = jnp.where(ti == si, jnp.sum(q * k, axis=-1, keepdims=True), 0.0)
            for m in levels:
                if m == 1:
                    r = jnp.where(second_half[m], q * f, k)
                else:
                    w = jnp.exp2((bcum - ref_rows(bcum, m)) * sign[m])
                    r = jnp.where(second_half[m], q, k) * w
                r = r.astype(BF16)
                a = a + _nt(r, r) * pair_mask[m]

            st = st_scr[h]
            o = _nn(a.astype(BF16), v) + _nt((q * jnp.exp2(bcum)).astype(BF16), st.astype(BF16))
            b_last = bcum[c - 1:c, :]
            kd = (k * jnp.exp2(b_last - bcum)).astype(BF16)
            st_scr[h] = jnp.exp2(b_last) * st + _tn(v, kd)

            o = o * lax.rsqrt(jnp.mean(o * o, axis=-1, keepdims=True) + EPS) * onorm
            gate = hg_ref[rows, lanes]
            o_ref[rows, lanes] = (o * (gate * _sigmoid(gate))).astype(o_ref.dtype)
        return carry

    lax.fori_loop(0, n_chunks, chunk, 0, unroll=4)


def _hgrn(hg4, lb_raw, out_norm, layer, b, s):
    tt = HG_TOKENS
    per = s // tt
    col = lambda k: pl.BlockSpec((tt, HG_WIDTH), lambda i, j, k=k: (i * per + j, k))
    kern = functools.partial(_hgrn_kernel, layer=layer, n_chunks=tt // HG_CHUNK)
    return pl.pallas_call(
        kern,
        grid=(b, per),
        in_specs=[col(0), col(1), col(2), col(3), _resident(lb_raw.shape), _resident((1, HG_DV))],
        out_specs=pl.BlockSpec((tt, HG_WIDTH), lambda i, j: (i * per + j, 0)),
        out_shape=jax.ShapeDtypeStruct((b * s, HG_WIDTH), BF16),
        scratch_shapes=[pltpu.VMEM((HG_HEADS, HG_DV, HG_DK), F32)],
        compiler_params=_params(("parallel", "arbitrary")),
        name="hgrn2",
    )(hg4, hg4, hg4, hg4, lb_raw, out_norm.reshape(1, -1))


def _mixer(h, l, b, s, mix_norm, w_ext, cmp_k, cmp_v, hgrn_lower_bound, hgrn_out_norm):
    q, ks, vs, kw, vw, kc_in, vc_in, gl, hg4 = _inproj(h, mix_norm[l], w_ext[l], s)
    kc = _compress(kc_in.reshape(b, s, KV_W), *(w[l] for w in cmp_k))
    vc2 = _compress(vc_in.reshape(b, s, KV_W), *(w[l] for w in cmp_v))
    ocmp, sel, picks = _cmp_attention(q, kc, vc2, b, s)
    o_nsa = _attention(q, ks, vs, kw, vw, sel, picks, gl, ocmp, b, s)
    o_hg = _hgrn(hg4, hgrn_lower_bound, hgrn_out_norm[l], l, b, s)
    return o_nsa.reshape(b * s, NSA_WIDTH), o_hg


def kernel(x, ffn1_norm, ffn1_w_gu, ffn1_w_down, mix_norm, w_in, cmp_pos_k, cmp_pos_v, cmp_k_w1, cmp_k_w2, cmp_v_w1, cmp_v_w2, hgrn_lower_bound, hgrn_out_norm, w_out, ffn2_norm, ffn2_w_gu, ffn2_w_down, final_norm):
    b, s, d = x.shape
    depth = ffn1_norm.shape[0]
    ffn1_w = _ffn_weights(ffn1_w_gu, ffn1_w_down)
    ffn2_w = _ffn_weights(ffn2_w_gu, ffn2_w_down)
    w_out = w_out.astype(BF16)
    w_ext = _build_w_in(w_in)
    cmp_k = _compress_weights(cmp_pos_k, cmp_k_w1, cmp_k_w2, (0, 1))
    cmp_v = _compress_weights(cmp_pos_v, cmp_v_w1, cmp_v_w2, (0, 1, 1, 0))
    h = x.reshape(b * s, d)
    for l in range(depth):
        h = _ffn(h, ffn1_norm[l], *(w[l] for w in ffn1_w), final_norm, False)
        o_nsa, o_hg = _mixer(h, l, b, s, mix_norm, w_ext, cmp_k, cmp_v, hgrn_lower_bound, hgrn_out_norm)
        h = _ffn(h, ffn2_norm[l], *(w[l] for w in ffn2_w), final_norm, l == depth - 1, (o_nsa, o_hg, w_out[l]))
    return h.reshape(b, s, d)
```

```python
import functools

import jax
import jax.numpy as jnp
import numpy as np
from jax import lax
from jax.experimental import pallas as pl
from jax.experimental.pallas import tpu as pltpu

F32 = jnp.float32
BF16 = jnp.bfloat16

D_MODEL = 1024
EPS = 1e-6
NEG = -1e30
FORCE = 1e6
F_FLOOR = 1e-30
NSA_HEADS = 8
NSA_KV_GROUPS = 2
HEADS_PER_GROUP = NSA_HEADS // NSA_KV_GROUPS
NSA_HD = 64
CMP_LEN = 32
CMP_STRIDE = 16
CMP_HID = 256
SEL_BLOCK = 64
SEL_TOPK = 16
WINDOW = 512
HG_HEADS = 4
HG_DK = 128
HG_DV = 128
HG_CHUNK = 64
D_FF = 2752
NSA_WIDTH = NSA_HEADS * NSA_HD
HG_WIDTH = HG_HEADS * HG_DV
KV_W = NSA_KV_GROUPS * NSA_HD

LANES = 128
D_FF_PAD = 2816
FF_CHUNK = 256
ROW_TILE = 512
ATT_TQ = 256
ATT_TK = 256
VMEM_LIMIT = 56 * 1024 * 1024

SEL_PENALTY = 2.0 ** 50
ALIBI_SLOPES = tuple(2.0 ** (-8.0 * (i + 1) / NSA_HEADS) for i in range(NSA_HEADS))

SEGMENTS = (
    ("q", NSA_HEADS * LANES, BF16, NSA_WIDTH),
    ("ks", NSA_KV_GROUPS * LANES, BF16, KV_W),
    ("vs", NSA_KV_GROUPS * LANES, BF16, KV_W),
    ("kw", NSA_KV_GROUPS * LANES, BF16, KV_W),
    ("vw", NSA_KV_GROUPS * LANES, BF16, KV_W),
    ("kc", KV_W, F32, KV_W),
    ("vc", KV_W, F32, KV_W),
    ("gl", LANES, F32, LANES),
    ("hg", HG_WIDTH, BF16, 4 * HG_WIDTH),
)
SEG_OFFSETS = tuple(int(v) for v in np.cumsum([0] + [s[3] for s in SEGMENTS]))


def _nn(a, b):
    return jnp.dot(a, b, preferred_element_type=F32)


def _nt(a, b):
    return lax.dot_general(a, b, (((1,), (1,)), ((), ())), preferred_element_type=F32)


def _tn(a, b):
    return lax.dot_general(a, b, (((0,), (0,)), ((), ())), preferred_element_type=F32)


def _split2(x):
    hi = x.astype(BF16)
    lo = (x - hi.astype(F32)).astype(BF16)
    return hi, lo


def _split3(x):
    hi = x.astype(BF16)
    r = x - hi.astype(F32)
    mid = r.astype(BF16)
    lo = (r - mid.astype(F32)).astype(BF16)
    return hi, mid, lo


def _dot3(a, b):
    ah, al = _split2(a)
    bh, bl = _split2(b)
    return _nn(ah, bh) + _nn(ah, bl) + _nn(al, bh)


def _sigmoid(x):
    return 1.0 / (1.0 + jnp.exp(-x))


def _rms(x, g):
    return x * lax.rsqrt(jnp.mean(x * x, axis=-1, keepdims=True) + EPS) * g


def _resident(shape):
    nd = len(shape)
    return pl.BlockSpec(shape, lambda *_: (0,) * nd, pipeline_mode=pl.Buffered(1))


def _params(sem):
    return pltpu.CompilerParams(dimension_semantics=sem, vmem_limit_bytes=VMEM_LIMIT)


def _ffn_kernel(*refs, final, proj):
    if proj:
        x_ref, a_ref, b_ref, wo_ref, g_ref, wg_ref, wu_ref, wd_ref, gf_ref, o_ref = refs
        x = x_ref[...] + _nn(a_ref[...], wo_ref[:NSA_WIDTH, :]) + _nn(b_ref[...], wo_ref[NSA_WIDTH:, :])
    else:
        x_ref, g_ref, wg_ref, wu_ref, wd_ref, gf_ref, o_ref = refs
        x = x_ref[...]
    xn = _rms(x, g_ref[...]).astype(BF16)
    acc = jnp.zeros(x.shape, F32)
    for k in range(D_FF_PAD // FF_CHUNK):
        sl = slice(k * FF_CHUNK, (k + 1) * FF_CHUNK)
        gate = _nn(xn, wg_ref[:, sl])
        up = _nn(xn, wu_ref[:, sl])
        h = (gate * _sigmoid(gate) * up).astype(BF16)
        acc = acc + _nn(h, wd_ref[sl, :])
    y = x + 0.5 * acc
    if final:
        y = _rms(y, gf_ref[...])
    o_ref[...] = y


def _ffn_weights(w_gu, w_down):
    pad = D_FF_PAD - D_FF
    wg = jnp.pad(w_gu[..., :D_FF], ((0, 0), (0, 0), (0, pad))).astype(BF16)
    wu = jnp.pad(w_gu[..., D_FF:], ((0, 0), (0, 0), (0, pad))).astype(BF16)
    wd = jnp.pad(w_down, ((0, 0), (0, pad), (0, 0))).astype(BF16)
    return wg, wu, wd


def _ffn(h, norm_g, wg, wu, wd, final_g, final, proj=None):
    t = h.shape[0]
    row = lambda w: pl.BlockSpec((ROW_TILE, w), lambda i: (i, 0))
    weights = [_resident((1, D_MODEL)), _resident(wg.shape), _resident(wu.shape), _resident(wd.shape),
               _resident((1, D_MODEL))]
    operands = [norm_g.reshape(1, -1), wg, wu, wd, final_g.reshape(1, -1)]
    if proj is None:
        in_specs, args = [row(D_MODEL)] + weights, [h] + operands
    else:
        o_nsa, o_hg, w_out = proj
        in_specs = [row(D_MODEL), row(NSA_WIDTH), row(HG_WIDTH), _resident(w_out.shape)] + weights
        args = [h, o_nsa, o_hg, w_out] + operands
    return pl.pallas_call(
        functools.partial(_ffn_kernel, final=final, proj=proj is not None),
        grid=(t // ROW_TILE,),
        in_specs=in_specs,
        out_specs=row(D_MODEL),
        out_shape=jax.ShapeDtypeStruct((t, D_MODEL), F32),
        compiler_params=_params(("parallel",)),
        name="ffn",
    )(*args)


def _inproj_kernel(x_ref, g_ref, w_ref, qfeat_ref, lbraw_ref, onorm_ref, *refs, seq, layer):
    *o_refs, st_scr = refs
    xn = _rms(x_ref[...], g_ref[...]).astype(BF16)
    rows = x_ref.shape[0]

    @pl.when(pl.program_id(0) % (seq // rows) == 0)
    def _():
        st_scr[...] = jnp.zeros(st_scr.shape, F32)

    kpos = (pl.program_id(0) % (seq // rows)) * rows + lax.broadcasted_iota(jnp.int32, (rows, LANES), 0)
    lane = lax.broadcasted_iota(jnp.int32, (rows, LANES), 1)
    blk = kpos // SEL_BLOCK
    digit = lambda f0: jnp.where(lane == f0, blk.astype(F32),
                                 jnp.where(lane == f0 + 1, (kpos % SEL_BLOCK).astype(F32), 0.0))
    halves = [NSA_HD * (1 - g) for g in range(NSA_KV_GROUPS)]
    kfeat = jnp.concatenate([digit(f0) for f0 in halves], axis=1)
    is_pen = lambda f0: (lane == f0 + 1 + blk) & (blk >= 1) & (blk <= seq // SEL_BLOCK - 2)
    kpen = jnp.concatenate([jnp.where(is_pen(f0), -SEL_PENALTY, 0.0) for f0 in halves], axis=1)
    lower = lane < NSA_HD
    swap = lambda a: pltpu.roll(a, NSA_HD, axis=1)
    small = [i for i, seg in enumerate(SEGMENTS) if seg[3] == LANES]
    assert small == list(range(small[0], small[-1] + 1))
    y_small = []

    def emit(i):
        o_ref, (name, _, dtype, cols) = o_refs[i], SEGMENTS[i]
        if i in small:
            if not y_small:
                y_small.append(_nn(xn, w_ref[:, SEG_OFFSETS[small[0]]:SEG_OFFSETS[small[-1] + 1]]))
            y = y_small[0][:, (i - small[0]) * LANES:(i - small[0] + 1) * LANES]
        else:
            y = _nn(xn, w_ref[:, SEG_OFFSETS[i]:SEG_OFFSETS[i] + cols])
        if name == "q":
            blocks = []
            for h in range(NSA_HEADS):
                pair = y[:, (h // 2) * LANES:(h // 2 + 1) * LANES]
                g = h // HEADS_PER_GROUP
                data = pair if h % 2 == g else swap(pair)
                feat = qfeat_ref[:, h * LANES:(h + 1) * LANES]
                blocks.append(jnp.where(lower == (g == 0), data, feat))
            y = jnp.concatenate(blocks, axis=1)
        elif name in ("ks", "kw"):
            feat = kfeat + kpen if name == "ks" else kfeat
            y = jnp.concatenate([jnp.where(lower == (g == 0), y, feat[:, g * LANES:(g + 1) * LANES])
                                 for g in range(NSA_KV_GROUPS)], axis=1)
        elif name in ("vs", "vw"):
            y = jnp.concatenate([jnp.where(lower, y if g == 0 else swap(y), 1.0)
                                 for g in range(NSA_KV_GROUPS)], axis=1)
        elif name == "hg":
            others = [j for j in range(len(SEGMENTS)) if j != i]
            y = _hgrn_rows(y, lbraw_ref[...], onorm_ref[...], st_scr, layer,
                           between=[functools.partial(emit, j) for j in others])
        o_ref[...] = y.astype(dtype)

    emit([name for name, *_ in SEGMENTS].index("hg"))


def _q_features():
    feat = np.zeros((1, NSA_HEADS * LANES), np.float32)
    for h in range(NSA_HEADS):
        f0 = h * LANES + NSA_HD * (1 - h // HEADS_PER_GROUP)
        feat[0, f0] = SEL_BLOCK * ALIBI_SLOPES[h]
        feat[0, f0 + 1] = ALIBI_SLOPES[h]
    return jnp.asarray(feat)


def _build_w_in(w_in):
    sizes = (NSA_WIDTH, KV_W, KV_W, KV_W, KV_W, KV_W, KV_W, NSA_HEADS * 3,
             HG_WIDTH, HG_WIDTH, HG_WIDTH, HG_WIDTH)
    splits = [int(v) for v in np.cumsum(sizes)[:-1]]
    w_in = w_in.astype(BF16)
    wq, wkc, wvc, wks, wvs, wkw, wvw, wgl, whq, whf, whi, whg = jnp.split(w_in, splits, axis=-1)
    lead = w_in.shape[:-1]
    gl = jnp.swapaxes(wgl.reshape(lead + (NSA_HEADS, 3)), -1, -2).reshape(lead + (3 * NSA_HEADS,))
    gl = jnp.pad(gl, [(0, 0)] * len(lead) + [(0, LANES - 3 * NSA_HEADS)])
    cols = [wq * NSA_HD ** -0.5, wks, wvs, wkw, wvw, wkc, wvc, gl, whq, whf, whi, whg]
    return jnp.concatenate(cols, axis=-1)


def _inproj(h, norm_g, w_ext, lb_raw, out_norm, seq, layer):
    t = h.shape[0]
    assert seq % ROW_TILE == 0 and ROW_TILE % HG_CHUNK == 0
    row = lambda w: pl.BlockSpec((ROW_TILE, w), lambda i: (i, 0))
    return pl.pallas_call(
        functools.partial(_inproj_kernel, seq=seq, layer=layer),
        grid=(t // ROW_TILE,),
        in_specs=[row(D_MODEL), _resident((1, D_MODEL)), _resident(w_ext.shape),
                  _resident((1, NSA_HEADS * LANES)), _resident(lb_raw.shape), _resident((1, HG_DV))],
        out_specs=[row(w) for _, w, _, _ in SEGMENTS],
        out_shape=[jax.ShapeDtypeStruct((t, w), dt) for _, w, dt, _ in SEGMENTS],
        scratch_shapes=[pltpu.VMEM((HG_HEADS, HG_DV, HG_DK), F32)],
        compiler_params=_params(("arbitrary",)),
        name="inproj_hgrn",
    )(h, norm_g.reshape(1, -1), w_ext, _q_features(), lb_raw, out_norm.reshape(1, -1))


def _gelu_tanh(x):
    return 0.5 * x * (1.0 + jnp.tanh(0.7978845608028654 * (x + 0.044715 * (x * x * x))))


def _compress_kernel(kv_ref, pos_ref, w1_ref, w2_ref, o_ref):
    nbp = o_ref.shape[1]
    hid = NSA_KV_GROUPS * CMP_HID
    ha = jnp.zeros((nbp, hid), F32)
    hb = jnp.zeros((nbp, hid), F32)
    for l in range(CMP_STRIDE):
        x = kv_ref[0, pl.ds(l, nbp, stride=CMP_STRIDE), :]
        ha = ha + _nn((x + pos_ref[l:l + 1, :]).astype(BF16), w1_ref[l])
        hb = hb + _nn((x + pos_ref[CMP_STRIDE + l:CMP_STRIDE + l + 1, :]).astype(BF16), w1_ref[CMP_STRIDE + l])
    act = _gelu_tanh(ha + pltpu.roll(hb, nbp - 1, axis=0)).astype(BF16)
    out = jnp.zeros(o_ref.shape[1:], F32)
    for g in range(NSA_KV_GROUPS):
        out = out + _nn(act[:, g * CMP_HID:(g + 1) * CMP_HID], w2_ref[g])
    o_ref[0] = out


def _compress_weights(pos, w1, w2, reps):
    layers = w1.shape[0]
    w1l = w1.astype(BF16).reshape(layers, CMP_LEN, NSA_HD, CMP_HID)
    z1 = jnp.zeros_like(w1l)
    w1p = jnp.concatenate([jnp.concatenate([w1l, z1], axis=3), jnp.concatenate([z1, w1l], axis=3)], axis=2)
    pos2 = jnp.concatenate([pos] * NSA_KV_GROUPS, axis=2)
    w2 = w2.astype(BF16)
    zero = jnp.zeros_like(w2)
    w2p = jnp.stack([jnp.concatenate([w2 if r == g else zero for r in reps], axis=2)
                     for g in range(NSA_KV_GROUPS)], axis=1)
    return pos2, w1p, w2p


def _compress(kv, pos2, w1p, w2p):
    b, s, _ = kv.shape
    nbp = s // CMP_STRIDE
    width = w2p.shape[-1]
    return pl.pallas_call(
        _compress_kernel,
        grid=(b,),
        in_specs=[pl.BlockSpec((1, s, KV_W), lambda i: (i, 0, 0)),
                  _resident(pos2.shape), _resident(w1p.shape), _resident(w2p.shape)],
        out_specs=pl.BlockSpec((1, nbp, width), lambda i: (i, 0, 0)),
        out_shape=jax.ShapeDtypeStruct((b, nbp, width), F32),
        compiler_params=_params(("parallel",)),
        name="compress",
    )(kv, pos2, w1p, w2p)


def _pair_blocks(g, jj):
    lower = slice(0, LANES) if g == 0 else slice(LANES, 2 * LANES)
    upper = slice(LANES, 2 * LANES) if g == 0 else slice(0, LANES)
    return lower, upper


def _cmp_kernel(q_ref, kc_ref, vc2_ref, mt_ref, ocmp_ref, sel_ref, picks_ref, *, tq, ns, n_sel):
    nbp = kc_ref.shape[1]
    q0 = pl.program_id(1) * tq
    pos = q0 + lax.broadcasted_iota(jnp.int32, (tq, nbp), 0)
    blk_end = lax.broadcasted_iota(jnp.int32, (tq, nbp), 1) * CMP_STRIDE + (CMP_LEN - 1)
    valid = blk_end <= pos
    row_ok = (q0 + lax.broadcasted_iota(jnp.int32, (tq, 1), 0)) >= CMP_LEN - 1
    kc = kc_ref[0].astype(BF16)
    vc2 = vc2_ref[0].astype(BF16)
    lane_c = lax.broadcasted_iota(jnp.int32, (nbp, LANES), 1)
    c_idx = lax.broadcasted_iota(jnp.int32, (nbp, LANES), 0)
    per = SEL_BLOCK // CMP_STRIDE
    feat_a = (c_idx // per - q0 // SEL_BLOCK).astype(F32)
    feat_b = ((c_idx % per) * CMP_STRIDE + (CMP_LEN - 1)).astype(F32)
    lane_lo = lax.broadcasted_iota(jnp.int32, (tq, LANES), 1) < NSA_HD
    blk_f = lax.broadcasted_iota(jnp.int32, (ns, tq), 0).astype(F32)
    blk = lax.broadcasted_iota(jnp.int32, (ns, tq), 0)
    pos_t = q0 + lax.broadcasted_iota(jnp.int32, (ns, tq), 1)
    cur = pos_t // SEL_BLOCK
    forced = (blk == 0) | (blk == cur) | (blk == cur - 1)
    causal = blk * SEL_BLOCK <= pos_t
    sel_t, picks = [], []
    for g in range(NSA_KV_GROUPS):
        f0 = NSA_HD * (1 - g)
        feat = jnp.where(lane_c == f0, feat_a, jnp.where(lane_c == f0 + 1, feat_b, 0.0))
        kc_g = jnp.where((lane_c // NSA_HD) == g, kc, feat.astype(BF16))
        imp = jnp.zeros((tq, nbp), F32)
        acc = []
        for hh in range(HEADS_PER_GROUP):
            h = g * HEADS_PER_GROUP + hh
            s = _nt(q_ref[0, :, h * LANES:(h + 1) * LANES], kc_g)
            s = jnp.where(valid, s, NEG)
            e = jnp.exp(s - jnp.max(s, axis=-1, keepdims=True))
            inv = jnp.where(row_ok, 1.0 / jnp.sum(e, axis=-1, keepdims=True), 0.0)
            p = e * inv
            imp = imp + p
            acc.append(_nn(p.astype(BF16), vc2))
        for jj in range(HEADS_PER_GROUP // 2):
            lower, upper = _pair_blocks(g, jj)
            blk_out = jnp.where(lane_lo, acc[2 * jj][:, lower], acc[2 * jj + 1][:, upper])
            c0 = (g * HEADS_PER_GROUP + 2 * jj) * NSA_HD
            ocmp_ref[0, :, c0:c0 + LANES] = blk_out
        p_slc = sum(_nt(mt_ref[...], part) for part in _split3(imp))
        taken = -jnp.inf
        score = jnp.where(forced, taken, jnp.where(causal, p_slc, NEG))
        for _ in range(n_sel - 3):
            top = jnp.max(score, axis=0, keepdims=True)
            first = jnp.min(jnp.where(score == top, blk_f, float(ns)), axis=0, keepdims=True)
            score = jnp.where(blk_f == first, taken, score)
        picked = jnp.where(score == taken, 1.0, 0.0).astype(BF16)
        picks.append(_nt(jnp.ones((8, tq), BF16), picked))
        pen = jnp.where(score == taken, 0.0, SEL_PENALTY)
        pen = jnp.where((blk >= 2) & (blk <= ns - 1), pltpu.roll(pen, 1, axis=0), 0.0)
        if ns < NSA_HD:
            pen = jnp.concatenate([pen, jnp.zeros((NSA_HD - ns, tq), F32)], axis=0)
        zero = jnp.zeros((NSA_HD, tq), F32)
        sel_t += [zero, pen] if g == 0 else [pen, zero]
    sel_ref[0] = jnp.concatenate(sel_t, axis=0).T.astype(BF16)
    picks_ref[0, 0] = jnp.concatenate(picks, axis=1)


def _importance_matrix(nbp, ns):
    per = SEL_BLOCK // CMP_STRIDE
    m = np.zeros((ns, nbp), np.float32)
    for n in range(ns):
        for c in range(per * n, per * (n + 1)):
            for cc in (c - 1, c):
                if 0 <= cc < nbp - 1:
                    m[n, cc] += 1.0
    return jnp.asarray(m, BF16)


def _cmp_attention(q, kc, vc2, b, s):
    nbp = s // CMP_STRIDE
    ns = s // SEL_BLOCK
    tq = ATT_TQ
    assert ns <= NSA_HD
    assert min(SEL_TOPK, ns) >= 3
    q3 = q.reshape(b, s, NSA_HEADS * LANES)
    kern = functools.partial(_cmp_kernel, tq=tq, ns=ns, n_sel=min(SEL_TOPK, ns))
    return pl.pallas_call(
        kern,
        grid=(b, s // tq),
        in_specs=[pl.BlockSpec((1, tq, NSA_HEADS * LANES), lambda i, j: (i, j, 0)),
                  pl.BlockSpec((1, nbp, KV_W), lambda i, j: (i, 0, 0)),
                  pl.BlockSpec((1, nbp, 2 * KV_W), lambda i, j: (i, 0, 0)),
                  _resident((ns, nbp))],
        out_specs=[pl.BlockSpec((1, tq, NSA_WIDTH), lambda i, j: (i, j, 0)),
                   pl.BlockSpec((1, tq, NSA_KV_GROUPS * LANES), lambda i, j: (i, j, 0)),
                   pl.BlockSpec((1, 1, 8, NSA_KV_GROUPS * ns), lambda i, j: (i, j, 0, 0))],
        out_shape=[jax.ShapeDtypeStruct((b, s, NSA_WIDTH), F32),
                   jax.ShapeDtypeStruct((b, s, NSA_KV_GROUPS * LANES), BF16),
                   jax.ShapeDtypeStruct((b, s // tq, 8, NSA_KV_GROUPS * ns), F32)],
        compiler_params=_params(("parallel", "parallel")),
        name="cmp_topk",
    )(q3, kc, vc2, _importance_matrix(nbp, ns))


def _attn_kernel(tiles_ref, ntiles_ref, q_ref, ks_ref, vs_ref, kw_ref, vw_ref, penq_ref, gl_ref, gexp_ref,
                 ocmp_ref, o_ref, qa_scr, s_scr, p_scr, al_scr, m_scr, acc_scr, *, tq, tk):
    i = pl.program_id(1)
    rows_of = lambda hh: slice(hh * tq, (hh + 1) * tq)
    lane_lo = lax.broadcasted_iota(jnp.int32, (tq, LANES), 1) < NSA_HD
    rel = (lax.broadcasted_iota(jnp.int32, (tq, tk), 1)
           - lax.broadcasted_iota(jnp.int32, (tq, tk), 0)).astype(F32)

    for hh in range(HEADS_PER_GROUP):
        qa_scr[rows_of(hh)] = q_ref[0, :, hh * LANES:(hh + 1) * LANES] + penq_ref[0]

    def scores(k_ref, j, slot):
        s_scr[slot] = _nt(qa_scr[...], k_ref[0, pl.ds(pl.multiple_of(j * tk, tk), tk), :])

    def probs(br, j, mode, slot, first, exists=None):
        for hh in range(HEADS_PER_GROUP):
            r = rows_of(hh)
            s = s_scr[slot, r]
            if mode == "causal":
                s = jnp.where(rel <= 0.0, s, NEG)
            elif mode == "lower":
                s = jnp.where(rel + (j * tk - i * tq).astype(F32) > -float(WINDOW), s, NEG)
            if exists is not None:
                s = jnp.where(exists, s, NEG)
            m_cur = jnp.max(s, axis=-1, keepdims=True)
            if first:
                m_new = jnp.broadcast_to(m_cur, (tq, LANES))
            else:
                m_prev = m_scr[br, r]
                m_new = jnp.maximum(m_prev, m_cur)
                al_scr[slot, r] = jnp.exp(m_prev - m_new)
            m_scr[br, r] = m_new
            p_scr[slot, r] = jnp.exp(s - jnp.concatenate([m_new] * (tk // LANES), axis=1)).astype(BF16)

    def accumulate(br, v_ref, j, slot, first):
        pv = _nn(p_scr[slot], v_ref[0, pl.ds(pl.multiple_of(j * tk, tk), tk), :])
        for hh in range(HEADS_PER_GROUP):
            r = rows_of(hh)
            if first:
                acc_scr[br, r] = pv[r]
            else:
                acc_scr[br, r] = al_scr[slot, r] * acc_scr[br, r] + pv[r]

    step = (pl.program_id(0) * pl.num_programs(1) + i) * pl.num_programs(2) + pl.program_id(2)
    n_tiles = ntiles_ref[step]
    last = jnp.maximum(n_tiles - 1, 0)
    tile_of = lambda n: tiles_ref[step * pl.num_programs(1) + jnp.minimum(n - 1, last)]
    scores(ks_ref, i, 0)
    scores(ks_ref, tile_of(1), 1)
    probs(0, i, "causal", 0, True)
    scores(ks_ref, tile_of(2), 0)
    probs(0, tile_of(1), None, 1, False)
    accumulate(0, vs_ref, i, 0, True)

    def sel_step(n, parity):
        scores(ks_ref, tile_of(n + 2), parity)
        probs(0, tile_of(n + 1), None, 1 - parity, False)
        accumulate(0, vs_ref, tile_of(n), parity, False)

    def sel_two_steps(t, carry):
        sel_step(2 * t + 1, 1)
        sel_step(2 * t + 2, 0)
        return carry

    lax.fori_loop(0, n_tiles // 2, sel_two_steps, 0)

    @pl.when(n_tiles % 2 == 1)
    def _():
        sel_step(n_tiles, 1)

    n_back = WINDOW // tk
    back = [(jnp.maximum(i - d, 0), "lower" if d == n_back else None, i >= d) for d in range(1, n_back + 1)]
    scores(kw_ref, i, 0)
    scores(kw_ref, back[0][0], 1)
    probs(1, i, "causal", 0, True)
    accumulate(1, vw_ref, i, 0, True)
    for d, (j, mode, exists) in enumerate(back, start=1):
        if d < n_back:
            scores(kw_ref, back[d][0], (d + 1) % 2)
        probs(1, j, mode, d % 2, False, exists)
        accumulate(1, vw_ref, j, d % 2, False)

    gexp = sum(_nn(part, gexp_ref[0]) for part in _split2(_sigmoid(gl_ref[0])))
    n_pairs = HEADS_PER_GROUP // 2
    gate_blk = lambda br, jj: gexp[:, (br * n_pairs + jj) * LANES:(br * n_pairs + jj + 1) * LANES]
    for jj in range(n_pairs):
        cols = slice(jj * LANES, (jj + 1) * LANES)
        blk = gate_blk(0, jj) * ocmp_ref[0, :, cols]
        for br in range(2):
            even = acc_scr[br, rows_of(2 * jj)]
            odd = acc_scr[br, rows_of(2 * jj + 1)]
            low = even / pltpu.roll(even, NSA_HD, axis=1)
            up = pltpu.roll(odd, NSA_HD, axis=1) / odd
            blk = blk + gate_blk(br + 1, jj) * jnp.where(lane_lo, low, up)
        o_ref[0, :, cols] = blk.astype(o_ref.dtype)


def _gate_expansion():
    n_pairs = HEADS_PER_GROUP // 2
    r = np.zeros((NSA_KV_GROUPS, LANES, 3 * n_pairs * LANES), np.float32)
    for g in range(NSA_KV_GROUPS):
        for br in range(3):
            for jj in range(n_pairs):
                for odd in range(2):
                    src = br * NSA_HEADS + g * HEADS_PER_GROUP + 2 * jj + odd
                    dst = (br * n_pairs + jj) * LANES + odd * NSA_HD
                    r[g, src, dst:dst + NSA_HD] = 1.0
    return jnp.asarray(r, BF16)


def _picked_tiles(picks, b, s):
    ns, nq = s // SEL_BLOCK, s // ATT_TQ
    per_tile = ATT_TK // SEL_BLOCK
    per_block = picks[:, :, 0, :].reshape(b, nq, NSA_KV_GROUPS, ns // per_tile, per_tile)
    j = jnp.arange(ns // per_tile, dtype=jnp.int32)
    i = jnp.arange(nq, dtype=jnp.int32)[None, :, None, None]
    active = ((per_block.sum(-1) > 0) | (j == 0)) & (j < i)
    slot = jnp.cumsum(active, axis=-1) - 1
    hit = active[..., None, :] & (slot[..., None, :] == j[:, None])
    tiles = jnp.sum(jnp.where(hit, j, 0), axis=-1).astype(jnp.int32)
    return tiles.reshape(-1), active.sum(-1).astype(jnp.int32).reshape(-1)


def _attention(q, ks, vs, kw, vw, pen, picks, gl, ocmp, b, s):
    tq, tk = ATT_TQ, ATT_TK
    assert tq == tk and WINDOW % tk == 0 and WINDOW // tk <= 2 and tq % SEL_BLOCK == 0
    gw = HEADS_PER_GROUP * LANES
    ow = HEADS_PER_GROUP * NSA_HD
    r3 = lambda a: a.reshape(b, s, a.shape[-1])
    tile = lambda w: pl.BlockSpec((1, tq, w), lambda i, j, g, *_: (i, j, 0))
    gtile = lambda w: pl.BlockSpec((1, tq, w), lambda i, j, g, *_: (i, j, g))
    gfull = pl.BlockSpec((1, s, LANES), lambda i, j, g, *_: (i, 0, g))
    gexp = _gate_expansion()
    rows = HEADS_PER_GROUP * tq
    kern = functools.partial(_attn_kernel, tq=tq, tk=tk)
    tiles, n_tiles = _picked_tiles(picks, b, s)
    grid_spec = pltpu.PrefetchScalarGridSpec(
        num_scalar_prefetch=2,
        grid=(b, s // tq, NSA_KV_GROUPS),
        in_specs=[gtile(gw), gfull, gfull, gfull, gfull, gtile(LANES), tile(LANES),
                  pl.BlockSpec((1,) + gexp.shape[1:], lambda i, j, g, *_: (g, 0, 0)), gtile(ow)],
        out_specs=gtile(ow),
        scratch_shapes=[pltpu.VMEM((rows, LANES), BF16), pltpu.VMEM((2, rows, tk), F32),
                        pltpu.VMEM((2, rows, tk), BF16), pltpu.VMEM((2, rows, LANES), F32),
                        pltpu.VMEM((2, rows, LANES), F32), pltpu.VMEM((2, rows, LANES), F32)])
    return pl.pallas_call(
        kern,
        grid_spec=grid_spec,
        out_shape=jax.ShapeDtypeStruct((b, s, NSA_WIDTH), BF16),
        compiler_params=_params(("parallel", "parallel", "parallel")),
        name="sel_win_attention",
    )(tiles, n_tiles, r3(q), r3(ks), r3(vs), r3(kw), r3(vw), pen, r3(gl), gexp, ocmp)


def _hgrn_rows(y, raw, onorm, st_scr, layer, between=()):
    between = list(between)
    c = HG_CHUNK
    ex = jnp.exp(raw - jnp.max(raw, axis=0, keepdims=True))
    sm = ex / jnp.sum(ex, axis=0, keepdims=True)
    lb_all = jnp.zeros((1, raw.shape[1]), F32)
    for l in range(1, layer + 1):
        lb_all = lb_all + sm[l:l + 1, :]

    t_idx = lax.broadcasted_iota(jnp.int32, (c, HG_DK), 0)
    sub = lax.broadcasted_iota(jnp.int32, (8, HG_DK), 0)
    ti = lax.broadcasted_iota(jnp.int32, (c, c), 0)
    si = lax.broadcasted_iota(jnp.int32, (c, c), 1)
    tril = jnp.where(si <= ti, 1.0, 0.0).astype(BF16)
    levels = (32, 16, 8, 4, 2, 1)
    split_bit = ti ^ si
    pair_mask = {m: jnp.where((ti > si) & (split_bit >= m) & (split_bit < 2 * m), 1.0, 0.0) for m in levels}
    second_half = {m: (t_idx & m) != 0 for m in levels}
    sign = {m: jnp.where(second_half[m], 1.0, -1.0) for m in levels}

    def ref_rows(b, m):
        row = lambda r, n: jnp.broadcast_to(b[r:r + 1, :], (n, HG_DK))
        if m >= 4:
            return jnp.concatenate([row(s0 + m - 1, 2 * m) for s0 in range(0, c, 2 * m)], axis=0)
        return jnp.concatenate([jnp.where(sub < 4, row(s0 + 1, 8), row(s0 + 5, 8)) for s0 in range(0, c, 8)],
                               axis=0)

    part = lambda which, rows, h: y[rows, which * HG_WIDTH + h * HG_DK:which * HG_WIDTH + (h + 1) * HG_DK]
    out = []
    for ci in range(y.shape[0] // c):
        rows = slice(ci * c, (ci + 1) * c)
        heads = []
        for h in range(HG_HEADS):
            lb = lb_all[:, h * HG_DK:(h + 1) * HG_DK]
            z = part(1, rows, h)
            hq = part(0, rows, h)
            q = hq * _sigmoid(hq)
            ez = jnp.exp(-jnp.abs(z))
            big = 1.0 / (1.0 + ez)
            small = ez * big
            f = jnp.maximum(lb + (1.0 - lb) * jnp.where(z >= 0.0, big, small), F_FLOOR)
            logf = jnp.log2(f)
            k = (1.0 - lb) * jnp.where(z >= 0.0, small, big)
            v = part(2, rows, h).astype(BF16)
            bcum = sum(_nn(tril, part) for part in _split3(logf))

            a = jnp.where(ti == si, jnp.sum(q * k, axis=-1, keepdims=True), 0.0)
            for m in levels:
                if m == 1:
                    r = jnp.where(second_half[m], q * f, k)
                else:
                    w = jnp.exp2((bcum - ref_rows(bcum, m)) * sign[m])
                    r = jnp.where(second_half[m], q, k) * w
                r = r.astype(BF16)
                a = a + _nt(r, r) * pair_mask[m]

            st = st_scr[h]
            o = _nn(a.astype(BF16), v) + _nt((q * jnp.exp2(bcum)).astype(BF16), st.astype(BF16))
            b_last = bcum[c - 1:c, :]
            kd = (k * jnp.exp2(b_last - bcum)).astype(BF16)
            st_scr[h] = jnp.exp2(b_last) * st + _tn(v, kd)

            o = o * lax.rsqrt(jnp.mean(o * o, axis=-1, keepdims=True) + EPS) * onorm
            gate = part(3, rows, h)
            heads.append((o * (gate * _sigmoid(gate))).astype(BF16))
        out.append(jnp.concatenate(heads, axis=1))
        if between:
            between.pop(0)()
    for thunk in between:
        thunk()
    return jnp.concatenate(out, axis=0)


def _mixer(h, l, b, s, mix_norm, w_ext, cmp_k, cmp_v, hgrn_lower_bound, hgrn_out_norm):
    q, ks, vs, kw, vw, kc_in, vc_in, gl, o_hg = _inproj(h, mix_norm[l], w_ext[l], hgrn_lower_bound,
                                                        hgrn_out_norm[l], s, l)
    kc = _compress(kc_in.reshape(b, s, KV_W), *(w[l] for w in cmp_k))
    vc2 = _compress(vc_in.reshape(b, s, KV_W), *(w[l] for w in cmp_v))
    ocmp, sel, picks = _cmp_attention(q, kc, vc2, b, s)
    o_nsa = _attention(q, ks, vs, kw, vw, sel, picks, gl, ocmp, b, s)
    return o_nsa.reshape(b * s, NSA_WIDTH), o_hg


def kernel(x, ffn1_norm, ffn1_w_gu, ffn1_w_down, mix_norm, w_in, cmp_pos_k, cmp_pos_v, cmp_k_w1, cmp_k_w2, cmp_v_w1, cmp_v_w2, hgrn_lower_bound, hgrn_out_norm, w_out, ffn2_norm, ffn2_w_gu, ffn2_w_down, final_norm):
    b, s, d = x.shape
    depth = ffn1_norm.shape[0]
    ffn1_w = _ffn_weights(ffn1_w_gu, ffn1_w_down)
    ffn2_w = _ffn_weights(ffn2_w_gu, ffn2_w_down)
    w_out = w_out.astype(BF16)
    w_ext = _build_w_in(w_in)
    cmp_k = _compress_weights(cmp_pos_k, cmp_k_w1, cmp_k_w2, (0, 1))
    cmp_v = _compress_weights(cmp_pos_v, cmp_v_w1, cmp_v_w2, (0, 1, 1, 0))
    h = x.reshape(b * s, d)
    for l in range(depth):
        h = _ffn(h, ffn1_norm[l], *(w[l] for w in ffn1_w), final_norm, False)
        o_nsa, o_hg = _mixer(h, l, b, s, mix_norm, w_ext, cmp_k, cmp_v, hgrn_lower_bound, hgrn_out_norm)
        h = _ffn(h, ffn2_norm[l], *(w[l] for w in ffn2_w), final_norm, l == depth - 1, (o_nsa, o_hg, w_out[l]))
    return h.reshape(b, s, d)
```

```python
import functools

import jax
import jax.numpy as jnp
import numpy as np
from jax import lax
from jax.experimental import pallas as pl
from jax.experimental.pallas import tpu as pltpu

F32 = jnp.float32
BF16 = jnp.bfloat16

D_MODEL = 1024
EPS = 1e-6
NEG = -1e30
FORCE = 1e6
F_FLOOR = 1e-30
NSA_HEADS = 8
NSA_KV_GROUPS = 2
HEADS_PER_GROUP = NSA_HEADS // NSA_KV_GROUPS
NSA_HD = 64
CMP_LEN = 32
CMP_STRIDE = 16
CMP_HID = 256
SEL_BLOCK = 64
SEL_TOPK = 16
WINDOW = 512
HG_HEADS = 4
HG_DK = 128
HG_DV = 128
HG_CHUNK = 64
D_FF = 2752
NSA_WIDTH = NSA_HEADS * NSA_HD
HG_WIDTH = HG_HEADS * HG_DV
KV_W = NSA_KV_GROUPS * NSA_HD

LANES = 128
D_FF_PAD = 2816
FF_CHUNK = 256
ROW_TILE = 512
ATT_TQ = 256
ATT_TK = 256
VMEM_LIMIT = 56 * 1024 * 1024

SEL_PENALTY = 2.0 ** 50
ALIBI_SLOPES = tuple(2.0 ** (-8.0 * (i + 1) / NSA_HEADS) for i in range(NSA_HEADS))

SEGMENTS = (
    ("q", NSA_HEADS * LANES, BF16, NSA_WIDTH),
    ("ks", NSA_KV_GROUPS * LANES, BF16, KV_W),
    ("vs", NSA_KV_GROUPS * LANES, BF16, KV_W),
    ("kw", NSA_KV_GROUPS * LANES, BF16, KV_W),
    ("vw", NSA_KV_GROUPS * LANES, BF16, KV_W),
    ("kc", KV_W, F32, KV_W),
    ("vc", KV_W, F32, KV_W),
    ("gl", LANES, F32, LANES),
    ("hg", HG_WIDTH, BF16, 4 * HG_WIDTH),
)
SEG_OFFSETS = tuple(int(v) for v in np.cumsum([0] + [s[3] for s in SEGMENTS]))


def _nn(a, b):
    return jnp.dot(a, b, preferred_element_type=F32)


def _nt(a, b):
    return lax.dot_general(a, b, (((1,), (1,)), ((), ())), preferred_element_type=F32)


def _tn(a, b):
    return lax.dot_general(a, b, (((0,), (0,)), ((), ())), preferred_element_type=F32)


def _split2(x):
    hi = x.astype(BF16)
    lo = (x - hi.astype(F32)).astype(BF16)
    return hi, lo


def _split3(x):
    hi = x.astype(BF16)
    r = x - hi.astype(F32)
    mid = r.astype(BF16)
    lo = (r - mid.astype(F32)).astype(BF16)
    return hi, mid, lo


def _dot3(a, b):
    ah, al = _split2(a)
    bh, bl = _split2(b)
    return _nn(ah, bh) + _nn(ah, bl) + _nn(al, bh)


def _sigmoid(x):
    return 1.0 / (1.0 + jnp.exp(-x))


def _rms(x, g):
    return x * lax.rsqrt(jnp.mean(x * x, axis=-1, keepdims=True) + EPS) * g


def _resident(shape):
    nd = len(shape)
    return pl.BlockSpec(shape, lambda *_: (0,) * nd, pipeline_mode=pl.Buffered(1))


def _params(sem):
    return pltpu.CompilerParams(dimension_semantics=sem, vmem_limit_bytes=VMEM_LIMIT)


def _ffn_kernel(*refs, final, proj):
    if proj:
        x_ref, a_ref, b_ref, wo_ref, g_ref, wg_ref, wu_ref, wd_ref, gf_ref, o_ref = refs
        x = x_ref[...] + _nn(a_ref[...], wo_ref[:NSA_WIDTH, :]) + _nn(b_ref[...], wo_ref[NSA_WIDTH:, :])
    else:
        x_ref, g_ref, wg_ref, wu_ref, wd_ref, gf_ref, o_ref = refs
        x = x_ref[...]
    xn = _rms(x, g_ref[...]).astype(BF16)
    acc = jnp.zeros(x.shape, F32)
    for k in range(D_FF_PAD // FF_CHUNK):
        sl = slice(k * FF_CHUNK, (k + 1) * FF_CHUNK)
        gate = _nn(xn, wg_ref[:, sl])
        up = _nn(xn, wu_ref[:, sl])
        h = (gate * _sigmoid(gate) * up).astype(BF16)
        acc = acc + _nn(h, wd_ref[sl, :])
    y = x + 0.5 * acc
    if final:
        y = _rms(y, gf_ref[...])
    o_ref[...] = y


def _ffn_weights(w_gu, w_down):
    pad = D_FF_PAD - D_FF
    wg = jnp.pad(w_gu[..., :D_FF], ((0, 0), (0, 0), (0, pad))).astype(BF16)
    wu = jnp.pad(w_gu[..., D_FF:], ((0, 0), (0, 0), (0, pad))).astype(BF16)
    wd = jnp.pad(w_down, ((0, 0), (0, pad), (0, 0))).astype(BF16)
    return wg, wu, wd


def _ffn(h, norm_g, wg, wu, wd, final_g, final, proj=None):
    t = h.shape[0]
    row = lambda w: pl.BlockSpec((ROW_TILE, w), lambda i: (i, 0))
    weights = [_resident((1, D_MODEL)), _resident(wg.shape), _resident(wu.shape), _resident(wd.shape),
               _resident((1, D_MODEL))]
    operands = [norm_g.reshape(1, -1), wg, wu, wd, final_g.reshape(1, -1)]
    if proj is None:
        in_specs, args = [row(D_MODEL)] + weights, [h] + operands
    else:
        o_nsa, o_hg, w_out = proj
        in_specs = [row(D_MODEL), row(NSA_WIDTH), row(HG_WIDTH), _resident(w_out.shape)] + weights
        args = [h, o_nsa, o_hg, w_out] + operands
    return pl.pallas_call(
        functools.partial(_ffn_kernel, final=final, proj=proj is not None),
        grid=(t // ROW_TILE,),
        in_specs=in_specs,
        out_specs=row(D_MODEL),
        out_shape=jax.ShapeDtypeStruct((t, D_MODEL), F32),
        compiler_params=_params(("parallel",)),
        name="ffn",
    )(*args)


def _inproj_kernel(x_ref, g_ref, w_ref, qfeat_ref, lbraw_ref, onorm_ref, *refs, seq, layer):
    *o_refs, st_scr = refs
    xn = _rms(x_ref[...], g_ref[...]).astype(BF16)
    rows = x_ref.shape[0]

    @pl.when(pl.program_id(0) % (seq // rows) == 0)
    def _():
        st_scr[...] = jnp.zeros(st_scr.shape, F32)

    kpos = (pl.program_id(0) % (seq // rows)) * rows + lax.broadcasted_iota(jnp.int32, (rows, LANES), 0)
    lane = lax.broadcasted_iota(jnp.int32, (rows, LANES), 1)
    blk = kpos // SEL_BLOCK
    digit = lambda f0: jnp.where(lane == f0, blk.astype(F32),
                                 jnp.where(lane == f0 + 1, (kpos % SEL_BLOCK).astype(F32), 0.0))
    halves = [NSA_HD * (1 - g) for g in range(NSA_KV_GROUPS)]
    kfeat = jnp.concatenate([digit(f0) for f0 in halves], axis=1)
    is_pen = lambda f0: (lane == f0 + 1 + blk) & (blk >= 1) & (blk <= seq // SEL_BLOCK - 2)
    kpen = jnp.concatenate([jnp.where(is_pen(f0), -SEL_PENALTY, 0.0) for f0 in halves], axis=1)
    lower = lane < NSA_HD
    swap = lambda a: pltpu.roll(a, NSA_HD, axis=1)
    small = [i for i, seg in enumerate(SEGMENTS) if seg[3] == LANES]
    assert small == list(range(small[0], small[-1] + 1))
    y_small = []

    def emit(i):
        o_ref, (name, _, dtype, cols) = o_refs[i], SEGMENTS[i]
        if i in small:
            if not y_small:
                y_small.append(_nn(xn, w_ref[:, SEG_OFFSETS[small[0]]:SEG_OFFSETS[small[-1] + 1]]))
            y = y_small[0][:, (i - small[0]) * LANES:(i - small[0] + 1) * LANES]
        else:
            y = _nn(xn, w_ref[:, SEG_OFFSETS[i]:SEG_OFFSETS[i] + cols])
        if name == "q":
            blocks = []
            for h in range(NSA_HEADS):
                pair = y[:, (h // 2) * LANES:(h // 2 + 1) * LANES]
                g = h // HEADS_PER_GROUP
                data = pair if h % 2 == g else swap(pair)
                feat = qfeat_ref[:, h * LANES:(h + 1) * LANES]
                blocks.append(jnp.where(lower == (g == 0), data, feat))
            y = jnp.concatenate(blocks, axis=1)
        elif name in ("ks", "kw"):
            feat = kfeat + kpen if name == "ks" else kfeat
            y = jnp.concatenate([jnp.where(lower == (g == 0), y, feat[:, g * LANES:(g + 1) * LANES])
                                 for g in range(NSA_KV_GROUPS)], axis=1)
        elif name in ("vs", "vw"):
            y = jnp.concatenate([jnp.where(lower, y if g == 0 else swap(y), 1.0)
                                 for g in range(NSA_KV_GROUPS)], axis=1)
        elif name == "hg":
            others = [j for j in range(len(SEGMENTS)) if j != i]
            y = _hgrn_rows(y, lbraw_ref[...], onorm_ref[...], st_scr, layer,
                           between=[functools.partial(emit, j) for j in others])
        o_ref[...] = y.astype(dtype)

    emit([name for name, *_ in SEGMENTS].index("hg"))


def _q_features():
    feat = np.zeros((1, NSA_HEADS * LANES), np.float32)
    for h in range(NSA_HEADS):
        f0 = h * LANES + NSA_HD * (1 - h // HEADS_PER_GROUP)
        feat[0, f0] = SEL_BLOCK * ALIBI_SLOPES[h]
        feat[0, f0 + 1] = ALIBI_SLOPES[h]
    return jnp.asarray(feat)


def _build_w_in(w_in):
    sizes = (NSA_WIDTH, KV_W, KV_W, KV_W, KV_W, KV_W, KV_W, NSA_HEADS * 3,
             HG_WIDTH, HG_WIDTH, HG_WIDTH, HG_WIDTH)
    splits = [int(v) for v in np.cumsum(sizes)[:-1]]
    w_in = w_in.astype(BF16)
    wq, wkc, wvc, wks, wvs, wkw, wvw, wgl, whq, whf, whi, whg = jnp.split(w_in, splits, axis=-1)
    lead = w_in.shape[:-1]
    gl = jnp.swapaxes(wgl.reshape(lead + (NSA_HEADS, 3)), -1, -2).reshape(lead + (3 * NSA_HEADS,))
    gl = jnp.pad(gl, [(0, 0)] * len(lead) + [(0, LANES - 3 * NSA_HEADS)])
    cols = [wq * NSA_HD ** -0.5, wks, wvs, wkw, wvw, wkc, wvc, gl, whq, whf, whi, whg]
    return jnp.concatenate(cols, axis=-1)


def _inproj(h, norm_g, w_ext, lb_raw, out_norm, seq, layer):
    t = h.shape[0]
    assert seq % ROW_TILE == 0 and ROW_TILE % HG_CHUNK == 0
    row = lambda w: pl.BlockSpec((ROW_TILE, w), lambda i: (i, 0))
    return pl.pallas_call(
        functools.partial(_inproj_kernel, seq=seq, layer=layer),
        grid=(t // ROW_TILE,),
        in_specs=[row(D_MODEL), _resident((1, D_MODEL)), _resident(w_ext.shape),
                  _resident((1, NSA_HEADS * LANES)), _resident(lb_raw.shape), _resident((1, HG_DV))],
        out_specs=[row(w) for _, w, _, _ in SEGMENTS],
        out_shape=[jax.ShapeDtypeStruct((t, w), dt) for _, w, dt, _ in SEGMENTS],
        scratch_shapes=[pltpu.VMEM((HG_HEADS, HG_DV, HG_DK), F32)],
        compiler_params=_params(("arbitrary",)),
        name="inproj_hgrn",
    )(h, norm_g.reshape(1, -1), w_ext, _q_features(), lb_raw, out_norm.reshape(1, -1))


def _gelu_tanh(x):
    return 0.5 * x * (1.0 + jnp.tanh(0.7978845608028654 * (x + 0.044715 * (x * x * x))))


def _compress_kernel(kv_ref, pos_ref, w1_ref, w2_ref, o_ref):
    nbp = o_ref.shape[1]
    hid = NSA_KV_GROUPS * CMP_HID
    ha = jnp.zeros((nbp, hid), F32)
    hb = jnp.zeros((nbp, hid), F32)
    for l in range(CMP_STRIDE):
        x = kv_ref[0, pl.ds(l, nbp, stride=CMP_STRIDE), :]
        ha = ha + _nn((x + pos_ref[l:l + 1, :]).astype(BF16), w1_ref[l])
        hb = hb + _nn((x + pos_ref[CMP_STRIDE + l:CMP_STRIDE + l + 1, :]).astype(BF16), w1_ref[CMP_STRIDE + l])
    act = _gelu_tanh(ha + pltpu.roll(hb, nbp - 1, axis=0)).astype(BF16)
    out = jnp.zeros(o_ref.shape[1:], F32)
    for g in range(NSA_KV_GROUPS):
        out = out + _nn(act[:, g * CMP_HID:(g + 1) * CMP_HID], w2_ref[g])
    o_ref[0] = out


def _compress_weights(pos, w1, w2, reps):
    layers = w1.shape[0]
    w1l = w1.astype(BF16).reshape(layers, CMP_LEN, NSA_HD, CMP_HID)
    z1 = jnp.zeros_like(w1l)
    w1p = jnp.concatenate([jnp.concatenate([w1l, z1], axis=3), jnp.concatenate([z1, w1l], axis=3)], axis=2)
    pos2 = jnp.concatenate([pos] * NSA_KV_GROUPS, axis=2)
    w2 = w2.astype(BF16)
    zero = jnp.zeros_like(w2)
    w2p = jnp.stack([jnp.concatenate([w2 if r == g else zero for r in reps], axis=2)
                     for g in range(NSA_KV_GROUPS)], axis=1)
    return pos2, w1p, w2p


def _compress(kv, pos2, w1p, w2p):
    b, s, _ = kv.shape
    nbp = s // CMP_STRIDE
    width = w2p.shape[-1]
    return pl.pallas_call(
        _compress_kernel,
        grid=(b,),
        in_specs=[pl.BlockSpec((1, s, KV_W), lambda i: (i, 0, 0)),
                  _resident(pos2.shape), _resident(w1p.shape), _resident(w2p.shape)],
        out_specs=pl.BlockSpec((1, nbp, width), lambda i: (i, 0, 0)),
        out_shape=jax.ShapeDtypeStruct((b, nbp, width), F32),
        compiler_params=_params(("parallel",)),
        name="compress",
    )(kv, pos2, w1p, w2p)


def _pair_blocks(g, jj):
    lower = slice(0, LANES) if g == 0 else slice(LANES, 2 * LANES)
    upper = slice(LANES, 2 * LANES) if g == 0 else slice(0, LANES)
    return lower, upper


def _cmp_kernel(q_ref, kc_ref, vc2_ref, mt_ref, ocmp_ref, sel_ref, picks_ref, *, tq, ns, n_sel):
    nbp = kc_ref.shape[1]
    q0 = pl.program_id(1) * tq
    pos = q0 + lax.broadcasted_iota(jnp.int32, (tq, nbp), 0)
    blk_end = lax.broadcasted_iota(jnp.int32, (tq, nbp), 1) * CMP_STRIDE + (CMP_LEN - 1)
    valid = blk_end <= pos
    row_ok = (q0 + lax.broadcasted_iota(jnp.int32, (tq, 1), 0)) >= CMP_LEN - 1
    kc = kc_ref[0].astype(BF16)
    vc2 = vc2_ref[0].astype(BF16)
    lane_c = lax.broadcasted_iota(jnp.int32, (nbp, LANES), 1)
    c_idx = lax.broadcasted_iota(jnp.int32, (nbp, LANES), 0)
    per = SEL_BLOCK // CMP_STRIDE
    feat_a = (c_idx // per - q0 // SEL_BLOCK).astype(F32)
    feat_b = ((c_idx % per) * CMP_STRIDE + (CMP_LEN - 1)).astype(F32)
    lane_lo = lax.broadcasted_iota(jnp.int32, (tq, LANES), 1) < NSA_HD
    blk_f = lax.broadcasted_iota(jnp.int32, (ns, tq), 0).astype(F32)
    blk = lax.broadcasted_iota(jnp.int32, (ns, tq), 0)
    pos_t = q0 + lax.broadcasted_iota(jnp.int32, (ns, tq), 1)
    cur = pos_t // SEL_BLOCK
    forced = (blk == 0) | (blk == cur) | (blk == cur - 1)
    causal = blk * SEL_BLOCK <= pos_t
    sel_t, picks = [], []
    for g in range(NSA_KV_GROUPS):
        f0 = NSA_HD * (1 - g)
        feat = jnp.where(lane_c == f0, feat_a, jnp.where(lane_c == f0 + 1, feat_b, 0.0))
        kc_g = jnp.where((lane_c // NSA_HD) == g, kc, feat.astype(BF16))
        imp = jnp.zeros((tq, nbp), F32)
        acc = []
        for hh in range(HEADS_PER_GROUP):
            h = g * HEADS_PER_GROUP + hh
            s = _nt(q_ref[0, :, h * LANES:(h + 1) * LANES], kc_g)
            s = jnp.where(valid, s, NEG)
            e = jnp.exp(s - jnp.max(s, axis=-1, keepdims=True))
            inv = jnp.where(row_ok, 1.0 / jnp.sum(e, axis=-1, keepdims=True), 0.0)
            p = e * inv
            imp = imp + p
            acc.append(_nn(p.astype(BF16), vc2))
        for jj in range(HEADS_PER_GROUP // 2):
            lower, upper = _pair_blocks(g, jj)
            blk_out = jnp.where(lane_lo, acc[2 * jj][:, lower], acc[2 * jj + 1][:, upper])
            c0 = (g * HEADS_PER_GROUP + 2 * jj) * NSA_HD
            ocmp_ref[0, :, c0:c0 + LANES] = blk_out
        p_slc = sum(_nt(mt_ref[...], part) for part in _split3(imp))
        taken = -jnp.inf
        score = jnp.where(forced, taken, jnp.where(causal, p_slc, NEG))
        for _ in range(n_sel - 3):
            top = jnp.max(score, axis=0, keepdims=True)
            first = jnp.min(jnp.where(score == top, blk_f, float(ns)), axis=0, keepdims=True)
            score = jnp.where(blk_f == first, taken, score)
        picked = jnp.where(score == taken, 1.0, 0.0).astype(BF16)
        picks.append(_nt(jnp.ones((8, tq), BF16), picked))
        pen = jnp.where(score == taken, 0.0, SEL_PENALTY)
        pen = jnp.where((blk >= 2) & (blk <= ns - 1), pltpu.roll(pen, 1, axis=0), 0.0)
        if ns < NSA_HD:
            pen = jnp.concatenate([pen, jnp.zeros((NSA_HD - ns, tq), F32)], axis=0)
        zero = jnp.zeros((NSA_HD, tq), F32)
        sel_t += [zero, pen] if g == 0 else [pen, zero]
    sel_ref[0] = jnp.concatenate(sel_t, axis=0).T.astype(BF16)
    picks_ref[0, 0] = jnp.concatenate(picks, axis=1)


def _importance_matrix(nbp, ns):
    per = SEL_BLOCK // CMP_STRIDE
    m = np.zeros((ns, nbp), np.float32)
    for n in range(ns):
        for c in range(per * n, per * (n + 1)):
            for cc in (c - 1, c):
                if 0 <= cc < nbp - 1:
                    m[n, cc] += 1.0
    return jnp.asarray(m, BF16)


def _cmp_attention(q, kc, vc2, b, s):
    nbp = s // CMP_STRIDE
    ns = s // SEL_BLOCK
    tq = ATT_TQ
    assert ns <= NSA_HD
    assert min(SEL_TOPK, ns) >= 3
    q3 = q.reshape(b, s, NSA_HEADS * LANES)
    kern = functools.partial(_cmp_kernel, tq=tq, ns=ns, n_sel=min(SEL_TOPK, ns))
    return pl.pallas_call(
        kern,
        grid=(b, s // tq),
        in_specs=[pl.BlockSpec((1, tq, NSA_HEADS * LANES), lambda i, j: (i, j, 0)),
                  pl.BlockSpec((1, nbp, KV_W), lambda i, j: (i, 0, 0)),
                  pl.BlockSpec((1, nbp, 2 * KV_W), lambda i, j: (i, 0, 0)),
                  _resident((ns, nbp))],
        out_specs=[pl.BlockSpec((1, tq, NSA_WIDTH), lambda i, j: (i, j, 0)),
                   pl.BlockSpec((1, tq, NSA_KV_GROUPS * LANES), lambda i, j: (i, j, 0)),
                   pl.BlockSpec((1, 1, 8, NSA_KV_GROUPS * ns), lambda i, j: (i, j, 0, 0))],
        out_shape=[jax.ShapeDtypeStruct((b, s, NSA_WIDTH), F32),
                   jax.ShapeDtypeStruct((b, s, NSA_KV_GROUPS * LANES), BF16),
                   jax.ShapeDtypeStruct((b, s // tq, 8, NSA_KV_GROUPS * ns), F32)],
        compiler_params=_params(("parallel", "parallel")),
        name="cmp_topk",
    )(q3, kc, vc2, _importance_matrix(nbp, ns))


def _attn_kernel(tiles_ref, ntiles_ref, q_ref, ks_ref, vs_ref, kw_ref, vw_ref, penq_ref, gl_ref, gexp_ref,
                 ocmp_ref, o_ref, qa_scr, s_scr, p_scr, al_scr, m_scr, acc_scr, *, tq, tk):
    i = pl.program_id(1)
    rows_of = lambda hh: slice(hh * tq, (hh + 1) * tq)
    lane_lo = lax.broadcasted_iota(jnp.int32, (tq, LANES), 1) < NSA_HD
    rel = (lax.broadcasted_iota(jnp.int32, (tq, tk), 1)
           - lax.broadcasted_iota(jnp.int32, (tq, tk), 0)).astype(F32)

    for hh in range(HEADS_PER_GROUP):
        qa_scr[rows_of(hh)] = q_ref[0, :, hh * LANES:(hh + 1) * LANES] + penq_ref[0]

    def scores(k_ref, j, slot):
        s_scr[slot] = _nt(qa_scr[...], k_ref[0, pl.ds(pl.multiple_of(j * tk, tk), tk), :])

    def probs(br, j, mode, slot, first, exists=None):
        for hh in range(HEADS_PER_GROUP):
            r = rows_of(hh)
            s = s_scr[slot, r]
            if mode == "causal":
                s = jnp.where(rel <= 0.0, s, NEG)
            elif mode == "lower":
                s = jnp.where(rel + (j * tk - i * tq).astype(F32) > -float(WINDOW), s, NEG)
            if exists is not None:
                s = jnp.where(exists, s, NEG)
            m_cur = jnp.max(s, axis=-1, keepdims=True)
            if first:
                m_new = jnp.broadcast_to(m_cur, (tq, LANES))
            else:
                m_prev = m_scr[br, r]
                m_new = jnp.maximum(m_prev, m_cur)
                al_scr[slot, r] = jnp.exp(m_prev - m_new)
            m_scr[br, r] = m_new
            p_scr[slot, r] = jnp.exp(s - jnp.concatenate([m_new] * (tk // LANES), axis=1)).astype(BF16)

    def accumulate(br, v_ref, j, slot, first):
        pv = _nn(p_scr[slot], v_ref[0, pl.ds(pl.multiple_of(j * tk, tk), tk), :])
        for hh in range(HEADS_PER_GROUP):
            r = rows_of(hh)
            if first:
                acc_scr[br, r] = pv[r]
            else:
                acc_scr[br, r] = al_scr[slot, r] * acc_scr[br, r] + pv[r]

    step = (pl.program_id(0) * pl.num_programs(1) + i) * pl.num_programs(2) + pl.program_id(2)
    n_tiles = ntiles_ref[step]
    last = jnp.maximum(n_tiles - 1, 0)
    tile_of = lambda n: tiles_ref[step * pl.num_programs(1) + jnp.minimum(n - 1, last)]
    scores(ks_ref, i, 0)
    scores(ks_ref, tile_of(1), 1)
    probs(0, i, "causal", 0, True)
    scores(ks_ref, tile_of(2), 0)
    probs(0, tile_of(1), None, 1, False)
    accumulate(0, vs_ref, i, 0, True)

    def sel_step(n, parity):
        scores(ks_ref, tile_of(n + 2), parity)
        probs(0, tile_of(n + 1), None, 1 - parity, False)
        accumulate(0, vs_ref, tile_of(n), parity, False)

    def sel_two_steps(t, carry):
        sel_step(2 * t + 1, 1)
        sel_step(2 * t + 2, 0)
        return carry

    lax.fori_loop(0, n_tiles // 2, sel_two_steps, 0)

    @pl.when(n_tiles % 2 == 1)
    def _():
        sel_step(n_tiles, 1)

    n_back = WINDOW // tk
    back = [(jnp.maximum(i - d, 0), "lower" if d == n_back else None, i >= d) for d in range(1, n_back + 1)]
    scores(kw_ref, i, 0)
    scores(kw_ref, back[0][0], 1)
    probs(1, i, "causal", 0, True)
    accumulate(1, vw_ref, i, 0, True)
    for d, (j, mode, exists) in enumerate(back, start=1):
        if d < n_back:
            scores(kw_ref, back[d][0], (d + 1) % 2)
        probs(1, j, mode, d % 2, False, exists)
        accumulate(1, vw_ref, j, d % 2, False)

    gexp = sum(_nn(part, gexp_ref[0]) for part in _split2(_sigmoid(gl_ref[0])))
    n_pairs = HEADS_PER_GROUP // 2
    gate_blk = lambda br, jj: gexp[:, (br * n_pairs + jj) * LANES:(br * n_pairs + jj + 1) * LANES]
    for jj in range(n_pairs):
        cols = slice(jj * LANES, (jj + 1) * LANES)
        blk = gate_blk(0, jj) * ocmp_ref[0, :, cols]
        for br in range(2):
            even = acc_scr[br, rows_of(2 * jj)]
            odd = acc_scr[br, rows_of(2 * jj + 1)]
            low = even / pltpu.roll(even, NSA_HD, axis=1)
            up = pltpu.roll(odd, NSA_HD, axis=1) / odd
            blk = blk + gate_blk(br + 1, jj) * jnp.where(lane_lo, low, up)
        o_ref[0, :, cols] = blk.astype(o_ref.dtype)


def _gate_expansion():
    n_pairs = HEADS_PER_GROUP // 2
    r = np.zeros((NSA_KV_GROUPS, LANES, 3 * n_pairs * LANES), np.float32)
    for g in range(NSA_KV_GROUPS):
        for br in range(3):
            for jj in range(n_pairs):
                for odd in range(2):
                    src = br * NSA_HEADS + g * HEADS_PER_GROUP + 2 * jj + odd
                    dst = (br * n_pairs + jj) * LANES + odd * NSA_HD
                    r[g, src, dst:dst + NSA_HD] = 1.0
    return jnp.asarray(r, BF16)


def _picked_tiles(picks, b, s):
    ns, nq = s // SEL_BLOCK, s // ATT_TQ
    per_tile = ATT_TK // SEL_BLOCK
    per_block = picks[:, :, 0, :].reshape(b, nq, NSA_KV_GROUPS, ns // per_tile, per_tile)
    j = jnp.arange(ns // per_tile, dtype=jnp.int32)
    i = jnp.arange(nq, dtype=jnp.int32)[None, :, None, None]
    active = ((per_block.sum(-1) > 0) | (j == 0)) & (j < i)
    slot = jnp.cumsum(active, axis=-1) - 1
    hit = active[..., None, :] & (slot[..., None, :] == j[:, None])
    tiles = jnp.sum(jnp.where(hit, j, 0), axis=-1).astype(jnp.int32)
    return tiles.reshape(-1), active.sum(-1).astype(jnp.int32).reshape(-1)


def _attention(q, ks, vs, kw, vw, pen, picks, gl, ocmp, b, s):
    tq, tk = ATT_TQ, ATT_TK
    assert tq == tk and WINDOW % tk == 0 and WINDOW // tk <= 2 and tq % SEL_BLOCK == 0
    gw = HEADS_PER_GROUP * LANES
    ow = HEADS_PER_GROUP * NSA_HD
    r3 = lambda a: a.reshape(b, s, a.shape[-1])
    tile = lambda w: pl.BlockSpec((1, tq, w), lambda i, j, g, *_: (i, j, 0))
    gtile = lambda w: pl.BlockSpec((1, tq, w), lambda i, j, g, *_: (i, j, g))
    gfull = pl.BlockSpec((1, s, LANES), lambda i, j, g, *_: (i, 0, g))
    gexp = _gate_expansion()
    rows = HEADS_PER_GROUP * tq
    kern = functools.partial(_attn_kernel, tq=tq, tk=tk)
    tiles, n_tiles = _picked_tiles(picks, b, s)
    grid_spec = pltpu.PrefetchScalarGridSpec(
        num_scalar_prefetch=2,
        grid=(b, s // tq, NSA_KV_GROUPS),
        in_specs=[gtile(gw), gfull, gfull, gfull, gfull, gtile(LANES), tile(LANES),
                  pl.BlockSpec((1,) + gexp.shape[1:], lambda i, j, g, *_: (g, 0, 0)), gtile(ow)],
        out_specs=gtile(ow),
        scratch_shapes=[pltpu.VMEM((rows, LANES), BF16), pltpu.VMEM((2, rows, tk), F32),
                        pltpu.VMEM((2, rows, tk), BF16), pltpu.VMEM((2, rows, LANES), F32),
                        pltpu.VMEM((2, rows, LANES), F32), pltpu.VMEM((2, rows, LANES), F32)])
    return pl.pallas_call(
        kern,
        grid_spec=grid_spec,
        out_shape=jax.ShapeDtypeStruct((b, s, NSA_WIDTH), BF16),
        compiler_params=_params(("parallel", "parallel", "parallel")),
        name="sel_win_attention",
    )(tiles, n_tiles, r3(q), r3(ks), r3(vs), r3(kw), r3(vw), pen, r3(gl), gexp, ocmp)


def _hgrn_rows(y, raw, onorm, st_scr, layer, between=()):
    between = list(between)
    c = HG_CHUNK
    ex = jnp.exp(raw - jnp.max(raw, axis=0, keepdims=True))
    sm = ex / jnp.sum(ex, axis=0, keepdims=True)
    lb_all = jnp.zeros((1, raw.shape[1]), F32)
    for l in range(1, layer + 1):
        lb_all = lb_all + sm[l:l + 1, :]

    t_idx = lax.broadcasted_iota(jnp.int32, (c, HG_DK), 0)
    sub = lax.broadcasted_iota(jnp.int32, (8, HG_DK), 0)
    ti = lax.broadcasted_iota(jnp.int32, (c, c), 0)
    si = lax.broadcasted_iota(jnp.int32, (c, c), 1)
    tril = jnp.where(si <= ti, 1.0, 0.0).astype(BF16)
    levels = (32, 16, 8, 4, 2, 1)
    hc = HG_HEADS * c
    tb = lax.broadcasted_iota(jnp.int32, (hc, hc), 0)
    sb = lax.broadcasted_iota(jnp.int32, (hc, hc), 1)
    same_head = (tb // c) == (sb // c)
    split_bit = tb ^ sb
    pair_mask = {m: jnp.where(same_head & (tb > sb) & (split_bit >= m) & (split_bit < 2 * m), 1.0, 0.0)
                 for m in levels}
    diagonal = tb == sb
    second_half = {m: (t_idx & m) != 0 for m in levels}
    sign = {m: jnp.where(second_half[m], 1.0, -1.0) for m in levels}

    def ref_rows(b, m):
        row = lambda r, n: jnp.broadcast_to(b[r:r + 1, :], (n, HG_DK))
        if m >= 4:
            return jnp.concatenate([row(s0 + m - 1, 2 * m) for s0 in range(0, c, 2 * m)], axis=0)
        return jnp.concatenate([jnp.where(sub < 4, row(s0 + 1, 8), row(s0 + 5, 8)) for s0 in range(0, c, 8)],
                               axis=0)

    part = lambda which, rows, h: y[rows, which * HG_WIDTH + h * HG_DK:which * HG_WIDTH + (h + 1) * HG_DK]
    out = []
    for ci in range(y.shape[0] // c):
        rows = slice(ci * c, (ci + 1) * c)
        q, k, f, v, logf = [], [], [], [], []
        for h in range(HG_HEADS):
            lb = lb_all[:, h * HG_DK:(h + 1) * HG_DK]
            z = part(1, rows, h)
            hq = part(0, rows, h)
            q.append(hq * _sigmoid(hq))
            ez = jnp.exp(-jnp.abs(z))
            big = 1.0 / (1.0 + ez)
            small = ez * big
            f.append(jnp.maximum(lb + (1.0 - lb) * jnp.where(z >= 0.0, big, small), F_FLOOR))
            logf.append(jnp.log2(f[h]))
            k.append((1.0 - lb) * jnp.where(z >= 0.0, small, big))
            v.append(part(2, rows, h).astype(BF16))
        parts = jnp.concatenate([p for h in range(HG_HEADS) for p in _split3(logf[h])], axis=1)
        csum = _nn(tril, parts)
        bcum = [sum(csum[:, (3 * h + i) * HG_DK:(3 * h + i + 1) * HG_DK] for i in range(3))
                for h in range(HG_HEADS)]

        rowdot = jnp.concatenate([jnp.sum(q[h] * k[h], axis=-1, keepdims=True) for h in range(HG_HEADS)], axis=0)
        a = jnp.where(diagonal, rowdot, 0.0)
        for m in levels:
            r = []
            for h in range(HG_HEADS):
                if m == 1:
                    r.append(jnp.where(second_half[m], q[h] * f[h], k[h]))
                else:
                    w = jnp.exp2((bcum[h] - ref_rows(bcum[h], m)) * sign[m])
                    r.append(jnp.where(second_half[m], q[h], k[h]) * w)
            r = jnp.concatenate(r, axis=0).astype(BF16)
            a = a + _nt(r, r) * pair_mask[m]
        o_intra = _nn(a.astype(BF16), jnp.concatenate(v, axis=0))

        heads = []
        for h in range(HG_HEADS):
            st = st_scr[h]
            o = o_intra[h * c:(h + 1) * c] + _nt((q[h] * jnp.exp2(bcum[h])).astype(BF16), st.astype(BF16))
            b_last = bcum[h][c - 1:c, :]
            kd = (k[h] * jnp.exp2(b_last - bcum[h])).astype(BF16)
            st_scr[h] = jnp.exp2(b_last) * st + _tn(v[h], kd)

            o = o * lax.rsqrt(jnp.mean(o * o, axis=-1, keepdims=True) + EPS) * onorm
            gate = part(3, rows, h)
            heads.append((o * (gate * _sigmoid(gate))).astype(BF16))
        out.append(jnp.concatenate(heads, axis=1))
        if between:
            between.pop(0)()
    for thunk in between:
        thunk()
    return jnp.concatenate(out, axis=0)


def _mixer(h, l, b, s, mix_norm, w_ext, cmp_k, cmp_v, hgrn_lower_bound, hgrn_out_norm):
    q, ks, vs, kw, vw, kc_in, vc_in, gl, o_hg = _inproj(h, mix_norm[l], w_ext[l], hgrn_lower_bound,
                                                        hgrn_out_norm[l], s, l)
    kc = _compress(kc_in.reshape(b, s, KV_W), *(w[l] for w in cmp_k))
    vc2 = _compress(vc_in.reshape(b, s, KV_W), *(w[l] for w in cmp_v))
    ocmp, sel, picks = _cmp_attention(q, kc, vc2, b, s)
    o_nsa = _attention(q, ks, vs, kw, vw, sel, picks, gl, ocmp, b, s)
    return o_nsa.reshape(b * s, NSA_WIDTH), o_hg


def kernel(x, ffn1_norm, ffn1_w_gu, ffn1_w_down, mix_norm, w_in, cmp_pos_k, cmp_pos_v, cmp_k_w1, cmp_k_w2, cmp_v_w1, cmp_v_w2, hgrn_lower_bound, hgrn_out_norm, w_out, ffn2_norm, ffn2_w_gu, ffn2_w_down, final_norm):
    b, s, d = x.shape
    depth = ffn1_norm.shape[0]
    ffn1_w = _ffn_weights(ffn1_w_gu, ffn1_w_down)
    ffn2_w = _ffn_weights(ffn2_w_gu, ffn2_w_down)
    w_out = w_out.astype(BF16)
    w_ext = _build_w_in(w_in)
    cmp_k = _compress_weights(cmp_pos_k, cmp_k_w1, cmp_k_w2, (0, 1))
    cmp_v = _compress_weights(cmp_pos_v, cmp_v_w1, cmp_v_w2, (0, 1, 1, 0))
    h = x.reshape(b * s, d)
    for l in range(depth):
        h = _ffn(h, ffn1_norm[l], *(w[l] for w in ffn1_w), final_norm, False)
        o_nsa, o_hg = _mixer(h, l, b, s, mix_norm, w_ext, cmp_k, cmp_v, hgrn_lower_bound, hgrn_out_norm)
        h = _ffn(h, ffn2_norm[l], *(w[l] for w in ffn2_w), final_norm, l == depth - 1, (o_nsa, o_hg, w_out[l]))
    return h.reshape(b, s, d)
```

```python
import functools

import jax
import jax.numpy as jnp
import numpy as np
from jax import lax
from jax.experimental import pallas as pl
from jax.experimental.pallas import tpu as pltpu

F32 = jnp.float32
BF16 = jnp.bfloat16

D_MODEL = 1024
EPS = 1e-6
NEG = -1e30
FORCE = 1e6
F_FLOOR = 1e-30
NSA_HEADS = 8
NSA_KV_GROUPS = 2
HEADS_PER_GROUP = NSA_HEADS // NSA_KV_GROUPS
NSA_HD = 64
CMP_LEN = 32
CMP_STRIDE = 16
CMP_HID = 256
SEL_BLOCK = 64
SEL_TOPK = 16
WINDOW = 512
HG_HEADS = 4
HG_DK = 128
HG_DV = 128
HG_CHUNK = 64
D_FF = 2752
NSA_WIDTH = NSA_HEADS * NSA_HD
HG_WIDTH = HG_HEADS * HG_DV
KV_W = NSA_KV_GROUPS * NSA_HD

LANES = 128
SUBLANES = 8
D_FF_PAD = 2816
FF_CHUNK = 256
ROW_TILE = 512
FFN_ROWS = 1024
ATT_TQ = 256
ATT_TK = 256
VMEM_LIMIT = 56 * 1024 * 1024

SEL_PENALTY = 2.0 ** 50
ALIBI_SLOPES = tuple(2.0 ** (-8.0 * (i + 1) / NSA_HEADS) for i in range(NSA_HEADS))

SEGMENTS = (
    ("q", NSA_HEADS * LANES, BF16, NSA_WIDTH),
    ("ks", NSA_KV_GROUPS * LANES, BF16, KV_W),
    ("vs", NSA_KV_GROUPS * LANES, BF16, KV_W),
    ("kw", NSA_KV_GROUPS * LANES, BF16, KV_W),
    ("vw", NSA_KV_GROUPS * LANES, BF16, KV_W),
    ("kc", KV_W, F32, KV_W),
    ("vc", KV_W, F32, KV_W),
    ("gl", LANES, F32, LANES),
    ("hg", HG_WIDTH, BF16, 4 * HG_WIDTH),
)
SEG_OFFSETS = tuple(int(v) for v in np.cumsum([0] + [s[3] for s in SEGMENTS]))


def _nn(a, b):
    return jnp.dot(a, b, preferred_element_type=F32)


def _nt(a, b):
    return lax.dot_general(a, b, (((1,), (1,)), ((), ())), preferred_element_type=F32)


def _tn(a, b):
    return lax.dot_general(a, b, (((0,), (0,)), ((), ())), preferred_element_type=F32)


def _split2(x):
    hi = x.astype(BF16)
    lo = (x - hi.astype(F32)).astype(BF16)
    return hi, lo


def _split3(x):
    hi = x.astype(BF16)
    r = x - hi.astype(F32)
    mid = r.astype(BF16)
    lo = (r - mid.astype(F32)).astype(BF16)
    return hi, mid, lo


def _dot3(a, b):
    ah, al = _split2(a)
    bh, bl = _split2(b)
    return _nn(ah, bh) + _nn(ah, bl) + _nn(al, bh)


def _sigmoid(x):
    return 1.0 / (1.0 + jnp.exp(-x))


def _rms(x, g):
    return x * lax.rsqrt(jnp.mean(x * x, axis=-1, keepdims=True) + EPS) * g


def _resident(shape):
    nd = len(shape)
    return pl.BlockSpec(shape, lambda *_: (0,) * nd, pipeline_mode=pl.Buffered(1))


def _params(sem):
    return pltpu.CompilerParams(dimension_semantics=sem, vmem_limit_bytes=VMEM_LIMIT)


def _ffn_kernel(*refs, final, proj):
    if proj:
        x_ref, a_ref, b_ref, wo_ref, g_ref, wg_ref, wu_ref, wd_ref, gf_ref, o_ref = refs
        x = x_ref[...] + _nn(a_ref[...], wo_ref[:NSA_WIDTH, :]) + _nn(b_ref[...], wo_ref[NSA_WIDTH:, :])
    else:
        x_ref, g_ref, wg_ref, wu_ref, wd_ref, gf_ref, o_ref = refs
        x = x_ref[...]
    xn = _rms(x, g_ref[...]).astype(BF16)
    acc = jnp.zeros(x.shape, F32)
    for k in range(D_FF_PAD // FF_CHUNK):
        sl = slice(k * FF_CHUNK, (k + 1) * FF_CHUNK)
        gate = _nn(xn, wg_ref[:, sl])
        up = _nn(xn, wu_ref[:, sl])
        h = (gate * _sigmoid(gate) * up).astype(BF16)
        acc = acc + _nn(h, wd_ref[sl, :])
    y = x + 0.5 * acc
    if final:
        y = _rms(y, gf_ref[...])
    o_ref[...] = y


def _ffn_weights(w_gu, w_down):
    pad = D_FF_PAD - D_FF
    wg = jnp.pad(w_gu[..., :D_FF], ((0, 0), (0, 0), (0, pad))).astype(BF16)
    wu = jnp.pad(w_gu[..., D_FF:], ((0, 0), (0, 0), (0, pad))).astype(BF16)
    wd = jnp.pad(w_down, ((0, 0), (0, pad), (0, 0))).astype(BF16)
    return wg, wu, wd


def _ffn(h, norm_g, wg, wu, wd, final_g, final, proj=None):
    t = h.shape[0]
    row = lambda w: pl.BlockSpec((FFN_ROWS, w), lambda i: (i, 0))
    weights = [_resident((1, D_MODEL)), _resident(wg.shape), _resident(wu.shape), _resident(wd.shape),
               _resident((1, D_MODEL))]
    operands = [norm_g.reshape(1, -1), wg, wu, wd, final_g.reshape(1, -1)]
    if proj is None:
        in_specs, args = [row(D_MODEL)] + weights, [h] + operands
    else:
        o_nsa, o_hg, w_out = proj
        in_specs = [row(D_MODEL), row(NSA_WIDTH), row(HG_WIDTH), _resident(w_out.shape)] + weights
        args = [h, o_nsa, o_hg, w_out] + operands
    return pl.pallas_call(
        functools.partial(_ffn_kernel, final=final, proj=proj is not None),
        grid=(t // FFN_ROWS,),
        in_specs=in_specs,
        out_specs=row(D_MODEL),
        out_shape=jax.ShapeDtypeStruct((t, D_MODEL), F32),
        compiler_params=_params(("parallel",)),
        name="ffn",
    )(*args)


def _inproj_kernel(x_ref, g_ref, w_ref, qfeat_ref, lbraw_ref, onorm_ref, *refs, seq, layer):
    *o_refs, st_scr = refs
    xn = _rms(x_ref[...], g_ref[...]).astype(BF16)
    rows = x_ref.shape[0]

    @pl.when(pl.program_id(0) % (seq // rows) == 0)
    def _():
        st_scr[...] = jnp.zeros(st_scr.shape, F32)

    kpos = (pl.program_id(0) % (seq // rows)) * rows + lax.broadcasted_iota(jnp.int32, (rows, LANES), 0)
    lane = lax.broadcasted_iota(jnp.int32, (rows, LANES), 1)
    blk = kpos // SEL_BLOCK
    digit = lambda f0: jnp.where(lane == f0, blk.astype(F32),
                                 jnp.where(lane == f0 + 1, (kpos % SEL_BLOCK).astype(F32), 0.0))
    halves = [NSA_HD * (1 - g) for g in range(NSA_KV_GROUPS)]
    kfeat = jnp.concatenate([digit(f0) for f0 in halves], axis=1)
    is_pen = lambda f0: (lane == f0 + 1 + blk) & (blk >= 1) & (blk <= seq // SEL_BLOCK - 2)
    kpen = jnp.concatenate([jnp.where(is_pen(f0), -SEL_PENALTY, 0.0) for f0 in halves], axis=1)
    lower = lane < NSA_HD
    swap = lambda a: pltpu.roll(a, NSA_HD, axis=1)
    small = [i for i, seg in enumerate(SEGMENTS) if seg[3] == LANES]
    assert small == list(range(small[0], small[-1] + 1))
    y_small = _nn(xn, w_ref[:, SEG_OFFSETS[small[0]]:SEG_OFFSETS[small[-1] + 1]])
    for i, (o_ref, (name, _, dtype, cols)) in enumerate(zip(o_refs, SEGMENTS, strict=True)):
        if i in small:
            y = y_small[:, (i - small[0]) * LANES:(i - small[0] + 1) * LANES]
        else:
            y = _nn(xn, w_ref[:, SEG_OFFSETS[i]:SEG_OFFSETS[i] + cols])
        if name == "q":
            blocks = []
            for h in range(NSA_HEADS):
                pair = y[:, (h // 2) * LANES:(h // 2 + 1) * LANES]
                g = h // HEADS_PER_GROUP
                data = pair if h % 2 == g else swap(pair)
                feat = qfeat_ref[:, h * LANES:(h + 1) * LANES]
                blocks.append(jnp.where(lower == (g == 0), data, feat))
            y = jnp.concatenate(blocks, axis=1)
        elif name in ("ks", "kw"):
            feat = kfeat + kpen if name == "ks" else kfeat
            y = jnp.concatenate([jnp.where(lower == (g == 0), y, feat[:, g * LANES:(g + 1) * LANES])
                                 for g in range(NSA_KV_GROUPS)], axis=1)
        elif name in ("vs", "vw"):
            y = jnp.concatenate([jnp.where(lower, y if g == 0 else swap(y), 1.0)
                                 for g in range(NSA_KV_GROUPS)], axis=1)
        elif name == "hg":
            y = _hgrn_rows(y, lbraw_ref[...], onorm_ref[...], st_scr, layer)
        o_ref[...] = y.astype(dtype)


def _q_features():
    feat = np.zeros((1, NSA_HEADS * LANES), np.float32)
    for h in range(NSA_HEADS):
        f0 = h * LANES + NSA_HD * (1 - h // HEADS_PER_GROUP)
        feat[0, f0] = SEL_BLOCK * ALIBI_SLOPES[h]
        feat[0, f0 + 1] = ALIBI_SLOPES[h]
    return jnp.asarray(feat)


def _build_w_in(w_in):
    sizes = (NSA_WIDTH, KV_W, KV_W, KV_W, KV_W, KV_W, KV_W, NSA_HEADS * 3,
             HG_WIDTH, HG_WIDTH, HG_WIDTH, HG_WIDTH)
    splits = [int(v) for v in np.cumsum(sizes)[:-1]]
    w_in = w_in.astype(BF16)
    wq, wkc, wvc, wks, wvs, wkw, wvw, wgl, whq, whf, whi, whg = jnp.split(w_in, splits, axis=-1)
    lead = w_in.shape[:-1]
    gl = jnp.swapaxes(wgl.reshape(lead + (NSA_HEADS, 3)), -1, -2).reshape(lead + (3 * NSA_HEADS,))
    gl = jnp.pad(gl, [(0, 0)] * len(lead) + [(0, LANES - 3 * NSA_HEADS)])
    cols = [wq * NSA_HD ** -0.5, wks, wvs, wkw, wvw, wkc, wvc, gl, whq, whf, whi, whg]
    return jnp.concatenate(cols, axis=-1)


def _inproj(h, norm_g, w_ext, lb_raw, out_norm, seq, layer):
    t = h.shape[0]
    assert seq % ROW_TILE == 0 and ROW_TILE % HG_CHUNK == 0
    row = lambda w: pl.BlockSpec((ROW_TILE, w), lambda i: (i, 0))
    return pl.pallas_call(
        functools.partial(_inproj_kernel, seq=seq, layer=layer),
        grid=(t // ROW_TILE,),
        in_specs=[row(D_MODEL), _resident((1, D_MODEL)), _resident(w_ext.shape),
                  _resident((1, NSA_HEADS * LANES)), _resident(lb_raw.shape), _resident((1, HG_DV))],
        out_specs=[row(w) for _, w, _, _ in SEGMENTS],
        out_shape=[jax.ShapeDtypeStruct((t, w), dt) for _, w, dt, _ in SEGMENTS],
        scratch_shapes=[pltpu.VMEM((HG_HEADS, HG_DV, HG_DK), F32)],
        compiler_params=_params(("arbitrary",)),
        name="inproj_hgrn",
    )(h, norm_g.reshape(1, -1), w_ext, _q_features(), lb_raw, out_norm.reshape(1, -1))


def _gelu_tanh(x):
    return 0.5 * x * (1.0 + jnp.tanh(0.7978845608028654 * (x + 0.044715 * (x * x * x))))


def _compress_kernel(kv_ref, pos_ref, w1_ref, w2_ref, o_ref):
    nbp = o_ref.shape[1]
    hid = NSA_KV_GROUPS * CMP_HID
    ha = jnp.zeros((nbp, hid), F32)
    hb = jnp.zeros((nbp, hid), F32)
    for l in range(CMP_STRIDE):
        x = kv_ref[0, pl.ds(l, nbp, stride=CMP_STRIDE), :]
        ha = ha + _nn((x + pos_ref[l:l + 1, :]).astype(BF16), w1_ref[l])
        hb = hb + _nn((x + pos_ref[CMP_STRIDE + l:CMP_STRIDE + l + 1, :]).astype(BF16), w1_ref[CMP_STRIDE + l])
    act = _gelu_tanh(ha + pltpu.roll(hb, nbp - 1, axis=0)).astype(BF16)
    out = jnp.zeros(o_ref.shape[1:], F32)
    for g in range(NSA_KV_GROUPS):
        out = out + _nn(act[:, g * CMP_HID:(g + 1) * CMP_HID], w2_ref[g])
    o_ref[0] = out


def _compress_weights(pos, w1, w2, reps):
    layers = w1.shape[0]
    w1l = w1.astype(BF16).reshape(layers, CMP_LEN, NSA_HD, CMP_HID)
    z1 = jnp.zeros_like(w1l)
    w1p = jnp.concatenate([jnp.concatenate([w1l, z1], axis=3), jnp.concatenate([z1, w1l], axis=3)], axis=2)
    pos2 = jnp.concatenate([pos] * NSA_KV_GROUPS, axis=2)
    w2 = w2.astype(BF16)
    zero = jnp.zeros_like(w2)
    w2p = jnp.stack([jnp.concatenate([w2 if r == g else zero for r in reps], axis=2)
                     for g in range(NSA_KV_GROUPS)], axis=1)
    return pos2, w1p, w2p


def _compress(kv, pos2, w1p, w2p):
    b, s, _ = kv.shape
    nbp = s // CMP_STRIDE
    width = w2p.shape[-1]
    return pl.pallas_call(
        _compress_kernel,
        grid=(b,),
        in_specs=[pl.BlockSpec((1, s, KV_W), lambda i: (i, 0, 0)),
                  _resident(pos2.shape), _resident(w1p.shape), _resident(w2p.shape)],
        out_specs=pl.BlockSpec((1, nbp, width), lambda i: (i, 0, 0)),
        out_shape=jax.ShapeDtypeStruct((b, nbp, width), F32),
        compiler_params=_params(("parallel",)),
        name="compress",
    )(kv, pos2, w1p, w2p)


def _pair_blocks(g, jj):
    lower = slice(0, LANES) if g == 0 else slice(LANES, 2 * LANES)
    upper = slice(LANES, 2 * LANES) if g == 0 else slice(0, LANES)
    return lower, upper


def _cmp_kernel(q_ref, kc_ref, vc2_ref, mt_ref, ocmp_ref, sel_ref, picks_ref, *, tq, ns, n_sel):
    nbp = kc_ref.shape[1]
    q0 = pl.program_id(1) * tq
    pos = q0 + lax.broadcasted_iota(jnp.int32, (tq, nbp), 0)
    blk_end = lax.broadcasted_iota(jnp.int32, (tq, nbp), 1) * CMP_STRIDE + (CMP_LEN - 1)
    valid = blk_end <= pos
    row_ok = (q0 + lax.broadcasted_iota(jnp.int32, (tq, 1), 0)) >= CMP_LEN - 1
    kc = kc_ref[0].astype(BF16)
    vc2 = vc2_ref[0].astype(BF16)
    lane_c = lax.broadcasted_iota(jnp.int32, (nbp, LANES), 1)
    c_idx = lax.broadcasted_iota(jnp.int32, (nbp, LANES), 0)
    per = SEL_BLOCK // CMP_STRIDE
    feat_a = (c_idx // per - q0 // SEL_BLOCK).astype(F32)
    feat_b = ((c_idx % per) * CMP_STRIDE + (CMP_LEN - 1)).astype(F32)
    lane_lo = lax.broadcasted_iota(jnp.int32, (tq, LANES), 1) < NSA_HD
    blk_f = lax.broadcasted_iota(jnp.int32, (ns, tq), 0).astype(F32)
    blk = lax.broadcasted_iota(jnp.int32, (ns, tq), 0)
    pos_t = q0 + lax.broadcasted_iota(jnp.int32, (ns, tq), 1)
    cur = pos_t // SEL_BLOCK
    forced = (blk == 0) | (blk == cur) | (blk == cur - 1)
    causal = blk * SEL_BLOCK <= pos_t
    sel_t, picks = [], []
    for g in range(NSA_KV_GROUPS):
        f0 = NSA_HD * (1 - g)
        feat = jnp.where(lane_c == f0, feat_a, jnp.where(lane_c == f0 + 1, feat_b, 0.0))
        kc_g = jnp.where((lane_c // NSA_HD) == g, kc, feat.astype(BF16))
        imp = jnp.zeros((tq, nbp), F32)
        acc = []
        for hh in range(HEADS_PER_GROUP):
            h = g * HEADS_PER_GROUP + hh
            s = _nt(q_ref[0, :, h * LANES:(h + 1) * LANES], kc_g)
            s = jnp.where(valid, s, NEG)
            e = jnp.exp(s - jnp.max(s, axis=-1, keepdims=True))
            inv = jnp.where(row_ok, 1.0 / jnp.sum(e, axis=-1, keepdims=True), 0.0)
            p = e * inv
            imp = imp + p
            acc.append(_nn(p.astype(BF16), vc2))
        for jj in range(HEADS_PER_GROUP // 2):
            lower, upper = _pair_blocks(g, jj)
            blk_out = jnp.where(lane_lo, acc[2 * jj][:, lower], acc[2 * jj + 1][:, upper])
            c0 = (g * HEADS_PER_GROUP + 2 * jj) * NSA_HD
            ocmp_ref[0, :, c0:c0 + LANES] = blk_out
        p_slc = sum(_nt(mt_ref[...], part) for part in _split3(imp))
        taken = -jnp.inf
        score = jnp.where(forced, taken, jnp.where(causal, p_slc, NEG))
        for _ in range(n_sel - 3):
            top = jnp.max(score, axis=0, keepdims=True)
            first = jnp.min(jnp.where(score == top, blk_f, float(ns)), axis=0, keepdims=True)
            score = jnp.where(blk_f == first, taken, score)
        picked = jnp.where(score == taken, 1.0, 0.0).astype(BF16)
        picks.append(_nt(jnp.ones((SUBLANES, tq), BF16), picked))
        pen = jnp.where(score == taken, 0.0, SEL_PENALTY)
        pen = jnp.where((blk >= 2) & (blk <= ns - 1), pltpu.roll(pen, 1, axis=0), 0.0)
        if ns < NSA_HD:
            pen = jnp.concatenate([pen, jnp.zeros((NSA_HD - ns, tq), F32)], axis=0)
        zero = jnp.zeros((NSA_HD, tq), F32)
        sel_t += [zero, pen] if g == 0 else [pen, zero]
    sel_ref[0] = jnp.concatenate(sel_t, axis=0).T.astype(BF16)
    picks_ref[0, 0] = jnp.concatenate(picks, axis=1)


def _importance_matrix(nbp, ns):
    per = SEL_BLOCK // CMP_STRIDE
    m = np.zeros((ns, nbp), np.float32)
    for n in range(ns):
        for c in range(per * n, per * (n + 1)):
            for cc in (c - 1, c):
                if 0 <= cc < nbp - 1:
                    m[n, cc] += 1.0
    return jnp.asarray(m, BF16)


def _cmp_attention(q, kc, vc2, b, s):
    nbp = s // CMP_STRIDE
    ns = s // SEL_BLOCK
    tq = ATT_TQ
    assert ns <= NSA_HD
    assert min(SEL_TOPK, ns) >= 3
    q3 = q.reshape(b, s, NSA_HEADS * LANES)
    kern = functools.partial(_cmp_kernel, tq=tq, ns=ns, n_sel=min(SEL_TOPK, ns))
    return pl.pallas_call(
        kern,
        grid=(b, s // tq),
        in_specs=[pl.BlockSpec((1, tq, NSA_HEADS * LANES), lambda i, j: (i, j, 0)),
                  pl.BlockSpec((1, nbp, KV_W), lambda i, j: (i, 0, 0)),
                  pl.BlockSpec((1, nbp, 2 * KV_W), lambda i, j: (i, 0, 0)),
                  _resident((ns, nbp))],
        out_specs=[pl.BlockSpec((1, tq, NSA_WIDTH), lambda i, j: (i, j, 0)),
                   pl.BlockSpec((1, tq, NSA_KV_GROUPS * LANES), lambda i, j: (i, j, 0)),
                   pl.BlockSpec((1, 1, SUBLANES, NSA_KV_GROUPS * ns), lambda i, j: (i, j, 0, 0))],
        out_shape=[jax.ShapeDtypeStruct((b, s, NSA_WIDTH), F32),
                   jax.ShapeDtypeStruct((b, s, NSA_KV_GROUPS * LANES), BF16),
                   jax.ShapeDtypeStruct((b, s // tq, SUBLANES, NSA_KV_GROUPS * ns), F32)],
        compiler_params=_params(("parallel", "parallel")),
        name="cmp_topk",
    )(q3, kc, vc2, _importance_matrix(nbp, ns))


def _attn_kernel(tiles_ref, ntiles_ref, q_ref, ks_ref, vs_ref, kw_ref, vw_ref, penq_ref, gl_ref, gexp_ref,
                 ocmp_ref, o_ref, qa_scr, s_scr, p_scr, al_scr, m_scr, acc_scr, *, tq, tk):
    i = pl.program_id(1)
    rows_of = lambda hh: slice(hh * tq, (hh + 1) * tq)
    lane_lo = lax.broadcasted_iota(jnp.int32, (tq, LANES), 1) < NSA_HD
    rel = (lax.broadcasted_iota(jnp.int32, (tq, tk), 1)
           - lax.broadcasted_iota(jnp.int32, (tq, tk), 0)).astype(F32)

    for hh in range(HEADS_PER_GROUP):
        qa_scr[rows_of(hh)] = q_ref[0, :, hh * LANES:(hh + 1) * LANES] + penq_ref[0]

    def scores(k_ref, j, slot):
        s_scr[slot] = _nt(qa_scr[...], k_ref[0, pl.ds(pl.multiple_of(j * tk, tk), tk), :])

    def probs(br, j, mode, slot, first, exists=None):
        for hh in range(HEADS_PER_GROUP):
            r = rows_of(hh)
            s = s_scr[slot, r]
            if mode == "causal":
                s = jnp.where(rel <= 0.0, s, NEG)
            elif mode == "lower":
                s = jnp.where(rel + (j * tk - i * tq).astype(F32) > -float(WINDOW), s, NEG)
            if exists is not None:
                s = jnp.where(exists, s, NEG)
            m_cur = jnp.max(s, axis=-1, keepdims=True)
            if first:
                m_new = jnp.broadcast_to(m_cur, (tq, LANES))
            else:
                m_prev = m_scr[br, r]
                m_new = jnp.maximum(m_prev, m_cur)
                al_scr[slot, r] = jnp.exp(m_prev - m_new)
            m_scr[br, r] = m_new
            p_scr[slot, r] = jnp.exp(s - jnp.concatenate([m_new] * (tk // LANES), axis=1)).astype(BF16)

    def accumulate(br, v_ref, j, slot, first):
        pv = _nn(p_scr[slot], v_ref[0, pl.ds(pl.multiple_of(j * tk, tk), tk), :])
        for hh in range(HEADS_PER_GROUP):
            r = rows_of(hh)
            if first:
                acc_scr[br, r] = pv[r]
            else:
                acc_scr[br, r] = al_scr[slot, r] * acc_scr[br, r] + pv[r]

    step = (pl.program_id(0) * pl.num_programs(1) + i) * pl.num_programs(2) + pl.program_id(2)
    n_tiles = ntiles_ref[step]
    last = jnp.maximum(n_tiles - 1, 0)
    tile_of = lambda n: tiles_ref[step * pl.num_programs(1) + jnp.minimum(n - 1, last)]
    scores(ks_ref, i, 0)
    scores(ks_ref, tile_of(1), 1)
    probs(0, i, "causal", 0, True)
    scores(ks_ref, tile_of(2), 0)
    probs(0, tile_of(1), None, 1, False)
    accumulate(0, vs_ref, i, 0, True)

    def sel_step(n, parity):
        scores(ks_ref, tile_of(n + 2), parity)
        probs(0, tile_of(n + 1), None, 1 - parity, False)
        accumulate(0, vs_ref, tile_of(n), parity, False)

    def sel_two_steps(t, carry):
        sel_step(2 * t + 1, 1)
        sel_step(2 * t + 2, 0)
        return carry

    lax.fori_loop(0, n_tiles // 2, sel_two_steps, 0)

    @pl.when(n_tiles % 2 == 1)
    def _():
        sel_step(n_tiles, 1)

    n_back = WINDOW // tk
    back = [(jnp.maximum(i - d, 0), "lower" if d == n_back else None, i >= d) for d in range(1, n_back + 1)]
    scores(kw_ref, i, 0)
    scores(kw_ref, back[0][0], 1)
    probs(1, i, "causal", 0, True)
    accumulate(1, vw_ref, i, 0, True)
    for d, (j, mode, exists) in enumerate(back, start=1):
        if d < n_back:
            scores(kw_ref, back[d][0], (d + 1) % 2)
        probs(1, j, mode, d % 2, False, exists)
        accumulate(1, vw_ref, j, d % 2, False)

    gexp = sum(_nn(part, gexp_ref[0]) for part in _split2(_sigmoid(gl_ref[0])))
    n_pairs = HEADS_PER_GROUP // 2
    gate_blk = lambda br, jj: gexp[:, (br * n_pairs + jj) * LANES:(br * n_pairs + jj + 1) * LANES]
    for jj in range(n_pairs):
        cols = slice(jj * LANES, (jj + 1) * LANES)
        blk = gate_blk(0, jj) * ocmp_ref[0, :, cols]
        for br in range(2):
            even = acc_scr[br, rows_of(2 * jj)]
            odd = acc_scr[br, rows_of(2 * jj + 1)]
            low = even / pltpu.roll(even, NSA_HD, axis=1)
            up = pltpu.roll(odd, NSA_HD, axis=1) / odd
            blk = blk + gate_blk(br + 1, jj) * jnp.where(lane_lo, low, up)
        o_ref[0, :, cols] = blk.astype(o_ref.dtype)


def _gate_expansion():
    n_pairs = HEADS_PER_GROUP // 2
    r = np.zeros((NSA_KV_GROUPS, LANES, 3 * n_pairs * LANES), np.float32)
    for g in range(NSA_KV_GROUPS):
        for br in range(3):
            for jj in range(n_pairs):
                for odd in range(2):
                    src = br * NSA_HEADS + g * HEADS_PER_GROUP + 2 * jj + odd
                    dst = (br * n_pairs + jj) * LANES + odd * NSA_HD
                    r[g, src, dst:dst + NSA_HD] = 1.0
    return jnp.asarray(r, BF16)


def _picked_tiles(picks, b, s):
    ns, nq = s // SEL_BLOCK, s // ATT_TQ
    per_tile = ATT_TK // SEL_BLOCK
    per_block = picks[:, :, 0, :].reshape(b, nq, NSA_KV_GROUPS, ns // per_tile, per_tile)
    j = jnp.arange(ns // per_tile, dtype=jnp.int32)
    i = jnp.arange(nq, dtype=jnp.int32)[None, :, None, None]
    active = ((per_block.sum(-1) > 0) | (j == 0)) & (j < i)
    slot = jnp.cumsum(active, axis=-1) - 1
    hit = active[..., None, :] & (slot[..., None, :] == j[:, None])
    tiles = jnp.sum(jnp.where(hit, j, 0), axis=-1).astype(jnp.int32)
    return tiles.reshape(-1), active.sum(-1).astype(jnp.int32).reshape(-1)


def _attention(q, ks, vs, kw, vw, pen, picks, gl, ocmp, b, s):
    tq, tk = ATT_TQ, ATT_TK
    assert tq == tk and WINDOW % tk == 0 and WINDOW // tk <= 2 and tq % SEL_BLOCK == 0
    gw = HEADS_PER_GROUP * LANES
    ow = HEADS_PER_GROUP * NSA_HD
    r3 = lambda a: a.reshape(b, s, a.shape[-1])
    tile = lambda w: pl.BlockSpec((1, tq, w), lambda i, j, g, *_: (i, j, 0))
    gtile = lambda w: pl.BlockSpec((1, tq, w), lambda i, j, g, *_: (i, j, g))
    gfull = pl.BlockSpec((1, s, LANES), lambda i, j, g, *_: (i, 0, g))
    gexp = _gate_expansion()
    rows = HEADS_PER_GROUP * tq
    kern = functools.partial(_attn_kernel, tq=tq, tk=tk)
    tiles, n_tiles = _picked_tiles(picks, b, s)
    grid_spec = pltpu.PrefetchScalarGridSpec(
        num_scalar_prefetch=2,
        grid=(b, s // tq, NSA_KV_GROUPS),
        in_specs=[gtile(gw), gfull, gfull, gfull, gfull, gtile(LANES), tile(LANES),
                  pl.BlockSpec((1,) + gexp.shape[1:], lambda i, j, g, *_: (g, 0, 0)), gtile(ow)],
        out_specs=gtile(ow),
        scratch_shapes=[pltpu.VMEM((rows, LANES), BF16), pltpu.VMEM((2, rows, tk), F32),
                        pltpu.VMEM((2, rows, tk), BF16), pltpu.VMEM((2, rows, LANES), F32),
                        pltpu.VMEM((2, rows, LANES), F32), pltpu.VMEM((2, rows, LANES), F32)])
    return pl.pallas_call(
        kern,
        grid_spec=grid_spec,
        out_shape=jax.ShapeDtypeStruct((b, s, NSA_WIDTH), BF16),
        compiler_params=_params(("parallel", "parallel", "parallel")),
        name="sel_win_attention",
    )(tiles, n_tiles, r3(q), r3(ks), r3(vs), r3(kw), r3(vw), pen, r3(gl), gexp, ocmp)


def _hgrn_rows(y, raw, onorm, st_scr, layer):
    c = HG_CHUNK
    ex = jnp.exp(raw - jnp.max(raw, axis=0, keepdims=True))
    sm = ex / jnp.sum(ex, axis=0, keepdims=True)
    lb_all = jnp.zeros((1, raw.shape[1]), F32)
    for l in range(1, layer + 1):
        lb_all = lb_all + sm[l:l + 1, :]

    t_idx = lax.broadcasted_iota(jnp.int32, (c, HG_DK), 0)
    sub = lax.broadcasted_iota(jnp.int32, (SUBLANES, HG_DK), 0)
    ti = lax.broadcasted_iota(jnp.int32, (c, c), 0)
    si = lax.broadcasted_iota(jnp.int32, (c, c), 1)
    tril = jnp.where(si <= ti, 1.0, 0.0).astype(BF16)
    levels = (32, 16, 8, 4, 2, 1)
    hc = HG_HEADS * c
    tb = lax.broadcasted_iota(jnp.int32, (hc, hc), 0)
    sb = lax.broadcasted_iota(jnp.int32, (hc, hc), 1)
    same_head = (tb // c) == (sb // c)
    split_bit = tb ^ sb
    pair_mask = {m: jnp.where(same_head & (tb > sb) & (split_bit >= m) & (split_bit < 2 * m), 1.0, 0.0)
                 for m in levels}
    diagonal = tb == sb
    second_half = {m: (t_idx & m) != 0 for m in levels}
    sign = {m: jnp.where(second_half[m], 1.0, -1.0) for m in levels}

    def ref_rows(b, m):
        row = lambda r, n: jnp.broadcast_to(b[r:r + 1, :], (n, HG_DK))
        if m >= 4:
            return jnp.concatenate([row(s0 + m - 1, 2 * m) for s0 in range(0, c, 2 * m)], axis=0)
        return jnp.concatenate([jnp.where(sub < 4, row(s0 + 1, SUBLANES), row(s0 + 5, SUBLANES))
                                for s0 in range(0, c, SUBLANES)], axis=0)

    part = lambda which, rows, h: y[rows, which * HG_WIDTH + h * HG_DK:which * HG_WIDTH + (h + 1) * HG_DK]
    out = []
    for ci in range(y.shape[0] // c):
        rows = slice(ci * c, (ci + 1) * c)
        q, k, f, v, logf = [], [], [], [], []
        for h in range(HG_HEADS):
            lb = lb_all[:, h * HG_DK:(h + 1) * HG_DK]
            z = part(1, rows, h)
            hq = part(0, rows, h)
            q.append(hq * _sigmoid(hq))
            ez = jnp.exp(-jnp.abs(z))
            big = 1.0 / (1.0 + ez)
            small = ez * big
            f.append(jnp.maximum(lb + (1.0 - lb) * jnp.where(z >= 0.0, big, small), F_FLOOR))
            logf.append(jnp.log2(f[h]))
            k.append((1.0 - lb) * jnp.where(z >= 0.0, small, big))
            v.append(part(2, rows, h).astype(BF16))
        parts = jnp.concatenate([p for h in range(HG_HEADS) for p in _split3(logf[h])], axis=1)
        csum = _nn(tril, parts)
        bcum = [sum(csum[:, (3 * h + i) * HG_DK:(3 * h + i + 1) * HG_DK] for i in range(3))
                for h in range(HG_HEADS)]

        rowdot = jnp.concatenate([jnp.sum(q[h] * k[h], axis=-1, keepdims=True) for h in range(HG_HEADS)], axis=0)
        a = jnp.where(diagonal, rowdot, 0.0)
        for m in levels:
            r = []
            for h in range(HG_HEADS):
                if m == 1:
                    r.append(jnp.where(second_half[m], q[h] * f[h], k[h]))
                else:
                    w = jnp.exp2((bcum[h] - ref_rows(bcum[h], m)) * sign[m])
                    r.append(jnp.where(second_half[m], q[h], k[h]) * w)
            r = jnp.concatenate(r, axis=0).astype(BF16)
            a = a + _nt(r, r) * pair_mask[m]
        o_intra = _nn(a.astype(BF16), jnp.concatenate(v, axis=0))

        heads = []
        for h in range(HG_HEADS):
            st = st_scr[h]
            o = o_intra[h * c:(h + 1) * c] + _nt((q[h] * jnp.exp2(bcum[h])).astype(BF16), st.astype(BF16))
            b_last = bcum[h][c - 1:c, :]
            kd = (k[h] * jnp.exp2(b_last - bcum[h])).astype(BF16)
            st_scr[h] = jnp.exp2(b_last) * st + _tn(v[h], kd)

            o = o * lax.rsqrt(jnp.mean(o * o, axis=-1, keepdims=True) + EPS) * onorm
            gate = part(3, rows, h)
            heads.append((o * (gate * _sigmoid(gate))).astype(BF16))
        out.append(jnp.concatenate(heads, axis=1))
    return jnp.concatenate(out, axis=0)


def _mixer(h, l, b, s, mix_norm, w_ext, cmp_k, cmp_v, hgrn_lower_bound, hgrn_out_norm):
    q, ks, vs, kw, vw, kc_in, vc_in, gl, o_hg = _inproj(h, mix_norm[l], w_ext[l], hgrn_lower_bound,
                                                        hgrn_out_norm[l], s, l)
    kc = _compress(kc_in.reshape(b, s, KV_W), *(w[l] for w in cmp_k))
    vc2 = _compress(vc_in.reshape(b, s, KV_W), *(w[l] for w in cmp_v))
    ocmp, sel, picks = _cmp_attention(q, kc, vc2, b, s)
    o_nsa = _attention(q, ks, vs, kw, vw, sel, picks, gl, ocmp, b, s)
    return o_nsa.reshape(b * s, NSA_WIDTH), o_hg


def kernel(x, ffn1_norm, ffn1_w_gu, ffn1_w_down, mix_norm, w_in, cmp_pos_k, cmp_pos_v, cmp_k_w1, cmp_k_w2, cmp_v_w1, cmp_v_w2, hgrn_lower_bound, hgrn_out_norm, w_out, ffn2_norm, ffn2_w_gu, ffn2_w_down, final_norm):
    b, s, d = x.shape
    depth = ffn1_norm.shape[0]
    ffn1_w = _ffn_weights(ffn1_w_gu, ffn1_w_down)
    ffn2_w = _ffn_weights(ffn2_w_gu, ffn2_w_down)
    w_out = w_out.astype(BF16)
    w_ext = _build_w_in(w_in)
    cmp_k = _compress_weights(cmp_pos_k, cmp_k_w1, cmp_k_w2, (0, 1))
    cmp_v = _compress_weights(cmp_pos_v, cmp_v_w1, cmp_v_w2, (0, 1, 1, 0))
    h = x.reshape(b * s, d)
    for l in range(depth):
        h = _ffn(h, ffn1_norm[l], *(w[l] for w in ffn1_w), final_norm, False)
        o_nsa, o_hg = _mixer(h, l, b, s, mix_norm, w_ext, cmp_k, cmp_v, hgrn_lower_bound, hgrn_out_norm)
        h = _ffn(h, ffn2_norm[l], *(w[l] for w in ffn2_w), final_norm, l == depth - 1, (o_nsa, o_hg, w_out[l]))
    return h.reshape(b, s, d)
```

```python
import functools

import jax
import jax.numpy as jnp
import numpy as np
from jax import lax
from jax.experimental import pallas as pl
from jax.experimental.pallas import tpu as pltpu

F32 = jnp.float32
BF16 = jnp.bfloat16

D_MODEL = 1024
EPS = 1e-6
NEG = -1e30
FORCE = 1e6
F_FLOOR = 1e-30
NSA_HEADS = 8
NSA_KV_GROUPS = 2
HEADS_PER_GROUP = NSA_HEADS // NSA_KV_GROUPS
NSA_HD = 64
CMP_LEN = 32
CMP_STRIDE = 16
CMP_HID = 256
SEL_BLOCK = 64
SEL_TOPK = 16
WINDOW = 512
HG_HEADS = 4
HG_DK = 128
HG_DV = 128
HG_CHUNK = 64
D_FF = 2752
NSA_WIDTH = NSA_HEADS * NSA_HD
HG_WIDTH = HG_HEADS * HG_DV
KV_W = NSA_KV_GROUPS * NSA_HD

LANES = 128
SUBLANES = 8
D_FF_PAD = 2816
FF_CHUNK = 256
ROW_TILE = 512
FFN_ROWS = 1024
ATT_TQ = 256
ATT_TK = 256
VMEM_LIMIT = 56 * 1024 * 1024

SEL_PENALTY = 2.0 ** 50
ALIBI_SLOPES = tuple(2.0 ** (-8.0 * (i + 1) / NSA_HEADS) for i in range(NSA_HEADS))

SEGMENTS = (
    ("q", NSA_HEADS * LANES, BF16, NSA_WIDTH),
    ("ks", NSA_KV_GROUPS * LANES, BF16, KV_W),
    ("vs", NSA_KV_GROUPS * LANES, BF16, KV_W),
    ("kw", NSA_KV_GROUPS * LANES, BF16, KV_W),
    ("vw", NSA_KV_GROUPS * LANES, BF16, KV_W),
    ("kc", KV_W, F32, KV_W),
    ("vc", KV_W, F32, KV_W),
    ("gl", LANES, F32, LANES),
    ("hg", HG_WIDTH, BF16, 4 * HG_WIDTH),
)
SEG_OFFSETS = tuple(int(v) for v in np.cumsum([0] + [s[3] for s in SEGMENTS]))


def _nn(a, b):
    return jnp.dot(a, b, preferred_element_type=F32)


def _nt(a, b):
    return lax.dot_general(a, b, (((1,), (1,)), ((), ())), preferred_element_type=F32)


def _tn(a, b):
    return lax.dot_general(a, b, (((0,), (0,)), ((), ())), preferred_element_type=F32)


def _split2(x):
    hi = x.astype(BF16)
    lo = (x - hi.astype(F32)).astype(BF16)
    return hi, lo


def _split3(x):
    hi = x.astype(BF16)
    r = x - hi.astype(F32)
    mid = r.astype(BF16)
    lo = (r - mid.astype(F32)).astype(BF16)
    return hi, mid, lo


def _dot3(a, b):
    ah, al = _split2(a)
    bh, bl = _split2(b)
    return _nn(ah, bh) + _nn(ah, bl) + _nn(al, bh)


def _sigmoid(x):
    return 1.0 / (1.0 + jnp.exp(-x))


def _rms(x, g):
    return x * lax.rsqrt(jnp.mean(x * x, axis=-1, keepdims=True) + EPS) * g


def _resident(shape):
    nd = len(shape)
    return pl.BlockSpec(shape, lambda *_: (0,) * nd, pipeline_mode=pl.Buffered(1))


def _resident_layer(stacked_shape, layer):
    nd = len(stacked_shape)
    return pl.BlockSpec((None,) + tuple(stacked_shape[1:]), lambda *_: (layer,) + (0,) * (nd - 1),
                        pipeline_mode=pl.Buffered(1))


def _params(sem):
    return pltpu.CompilerParams(dimension_semantics=sem, vmem_limit_bytes=VMEM_LIMIT)


def _ffn_kernel(*refs, final, proj):
    if proj:
        x_ref, a_ref, b_ref, wo_ref, g_ref, wg_ref, wu_ref, wd_ref, gf_ref, o_ref = refs
        x = x_ref[...] + _nn(a_ref[...], wo_ref[:NSA_WIDTH, :]) + _nn(b_ref[...], wo_ref[NSA_WIDTH:, :])
    else:
        x_ref, g_ref, wg_ref, wu_ref, wd_ref, gf_ref, o_ref = refs
        x = x_ref[...]
    xn = _rms(x, g_ref[...]).astype(BF16)
    acc = jnp.zeros(x.shape, F32)
    for k in range(D_FF_PAD // FF_CHUNK):
        sl = slice(k * FF_CHUNK, (k + 1) * FF_CHUNK)
        gate = _nn(xn, wg_ref[:, sl])
        up = _nn(xn, wu_ref[:, sl])
        h = (gate * _sigmoid(gate) * up).astype(BF16)
        acc = acc + _nn(h, wd_ref[sl, :])
    y = x + 0.5 * acc
    if final:
        y = _rms(y, gf_ref[...])
    o_ref[...] = y


def _ffn_weights(w_gu, w_down):
    pad = D_FF_PAD - D_FF
    wg = jnp.pad(w_gu[..., :D_FF], ((0, 0), (0, 0), (0, pad))).astype(BF16)
    wu = jnp.pad(w_gu[..., D_FF:], ((0, 0), (0, 0), (0, pad))).astype(BF16)
    wd = jnp.pad(w_down, ((0, 0), (0, pad), (0, 0))).astype(BF16)
    return wg, wu, wd


def _ffn(h, layer, norm_g, wg, wu, wd, final_g, final, proj=None):
    t = h.shape[0]
    row = lambda w: pl.BlockSpec((FFN_ROWS, w), lambda i: (i, 0))
    norm_g = norm_g[:, None, :]
    weights = [_resident_layer(a.shape, layer) for a in (norm_g, wg, wu, wd)] + [_resident((1, D_MODEL))]
    operands = [norm_g, wg, wu, wd, final_g.reshape(1, -1)]
    if proj is None:
        in_specs, args = [row(D_MODEL)] + weights, [h] + operands
    else:
        o_nsa, o_hg, w_out = proj
        in_specs = [row(D_MODEL), row(NSA_WIDTH), row(HG_WIDTH), _resident_layer(w_out.shape, layer)] + weights
        args = [h, o_nsa, o_hg, w_out] + operands
    return pl.pallas_call(
        functools.partial(_ffn_kernel, final=final, proj=proj is not None),
        grid=(t // FFN_ROWS,),
        in_specs=in_specs,
        out_specs=row(D_MODEL),
        out_shape=jax.ShapeDtypeStruct((t, D_MODEL), F32),
        compiler_params=_params(("parallel",)),
        name="ffn",
    )(*args)


def _inproj_kernel(x_ref, g_ref, w_ref, qfeat_ref, lbraw_ref, onorm_ref, *refs, seq, layer):
    *o_refs, st_scr = refs
    xn = _rms(x_ref[...], g_ref[...]).astype(BF16)
    rows = x_ref.shape[0]

    @pl.when(pl.program_id(0) % (seq // rows) == 0)
    def _():
        st_scr[...] = jnp.zeros(st_scr.shape, F32)

    kpos = (pl.program_id(0) % (seq // rows)) * rows + lax.broadcasted_iota(jnp.int32, (rows, LANES), 0)
    lane = lax.broadcasted_iota(jnp.int32, (rows, LANES), 1)
    blk = kpos // SEL_BLOCK
    digit = lambda f0: jnp.where(lane == f0, blk.astype(F32),
                                 jnp.where(lane == f0 + 1, (kpos % SEL_BLOCK).astype(F32), 0.0))
    halves = [NSA_HD * (1 - g) for g in range(NSA_KV_GROUPS)]
    kfeat = jnp.concatenate([digit(f0) for f0 in halves], axis=1)
    is_pen = lambda f0: (lane == f0 + 1 + blk) & (blk >= 1) & (blk <= seq // SEL_BLOCK - 2)
    kpen = jnp.concatenate([jnp.where(is_pen(f0), -SEL_PENALTY, 0.0) for f0 in halves], axis=1)
    lower = lane < NSA_HD
    swap = lambda a: pltpu.roll(a, NSA_HD, axis=1)
    small = [i for i, seg in enumerate(SEGMENTS) if seg[3] == LANES]
    assert small == list(range(small[0], small[-1] + 1))
    y_small = _nn(xn, w_ref[:, SEG_OFFSETS[small[0]]:SEG_OFFSETS[small[-1] + 1]])
    for i, (o_ref, (name, _, dtype, cols)) in enumerate(zip(o_refs, SEGMENTS, strict=True)):
        if i in small:
            y = y_small[:, (i - small[0]) * LANES:(i - small[0] + 1) * LANES]
        else:
            y = _nn(xn, w_ref[:, SEG_OFFSETS[i]:SEG_OFFSETS[i] + cols])
        if name == "q":
            blocks = []
            for h in range(NSA_HEADS):
                pair = y[:, (h // 2) * LANES:(h // 2 + 1) * LANES]
                g = h // HEADS_PER_GROUP
                data = pair if h % 2 == g else swap(pair)
                feat = qfeat_ref[:, h * LANES:(h + 1) * LANES]
                blocks.append(jnp.where(lower == (g == 0), data, feat))
            y = jnp.concatenate(blocks, axis=1)
        elif name in ("ks", "kw"):
            feat = kfeat + kpen if name == "ks" else kfeat
            y = jnp.concatenate([jnp.where(lower == (g == 0), y, feat[:, g * LANES:(g + 1) * LANES])
                                 for g in range(NSA_KV_GROUPS)], axis=1)
        elif name in ("vs", "vw"):
            y = jnp.concatenate([jnp.where(lower, y if g == 0 else swap(y), 1.0)
                                 for g in range(NSA_KV_GROUPS)], axis=1)
        elif name == "hg":
            y = _hgrn_rows(y, lbraw_ref[...], onorm_ref[...], st_scr, layer)
        o_ref[...] = y.astype(dtype)


def _q_features():
    feat = np.zeros((1, NSA_HEADS * LANES), np.float32)
    for h in range(NSA_HEADS):
        f0 = h * LANES + NSA_HD * (1 - h // HEADS_PER_GROUP)
        feat[0, f0] = SEL_BLOCK * ALIBI_SLOPES[h]
        feat[0, f0 + 1] = ALIBI_SLOPES[h]
    return jnp.asarray(feat)


def _build_w_in(w_in):
    sizes = (NSA_WIDTH, KV_W, KV_W, KV_W, KV_W, KV_W, KV_W, NSA_HEADS * 3,
             HG_WIDTH, HG_WIDTH, HG_WIDTH, HG_WIDTH)
    splits = [int(v) for v in np.cumsum(sizes)[:-1]]
    w_in = w_in.astype(BF16)
    wq, wkc, wvc, wks, wvs, wkw, wvw, wgl, whq, whf, whi, whg = jnp.split(w_in, splits, axis=-1)
    lead = w_in.shape[:-1]
    gl = jnp.swapaxes(wgl.reshape(lead + (NSA_HEADS, 3)), -1, -2).reshape(lead + (3 * NSA_HEADS,))
    gl = jnp.pad(gl, [(0, 0)] * len(lead) + [(0, LANES - 3 * NSA_HEADS)])
    cols = [wq * NSA_HD ** -0.5, wks, wvs, wkw, wvw, wkc, wvc, gl, whq, whf, whi, whg]
    return jnp.concatenate(cols, axis=-1)


def _inproj(h, norm_g, w_ext, lb_raw, out_norm, seq, layer):
    t = h.shape[0]
    assert seq % ROW_TILE == 0 and ROW_TILE % HG_CHUNK == 0
    row = lambda w: pl.BlockSpec((ROW_TILE, w), lambda i: (i, 0))
    norm_g, out_norm = norm_g[:, None, :], out_norm[:, None, :]
    return pl.pallas_call(
        functools.partial(_inproj_kernel, seq=seq, layer=layer),
        grid=(t // ROW_TILE,),
        in_specs=[row(D_MODEL), _resident_layer(norm_g.shape, layer), _resident_layer(w_ext.shape, layer),
                  _resident((1, NSA_HEADS * LANES)), _resident(lb_raw.shape),
                  _resident_layer(out_norm.shape, layer)],
        out_specs=[row(w) for _, w, _, _ in SEGMENTS],
        out_shape=[jax.ShapeDtypeStruct((t, w), dt) for _, w, dt, _ in SEGMENTS],
        scratch_shapes=[pltpu.VMEM((HG_HEADS, HG_DV, HG_DK), F32)],
        compiler_params=_params(("arbitrary",)),
        name="inproj_hgrn",
    )(h, norm_g, w_ext, _q_features(), lb_raw, out_norm)


def _gelu_tanh(x):
    return 0.5 * x * (1.0 + jnp.tanh(0.7978845608028654 * (x + 0.044715 * (x * x * x))))


def _compress_kernel(kv_ref, pos_ref, w1_ref, w2_ref, o_ref):
    nbp = o_ref.shape[1]
    hid = NSA_KV_GROUPS * CMP_HID
    ha = jnp.zeros((nbp, hid), F32)
    hb = jnp.zeros((nbp, hid), F32)
    for l in range(CMP_STRIDE):
        x = kv_ref[0, pl.ds(l, nbp, stride=CMP_STRIDE), :]
        ha = ha + _nn((x + pos_ref[l:l + 1, :]).astype(BF16), w1_ref[l])
        hb = hb + _nn((x + pos_ref[CMP_STRIDE + l:CMP_STRIDE + l + 1, :]).astype(BF16), w1_ref[CMP_STRIDE + l])
    act = _gelu_tanh(ha + pltpu.roll(hb, nbp - 1, axis=0)).astype(BF16)
    out = jnp.zeros(o_ref.shape[1:], F32)
    for g in range(NSA_KV_GROUPS):
        out = out + _nn(act[:, g * CMP_HID:(g + 1) * CMP_HID], w2_ref[g])
    o_ref[0] = out


def _compress_weights(pos, w1, w2, reps):
    layers = w1.shape[0]
    w1l = w1.astype(BF16).reshape(layers, CMP_LEN, NSA_HD, CMP_HID)
    z1 = jnp.zeros_like(w1l)
    w1p = jnp.concatenate([jnp.concatenate([w1l, z1], axis=3), jnp.concatenate([z1, w1l], axis=3)], axis=2)
    pos2 = jnp.concatenate([pos] * NSA_KV_GROUPS, axis=2)
    w2 = w2.astype(BF16)
    zero = jnp.zeros_like(w2)
    w2p = jnp.stack([jnp.concatenate([w2 if r == g else zero for r in reps], axis=2)
                     for g in range(NSA_KV_GROUPS)], axis=1)
    return pos2, w1p, w2p


def _compress(kv, layer, pos2, w1p, w2p):
    b, s, _ = kv.shape
    nbp = s // CMP_STRIDE
    width = w2p.shape[-1]
    return pl.pallas_call(
        _compress_kernel,
        grid=(b,),
        in_specs=[pl.BlockSpec((1, s, KV_W), lambda i: (i, 0, 0))]
        + [_resident_layer(a.shape, layer) for a in (pos2, w1p, w2p)],
        out_specs=pl.BlockSpec((1, nbp, width), lambda i: (i, 0, 0)),
        out_shape=jax.ShapeDtypeStruct((b, nbp, width), F32),
        compiler_params=_params(("parallel",)),
        name="compress",
    )(kv, pos2, w1p, w2p)


def _pair_blocks(g, jj):
    lower = slice(0, LANES) if g == 0 else slice(LANES, 2 * LANES)
    upper = slice(LANES, 2 * LANES) if g == 0 else slice(0, LANES)
    return lower, upper


def _cmp_kernel(q_ref, kc_ref, vc2_ref, mt_ref, ocmp_ref, sel_ref, picks_ref, *, tq, ns, n_sel):
    nbp = kc_ref.shape[1]
    q0 = pl.program_id(1) * tq
    pos = q0 + lax.broadcasted_iota(jnp.int32, (tq, nbp), 0)
    blk_end = lax.broadcasted_iota(jnp.int32, (tq, nbp), 1) * CMP_STRIDE + (CMP_LEN - 1)
    valid = blk_end <= pos
    row_ok = (q0 + lax.broadcasted_iota(jnp.int32, (tq, 1), 0)) >= CMP_LEN - 1
    kc = kc_ref[0].astype(BF16)
    vc2 = vc2_ref[0].astype(BF16)
    lane_c = lax.broadcasted_iota(jnp.int32, (nbp, LANES), 1)
    c_idx = lax.broadcasted_iota(jnp.int32, (nbp, LANES), 0)
    per = SEL_BLOCK // CMP_STRIDE
    feat_a = (c_idx // per - q0 // SEL_BLOCK).astype(F32)
    feat_b = ((c_idx % per) * CMP_STRIDE + (CMP_LEN - 1)).astype(F32)
    lane_lo = lax.broadcasted_iota(jnp.int32, (tq, LANES), 1) < NSA_HD
    blk_f = lax.broadcasted_iota(jnp.int32, (ns, tq), 0).astype(F32)
    blk = lax.broadcasted_iota(jnp.int32, (ns, tq), 0)
    pos_t = q0 + lax.broadcasted_iota(jnp.int32, (ns, tq), 1)
    cur = pos_t // SEL_BLOCK
    forced = (blk == 0) | (blk == cur) | (blk == cur - 1)
    causal = blk * SEL_BLOCK <= pos_t
    sel_t, picks = [], []
    for g in range(NSA_KV_GROUPS):
        f0 = NSA_HD * (1 - g)
        feat = jnp.where(lane_c == f0, feat_a, jnp.where(lane_c == f0 + 1, feat_b, 0.0))
        kc_g = jnp.where((lane_c // NSA_HD) == g, kc, feat.astype(BF16))
        imp = jnp.zeros((tq, nbp), F32)
        acc = []
        for hh in range(HEADS_PER_GROUP):
            h = g * HEADS_PER_GROUP + hh
            s = _nt(q_ref[0, :, h * LANES:(h + 1) * LANES], kc_g)
            s = jnp.where(valid, s, NEG)
            e = jnp.exp(s - jnp.max(s, axis=-1, keepdims=True))
            inv = jnp.where(row_ok, 1.0 / jnp.sum(e, axis=-1, keepdims=True), 0.0)
            p = e * inv
            imp = imp + p
            acc.append(_nn(p.astype(BF16), vc2))
        for jj in range(HEADS_PER_GROUP // 2):
            lower, upper = _pair_blocks(g, jj)
            blk_out = jnp.where(lane_lo, acc[2 * jj][:, lower], acc[2 * jj + 1][:, upper])
            c0 = (g * HEADS_PER_GROUP + 2 * jj) * NSA_HD
            ocmp_ref[0, :, c0:c0 + LANES] = blk_out
        p_slc = sum(_nt(mt_ref[...], part) for part in _split3(imp))
        taken = -jnp.inf
        score = jnp.where(forced, taken, jnp.where(causal, p_slc, NEG))
        for _ in range(n_sel - 3):
            top = jnp.max(score, axis=0, keepdims=True)
            first = jnp.min(jnp.where(score == top, blk_f, float(ns)), axis=0, keepdims=True)
            score = jnp.where(blk_f == first, taken, score)
        picked = jnp.where(score == taken, 1.0, 0.0).astype(BF16)
        picks.append(_nt(jnp.ones((SUBLANES, tq), BF16), picked))
        pen = jnp.where(score == taken, 0.0, SEL_PENALTY)
        pen = jnp.where((blk >= 2) & (blk <= ns - 1), pltpu.roll(pen, 1, axis=0), 0.0)
        if ns < NSA_HD:
            pen = jnp.concatenate([pen, jnp.zeros((NSA_HD - ns, tq), F32)], axis=0)
        zero = jnp.zeros((NSA_HD, tq), F32)
        sel_t += [zero, pen] if g == 0 else [pen, zero]
    sel_ref[0] = jnp.concatenate(sel_t, axis=0).T.astype(BF16)
    picks_ref[0, 0] = jnp.concatenate(picks, axis=1)


def _importance_matrix(nbp, ns):
    per = SEL_BLOCK // CMP_STRIDE
    m = np.zeros((ns, nbp), np.float32)
    for n in range(ns):
        for c in range(per * n, per * (n + 1)):
            for cc in (c - 1, c):
                if 0 <= cc < nbp - 1:
                    m[n, cc] += 1.0
    return jnp.asarray(m, BF16)


def _cmp_attention(q, kc, vc2, b, s):
    nbp = s // CMP_STRIDE
    ns = s // SEL_BLOCK
    tq = ATT_TQ
    assert ns <= NSA_HD
    assert min(SEL_TOPK, ns) >= 3
    q3 = q.reshape(b, s, NSA_HEADS * LANES)
    kern = functools.partial(_cmp_kernel, tq=tq, ns=ns, n_sel=min(SEL_TOPK, ns))
    return pl.pallas_call(
        kern,
        grid=(b, s // tq),
        in_specs=[pl.BlockSpec((1, tq, NSA_HEADS * LANES), lambda i, j: (i, j, 0)),
                  pl.BlockSpec((1, nbp, KV_W), lambda i, j: (i, 0, 0)),
                  pl.BlockSpec((1, nbp, 2 * KV_W), lambda i, j: (i, 0, 0)),
                  _resident((ns, nbp))],
        out_specs=[pl.BlockSpec((1, tq, NSA_WIDTH), lambda i, j: (i, j, 0)),
                   pl.BlockSpec((1, tq, NSA_KV_GROUPS * LANES), lambda i, j: (i, j, 0)),
                   pl.BlockSpec((1, 1, SUBLANES, NSA_KV_GROUPS * ns), lambda i, j: (i, j, 0, 0))],
        out_shape=[jax.ShapeDtypeStruct((b, s, NSA_WIDTH), F32),
                   jax.ShapeDtypeStruct((b, s, NSA_KV_GROUPS * LANES), BF16),
                   jax.ShapeDtypeStruct((b, s // tq, SUBLANES, NSA_KV_GROUPS * ns), F32)],
        compiler_params=_params(("parallel", "parallel")),
        name="cmp_topk",
    )(q3, kc, vc2, _importance_matrix(nbp, ns))


def _attn_kernel(tiles_ref, ntiles_ref, q_ref, ks_ref, vs_ref, kw_ref, vw_ref, penq_ref, gl_ref, gexp_ref,
                 ocmp_ref, o_ref, qa_scr, s_scr, p_scr, al_scr, m_scr, acc_scr, *, tq, tk):
    i = pl.program_id(1)
    rows_of = lambda hh: slice(hh * tq, (hh + 1) * tq)
    lane_lo = lax.broadcasted_iota(jnp.int32, (tq, LANES), 1) < NSA_HD
    rel = (lax.broadcasted_iota(jnp.int32, (tq, tk), 1)
           - lax.broadcasted_iota(jnp.int32, (tq, tk), 0)).astype(F32)

    for hh in range(HEADS_PER_GROUP):
        qa_scr[rows_of(hh)] = q_ref[0, :, hh * LANES:(hh + 1) * LANES] + penq_ref[0]

    def scores(k_ref, j, slot):
        s_scr[slot] = _nt(qa_scr[...], k_ref[0, pl.ds(pl.multiple_of(j * tk, tk), tk), :])

    def probs(br, j, mode, slot, first, exists=None):
        for hh in range(HEADS_PER_GROUP):
            r = rows_of(hh)
            s = s_scr[slot, r]
            if mode == "causal":
                s = jnp.where(rel <= 0.0, s, NEG)
            elif mode == "lower":
                s = jnp.where(rel + (j * tk - i * tq).astype(F32) > -float(WINDOW), s, NEG)
            if exists is not None:
                s = jnp.where(exists, s, NEG)
            m_cur = jnp.max(s, axis=-1, keepdims=True)
            if first:
                m_new = jnp.broadcast_to(m_cur, (tq, LANES))
            else:
                m_prev = m_scr[br, r]
                m_new = jnp.maximum(m_prev, m_cur)
                al_scr[slot, r] = jnp.exp(m_prev - m_new)
            m_scr[br, r] = m_new
            p_scr[slot, r] = jnp.exp(s - jnp.concatenate([m_new] * (tk // LANES), axis=1)).astype(BF16)

    def accumulate(br, v_ref, j, slot, first):
        pv = _nn(p_scr[slot], v_ref[0, pl.ds(pl.multiple_of(j * tk, tk), tk), :])
        for hh in range(HEADS_PER_GROUP):
            r = rows_of(hh)
            if first:
                acc_scr[br, r] = pv[r]
            else:
                acc_scr[br, r] = al_scr[slot, r] * acc_scr[br, r] + pv[r]

    step = (pl.program_id(0) * pl.num_programs(1) + i) * pl.num_programs(2) + pl.program_id(2)
    n_tiles = ntiles_ref[step]
    last = jnp.maximum(n_tiles - 1, 0)
    tile_of = lambda n: tiles_ref[step * pl.num_programs(1) + jnp.minimum(n - 1, last)]
    scores(ks_ref, i, 0)
    scores(ks_ref, tile_of(1), 1)
    probs(0, i, "causal", 0, True)
    scores(ks_ref, tile_of(2), 0)
    probs(0, tile_of(1), None, 1, False)
    accumulate(0, vs_ref, i, 0, True)

    def sel_step(n, parity):
        scores(ks_ref, tile_of(n + 2), parity)
        probs(0, tile_of(n + 1), None, 1 - parity, False)
        accumulate(0, vs_ref, tile_of(n), parity, False)

    def sel_two_steps(t, carry):
        sel_step(2 * t + 1, 1)
        sel_step(2 * t + 2, 0)
        return carry

    lax.fori_loop(0, n_tiles // 2, sel_two_steps, 0)

    @pl.when(n_tiles % 2 == 1)
    def _():
        sel_step(n_tiles, 1)

    n_back = WINDOW // tk
    back = [(jnp.maximum(i - d, 0), "lower" if d == n_back else None, i >= d) for d in range(1, n_back + 1)]
    scores(kw_ref, i, 0)
    scores(kw_ref, back[0][0], 1)
    probs(1, i, "causal", 0, True)
    accumulate(1, vw_ref, i, 0, True)
    for d, (j, mode, exists) in enumerate(back, start=1):
        if d < n_back:
            scores(kw_ref, back[d][0], (d + 1) % 2)
        probs(1, j, mode, d % 2, False, exists)
        accumulate(1, vw_ref, j, d % 2, False)

    gexp = sum(_nn(part, gexp_ref[0]) for part in _split2(_sigmoid(gl_ref[0])))
    n_pairs = HEADS_PER_GROUP // 2
    gate_blk = lambda br, jj: gexp[:, (br * n_pairs + jj) * LANES:(br * n_pairs + jj + 1) * LANES]
    for jj in range(n_pairs):
        cols = slice(jj * LANES, (jj + 1) * LANES)
        blk = gate_blk(0, jj) * ocmp_ref[0, :, cols]
        for br in range(2):
            even = acc_scr[br, rows_of(2 * jj)]
            odd = acc_scr[br, rows_of(2 * jj + 1)]
            low = even / pltpu.roll(even, NSA_HD, axis=1)
            up = pltpu.roll(odd, NSA_HD, axis=1) / odd
            blk = blk + gate_blk(br + 1, jj) * jnp.where(lane_lo, low, up)
        o_ref[0, :, cols] = blk.astype(o_ref.dtype)


def _gate_expansion():
    n_pairs = HEADS_PER_GROUP // 2
    r = np.zeros((NSA_KV_GROUPS, LANES, 3 * n_pairs * LANES), np.float32)
    for g in range(NSA_KV_GROUPS):
        for br in range(3):
            for jj in range(n_pairs):
                for odd in range(2):
                    src = br * NSA_HEADS + g * HEADS_PER_GROUP + 2 * jj + odd
                    dst = (br * n_pairs + jj) * LANES + odd * NSA_HD
                    r[g, src, dst:dst + NSA_HD] = 1.0
    return jnp.asarray(r, BF16)


def _picked_tiles(picks, b, s):
    ns, nq = s // SEL_BLOCK, s // ATT_TQ
    per_tile = ATT_TK // SEL_BLOCK
    per_block = picks[:, :, 0, :].reshape(b, nq, NSA_KV_GROUPS, ns // per_tile, per_tile)
    j = jnp.arange(ns // per_tile, dtype=jnp.int32)
    i = jnp.arange(nq, dtype=jnp.int32)[None, :, None, None]
    active = ((per_block.sum(-1) > 0) | (j == 0)) & (j < i)
    slot = jnp.cumsum(active, axis=-1) - 1
    hit = active[..., None, :] & (slot[..., None, :] == j[:, None])
    tiles = jnp.sum(jnp.where(hit, j, 0), axis=-1).astype(jnp.int32)
    return tiles.reshape(-1), active.sum(-1).astype(jnp.int32).reshape(-1)


def _attention(q, ks, vs, kw, vw, pen, picks, gl, ocmp, b, s):
    tq, tk = ATT_TQ, ATT_TK
    assert tq == tk and WINDOW % tk == 0 and WINDOW // tk <= 2 and tq % SEL_BLOCK == 0
    gw = HEADS_PER_GROUP * LANES
    ow = HEADS_PER_GROUP * NSA_HD
    r3 = lambda a: a.reshape(b, s, a.shape[-1])
    tile = lambda w: pl.BlockSpec((1, tq, w), lambda i, j, g, *_: (i, j, 0))
    gtile = lambda w: pl.BlockSpec((1, tq, w), lambda i, j, g, *_: (i, j, g))
    gfull = pl.BlockSpec((1, s, LANES), lambda i, j, g, *_: (i, 0, g))
    gexp = _gate_expansion()
    rows = HEADS_PER_GROUP * tq
    kern = functools.partial(_attn_kernel, tq=tq, tk=tk)
    tiles, n_tiles = _picked_tiles(picks, b, s)
    grid_spec = pltpu.PrefetchScalarGridSpec(
        num_scalar_prefetch=2,
        grid=(b, s // tq, NSA_KV_GROUPS),
        in_specs=[gtile(gw), gfull, gfull, gfull, gfull, gtile(LANES), tile(LANES),
                  pl.BlockSpec((1,) + gexp.shape[1:], lambda i, j, g, *_: (g, 0, 0)), gtile(ow)],
        out_specs=gtile(ow),
        scratch_shapes=[pltpu.VMEM((rows, LANES), BF16), pltpu.VMEM((2, rows, tk), F32),
                        pltpu.VMEM((2, rows, tk), BF16), pltpu.VMEM((2, rows, LANES), F32),
                        pltpu.VMEM((2, rows, LANES), F32), pltpu.VMEM((2, rows, LANES), F32)])
    return pl.pallas_call(
        kern,
        grid_spec=grid_spec,
        out_shape=jax.ShapeDtypeStruct((b, s, NSA_WIDTH), BF16),
        compiler_params=_params(("parallel", "parallel", "parallel")),
        name="sel_win_attention",
    )(tiles, n_tiles, r3(q), r3(ks), r3(vs), r3(kw), r3(vw), pen, r3(gl), gexp, ocmp)


def _hgrn_rows(y, raw, onorm, st_scr, layer):
    c = HG_CHUNK
    ex = jnp.exp(raw - jnp.max(raw, axis=0, keepdims=True))
    sm = ex / jnp.sum(ex, axis=0, keepdims=True)
    lb_all = jnp.zeros((1, raw.shape[1]), F32)
    for l in range(1, layer + 1):
        lb_all = lb_all + sm[l:l + 1, :]

    t_idx = lax.broadcasted_iota(jnp.int32, (c, HG_DK), 0)
    sub = lax.broadcasted_iota(jnp.int32, (SUBLANES, HG_DK), 0)
    ti = lax.broadcasted_iota(jnp.int32, (c, c), 0)
    si = lax.broadcasted_iota(jnp.int32, (c, c), 1)
    tril = jnp.where(si <= ti, 1.0, 0.0).astype(BF16)
    levels = (32, 16, 8, 4, 2, 1)
    hc = HG_HEADS * c
    tb = lax.broadcasted_iota(jnp.int32, (hc, hc), 0)
    sb = lax.broadcasted_iota(jnp.int32, (hc, hc), 1)
    same_head = (tb // c) == (sb // c)
    split_bit = tb ^ sb
    pair_mask = {m: jnp.where(same_head & (tb > sb) & (split_bit >= m) & (split_bit < 2 * m), 1.0, 0.0)
                 for m in levels}
    diagonal = tb == sb
    second_half = {m: (t_idx & m) != 0 for m in levels}
    sign = {m: jnp.where(second_half[m], 1.0, -1.0) for m in levels}

    def ref_rows(b, m):
        row = lambda r, n: jnp.broadcast_to(b[r:r + 1, :], (n, HG_DK))
        if m >= 4:
            return jnp.concatenate([row(s0 + m - 1, 2 * m) for s0 in range(0, c, 2 * m)], axis=0)
        return jnp.concatenate([jnp.where(sub < 4, row(s0 + 1, SUBLANES), row(s0 + 5, SUBLANES))
                                for s0 in range(0, c, SUBLANES)], axis=0)

    part = lambda which, rows, h: y[rows, which * HG_WIDTH + h * HG_DK:which * HG_WIDTH + (h + 1) * HG_DK]
    out = []
    for ci in range(y.shape[0] // c):
        rows = slice(ci * c, (ci + 1) * c)
        q, k, f, v, logf = [], [], [], [], []
        for h in range(HG_HEADS):
            lb = lb_all[:, h * HG_DK:(h + 1) * HG_DK]
            z = part(1, rows, h)
            hq = part(0, rows, h)
            q.append(hq * _sigmoid(hq))
            ez = jnp.exp(-jnp.abs(z))
            big = 1.0 / (1.0 + ez)
            small = ez * big
            f.append(jnp.maximum(lb + (1.0 - lb) * jnp.where(z >= 0.0, big, small), F_FLOOR))
            logf.append(jnp.log2(f[h]))
            k.append((1.0 - lb) * jnp.where(z >= 0.0, small, big))
            v.append(part(2, rows, h).astype(BF16))
        parts = jnp.concatenate([p for h in range(HG_HEADS) for p in _split3(logf[h])], axis=1)
        csum = _nn(tril, parts)
        bcum = [sum(csum[:, (3 * h + i) * HG_DK:(3 * h + i + 1) * HG_DK] for i in range(3))
                for h in range(HG_HEADS)]

        rowdot = jnp.concatenate([jnp.sum(q[h] * k[h], axis=-1, keepdims=True) for h in range(HG_HEADS)], axis=0)
        a = jnp.where(diagonal, rowdot, 0.0)
        for m in levels:
            r = []
            for h in range(HG_HEADS):
                if m == 1:
                    r.append(jnp.where(second_half[m], q[h] * f[h], k[h]))
                else:
                    w = jnp.exp2((bcum[h] - ref_rows(bcum[h], m)) * sign[m])
                    r.append(jnp.where(second_half[m], q[h], k[h]) * w)
            r = jnp.concatenate(r, axis=0).astype(BF16)
            a = a + _nt(r, r) * pair_mask[m]
        o_intra = _nn(a.astype(BF16), jnp.concatenate(v, axis=0))

        heads = []
        for h in range(HG_HEADS):
            st = st_scr[h]
            o = o_intra[h * c:(h + 1) * c] + _nt((q[h] * jnp.exp2(bcum[h])).astype(BF16), st.astype(BF16))
            b_last = bcum[h][c - 1:c, :]
            kd = (k[h] * jnp.exp2(b_last - bcum[h])).astype(BF16)
            st_scr[h] = jnp.exp2(b_last) * st + _tn(v[h], kd)

            o = o * lax.rsqrt(jnp.mean(o * o, axis=-1, keepdims=True) + EPS) * onorm
            gate = part(3, rows, h)
            heads.append((o * (gate * _sigmoid(gate))).astype(BF16))
        out.append(jnp.concatenate(heads, axis=1))
    return jnp.concatenate(out, axis=0)


def _mixer(h, l, b, s, mix_norm, w_ext, cmp_k, cmp_v, hgrn_lower_bound, hgrn_out_norm):
    q, ks, vs, kw, vw, kc_in, vc_in, gl, o_hg = _inproj(h, mix_norm, w_ext, hgrn_lower_bound, hgrn_out_norm, s, l)
    kc = _compress(kc_in.reshape(b, s, KV_W), l, *cmp_k)
    vc2 = _compress(vc_in.reshape(b, s, KV_W), l, *cmp_v)
    ocmp, sel, picks = _cmp_attention(q, kc, vc2, b, s)
    o_nsa = _attention(q, ks, vs, kw, vw, sel, picks, gl, ocmp, b, s)
    return o_nsa.reshape(b * s, NSA_WIDTH), o_hg


def kernel(x, ffn1_norm, ffn1_w_gu, ffn1_w_down, mix_norm, w_in, cmp_pos_k, cmp_pos_v, cmp_k_w1, cmp_k_w2, cmp_v_w1, cmp_v_w2, hgrn_lower_bound, hgrn_out_norm, w_out, ffn2_norm, ffn2_w_gu, ffn2_w_down, final_norm):
    b, s, d = x.shape
    depth = ffn1_norm.shape[0]
    ffn1_w = _ffn_weights(ffn1_w_gu, ffn1_w_down)
    ffn2_w = _ffn_weights(ffn2_w_gu, ffn2_w_down)
    w_out = w_out.astype(BF16)
    w_ext = _build_w_in(w_in)
    cmp_k = _compress_weights(cmp_pos_k, cmp_k_w1, cmp_k_w2, (0, 1))
    cmp_v = _compress_weights(cmp_pos_v, cmp_v_w1, cmp_v_w2, (0, 1, 1, 0))
    h = x.reshape(b * s, d)
    for l in range(depth):
        h = _ffn(h, l, ffn1_norm, *ffn1_w, final_norm, False)
        o_nsa, o_hg = _mixer(h, l, b, s, mix_norm, w_ext, cmp_k, cmp_v, hgrn_lower_bound, hgrn_out_norm)
        h = _ffn(h, l, ffn2_norm, *ffn2_w, final_norm, l == depth - 1, (o_nsa, o_hg, w_out))
    return h.reshape(b, s, d)
```

```python
import functools

import jax
import jax.numpy as jnp
import numpy as np
from jax import lax
from jax.experimental import pallas as pl
from jax.experimental.pallas import tpu as pltpu

F32 = jnp.float32
BF16 = jnp.bfloat16

D_MODEL = 1024
EPS = 1e-6
NEG = -1e30
FORCE = 1e6
F_FLOOR = 1e-30
NSA_HEADS = 8
NSA_KV_GROUPS = 2
HEADS_PER_GROUP = NSA_HEADS // NSA_KV_GROUPS
NSA_HD = 64
CMP_LEN = 32
CMP_STRIDE = 16
CMP_HID = 256
SEL_BLOCK = 64
SEL_TOPK = 16
WINDOW = 512
HG_HEADS = 4
HG_DK = 128
HG_DV = 128
HG_CHUNK = 64
D_FF = 2752
NSA_WIDTH = NSA_HEADS * NSA_HD
HG_WIDTH = HG_HEADS * HG_DV
KV_W = NSA_KV_GROUPS * NSA_HD

LANES = 128
SUBLANES = 8
D_FF_PAD = 2816
FF_CHUNK = 256
ROW_TILE = 512
FFN_ROWS = 1024
ATT_TQ = 256
ATT_TK = 256
CMP_TQ = 1024
VMEM_LIMIT = 56 * 1024 * 1024

SEL_PENALTY = 2.0 ** 50
ALIBI_SLOPES = tuple(2.0 ** (-8.0 * (i + 1) / NSA_HEADS) for i in range(NSA_HEADS))

SEGMENTS = (
    ("q", NSA_HEADS * LANES, BF16, NSA_WIDTH),
    ("ks", NSA_KV_GROUPS * LANES, BF16, KV_W),
    ("vs", NSA_KV_GROUPS * LANES, BF16, KV_W),
    ("kw", NSA_KV_GROUPS * LANES, BF16, KV_W),
    ("vw", NSA_KV_GROUPS * LANES, BF16, KV_W),
    ("kc", KV_W, F32, KV_W),
    ("vc", KV_W, F32, KV_W),
    ("gl", LANES, F32, LANES),
    ("hg", HG_WIDTH, BF16, 4 * HG_WIDTH),
)
SEG_OFFSETS = tuple(int(v) for v in np.cumsum([0] + [s[3] for s in SEGMENTS]))


def _nn(a, b):
    return jnp.dot(a, b, preferred_element_type=F32)


def _nt(a, b):
    return lax.dot_general(a, b, (((1,), (1,)), ((), ())), preferred_element_type=F32)


def _tn(a, b):
    return lax.dot_general(a, b, (((0,), (0,)), ((), ())), preferred_element_type=F32)


def _split2(x):
    hi = x.astype(BF16)
    lo = (x - hi.astype(F32)).astype(BF16)
    return hi, lo


def _split3(x):
    hi = x.astype(BF16)
    r = x - hi.astype(F32)
    mid = r.astype(BF16)
    lo = (r - mid.astype(F32)).astype(BF16)
    return hi, mid, lo


def _dot3(a, b):
    ah, al = _split2(a)
    bh, bl = _split2(b)
    return _nn(ah, bh) + _nn(ah, bl) + _nn(al, bh)


def _sigmoid(x):
    return 1.0 / (1.0 + jnp.exp(-x))


def _rms(x, g):
    return x * lax.rsqrt(jnp.mean(x * x, axis=-1, keepdims=True) + EPS) * g


def _resident(shape):
    nd = len(shape)
    return pl.BlockSpec(shape, lambda *_: (0,) * nd, pipeline_mode=pl.Buffered(1))


def _resident_layer(stacked_shape, layer):
    nd = len(stacked_shape)
    return pl.BlockSpec((None,) + tuple(stacked_shape[1:]), lambda *_: (layer,) + (0,) * (nd - 1),
                        pipeline_mode=pl.Buffered(1))


def _params(sem):
    return pltpu.CompilerParams(dimension_semantics=sem, vmem_limit_bytes=VMEM_LIMIT)


def _ffn_kernel(*refs, final, proj):
    if proj:
        x_ref, a_ref, b_ref, wo_ref, g_ref, wg_ref, wu_ref, wd_ref, gf_ref, o_ref = refs
        x = x_ref[...] + _nn(a_ref[...], wo_ref[:NSA_WIDTH, :]) + _nn(b_ref[...], wo_ref[NSA_WIDTH:, :])
    else:
        x_ref, g_ref, wg_ref, wu_ref, wd_ref, gf_ref, o_ref = refs
        x = x_ref[...]
    xn = _rms(x, g_ref[...]).astype(BF16)
    acc = jnp.zeros(x.shape, F32)
    for k in range(D_FF_PAD // FF_CHUNK):
        sl = slice(k * FF_CHUNK, (k + 1) * FF_CHUNK)
        gate = _nn(xn, wg_ref[:, sl])
        up = _nn(xn, wu_ref[:, sl])
        h = (gate * _sigmoid(gate) * up).astype(BF16)
        acc = acc + _nn(h, wd_ref[sl, :])
    y = x + 0.5 * acc
    if final:
        y = _rms(y, gf_ref[...])
    o_ref[...] = y


def _ffn_weights(w_gu, w_down):
    pad = D_FF_PAD - D_FF
    wg = jnp.pad(w_gu[..., :D_FF], ((0, 0), (0, 0), (0, pad))).astype(BF16)
    wu = jnp.pad(w_gu[..., D_FF:], ((0, 0), (0, 0), (0, pad))).astype(BF16)
    wd = jnp.pad(w_down, ((0, 0), (0, pad), (0, 0))).astype(BF16)
    return wg, wu, wd


def _ffn(h, layer, norm_g, wg, wu, wd, final_g, final, proj=None):
    t = h.shape[0]
    row = lambda w: pl.BlockSpec((FFN_ROWS, w), lambda i: (i, 0))
    norm_g = norm_g[:, None, :]
    weights = [_resident_layer(a.shape, layer) for a in (norm_g, wg, wu, wd)] + [_resident((1, D_MODEL))]
    operands = [norm_g, wg, wu, wd, final_g.reshape(1, -1)]
    if proj is None:
        in_specs, args = [row(D_MODEL)] + weights, [h] + operands
    else:
        o_nsa, o_hg, w_out = proj
        in_specs = [row(D_MODEL), row(NSA_WIDTH), row(HG_WIDTH), _resident_layer(w_out.shape, layer)] + weights
        args = [h, o_nsa, o_hg, w_out] + operands
    return pl.pallas_call(
        functools.partial(_ffn_kernel, final=final, proj=proj is not None),
        grid=(t // FFN_ROWS,),
        in_specs=in_specs,
        out_specs=row(D_MODEL),
        out_shape=jax.ShapeDtypeStruct((t, D_MODEL), F32),
        compiler_params=_params(("parallel",)),
        name="ffn",
    )(*args)


def _inproj_kernel(x_ref, g_ref, w_ref, qfeat_ref, lbraw_ref, onorm_ref, *refs, seq, layer):
    *o_refs, st_scr = refs
    xn = _rms(x_ref[...], g_ref[...]).astype(BF16)
    rows = x_ref.shape[0]

    @pl.when(pl.program_id(0) % (seq // rows) == 0)
    def _():
        st_scr[...] = jnp.zeros(st_scr.shape, F32)

    kpos = (pl.program_id(0) % (seq // rows)) * rows + lax.broadcasted_iota(jnp.int32, (rows, LANES), 0)
    lane = lax.broadcasted_iota(jnp.int32, (rows, LANES), 1)
    blk = kpos // SEL_BLOCK
    digit = lambda f0: jnp.where(lane == f0, blk.astype(F32),
                                 jnp.where(lane == f0 + 1, (kpos % SEL_BLOCK).astype(F32), 0.0))
    halves = [NSA_HD * (1 - g) for g in range(NSA_KV_GROUPS)]
    kfeat = jnp.concatenate([digit(f0) for f0 in halves], axis=1)
    is_pen = lambda f0: (lane == f0 + 1 + blk) & (blk >= 1) & (blk <= seq // SEL_BLOCK - 2)
    kpen = jnp.concatenate([jnp.where(is_pen(f0), -SEL_PENALTY, 0.0) for f0 in halves], axis=1)
    lower = lane < NSA_HD
    swap = lambda a: pltpu.roll(a, NSA_HD, axis=1)
    small = [i for i, seg in enumerate(SEGMENTS) if seg[3] == LANES]
    assert small == list(range(small[0], small[-1] + 1))
    y_small = _nn(xn, w_ref[:, SEG_OFFSETS[small[0]]:SEG_OFFSETS[small[-1] + 1]])
    for i, (o_ref, (name, _, dtype, cols)) in enumerate(zip(o_refs, SEGMENTS, strict=True)):
        if i in small:
            y = y_small[:, (i - small[0]) * LANES:(i - small[0] + 1) * LANES]
        else:
            y = _nn(xn, w_ref[:, SEG_OFFSETS[i]:SEG_OFFSETS[i] + cols])
        if name == "q":
            blocks = []
            for h in range(NSA_HEADS):
                pair = y[:, (h // 2) * LANES:(h // 2 + 1) * LANES]
                g = h // HEADS_PER_GROUP
                data = pair if h % 2 == g else swap(pair)
                feat = qfeat_ref[:, h * LANES:(h + 1) * LANES]
                blocks.append(jnp.where(lower == (g == 0), data, feat))
            y = jnp.concatenate(blocks, axis=1)
        elif name in ("ks", "kw"):
            feat = kfeat + kpen if name == "ks" else kfeat
            y = jnp.concatenate([jnp.where(lower == (g == 0), y, feat[:, g * LANES:(g + 1) * LANES])
                                 for g in range(NSA_KV_GROUPS)], axis=1)
        elif name in ("vs", "vw"):
            y = jnp.concatenate([jnp.where(lower, y if g == 0 else swap(y), 1.0)
                                 for g in range(NSA_KV_GROUPS)], axis=1)
        elif name == "hg":
            y = _hgrn_rows(y, lbraw_ref[...], onorm_ref[...], st_scr, layer)
        o_ref[...] = y.astype(dtype)


def _q_features():
    feat = np.zeros((1, NSA_HEADS * LANES), np.float32)
    for h in range(NSA_HEADS):
        f0 = h * LANES + NSA_HD * (1 - h // HEADS_PER_GROUP)
        feat[0, f0] = SEL_BLOCK * ALIBI_SLOPES[h]
        feat[0, f0 + 1] = ALIBI_SLOPES[h]
    return jnp.asarray(feat)


def _build_w_in(w_in):
    sizes = (NSA_WIDTH, KV_W, KV_W, KV_W, KV_W, KV_W, KV_W, NSA_HEADS * 3,
             HG_WIDTH, HG_WIDTH, HG_WIDTH, HG_WIDTH)
    splits = [int(v) for v in np.cumsum(sizes)[:-1]]
    w_in = w_in.astype(BF16)
    wq, wkc, wvc, wks, wvs, wkw, wvw, wgl, whq, whf, whi, whg = jnp.split(w_in, splits, axis=-1)
    lead = w_in.shape[:-1]
    gl = jnp.swapaxes(wgl.reshape(lead + (NSA_HEADS, 3)), -1, -2).reshape(lead + (3 * NSA_HEADS,))
    gl = jnp.pad(gl, [(0, 0)] * len(lead) + [(0, LANES - 3 * NSA_HEADS)])
    cols = [wq * NSA_HD ** -0.5, wks, wvs, wkw, wvw, wkc, wvc, gl, whq, whf, whi, whg]
    return jnp.concatenate(cols, axis=-1)


def _inproj(h, norm_g, w_ext, lb_raw, out_norm, seq, layer):
    t = h.shape[0]
    assert seq % ROW_TILE == 0 and ROW_TILE % HG_CHUNK == 0
    row = lambda w: pl.BlockSpec((ROW_TILE, w), lambda i: (i, 0))
    norm_g, out_norm = norm_g[:, None, :], out_norm[:, None, :]
    return pl.pallas_call(
        functools.partial(_inproj_kernel, seq=seq, layer=layer),
        grid=(t // ROW_TILE,),
        in_specs=[row(D_MODEL), _resident_layer(norm_g.shape, layer), _resident_layer(w_ext.shape, layer),
                  _resident((1, NSA_HEADS * LANES)), _resident(lb_raw.shape),
                  _resident_layer(out_norm.shape, layer)],
        out_specs=[row(w) for _, w, _, _ in SEGMENTS],
        out_shape=[jax.ShapeDtypeStruct((t, w), dt) for _, w, dt, _ in SEGMENTS],
        scratch_shapes=[pltpu.VMEM((HG_HEADS, HG_DV, HG_DK), F32)],
        compiler_params=_params(("arbitrary",)),
        name="inproj_hgrn",
    )(h, norm_g, w_ext, _q_features(), lb_raw, out_norm)


def _gelu_tanh(x):
    return 0.5 * x * (1.0 + jnp.tanh(0.7978845608028654 * (x + 0.044715 * (x * x * x))))


def _compress_kernel(kv_ref, pos_ref, w1_ref, w2_ref, o_ref):
    nbp = o_ref.shape[1]
    hid = NSA_KV_GROUPS * CMP_HID
    ha = jnp.zeros((nbp, hid), F32)
    hb = jnp.zeros((nbp, hid), F32)
    for l in range(CMP_STRIDE):
        x = kv_ref[0, pl.ds(l, nbp, stride=CMP_STRIDE), :]
        ha = ha + _nn((x + pos_ref[l:l + 1, :]).astype(BF16), w1_ref[l])
        hb = hb + _nn((x + pos_ref[CMP_STRIDE + l:CMP_STRIDE + l + 1, :]).astype(BF16), w1_ref[CMP_STRIDE + l])
    act = _gelu_tanh(ha + pltpu.roll(hb, nbp - 1, axis=0)).astype(BF16)
    out = jnp.zeros(o_ref.shape[1:], F32)
    for g in range(NSA_KV_GROUPS):
        out = out + _nn(act[:, g * CMP_HID:(g + 1) * CMP_HID], w2_ref[g])
    o_ref[0] = out


def _compress_weights(pos, w1, w2, reps):
    layers = w1.shape[0]
    w1l = w1.astype(BF16).reshape(layers, CMP_LEN, NSA_HD, CMP_HID)
    z1 = jnp.zeros_like(w1l)
    w1p = jnp.concatenate([jnp.concatenate([w1l, z1], axis=3), jnp.concatenate([z1, w1l], axis=3)], axis=2)
    pos2 = jnp.concatenate([pos] * NSA_KV_GROUPS, axis=2)
    w2 = w2.astype(BF16)
    zero = jnp.zeros_like(w2)
    w2p = jnp.stack([jnp.concatenate([w2 if r == g else zero for r in reps], axis=2)
                     for g in range(NSA_KV_GROUPS)], axis=1)
    return pos2, w1p, w2p


def _compress(kv, layer, pos2, w1p, w2p):
    b, s, _ = kv.shape
    nbp = s // CMP_STRIDE
    width = w2p.shape[-1]
    return pl.pallas_call(
        _compress_kernel,
        grid=(b,),
        in_specs=[pl.BlockSpec((1, s, KV_W), lambda i: (i, 0, 0))]
        + [_resident_layer(a.shape, layer) for a in (pos2, w1p, w2p)],
        out_specs=pl.BlockSpec((1, nbp, width), lambda i: (i, 0, 0)),
        out_shape=jax.ShapeDtypeStruct((b, nbp, width), F32),
        compiler_params=_params(("parallel",)),
        name="compress",
    )(kv, pos2, w1p, w2p)


def _pair_blocks(g, jj):
    lower = slice(0, LANES) if g == 0 else slice(LANES, 2 * LANES)
    upper = slice(LANES, 2 * LANES) if g == 0 else slice(0, LANES)
    return lower, upper


def _cmp_kernel(q_ref, kc_ref, vc2_ref, mt_ref, ocmp_ref, sel_ref, picks_ref, *, tq, ns, n_sel):
    nbp = kc_ref.shape[1]
    q0 = pl.program_id(1) * tq
    pos = q0 + lax.broadcasted_iota(jnp.int32, (tq, nbp), 0)
    blk_end = lax.broadcasted_iota(jnp.int32, (tq, nbp), 1) * CMP_STRIDE + (CMP_LEN - 1)
    valid = blk_end <= pos
    row_ok = (q0 + lax.broadcasted_iota(jnp.int32, (tq, 1), 0)) >= CMP_LEN - 1
    kc = kc_ref[0].astype(BF16)
    vc2 = vc2_ref[0].astype(BF16)
    lane_c = lax.broadcasted_iota(jnp.int32, (nbp, LANES), 1)
    c_idx = lax.broadcasted_iota(jnp.int32, (nbp, LANES), 0)
    per = SEL_BLOCK // CMP_STRIDE
    feat_a = (c_idx // per - q0 // SEL_BLOCK).astype(F32)
    feat_b = ((c_idx % per) * CMP_STRIDE + (CMP_LEN - 1)).astype(F32)
    lane_lo = lax.broadcasted_iota(jnp.int32, (tq, LANES), 1) < NSA_HD
    blk_f = lax.broadcasted_iota(jnp.int32, (ns, tq), 0).astype(F32)
    blk = lax.broadcasted_iota(jnp.int32, (ns, tq), 0)
    pos_t = q0 + lax.broadcasted_iota(jnp.int32, (ns, tq), 1)
    cur = pos_t // SEL_BLOCK
    forced = (blk == 0) | (blk == cur) | (blk == cur - 1)
    causal = blk * SEL_BLOCK <= pos_t
    sel_t, picks = [], []
    for g in range(NSA_KV_GROUPS):
        f0 = NSA_HD * (1 - g)
        feat = jnp.where(lane_c == f0, feat_a, jnp.where(lane_c == f0 + 1, feat_b, 0.0))
        kc_g = jnp.where((lane_c // NSA_HD) == g, kc, feat.astype(BF16))
        imp = jnp.zeros((tq, nbp), F32)
        acc = []
        for hh in range(HEADS_PER_GROUP):
            h = g * HEADS_PER_GROUP + hh
            s = _nt(q_ref[0, :, h * LANES:(h + 1) * LANES], kc_g)
            s = jnp.where(valid, s, NEG)
            e = jnp.exp(s - jnp.max(s, axis=-1, keepdims=True))
            inv = jnp.where(row_ok, 1.0 / jnp.sum(e, axis=-1, keepdims=True), 0.0)
            p = e * inv
            imp = imp + p
            acc.append(_nn(p.astype(BF16), vc2))
        for jj in range(HEADS_PER_GROUP // 2):
            lower, upper = _pair_blocks(g, jj)
            blk_out = jnp.where(lane_lo, acc[2 * jj][:, lower], acc[2 * jj + 1][:, upper])
            c0 = (g * HEADS_PER_GROUP + 2 * jj) * NSA_HD
            ocmp_ref[0, :, c0:c0 + LANES] = blk_out
        p_slc = sum(_nt(mt_ref[...], part) for part in _split3(imp))
        taken = -jnp.inf
        score = jnp.where(forced, taken, jnp.where(causal, p_slc, NEG))
        for _ in range(n_sel - 3):
            top = jnp.max(score, axis=0, keepdims=True)
            first = jnp.min(jnp.where(score == top, blk_f, float(ns)), axis=0, keepdims=True)
            score = jnp.where(blk_f == first, taken, score)
        picked = jnp.where(score == taken, 1.0, 0.0).astype(BF16)
        picks.append([_nt(jnp.ones((SUBLANES, ATT_TQ), BF16), picked[:, t * ATT_TQ:(t + 1) * ATT_TQ])
                      for t in range(tq // ATT_TQ)])
        pen = jnp.where(score == taken, 0.0, SEL_PENALTY)
        pen = jnp.where((blk >= 2) & (blk <= ns - 1), pltpu.roll(pen, 1, axis=0), 0.0)
        if ns < NSA_HD:
            pen = jnp.concatenate([pen, jnp.zeros((NSA_HD - ns, tq), F32)], axis=0)
        zero = jnp.zeros((NSA_HD, tq), F32)
        sel_t += [zero, pen] if g == 0 else [pen, zero]
    sel_ref[0] = jnp.concatenate(sel_t, axis=0).T.astype(BF16)
    for t in range(tq // ATT_TQ):
        picks_ref[0, t] = jnp.concatenate([group[t] for group in picks], axis=1)


def _importance_matrix(nbp, ns):
    per = SEL_BLOCK // CMP_STRIDE
    m = np.zeros((ns, nbp), np.float32)
    for n in range(ns):
        for c in range(per * n, per * (n + 1)):
            for cc in (c - 1, c):
                if 0 <= cc < nbp - 1:
                    m[n, cc] += 1.0
    return jnp.asarray(m, BF16)


def _cmp_attention(q, kc, vc2, b, s):
    nbp = s // CMP_STRIDE
    ns = s // SEL_BLOCK
    tq = min(CMP_TQ, s)
    assert ns <= NSA_HD
    assert min(SEL_TOPK, ns) >= 3 and tq % ATT_TQ == 0
    q3 = q.reshape(b, s, NSA_HEADS * LANES)
    kern = functools.partial(_cmp_kernel, tq=tq, ns=ns, n_sel=min(SEL_TOPK, ns))
    return pl.pallas_call(
        kern,
        grid=(b, s // tq),
        in_specs=[pl.BlockSpec((1, tq, NSA_HEADS * LANES), lambda i, j: (i, j, 0)),
                  pl.BlockSpec((1, nbp, KV_W), lambda i, j: (i, 0, 0)),
                  pl.BlockSpec((1, nbp, 2 * KV_W), lambda i, j: (i, 0, 0)),
                  _resident((ns, nbp))],
        out_specs=[pl.BlockSpec((1, tq, NSA_WIDTH), lambda i, j: (i, j, 0)),
                   pl.BlockSpec((1, tq, NSA_KV_GROUPS * LANES), lambda i, j: (i, j, 0)),
                   pl.BlockSpec((1, tq // ATT_TQ, SUBLANES, NSA_KV_GROUPS * ns), lambda i, j: (i, j, 0, 0))],
        out_shape=[jax.ShapeDtypeStruct((b, s, NSA_WIDTH), F32),
                   jax.ShapeDtypeStruct((b, s, NSA_KV_GROUPS * LANES), BF16),
                   jax.ShapeDtypeStruct((b, s // ATT_TQ, SUBLANES, NSA_KV_GROUPS * ns), F32)],
        compiler_params=_params(("parallel", "parallel")),
        name="cmp_topk",
    )(q3, kc, vc2, _importance_matrix(nbp, ns))


def _attn_kernel(tiles_ref, ntiles_ref, q_ref, ks_ref, vs_ref, kw_ref, vw_ref, penq_ref, gl_ref, gexp_ref,
                 ocmp_ref, o_ref, qa_scr, s_scr, p_scr, al_scr, m_scr, acc_scr, *, tq, tk):
    i = pl.program_id(1)
    rows_of = lambda hh: slice(hh * tq, (hh + 1) * tq)
    lane_lo = lax.broadcasted_iota(jnp.int32, (tq, LANES), 1) < NSA_HD
    rel = (lax.broadcasted_iota(jnp.int32, (tq, tk), 1)
           - lax.broadcasted_iota(jnp.int32, (tq, tk), 0)).astype(F32)

    for hh in range(HEADS_PER_GROUP):
        qa_scr[rows_of(hh)] = q_ref[0, :, hh * LANES:(hh + 1) * LANES] + penq_ref[0]

    def scores(k_ref, j, slot):
        s_scr[slot] = _nt(qa_scr[...], k_ref[0, pl.ds(pl.multiple_of(j * tk, tk), tk), :])

    def probs(br, j, mode, slot, first, exists=None):
        for hh in range(HEADS_PER_GROUP):
            r = rows_of(hh)
            s = s_scr[slot, r]
            if mode == "causal":
                s = jnp.where(rel <= 0.0, s, NEG)
            elif mode == "lower":
                s = jnp.where(rel + (j * tk - i * tq).astype(F32) > -float(WINDOW), s, NEG)
            if exists is not None:
                s = jnp.where(exists, s, NEG)
            m_cur = jnp.max(s, axis=-1, keepdims=True)
            if first:
                m_new = jnp.broadcast_to(m_cur, (tq, LANES))
            else:
                m_prev = m_scr[br, r]
                m_new = jnp.maximum(m_prev, m_cur)
                al_scr[slot, r] = jnp.exp(m_prev - m_new)
            m_scr[br, r] = m_new
            p_scr[slot, r] = jnp.exp(s - jnp.concatenate([m_new] * (tk // LANES), axis=1)).astype(BF16)

    def accumulate(br, v_ref, j, slot, first):
        pv = _nn(p_scr[slot], v_ref[0, pl.ds(pl.multiple_of(j * tk, tk), tk), :])
        for hh in range(HEADS_PER_GROUP):
            r = rows_of(hh)
            if first:
                acc_scr[br, r] = pv[r]
            else:
                acc_scr[br, r] = al_scr[slot, r] * acc_scr[br, r] + pv[r]

    step = (pl.program_id(0) * pl.num_programs(1) + i) * pl.num_programs(2) + pl.program_id(2)
    n_tiles = ntiles_ref[step]
    last = jnp.maximum(n_tiles - 1, 0)
    tile_of = lambda n: tiles_ref[step * pl.num_programs(1) + jnp.minimum(n - 1, last)]
    scores(ks_ref, i, 0)
    scores(ks_ref, tile_of(1), 1)
    probs(0, i, "causal", 0, True)
    scores(ks_ref, tile_of(2), 0)
    probs(0, tile_of(1), None, 1, False)
    accumulate(0, vs_ref, i, 0, True)

    def sel_step(n, parity):
        scores(ks_ref, tile_of(n + 2), parity)
        probs(0, tile_of(n + 1), None, 1 - parity, False)
        accumulate(0, vs_ref, tile_of(n), parity, False)

    def sel_two_steps(t, carry):
        sel_step(2 * t + 1, 1)
        sel_step(2 * t + 2, 0)
        return carry

    lax.fori_loop(0, n_tiles // 2, sel_two_steps, 0)

    @pl.when(n_tiles % 2 == 1)
    def _():
        sel_step(n_tiles, 1)

    n_back = WINDOW // tk
    back = [(jnp.maximum(i - d, 0), "lower" if d == n_back else None, i >= d) for d in range(1, n_back + 1)]
    scores(kw_ref, i, 0)
    scores(kw_ref, back[0][0], 1)
    probs(1, i, "causal", 0, True)
    accumulate(1, vw_ref, i, 0, True)
    for d, (j, mode, exists) in enumerate(back, start=1):
        if d < n_back:
            scores(kw_ref, back[d][0], (d + 1) % 2)
        probs(1, j, mode, d % 2, False, exists)
        accumulate(1, vw_ref, j, d % 2, False)

    gexp = sum(_nn(part, gexp_ref[0]) for part in _split2(_sigmoid(gl_ref[0])))
    n_pairs = HEADS_PER_GROUP // 2
    gate_blk = lambda br, jj: gexp[:, (br * n_pairs + jj) * LANES:(br * n_pairs + jj + 1) * LANES]
    for jj in range(n_pairs):
        cols = slice(jj * LANES, (jj + 1) * LANES)
        blk = gate_blk(0, jj) * ocmp_ref[0, :, cols]
        for br in range(2):
            even = acc_scr[br, rows_of(2 * jj)]
            odd = acc_scr[br, rows_of(2 * jj + 1)]
            low = even / pltpu.roll(even, NSA_HD, axis=1)
            up = pltpu.roll(odd, NSA_HD, axis=1) / odd
            blk = blk + gate_blk(br + 1, jj) * jnp.where(lane_lo, low, up)
        o_ref[0, :, cols] = blk.astype(o_ref.dtype)


def _gate_expansion():
    n_pairs = HEADS_PER_GROUP // 2
    r = np.zeros((NSA_KV_GROUPS, LANES, 3 * n_pairs * LANES), np.float32)
    for g in range(NSA_KV_GROUPS):
        for br in range(3):
            for jj in range(n_pairs):
                for odd in range(2):
                    src = br * NSA_HEADS + g * HEADS_PER_GROUP + 2 * jj + odd
                    dst = (br * n_pairs + jj) * LANES + odd * NSA_HD
                    r[g, src, dst:dst + NSA_HD] = 1.0
    return jnp.asarray(r, BF16)


def _picked_tiles(picks, b, s):
    ns, nq = s // SEL_BLOCK, s // ATT_TQ
    per_tile = ATT_TK // SEL_BLOCK
    per_block = picks[:, :, 0, :].reshape(b, nq, NSA_KV_GROUPS, ns // per_tile, per_tile)
    j = jnp.arange(ns // per_tile, dtype=jnp.int32)
    i = jnp.arange(nq, dtype=jnp.int32)[None, :, None, None]
    active = ((per_block.sum(-1) > 0) | (j == 0)) & (j < i)
    slot = jnp.cumsum(active, axis=-1) - 1
    hit = active[..., None, :] & (slot[..., None, :] == j[:, None])
    tiles = jnp.sum(jnp.where(hit, j, 0), axis=-1).astype(jnp.int32)
    return tiles.reshape(-1), active.sum(-1).astype(jnp.int32).reshape(-1)


def _attention(q, ks, vs, kw, vw, pen, picks, gl, ocmp, b, s):
    tq, tk = ATT_TQ, ATT_TK
    assert tq == tk and WINDOW % tk == 0 and WINDOW // tk <= 2 and tq % SEL_BLOCK == 0
    gw = HEADS_PER_GROUP * LANES
    ow = HEADS_PER_GROUP * NSA_HD
    r3 = lambda a: a.reshape(b, s, a.shape[-1])
    tile = lambda w: pl.BlockSpec((1, tq, w), lambda i, j, g, *_: (i, j, 0))
    gtile = lambda w: pl.BlockSpec((1, tq, w), lambda i, j, g, *_: (i, j, g))
    gfull = pl.BlockSpec((1, s, LANES), lambda i, j, g, *_: (i, 0, g))
    gexp = _gate_expansion()
    rows = HEADS_PER_GROUP * tq
    kern = functools.partial(_attn_kernel, tq=tq, tk=tk)
    tiles, n_tiles = _picked_tiles(picks, b, s)
    grid_spec = pltpu.PrefetchScalarGridSpec(
        num_scalar_prefetch=2,
        grid=(b, s // tq, NSA_KV_GROUPS),
        in_specs=[gtile(gw), gfull, gfull, gfull, gfull, gtile(LANES), tile(LANES),
                  pl.BlockSpec((1,) + gexp.shape[1:], lambda i, j, g, *_: (g, 0, 0)), gtile(ow)],
        out_specs=gtile(ow),
        scratch_shapes=[pltpu.VMEM((rows, LANES), BF16), pltpu.VMEM((2, rows, tk), F32),
                        pltpu.VMEM((2, rows, tk), BF16), pltpu.VMEM((2, rows, LANES), F32),
                        pltpu.VMEM((2, rows, LANES), F32), pltpu.VMEM((2, rows, LANES), F32)])
    return pl.pallas_call(
        kern,
        grid_spec=grid_spec,
        out_shape=jax.ShapeDtypeStruct((b, s, NSA_WIDTH), BF16),
        compiler_params=_params(("parallel", "parallel", "parallel")),
        name="sel_win_attention",
    )(tiles, n_tiles, r3(q), r3(ks), r3(vs), r3(kw), r3(vw), pen, r3(gl), gexp, ocmp)


def _hgrn_rows(y, raw, onorm, st_scr, layer):
    c = HG_CHUNK
    ex = jnp.exp(raw - jnp.max(raw, axis=0, keepdims=True))
    sm = ex / jnp.sum(ex, axis=0, keepdims=True)
    lb_all = jnp.zeros((1, raw.shape[1]), F32)
    for l in range(1, layer + 1):
        lb_all = lb_all + sm[l:l + 1, :]

    t_idx = lax.broadcasted_iota(jnp.int32, (c, HG_DK), 0)
    sub = lax.broadcasted_iota(jnp.int32, (SUBLANES, HG_DK), 0)
    ti = lax.broadcasted_iota(jnp.int32, (c, c), 0)
    si = lax.broadcasted_iota(jnp.int32, (c, c), 1)
    tril = jnp.where(si <= ti, 1.0, 0.0).astype(BF16)
    levels = (32, 16, 8, 4, 2, 1)
    hc = HG_HEADS * c
    tb = lax.broadcasted_iota(jnp.int32, (hc, hc), 0)
    sb = lax.broadcasted_iota(jnp.int32, (hc, hc), 1)
    same_head = (tb // c) == (sb // c)
    split_bit = tb ^ sb
    pair_mask = {m: jnp.where(same_head & (tb > sb) & (split_bit >= m) & (split_bit < 2 * m), 1.0, 0.0)
                 for m in levels}
    diagonal = tb == sb
    second_half = {m: (t_idx & m) != 0 for m in levels}
    sign = {m: jnp.where(second_half[m], 1.0, -1.0) for m in levels}

    def ref_rows(b, m):
        row = lambda r, n: jnp.broadcast_to(b[r:r + 1, :], (n, HG_DK))
        if m >= 4:
            return jnp.concatenate([row(s0 + m - 1, 2 * m) for s0 in range(0, c, 2 * m)], axis=0)
        return jnp.concatenate([jnp.where(sub < 4, row(s0 + 1, SUBLANES), row(s0 + 5, SUBLANES))
                                for s0 in range(0, c, SUBLANES)], axis=0)

    part = lambda which, rows, h: y[rows, which * HG_WIDTH + h * HG_DK:which * HG_WIDTH + (h + 1) * HG_DK]
    out = []
    for ci in range(y.shape[0] // c):
        rows = slice(ci * c, (ci + 1) * c)
        q, k, f, v, logf = [], [], [], [], []
        for h in range(HG_HEADS):
            lb = lb_all[:, h * HG_DK:(h + 1) * HG_DK]
            z = part(1, rows, h)
            hq = part(0, rows, h)
            q.append(hq * _sigmoid(hq))
            ez = jnp.exp(-jnp.abs(z))
            big = 1.0 / (1.0 + ez)
            small = ez * big
            f.append(jnp.maximum(lb + (1.0 - lb) * jnp.where(z >= 0.0, big, small), F_FLOOR))
            logf.append(jnp.log2(f[h]))
            k.append((1.0 - lb) * jnp.where(z >= 0.0, small, big))
            v.append(part(2, rows, h).astype(BF16))
        parts = jnp.concatenate([p for h in range(HG_HEADS) for p in _split3(logf[h])], axis=1)
        csum = _nn(tril, parts)
        bcum = [sum(csum[:, (3 * h + i) * HG_DK:(3 * h + i + 1) * HG_DK] for i in range(3))
                for h in range(HG_HEADS)]

        rowdot = jnp.concatenate([jnp.sum(q[h] * k[h], axis=-1, keepdims=True) for h in range(HG_HEADS)], axis=0)
        a = jnp.where(diagonal, rowdot, 0.0)
        for m in levels:
            r = []
            for h in range(HG_HEADS):
                if m == 1:
                    r.append(jnp.where(second_half[m], q[h] * f[h], k[h]))
                else:
                    w = jnp.exp2((bcum[h] - ref_rows(bcum[h], m)) * sign[m])
                    r.append(jnp.where(second_half[m], q[h], k[h]) * w)
            r = jnp.concatenate(r, axis=0).astype(BF16)
            a = a + _nt(r, r) * pair_mask[m]
        o_intra = _nn(a.astype(BF16), jnp.concatenate(v, axis=0))

        heads = []
        for h in range(HG_HEADS):
            st = st_scr[h]
            o = o_intra[h * c:(h + 1) * c] + _nt((q[h] * jnp.exp2(bcum[h])).astype(BF16), st.astype(BF16))
            b_last = bcum[h][c - 1:c, :]
            kd = (k[h] * jnp.exp2(b_last - bcum[h])).astype(BF16)
            st_scr[h] = jnp.exp2(b_last) * st + _tn(v[h], kd)

            o = o * lax.rsqrt(jnp.mean(o * o, axis=-1, keepdims=True) + EPS) * onorm
            gate = part(3, rows, h)
            heads.append((o * (gate * _sigmoid(gate))).astype(BF16))
        out.append(jnp.concatenate(heads, axis=1))
    return jnp.concatenate(out, axis=0)


def _mixer(h, l, b, s, mix_norm, w_ext, cmp_k, cmp_v, hgrn_lower_bound, hgrn_out_norm):
    q, ks, vs, kw, vw, kc_in, vc_in, gl, o_hg = _inproj(h, mix_norm, w_ext, hgrn_lower_bound, hgrn_out_norm, s, l)
    kc = _compress(kc_in.reshape(b, s, KV_W), l, *cmp_k)
    vc2 = _compress(vc_in.reshape(b, s, KV_W), l, *cmp_v)
    ocmp, sel, picks = _cmp_attention(q, kc, vc2, b, s)
    o_nsa = _attention(q, ks, vs, kw, vw, sel, picks, gl, ocmp, b, s)
    return o_nsa.reshape(b * s, NSA_WIDTH), o_hg


def kernel(x, ffn1_norm, ffn1_w_gu, ffn1_w_down, mix_norm, w_in, cmp_pos_k, cmp_pos_v, cmp_k_w1, cmp_k_w2, cmp_v_w1, cmp_v_w2, hgrn_lower_bound, hgrn_out_norm, w_out, ffn2_norm, ffn2_w_gu, ffn2_w_down, final_norm):
    b, s, d = x.shape
    depth = ffn1_norm.shape[0]
    ffn1_w = _ffn_weights(ffn1_w_gu, ffn1_w_down)
    ffn2_w = _ffn_weights(ffn2_w_gu, ffn2_w_down)
    w_out = w_out.astype(BF16)
    w_ext = _build_w_in(w_in)
    cmp_k = _compress_weights(cmp_pos_k, cmp_k_w1, cmp_k_w2, (0, 1))
    cmp_v = _compress_weights(cmp_pos_v, cmp_v_w1, cmp_v_w2, (0, 1, 1, 0))
    h = x.reshape(b * s, d)
    for l in range(depth):
        h = _ffn(h, l, ffn1_norm, *ffn1_w, final_norm, False)
        o_nsa, o_hg = _mixer(h, l, b, s, mix_norm, w_ext, cmp_k, cmp_v, hgrn_lower_bound, hgrn_out_norm)
        h = _ffn(h, l, ffn2_norm, *ffn2_w, final_norm, l == depth - 1, (o_nsa, o_hg, w_out))
    return h.reshape(b, s, d)
```

```python
import functools

import jax
import jax.numpy as jnp
import numpy as np
from jax import lax
from jax.experimental import pallas as pl
from jax.experimental.pallas import tpu as pltpu

F32 = jnp.float32
BF16 = jnp.bfloat16

D_MODEL = 1024
EPS = 1e-6
NEG = -1e30
F_FLOOR = 1e-30
NSA_HEADS = 8
NSA_KV_GROUPS = 2
HEADS_PER_GROUP = NSA_HEADS // NSA_KV_GROUPS
NSA_HD = 64
CMP_LEN = 32
CMP_STRIDE = 16
CMP_HID = 256
SEL_BLOCK = 64
SEL_TOPK = 16
WINDOW = 512
HG_HEADS = 4
HG_DK = 128
HG_DV = 128
HG_CHUNK = 64
D_FF = 2752
NSA_WIDTH = NSA_HEADS * NSA_HD
HG_WIDTH = HG_HEADS * HG_DV
KV_W = NSA_KV_GROUPS * NSA_HD

LANES = 128
SUBLANES = 8
FF_CHUNK = 256
D_FF_PAD = -(-D_FF // FF_CHUNK) * FF_CHUNK
ROW_TILE = 512
FFN_ROWS = 1024
ATT_TQ = 256
ATT_TK = 256
CMP_TQ = 1024
VMEM_LIMIT = 56 * 1024 * 1024

SEL_PENALTY = 2.0 ** 50
ALIBI_SLOPES = tuple(2.0 ** (-8.0 * (i + 1) / NSA_HEADS) for i in range(NSA_HEADS))

SEGMENTS = (
    ("q", NSA_HEADS * LANES, BF16, NSA_WIDTH),
    ("ks", NSA_KV_GROUPS * LANES, BF16, KV_W),
    ("vs", NSA_KV_GROUPS * LANES, BF16, KV_W),
    ("kw", NSA_KV_GROUPS * LANES, BF16, KV_W),
    ("vw", NSA_KV_GROUPS * LANES, BF16, KV_W),
    ("kc", KV_W, F32, KV_W),
    ("vc", KV_W, F32, KV_W),
    ("gl", LANES, F32, LANES),
    ("hg", HG_WIDTH, BF16, 4 * HG_WIDTH),
)
SEG_OFFSETS = tuple(int(v) for v in np.cumsum([0] + [s[3] for s in SEGMENTS]))


def _nn(a, b):
    return jnp.dot(a, b, preferred_element_type=F32)


def _nt(a, b):
    return lax.dot_general(a, b, (((1,), (1,)), ((), ())), preferred_element_type=F32)


def _tn(a, b):
    return lax.dot_general(a, b, (((0,), (0,)), ((), ())), preferred_element_type=F32)


def _split2(x):
    hi = x.astype(BF16)
    lo = (x - hi.astype(F32)).astype(BF16)
    return hi, lo


def _split3(x):
    hi = x.astype(BF16)
    r = x - hi.astype(F32)
    mid = r.astype(BF16)
    lo = (r - mid.astype(F32)).astype(BF16)
    return hi, mid, lo


def _sigmoid(x):
    return 1.0 / (1.0 + jnp.exp(-x))


def _rms(x, g):
    return x * lax.rsqrt(jnp.mean(x * x, axis=-1, keepdims=True) + EPS) * g


def _resident(shape):
    nd = len(shape)
    return pl.BlockSpec(shape, lambda *_: (0,) * nd, pipeline_mode=pl.Buffered(1))


def _resident_layer(stacked_shape, layer):
    nd = len(stacked_shape)
    return pl.BlockSpec((None,) + tuple(stacked_shape[1:]), lambda *_: (layer,) + (0,) * (nd - 1),
                        pipeline_mode=pl.Buffered(1))


def _params(sem):
    return pltpu.CompilerParams(dimension_semantics=sem, vmem_limit_bytes=VMEM_LIMIT)


def _ffn_kernel(*refs, final, proj):
    if proj:
        x_ref, a_ref, b_ref, wo_ref, g_ref, wg_ref, wu_ref, wd_ref, gf_ref, o_ref = refs
        x = x_ref[...] + _nn(a_ref[...], wo_ref[:NSA_WIDTH, :]) + _nn(b_ref[...], wo_ref[NSA_WIDTH:, :])
    else:
        x_ref, g_ref, wg_ref, wu_ref, wd_ref, gf_ref, o_ref = refs
        x = x_ref[...]
    xn = _rms(x, g_ref[...]).astype(BF16)
    acc = jnp.zeros(x.shape, F32)
    for k in range(D_FF_PAD // FF_CHUNK):
        sl = slice(k * FF_CHUNK, (k + 1) * FF_CHUNK)
        gate = _nn(xn, wg_ref[:, sl])
        up = _nn(xn, wu_ref[:, sl])
        h = (gate * _sigmoid(gate) * up).astype(BF16)
        acc = acc + _nn(h, wd_ref[sl, :])
    y = x + 0.5 * acc
    if final:
        y = _rms(y, gf_ref[...])
    o_ref[...] = y


def _ffn_weights(w_gu, w_down):
    pad = D_FF_PAD - D_FF
    wg = jnp.pad(w_gu[..., :D_FF], ((0, 0), (0, 0), (0, pad))).astype(BF16)
    wu = jnp.pad(w_gu[..., D_FF:], ((0, 0), (0, 0), (0, pad))).astype(BF16)
    wd = jnp.pad(w_down, ((0, 0), (0, pad), (0, 0))).astype(BF16)
    return wg, wu, wd


def _ffn(h, layer, norm_g, wg, wu, wd, final_g, final, proj=None):
    t = h.shape[0]
    row = lambda w: pl.BlockSpec((FFN_ROWS, w), lambda i: (i, 0))
    norm_g = norm_g[:, None, :]
    weights = [_resident_layer(a.shape, layer) for a in (norm_g, wg, wu, wd)] + [_resident((1, D_MODEL))]
    operands = [norm_g, wg, wu, wd, final_g.reshape(1, -1)]
    if proj is None:
        in_specs, args = [row(D_MODEL)] + weights, [h] + operands
    else:
        o_nsa, o_hg, w_out = proj
        in_specs = [row(D_MODEL), row(NSA_WIDTH), row(HG_WIDTH), _resident_layer(w_out.shape, layer)] + weights
        args = [h, o_nsa, o_hg, w_out] + operands
    return pl.pallas_call(
        functools.partial(_ffn_kernel, final=final, proj=proj is not None),
        grid=(t // FFN_ROWS,),
        in_specs=in_specs,
        out_specs=row(D_MODEL),
        out_shape=jax.ShapeDtypeStruct((t, D_MODEL), F32),
        compiler_params=_params(("parallel",)),
        name="ffn",
    )(*args)


def _inproj_kernel(x_ref, g_ref, w_ref, qfeat_ref, lbraw_ref, onorm_ref, *refs, seq, layer):
    *o_refs, st_scr = refs
    xn = _rms(x_ref[...], g_ref[...]).astype(BF16)
    rows = x_ref.shape[0]

    @pl.when(pl.program_id(0) % (seq // rows) == 0)
    def _():
        st_scr[...] = jnp.zeros(st_scr.shape, F32)

    kpos = (pl.program_id(0) % (seq // rows)) * rows + lax.broadcasted_iota(jnp.int32, (rows, LANES), 0)
    lane = lax.broadcasted_iota(jnp.int32, (rows, LANES), 1)
    blk = kpos // SEL_BLOCK
    digit = lambda f0: jnp.where(lane == f0, blk.astype(F32),
                                 jnp.where(lane == f0 + 1, (kpos % SEL_BLOCK).astype(F32), 0.0))
    halves = [NSA_HD * (1 - g) for g in range(NSA_KV_GROUPS)]
    kfeat = jnp.concatenate([digit(f0) for f0 in halves], axis=1)
    is_pen = lambda f0: (lane == f0 + 1 + blk) & (blk >= 1) & (blk <= seq // SEL_BLOCK - 2)
    kpen = jnp.concatenate([jnp.where(is_pen(f0), -SEL_PENALTY, 0.0) for f0 in halves], axis=1)
    lower = lane < NSA_HD
    swap = lambda a: pltpu.roll(a, NSA_HD, axis=1)
    small = [i for i, seg in enumerate(SEGMENTS) if seg[3] == LANES]
    assert small == list(range(small[0], small[-1] + 1))
    y_small = _nn(xn, w_ref[:, SEG_OFFSETS[small[0]]:SEG_OFFSETS[small[-1] + 1]])
    for i, (o_ref, (name, _, dtype, cols)) in enumerate(zip(o_refs, SEGMENTS, strict=True)):
        if i in small:
            y = y_small[:, (i - small[0]) * LANES:(i - small[0] + 1) * LANES]
        else:
            y = _nn(xn, w_ref[:, SEG_OFFSETS[i]:SEG_OFFSETS[i] + cols])
        if name == "q":
            blocks = []
            for h in range(NSA_HEADS):
                pair = y[:, (h // 2) * LANES:(h // 2 + 1) * LANES]
                g = h // HEADS_PER_GROUP
                data = pair if h % 2 == g else swap(pair)
                feat = qfeat_ref[:, h * LANES:(h + 1) * LANES]
                blocks.append(jnp.where(lower == (g == 0), data, feat))
            y = jnp.concatenate(blocks, axis=1)
        elif name in ("ks", "kw"):
            feat = kfeat + kpen if name == "ks" else kfeat
            y = jnp.concatenate([jnp.where(lower == (g == 0), y, feat[:, g * LANES:(g + 1) * LANES])
                                 for g in range(NSA_KV_GROUPS)], axis=1)
        elif name in ("vs", "vw"):
            y = jnp.concatenate([jnp.where(lower, y if g == 0 else swap(y), 1.0)
                                 for g in range(NSA_KV_GROUPS)], axis=1)
        elif name == "hg":
            y = _hgrn_rows(y, lbraw_ref[...], onorm_ref[...], st_scr, layer)
        o_ref[...] = y.astype(dtype)


def _q_features():
    feat = np.zeros((1, NSA_HEADS * LANES), np.float32)
    for h in range(NSA_HEADS):
        f0 = h * LANES + NSA_HD * (1 - h // HEADS_PER_GROUP)
        feat[0, f0] = SEL_BLOCK * ALIBI_SLOPES[h]
        feat[0, f0 + 1] = ALIBI_SLOPES[h]
    return jnp.asarray(feat)


def _build_w_in(w_in):
    sizes = (NSA_WIDTH, KV_W, KV_W, KV_W, KV_W, KV_W, KV_W, NSA_HEADS * 3,
             HG_WIDTH, HG_WIDTH, HG_WIDTH, HG_WIDTH)
    splits = [int(v) for v in np.cumsum(sizes)[:-1]]
    w_in = w_in.astype(BF16)
    wq, wkc, wvc, wks, wvs, wkw, wvw, wgl, whq, whf, whi, whg = jnp.split(w_in, splits, axis=-1)
    lead = w_in.shape[:-1]
    gl = jnp.swapaxes(wgl.reshape(lead + (NSA_HEADS, 3)), -1, -2).reshape(lead + (3 * NSA_HEADS,))
    gl = jnp.pad(gl, [(0, 0)] * len(lead) + [(0, LANES - 3 * NSA_HEADS)])
    cols = [wq * NSA_HD ** -0.5, wks, wvs, wkw, wvw, wkc, wvc, gl, whq, whf, whi, whg]
    return jnp.concatenate(cols, axis=-1)


def _inproj(h, norm_g, w_ext, lb_raw, out_norm, seq, layer):
    t = h.shape[0]
    assert seq % ROW_TILE == 0 and ROW_TILE % HG_CHUNK == 0
    row = lambda w: pl.BlockSpec((ROW_TILE, w), lambda i: (i, 0))
    norm_g, out_norm = norm_g[:, None, :], out_norm[:, None, :]
    return pl.pallas_call(
        functools.partial(_inproj_kernel, seq=seq, layer=layer),
        grid=(t // ROW_TILE,),
        in_specs=[row(D_MODEL), _resident_layer(norm_g.shape, layer), _resident_layer(w_ext.shape, layer),
                  _resident((1, NSA_HEADS * LANES)), _resident(lb_raw.shape),
                  _resident_layer(out_norm.shape, layer)],
        out_specs=[row(w) for _, w, _, _ in SEGMENTS],
        out_shape=[jax.ShapeDtypeStruct((t, w), dt) for _, w, dt, _ in SEGMENTS],
        scratch_shapes=[pltpu.VMEM((HG_HEADS, HG_DV, HG_DK), F32)],
        compiler_params=_params(("arbitrary",)),
        name="inproj_hgrn",
    )(h, norm_g, w_ext, _q_features(), lb_raw, out_norm)


def _gelu_tanh(x):
    return 0.5 * x * (1.0 + jnp.tanh(0.7978845608028654 * (x + 0.044715 * (x * x * x))))


def _compress_kernel(kv_ref, pos_ref, w1_ref, w2_ref, o_ref):
    nbp = o_ref.shape[1]
    hid = NSA_KV_GROUPS * CMP_HID
    ha = jnp.zeros((nbp, hid), F32)
    hb = jnp.zeros((nbp, hid), F32)
    for l in range(CMP_STRIDE):
        x = kv_ref[0, pl.ds(l, nbp, stride=CMP_STRIDE), :]
        ha = ha + _nn((x + pos_ref[l:l + 1, :]).astype(BF16), w1_ref[l])
        hb = hb + _nn((x + pos_ref[CMP_STRIDE + l:CMP_STRIDE + l + 1, :]).astype(BF16), w1_ref[CMP_STRIDE + l])
    act = _gelu_tanh(ha + pltpu.roll(hb, nbp - 1, axis=0)).astype(BF16)
    out = jnp.zeros(o_ref.shape[1:], F32)
    for g in range(NSA_KV_GROUPS):
        out = out + _nn(act[:, g * CMP_HID:(g + 1) * CMP_HID], w2_ref[g])
    o_ref[0] = out


def _compress_weights(pos, w1, w2, reps):
    layers = w1.shape[0]
    w1l = w1.astype(BF16).reshape(layers, CMP_LEN, NSA_HD, CMP_HID)
    z1 = jnp.zeros_like(w1l)
    w1p = jnp.concatenate([jnp.concatenate([w1l, z1], axis=3), jnp.concatenate([z1, w1l], axis=3)], axis=2)
    pos2 = jnp.concatenate([pos] * NSA_KV_GROUPS, axis=2)
    w2 = w2.astype(BF16)
    zero = jnp.zeros_like(w2)
    w2p = jnp.stack([jnp.concatenate([w2 if r == g else zero for r in reps], axis=2)
                     for g in range(NSA_KV_GROUPS)], axis=1)
    return pos2, w1p, w2p


def _compress(kv, layer, pos2, w1p, w2p):
    b, s, _ = kv.shape
    nbp = s // CMP_STRIDE
    width = w2p.shape[-1]
    return pl.pallas_call(
        _compress_kernel,
        grid=(b,),
        in_specs=[pl.BlockSpec((1, s, KV_W), lambda i: (i, 0, 0))]
        + [_resident_layer(a.shape, layer) for a in (pos2, w1p, w2p)],
        out_specs=pl.BlockSpec((1, nbp, width), lambda i: (i, 0, 0)),
        out_shape=jax.ShapeDtypeStruct((b, nbp, width), F32),
        compiler_params=_params(("parallel",)),
        name="compress",
    )(kv, pos2, w1p, w2p)


def _pair_blocks(g, jj):
    lower = slice(0, LANES) if g == 0 else slice(LANES, 2 * LANES)
    upper = slice(LANES, 2 * LANES) if g == 0 else slice(0, LANES)
    return lower, upper


def _cmp_kernel(q_ref, kc_ref, vc2_ref, mt_ref, ocmp_ref, sel_ref, picks_ref, *, tq, ns, n_sel):
    nbp = kc_ref.shape[1]
    q0 = pl.program_id(1) * tq
    pos = q0 + lax.broadcasted_iota(jnp.int32, (tq, nbp), 0)
    blk_end = lax.broadcasted_iota(jnp.int32, (tq, nbp), 1) * CMP_STRIDE + (CMP_LEN - 1)
    valid = blk_end <= pos
    row_ok = (q0 + lax.broadcasted_iota(jnp.int32, (tq, 1), 0)) >= CMP_LEN - 1
    kc = kc_ref[0].astype(BF16)
    vc2 = vc2_ref[0].astype(BF16)
    lane_c = lax.broadcasted_iota(jnp.int32, (nbp, LANES), 1)
    c_idx = lax.broadcasted_iota(jnp.int32, (nbp, LANES), 0)
    per = SEL_BLOCK // CMP_STRIDE
    feat_a = (c_idx // per - q0 // SEL_BLOCK).astype(F32)
    feat_b = ((c_idx % per) * CMP_STRIDE + (CMP_LEN - 1)).astype(F32)
    lane_lo = lax.broadcasted_iota(jnp.int32, (tq, LANES), 1) < NSA_HD
    blk_f = lax.broadcasted_iota(jnp.int32, (ns, tq), 0).astype(F32)
    blk = lax.broadcasted_iota(jnp.int32, (ns, tq), 0)
    pos_t = q0 + lax.broadcasted_iota(jnp.int32, (ns, tq), 1)
    cur = pos_t // SEL_BLOCK
    forced = (blk == 0) | (blk == cur) | (blk == cur - 1)
    causal = blk * SEL_BLOCK <= pos_t
    sel_t, picks = [], []
    taken = -jnp.inf
    for g in range(NSA_KV_GROUPS):
        f0 = NSA_HD * (1 - g)
        feat = jnp.where(lane_c == f0, feat_a, jnp.where(lane_c == f0 + 1, feat_b, 0.0))
        kc_g = jnp.where((lane_c // NSA_HD) == g, kc, feat.astype(BF16))
        imp = jnp.zeros((tq, nbp), F32)
        acc = []
        for hh in range(HEADS_PER_GROUP):
            h = g * HEADS_PER_GROUP + hh
            s = _nt(q_ref[0, :, h * LANES:(h + 1) * LANES], kc_g)
            s = jnp.where(valid, s, NEG)
            e = jnp.exp(s - jnp.max(s, axis=-1, keepdims=True))
            inv = jnp.where(row_ok, 1.0 / jnp.sum(e, axis=-1, keepdims=True), 0.0)
            p = e * inv
            imp = imp + p
            acc.append(_nn(p.astype(BF16), vc2))
        for jj in range(HEADS_PER_GROUP // 2):
            lower, upper = _pair_blocks(g, jj)
            blk_out = jnp.where(lane_lo, acc[2 * jj][:, lower], acc[2 * jj + 1][:, upper])
            c0 = (g * HEADS_PER_GROUP + 2 * jj) * NSA_HD
            ocmp_ref[0, :, c0:c0 + LANES] = blk_out
        p_slc = sum(_nt(mt_ref[...], part) for part in _split3(imp))
        score = jnp.where(forced, taken, jnp.where(causal, p_slc, NEG))
        for _ in range(n_sel - 3):
            top = jnp.max(score, axis=0, keepdims=True)
            first = jnp.min(jnp.where(score == top, blk_f, float(ns)), axis=0, keepdims=True)
            score = jnp.where(blk_f == first, taken, score)
        picked = jnp.where(score == taken, 1.0, 0.0).astype(BF16)
        picks.append([_nt(jnp.ones((SUBLANES, ATT_TQ), BF16), picked[:, t * ATT_TQ:(t + 1) * ATT_TQ])
                      for t in range(tq // ATT_TQ)])
        pen = jnp.where(score == taken, 0.0, SEL_PENALTY)
        pen = jnp.where((blk >= 2) & (blk <= ns - 1), pltpu.roll(pen, 1, axis=0), 0.0)
        if ns < NSA_HD:
            pen = jnp.concatenate([pen, jnp.zeros((NSA_HD - ns, tq), F32)], axis=0)
        zero = jnp.zeros((NSA_HD, tq), F32)
        sel_t += [zero, pen] if g == 0 else [pen, zero]
    sel_ref[0] = jnp.concatenate(sel_t, axis=0).T.astype(BF16)
    for t in range(tq // ATT_TQ):
        picks_ref[0, t] = jnp.concatenate([group[t] for group in picks], axis=1)


def _importance_matrix(nbp, ns):
    per = SEL_BLOCK // CMP_STRIDE
    m = np.zeros((ns, nbp), np.float32)
    for n in range(ns):
        for c in range(per * n, per * (n + 1)):
            for cc in (c - 1, c):
                if 0 <= cc < nbp - 1:
                    m[n, cc] += 1.0
    return jnp.asarray(m, BF16)


def _cmp_attention(q, kc, vc2, b, s):
    nbp = s // CMP_STRIDE
    ns = s // SEL_BLOCK
    tq = min(CMP_TQ, s)
    assert ns <= NSA_HD
    assert min(SEL_TOPK, ns) >= 3 and tq % ATT_TQ == 0
    q3 = q.reshape(b, s, NSA_HEADS * LANES)
    kern = functools.partial(_cmp_kernel, tq=tq, ns=ns, n_sel=min(SEL_TOPK, ns))
    return pl.pallas_call(
        kern,
        grid=(b, s // tq),
        in_specs=[pl.BlockSpec((1, tq, NSA_HEADS * LANES), lambda i, j: (i, j, 0)),
                  pl.BlockSpec((1, nbp, KV_W), lambda i, j: (i, 0, 0)),
                  pl.BlockSpec((1, nbp, 2 * KV_W), lambda i, j: (i, 0, 0)),
                  _resident((ns, nbp))],
        out_specs=[pl.BlockSpec((1, tq, NSA_WIDTH), lambda i, j: (i, j, 0)),
                   pl.BlockSpec((1, tq, NSA_KV_GROUPS * LANES), lambda i, j: (i, j, 0)),
                   pl.BlockSpec((1, tq // ATT_TQ, SUBLANES, NSA_KV_GROUPS * ns), lambda i, j: (i, j, 0, 0))],
        out_shape=[jax.ShapeDtypeStruct((b, s, NSA_WIDTH), F32),
                   jax.ShapeDtypeStruct((b, s, NSA_KV_GROUPS * LANES), BF16),
                   jax.ShapeDtypeStruct((b, s // ATT_TQ, SUBLANES, NSA_KV_GROUPS * ns), F32)],
        compiler_params=_params(("parallel", "parallel")),
        name="cmp_topk",
    )(q3, kc, vc2, _importance_matrix(nbp, ns))


def _attn_kernel(tiles_ref, ntiles_ref, q_ref, ks_ref, vs_ref, kw_ref, vw_ref, penq_ref, gl_ref, gexp_ref,
                 ocmp_ref, o_ref, qa_scr, s_scr, p_scr, al_scr, m_scr, acc_scr, *, tq, tk):
    i = pl.program_id(1)
    rows_of = lambda hh: slice(hh * tq, (hh + 1) * tq)
    lane_lo = lax.broadcasted_iota(jnp.int32, (tq, LANES), 1) < NSA_HD
    rel = (lax.broadcasted_iota(jnp.int32, (tq, tk), 1)
           - lax.broadcasted_iota(jnp.int32, (tq, tk), 0)).astype(F32)

    for hh in range(HEADS_PER_GROUP):
        qa_scr[rows_of(hh)] = q_ref[0, :, hh * LANES:(hh + 1) * LANES] + penq_ref[0]

    def scores(k_ref, j, slot):
        s_scr[slot] = _nt(qa_scr[...], k_ref[0, pl.ds(pl.multiple_of(j * tk, tk), tk), :])

    def probs(br, j, mode, slot, first, exists=None):
        for hh in range(HEADS_PER_GROUP):
            r = rows_of(hh)
            s = s_scr[slot, r]
            if mode == "causal":
                s = jnp.where(rel <= 0.0, s, NEG)
            elif mode == "lower":
                s = jnp.where(rel + (j * tk - i * tq).astype(F32) > -float(WINDOW), s, NEG)
            if exists is not None:
                s = jnp.where(exists, s, NEG)
            m_cur = jnp.max(s, axis=-1, keepdims=True)
            if first:
                m_new = jnp.broadcast_to(m_cur, (tq, LANES))
            else:
                m_prev = m_scr[br, r]
                m_new = jnp.maximum(m_prev, m_cur)
                al_scr[slot, r] = jnp.exp(m_prev - m_new)
            m_scr[br, r] = m_new
            p_scr[slot, r] = jnp.exp(s - jnp.concatenate([m_new] * (tk // LANES), axis=1)).astype(BF16)

    def accumulate(br, v_ref, j, slot, first):
        pv = _nn(p_scr[slot], v_ref[0, pl.ds(pl.multiple_of(j * tk, tk), tk), :])
        for hh in range(HEADS_PER_GROUP):
            r = rows_of(hh)
            if first:
                acc_scr[br, r] = pv[r]
            else:
                acc_scr[br, r] = al_scr[slot, r] * acc_scr[br, r] + pv[r]

    step = (pl.program_id(0) * pl.num_programs(1) + i) * pl.num_programs(2) + pl.program_id(2)
    n_tiles = ntiles_ref[step]
    last = jnp.maximum(n_tiles - 1, 0)
    tile_of = lambda n: tiles_ref[step * pl.num_programs(1) + jnp.minimum(n - 1, last)]
    scores(ks_ref, i, 0)
    scores(ks_ref, tile_of(1), 1)
    probs(0, i, "causal", 0, True)
    scores(ks_ref, tile_of(2), 0)
    probs(0, tile_of(1), None, 1, False)
    accumulate(0, vs_ref, i, 0, True)

    def sel_step(n, parity):
        scores(ks_ref, tile_of(n + 2), parity)
        probs(0, tile_of(n + 1), None, 1 - parity, False)
        accumulate(0, vs_ref, tile_of(n), parity, False)

    def sel_two_steps(t, carry):
        sel_step(2 * t + 1, 1)
        sel_step(2 * t + 2, 0)
        return carry

    lax.fori_loop(0, n_tiles // 2, sel_two_steps, 0)

    @pl.when(n_tiles % 2 == 1)
    def _():
        sel_step(n_tiles, 1)

    n_back = WINDOW // tk
    back = [(jnp.maximum(i - d, 0), "lower" if d == n_back else None, i >= d) for d in range(1, n_back + 1)]
    scores(kw_ref, i, 0)
    scores(kw_ref, back[0][0], 1)
    probs(1, i, "causal", 0, True)
    accumulate(1, vw_ref, i, 0, True)
    for d, (j, mode, exists) in enumerate(back, start=1):
        if d < n_back:
            scores(kw_ref, back[d][0], (d + 1) % 2)
        probs(1, j, mode, d % 2, False, exists)
        accumulate(1, vw_ref, j, d % 2, False)

    gexp = sum(_nn(part, gexp_ref[0]) for part in _split2(_sigmoid(gl_ref[0])))
    n_pairs = HEADS_PER_GROUP // 2
    gate_blk = lambda br, jj: gexp[:, (br * n_pairs + jj) * LANES:(br * n_pairs + jj + 1) * LANES]
    for jj in range(n_pairs):
        cols = slice(jj * LANES, (jj + 1) * LANES)
        blk = gate_blk(0, jj) * ocmp_ref[0, :, cols]
        for br in range(2):
            even = acc_scr[br, rows_of(2 * jj)]
            odd = acc_scr[br, rows_of(2 * jj + 1)]
            low = even / pltpu.roll(even, NSA_HD, axis=1)
            up = pltpu.roll(odd, NSA_HD, axis=1) / odd
            blk = blk + gate_blk(br + 1, jj) * jnp.where(lane_lo, low, up)
        o_ref[0, :, cols] = blk.astype(o_ref.dtype)


def _gate_expansion():
    n_pairs = HEADS_PER_GROUP // 2
    r = np.zeros((NSA_KV_GROUPS, LANES, 3 * n_pairs * LANES), np.float32)
    for g in range(NSA_KV_GROUPS):
        for br in range(3):
            for jj in range(n_pairs):
                for odd in range(2):
                    src = br * NSA_HEADS + g * HEADS_PER_GROUP + 2 * jj + odd
                    dst = (br * n_pairs + jj) * LANES + odd * NSA_HD
                    r[g, src, dst:dst + NSA_HD] = 1.0
    return jnp.asarray(r, BF16)


def _picked_tiles(picks, b, s):
    ns, nq = s // SEL_BLOCK, s // ATT_TQ
    per_tile = ATT_TK // SEL_BLOCK
    per_block = picks[:, :, 0, :].reshape(b, nq, NSA_KV_GROUPS, ns // per_tile, per_tile)
    j = jnp.arange(ns // per_tile, dtype=jnp.int32)
    i = jnp.arange(nq, dtype=jnp.int32)[None, :, None, None]
    active = ((per_block.sum(-1) > 0) | (j == 0)) & (j < i)
    slot = jnp.cumsum(active, axis=-1) - 1
    hit = active[..., None, :] & (slot[..., None, :] == j[:, None])
    tiles = jnp.sum(jnp.where(hit, j, 0), axis=-1).astype(jnp.int32)
    return tiles.reshape(-1), active.sum(-1).astype(jnp.int32).reshape(-1)


def _attention(q, ks, vs, kw, vw, pen, picks, gl, ocmp, b, s):
    tq, tk = ATT_TQ, ATT_TK
    assert tq == tk and WINDOW % tk == 0 and WINDOW // tk <= 2 and tq % SEL_BLOCK == 0
    gw = HEADS_PER_GROUP * LANES
    ow = HEADS_PER_GROUP * NSA_HD
    r3 = lambda a: a.reshape(b, s, a.shape[-1])
    tile = lambda w: pl.BlockSpec((1, tq, w), lambda i, j, g, *_: (i, j, 0))
    gtile = lambda w: pl.BlockSpec((1, tq, w), lambda i, j, g, *_: (i, j, g))
    gfull = pl.BlockSpec((1, s, LANES), lambda i, j, g, *_: (i, 0, g))
    gexp = _gate_expansion()
    rows = HEADS_PER_GROUP * tq
    kern = functools.partial(_attn_kernel, tq=tq, tk=tk)
    tiles, n_tiles = _picked_tiles(picks, b, s)
    grid_spec = pltpu.PrefetchScalarGridSpec(
        num_scalar_prefetch=2,
        grid=(b, s // tq, NSA_KV_GROUPS),
        in_specs=[gtile(gw), gfull, gfull, gfull, gfull, gtile(LANES), tile(LANES),
                  pl.BlockSpec((1,) + gexp.shape[1:], lambda i, j, g, *_: (g, 0, 0)), gtile(ow)],
        out_specs=gtile(ow),
        scratch_shapes=[pltpu.VMEM((rows, LANES), BF16), pltpu.VMEM((2, rows, tk), F32),
                        pltpu.VMEM((2, rows, tk), BF16), pltpu.VMEM((2, rows, LANES), F32),
                        pltpu.VMEM((2, rows, LANES), F32), pltpu.VMEM((2, rows, LANES), F32)])
    return pl.pallas_call(
        kern,
        grid_spec=grid_spec,
        out_shape=jax.ShapeDtypeStruct((b, s, NSA_WIDTH), BF16),
        compiler_params=_params(("parallel", "parallel", "parallel")),
        name="sel_win_attention",
    )(tiles, n_tiles, r3(q), r3(ks), r3(vs), r3(kw), r3(vw), pen, r3(gl), gexp, ocmp)


def _hgrn_rows(y, raw, onorm, st_scr, layer):
    c = HG_CHUNK
    ex = jnp.exp(raw - jnp.max(raw, axis=0, keepdims=True))
    sm = ex / jnp.sum(ex, axis=0, keepdims=True)
    lb_all = jnp.zeros((1, raw.shape[1]), F32)
    for l in range(1, layer + 1):
        lb_all = lb_all + sm[l:l + 1, :]

    t_idx = lax.broadcasted_iota(jnp.int32, (c, HG_DK), 0)
    sub = lax.broadcasted_iota(jnp.int32, (SUBLANES, HG_DK), 0)
    ti = lax.broadcasted_iota(jnp.int32, (c, c), 0)
    si = lax.broadcasted_iota(jnp.int32, (c, c), 1)
    tril = jnp.where(si <= ti, 1.0, 0.0).astype(BF16)
    levels = (32, 16, 8, 4, 2, 1)
    hc = HG_HEADS * c
    tb = lax.broadcasted_iota(jnp.int32, (hc, hc), 0)
    sb = lax.broadcasted_iota(jnp.int32, (hc, hc), 1)
    same_head = (tb // c) == (sb // c)
    split_bit = tb ^ sb
    pair_mask = {m: jnp.where(same_head & (tb > sb) & (split_bit >= m) & (split_bit < 2 * m), 1.0, 0.0)
                 for m in levels}
    diagonal = tb == sb
    second_half = {m: (t_idx & m) != 0 for m in levels}
    sign = {m: jnp.where(second_half[m], 1.0, -1.0) for m in levels}

    def ref_rows(b, m):
        row = lambda r, n: jnp.broadcast_to(b[r:r + 1, :], (n, HG_DK))
        if m >= 4:
            return jnp.concatenate([row(s0 + m - 1, 2 * m) for s0 in range(0, c, 2 * m)], axis=0)
        return jnp.concatenate([jnp.where(sub < 4, row(s0 + 1, SUBLANES), row(s0 + 5, SUBLANES))
                                for s0 in range(0, c, SUBLANES)], axis=0)

    part = lambda which, rows, h: y[rows, which * HG_WIDTH + h * HG_DK:which * HG_WIDTH + (h + 1) * HG_DK]
    out = []
    for ci in range(y.shape[0] // c):
        rows = slice(ci * c, (ci + 1) * c)
        q, k, f, v, logf = [], [], [], [], []
        for h in range(HG_HEADS):
            lb = lb_all[:, h * HG_DK:(h + 1) * HG_DK]
            z = part(1, rows, h)
            hq = part(0, rows, h)
            q.append(hq * _sigmoid(hq))
            ez = jnp.exp(-jnp.abs(z))
            big = 1.0 / (1.0 + ez)
            small = ez * big
            f.append(jnp.maximum(lb + (1.0 - lb) * jnp.where(z >= 0.0, big, small), F_FLOOR))
            logf.append(jnp.log2(f[h]))
            k.append((1.0 - lb) * jnp.where(z >= 0.0, small, big))
            v.append(part(2, rows, h).astype(BF16))
        parts = jnp.concatenate([p for h in range(HG_HEADS) for p in _split3(logf[h])], axis=1)
        csum = _nn(tril, parts)
        bcum = [sum(csum[:, (3 * h + i) * HG_DK:(3 * h + i + 1) * HG_DK] for i in range(3))
                for h in range(HG_HEADS)]

        rowdot = jnp.concatenate([jnp.sum(q[h] * k[h], axis=-1, keepdims=True) for h in range(HG_HEADS)], axis=0)
        a = jnp.where(diagonal, rowdot, 0.0)
        for m in levels:
            r = []
            for h in range(HG_HEADS):
                if m == 1:
                    r.append(jnp.where(second_half[m], q[h] * f[h], k[h]))
                else:
                    w = jnp.exp2((bcum[h] - ref_rows(bcum[h], m)) * sign[m])
                    r.append(jnp.where(second_half[m], q[h], k[h]) * w)
            r = jnp.concatenate(r, axis=0).astype(BF16)
            a = a + _nt(r, r) * pair_mask[m]
        o_intra = _nn(a.astype(BF16), jnp.concatenate(v, axis=0))

        heads = []
        for h in range(HG_HEADS):
            st = st_scr[h]
            o = o_intra[h * c:(h + 1) * c] + _nt((q[h] * jnp.exp2(bcum[h])).astype(BF16), st.astype(BF16))
            b_last = bcum[h][c - 1:c, :]
            kd = (k[h] * jnp.exp2(b_last - bcum[h])).astype(BF16)
            st_scr[h] = jnp.exp2(b_last) * st + _tn(v[h], kd)

            o = o * lax.rsqrt(jnp.mean(o * o, axis=-1, keepdims=True) + EPS) * onorm
            gate = part(3, rows, h)
            heads.append((o * (gate * _sigmoid(gate))).astype(BF16))
        out.append(jnp.concatenate(heads, axis=1))
    return jnp.concatenate(out, axis=0)


def _mixer(h, l, b, s, mix_norm, w_ext, cmp_k, cmp_v, hgrn_lower_bound, hgrn_out_norm):
    q, ks, vs, kw, vw, kc_in, vc_in, gl, o_hg = _inproj(h, mix_norm, w_ext, hgrn_lower_bound, hgrn_out_norm, s, l)
    kc = _compress(kc_in.reshape(b, s, KV_W), l, *cmp_k)
    vc2 = _compress(vc_in.reshape(b, s, KV_W), l, *cmp_v)
    ocmp, sel, picks = _cmp_attention(q, kc, vc2, b, s)
    o_nsa = _attention(q, ks, vs, kw, vw, sel, picks, gl, ocmp, b, s)
    return o_nsa.reshape(b * s, NSA_WIDTH), o_hg


def kernel(x, ffn1_norm, ffn1_w_gu, ffn1_w_down, mix_norm, w_in, cmp_pos_k, cmp_pos_v, cmp_k_w1, cmp_k_w2, cmp_v_w1, cmp_v_w2, hgrn_lower_bound, hgrn_out_norm, w_out, ffn2_norm, ffn2_w_gu, ffn2_w_down, final_norm):
    b, s, d = x.shape
    depth = ffn1_norm.shape[0]
    ffn1_w = _ffn_weights(ffn1_w_gu, ffn1_w_down)
    ffn2_w = _ffn_weights(ffn2_w_gu, ffn2_w_down)
    w_out = w_out.astype(BF16)
    w_ext = _build_w_in(w_in)
    cmp_k = _compress_weights(cmp_pos_k, cmp_k_w1, cmp_k_w2, (0, 1))
    cmp_v = _compress_weights(cmp_pos_v, cmp_v_w1, cmp_v_w2, (0, 1, 1, 0))
    h = x.reshape(b * s, d)
    for l in range(depth):
        h = _ffn(h, l, ffn1_norm, *ffn1_w, final_norm, False)
        o_nsa, o_hg = _mixer(h, l, b, s, mix_norm, w_ext, cmp_k, cmp_v, hgrn_lower_bound, hgrn_out_norm)
        h = _ffn(h, l, ffn2_norm, *ffn2_w, final_norm, l == depth - 1, (o_nsa, o_hg, w_out))
    return h.reshape(b, s, d)
```

```python
import functools

import jax
import jax.numpy as jnp
import numpy as np
from jax import lax
from jax.experimental import pallas as pl
from jax.experimental.pallas import tpu as pltpu

F32 = jnp.float32
BF16 = jnp.bfloat16

D_MODEL = 1024
EPS = 1e-6
NEG = -1e30
F_FLOOR = 1e-30
NSA_HEADS = 8
NSA_KV_GROUPS = 2
HEADS_PER_GROUP = NSA_HEADS // NSA_KV_GROUPS
NSA_HD = 64
CMP_LEN = 32
CMP_STRIDE = 16
CMP_HID = 256
SEL_BLOCK = 64
SEL_TOPK = 16
WINDOW = 512
HG_HEADS = 4
HG_DK = 128
HG_DV = 128
HG_CHUNK = 64
D_FF = 2752
NSA_WIDTH = NSA_HEADS * NSA_HD
HG_WIDTH = HG_HEADS * HG_DV
KV_W = NSA_KV_GROUPS * NSA_HD

LANES = 128
SUBLANES = 8
FF_CHUNK = 256
D_FF_PAD = -(-D_FF // FF_CHUNK) * FF_CHUNK
ROW_TILE = 512
FFN_ROWS = 1024
ATT_TQ = 256
ATT_TK = 256
CMP_TQ = 1024
VMEM_LIMIT = 56 * 1024 * 1024

SEL_PENALTY = 2.0 ** 50
ALIBI_SLOPES = tuple(2.0 ** (-8.0 * (i + 1) / NSA_HEADS) for i in range(NSA_HEADS))

SEGMENTS = (
    ("q", NSA_HEADS * LANES, BF16, NSA_WIDTH),
    ("ks", NSA_KV_GROUPS * LANES, BF16, KV_W),
    ("vs", NSA_KV_GROUPS * LANES, BF16, KV_W),
    ("kw", NSA_KV_GROUPS * LANES, BF16, KV_W),
    ("vw", NSA_KV_GROUPS * LANES, BF16, KV_W),
    ("kc", KV_W, F32, KV_W),
    ("vc", KV_W, F32, KV_W),
    ("gl", LANES, F32, LANES),
    ("hg", HG_WIDTH, BF16, 4 * HG_WIDTH),
)
SEG_OFFSETS = tuple(int(v) for v in np.cumsum([0] + [s[3] for s in SEGMENTS]))


def _nn(a, b):
    return jnp.dot(a, b, preferred_element_type=F32)


def _nt(a, b):
    return lax.dot_general(a, b, (((1,), (1,)), ((), ())), preferred_element_type=F32)


def _tn(a, b):
    return lax.dot_general(a, b, (((0,), (0,)), ((), ())), preferred_element_type=F32)


def _split2(x):
    hi = x.astype(BF16)
    lo = (x - hi.astype(F32)).astype(BF16)
    return hi, lo


def _split3(x):
    hi = x.astype(BF16)
    r = x - hi.astype(F32)
    mid = r.astype(BF16)
    lo = (r - mid.astype(F32)).astype(BF16)
    return hi, mid, lo


def _sigmoid(x):
    return 1.0 / (1.0 + jnp.exp(-x))


def _rms(x, g):
    return x * lax.rsqrt(jnp.mean(x * x, axis=-1, keepdims=True) + EPS) * g


def _resident(shape):
    nd = len(shape)
    return pl.BlockSpec(shape, lambda *_: (0,) * nd, pipeline_mode=pl.Buffered(1))


def _resident_layer(stacked_shape, layer):
    nd = len(stacked_shape)
    return pl.BlockSpec((None,) + tuple(stacked_shape[1:]), lambda *_: (layer,) + (0,) * (nd - 1),
                        pipeline_mode=pl.Buffered(1))


def _params(sem):
    return pltpu.CompilerParams(dimension_semantics=sem, vmem_limit_bytes=VMEM_LIMIT)


def _ffn_kernel(*refs, final, proj):
    if proj:
        x_ref, a_ref, b_ref, wo_ref, g_ref, wg_ref, wu_ref, wd_ref, gf_ref, o_ref = refs
        x = x_ref[...] + _nn(a_ref[...], wo_ref[:NSA_WIDTH, :]) + _nn(b_ref[...], wo_ref[NSA_WIDTH:, :])
    else:
        x_ref, g_ref, wg_ref, wu_ref, wd_ref, gf_ref, o_ref = refs
        x = x_ref[...]
    xn = _rms(x, g_ref[...]).astype(BF16)
    acc = jnp.zeros(x.shape, F32)
    for k in range(D_FF_PAD // FF_CHUNK):
        sl = slice(k * FF_CHUNK, (k + 1) * FF_CHUNK)
        gate = _nn(xn, wg_ref[:, sl])
        up = _nn(xn, wu_ref[:, sl])
        h = (gate * _sigmoid(gate) * up).astype(BF16)
        acc = acc + _nn(h, wd_ref[sl, :])
    y = x + 0.5 * acc
    if final:
        y = _rms(y, gf_ref[...])
    o_ref[...] = y


def _ffn_weights(w_gu, w_down):
    pad = D_FF_PAD - D_FF
    wg = jnp.pad(w_gu[..., :D_FF], ((0, 0), (0, 0), (0, pad))).astype(BF16)
    wu = jnp.pad(w_gu[..., D_FF:], ((0, 0), (0, 0), (0, pad))).astype(BF16)
    wd = jnp.pad(w_down, ((0, 0), (0, pad), (0, 0))).astype(BF16)
    return wg, wu, wd


def _ffn(h, layer, norm_g, wg, wu, wd, final_g, final, proj=None):
    t = h.shape[0]
    row = lambda w: pl.BlockSpec((FFN_ROWS, w), lambda i: (i, 0))
    norm_g = norm_g[:, None, :]
    weights = [_resident_layer(a.shape, layer) for a in (norm_g, wg, wu, wd)] + [_resident((1, D_MODEL))]
    operands = [norm_g, wg, wu, wd, final_g.reshape(1, -1)]
    if proj is None:
        in_specs, args = [row(D_MODEL)] + weights, [h] + operands
    else:
        o_nsa, o_hg, w_out = proj
        in_specs = [row(D_MODEL), row(NSA_WIDTH), row(HG_WIDTH), _resident_layer(w_out.shape, layer)] + weights
        args = [h, o_nsa, o_hg, w_out] + operands
    return pl.pallas_call(
        functools.partial(_ffn_kernel, final=final, proj=proj is not None),
        grid=(t // FFN_ROWS,),
        in_specs=in_specs,
        out_specs=row(D_MODEL),
        out_shape=jax.ShapeDtypeStruct((t, D_MODEL), F32),
        compiler_params=_params(("parallel",)),
        name="ffn",
    )(*args)


def _inproj_kernel(x_ref, g_ref, w_ref, qfeat_ref, lbraw_ref, onorm_ref, *refs, seq, layer):
    *o_refs, st_scr = refs
    xn = _rms(x_ref[...], g_ref[...]).astype(BF16)
    rows = x_ref.shape[0]

    @pl.when(pl.program_id(0) % (seq // rows) == 0)
    def _():
        st_scr[...] = jnp.zeros(st_scr.shape, F32)

    kpos = (pl.program_id(0) % (seq // rows)) * rows + lax.broadcasted_iota(jnp.int32, (rows, LANES), 0)
    lane = lax.broadcasted_iota(jnp.int32, (rows, LANES), 1)
    blk = kpos // SEL_BLOCK
    digit = lambda f0: jnp.where(lane == f0, blk.astype(F32),
                                 jnp.where(lane == f0 + 1, (kpos % SEL_BLOCK).astype(F32), 0.0))
    halves = [NSA_HD * (1 - g) for g in range(NSA_KV_GROUPS)]
    kfeat = jnp.concatenate([digit(f0) for f0 in halves], axis=1)
    is_pen = lambda f0: (lane == f0 + 1 + blk) & (blk >= 1) & (blk <= seq // SEL_BLOCK - 2)
    kpen = jnp.concatenate([jnp.where(is_pen(f0), -SEL_PENALTY, 0.0) for f0 in halves], axis=1)
    lower = lane < NSA_HD
    swap = lambda a: pltpu.roll(a, NSA_HD, axis=1)
    small = [i for i, seg in enumerate(SEGMENTS) if seg[3] == LANES]
    assert small == list(range(small[0], small[-1] + 1))
    y_small = _nn(xn, w_ref[:, SEG_OFFSETS[small[0]]:SEG_OFFSETS[small[-1] + 1]])
    for i, (o_ref, (name, _, dtype, cols)) in enumerate(zip(o_refs, SEGMENTS, strict=True)):
        if i in small:
            y = y_small[:, (i - small[0]) * LANES:(i - small[0] + 1) * LANES]
        else:
            y = _nn(xn, w_ref[:, SEG_OFFSETS[i]:SEG_OFFSETS[i] + cols])
        if name == "q":
            blocks = []
            for h in range(NSA_HEADS):
                pair = y[:, (h // 2) * LANES:(h // 2 + 1) * LANES]
                g = h // HEADS_PER_GROUP
                data = pair if h % 2 == g else swap(pair)
                feat = qfeat_ref[:, h * LANES:(h + 1) * LANES]
                blocks.append(jnp.where(lower == (g == 0), data, feat))
            y = jnp.concatenate(blocks, axis=1)
        elif name in ("ks", "kw"):
            feat = kfeat + kpen if name == "ks" else kfeat
            y = jnp.concatenate([jnp.where(lower == (g == 0), y, feat[:, g * LANES:(g + 1) * LANES])
                                 for g in range(NSA_KV_GROUPS)], axis=1)
        elif name in ("vs", "vw"):
            y = jnp.concatenate([jnp.where(lower, y if g == 0 else swap(y), 1.0)
                                 for g in range(NSA_KV_GROUPS)], axis=1)
        elif name == "hg":
            y = _hgrn_rows(y, lbraw_ref[...], onorm_ref[...], st_scr, layer)
        o_ref[...] = y.astype(dtype)


def _q_features():
    feat = np.zeros((1, NSA_HEADS * LANES), np.float32)
    for h in range(NSA_HEADS):
        f0 = h * LANES + NSA_HD * (1 - h // HEADS_PER_GROUP)
        feat[0, f0] = SEL_BLOCK * ALIBI_SLOPES[h]
        feat[0, f0 + 1] = ALIBI_SLOPES[h]
    return jnp.asarray(feat)


def _build_w_in(w_in):
    sizes = (NSA_WIDTH, KV_W, KV_W, KV_W, KV_W, KV_W, KV_W, NSA_HEADS * 3,
             HG_WIDTH, HG_WIDTH, HG_WIDTH, HG_WIDTH)
    splits = [int(v) for v in np.cumsum(sizes)[:-1]]
    w_in = w_in.astype(BF16)
    wq, wkc, wvc, wks, wvs, wkw, wvw, wgl, whq, whf, whi, whg = jnp.split(w_in, splits, axis=-1)
    lead = w_in.shape[:-1]
    gl = jnp.swapaxes(wgl.reshape(lead + (NSA_HEADS, 3)), -1, -2).reshape(lead + (3 * NSA_HEADS,))
    gl = jnp.pad(gl, [(0, 0)] * len(lead) + [(0, LANES - 3 * NSA_HEADS)])
    cols = [wq * NSA_HD ** -0.5, wks, wvs, wkw, wvw, wkc, wvc, gl, whq, whf, whi, whg]
    return jnp.concatenate(cols, axis=-1)


def _inproj(h, norm_g, w_ext, lb_raw, out_norm, seq, layer):
    t = h.shape[0]
    assert seq % ROW_TILE == 0 and ROW_TILE % HG_CHUNK == 0
    row = lambda w: pl.BlockSpec((ROW_TILE, w), lambda i: (i, 0))
    norm_g, out_norm = norm_g[:, None, :], out_norm[:, None, :]
    return pl.pallas_call(
        functools.partial(_inproj_kernel, seq=seq, layer=layer),
        grid=(t // ROW_TILE,),
        in_specs=[row(D_MODEL), _resident_layer(norm_g.shape, layer), _resident_layer(w_ext.shape, layer),
                  _resident((1, NSA_HEADS * LANES)), _resident(lb_raw.shape),
                  _resident_layer(out_norm.shape, layer)],
        out_specs=[row(w) for _, w, _, _ in SEGMENTS],
        out_shape=[jax.ShapeDtypeStruct((t, w), dt) for _, w, dt, _ in SEGMENTS],
        scratch_shapes=[pltpu.VMEM((HG_HEADS, HG_DV, HG_DK), F32)],
        compiler_params=_params(("arbitrary",)),
        name="inproj_hgrn",
    )(h, norm_g, w_ext, _q_features(), lb_raw, out_norm)


def _gelu_tanh(x):
    return 0.5 * x * (1.0 + jnp.tanh(0.7978845608028654 * (x + 0.044715 * (x * x * x))))


def _compress_kernel(kv_ref, pos_ref, w1_ref, w2_ref, o_ref):
    nbp = o_ref.shape[1]
    hid = NSA_KV_GROUPS * CMP_HID
    ha = jnp.zeros((nbp, hid), F32)
    hb = jnp.zeros((nbp, hid), F32)
    for l in range(CMP_STRIDE):
        x = kv_ref[0, pl.ds(l, nbp, stride=CMP_STRIDE), :]
        ha = ha + _nn((x + pos_ref[l:l + 1, :]).astype(BF16), w1_ref[l])
        hb = hb + _nn((x + pos_ref[CMP_STRIDE + l:CMP_STRIDE + l + 1, :]).astype(BF16), w1_ref[CMP_STRIDE + l])
    act = _gelu_tanh(ha + pltpu.roll(hb, nbp - 1, axis=0)).astype(BF16)
    out = jnp.zeros(o_ref.shape[1:], F32)
    for g in range(NSA_KV_GROUPS):
        out = out + _nn(act[:, g * CMP_HID:(g + 1) * CMP_HID], w2_ref[g])
    o_ref[0] = out


def _compress_weights(pos, w1, w2, reps):
    layers = w1.shape[0]
    w1l = w1.astype(BF16).reshape(layers, CMP_LEN, NSA_HD, CMP_HID)
    z1 = jnp.zeros_like(w1l)
    w1p = jnp.concatenate([jnp.concatenate([w1l, z1], axis=3), jnp.concatenate([z1, w1l], axis=3)], axis=2)
    pos2 = jnp.concatenate([pos] * NSA_KV_GROUPS, axis=2)
    w2 = w2.astype(BF16)
    zero = jnp.zeros_like(w2)
    w2p = jnp.stack([jnp.concatenate([w2 if r == g else zero for r in reps], axis=2)
                     for g in range(NSA_KV_GROUPS)], axis=1)
    return pos2, w1p, w2p


def _compress(kv, layer, pos2, w1p, w2p):
    b, s, _ = kv.shape
    nbp = s // CMP_STRIDE
    width = w2p.shape[-1]
    return pl.pallas_call(
        _compress_kernel,
        grid=(b,),
        in_specs=[pl.BlockSpec((1, s, KV_W), lambda i: (i, 0, 0))]
        + [_resident_layer(a.shape, layer) for a in (pos2, w1p, w2p)],
        out_specs=pl.BlockSpec((1, nbp, width), lambda i: (i, 0, 0)),
        out_shape=jax.ShapeDtypeStruct((b, nbp, width), F32),
        compiler_params=_params(("parallel",)),
        name="compress",
    )(kv, pos2, w1p, w2p)


def _pair_blocks(g, jj):
    lower = slice(0, LANES) if g == 0 else slice(LANES, 2 * LANES)
    upper = slice(LANES, 2 * LANES) if g == 0 else slice(0, LANES)
    return lower, upper


def _cmp_kernel(q_ref, kc_ref, vc2_ref, mt_ref, ocmp_ref, sel_ref, picks_ref, *, tq, ns, n_sel):
    nbp = kc_ref.shape[1]
    q0 = pl.program_id(1) * tq
    pos = q0 + lax.broadcasted_iota(jnp.int32, (tq, nbp), 0)
    blk_end = lax.broadcasted_iota(jnp.int32, (tq, nbp), 1) * CMP_STRIDE + (CMP_LEN - 1)
    valid = blk_end <= pos
    row_ok = (q0 + lax.broadcasted_iota(jnp.int32, (tq, 1), 0)) >= CMP_LEN - 1
    kc = kc_ref[0].astype(BF16)
    vc2 = vc2_ref[0].astype(BF16)
    lane_c = lax.broadcasted_iota(jnp.int32, (nbp, LANES), 1)
    c_idx = lax.broadcasted_iota(jnp.int32, (nbp, LANES), 0)
    per = SEL_BLOCK // CMP_STRIDE
    feat_a = (c_idx // per - q0 // SEL_BLOCK).astype(F32)
    feat_b = ((c_idx % per) * CMP_STRIDE + (CMP_LEN - 1)).astype(F32)
    lane_lo = lax.broadcasted_iota(jnp.int32, (tq, LANES), 1) < NSA_HD
    blk_f = lax.broadcasted_iota(jnp.int32, (ns, tq), 0).astype(F32)
    blk = lax.broadcasted_iota(jnp.int32, (ns, tq), 0)
    pos_t = q0 + lax.broadcasted_iota(jnp.int32, (ns, tq), 1)
    cur = pos_t // SEL_BLOCK
    forced = (blk == 0) | (blk == cur) | (blk == cur - 1)
    causal = blk * SEL_BLOCK <= pos_t
    sel_t, picks = [], []
    taken = -jnp.inf
    for g in range(NSA_KV_GROUPS):
        f0 = NSA_HD * (1 - g)
        feat = jnp.where(lane_c == f0, feat_a, jnp.where(lane_c == f0 + 1, feat_b, 0.0))
        kc_g = jnp.where((lane_c // NSA_HD) == g, kc, feat.astype(BF16))
        imp = jnp.zeros((tq, nbp), F32)
        acc = []
        for hh in range(HEADS_PER_GROUP):
            h = g * HEADS_PER_GROUP + hh
            s = _nt(q_ref[0, :, h * LANES:(h + 1) * LANES], kc_g)
            s = jnp.where(valid, s, NEG)
            e = jnp.exp(s - jnp.max(s, axis=-1, keepdims=True))
            inv = jnp.where(row_ok, 1.0 / jnp.sum(e, axis=-1, keepdims=True), 0.0)
            p = e * inv
            imp = imp + p
            acc.append(_nn(p.astype(BF16), vc2))
        for jj in range(HEADS_PER_GROUP // 2):
            lower, upper = _pair_blocks(g, jj)
            blk_out = jnp.where(lane_lo, acc[2 * jj][:, lower], acc[2 * jj + 1][:, upper])
            c0 = (g * HEADS_PER_GROUP + 2 * jj) * NSA_HD
            ocmp_ref[0, :, c0:c0 + LANES] = blk_out
        p_slc = sum(_nt(mt_ref[...], part) for part in _split3(imp))
        score = jnp.where(forced, taken, jnp.where(causal, p_slc, NEG))
        for _ in range(n_sel - 3):
            top = jnp.max(score, axis=0, keepdims=True)
            first = jnp.min(jnp.where(score == top, blk_f, float(ns)), axis=0, keepdims=True)
            score = jnp.where(blk_f == first, taken, score)
        picked = jnp.where(score == taken, 1.0, 0.0).astype(BF16)
        picks.append([_nt(jnp.ones((SUBLANES, ATT_TQ), BF16), picked[:, t * ATT_TQ:(t + 1) * ATT_TQ])
                      for t in range(tq // ATT_TQ)])
        pen = jnp.where(score == taken, 0.0, SEL_PENALTY)
        pen = jnp.where((blk >= 2) & (blk <= ns - 1), pltpu.roll(pen, 1, axis=0), 0.0)
        if ns < NSA_HD:
            pen = jnp.concatenate([pen, jnp.zeros((NSA_HD - ns, tq), F32)], axis=0)
        zero = jnp.zeros((NSA_HD, tq), F32)
        sel_t += [zero, pen] if g == 0 else [pen, zero]
    sel_ref[0] = jnp.concatenate(sel_t, axis=0).T.astype(BF16)
    for t in range(tq // ATT_TQ):
        picks_ref[0, t] = jnp.concatenate([group[t] for group in picks], axis=1)


def _importance_matrix(nbp, ns):
    per = SEL_BLOCK // CMP_STRIDE
    m = np.zeros((ns, nbp), np.float32)
    for n in range(ns):
        for c in range(per * n, per * (n + 1)):
            for cc in (c - 1, c):
                if 0 <= cc < nbp - 1:
                    m[n, cc] += 1.0
    return jnp.asarray(m, BF16)


def _cmp_attention(q, kc, vc2, b, s):
    nbp = s // CMP_STRIDE
    ns = s // SEL_BLOCK
    tq = min(CMP_TQ, s)
    assert ns <= NSA_HD
    assert min(SEL_TOPK, ns) >= 3 and tq % ATT_TQ == 0
    q3 = q.reshape(b, s, NSA_HEADS * LANES)
    kern = functools.partial(_cmp_kernel, tq=tq, ns=ns, n_sel=min(SEL_TOPK, ns))
    return pl.pallas_call(
        kern,
        grid=(b, s // tq),
        in_specs=[pl.BlockSpec((1, tq, NSA_HEADS * LANES), lambda i, j: (i, j, 0)),
                  pl.BlockSpec((1, nbp, KV_W), lambda i, j: (i, 0, 0)),
                  pl.BlockSpec((1, nbp, 2 * KV_W), lambda i, j: (i, 0, 0)),
                  _resident((ns, nbp))],
        out_specs=[pl.BlockSpec((1, tq, NSA_WIDTH), lambda i, j: (i, j, 0)),
                   pl.BlockSpec((1, tq, NSA_KV_GROUPS * LANES), lambda i, j: (i, j, 0)),
                   pl.BlockSpec((1, tq // ATT_TQ, SUBLANES, NSA_KV_GROUPS * ns), lambda i, j: (i, j, 0, 0))],
        out_shape=[jax.ShapeDtypeStruct((b, s, NSA_WIDTH), F32),
                   jax.ShapeDtypeStruct((b, s, NSA_KV_GROUPS * LANES), BF16),
                   jax.ShapeDtypeStruct((b, s // ATT_TQ, SUBLANES, NSA_KV_GROUPS * ns), F32)],
        compiler_params=_params(("parallel", "parallel")),
        name="cmp_topk",
    )(q3, kc, vc2, _importance_matrix(nbp, ns))


def _attn_kernel(tiles_ref, ntiles_ref, q_ref, ks_ref, vs_ref, kw_ref, vw_ref, penq_ref, gl_ref, gexp_ref,
                 ocmp_ref, o_ref, qa_scr, s_scr, p_scr, al_scr, m_scr, acc_scr, *, tq, tk):
    i = pl.program_id(1)
    rows_of = lambda hh: slice(hh * tq, (hh + 1) * tq)
    lane_lo = lax.broadcasted_iota(jnp.int32, (tq, LANES), 1) < NSA_HD
    rel = (lax.broadcasted_iota(jnp.int32, (tq, tk), 1)
           - lax.broadcasted_iota(jnp.int32, (tq, tk), 0)).astype(F32)

    for hh in range(HEADS_PER_GROUP):
        qa_scr[rows_of(hh)] = q_ref[0, :, hh * LANES:(hh + 1) * LANES] + penq_ref[0]

    def scores(k_ref, j, slot):
        s_scr[slot] = _nt(qa_scr[...], k_ref[0, pl.ds(pl.multiple_of(j * tk, tk), tk), :])

    def probs(br, j, mode, slot, first, exists=None):
        for hh in range(HEADS_PER_GROUP):
            r = rows_of(hh)
            s = s_scr[slot, r]
            if mode == "causal":
                s = jnp.where(rel <= 0.0, s, NEG)
            elif mode == "lower":
                s = jnp.where(rel + (j * tk - i * tq).astype(F32) > -float(WINDOW), s, NEG)
            if exists is not None:
                s = jnp.where(exists, s, NEG)
            m_cur = jnp.max(s, axis=-1, keepdims=True)
            if first:
                m_new = jnp.broadcast_to(m_cur, (tq, LANES))
            else:
                m_prev = m_scr[br, r]
                m_new = jnp.maximum(m_prev, m_cur)
                al_scr[slot, r] = jnp.exp(m_prev - m_new)
            m_scr[br, r] = m_new
            p_scr[slot, r] = jnp.exp(s - jnp.concatenate([m_new] * (tk // LANES), axis=1)).astype(BF16)

    def accumulate(br, v_ref, j, slot, first):
        pv = _nn(p_scr[slot], v_ref[0, pl.ds(pl.multiple_of(j * tk, tk), tk), :])
        for hh in range(HEADS_PER_GROUP):
            r = rows_of(hh)
            if first:
                acc_scr[br, r] = pv[r]
            else:
                acc_scr[br, r] = al_scr[slot, r] * acc_scr[br, r] + pv[r]

    step = (pl.program_id(0) * pl.num_programs(1) + i) * pl.num_programs(2) + pl.program_id(2)
    n_tiles = ntiles_ref[step]
    item = lambda br, j, mode, first, exists=None: (br, j, mode, first, exists)
    sel_item = lambda n: item(0, i, "causal", True) if isinstance(n, int) and n == 0 else item(
        0, tiles_ref[step * pl.num_programs(1) + n - 1], None, False)
    n_back = WINDOW // tk
    win_items = [item(1, i, "causal", True)] + [
        item(1, jnp.maximum(i - d, 0), "lower" if d == n_back else None, False, i >= d)
        for d in range(1, n_back + 1)]
    k_refs, v_refs = (ks_ref, kw_ref), (vs_ref, vw_ref)

    def run(items, slot0, done=(0, 0, 0), upto=None):
        n_items = len(items)
        slot = lambda n: (slot0 + n) % 2
        n_s, n_p, n_a = done

        def do_scores(n):
            br, j, _, _, _ = items[n]
            scores(k_refs[br], j, slot(n))

        def do_probs(n):
            br, j, mode, first, exists = items[n]
            probs(br, j, mode, slot(n), first, exists)

        def do_acc(n):
            br, j, _, first, _ = items[n]
            accumulate(br, v_refs[br], j, slot(n), first)

        for n in range(n_s, min(n_a + 2, n_items)):
            do_scores(n)
        n_s = max(n_s, min(n_a + 2, n_items))
        for n in range(n_p, min(n_a + 1, n_items)):
            do_probs(n)
        n_p = max(n_p, min(n_a + 1, n_items))
        for k in range(n_a, n_items if upto is None else upto):
            if n_s <= k + 2 < n_items:
                do_scores(k + 2)
                n_s = k + 3
            if n_p <= k + 1 < n_items:
                do_probs(k + 1)
                n_p = k + 2
            do_acc(k)

    for n_static in range(2):
        @pl.when(n_tiles == n_static)
        def _(n_static=n_static):
            run([sel_item(n) for n in range(n_static + 1)] + win_items, 0)

    @pl.when(n_tiles >= 2)
    def _():
        run([sel_item(n) for n in range(3)], 0, upto=1)

        def sel_step(n, parity):
            scores(ks_ref, sel_item(n + 2)[1], parity)
            probs(0, None, None, 1 - parity, False)
            accumulate(0, vs_ref, sel_item(n)[1], parity, False)

        def sel_two_steps(t, carry):
            sel_step(2 * t + 1, 1)
            sel_step(2 * t + 2, 0)
            return carry

        lax.fori_loop(0, (n_tiles - 2) // 2, sel_two_steps, 0)

        def drain(parity):
            tail = [sel_item(n_tiles - 1), sel_item(n_tiles)] + win_items
            run(tail, 1 - parity, done=(2, 1, 0))

        @pl.when(n_tiles % 2 == 0)
        def _():
            drain(0)

        @pl.when(n_tiles % 2 == 1)
        def _():
            sel_step(n_tiles - 2, 1)
            drain(1)

    gexp = sum(_nn(part, gexp_ref[0]) for part in _split2(_sigmoid(gl_ref[0])))
    n_pairs = HEADS_PER_GROUP // 2
    gate_blk = lambda br, jj: gexp[:, (br * n_pairs + jj) * LANES:(br * n_pairs + jj + 1) * LANES]
    for jj in range(n_pairs):
        cols = slice(jj * LANES, (jj + 1) * LANES)
        blk = gate_blk(0, jj) * ocmp_ref[0, :, cols]
        for br in range(2):
            even = acc_scr[br, rows_of(2 * jj)]
            odd = acc_scr[br, rows_of(2 * jj + 1)]
            low = even / pltpu.roll(even, NSA_HD, axis=1)
            up = pltpu.roll(odd, NSA_HD, axis=1) / odd
            blk = blk + gate_blk(br + 1, jj) * jnp.where(lane_lo, low, up)
        o_ref[0, :, cols] = blk.astype(o_ref.dtype)


def _gate_expansion():
    n_pairs = HEADS_PER_GROUP // 2
    r = np.zeros((NSA_KV_GROUPS, LANES, 3 * n_pairs * LANES), np.float32)
    for g in range(NSA_KV_GROUPS):
        for br in range(3):
            for jj in range(n_pairs):
                for odd in range(2):
                    src = br * NSA_HEADS + g * HEADS_PER_GROUP + 2 * jj + odd
                    dst = (br * n_pairs + jj) * LANES + odd * NSA_HD
                    r[g, src, dst:dst + NSA_HD] = 1.0
    return jnp.asarray(r, BF16)


def _picked_tiles(picks, b, s):
    ns, nq = s // SEL_BLOCK, s // ATT_TQ
    per_tile = ATT_TK // SEL_BLOCK
    per_block = picks[:, :, 0, :].reshape(b, nq, NSA_KV_GROUPS, ns // per_tile, per_tile)
    j = jnp.arange(ns // per_tile, dtype=jnp.int32)
    i = jnp.arange(nq, dtype=jnp.int32)[None, :, None, None]
    active = ((per_block.sum(-1) > 0) | (j == 0)) & (j < i)
    slot = jnp.cumsum(active, axis=-1) - 1
    hit = active[..., None, :] & (slot[..., None, :] == j[:, None])
    tiles = jnp.sum(jnp.where(hit, j, 0), axis=-1).astype(jnp.int32)
    return tiles.reshape(-1), active.sum(-1).astype(jnp.int32).reshape(-1)


def _attention(q, ks, vs, kw, vw, pen, picks, gl, ocmp, b, s):
    tq, tk = ATT_TQ, ATT_TK
    assert tq == tk and WINDOW % tk == 0 and WINDOW // tk <= 2 and tq % SEL_BLOCK == 0
    gw = HEADS_PER_GROUP * LANES
    ow = HEADS_PER_GROUP * NSA_HD
    r3 = lambda a: a.reshape(b, s, a.shape[-1])
    tile = lambda w: pl.BlockSpec((1, tq, w), lambda i, j, g, *_: (i, j, 0))
    gtile = lambda w: pl.BlockSpec((1, tq, w), lambda i, j, g, *_: (i, j, g))
    gfull = pl.BlockSpec((1, s, LANES), lambda i, j, g, *_: (i, 0, g))
    gexp = _gate_expansion()
    rows = HEADS_PER_GROUP * tq
    kern = functools.partial(_attn_kernel, tq=tq, tk=tk)
    tiles, n_tiles = _picked_tiles(picks, b, s)
    grid_spec = pltpu.PrefetchScalarGridSpec(
        num_scalar_prefetch=2,
        grid=(b, s // tq, NSA_KV_GROUPS),
        in_specs=[gtile(gw), gfull, gfull, gfull, gfull, gtile(LANES), tile(LANES),
                  pl.BlockSpec((1,) + gexp.shape[1:], lambda i, j, g, *_: (g, 0, 0)), gtile(ow)],
        out_specs=gtile(ow),
        scratch_shapes=[pltpu.VMEM((rows, LANES), BF16), pltpu.VMEM((2, rows, tk), F32),
                        pltpu.VMEM((2, rows, tk), BF16), pltpu.VMEM((2, rows, LANES), F32),
                        pltpu.VMEM((2, rows, LANES), F32), pltpu.VMEM((2, rows, LANES), F32)])
    return pl.pallas_call(
        kern,
        grid_spec=grid_spec,
        out_shape=jax.ShapeDtypeStruct((b, s, NSA_WIDTH), BF16),
        compiler_params=_params(("parallel", "parallel", "parallel")),
        name="sel_win_attention",
    )(tiles, n_tiles, r3(q), r3(ks), r3(vs), r3(kw), r3(vw), pen, r3(gl), gexp, ocmp)


def _hgrn_rows(y, raw, onorm, st_scr, layer):
    c = HG_CHUNK
    ex = jnp.exp(raw - jnp.max(raw, axis=0, keepdims=True))
    sm = ex / jnp.sum(ex, axis=0, keepdims=True)
    lb_all = jnp.zeros((1, raw.shape[1]), F32)
    for l in range(1, layer + 1):
        lb_all = lb_all + sm[l:l + 1, :]

    t_idx = lax.broadcasted_iota(jnp.int32, (c, HG_DK), 0)
    sub = lax.broadcasted_iota(jnp.int32, (SUBLANES, HG_DK), 0)
    ti = lax.broadcasted_iota(jnp.int32, (c, c), 0)
    si = lax.broadcasted_iota(jnp.int32, (c, c), 1)
    tril = jnp.where(si <= ti, 1.0, 0.0).astype(BF16)
    levels = (32, 16, 8, 4, 2, 1)
    hc = HG_HEADS * c
    tb = lax.broadcasted_iota(jnp.int32, (hc, hc), 0)
    sb = lax.broadcasted_iota(jnp.int32, (hc, hc), 1)
    same_head = (tb // c) == (sb // c)
    split_bit = tb ^ sb
    pair_mask = {m: jnp.where(same_head & (tb > sb) & (split_bit >= m) & (split_bit < 2 * m), 1.0, 0.0)
                 for m in levels}
    diagonal = tb == sb
    second_half = {m: (t_idx & m) != 0 for m in levels}
    sign = {m: jnp.where(second_half[m], 1.0, -1.0) for m in levels}

    def ref_rows(b, m):
        row = lambda r, n: jnp.broadcast_to(b[r:r + 1, :], (n, HG_DK))
        if m >= 4:
            return jnp.concatenate([row(s0 + m - 1, 2 * m) for s0 in range(0, c, 2 * m)], axis=0)
        return jnp.concatenate([jnp.where(sub < 4, row(s0 + 1, SUBLANES), row(s0 + 5, SUBLANES))
                                for s0 in range(0, c, SUBLANES)], axis=0)

    part = lambda which, rows, h: y[rows, which * HG_WIDTH + h * HG_DK:which * HG_WIDTH + (h + 1) * HG_DK]
    out = []
    for ci in range(y.shape[0] // c):
        rows = slice(ci * c, (ci + 1) * c)
        q, k, f, v, logf = [], [], [], [], []
        for h in range(HG_HEADS):
            lb = lb_all[:, h * HG_DK:(h + 1) * HG_DK]
            z = part(1, rows, h)
            hq = part(0, rows, h)
            q.append(hq * _sigmoid(hq))
            ez = jnp.exp(-jnp.abs(z))
            big = 1.0 / (1.0 + ez)
            small = ez * big
            f.append(jnp.maximum(lb + (1.0 - lb) * jnp.where(z >= 0.0, big, small), F_FLOOR))
            logf.append(jnp.log2(f[h]))
            k.append((1.0 - lb) * jnp.where(z >= 0.0, small, big))
            v.append(part(2, rows, h).astype(BF16))
        parts = jnp.concatenate([p for h in range(HG_HEADS) for p in _split3(logf[h])], axis=1)
        csum = _nn(tril, parts)
        bcum = [sum(csum[:, (3 * h + i) * HG_DK:(3 * h + i + 1) * HG_DK] for i in range(3))
                for h in range(HG_HEADS)]

        rowdot = jnp.concatenate([jnp.sum(q[h] * k[h], axis=-1, keepdims=True) for h in range(HG_HEADS)], axis=0)
        a = jnp.where(diagonal, rowdot, 0.0)
        for m in levels:
            r = []
            for h in range(HG_HEADS):
                if m == 1:
                    r.append(jnp.where(second_half[m], q[h] * f[h], k[h]))
                else:
                    w = jnp.exp2((bcum[h] - ref_rows(bcum[h], m)) * sign[m])
                    r.append(jnp.where(second_half[m], q[h], k[h]) * w)
            r = jnp.concatenate(r, axis=0).astype(BF16)
            a = a + _nt(r, r) * pair_mask[m]
        o_intra = _nn(a.astype(BF16), jnp.concatenate(v, axis=0))

        heads = []
        for h in range(HG_HEADS):
            st = st_scr[h]
            o = o_intra[h * c:(h + 1) * c] + _nt((q[h] * jnp.exp2(bcum[h])).astype(BF16), st.astype(BF16))
            b_last = bcum[h][c - 1:c, :]
            kd = (k[h] * jnp.exp2(b_last - bcum[h])).astype(BF16)
            st_scr[h] = jnp.exp2(b_last) * st + _tn(v[h], kd)

            o = o * lax.rsqrt(jnp.mean(o * o, axis=-1, keepdims=True) + EPS) * onorm
            gate = part(3, rows, h)
            heads.append((o * (gate * _sigmoid(gate))).astype(BF16))
        out.append(jnp.concatenate(heads, axis=1))
    return jnp.concatenate(out, axis=0)


def _mixer(h, l, b, s, mix_norm, w_ext, cmp_k, cmp_v, hgrn_lower_bound, hgrn_out_norm):
    q, ks, vs, kw, vw, kc_in, vc_in, gl, o_hg = _inproj(h, mix_norm, w_ext, hgrn_lower_bound, hgrn_out_norm, s, l)
    kc = _compress(kc_in.reshape(b, s, KV_W), l, *cmp_k)
    vc2 = _compress(vc_in.reshape(b, s, KV_W), l, *cmp_v)
    ocmp, sel, picks = _cmp_attention(q, kc, vc2, b, s)
    o_nsa = _attention(q, ks, vs, kw, vw, sel, picks, gl, ocmp, b, s)
    return o_nsa.reshape(b * s, NSA_WIDTH), o_hg


def kernel(x, ffn1_norm, ffn1_w_gu, ffn1_w_down, mix_norm, w_in, cmp_pos_k, cmp_pos_v, cmp_k_w1, cmp_k_w2, cmp_v_w1, cmp_v_w2, hgrn_lower_bound, hgrn_out_norm, w_out, ffn2_norm, ffn2_w_gu, ffn2_w_down, final_norm):
    b, s, d = x.shape
    depth = ffn1_norm.shape[0]
    ffn1_w = _ffn_weights(ffn1_w_gu, ffn1_w_down)
    ffn2_w = _ffn_weights(ffn2_w_gu, ffn2_w_down)
    w_out = w_out.astype(BF16)
    w_ext = _build_w_in(w_in)
    cmp_k = _compress_weights(cmp_pos_k, cmp_k_w1, cmp_k_w2, (0, 1))
    cmp_v = _compress_weights(cmp_pos_v, cmp_v_w1, cmp_v_w2, (0, 1, 1, 0))
    h = x.reshape(b * s, d)
    for l in range(depth):
        h = _ffn(h, l, ffn1_norm, *ffn1_w, final_norm, False)
        o_nsa, o_hg = _mixer(h, l, b, s, mix_norm, w_ext, cmp_k, cmp_v, hgrn_lower_bound, hgrn_out_norm)
        h = _ffn(h, l, ffn2_norm, *ffn2_w, final_norm, l == depth - 1, (o_nsa, o_hg, w_out))
    return h.reshape(b, s, d)
```

```python
import functools

import jax
import jax.numpy as jnp
import numpy as np
from jax import lax
from jax.experimental import pallas as pl
from jax.experimental.pallas import tpu as pltpu

F32 = jnp.float32
BF16 = jnp.bfloat16

D_MODEL = 1024
EPS = 1e-6
NEG = -1e30
F_FLOOR = 1e-30
NSA_HEADS = 8
NSA_KV_GROUPS = 2
HEADS_PER_GROUP = NSA_HEADS // NSA_KV_GROUPS
NSA_HD = 64
CMP_LEN = 32
CMP_STRIDE = 16
CMP_HID = 256
SEL_BLOCK = 64
SEL_TOPK = 16
WINDOW = 512
HG_HEADS = 4
HG_DK = 128
HG_DV = 128
HG_CHUNK = 64
D_FF = 2752
NSA_WIDTH = NSA_HEADS * NSA_HD
HG_WIDTH = HG_HEADS * HG_DV
KV_W = NSA_KV_GROUPS * NSA_HD

LANES = 128
SUBLANES = 8
FF_CHUNK = 256
D_FF_PAD = -(-D_FF // FF_CHUNK) * FF_CHUNK
ROW_TILE = 512
FFN_ROWS = 1024
ATT_TQ = 256
ATT_TK = 256
CMP_TQ = 1024
VMEM_LIMIT = 56 * 1024 * 1024

SEL_PENALTY = 2.0 ** 50
ALIBI_SLOPES = tuple(2.0 ** (-8.0 * (i + 1) / NSA_HEADS) for i in range(NSA_HEADS))

SEGMENTS = (
    ("q", NSA_HEADS * LANES, BF16, NSA_WIDTH),
    ("ks", NSA_KV_GROUPS * LANES, BF16, KV_W),
    ("vs", NSA_KV_GROUPS * LANES, BF16, KV_W),
    ("kw", NSA_KV_GROUPS * LANES, BF16, KV_W),
    ("vw", NSA_KV_GROUPS * LANES, BF16, KV_W),
    ("kc", KV_W, F32, KV_W),
    ("vc", KV_W, F32, KV_W),
    ("gl", LANES, F32, LANES),
    ("hg", HG_WIDTH, BF16, 4 * HG_WIDTH),
)
SEG_OFFSETS = tuple(int(v) for v in np.cumsum([0] + [s[3] for s in SEGMENTS]))
OUT_GROUPS = (("q",), ("ks", "vs", "kw", "vw"), ("kc", "vc", "gl"), ("hg",))


def _out_slot(name):
    width = {seg[0]: seg[1] for seg in SEGMENTS}
    for gi, group in enumerate(OUT_GROUPS):
        if name in group:
            return gi, sum(width[n] for n in group[:group.index(name)])
    raise KeyError(name)


def _nn(a, b):
    return jnp.dot(a, b, preferred_element_type=F32)


def _nt(a, b):
    return lax.dot_general(a, b, (((1,), (1,)), ((), ())), preferred_element_type=F32)


def _tn(a, b):
    return lax.dot_general(a, b, (((0,), (0,)), ((), ())), preferred_element_type=F32)


def _split2(x):
    hi = x.astype(BF16)
    lo = (x - hi.astype(F32)).astype(BF16)
    return hi, lo


def _split3(x):
    hi = x.astype(BF16)
    r = x - hi.astype(F32)
    mid = r.astype(BF16)
    lo = (r - mid.astype(F32)).astype(BF16)
    return hi, mid, lo


def _sigmoid(x):
    return 1.0 / (1.0 + jnp.exp(-x))


def _rms(x, g):
    return x * lax.rsqrt(jnp.mean(x * x, axis=-1, keepdims=True) + EPS) * g


def _resident(shape):
    nd = len(shape)
    return pl.BlockSpec(shape, lambda *_: (0,) * nd, pipeline_mode=pl.Buffered(1))


def _resident_layer(stacked_shape, layer):
    nd = len(stacked_shape)
    return pl.BlockSpec((None,) + tuple(stacked_shape[1:]), lambda *_: (layer,) + (0,) * (nd - 1),
                        pipeline_mode=pl.Buffered(1))


def _params(sem):
    return pltpu.CompilerParams(dimension_semantics=sem, vmem_limit_bytes=VMEM_LIMIT)


def _ffn_kernel(*refs, final, proj):
    if proj:
        x_ref, a_ref, b_ref, wo_ref, g_ref, wg_ref, wu_ref, wd_ref, gf_ref, o_ref = refs
        x = x_ref[...] + _nn(a_ref[...], wo_ref[:NSA_WIDTH, :]) + _nn(b_ref[...], wo_ref[NSA_WIDTH:, :])
    else:
        x_ref, g_ref, wg_ref, wu_ref, wd_ref, gf_ref, o_ref = refs
        x = x_ref[...]
    xn = _rms(x, g_ref[...]).astype(BF16)
    acc = jnp.zeros(x.shape, F32)
    for k in range(D_FF_PAD // FF_CHUNK):
        sl = slice(k * FF_CHUNK, (k + 1) * FF_CHUNK)
        gate = _nn(xn, wg_ref[:, sl])
        up = _nn(xn, wu_ref[:, sl])
        h = (gate * _sigmoid(gate) * up).astype(BF16)
        acc = acc + _nn(h, wd_ref[sl, :])
    y = x + 0.5 * acc
    if final:
        y = _rms(y, gf_ref[...])
    o_ref[...] = y


def _ffn_weights(w_gu, w_down):
    pad = D_FF_PAD - D_FF
    wg = jnp.pad(w_gu[..., :D_FF], ((0, 0), (0, 0), (0, pad))).astype(BF16)
    wu = jnp.pad(w_gu[..., D_FF:], ((0, 0), (0, 0), (0, pad))).astype(BF16)
    wd = jnp.pad(w_down, ((0, 0), (0, pad), (0, 0))).astype(BF16)
    return wg, wu, wd


def _ffn(h, layer, norm_g, wg, wu, wd, final_g, final, proj=None):
    t = h.shape[0]
    row = lambda w: pl.BlockSpec((FFN_ROWS, w), lambda i: (i, 0))
    norm_g = norm_g[:, None, :]
    weights = [_resident_layer(a.shape, layer) for a in (norm_g, wg, wu, wd)] + [_resident((1, D_MODEL))]
    operands = [norm_g, wg, wu, wd, final_g.reshape(1, -1)]
    if proj is None:
        in_specs, args = [row(D_MODEL)] + weights, [h] + operands
    else:
        o_nsa, o_hg, w_out = proj
        in_specs = [row(D_MODEL), row(NSA_WIDTH), row(HG_WIDTH), _resident_layer(w_out.shape, layer)] + weights
        args = [h, o_nsa, o_hg, w_out] + operands
    return pl.pallas_call(
        functools.partial(_ffn_kernel, final=final, proj=proj is not None),
        grid=(t // FFN_ROWS,),
        in_specs=in_specs,
        out_specs=row(D_MODEL),
        out_shape=jax.ShapeDtypeStruct((t, D_MODEL), F32),
        compiler_params=_params(("parallel",)),
        name="ffn",
    )(*args)


def _inproj_kernel(x_ref, g_ref, w_ref, qfeat_ref, lbraw_ref, onorm_ref, *refs, seq, layer):
    *o_refs, st_scr = refs
    xn = _rms(x_ref[...], g_ref[...]).astype(BF16)
    rows = x_ref.shape[0]

    @pl.when(pl.program_id(0) % (seq // rows) == 0)
    def _():
        st_scr[...] = jnp.zeros(st_scr.shape, F32)

    kpos = (pl.program_id(0) % (seq // rows)) * rows + lax.broadcasted_iota(jnp.int32, (rows, LANES), 0)
    lane = lax.broadcasted_iota(jnp.int32, (rows, LANES), 1)
    blk = kpos // SEL_BLOCK
    digit = lambda f0: jnp.where(lane == f0, blk.astype(F32),
                                 jnp.where(lane == f0 + 1, (kpos % SEL_BLOCK).astype(F32), 0.0))
    halves = [NSA_HD * (1 - g) for g in range(NSA_KV_GROUPS)]
    kfeat = jnp.concatenate([digit(f0) for f0 in halves], axis=1)
    is_pen = lambda f0: (lane == f0 + 1 + blk) & (blk >= 1) & (blk <= seq // SEL_BLOCK - 2)
    kpen = jnp.concatenate([jnp.where(is_pen(f0), -SEL_PENALTY, 0.0) for f0 in halves], axis=1)
    lower = lane < NSA_HD
    swap = lambda a: pltpu.roll(a, NSA_HD, axis=1)
    small = [i for i, seg in enumerate(SEGMENTS) if seg[3] == LANES]
    assert small == list(range(small[0], small[-1] + 1))
    y_small = _nn(xn, w_ref[:, SEG_OFFSETS[small[0]]:SEG_OFFSETS[small[-1] + 1]])
    for i, (name, width, dtype, cols) in enumerate(SEGMENTS):
        if i in small:
            y = y_small[:, (i - small[0]) * LANES:(i - small[0] + 1) * LANES]
        else:
            y = _nn(xn, w_ref[:, SEG_OFFSETS[i]:SEG_OFFSETS[i] + cols])
        if name == "q":
            blocks = []
            for h in range(NSA_HEADS):
                pair = y[:, (h // 2) * LANES:(h // 2 + 1) * LANES]
                g = h // HEADS_PER_GROUP
                data = pair if h % 2 == g else swap(pair)
                feat = qfeat_ref[:, h * LANES:(h + 1) * LANES]
                blocks.append(jnp.where(lower == (g == 0), data, feat))
            y = jnp.concatenate(blocks, axis=1)
        elif name in ("ks", "kw"):
            feat = kfeat + kpen if name == "ks" else kfeat
            y = jnp.concatenate([jnp.where(lower == (g == 0), y, feat[:, g * LANES:(g + 1) * LANES])
                                 for g in range(NSA_KV_GROUPS)], axis=1)
        elif name in ("vs", "vw"):
            y = jnp.concatenate([jnp.where(lower, y if g == 0 else swap(y), 1.0)
                                 for g in range(NSA_KV_GROUPS)], axis=1)
        elif name == "hg":
            y = _hgrn_rows(y, lbraw_ref[...], onorm_ref[...], st_scr, layer)
        out, col = _out_slot(name)
        o_refs[out][:, col:col + width] = y.astype(dtype)


def _q_features():
    feat = np.zeros((1, NSA_HEADS * LANES), np.float32)
    for h in range(NSA_HEADS):
        f0 = h * LANES + NSA_HD * (1 - h // HEADS_PER_GROUP)
        feat[0, f0] = SEL_BLOCK * ALIBI_SLOPES[h]
        feat[0, f0 + 1] = ALIBI_SLOPES[h]
    return jnp.asarray(feat)


def _build_w_in(w_in):
    sizes = (NSA_WIDTH, KV_W, KV_W, KV_W, KV_W, KV_W, KV_W, NSA_HEADS * 3,
             HG_WIDTH, HG_WIDTH, HG_WIDTH, HG_WIDTH)
    splits = [int(v) for v in np.cumsum(sizes)[:-1]]
    w_in = w_in.astype(BF16)
    wq, wkc, wvc, wks, wvs, wkw, wvw, wgl, whq, whf, whi, whg = jnp.split(w_in, splits, axis=-1)
    lead = w_in.shape[:-1]
    gl = jnp.swapaxes(wgl.reshape(lead + (NSA_HEADS, 3)), -1, -2).reshape(lead + (3 * NSA_HEADS,))
    gl = jnp.pad(gl, [(0, 0)] * len(lead) + [(0, LANES - 3 * NSA_HEADS)])
    cols = [wq * NSA_HD ** -0.5, wks, wvs, wkw, wvw, wkc, wvc, gl, whq, whf, whi, whg]
    return jnp.concatenate(cols, axis=-1)


def _inproj(h, norm_g, w_ext, lb_raw, out_norm, seq, layer):
    t = h.shape[0]
    assert seq % ROW_TILE == 0 and ROW_TILE % HG_CHUNK == 0
    row = lambda w: pl.BlockSpec((ROW_TILE, w), lambda i: (i, 0))
    norm_g, out_norm = norm_g[:, None, :], out_norm[:, None, :]
    seg = {name: (width, dtype) for name, width, dtype, _ in SEGMENTS}
    out_arrays = [(sum(seg[n][0] for n in group), seg[group[0]][1]) for group in OUT_GROUPS]
    return pl.pallas_call(
        functools.partial(_inproj_kernel, seq=seq, layer=layer),
        grid=(t // ROW_TILE,),
        in_specs=[row(D_MODEL), _resident_layer(norm_g.shape, layer), _resident_layer(w_ext.shape, layer),
                  _resident((1, NSA_HEADS * LANES)), _resident(lb_raw.shape),
                  _resident_layer(out_norm.shape, layer)],
        out_specs=[row(w) for w, _ in out_arrays],
        out_shape=[jax.ShapeDtypeStruct((t, w), dt) for w, dt in out_arrays],
        scratch_shapes=[pltpu.VMEM((HG_HEADS, HG_DV, HG_DK), F32)],
        compiler_params=_params(("arbitrary",)),
        name="inproj_hgrn",
    )(h, norm_g, w_ext, _q_features(), lb_raw, out_norm)


def _gelu_tanh(x):
    return 0.5 * x * (1.0 + jnp.tanh(0.7978845608028654 * (x + 0.044715 * (x * x * x))))


def _compress_kernel(kv_ref, pos_ref, w1_ref, w2_ref, o_ref):
    nbp = o_ref.shape[1]
    hid = NSA_KV_GROUPS * CMP_HID
    ha = jnp.zeros((nbp, hid), F32)
    hb = jnp.zeros((nbp, hid), F32)
    for l in range(CMP_STRIDE):
        x = kv_ref[0, pl.ds(l, nbp, stride=CMP_STRIDE), :]
        ha = ha + _nn((x + pos_ref[l:l + 1, :]).astype(BF16), w1_ref[l])
        hb = hb + _nn((x + pos_ref[CMP_STRIDE + l:CMP_STRIDE + l + 1, :]).astype(BF16), w1_ref[CMP_STRIDE + l])
    act = _gelu_tanh(ha + pltpu.roll(hb, nbp - 1, axis=0)).astype(BF16)
    out = jnp.zeros(o_ref.shape[1:], F32)
    for g in range(NSA_KV_GROUPS):
        out = out + _nn(act[:, g * CMP_HID:(g + 1) * CMP_HID], w2_ref[g])
    o_ref[0] = out


def _compress_weights(pos, w1, w2, reps):
    layers = w1.shape[0]
    w1l = w1.astype(BF16).reshape(layers, CMP_LEN, NSA_HD, CMP_HID)
    z1 = jnp.zeros_like(w1l)
    w1p = jnp.concatenate([jnp.concatenate([w1l, z1], axis=3), jnp.concatenate([z1, w1l], axis=3)], axis=2)
    pos2 = jnp.concatenate([pos] * NSA_KV_GROUPS, axis=2)
    w2 = w2.astype(BF16)
    zero = jnp.zeros_like(w2)
    w2p = jnp.stack([jnp.concatenate([w2 if r == g else zero for r in reps], axis=2)
                     for g in range(NSA_KV_GROUPS)], axis=1)
    return pos2, w1p, w2p


def _compress(kv, col, layer, pos2, w1p, w2p):
    b, s, _ = kv.shape
    nbp = s // CMP_STRIDE
    width = w2p.shape[-1]
    return pl.pallas_call(
        _compress_kernel,
        grid=(b,),
        in_specs=[pl.BlockSpec((1, s, KV_W), lambda i: (i, 0, col))]
        + [_resident_layer(a.shape, layer) for a in (pos2, w1p, w2p)],
        out_specs=pl.BlockSpec((1, nbp, width), lambda i: (i, 0, 0)),
        out_shape=jax.ShapeDtypeStruct((b, nbp, width), F32),
        compiler_params=_params(("parallel",)),
        name="compress",
    )(kv, pos2, w1p, w2p)


def _pair_blocks(g, jj):
    lower = slice(0, LANES) if g == 0 else slice(LANES, 2 * LANES)
    upper = slice(LANES, 2 * LANES) if g == 0 else slice(0, LANES)
    return lower, upper


def _cmp_kernel(q_ref, kc_ref, vc2_ref, mt_ref, ocmp_ref, sel_ref, picks_ref, *, tq, ns, n_sel):
    nbp = kc_ref.shape[1]
    q0 = pl.program_id(1) * tq
    pos = q0 + lax.broadcasted_iota(jnp.int32, (tq, nbp), 0)
    blk_end = lax.broadcasted_iota(jnp.int32, (tq, nbp), 1) * CMP_STRIDE + (CMP_LEN - 1)
    valid = blk_end <= pos
    row_ok = (q0 + lax.broadcasted_iota(jnp.int32, (tq, 1), 0)) >= CMP_LEN - 1
    kc = kc_ref[0].astype(BF16)
    vc2 = vc2_ref[0].astype(BF16)
    lane_c = lax.broadcasted_iota(jnp.int32, (nbp, LANES), 1)
    c_idx = lax.broadcasted_iota(jnp.int32, (nbp, LANES), 0)
    per = SEL_BLOCK // CMP_STRIDE
    feat_a = (c_idx // per - q0 // SEL_BLOCK).astype(F32)
    feat_b = ((c_idx % per) * CMP_STRIDE + (CMP_LEN - 1)).astype(F32)
    lane_lo = lax.broadcasted_iota(jnp.int32, (tq, LANES), 1) < NSA_HD
    blk_f = lax.broadcasted_iota(jnp.int32, (ns, tq), 0).astype(F32)
    blk = lax.broadcasted_iota(jnp.int32, (ns, tq), 0)
    pos_t = q0 + lax.broadcasted_iota(jnp.int32, (ns, tq), 1)
    cur = pos_t // SEL_BLOCK
    forced = (blk == 0) | (blk == cur) | (blk == cur - 1)
    causal = blk * SEL_BLOCK <= pos_t
    sel_t, picks = [], []
    taken = -jnp.inf
    for g in range(NSA_KV_GROUPS):
        f0 = NSA_HD * (1 - g)
        feat = jnp.where(lane_c == f0, feat_a, jnp.where(lane_c == f0 + 1, feat_b, 0.0))
        kc_g = jnp.where((lane_c // NSA_HD) == g, kc, feat.astype(BF16))
        imp = jnp.zeros((tq, nbp), F32)
        acc = []
        for hh in range(HEADS_PER_GROUP):
            h = g * HEADS_PER_GROUP + hh
            s = _nt(q_ref[0, :, h * LANES:(h + 1) * LANES], kc_g)
            s = jnp.where(valid, s, NEG)
            e = jnp.exp(s - jnp.max(s, axis=-1, keepdims=True))
            inv = jnp.where(row_ok, 1.0 / jnp.sum(e, axis=-1, keepdims=True), 0.0)
            p = e * inv
            imp = imp + p
            acc.append(_nn(p.astype(BF16), vc2))
        for jj in range(HEADS_PER_GROUP // 2):
            lower, upper = _pair_blocks(g, jj)
            blk_out = jnp.where(lane_lo, acc[2 * jj][:, lower], acc[2 * jj + 1][:, upper])
            c0 = (g * HEADS_PER_GROUP + 2 * jj) * NSA_HD
            ocmp_ref[0, :, c0:c0 + LANES] = blk_out
        p_slc = sum(_nt(mt_ref[...], part) for part in _split3(imp))
        score = jnp.where(forced, taken, jnp.where(causal, p_slc, NEG))
        for _ in range(n_sel - 3):
            top = jnp.max(score, axis=0, keepdims=True)
            first = jnp.min(jnp.where(score == top, blk_f, float(ns)), axis=0, keepdims=True)
            score = jnp.where(blk_f == first, taken, score)
        picked = jnp.where(score == taken, 1.0, 0.0).astype(BF16)
        picks.append([_nt(jnp.ones((SUBLANES, ATT_TQ), BF16), picked[:, t * ATT_TQ:(t + 1) * ATT_TQ])
                      for t in range(tq // ATT_TQ)])
        pen = jnp.where(score == taken, 0.0, SEL_PENALTY)
        pen = jnp.where((blk >= 2) & (blk <= ns - 1), pltpu.roll(pen, 1, axis=0), 0.0)
        if ns < NSA_HD:
            pen = jnp.concatenate([pen, jnp.zeros((NSA_HD - ns, tq), F32)], axis=0)
        zero = jnp.zeros((NSA_HD, tq), F32)
        sel_t += [zero, pen] if g == 0 else [pen, zero]
    sel_ref[0] = jnp.concatenate(sel_t, axis=0).T.astype(BF16)
    for t in range(tq // ATT_TQ):
        picks_ref[0, t] = jnp.concatenate([group[t] for group in picks], axis=1)


def _importance_matrix(nbp, ns):
    per = SEL_BLOCK // CMP_STRIDE
    m = np.zeros((ns, nbp), np.float32)
    for n in range(ns):
        for c in range(per * n, per * (n + 1)):
            for cc in (c - 1, c):
                if 0 <= cc < nbp - 1:
                    m[n, cc] += 1.0
    return jnp.asarray(m, BF16)


def _cmp_attention(q, kc, vc2, b, s):
    nbp = s // CMP_STRIDE
    ns = s // SEL_BLOCK
    tq = min(CMP_TQ, s)
    assert ns <= NSA_HD
    assert min(SEL_TOPK, ns) >= 3 and tq % ATT_TQ == 0
    q3 = q.reshape(b, s, NSA_HEADS * LANES)
    kern = functools.partial(_cmp_kernel, tq=tq, ns=ns, n_sel=min(SEL_TOPK, ns))
    return pl.pallas_call(
        kern,
        grid=(b, s // tq),
        in_specs=[pl.BlockSpec((1, tq, NSA_HEADS * LANES), lambda i, j: (i, j, 0)),
                  pl.BlockSpec((1, nbp, KV_W), lambda i, j: (i, 0, 0)),
                  pl.BlockSpec((1, nbp, 2 * KV_W), lambda i, j: (i, 0, 0)),
                  _resident((ns, nbp))],
        out_specs=[pl.BlockSpec((1, tq, NSA_WIDTH), lambda i, j: (i, j, 0)),
                   pl.BlockSpec((1, tq, NSA_KV_GROUPS * LANES), lambda i, j: (i, j, 0)),
                   pl.BlockSpec((1, tq // ATT_TQ, SUBLANES, NSA_KV_GROUPS * ns), lambda i, j: (i, j, 0, 0))],
        out_shape=[jax.ShapeDtypeStruct((b, s, NSA_WIDTH), F32),
                   jax.ShapeDtypeStruct((b, s, NSA_KV_GROUPS * LANES), BF16),
                   jax.ShapeDtypeStruct((b, s // ATT_TQ, SUBLANES, NSA_KV_GROUPS * ns), F32)],
        compiler_params=_params(("parallel", "parallel")),
        name="cmp_topk",
    )(q3, kc, vc2, _importance_matrix(nbp, ns))


def _attn_kernel(tiles_ref, ntiles_ref, q_ref, ks_ref, vs_ref, kw_ref, vw_ref, penq_ref, gl_ref, gexp_ref,
                 ocmp_ref, o_ref, qa_scr, s_scr, p_scr, al_scr, m_scr, acc_scr, *, tq, tk):
    i = pl.program_id(1)
    rows_of = lambda hh: slice(hh * tq, (hh + 1) * tq)
    lane_lo = lax.broadcasted_iota(jnp.int32, (tq, LANES), 1) < NSA_HD
    rel = (lax.broadcasted_iota(jnp.int32, (tq, tk), 1)
           - lax.broadcasted_iota(jnp.int32, (tq, tk), 0)).astype(F32)

    for hh in range(HEADS_PER_GROUP):
        qa_scr[rows_of(hh)] = q_ref[0, :, hh * LANES:(hh + 1) * LANES] + penq_ref[0]

    def scores(k_ref, j, slot):
        s_scr[slot] = _nt(qa_scr[...], k_ref[0, pl.ds(pl.multiple_of(j * tk, tk), tk), :])

    def probs(br, j, mode, slot, first, exists=None):
        for hh in range(HEADS_PER_GROUP):
            r = rows_of(hh)
            s = s_scr[slot, r]
            if mode == "causal":
                s = jnp.where(rel <= 0.0, s, NEG)
            elif mode == "lower":
                s = jnp.where(rel + (j * tk - i * tq).astype(F32) > -float(WINDOW), s, NEG)
            if exists is not None:
                s = jnp.where(exists, s, NEG)
            m_cur = jnp.max(s, axis=-1, keepdims=True)
            if first:
                m_new = jnp.broadcast_to(m_cur, (tq, LANES))
            else:
                m_prev = m_scr[br, r]
                m_new = jnp.maximum(m_prev, m_cur)
                al_scr[slot, r] = jnp.exp(m_prev - m_new)
            m_scr[br, r] = m_new
            p_scr[slot, r] = jnp.exp(s - jnp.concatenate([m_new] * (tk // LANES), axis=1)).astype(BF16)

    def accumulate(br, v_ref, j, slot, first):
        pv = _nn(p_scr[slot], v_ref[0, pl.ds(pl.multiple_of(j * tk, tk), tk), :])
        for hh in range(HEADS_PER_GROUP):
            r = rows_of(hh)
            if first:
                acc_scr[br, r] = pv[r]
            else:
                acc_scr[br, r] = al_scr[slot, r] * acc_scr[br, r] + pv[r]

    step = (pl.program_id(0) * pl.num_programs(1) + i) * pl.num_programs(2) + pl.program_id(2)
    n_tiles = ntiles_ref[step]
    item = lambda br, j, mode, first, exists=None: (br, j, mode, first, exists)
    sel_item = lambda n: item(0, i, "causal", True) if isinstance(n, int) and n == 0 else item(
        0, tiles_ref[step * pl.num_programs(1) + n - 1], None, False)
    n_back = WINDOW // tk
    win_items = [item(1, i, "causal", True)] + [
        item(1, jnp.maximum(i - d, 0), "lower" if d == n_back else None, False, i >= d)
        for d in range(1, n_back + 1)]
    k_refs, v_refs = (ks_ref, kw_ref), (vs_ref, vw_ref)

    def run(items, slot0, done=(0, 0, 0), upto=None):
        n_items = len(items)
        slot = lambda n: (slot0 + n) % 2
        n_s, n_p, n_a = done

        def do_scores(n):
            br, j, _, _, _ = items[n]
            scores(k_refs[br], j, slot(n))

        def do_probs(n):
            br, j, mode, first, exists = items[n]
            probs(br, j, mode, slot(n), first, exists)

        def do_acc(n):
            br, j, _, first, _ = items[n]
            accumulate(br, v_refs[br], j, slot(n), first)

        for n in range(n_s, min(n_a + 2, n_items)):
            do_scores(n)
        n_s = max(n_s, min(n_a + 2, n_items))
        for n in range(n_p, min(n_a + 1, n_items)):
            do_probs(n)
        n_p = max(n_p, min(n_a + 1, n_items))
        for k in range(n_a, n_items if upto is None else upto):
            if n_s <= k + 2 < n_items:
                do_scores(k + 2)
                n_s = k + 3
            if n_p <= k + 1 < n_items:
                do_probs(k + 1)
                n_p = k + 2
            do_acc(k)

    for n_static in range(2):
        @pl.when(n_tiles == n_static)
        def _(n_static=n_static):
            run([sel_item(n) for n in range(n_static + 1)] + win_items, 0)

    @pl.when(n_tiles >= 2)
    def _():
        run([sel_item(n) for n in range(3)], 0, upto=1)

        def sel_step(n, parity):
            scores(ks_ref, sel_item(n + 2)[1], parity)
            probs(0, None, None, 1 - parity, False)
            accumulate(0, vs_ref, sel_item(n)[1], parity, False)

        def sel_two_steps(t, carry):
            sel_step(2 * t + 1, 1)
            sel_step(2 * t + 2, 0)
            return carry

        lax.fori_loop(0, (n_tiles - 2) // 2, sel_two_steps, 0)

        def drain(parity):
            tail = [sel_item(n_tiles - 1), sel_item(n_tiles)] + win_items
            run(tail, 1 - parity, done=(2, 1, 0))

        @pl.when(n_tiles % 2 == 0)
        def _():
            drain(0)

        @pl.when(n_tiles % 2 == 1)
        def _():
            sel_step(n_tiles - 2, 1)
            drain(1)

    gexp = sum(_nn(part, gexp_ref[0]) for part in _split2(_sigmoid(gl_ref[0])))
    n_pairs = HEADS_PER_GROUP // 2
    gate_blk = lambda br, jj: gexp[:, (br * n_pairs + jj) * LANES:(br * n_pairs + jj + 1) * LANES]
    for jj in range(n_pairs):
        cols = slice(jj * LANES, (jj + 1) * LANES)
        blk = gate_blk(0, jj) * ocmp_ref[0, :, cols]
        for br in range(2):
            even = acc_scr[br, rows_of(2 * jj)]
            odd = acc_scr[br, rows_of(2 * jj + 1)]
            low = even / pltpu.roll(even, NSA_HD, axis=1)
            up = pltpu.roll(odd, NSA_HD, axis=1) / odd
            blk = blk + gate_blk(br + 1, jj) * jnp.where(lane_lo, low, up)
        o_ref[0, :, cols] = blk.astype(o_ref.dtype)


def _gate_expansion():
    n_pairs = HEADS_PER_GROUP // 2
    r = np.zeros((NSA_KV_GROUPS, LANES, 3 * n_pairs * LANES), np.float32)
    for g in range(NSA_KV_GROUPS):
        for br in range(3):
            for jj in range(n_pairs):
                for odd in range(2):
                    src = br * NSA_HEADS + g * HEADS_PER_GROUP + 2 * jj + odd
                    dst = (br * n_pairs + jj) * LANES + odd * NSA_HD
                    r[g, src, dst:dst + NSA_HD] = 1.0
    return jnp.asarray(r, BF16)


def _picked_tiles(picks, b, s):
    ns, nq = s // SEL_BLOCK, s // ATT_TQ
    per_tile = ATT_TK // SEL_BLOCK
    per_block = picks[:, :, 0, :].reshape(b, nq, NSA_KV_GROUPS, ns // per_tile, per_tile)
    j = jnp.arange(ns // per_tile, dtype=jnp.int32)
    i = jnp.arange(nq, dtype=jnp.int32)[None, :, None, None]
    active = ((per_block.sum(-1) > 0) | (j == 0)) & (j < i)
    slot = jnp.cumsum(active, axis=-1) - 1
    hit = active[..., None, :] & (slot[..., None, :] == j[:, None])
    tiles = jnp.sum(jnp.where(hit, j, 0), axis=-1).astype(jnp.int32)
    return tiles.reshape(-1), active.sum(-1).astype(jnp.int32).reshape(-1)


def _attention(q, kv, small, pen, picks, ocmp, b, s):
    tq, tk = ATT_TQ, ATT_TK
    assert tq == tk and WINDOW % tk == 0 and WINDOW // tk <= 2 and tq % SEL_BLOCK == 0
    gw = HEADS_PER_GROUP * LANES
    ow = HEADS_PER_GROUP * NSA_HD
    r3 = lambda a: a.reshape(b, s, a.shape[-1])
    gtile = lambda w: pl.BlockSpec((1, tq, w), lambda i, j, g, *_: (i, j, g))
    kv_block = lambda name: pl.BlockSpec(
        (1, s, LANES), lambda i, j, g, *_, c=_out_slot(name)[1] // LANES: (i, 0, c + g))
    gl_block = pl.BlockSpec((1, tq, LANES), lambda i, j, g, *_: (i, j, _out_slot("gl")[1] // LANES))
    gexp = _gate_expansion()
    rows = HEADS_PER_GROUP * tq
    kern = functools.partial(_attn_kernel, tq=tq, tk=tk)
    tiles, n_tiles = _picked_tiles(picks, b, s)
    grid_spec = pltpu.PrefetchScalarGridSpec(
        num_scalar_prefetch=2,
        grid=(b, s // tq, NSA_KV_GROUPS),
        in_specs=[gtile(gw), kv_block("ks"), kv_block("vs"), kv_block("kw"), kv_block("vw"), gtile(LANES),
                  gl_block, pl.BlockSpec((1,) + gexp.shape[1:], lambda i, j, g, *_: (g, 0, 0)), gtile(ow)],
        out_specs=gtile(ow),
        scratch_shapes=[pltpu.VMEM((rows, LANES), BF16), pltpu.VMEM((2, rows, tk), F32),
                        pltpu.VMEM((2, rows, tk), BF16), pltpu.VMEM((2, rows, LANES), F32),
                        pltpu.VMEM((2, rows, LANES), F32), pltpu.VMEM((2, rows, LANES), F32)])
    return pl.pallas_call(
        kern,
        grid_spec=grid_spec,
        out_shape=jax.ShapeDtypeStruct((b, s, NSA_WIDTH), BF16),
        compiler_params=_params(("parallel", "parallel", "parallel")),
        name="sel_win_attention",
    )(tiles, n_tiles, r3(q), r3(kv), r3(kv), r3(kv), r3(kv), pen, r3(small), gexp, ocmp)


def _hgrn_rows(y, raw, onorm, st_scr, layer):
    c = HG_CHUNK
    ex = jnp.exp(raw - jnp.max(raw, axis=0, keepdims=True))
    sm = ex / jnp.sum(ex, axis=0, keepdims=True)
    lb_all = jnp.zeros((1, raw.shape[1]), F32)
    for l in range(1, layer + 1):
        lb_all = lb_all + sm[l:l + 1, :]

    t_idx = lax.broadcasted_iota(jnp.int32, (c, HG_DK), 0)
    sub = lax.broadcasted_iota(jnp.int32, (SUBLANES, HG_DK), 0)
    ti = lax.broadcasted_iota(jnp.int32, (c, c), 0)
    si = lax.broadcasted_iota(jnp.int32, (c, c), 1)
    tril = jnp.where(si <= ti, 1.0, 0.0).astype(BF16)
    levels = (32, 16, 8, 4, 2, 1)
    hc = HG_HEADS * c
    tb = lax.broadcasted_iota(jnp.int32, (hc, hc), 0)
    sb = lax.broadcasted_iota(jnp.int32, (hc, hc), 1)
    same_head = (tb // c) == (sb // c)
    split_bit = tb ^ sb
    pair_mask = {m: jnp.where(same_head & (tb > sb) & (split_bit >= m) & (split_bit < 2 * m), 1.0, 0.0)
                 for m in levels}
    diagonal = tb == sb
    second_half = {m: (t_idx & m) != 0 for m in levels}
    sign = {m: jnp.where(second_half[m], 1.0, -1.0) for m in levels}

    def ref_rows(b, m):
        row = lambda r, n: jnp.broadcast_to(b[r:r + 1, :], (n, HG_DK))
        if m >= 4:
            return jnp.concatenate([row(s0 + m - 1, 2 * m) for s0 in range(0, c, 2 * m)], axis=0)
        return jnp.concatenate([jnp.where(sub < 4, row(s0 + 1, SUBLANES), row(s0 + 5, SUBLANES))
                                for s0 in range(0, c, SUBLANES)], axis=0)

    part = lambda which, rows, h: y[rows, which * HG_WIDTH + h * HG_DK:which * HG_WIDTH + (h + 1) * HG_DK]
    out = []
    for ci in range(y.shape[0] // c):
        rows = slice(ci * c, (ci + 1) * c)
        q, k, f, v, logf = [], [], [], [], []
        for h in range(HG_HEADS):
            lb = lb_all[:, h * HG_DK:(h + 1) * HG_DK]
            z = part(1, rows, h)
            hq = part(0, rows, h)
            q.append(hq * _sigmoid(hq))
            ez = jnp.exp(-jnp.abs(z))
            big = 1.0 / (1.0 + ez)
            small = ez * big
            f.append(jnp.maximum(lb + (1.0 - lb) * jnp.where(z >= 0.0, big, small), F_FLOOR))
            logf.append(jnp.log2(f[h]))
            k.append((1.0 - lb) * jnp.where(z >= 0.0, small, big))
            v.append(part(2, rows, h).astype(BF16))
        parts = jnp.concatenate([p for h in range(HG_HEADS) for p in _split3(logf[h])], axis=1)
        csum = _nn(tril, parts)
        bcum = [sum(csum[:, (3 * h + i) * HG_DK:(3 * h + i + 1) * HG_DK] for i in range(3))
                for h in range(HG_HEADS)]

        rowdot = jnp.concatenate([jnp.sum(q[h] * k[h], axis=-1, keepdims=True) for h in range(HG_HEADS)], axis=0)
        a = jnp.where(diagonal, rowdot, 0.0)
        for m in levels:
            r = []
            for h in range(HG_HEADS):
                if m == 1:
                    r.append(jnp.where(second_half[m], q[h] * f[h], k[h]))
                else:
                    w = jnp.exp2((bcum[h] - ref_rows(bcum[h], m)) * sign[m])
                    r.append(jnp.where(second_half[m], q[h], k[h]) * w)
            r = jnp.concatenate(r, axis=0).astype(BF16)
            a = a + _nt(r, r) * pair_mask[m]
        o_intra = _nn(a.astype(BF16), jnp.concatenate(v, axis=0))

        heads = []
        for h in range(HG_HEADS):
            st = st_scr[h]
            o = o_intra[h * c:(h + 1) * c] + _nt((q[h] * jnp.exp2(bcum[h])).astype(BF16), st.astype(BF16))
            b_last = bcum[h][c - 1:c, :]
            kd = (k[h] * jnp.exp2(b_last - bcum[h])).astype(BF16)
            st_scr[h] = jnp.exp2(b_last) * st + _tn(v[h], kd)

            o = o * lax.rsqrt(jnp.mean(o * o, axis=-1, keepdims=True) + EPS) * onorm
            gate = part(3, rows, h)
            heads.append((o * (gate * _sigmoid(gate))).astype(BF16))
        out.append(jnp.concatenate(heads, axis=1))
    return jnp.concatenate(out, axis=0)


def _mixer(h, l, b, s, mix_norm, w_ext, cmp_k, cmp_v, hgrn_lower_bound, hgrn_out_norm):
    q, kv, small, o_hg = _inproj(h, mix_norm, w_ext, hgrn_lower_bound, hgrn_out_norm, s, l)
    small3 = small.reshape(b, s, small.shape[-1])
    kc = _compress(small3, _out_slot("kc")[1] // KV_W, l, *cmp_k)
    vc2 = _compress(small3, _out_slot("vc")[1] // KV_W, l, *cmp_v)
    ocmp, sel, picks = _cmp_attention(q, kc, vc2, b, s)
    o_nsa = _attention(q, kv, small, sel, picks, ocmp, b, s)
    return o_nsa.reshape(b * s, NSA_WIDTH), o_hg


def kernel(x, ffn1_norm, ffn1_w_gu, ffn1_w_down, mix_norm, w_in, cmp_pos_k, cmp_pos_v, cmp_k_w1, cmp_k_w2, cmp_v_w1, cmp_v_w2, hgrn_lower_bound, hgrn_out_norm, w_out, ffn2_norm, ffn2_w_gu, ffn2_w_down, final_norm):
    b, s, d = x.shape
    depth = ffn1_norm.shape[0]
    ffn1_w = _ffn_weights(ffn1_w_gu, ffn1_w_down)
    ffn2_w = _ffn_weights(ffn2_w_gu, ffn2_w_down)
    w_out = w_out.astype(BF16)
    w_ext = _build_w_in(w_in)
    cmp_k = _compress_weights(cmp_pos_k, cmp_k_w1, cmp_k_w2, (0, 1))
    cmp_v = _compress_weights(cmp_pos_v, cmp_v_w1, cmp_v_w2, (0, 1, 1, 0))
    h = x.reshape(b * s, d)
    for l in range(depth):
        h = _ffn(h, l, ffn1_norm, *ffn1_w, final_norm, False)
        o_nsa, o_hg = _mixer(h, l, b, s, mix_norm, w_ext, cmp_k, cmp_v, hgrn_lower_bound, hgrn_out_norm)
        h = _ffn(h, l, ffn2_norm, *ffn2_w, final_norm, l == depth - 1, (o_nsa, o_hg, w_out))
    return h.reshape(b, s, d)
```

```python
import functools

import jax
import jax.numpy as jnp
import numpy as np
from jax import lax
from jax.experimental import pallas as pl
from jax.experimental.pallas import tpu as pltpu

F32 = jnp.float32
BF16 = jnp.bfloat16

D_MODEL = 1024
EPS = 1e-6
NEG = -1e30
F_FLOOR = 1e-30
NSA_HEADS = 8
NSA_KV_GROUPS = 2
HEADS_PER_GROUP = NSA_HEADS // NSA_KV_GROUPS
NSA_HD = 64
CMP_LEN = 32
CMP_STRIDE = 16
CMP_HID = 256
SEL_BLOCK = 64
SEL_TOPK = 16
WINDOW = 512
HG_HEADS = 4
HG_DK = 128
HG_DV = 128
HG_CHUNK = 64
D_FF = 2752
NSA_WIDTH = NSA_HEADS * NSA_HD
HG_WIDTH = HG_HEADS * HG_DV
KV_W = NSA_KV_GROUPS * NSA_HD

LANES = 128
SUBLANES = 8
FF_CHUNK = 256
D_FF_PAD = -(-D_FF // FF_CHUNK) * FF_CHUNK
ROW_TILE = 512
FFN_ROWS = 1024
ATT_TQ = 256
ATT_TK = 256
CMP_TQ = 1024
VMEM_LIMIT = 56 * 1024 * 1024

SEL_PENALTY = 2.0 ** 50
ALIBI_SLOPES = tuple(2.0 ** (-8.0 * (i + 1) / NSA_HEADS) for i in range(NSA_HEADS))

SEGMENTS = (
    ("q", NSA_HEADS * LANES, BF16, NSA_WIDTH),
    ("ks", NSA_KV_GROUPS * LANES, BF16, KV_W),
    ("vs", NSA_KV_GROUPS * LANES, BF16, KV_W),
    ("kw", NSA_KV_GROUPS * LANES, BF16, KV_W),
    ("vw", NSA_KV_GROUPS * LANES, BF16, KV_W),
    ("kc", KV_W, F32, KV_W),
    ("vc", KV_W, F32, KV_W),
    ("gl", LANES, F32, LANES),
    ("hg", HG_WIDTH, BF16, 4 * HG_WIDTH),
)
SEG_OFFSETS = tuple(int(v) for v in np.cumsum([0] + [s[3] for s in SEGMENTS]))


def _nn(a, b):
    return jnp.dot(a, b, preferred_element_type=F32)


def _nt(a, b):
    return lax.dot_general(a, b, (((1,), (1,)), ((), ())), preferred_element_type=F32)


def _tn(a, b):
    return lax.dot_general(a, b, (((0,), (0,)), ((), ())), preferred_element_type=F32)


def _split2(x):
    hi = x.astype(BF16)
    lo = (x - hi.astype(F32)).astype(BF16)
    return hi, lo


def _split3(x):
    hi = x.astype(BF16)
    r = x - hi.astype(F32)
    mid = r.astype(BF16)
    lo = (r - mid.astype(F32)).astype(BF16)
    return hi, mid, lo


def _sigmoid(x):
    return 1.0 / (1.0 + jnp.exp(-x))


def _rms(x, g):
    return x * lax.rsqrt(jnp.mean(x * x, axis=-1, keepdims=True) + EPS) * g


def _resident(shape):
    nd = len(shape)
    return pl.BlockSpec(shape, lambda *_: (0,) * nd, pipeline_mode=pl.Buffered(1))


def _resident_layer(stacked_shape, layer):
    nd = len(stacked_shape)
    return pl.BlockSpec((None,) + tuple(stacked_shape[1:]), lambda *_: (layer,) + (0,) * (nd - 1),
                        pipeline_mode=pl.Buffered(1))


def _params(sem):
    return pltpu.CompilerParams(dimension_semantics=sem, vmem_limit_bytes=VMEM_LIMIT)


def _ffn_kernel(*refs, final, proj):
    if proj:
        x_ref, a_ref, b_ref, wo_ref, g_ref, wg_ref, wu_ref, wd_ref, gf_ref, o_ref = refs
        x = x_ref[...] + _nn(a_ref[...], wo_ref[:NSA_WIDTH, :]) + _nn(b_ref[...], wo_ref[NSA_WIDTH:, :])
    else:
        x_ref, g_ref, wg_ref, wu_ref, wd_ref, gf_ref, o_ref = refs
        x = x_ref[...]
    xn = _rms(x, g_ref[...]).astype(BF16)
    acc = jnp.zeros(x.shape, F32)
    for k in range(D_FF_PAD // FF_CHUNK):
        sl = slice(k * FF_CHUNK, (k + 1) * FF_CHUNK)
        gate = _nn(xn, wg_ref[:, sl])
        up = _nn(xn, wu_ref[:, sl])
        h = (gate * _sigmoid(gate) * up).astype(BF16)
        acc = acc + _nn(h, wd_ref[sl, :])
    y = x + 0.5 * acc
    if final:
        y = _rms(y, gf_ref[...])
    o_ref[...] = y


def _ffn_weights(w_gu, w_down):
    pad = D_FF_PAD - D_FF
    wg = jnp.pad(w_gu[..., :D_FF], ((0, 0), (0, 0), (0, pad))).astype(BF16)
    wu = jnp.pad(w_gu[..., D_FF:], ((0, 0), (0, 0), (0, pad))).astype(BF16)
    wd = jnp.pad(w_down, ((0, 0), (0, pad), (0, 0))).astype(BF16)
    return wg, wu, wd


def _ffn(h, layer, norm_g, wg, wu, wd, final_g, final, proj=None):
    t = h.shape[0]
    row = lambda w: pl.BlockSpec((FFN_ROWS, w), lambda i: (i, 0))
    norm_g = norm_g[:, None, :]
    weights = [_resident_layer(a.shape, layer) for a in (norm_g, wg, wu, wd)] + [_resident((1, D_MODEL))]
    operands = [norm_g, wg, wu, wd, final_g.reshape(1, -1)]
    if proj is None:
        in_specs, args = [row(D_MODEL)] + weights, [h] + operands
    else:
        o_nsa, o_hg, w_out = proj
        in_specs = [row(D_MODEL), row(NSA_WIDTH), row(HG_WIDTH), _resident_layer(w_out.shape, layer)] + weights
        args = [h, o_nsa, o_hg, w_out] + operands
    return pl.pallas_call(
        functools.partial(_ffn_kernel, final=final, proj=proj is not None),
        grid=(t // FFN_ROWS,),
        in_specs=in_specs,
        out_specs=row(D_MODEL),
        out_shape=jax.ShapeDtypeStruct((t, D_MODEL), F32),
        compiler_params=_params(("parallel",)),
        name="ffn",
    )(*args)


def _inproj_kernel(x_ref, g_ref, w_ref, qfeat_ref, lbraw_ref, onorm_ref, *refs, seq, layer):
    *o_refs, st_scr = refs
    xn = _rms(x_ref[...], g_ref[...]).astype(BF16)
    rows = x_ref.shape[0]

    @pl.when(pl.program_id(0) % (seq // rows) == 0)
    def _():
        st_scr[...] = jnp.zeros(st_scr.shape, F32)

    kpos = (pl.program_id(0) % (seq // rows)) * rows + lax.broadcasted_iota(jnp.int32, (rows, LANES), 0)
    lane = lax.broadcasted_iota(jnp.int32, (rows, LANES), 1)
    blk = kpos // SEL_BLOCK
    digit = lambda f0: jnp.where(lane == f0, blk.astype(F32),
                                 jnp.where(lane == f0 + 1, (kpos % SEL_BLOCK).astype(F32), 0.0))
    halves = [NSA_HD * (1 - g) for g in range(NSA_KV_GROUPS)]
    kfeat = jnp.concatenate([digit(f0) for f0 in halves], axis=1)
    is_pen = lambda f0: (lane == f0 + 1 + blk) & (blk >= 1) & (blk <= seq // SEL_BLOCK - 2)
    kpen = jnp.concatenate([jnp.where(is_pen(f0), -SEL_PENALTY, 0.0) for f0 in halves], axis=1)
    lower = lane < NSA_HD
    swap = lambda a: pltpu.roll(a, NSA_HD, axis=1)
    small = [i for i, seg in enumerate(SEGMENTS) if seg[3] == LANES]
    assert small == list(range(small[0], small[-1] + 1))
    y_small = _nn(xn, w_ref[:, SEG_OFFSETS[small[0]]:SEG_OFFSETS[small[-1] + 1]])
    for i, (o_ref, (name, _, dtype, cols)) in enumerate(zip(o_refs, SEGMENTS, strict=True)):
        if i in small:
            y = y_small[:, (i - small[0]) * LANES:(i - small[0] + 1) * LANES]
        else:
            y = _nn(xn, w_ref[:, SEG_OFFSETS[i]:SEG_OFFSETS[i] + cols])
        if name == "q":
            blocks = []
            for h in range(NSA_HEADS):
                pair = y[:, (h // 2) * LANES:(h // 2 + 1) * LANES]
                g = h // HEADS_PER_GROUP
                data = pair if h % 2 == g else swap(pair)
                feat = qfeat_ref[:, h * LANES:(h + 1) * LANES]
                blocks.append(jnp.where(lower == (g == 0), data, feat))
            y = jnp.concatenate(blocks, axis=1)
        elif name in ("ks", "kw"):
            feat = kfeat + kpen if name == "ks" else kfeat
            y = jnp.concatenate([jnp.where(lower == (g == 0), y, feat[:, g * LANES:(g + 1) * LANES])
                                 for g in range(NSA_KV_GROUPS)], axis=1)
        elif name in ("vs", "vw"):
            y = jnp.concatenate([jnp.where(lower, y if g == 0 else swap(y), 1.0)
                                 for g in range(NSA_KV_GROUPS)], axis=1)
        elif name == "hg":
            y = _hgrn_rows(y, lbraw_ref[...], onorm_ref[...], st_scr, layer)
        o_ref[...] = y.astype(dtype)


def _q_features():
    feat = np.zeros((1, NSA_HEADS * LANES), np.float32)
    for h in range(NSA_HEADS):
        f0 = h * LANES + NSA_HD * (1 - h // HEADS_PER_GROUP)
        feat[0, f0] = SEL_BLOCK * ALIBI_SLOPES[h]
        feat[0, f0 + 1] = ALIBI_SLOPES[h]
    return jnp.asarray(feat)


def _build_w_in(w_in):
    sizes = (NSA_WIDTH, KV_W, KV_W, KV_W, KV_W, KV_W, KV_W, NSA_HEADS * 3,
             HG_WIDTH, HG_WIDTH, HG_WIDTH, HG_WIDTH)
    splits = [int(v) for v in np.cumsum(sizes)[:-1]]
    w_in = w_in.astype(BF16)
    wq, wkc, wvc, wks, wvs, wkw, wvw, wgl, whq, whf, whi, whg = jnp.split(w_in, splits, axis=-1)
    lead = w_in.shape[:-1]
    gl = jnp.swapaxes(wgl.reshape(lead + (NSA_HEADS, 3)), -1, -2).reshape(lead + (3 * NSA_HEADS,))
    gl = jnp.pad(gl, [(0, 0)] * len(lead) + [(0, LANES - 3 * NSA_HEADS)])
    cols = [wq * NSA_HD ** -0.5, wks, wvs, wkw, wvw, wkc, wvc, gl, whq, whf, whi, whg]
    return jnp.concatenate(cols, axis=-1)


def _inproj(h, norm_g, w_ext, lb_raw, out_norm, seq, layer):
    t = h.shape[0]
    assert seq % ROW_TILE == 0 and ROW_TILE % HG_CHUNK == 0
    row = lambda w: pl.BlockSpec((ROW_TILE, w), lambda i: (i, 0))
    norm_g, out_norm = norm_g[:, None, :], out_norm[:, None, :]
    return pl.pallas_call(
        functools.partial(_inproj_kernel, seq=seq, layer=layer),
        grid=(t // ROW_TILE,),
        in_specs=[row(D_MODEL), _resident_layer(norm_g.shape, layer), _resident_layer(w_ext.shape, layer),
                  _resident((1, NSA_HEADS * LANES)), _resident(lb_raw.shape),
                  _resident_layer(out_norm.shape, layer)],
        out_specs=[row(w) for _, w, _, _ in SEGMENTS],
        out_shape=[jax.ShapeDtypeStruct((t, w), dt) for _, w, dt, _ in SEGMENTS],
        scratch_shapes=[pltpu.VMEM((HG_HEADS, HG_DV, HG_DK), F32)],
        compiler_params=_params(("arbitrary",)),
        name="inproj_hgrn",
    )(h, norm_g, w_ext, _q_features(), lb_raw, out_norm)


def _gelu_tanh(x):
    return 0.5 * x * (1.0 + jnp.tanh(0.7978845608028654 * (x + 0.044715 * (x * x * x))))


def _compress_kernel(kv_ref, pos_ref, w1_ref, w2_ref, o_ref):
    nbp = o_ref.shape[1]
    tokens = [kv_ref[0, pl.ds(l, nbp, stride=CMP_STRIDE), :] for l in range(CMP_STRIDE)]
    chunk = lambda first: jnp.concatenate(
        [(tokens[l] + pos_ref[first + l:first + l + 1, :]).astype(BF16) for l in range(CMP_STRIDE)], axis=1)
    slab = lambda first: w1_ref[first:first + CMP_STRIDE].reshape(CMP_STRIDE * KV_W, NSA_KV_GROUPS * CMP_HID)
    ha = _nn(chunk(0), slab(0))
    hb = _nn(chunk(CMP_STRIDE), slab(CMP_STRIDE))
    act = _gelu_tanh(ha + pltpu.roll(hb, nbp - 1, axis=0)).astype(BF16)
    out = jnp.zeros(o_ref.shape[1:], F32)
    for g in range(NSA_KV_GROUPS):
        out = out + _nn(act[:, g * CMP_HID:(g + 1) * CMP_HID], w2_ref[g])
    o_ref[0] = out


def _compress_weights(pos, w1, w2, reps):
    layers = w1.shape[0]
    w1l = w1.astype(BF16).reshape(layers, CMP_LEN, NSA_HD, CMP_HID)
    z1 = jnp.zeros_like(w1l)
    w1p = jnp.concatenate([jnp.concatenate([w1l, z1], axis=3), jnp.concatenate([z1, w1l], axis=3)], axis=2)
    pos2 = jnp.concatenate([pos] * NSA_KV_GROUPS, axis=2)
    w2 = w2.astype(BF16)
    zero = jnp.zeros_like(w2)
    w2p = jnp.stack([jnp.concatenate([w2 if r == g else zero for r in reps], axis=2)
                     for g in range(NSA_KV_GROUPS)], axis=1)
    return pos2, w1p, w2p


def _compress(kv, layer, pos2, w1p, w2p):
    b, s, _ = kv.shape
    nbp = s // CMP_STRIDE
    width = w2p.shape[-1]
    return pl.pallas_call(
        _compress_kernel,
        grid=(b,),
        in_specs=[pl.BlockSpec((1, s, KV_W), lambda i: (i, 0, 0))]
        + [_resident_layer(a.shape, layer) for a in (pos2, w1p, w2p)],
        out_specs=pl.BlockSpec((1, nbp, width), lambda i: (i, 0, 0)),
        out_shape=jax.ShapeDtypeStruct((b, nbp, width), F32),
        compiler_params=_params(("parallel",)),
        name="compress",
    )(kv, pos2, w1p, w2p)


def _pair_blocks(g, jj):
    lower = slice(0, LANES) if g == 0 else slice(LANES, 2 * LANES)
    upper = slice(LANES, 2 * LANES) if g == 0 else slice(0, LANES)
    return lower, upper


def _cmp_kernel(q_ref, kc_ref, vc2_ref, mt_ref, ocmp_ref, sel_ref, picks_ref, *, tq, ns, n_sel):
    nbp = kc_ref.shape[1]
    q0 = pl.program_id(1) * tq
    pos = q0 + lax.broadcasted_iota(jnp.int32, (tq, nbp), 0)
    blk_end = lax.broadcasted_iota(jnp.int32, (tq, nbp), 1) * CMP_STRIDE + (CMP_LEN - 1)
    valid = blk_end <= pos
    row_ok = (q0 + lax.broadcasted_iota(jnp.int32, (tq, 1), 0)) >= CMP_LEN - 1
    kc = kc_ref[0].astype(BF16)
    vc2 = vc2_ref[0].astype(BF16)
    lane_c = lax.broadcasted_iota(jnp.int32, (nbp, LANES), 1)
    c_idx = lax.broadcasted_iota(jnp.int32, (nbp, LANES), 0)
    per = SEL_BLOCK // CMP_STRIDE
    feat_a = (c_idx // per - q0 // SEL_BLOCK).astype(F32)
    feat_b = ((c_idx % per) * CMP_STRIDE + (CMP_LEN - 1)).astype(F32)
    lane_lo = lax.broadcasted_iota(jnp.int32, (tq, LANES), 1) < NSA_HD
    blk_f = lax.broadcasted_iota(jnp.int32, (ns, tq), 0).astype(F32)
    blk = lax.broadcasted_iota(jnp.int32, (ns, tq), 0)
    pos_t = q0 + lax.broadcasted_iota(jnp.int32, (ns, tq), 1)
    cur = pos_t // SEL_BLOCK
    forced = (blk == 0) | (blk == cur) | (blk == cur - 1)
    causal = blk * SEL_BLOCK <= pos_t
    sel_t, picks = [], []
    taken = -jnp.inf
    for g in range(NSA_KV_GROUPS):
        f0 = NSA_HD * (1 - g)
        feat = jnp.where(lane_c == f0, feat_a, jnp.where(lane_c == f0 + 1, feat_b, 0.0))
        kc_g = jnp.where((lane_c // NSA_HD) == g, kc, feat.astype(BF16))
        imp = jnp.zeros((tq, nbp), F32)
        acc = []
        for hh in range(HEADS_PER_GROUP):
            h = g * HEADS_PER_GROUP + hh
            s = _nt(q_ref[0, :, h * LANES:(h + 1) * LANES], kc_g)
            s = jnp.where(valid, s, NEG)
            e = jnp.exp(s - jnp.max(s, axis=-1, keepdims=True))
            inv = jnp.where(row_ok, 1.0 / jnp.sum(e, axis=-1, keepdims=True), 0.0)
            p = e * inv
            imp = imp + p
            acc.append(_nn(p.astype(BF16), vc2))
        for jj in range(HEADS_PER_GROUP // 2):
            lower, upper = _pair_blocks(g, jj)
            blk_out = jnp.where(lane_lo, acc[2 * jj][:, lower], acc[2 * jj + 1][:, upper])
            c0 = (g * HEADS_PER_GROUP + 2 * jj) * NSA_HD
            ocmp_ref[0, :, c0:c0 + LANES] = blk_out
        p_slc = sum(_nt(mt_ref[...], part) for part in _split3(imp))
        score = jnp.where(forced, taken, jnp.where(causal, p_slc, NEG))
        for _ in range(n_sel - 3):
            top = jnp.max(score, axis=0, keepdims=True)
            first = jnp.min(jnp.where(score == top, blk_f, float(ns)), axis=0, keepdims=True)
            score = jnp.where(blk_f == first, taken, score)
        picked = jnp.where(score == taken, 1.0, 0.0).astype(BF16)
        picks.append([_nt(jnp.ones((SUBLANES, ATT_TQ), BF16), picked[:, t * ATT_TQ:(t + 1) * ATT_TQ])
                      for t in range(tq // ATT_TQ)])
        pen = jnp.where(score == taken, 0.0, SEL_PENALTY)
        pen = jnp.where((blk >= 2) & (blk <= ns - 1), pltpu.roll(pen, 1, axis=0), 0.0)
        if ns < NSA_HD:
            pen = jnp.concatenate([pen, jnp.zeros((NSA_HD - ns, tq), F32)], axis=0)
        zero = jnp.zeros((NSA_HD, tq), F32)
        sel_t += [zero, pen] if g == 0 else [pen, zero]
    sel_ref[0] = jnp.concatenate(sel_t, axis=0).T.astype(BF16)
    for t in range(tq // ATT_TQ):
        picks_ref[0, t] = jnp.concatenate([group[t] for group in picks], axis=1)


def _importance_matrix(nbp, ns):
    per = SEL_BLOCK // CMP_STRIDE
    m = np.zeros((ns, nbp), np.float32)
    for n in range(ns):
        for c in range(per * n, per * (n + 1)):
            for cc in (c - 1, c):
                if 0 <= cc < nbp - 1:
                    m[n, cc] += 1.0
    return jnp.asarray(m, BF16)


def _cmp_attention(q, kc, vc2, b, s):
    nbp = s // CMP_STRIDE
    ns = s // SEL_BLOCK
    tq = min(CMP_TQ, s)
    assert ns <= NSA_HD
    assert min(SEL_TOPK, ns) >= 3 and tq % ATT_TQ == 0
    q3 = q.reshape(b, s, NSA_HEADS * LANES)
    kern = functools.partial(_cmp_kernel, tq=tq, ns=ns, n_sel=min(SEL_TOPK, ns))
    return pl.pallas_call(
        kern,
        grid=(b, s // tq),
        in_specs=[pl.BlockSpec((1, tq, NSA_HEADS * LANES), lambda i, j: (i, j, 0)),
                  pl.BlockSpec((1, nbp, KV_W), lambda i, j: (i, 0, 0)),
                  pl.BlockSpec((1, nbp, 2 * KV_W), lambda i, j: (i, 0, 0)),
                  _resident((ns, nbp))],
        out_specs=[pl.BlockSpec((1, tq, NSA_WIDTH), lambda i, j: (i, j, 0)),
                   pl.BlockSpec((1, tq, NSA_KV_GROUPS * LANES), lambda i, j: (i, j, 0)),
                   pl.BlockSpec((1, tq // ATT_TQ, SUBLANES, NSA_KV_GROUPS * ns), lambda i, j: (i, j, 0, 0))],
        out_shape=[jax.ShapeDtypeStruct((b, s, NSA_WIDTH), F32),
                   jax.ShapeDtypeStruct((b, s, NSA_KV_GROUPS * LANES), BF16),
                   jax.ShapeDtypeStruct((b, s // ATT_TQ, SUBLANES, NSA_KV_GROUPS * ns), F32)],
        compiler_params=_params(("parallel", "parallel")),
        name="cmp_topk",
    )(q3, kc, vc2, _importance_matrix(nbp, ns))


def _attn_kernel(tiles_ref, ntiles_ref, q_ref, ks_ref, vs_ref, kw_ref, vw_ref, penq_ref, gl_ref, gexp_ref,
                 ocmp_ref, o_ref, qa_scr, s_scr, p_scr, al_scr, m_scr, acc_scr, *, tq, tk):
    i = pl.program_id(1)
    rows_of = lambda hh: slice(hh * tq, (hh + 1) * tq)
    lane_lo = lax.broadcasted_iota(jnp.int32, (tq, LANES), 1) < NSA_HD
    rel = (lax.broadcasted_iota(jnp.int32, (tq, tk), 1)
           - lax.broadcasted_iota(jnp.int32, (tq, tk), 0)).astype(F32)

    for hh in range(HEADS_PER_GROUP):
        qa_scr[rows_of(hh)] = q_ref[0, :, hh * LANES:(hh + 1) * LANES] + penq_ref[0]

    def scores(k_ref, j, slot):
        s_scr[slot] = _nt(qa_scr[...], k_ref[0, pl.ds(pl.multiple_of(j * tk, tk), tk), :])

    def probs(br, j, mode, slot, first, exists=None):
        for hh in range(HEADS_PER_GROUP):
            r = rows_of(hh)
            s = s_scr[slot, r]
            if mode == "causal":
                s = jnp.where(rel <= 0.0, s, NEG)
            elif mode == "lower":
                s = jnp.where(rel + (j * tk - i * tq).astype(F32) > -float(WINDOW), s, NEG)
            if exists is not None:
                s = jnp.where(exists, s, NEG)
            m_cur = jnp.max(s, axis=-1, keepdims=True)
            if first:
                m_new = jnp.broadcast_to(m_cur, (tq, LANES))
            else:
                m_prev = m_scr[br, r]
                m_new = jnp.maximum(m_prev, m_cur)
                al_scr[slot, r] = jnp.exp(m_prev - m_new)
            m_scr[br, r] = m_new
            p_scr[slot, r] = jnp.exp(s - jnp.concatenate([m_new] * (tk // LANES), axis=1)).astype(BF16)

    def accumulate(br, v_ref, j, slot, first):
        pv = _nn(p_scr[slot], v_ref[0, pl.ds(pl.multiple_of(j * tk, tk), tk), :])
        for hh in range(HEADS_PER_GROUP):
            r = rows_of(hh)
            if first:
                acc_scr[br, r] = pv[r]
            else:
                acc_scr[br, r] = al_scr[slot, r] * acc_scr[br, r] + pv[r]

    step = (pl.program_id(0) * pl.num_programs(1) + i) * pl.num_programs(2) + pl.program_id(2)
    n_tiles = ntiles_ref[step]
    item = lambda br, j, mode, first, exists=None: (br, j, mode, first, exists)
    sel_item = lambda n: item(0, i, "causal", True) if isinstance(n, int) and n == 0 else item(
        0, tiles_ref[step * pl.num_programs(1) + n - 1], None, False)
    n_back = WINDOW // tk
    win_items = [item(1, i, "causal", True)] + [
        item(1, jnp.maximum(i - d, 0), "lower" if d == n_back else None, False, i >= d)
        for d in range(1, n_back + 1)]
    k_refs, v_refs = (ks_ref, kw_ref), (vs_ref, vw_ref)

    def run(items, slot0, done=(0, 0, 0), upto=None):
        n_items = len(items)
        slot = lambda n: (slot0 + n) % 2
        n_s, n_p, n_a = done

        def do_scores(n):
            br, j, _, _, _ = items[n]
            scores(k_refs[br], j, slot(n))

        def do_probs(n):
            br, j, mode, first, exists = items[n]
            probs(br, j, mode, slot(n), first, exists)

        def do_acc(n):
            br, j, _, first, _ = items[n]
            accumulate(br, v_refs[br], j, slot(n), first)

        for n in range(n_s, min(n_a + 2, n_items)):
            do_scores(n)
        n_s = max(n_s, min(n_a + 2, n_items))
        for n in range(n_p, min(n_a + 1, n_items)):
            do_probs(n)
        n_p = max(n_p, min(n_a + 1, n_items))
        for k in range(n_a, n_items if upto is None else upto):
            if n_s <= k + 2 < n_items:
                do_scores(k + 2)
                n_s = k + 3
            if n_p <= k + 1 < n_items:
                do_probs(k + 1)
                n_p = k + 2
            do_acc(k)

    for n_static in range(2):
        @pl.when(n_tiles == n_static)
        def _(n_static=n_static):
            run([sel_item(n) for n in range(n_static + 1)] + win_items, 0)

    @pl.when(n_tiles >= 2)
    def _():
        run([sel_item(n) for n in range(3)], 0, upto=1)

        def sel_step(n, parity):
            scores(ks_ref, sel_item(n + 2)[1], parity)
            probs(0, None, None, 1 - parity, False)
            accumulate(0, vs_ref, sel_item(n)[1], parity, False)

        def sel_two_steps(t, carry):
            sel_step(2 * t + 1, 1)
            sel_step(2 * t + 2, 0)
            return carry

        lax.fori_loop(0, (n_tiles - 2) // 2, sel_two_steps, 0)

        def drain(parity):
            tail = [sel_item(n_tiles - 1), sel_item(n_tiles)] + win_items
            run(tail, 1 - parity, done=(2, 1, 0))

        @pl.when(n_tiles % 2 == 0)
        def _():
            drain(0)

        @pl.when(n_tiles % 2 == 1)
        def _():
            sel_step(n_tiles - 2, 1)
            drain(1)

    gexp = sum(_nn(part, gexp_ref[0]) for part in _split2(_sigmoid(gl_ref[0])))
    n_pairs = HEADS_PER_GROUP // 2
    gate_blk = lambda br, jj: gexp[:, (br * n_pairs + jj) * LANES:(br * n_pairs + jj + 1) * LANES]
    for jj in range(n_pairs):
        cols = slice(jj * LANES, (jj + 1) * LANES)
        blk = gate_blk(0, jj) * ocmp_ref[0, :, cols]
        for br in range(2):
            even = acc_scr[br, rows_of(2 * jj)]
            odd = acc_scr[br, rows_of(2 * jj + 1)]
            low = even / pltpu.roll(even, NSA_HD, axis=1)
            up = pltpu.roll(odd, NSA_HD, axis=1) / odd
            blk = blk + gate_blk(br + 1, jj) * jnp.where(lane_lo, low, up)
        o_ref[0, :, cols] = blk.astype(o_ref.dtype)


def _gate_expansion():
    n_pairs = HEADS_PER_GROUP // 2
    r = np.zeros((NSA_KV_GROUPS, LANES, 3 * n_pairs * LANES), np.float32)
    for g in range(NSA_KV_GROUPS):
        for br in range(3):
            for jj in range(n_pairs):
                for odd in range(2):
                    src = br * NSA_HEADS + g * HEADS_PER_GROUP + 2 * jj + odd
                    dst = (br * n_pairs + jj) * LANES + odd * NSA_HD
                    r[g, src, dst:dst + NSA_HD] = 1.0
    return jnp.asarray(r, BF16)


def _picked_tiles(picks, b, s):
    ns, nq = s // SEL_BLOCK, s // ATT_TQ
    per_tile = ATT_TK // SEL_BLOCK
    per_block = picks[:, :, 0, :].reshape(b, nq, NSA_KV_GROUPS, ns // per_tile, per_tile)
    j = jnp.arange(ns // per_tile, dtype=jnp.int32)
    i = jnp.arange(nq, dtype=jnp.int32)[None, :, None, None]
    active = ((per_block.sum(-1) > 0) | (j == 0)) & (j < i)
    slot = jnp.cumsum(active, axis=-1) - 1
    hit = active[..., None, :] & (slot[..., None, :] == j[:, None])
    tiles = jnp.sum(jnp.where(hit, j, 0), axis=-1).astype(jnp.int32)
    return tiles.reshape(-1), active.sum(-1).astype(jnp.int32).reshape(-1)


def _attention(q, ks, vs, kw, vw, pen, picks, gl, ocmp, b, s):
    tq, tk = ATT_TQ, ATT_TK
    assert tq == tk and WINDOW % tk == 0 and WINDOW // tk <= 2 and tq % SEL_BLOCK == 0
    gw = HEADS_PER_GROUP * LANES
    ow = HEADS_PER_GROUP * NSA_HD
    r3 = lambda a: a.reshape(b, s, a.shape[-1])
    tile = lambda w: pl.BlockSpec((1, tq, w), lambda i, j, g, *_: (i, j, 0))
    gtile = lambda w: pl.BlockSpec((1, tq, w), lambda i, j, g, *_: (i, j, g))
    gfull = pl.BlockSpec((1, s, LANES), lambda i, j, g, *_: (i, 0, g))
    gexp = _gate_expansion()
    rows = HEADS_PER_GROUP * tq
    kern = functools.partial(_attn_kernel, tq=tq, tk=tk)
    tiles, n_tiles = _picked_tiles(picks, b, s)
    grid_spec = pltpu.PrefetchScalarGridSpec(
        num_scalar_prefetch=2,
        grid=(b, s // tq, NSA_KV_GROUPS),
        in_specs=[gtile(gw), gfull, gfull, gfull, gfull, gtile(LANES), tile(LANES),
                  pl.BlockSpec((1,) + gexp.shape[1:], lambda i, j, g, *_: (g, 0, 0)), gtile(ow)],
        out_specs=gtile(ow),
        scratch_shapes=[pltpu.VMEM((rows, LANES), BF16), pltpu.VMEM((2, rows, tk), F32),
                        pltpu.VMEM((2, rows, tk), BF16), pltpu.VMEM((2, rows, LANES), F32),
                        pltpu.VMEM((2, rows, LANES), F32), pltpu.VMEM((2, rows, LANES), F32)])
    return pl.pallas_call(
        kern,
        grid_spec=grid_spec,
        out_shape=jax.ShapeDtypeStruct((b, s, NSA_WIDTH), BF16),
        compiler_params=_params(("parallel", "parallel", "parallel")),
        name="sel_win_attention",
    )(tiles, n_tiles, r3(q), r3(ks), r3(vs), r3(kw), r3(vw), pen, r3(gl), gexp, ocmp)


def _hgrn_rows(y, raw, onorm, st_scr, layer):
    c = HG_CHUNK
    ex = jnp.exp(raw - jnp.max(raw, axis=0, keepdims=True))
    sm = ex / jnp.sum(ex, axis=0, keepdims=True)
    lb_all = jnp.zeros((1, raw.shape[1]), F32)
    for l in range(1, layer + 1):
        lb_all = lb_all + sm[l:l + 1, :]

    t_idx = lax.broadcasted_iota(jnp.int32, (c, HG_DK), 0)
    sub = lax.broadcasted_iota(jnp.int32, (SUBLANES, HG_DK), 0)
    ti = lax.broadcasted_iota(jnp.int32, (c, c), 0)
    si = lax.broadcasted_iota(jnp.int32, (c, c), 1)
    tril = jnp.where(si <= ti, 1.0, 0.0).astype(BF16)
    levels = (32, 16, 8, 4, 2, 1)
    hc = HG_HEADS * c
    tb = lax.broadcasted_iota(jnp.int32, (hc, hc), 0)
    sb = lax.broadcasted_iota(jnp.int32, (hc, hc), 1)
    same_head = (tb // c) == (sb // c)
    split_bit = tb ^ sb
    pair_mask = {m: jnp.where(same_head & (tb > sb) & (split_bit >= m) & (split_bit < 2 * m), 1.0, 0.0)
                 for m in levels}
    diagonal = tb == sb
    second_half = {m: (t_idx & m) != 0 for m in levels}
    sign = {m: jnp.where(second_half[m], 1.0, -1.0) for m in levels}

    def ref_rows(b, m):
        row = lambda r, n: jnp.broadcast_to(b[r:r + 1, :], (n, HG_DK))
        if m >= 4:
            return jnp.concatenate([row(s0 + m - 1, 2 * m) for s0 in range(0, c, 2 * m)], axis=0)
        return jnp.concatenate([jnp.where(sub < 4, row(s0 + 1, SUBLANES), row(s0 + 5, SUBLANES))
                                for s0 in range(0, c, SUBLANES)], axis=0)

    part = lambda which, rows, h: y[rows, which * HG_WIDTH + h * HG_DK:which * HG_WIDTH + (h + 1) * HG_DK]
    out = []
    for ci in range(y.shape[0] // c):
        rows = slice(ci * c, (ci + 1) * c)
        q, k, f, v, logf = [], [], [], [], []
        for h in range(HG_HEADS):
            lb = lb_all[:, h * HG_DK:(h + 1) * HG_DK]
            z = part(1, rows, h)
            hq = part(0, rows, h)
            q.append(hq * _sigmoid(hq))
            ez = jnp.exp(-jnp.abs(z))
            big = 1.0 / (1.0 + ez)
            small = ez * big
            f.append(jnp.maximum(lb + (1.0 - lb) * jnp.where(z >= 0.0, big, small), F_FLOOR))
            logf.append(jnp.log2(f[h]))
            k.append((1.0 - lb) * jnp.where(z >= 0.0, small, big))
            v.append(part(2, rows, h).astype(BF16))
        parts = jnp.concatenate([p for h in range(HG_HEADS) for p in _split3(logf[h])], axis=1)
        csum = _nn(tril, parts)
        bcum = [sum(csum[:, (3 * h + i) * HG_DK:(3 * h + i + 1) * HG_DK] for i in range(3))
                for h in range(HG_HEADS)]

        rowdot = jnp.concatenate([jnp.sum(q[h] * k[h], axis=-1, keepdims=True) for h in range(HG_HEADS)], axis=0)
        a = jnp.where(diagonal, rowdot, 0.0)
        for m in levels:
            r = []
            for h in range(HG_HEADS):
                if m == 1:
                    r.append(jnp.where(second_half[m], q[h] * f[h], k[h]))
                else:
                    w = jnp.exp2((bcum[h] - ref_rows(bcum[h], m)) * sign[m])
                    r.append(jnp.where(second_half[m], q[h], k[h]) * w)
            r = jnp.concatenate(r, axis=0).astype(BF16)
            a = a + _nt(r, r) * pair_mask[m]
        o_intra = _nn(a.astype(BF16), jnp.concatenate(v, axis=0))

        heads = []
        for h in range(HG_HEADS):
            st = st_scr[h]
            o = o_intra[h * c:(h + 1) * c] + _nt((q[h] * jnp.exp2(bcum[h])).astype(BF16), st.astype(BF16))
            b_last = bcum[h][c - 1:c, :]
            kd = (k[h] * jnp.exp2(b_last - bcum[h])).astype(BF16)
            st_scr[h] = jnp.exp2(b_last) * st + _tn(v[h], kd)

            o = o * lax.rsqrt(jnp.mean(o * o, axis=-1, keepdims=True) + EPS) * onorm
            gate = part(3, rows, h)
            heads.append((o * (gate * _sigmoid(gate))).astype(BF16))
        out.append(jnp.concatenate(heads, axis=1))
    return jnp.concatenate(out, axis=0)


def _mixer(h, l, b, s, mix_norm, w_ext, cmp_k, cmp_v, hgrn_lower_bound, hgrn_out_norm):
    q, ks, vs, kw, vw, kc_in, vc_in, gl, o_hg = _inproj(h, mix_norm, w_ext, hgrn_lower_bound, hgrn_out_norm, s, l)
    kc = _compress(kc_in.reshape(b, s, KV_W), l, *cmp_k)
    vc2 = _compress(vc_in.reshape(b, s, KV_W), l, *cmp_v)
    ocmp, sel, picks = _cmp_attention(q, kc, vc2, b, s)
    o_nsa = _attention(q, ks, vs, kw, vw, sel, picks, gl, ocmp, b, s)
    return o_nsa.reshape(b * s, NSA_WIDTH), o_hg


def kernel(x, ffn1_norm, ffn1_w_gu, ffn1_w_down, mix_norm, w_in, cmp_pos_k, cmp_pos_v, cmp_k_w1, cmp_k_w2, cmp_v_w1, cmp_v_w2, hgrn_lower_bound, hgrn_out_norm, w_out, ffn2_norm, ffn2_w_gu, ffn2_w_down, final_norm):
    b, s, d = x.shape
    depth = ffn1_norm.shape[0]
    ffn1_w = _ffn_weights(ffn1_w_gu, ffn1_w_down)
    ffn2_w = _ffn_weights(ffn2_w_gu, ffn2_w_down)
    w_out = w_out.astype(BF16)
    w_ext = _build_w_in(w_in)
    cmp_k = _compress_weights(cmp_pos_k, cmp_k_w1, cmp_k_w2, (0, 1))
    cmp_v = _compress_weights(cmp_pos_v, cmp_v_w1, cmp_v_w2, (0, 1, 1, 0))
    h = x.reshape(b * s, d)
    for l in range(depth):
        h = _ffn(h, l, ffn1_norm, *ffn1_w, final_norm, False)
        o_nsa, o_hg = _mixer(h, l, b, s, mix_norm, w_ext, cmp_k, cmp_v, hgrn_lower_bound, hgrn_out_norm)
        h = _ffn(h, l, ffn2_norm, *ffn2_w, final_norm, l == depth - 1, (o_nsa, o_hg, w_out))
    return h.reshape(b, s, d)
```

```python
import functools

import jax
import jax.numpy as jnp
import numpy as np
from jax import lax
from jax.experimental import pallas as pl
from jax.experimental.pallas import tpu as pltpu

F32 = jnp.float32
BF16 = jnp.bfloat16

D_MODEL = 1024
EPS = 1e-6
NEG = -1e30
F_FLOOR = 1e-30
NSA_HEADS = 8
NSA_KV_GROUPS = 2
HEADS_PER_GROUP = NSA_HEADS // NSA_KV_GROUPS
NSA_HD = 64
CMP_LEN = 32
CMP_STRIDE = 16
CMP_HID = 256
SEL_BLOCK = 64
SEL_TOPK = 16
WINDOW = 512
HG_HEADS = 4
HG_DK = 128
HG_DV = 128
HG_CHUNK = 64
D_FF = 2752
NSA_WIDTH = NSA_HEADS * NSA_HD
HG_WIDTH = HG_HEADS * HG_DV
KV_W = NSA_KV_GROUPS * NSA_HD

LANES = 128
SUBLANES = 8
FF_CHUNK = 256
D_FF_PAD = -(-D_FF // FF_CHUNK) * FF_CHUNK
ROW_TILE = 512
FFN_ROWS = 1024
ATT_TQ = 256
ATT_TK = 256
CMP_TQ = 1024
VMEM_LIMIT = 56 * 1024 * 1024

SEL_PENALTY = 2.0 ** 50
ALIBI_SLOPES = tuple(2.0 ** (-8.0 * (i + 1) / NSA_HEADS) for i in range(NSA_HEADS))

SEGMENTS = (
    ("q", NSA_HEADS * LANES, BF16, NSA_WIDTH),
    ("ks", NSA_KV_GROUPS * LANES, BF16, KV_W),
    ("vs", NSA_KV_GROUPS * LANES, BF16, KV_W),
    ("kw", NSA_KV_GROUPS * LANES, BF16, KV_W),
    ("vw", NSA_KV_GROUPS * LANES, BF16, KV_W),
    ("kc", KV_W, F32, KV_W),
    ("vc", KV_W, F32, KV_W),
    ("gl", LANES, F32, LANES),
    ("hg", HG_WIDTH, BF16, 4 * HG_WIDTH),
)
SEG_OFFSETS = tuple(int(v) for v in np.cumsum([0] + [s[3] for s in SEGMENTS]))


def _nn(a, b):
    return jnp.dot(a, b, preferred_element_type=F32)


def _nt(a, b):
    return lax.dot_general(a, b, (((1,), (1,)), ((), ())), preferred_element_type=F32)


def _tn(a, b):
    return lax.dot_general(a, b, (((0,), (0,)), ((), ())), preferred_element_type=F32)


def _split2(x):
    hi = x.astype(BF16)
    lo = (x - hi.astype(F32)).astype(BF16)
    return hi, lo


def _split3(x):
    hi = x.astype(BF16)
    r = x - hi.astype(F32)
    mid = r.astype(BF16)
    lo = (r - mid.astype(F32)).astype(BF16)
    return hi, mid, lo


def _sigmoid(x):
    return 1.0 / (1.0 + jnp.exp(-x))


def _rms(x, g):
    return x * lax.rsqrt(jnp.mean(x * x, axis=-1, keepdims=True) + EPS) * g


def _resident(shape):
    nd = len(shape)
    return pl.BlockSpec(shape, lambda *_: (0,) * nd, pipeline_mode=pl.Buffered(1))


def _resident_layer(stacked_shape, layer):
    nd = len(stacked_shape)
    return pl.BlockSpec((None,) + tuple(stacked_shape[1:]), lambda *_: (layer,) + (0,) * (nd - 1),
                        pipeline_mode=pl.Buffered(1))


def _params(sem):
    return pltpu.CompilerParams(dimension_semantics=sem, vmem_limit_bytes=VMEM_LIMIT)


def _ffn_kernel(*refs, final, proj):
    if proj:
        x_ref, a_ref, b_ref, wo_ref, g_ref, wg_ref, wu_ref, wd_ref, gf_ref, o_ref = refs
        x = x_ref[...] + _nn(a_ref[...], wo_ref[:NSA_WIDTH, :]) + _nn(b_ref[...], wo_ref[NSA_WIDTH:, :])
    else:
        x_ref, g_ref, wg_ref, wu_ref, wd_ref, gf_ref, o_ref = refs
        x = x_ref[...]
    xn = _rms(x, g_ref[...]).astype(BF16)
    acc = jnp.zeros(x.shape, F32)
    for k in range(D_FF_PAD // FF_CHUNK):
        sl = slice(k * FF_CHUNK, (k + 1) * FF_CHUNK)
        gate = _nn(xn, wg_ref[:, sl])
        up = _nn(xn, wu_ref[:, sl])
        h = (gate * _sigmoid(gate) * up).astype(BF16)
        acc = acc + _nn(h, wd_ref[sl, :])
    y = x + 0.5 * acc
    if final:
        y = _rms(y, gf_ref[...])
    o_ref[...] = y


def _ffn_weights(w_gu, w_down):
    pad = D_FF_PAD - D_FF
    wg = jnp.pad(w_gu[..., :D_FF], ((0, 0), (0, 0), (0, pad))).astype(BF16)
    wu = jnp.pad(w_gu[..., D_FF:], ((0, 0), (0, 0), (0, pad))).astype(BF16)
    wd = jnp.pad(w_down, ((0, 0), (0, pad), (0, 0))).astype(BF16)
    return wg, wu, wd


def _ffn(h, layer, norm_g, wg, wu, wd, final_g, final, proj=None):
    t = h.shape[0]
    row = lambda w: pl.BlockSpec((FFN_ROWS, w), lambda i: (i, 0))
    norm_g = norm_g[:, None, :]
    weights = [_resident_layer(a.shape, layer) for a in (norm_g, wg, wu, wd)] + [_resident((1, D_MODEL))]
    operands = [norm_g, wg, wu, wd, final_g.reshape(1, -1)]
    if proj is None:
        in_specs, args = [row(D_MODEL)] + weights, [h] + operands
    else:
        o_nsa, o_hg, w_out = proj
        in_specs = [row(D_MODEL), row(NSA_WIDTH), row(HG_WIDTH), _resident_layer(w_out.shape, layer)] + weights
        args = [h, o_nsa, o_hg, w_out] + operands
    return pl.pallas_call(
        functools.partial(_ffn_kernel, final=final, proj=proj is not None),
        grid=(t // FFN_ROWS,),
        in_specs=in_specs,
        out_specs=row(D_MODEL),
        out_shape=jax.ShapeDtypeStruct((t, D_MODEL), F32),
        compiler_params=_params(("parallel",)),
        name="ffn",
    )(*args)


def _inproj_kernel(x_ref, g_ref, w_ref, qfeat_ref, lbraw_ref, onorm_ref, *refs, seq, layer):
    *o_refs, st_scr = refs
    xn = _rms(x_ref[...], g_ref[...]).astype(BF16)
    rows = x_ref.shape[0]

    @pl.when(pl.program_id(0) % (seq // rows) == 0)
    def _():
        st_scr[...] = jnp.zeros(st_scr.shape, F32)

    kpos = (pl.program_id(0) % (seq // rows)) * rows + lax.broadcasted_iota(jnp.int32, (rows, LANES), 0)
    lane = lax.broadcasted_iota(jnp.int32, (rows, LANES), 1)
    blk = kpos // SEL_BLOCK
    digit = lambda f0: jnp.where(lane == f0, blk.astype(F32),
                                 jnp.where(lane == f0 + 1, (kpos % SEL_BLOCK).astype(F32), 0.0))
    halves = [NSA_HD * (1 - g) for g in range(NSA_KV_GROUPS)]
    kfeat = jnp.concatenate([digit(f0) for f0 in halves], axis=1)
    is_pen = lambda f0: (lane == f0 + 1 + blk) & (blk >= 1) & (blk <= seq // SEL_BLOCK - 2)
    kpen = jnp.concatenate([jnp.where(is_pen(f0), -SEL_PENALTY, 0.0) for f0 in halves], axis=1)
    lower = lane < NSA_HD
    swap = lambda a: pltpu.roll(a, NSA_HD, axis=1)
    small = [i for i, seg in enumerate(SEGMENTS) if seg[3] == LANES]
    assert small == list(range(small[0], small[-1] + 1))
    y_small = _nn(xn, w_ref[:, SEG_OFFSETS[small[0]]:SEG_OFFSETS[small[-1] + 1]])
    for i, (o_ref, (name, _, dtype, cols)) in enumerate(zip(o_refs, SEGMENTS, strict=True)):
        if i in small:
            y = y_small[:, (i - small[0]) * LANES:(i - small[0] + 1) * LANES]
        else:
            y = _nn(xn, w_ref[:, SEG_OFFSETS[i]:SEG_OFFSETS[i] + cols])
        if name == "q":
            blocks = []
            for h in range(NSA_HEADS):
                pair = y[:, (h // 2) * LANES:(h // 2 + 1) * LANES]
                g = h // HEADS_PER_GROUP
                data = pair if h % 2 == g else swap(pair)
                feat = qfeat_ref[:, h * LANES:(h + 1) * LANES]
                blocks.append(jnp.where(lower == (g == 0), data, feat))
            y = jnp.concatenate(blocks, axis=1)
        elif name in ("ks", "kw"):
            feat = kfeat + kpen if name == "ks" else kfeat
            y = jnp.concatenate([jnp.where(lower == (g == 0), y, feat[:, g * LANES:(g + 1) * LANES])
                                 for g in range(NSA_KV_GROUPS)], axis=1)
        elif name in ("vs", "vw"):
            y = jnp.concatenate([jnp.where(lower, y if g == 0 else swap(y), 1.0)
                                 for g in range(NSA_KV_GROUPS)], axis=1)
        elif name == "hg":
            y = _hgrn_rows(y, lbraw_ref[...], onorm_ref[...], st_scr, layer)
        o_ref[...] = y.astype(dtype)


def _q_features():
    feat = np.zeros((1, NSA_HEADS * LANES), np.float32)
    for h in range(NSA_HEADS):
        f0 = h * LANES + NSA_HD * (1 - h // HEADS_PER_GROUP)
        feat[0, f0] = SEL_BLOCK * ALIBI_SLOPES[h]
        feat[0, f0 + 1] = ALIBI_SLOPES[h]
    return jnp.asarray(feat)


def _build_w_in(w_in):
    sizes = (NSA_WIDTH, KV_W, KV_W, KV_W, KV_W, KV_W, KV_W, NSA_HEADS * 3,
             HG_WIDTH, HG_WIDTH, HG_WIDTH, HG_WIDTH)
    splits = [int(v) for v in np.cumsum(sizes)[:-1]]
    w_in = w_in.astype(BF16)
    wq, wkc, wvc, wks, wvs, wkw, wvw, wgl, whq, whf, whi, whg = jnp.split(w_in, splits, axis=-1)
    lead = w_in.shape[:-1]
    gl = jnp.swapaxes(wgl.reshape(lead + (NSA_HEADS, 3)), -1, -2).reshape(lead + (3 * NSA_HEADS,))
    gl = jnp.pad(gl, [(0, 0)] * len(lead) + [(0, LANES - 3 * NSA_HEADS)])
    cols = [wq * NSA_HD ** -0.5, wks, wvs, wkw, wvw, wkc, wvc, gl, whq, whf, whi, whg]
    return jnp.concatenate(cols, axis=-1)


def _inproj(h, norm_g, w_ext, lb_raw, out_norm, seq, layer):
    t = h.shape[0]
    assert seq % ROW_TILE == 0 and ROW_TILE % HG_CHUNK == 0
    row = lambda w: pl.BlockSpec((ROW_TILE, w), lambda i: (i, 0))
    norm_g, out_norm = norm_g[:, None, :], out_norm[:, None, :]
    return pl.pallas_call(
        functools.partial(_inproj_kernel, seq=seq, layer=layer),
        grid=(t // ROW_TILE,),
        in_specs=[row(D_MODEL), _resident_layer(norm_g.shape, layer), _resident_layer(w_ext.shape, layer),
                  _resident((1, NSA_HEADS * LANES)), _resident(lb_raw.shape),
                  _resident_layer(out_norm.shape, layer)],
        out_specs=[row(w) for _, w, _, _ in SEGMENTS],
        out_shape=[jax.ShapeDtypeStruct((t, w), dt) for _, w, dt, _ in SEGMENTS],
        scratch_shapes=[pltpu.VMEM((HG_HEADS, HG_DV, HG_DK), F32)],
        compiler_params=_params(("arbitrary",)),
        name="inproj_hgrn",
    )(h, norm_g, w_ext, _q_features(), lb_raw, out_norm)


def _gelu_tanh(x):
    return 0.5 * x * (1.0 + jnp.tanh(0.7978845608028654 * (x + 0.044715 * (x * x * x))))


def _compress_kernel(kv_ref, pos_ref, w1_ref, w2_ref, o_ref):
    nbp = o_ref.shape[1]
    tokens = [kv_ref[0, pl.ds(l, nbp, stride=CMP_STRIDE), :] for l in range(CMP_STRIDE)]
    chunk = lambda first: jnp.concatenate(
        [(tokens[l] + pos_ref[first + l:first + l + 1, :]).astype(BF16) for l in range(CMP_STRIDE)], axis=1)
    slab = lambda first: w1_ref[first:first + CMP_STRIDE].reshape(CMP_STRIDE * KV_W, NSA_KV_GROUPS * CMP_HID)
    ha = _nn(chunk(0), slab(0))
    hb = _nn(chunk(CMP_STRIDE), slab(CMP_STRIDE))
    act = _gelu_tanh(ha + pltpu.roll(hb, nbp - 1, axis=0)).astype(BF16)
    out = jnp.zeros(o_ref.shape[1:], F32)
    for g in range(NSA_KV_GROUPS):
        out = out + _nn(act[:, g * CMP_HID:(g + 1) * CMP_HID], w2_ref[g])
    o_ref[0] = out


def _compress_weights(pos, w1, w2, reps):
    layers = w1.shape[0]
    w1l = w1.astype(BF16).reshape(layers, CMP_LEN, NSA_HD, CMP_HID)
    z1 = jnp.zeros_like(w1l)
    w1p = jnp.concatenate([jnp.concatenate([w1l, z1], axis=3), jnp.concatenate([z1, w1l], axis=3)], axis=2)
    pos2 = jnp.concatenate([pos] * NSA_KV_GROUPS, axis=2)
    w2 = w2.astype(BF16)
    zero = jnp.zeros_like(w2)
    w2p = jnp.stack([jnp.concatenate([w2 if r == g else zero for r in reps], axis=2)
                     for g in range(NSA_KV_GROUPS)], axis=1)
    return pos2, w1p, w2p


def _compress(kv, layer, pos2, w1p, w2p):
    b, s, _ = kv.shape
    nbp = s // CMP_STRIDE
    width = w2p.shape[-1]
    return pl.pallas_call(
        _compress_kernel,
        grid=(b,),
        in_specs=[pl.BlockSpec((1, s, KV_W), lambda i: (i, 0, 0))]
        + [_resident_layer(a.shape, layer) for a in (pos2, w1p, w2p)],
        out_specs=pl.BlockSpec((1, nbp, width), lambda i: (i, 0, 0)),
        out_shape=jax.ShapeDtypeStruct((b, nbp, width), F32),
        compiler_params=_params(("parallel",)),
        name="compress",
    )(kv, pos2, w1p, w2p)


def _pair_blocks(g, jj):
    lower = slice(0, LANES) if g == 0 else slice(LANES, 2 * LANES)
    upper = slice(LANES, 2 * LANES) if g == 0 else slice(0, LANES)
    return lower, upper


def _cmp_kernel(*refs, tq, ns, n_sel):
    total = refs[1].shape[1]
    per_tile = tq // CMP_STRIDE
    cols_for = lambda j, minimum: minimum(total, ((j + 1) * per_tile + LANES - 1) // LANES * LANES)
    few_for = lambda j: (j + 1) * tq <= n_sel * SEL_BLOCK
    j = pl.program_id(1)
    for nbp, few in sorted({(cols_for(t, min), few_for(t)) for t in range(total // per_tile)}):
        @pl.when((cols_for(j, jnp.minimum) == nbp) & (few_for(j) == few))
        def _(nbp=nbp, few=few):
            _cmp_tile(*refs, tq=tq, ns=ns, n_sel=n_sel, nbp=nbp, rounds=0 if few else n_sel - 3)


def _cmp_tile(q_ref, kc_ref, vc2_ref, mt_ref, ocmp_ref, sel_ref, picks_ref, *, tq, ns, n_sel, nbp, rounds):
    q0 = pl.program_id(1) * tq
    pos = q0 + lax.broadcasted_iota(jnp.int32, (tq, nbp), 0)
    blk_end = lax.broadcasted_iota(jnp.int32, (tq, nbp), 1) * CMP_STRIDE + (CMP_LEN - 1)
    valid = blk_end <= pos
    row_ok = (q0 + lax.broadcasted_iota(jnp.int32, (tq, 1), 0)) >= CMP_LEN - 1
    kc = kc_ref[0, :nbp, :].astype(BF16)
    vc2 = vc2_ref[0, :nbp, :].astype(BF16)
    lane_c = lax.broadcasted_iota(jnp.int32, (nbp, LANES), 1)
    c_idx = lax.broadcasted_iota(jnp.int32, (nbp, LANES), 0)
    per = SEL_BLOCK // CMP_STRIDE
    feat_a = (c_idx // per - q0 // SEL_BLOCK).astype(F32)
    feat_b = ((c_idx % per) * CMP_STRIDE + (CMP_LEN - 1)).astype(F32)
    lane_lo = lax.broadcasted_iota(jnp.int32, (tq, LANES), 1) < NSA_HD
    blk_f = lax.broadcasted_iota(jnp.int32, (ns, tq), 0).astype(F32)
    blk = lax.broadcasted_iota(jnp.int32, (ns, tq), 0)
    pos_t = q0 + lax.broadcasted_iota(jnp.int32, (ns, tq), 1)
    cur = pos_t // SEL_BLOCK
    forced = (blk == 0) | (blk == cur) | (blk == cur - 1)
    causal = blk * SEL_BLOCK <= pos_t
    sel_t, picks = [], []
    taken = -jnp.inf
    for g in range(NSA_KV_GROUPS):
        f0 = NSA_HD * (1 - g)
        feat = jnp.where(lane_c == f0, feat_a, jnp.where(lane_c == f0 + 1, feat_b, 0.0))
        kc_g = jnp.where((lane_c // NSA_HD) == g, kc, feat.astype(BF16))
        imp = jnp.zeros((tq, nbp), F32)
        acc = []
        for hh in range(HEADS_PER_GROUP):
            h = g * HEADS_PER_GROUP + hh
            s = _nt(q_ref[0, :, h * LANES:(h + 1) * LANES], kc_g)
            s = jnp.where(valid, s, NEG)
            e = jnp.exp(s - jnp.max(s, axis=-1, keepdims=True))
            inv = jnp.where(row_ok, 1.0 / jnp.sum(e, axis=-1, keepdims=True), 0.0)
            p = e * inv
            imp = imp + p
            acc.append(_nn(p.astype(BF16), vc2))
        for jj in range(HEADS_PER_GROUP // 2):
            lower, upper = _pair_blocks(g, jj)
            blk_out = jnp.where(lane_lo, acc[2 * jj][:, lower], acc[2 * jj + 1][:, upper])
            c0 = (g * HEADS_PER_GROUP + 2 * jj) * NSA_HD
            ocmp_ref[0, :, c0:c0 + LANES] = blk_out
        p_slc = sum(_nt(mt_ref[:, :nbp], part) for part in _split3(imp))
        if rounds == 0:
            score = jnp.where(forced | causal, taken, NEG)
        else:
            score = jnp.where(forced, taken, jnp.where(causal, p_slc, NEG))
        for _ in range(rounds):
            top = jnp.max(score, axis=0, keepdims=True)
            first = jnp.min(jnp.where(score == top, blk_f, float(ns)), axis=0, keepdims=True)
            score = jnp.where(blk_f == first, taken, score)
        picked = jnp.where(score == taken, 1.0, 0.0).astype(BF16)
        picks.append([_nt(jnp.ones((SUBLANES, ATT_TQ), BF16), picked[:, t * ATT_TQ:(t + 1) * ATT_TQ])
                      for t in range(tq // ATT_TQ)])
        pen = jnp.where(score == taken, 0.0, SEL_PENALTY)
        pen = jnp.where((blk >= 2) & (blk <= ns - 1), pltpu.roll(pen, 1, axis=0), 0.0)
        if ns < NSA_HD:
            pen = jnp.concatenate([pen, jnp.zeros((NSA_HD - ns, tq), F32)], axis=0)
        zero = jnp.zeros((NSA_HD, tq), F32)
        sel_t += [zero, pen] if g == 0 else [pen, zero]
    sel_ref[0] = jnp.concatenate(sel_t, axis=0).T.astype(BF16)
    for t in range(tq // ATT_TQ):
        picks_ref[0, t] = jnp.concatenate([group[t] for group in picks], axis=1)


def _importance_matrix(nbp, ns):
    per = SEL_BLOCK // CMP_STRIDE
    m = np.zeros((ns, nbp), np.float32)
    for n in range(ns):
        for c in range(per * n, per * (n + 1)):
            for cc in (c - 1, c):
                if 0 <= cc < nbp - 1:
                    m[n, cc] += 1.0
    return jnp.asarray(m, BF16)


def _cmp_attention(q, kc, vc2, b, s):
    nbp = s // CMP_STRIDE
    ns = s // SEL_BLOCK
    tq = min(CMP_TQ, s)
    assert ns <= NSA_HD
    assert min(SEL_TOPK, ns) >= 3 and tq % ATT_TQ == 0
    q3 = q.reshape(b, s, NSA_HEADS * LANES)
    kern = functools.partial(_cmp_kernel, tq=tq, ns=ns, n_sel=min(SEL_TOPK, ns))
    return pl.pallas_call(
        kern,
        grid=(b, s // tq),
        in_specs=[pl.BlockSpec((1, tq, NSA_HEADS * LANES), lambda i, j: (i, j, 0)),
                  pl.BlockSpec((1, nbp, KV_W), lambda i, j: (i, 0, 0)),
                  pl.BlockSpec((1, nbp, 2 * KV_W), lambda i, j: (i, 0, 0)),
                  _resident((ns, nbp))],
        out_specs=[pl.BlockSpec((1, tq, NSA_WIDTH), lambda i, j: (i, j, 0)),
                   pl.BlockSpec((1, tq, NSA_KV_GROUPS * LANES), lambda i, j: (i, j, 0)),
                   pl.BlockSpec((1, tq // ATT_TQ, SUBLANES, NSA_KV_GROUPS * ns), lambda i, j: (i, j, 0, 0))],
        out_shape=[jax.ShapeDtypeStruct((b, s, NSA_WIDTH), F32),
                   jax.ShapeDtypeStruct((b, s, NSA_KV_GROUPS * LANES), BF16),
                   jax.ShapeDtypeStruct((b, s // ATT_TQ, SUBLANES, NSA_KV_GROUPS * ns), F32)],
        compiler_params=_params(("parallel", "parallel")),
        name="cmp_topk",
    )(q3, kc, vc2, _importance_matrix(nbp, ns))


def _attn_kernel(tiles_ref, ntiles_ref, q_ref, ks_ref, vs_ref, kw_ref, vw_ref, penq_ref, gl_ref, gexp_ref,
                 ocmp_ref, o_ref, qa_scr, s_scr, p_scr, al_scr, m_scr, acc_scr, *, tq, tk):
    i = pl.program_id(1)
    rows_of = lambda hh: slice(hh * tq, (hh + 1) * tq)
    lane_lo = lax.broadcasted_iota(jnp.int32, (tq, LANES), 1) < NSA_HD
    rel = (lax.broadcasted_iota(jnp.int32, (tq, tk), 1)
           - lax.broadcasted_iota(jnp.int32, (tq, tk), 0)).astype(F32)

    for hh in range(HEADS_PER_GROUP):
        qa_scr[rows_of(hh)] = q_ref[0, :, hh * LANES:(hh + 1) * LANES] + penq_ref[0]

    def scores(k_ref, j, slot):
        s_scr[slot] = _nt(qa_scr[...], k_ref[0, pl.ds(pl.multiple_of(j * tk, tk), tk), :])

    def probs(br, j, mode, slot, first, exists=None):
        for hh in range(HEADS_PER_GROUP):
            r = rows_of(hh)
            s = s_scr[slot, r]
            if mode == "causal":
                s = jnp.where(rel <= 0.0, s, NEG)
            elif mode == "lower":
                s = jnp.where(rel + (j * tk - i * tq).astype(F32) > -float(WINDOW), s, NEG)
            if exists is not None:
                s = jnp.where(exists, s, NEG)
            m_cur = jnp.max(s, axis=-1, keepdims=True)
            if first:
                m_new = jnp.broadcast_to(m_cur, (tq, LANES))
            else:
                m_prev = m_scr[br, r]
                m_new = jnp.maximum(m_prev, m_cur)
                al_scr[slot, r] = jnp.exp(m_prev - m_new)
            m_scr[br, r] = m_new
            p_scr[slot, r] = jnp.exp(s - jnp.concatenate([m_new] * (tk // LANES), axis=1)).astype(BF16)

    def accumulate(br, v_ref, j, slot, first):
        pv = _nn(p_scr[slot], v_ref[0, pl.ds(pl.multiple_of(j * tk, tk), tk), :])
        for hh in range(HEADS_PER_GROUP):
            r = rows_of(hh)
            if first:
                acc_scr[br, r] = pv[r]
            else:
                acc_scr[br, r] = al_scr[slot, r] * acc_scr[br, r] + pv[r]

    step = (pl.program_id(0) * pl.num_programs(1) + i) * pl.num_programs(2) + pl.program_id(2)
    n_tiles = ntiles_ref[step]
    item = lambda br, j, mode, first, exists=None: (br, j, mode, first, exists)
    sel_item = lambda n: item(0, i, "causal", True) if isinstance(n, int) and n == 0 else item(
        0, tiles_ref[step * pl.num_programs(1) + n - 1], None, False)
    n_back = WINDOW // tk
    win_items = [item(1, i, "causal", True)] + [
        item(1, jnp.maximum(i - d, 0), "lower" if d == n_back else None, False, i >= d)
        for d in range(1, n_back + 1)]
    k_refs, v_refs = (ks_ref, kw_ref), (vs_ref, vw_ref)

    def run(items, slot0, done=(0, 0, 0), upto=None):
        n_items = len(items)
        slot = lambda n: (slot0 + n) % 2
        n_s, n_p, n_a = done

        def do_scores(n):
            br, j, _, _, _ = items[n]
            scores(k_refs[br], j, slot(n))

        def do_probs(n):
            br, j, mode, first, exists = items[n]
            probs(br, j, mode, slot(n), first, exists)

        def do_acc(n):
            br, j, _, first, _ = items[n]
            accumulate(br, v_refs[br], j, slot(n), first)

        for n in range(n_s, min(n_a + 2, n_items)):
            do_scores(n)
        n_s = max(n_s, min(n_a + 2, n_items))
        for n in range(n_p, min(n_a + 1, n_items)):
            do_probs(n)
        n_p = max(n_p, min(n_a + 1, n_items))
        for k in range(n_a, n_items if upto is None else upto):
            if n_s <= k + 2 < n_items:
                do_scores(k + 2)
                n_s = k + 3
            if n_p <= k + 1 < n_items:
                do_probs(k + 1)
                n_p = k + 2
            do_acc(k)

    for n_static in range(2):
        @pl.when(n_tiles == n_static)
        def _(n_static=n_static):
            run([sel_item(n) for n in range(n_static + 1)] + win_items, 0)

    @pl.when(n_tiles >= 2)
    def _():
        run([sel_item(n) for n in range(3)], 0, upto=1)

        def sel_step(n, parity):
            scores(ks_ref, sel_item(n + 2)[1], parity)
            probs(0, None, None, 1 - parity, False)
            accumulate(0, vs_ref, sel_item(n)[1], parity, False)

        def sel_two_steps(t, carry):
            sel_step(2 * t + 1, 1)
            sel_step(2 * t + 2, 0)
            return carry

        lax.fori_loop(0, (n_tiles - 2) // 2, sel_two_steps, 0)

        def drain(parity):
            tail = [sel_item(n_tiles - 1), sel_item(n_tiles)] + win_items
            run(tail, 1 - parity, done=(2, 1, 0))

        @pl.when(n_tiles % 2 == 0)
        def _():
            drain(0)

        @pl.when(n_tiles % 2 == 1)
        def _():
            sel_step(n_tiles - 2, 1)
            drain(1)

    gexp = sum(_nn(part, gexp_ref[0]) for part in _split2(_sigmoid(gl_ref[0])))
    n_pairs = HEADS_PER_GROUP // 2
    gate_blk = lambda br, jj: gexp[:, (br * n_pairs + jj) * LANES:(br * n_pairs + jj + 1) * LANES]
    for jj in range(n_pairs):
        cols = slice(jj * LANES, (jj + 1) * LANES)
        blk = gate_blk(0, jj) * ocmp_ref[0, :, cols]
        for br in range(2):
            even = acc_scr[br, rows_of(2 * jj)]
            odd = acc_scr[br, rows_of(2 * jj + 1)]
            low = even / pltpu.roll(even, NSA_HD, axis=1)
            up = pltpu.roll(odd, NSA_HD, axis=1) / odd
            blk = blk + gate_blk(br + 1, jj) * jnp.where(lane_lo, low, up)
        o_ref[0, :, cols] = blk.astype(o_ref.dtype)


def _gate_expansion():
    n_pairs = HEADS_PER_GROUP // 2
    r = np.zeros((NSA_KV_GROUPS, LANES, 3 * n_pairs * LANES), np.float32)
    for g in range(NSA_KV_GROUPS):
        for br in range(3):
            for jj in range(n_pairs):
                for odd in range(2):
                    src = br * NSA_HEADS + g * HEADS_PER_GROUP + 2 * jj + odd
                    dst = (br * n_pairs + jj) * LANES + odd * NSA_HD
                    r[g, src, dst:dst + NSA_HD] = 1.0
    return jnp.asarray(r, BF16)


def _picked_tiles(picks, b, s):
    ns, nq = s // SEL_BLOCK, s // ATT_TQ
    per_tile = ATT_TK // SEL_BLOCK
    per_block = picks[:, :, 0, :].reshape(b, nq, NSA_KV_GROUPS, ns // per_tile, per_tile)
    j = jnp.arange(ns // per_tile, dtype=jnp.int32)
    i = jnp.arange(nq, dtype=jnp.int32)[None, :, None, None]
    active = ((per_block.sum(-1) > 0) | (j == 0)) & (j < i)
    slot = jnp.cumsum(active, axis=-1) - 1
    hit = active[..., None, :] & (slot[..., None, :] == j[:, None])
    tiles = jnp.sum(jnp.where(hit, j, 0), axis=-1).astype(jnp.int32)
    return tiles.reshape(-1), active.sum(-1).astype(jnp.int32).reshape(-1)


def _attention(q, ks, vs, kw, vw, pen, picks, gl, ocmp, b, s):
    tq, tk = ATT_TQ, ATT_TK
    assert tq == tk and WINDOW % tk == 0 and WINDOW // tk <= 2 and tq % SEL_BLOCK == 0
    gw = HEADS_PER_GROUP * LANES
    ow = HEADS_PER_GROUP * NSA_HD
    r3 = lambda a: a.reshape(b, s, a.shape[-1])
    tile = lambda w: pl.BlockSpec((1, tq, w), lambda i, j, g, *_: (i, j, 0))
    gtile = lambda w: pl.BlockSpec((1, tq, w), lambda i, j, g, *_: (i, j, g))
    gfull = pl.BlockSpec((1, s, LANES), lambda i, j, g, *_: (i, 0, g))
    gexp = _gate_expansion()
    rows = HEADS_PER_GROUP * tq
    kern = functools.partial(_attn_kernel, tq=tq, tk=tk)
    tiles, n_tiles = _picked_tiles(picks, b, s)
    grid_spec = pltpu.PrefetchScalarGridSpec(
        num_scalar_prefetch=2,
        grid=(b, s // tq, NSA_KV_GROUPS),
        in_specs=[gtile(gw), gfull, gfull, gfull, gfull, gtile(LANES), tile(LANES),
                  pl.BlockSpec((1,) + gexp.shape[1:], lambda i, j, g, *_: (g, 0, 0)), gtile(ow)],
        out_specs=gtile(ow),
        scratch_shapes=[pltpu.VMEM((rows, LANES), BF16), pltpu.VMEM((2, rows, tk), F32),
                        pltpu.VMEM((2, rows, tk), BF16), pltpu.VMEM((2, rows, LANES), F32),
                        pltpu.VMEM((2, rows, LANES), F32), pltpu.VMEM((2, rows, LANES), F32)])
    return pl.pallas_call(
        kern,
        grid_spec=grid_spec,
        out_shape=jax.ShapeDtypeStruct((b, s, NSA_WIDTH), BF16),
        compiler_params=_params(("parallel", "parallel", "parallel")),
        name="sel_win_attention",
    )(tiles, n_tiles, r3(q), r3(ks), r3(vs), r3(kw), r3(vw), pen, r3(gl), gexp, ocmp)


def _hgrn_rows(y, raw, onorm, st_scr, layer):
    c = HG_CHUNK
    ex = jnp.exp(raw - jnp.max(raw, axis=0, keepdims=True))
    sm = ex / jnp.sum(ex, axis=0, keepdims=True)
    lb_all = jnp.zeros((1, raw.shape[1]), F32)
    for l in range(1, layer + 1):
        lb_all = lb_all + sm[l:l + 1, :]

    t_idx = lax.broadcasted_iota(jnp.int32, (c, HG_DK), 0)
    sub = lax.broadcasted_iota(jnp.int32, (SUBLANES, HG_DK), 0)
    ti = lax.broadcasted_iota(jnp.int32, (c, c), 0)
    si = lax.broadcasted_iota(jnp.int32, (c, c), 1)
    tril = jnp.where(si <= ti, 1.0, 0.0).astype(BF16)
    levels = (32, 16, 8, 4, 2, 1)
    hc = HG_HEADS * c
    tb = lax.broadcasted_iota(jnp.int32, (hc, hc), 0)
    sb = lax.broadcasted_iota(jnp.int32, (hc, hc), 1)
    same_head = (tb // c) == (sb // c)
    split_bit = tb ^ sb
    pair_mask = {m: jnp.where(same_head & (tb > sb) & (split_bit >= m) & (split_bit < 2 * m), 1.0, 0.0)
                 for m in levels}
    diagonal = tb == sb
    second_half = {m: (t_idx & m) != 0 for m in levels}
    sign = {m: jnp.where(second_half[m], 1.0, -1.0) for m in levels}

    def ref_rows(b, m):
        row = lambda r, n: jnp.broadcast_to(b[r:r + 1, :], (n, HG_DK))
        if m >= 4:
            return jnp.concatenate([row(s0 + m - 1, 2 * m) for s0 in range(0, c, 2 * m)], axis=0)
        return jnp.concatenate([jnp.where(sub < 4, row(s0 + 1, SUBLANES), row(s0 + 5, SUBLANES))
                                for s0 in range(0, c, SUBLANES)], axis=0)

    part = lambda which, rows, h: y[rows, which * HG_WIDTH + h * HG_DK:which * HG_WIDTH + (h + 1) * HG_DK]
    out = []
    for ci in range(y.shape[0] // c):
        rows = slice(ci * c, (ci + 1) * c)
        q, k, f, v, logf = [], [], [], [], []
        for h in range(HG_HEADS):
            lb = lb_all[:, h * HG_DK:(h + 1) * HG_DK]
            z = part(1, rows, h)
            hq = part(0, rows, h)
            q.append(hq * _sigmoid(hq))
            ez = jnp.exp(-jnp.abs(z))
            big = 1.0 / (1.0 + ez)
            small = ez * big
            f.append(jnp.maximum(lb + (1.0 - lb) * jnp.where(z >= 0.0, big, small), F_FLOOR))
            logf.append(jnp.log2(f[h]))
            k.append((1.0 - lb) * jnp.where(z >= 0.0, small, big))
            v.append(part(2, rows, h).astype(BF16))
        parts = jnp.concatenate([p for h in range(HG_HEADS) for p in _split3(logf[h])], axis=1)
        csum = _nn(tril, parts)
        bcum = [sum(csum[:, (3 * h + i) * HG_DK:(3 * h + i + 1) * HG_DK] for i in range(3))
                for h in range(HG_HEADS)]

        rowdot = jnp.concatenate([jnp.sum(q[h] * k[h], axis=-1, keepdims=True) for h in range(HG_HEADS)], axis=0)
        a = jnp.where(diagonal, rowdot, 0.0)
        for m in levels:
            r = []
            for h in range(HG_HEADS):
                if m == 1:
                    r.append(jnp.where(second_half[m], q[h] * f[h], k[h]))
                else:
                    w = jnp.exp2((bcum[h] - ref_rows(bcum[h], m)) * sign[m])
                    r.append(jnp.where(second_half[m], q[h], k[h]) * w)
            r = jnp.concatenate(r, axis=0).astype(BF16)
            a = a + _nt(r, r) * pair_mask[m]
        o_intra = _nn(a.astype(BF16), jnp.concatenate(v, axis=0))

        heads = []
        for h in range(HG_HEADS):
            st = st_scr[h]
            o = o_intra[h * c:(h + 1) * c] + _nt((q[h] * jnp.exp2(bcum[h])).astype(BF16), st.astype(BF16))
            b_last = bcum[h][c - 1:c, :]
            kd = (k[h] * jnp.exp2(b_last - bcum[h])).astype(BF16)
            st_scr[h] = jnp.exp2(b_last) * st + _tn(v[h], kd)

            o = o * lax.rsqrt(jnp.mean(o * o, axis=-1, keepdims=True) + EPS) * onorm
            gate = part(3, rows, h)
            heads.append((o * (gate * _sigmoid(gate))).astype(BF16))
        out.append(jnp.concatenate(heads, axis=1))
    return jnp.concatenate(out, axis=0)


def _mixer(h, l, b, s, mix_norm, w_ext, cmp_k, cmp_v, hgrn_lower_bound, hgrn_out_norm):
    q, ks, vs, kw, vw, kc_in, vc_in, gl, o_hg = _inproj(h, mix_norm, w_ext, hgrn_lower_bound, hgrn_out_norm, s, l)
    kc = _compress(kc_in.reshape(b, s, KV_W), l, *cmp_k)
    vc2 = _compress(vc_in.reshape(b, s, KV_W), l, *cmp_v)
    ocmp, sel, picks = _cmp_attention(q, kc, vc2, b, s)
    o_nsa = _attention(q, ks, vs, kw, vw, sel, picks, gl, ocmp, b, s)
    return o_nsa.reshape(b * s, NSA_WIDTH), o_hg


def kernel(x, ffn1_norm, ffn1_w_gu, ffn1_w_down, mix_norm, w_in, cmp_pos_k, cmp_pos_v, cmp_k_w1, cmp_k_w2, cmp_v_w1, cmp_v_w2, hgrn_lower_bound, hgrn_out_norm, w_out, ffn2_norm, ffn2_w_gu, ffn2_w_down, final_norm):
    b, s, d = x.shape
    depth = ffn1_norm.shape[0]
    ffn1_w = _ffn_weights(ffn1_w_gu, ffn1_w_down)
    ffn2_w = _ffn_weights(ffn2_w_gu, ffn2_w_down)
    w_out = w_out.astype(BF16)
    w_ext = _build_w_in(w_in)
    cmp_k = _compress_weights(cmp_pos_k, cmp_k_w1, cmp_k_w2, (0, 1))
    cmp_v = _compress_weights(cmp_pos_v, cmp_v_w1, cmp_v_w2, (0, 1, 1, 0))
    h = x.reshape(b * s, d)
    for l in range(depth):
        h = _ffn(h, l, ffn1_norm, *ffn1_w, final_norm, False)
        o_nsa, o_hg = _mixer(h, l, b, s, mix_norm, w_ext, cmp_k, cmp_v, hgrn_lower_bound, hgrn_out_norm)
        h = _ffn(h, l, ffn2_norm, *ffn2_w, final_norm, l == depth - 1, (o_nsa, o_hg, w_out))
    return h.reshape(b, s, d)
```

```python
import functools

import jax
import jax.numpy as jnp
import numpy as np
from jax import lax
from jax.experimental import pallas as pl
from jax.experimental.pallas import tpu as pltpu

F32 = jnp.float32
BF16 = jnp.bfloat16

D_MODEL = 1024
EPS = 1e-6
NEG = -1e30
F_FLOOR = 1e-30
NSA_HEADS = 8
NSA_KV_GROUPS = 2
HEADS_PER_GROUP = NSA_HEADS // NSA_KV_GROUPS
NSA_HD = 64
CMP_LEN = 32
CMP_STRIDE = 16
CMP_HID = 256
SEL_BLOCK = 64
SEL_TOPK = 16
WINDOW = 512
HG_HEADS = 4
HG_DK = 128
HG_DV = 128
HG_CHUNK = 64
D_FF = 2752
NSA_WIDTH = NSA_HEADS * NSA_HD
HG_WIDTH = HG_HEADS * HG_DV
KV_W = NSA_KV_GROUPS * NSA_HD

LANES = 128
SUBLANES = 8
FF_CHUNK = 256
D_FF_PAD = -(-D_FF // FF_CHUNK) * FF_CHUNK
ROW_TILE = 512
FFN_ROWS = 1024
ATT_TQ = 256
ATT_TK = 256
CMP_TQ = 1024
VMEM_LIMIT = 56 * 1024 * 1024

SEL_PENALTY = 2.0 ** 50
ALIBI_SLOPES = tuple(2.0 ** (-8.0 * (i + 1) / NSA_HEADS) for i in range(NSA_HEADS))

SEGMENTS = (
    ("q", NSA_HEADS * LANES, BF16, NSA_WIDTH),
    ("ks", NSA_KV_GROUPS * LANES, BF16, KV_W),
    ("vs", NSA_KV_GROUPS * LANES, BF16, KV_W),
    ("kw", NSA_KV_GROUPS * LANES, BF16, KV_W),
    ("vw", NSA_KV_GROUPS * LANES, BF16, KV_W),
    ("kc", KV_W, F32, KV_W),
    ("vc", KV_W, F32, KV_W),
    ("gl", LANES, F32, LANES),
    ("hg", HG_WIDTH, BF16, 4 * HG_WIDTH),
)
SEG_OFFSETS = tuple(int(v) for v in np.cumsum([0] + [s[3] for s in SEGMENTS]))


def _nn(a, b):
    return jnp.dot(a, b, preferred_element_type=F32)


def _nt(a, b):
    return lax.dot_general(a, b, (((1,), (1,)), ((), ())), preferred_element_type=F32)


def _tn(a, b):
    return lax.dot_general(a, b, (((0,), (0,)), ((), ())), preferred_element_type=F32)


def _split2(x):
    hi = x.astype(BF16)
    lo = (x - hi.astype(F32)).astype(BF16)
    return hi, lo


def _split3(x):
    hi = x.astype(BF16)
    r = x - hi.astype(F32)
    mid = r.astype(BF16)
    lo = (r - mid.astype(F32)).astype(BF16)
    return hi, mid, lo


def _sigmoid(x):
    return 1.0 / (1.0 + jnp.exp(-x))


def _rms(x, g):
    return x * lax.rsqrt(jnp.mean(x * x, axis=-1, keepdims=True) + EPS) * g


def _resident(shape):
    nd = len(shape)
    return pl.BlockSpec(shape, lambda *_: (0,) * nd, pipeline_mode=pl.Buffered(1))


def _resident_layer(stacked_shape, layer):
    nd = len(stacked_shape)
    return pl.BlockSpec((None,) + tuple(stacked_shape[1:]), lambda *_: (layer,) + (0,) * (nd - 1),
                        pipeline_mode=pl.Buffered(1))


def _params(sem):
    return pltpu.CompilerParams(dimension_semantics=sem, vmem_limit_bytes=VMEM_LIMIT)


def _ffn_kernel(*refs, final, proj):
    if proj:
        x_ref, a_ref, b_ref, wo_ref, g_ref, wg_ref, wu_ref, wd_ref, gf_ref, o_ref = refs
        x = x_ref[...] + _nn(a_ref[...], wo_ref[:NSA_WIDTH, :]) + _nn(b_ref[...], wo_ref[NSA_WIDTH:, :])
    else:
        x_ref, g_ref, wg_ref, wu_ref, wd_ref, gf_ref, o_ref = refs
        x = x_ref[...]
    xn = _rms(x, g_ref[...]).astype(BF16)
    acc = jnp.zeros(x.shape, F32)
    for k in range(D_FF_PAD // FF_CHUNK):
        sl = slice(k * FF_CHUNK, (k + 1) * FF_CHUNK)
        gate = _nn(xn, wg_ref[:, sl])
        up = _nn(xn, wu_ref[:, sl])
        h = (gate * _sigmoid(gate) * up).astype(BF16)
        acc = acc + _nn(h, wd_ref[sl, :])
    y = x + 0.5 * acc
    if final:
        y = _rms(y, gf_ref[...])
    o_ref[...] = y


def _ffn_weights(w_gu, w_down):
    pad = D_FF_PAD - D_FF
    wg = jnp.pad(w_gu[..., :D_FF], ((0, 0), (0, 0), (0, pad))).astype(BF16)
    wu = jnp.pad(w_gu[..., D_FF:], ((0, 0), (0, 0), (0, pad))).astype(BF16)
    wd = jnp.pad(w_down, ((0, 0), (0, pad), (0, 0))).astype(BF16)
    return wg, wu, wd


def _ffn(h, layer, norm_g, wg, wu, wd, final_g, final, proj=None):
    t = h.shape[0]
    row = lambda w: pl.BlockSpec((FFN_ROWS, w), lambda i: (i, 0))
    norm_g = norm_g[:, None, :]
    weights = [_resident_layer(a.shape, layer) for a in (norm_g, wg, wu, wd)] + [_resident((1, D_MODEL))]
    operands = [norm_g, wg, wu, wd, final_g.reshape(1, -1)]
    if proj is None:
        in_specs, args = [row(D_MODEL)] + weights, [h] + operands
    else:
        o_nsa, o_hg, w_out = proj
        in_specs = [row(D_MODEL), row(NSA_WIDTH), row(HG_WIDTH), _resident_layer(w_out.shape, layer)] + weights
        args = [h, o_nsa, o_hg, w_out] + operands
    return pl.pallas_call(
        functools.partial(_ffn_kernel, final=final, proj=proj is not None),
        grid=(t // FFN_ROWS,),
        in_specs=in_specs,
        out_specs=row(D_MODEL),
        out_shape=jax.ShapeDtypeStruct((t, D_MODEL), F32),
        compiler_params=_params(("parallel",)),
        name="ffn",
    )(*args)


def _inproj_kernel(x_ref, g_ref, w_ref, qfeat_ref, lbraw_ref, onorm_ref, *refs, seq, layer):
    *o_refs, st_scr = refs
    xn = _rms(x_ref[...], g_ref[...]).astype(BF16)
    rows = x_ref.shape[0]

    @pl.when(pl.program_id(0) % (seq // rows) == 0)
    def _():
        st_scr[...] = jnp.zeros(st_scr.shape, F32)

    kpos = (pl.program_id(0) % (seq // rows)) * rows + lax.broadcasted_iota(jnp.int32, (rows, LANES), 0)
    lane = lax.broadcasted_iota(jnp.int32, (rows, LANES), 1)
    blk = kpos // SEL_BLOCK
    digit = lambda f0: jnp.where(lane == f0, blk.astype(F32),
                                 jnp.where(lane == f0 + 1, (kpos % SEL_BLOCK).astype(F32), 0.0))
    halves = [NSA_HD * (1 - g) for g in range(NSA_KV_GROUPS)]
    kfeat = jnp.concatenate([digit(f0) for f0 in halves], axis=1)
    is_pen = lambda f0: (lane == f0 + 1 + blk) & (blk >= 1) & (blk <= seq // SEL_BLOCK - 2)
    kpen = jnp.concatenate([jnp.where(is_pen(f0), -SEL_PENALTY, 0.0) for f0 in halves], axis=1)
    lower = lane < NSA_HD
    swap = lambda a: pltpu.roll(a, NSA_HD, axis=1)
    small = [i for i, seg in enumerate(SEGMENTS) if seg[3] == LANES]
    assert small == list(range(small[0], small[-1] + 1))
    y_small = _nn(xn, w_ref[:, SEG_OFFSETS[small[0]]:SEG_OFFSETS[small[-1] + 1]])
    for i, (o_ref, (name, _, dtype, cols)) in enumerate(zip(o_refs, SEGMENTS, strict=True)):
        if i in small:
            y = y_small[:, (i - small[0]) * LANES:(i - small[0] + 1) * LANES]
        else:
            y = _nn(xn, w_ref[:, SEG_OFFSETS[i]:SEG_OFFSETS[i] + cols])
        if name == "q":
            blocks = []
            for h in range(NSA_HEADS):
                pair = y[:, (h // 2) * LANES:(h // 2 + 1) * LANES]
                g = h // HEADS_PER_GROUP
                data = pair if h % 2 == g else swap(pair)
                feat = qfeat_ref[:, h * LANES:(h + 1) * LANES]
                blocks.append(jnp.where(lower == (g == 0), data, feat))
            y = jnp.concatenate(blocks, axis=1)
        elif name in ("ks", "kw"):
            feat = kfeat + kpen if name == "ks" else kfeat
            y = jnp.concatenate([jnp.where(lower == (g == 0), y, feat[:, g * LANES:(g + 1) * LANES])
                                 for g in range(NSA_KV_GROUPS)], axis=1)
        elif name in ("vs", "vw"):
            y = jnp.concatenate([jnp.where(lower, y if g == 0 else swap(y), 1.0)
                                 for g in range(NSA_KV_GROUPS)], axis=1)
        elif name == "hg":
            y = _hgrn_rows(y, lbraw_ref[...], onorm_ref[...], st_scr, layer)
        o_ref[...] = y.astype(dtype)


def _q_features():
    feat = np.zeros((1, NSA_HEADS * LANES), np.float32)
    for h in range(NSA_HEADS):
        f0 = h * LANES + NSA_HD * (1 - h // HEADS_PER_GROUP)
        feat[0, f0] = SEL_BLOCK * ALIBI_SLOPES[h]
        feat[0, f0 + 1] = ALIBI_SLOPES[h]
    return jnp.asarray(feat)


def _build_w_in(w_in):
    sizes = (NSA_WIDTH, KV_W, KV_W, KV_W, KV_W, KV_W, KV_W, NSA_HEADS * 3,
             HG_WIDTH, HG_WIDTH, HG_WIDTH, HG_WIDTH)
    splits = [int(v) for v in np.cumsum(sizes)[:-1]]
    w_in = w_in.astype(BF16)
    wq, wkc, wvc, wks, wvs, wkw, wvw, wgl, whq, whf, whi, whg = jnp.split(w_in, splits, axis=-1)
    lead = w_in.shape[:-1]
    gl = jnp.swapaxes(wgl.reshape(lead + (NSA_HEADS, 3)), -1, -2).reshape(lead + (3 * NSA_HEADS,))
    gl = jnp.pad(gl, [(0, 0)] * len(lead) + [(0, LANES - 3 * NSA_HEADS)])
    cols = [wq * NSA_HD ** -0.5, wks, wvs, wkw, wvw, wkc, wvc, gl, whq, whf, whi, whg]
    return jnp.concatenate(cols, axis=-1)


def _inproj(h, norm_g, w_ext, lb_raw, out_norm, seq, layer):
    t = h.shape[0]
    assert seq % ROW_TILE == 0 and ROW_TILE % HG_CHUNK == 0
    row = lambda w: pl.BlockSpec((ROW_TILE, w), lambda i: (i, 0))
    norm_g, out_norm = norm_g[:, None, :], out_norm[:, None, :]
    return pl.pallas_call(
        functools.partial(_inproj_kernel, seq=seq, layer=layer),
        grid=(t // ROW_TILE,),
        in_specs=[row(D_MODEL), _resident_layer(norm_g.shape, layer), _resident_layer(w_ext.shape, layer),
                  _resident((1, NSA_HEADS * LANES)), _resident(lb_raw.shape),
                  _resident_layer(out_norm.shape, layer)],
        out_specs=[row(w) for _, w, _, _ in SEGMENTS],
        out_shape=[jax.ShapeDtypeStruct((t, w), dt) for _, w, dt, _ in SEGMENTS],
        scratch_shapes=[pltpu.VMEM((HG_HEADS, HG_DV, HG_DK), F32)],
        compiler_params=_params(("arbitrary",)),
        name="inproj_hgrn",
    )(h, norm_g, w_ext, _q_features(), lb_raw, out_norm)


def _gelu_tanh(x):
    return 0.5 * x * (1.0 + jnp.tanh(0.7978845608028654 * (x + 0.044715 * (x * x * x))))


def _compress_kernel(kv_ref, pos_ref, w1_ref, w2_ref, o_ref):
    nbp = o_ref.shape[1]
    tokens = [kv_ref[0, pl.ds(l, nbp, stride=CMP_STRIDE), :] for l in range(CMP_STRIDE)]
    chunk = lambda first: jnp.concatenate(
        [(tokens[l] + pos_ref[first + l:first + l + 1, :]).astype(BF16) for l in range(CMP_STRIDE)], axis=1)
    slab = lambda first: w1_ref[first:first + CMP_STRIDE].reshape(CMP_STRIDE * KV_W, NSA_KV_GROUPS * CMP_HID)
    ha = _nn(chunk(0), slab(0))
    hb = _nn(chunk(CMP_STRIDE), slab(CMP_STRIDE))
    act = _gelu_tanh(ha + pltpu.roll(hb, nbp - 1, axis=0)).astype(BF16)
    out = jnp.zeros(o_ref.shape[1:], F32)
    for g in range(NSA_KV_GROUPS):
        out = out + _nn(act[:, g * CMP_HID:(g + 1) * CMP_HID], w2_ref[g])
    o_ref[0] = out


def _compress_weights(pos, w1, w2, reps):
    layers = w1.shape[0]
    w1l = w1.astype(BF16).reshape(layers, CMP_LEN, NSA_HD, CMP_HID)
    z1 = jnp.zeros_like(w1l)
    w1p = jnp.concatenate([jnp.concatenate([w1l, z1], axis=3), jnp.concatenate([z1, w1l], axis=3)], axis=2)
    pos2 = jnp.concatenate([pos] * NSA_KV_GROUPS, axis=2)
    w2 = w2.astype(BF16)
    zero = jnp.zeros_like(w2)
    w2p = jnp.stack([jnp.concatenate([w2 if r == g else zero for r in reps], axis=2)
                     for g in range(NSA_KV_GROUPS)], axis=1)
    return pos2, w1p, w2p


def _compress(kv, layer, pos2, w1p, w2p):
    b, s, _ = kv.shape
    nbp = s // CMP_STRIDE
    width = w2p.shape[-1]
    return pl.pallas_call(
        _compress_kernel,
        grid=(b,),
        in_specs=[pl.BlockSpec((1, s, KV_W), lambda i: (i, 0, 0))]
        + [_resident_layer(a.shape, layer) for a in (pos2, w1p, w2p)],
        out_specs=pl.BlockSpec((1, nbp, width), lambda i: (i, 0, 0)),
        out_shape=jax.ShapeDtypeStruct((b, nbp, width), F32),
        compiler_params=_params(("parallel",)),
        name="compress",
    )(kv, pos2, w1p, w2p)


def _pair_blocks(g, jj):
    lower = slice(0, LANES) if g == 0 else slice(LANES, 2 * LANES)
    upper = slice(LANES, 2 * LANES) if g == 0 else slice(0, LANES)
    return lower, upper


def _cmp_kernel(*refs, tq, ns, n_sel):
    total = refs[1].shape[1]
    per_tile = tq // CMP_STRIDE
    for t in range(total // per_tile):
        nbp = min(total, ((t + 1) * per_tile + LANES - 1) // LANES * LANES)
        sel_rows = min(ns, (t + 1) * tq // SEL_BLOCK)
        rounds = 0 if sel_rows <= n_sel else n_sel - 3

        @pl.when(pl.program_id(1) == t)
        def _(nbp=nbp, sel_rows=sel_rows, rounds=rounds):
            _cmp_tile(*refs, tq=tq, ns=ns, nbp=nbp, sel_rows=sel_rows, rounds=rounds)


def _cmp_tile(q_ref, kc_ref, vc2_ref, mt_ref, ocmp_ref, sel_ref, picks_ref, *, tq, ns, nbp, sel_rows, rounds):
    q0 = pl.program_id(1) * tq
    pos = q0 + lax.broadcasted_iota(jnp.int32, (tq, nbp), 0)
    blk_end = lax.broadcasted_iota(jnp.int32, (tq, nbp), 1) * CMP_STRIDE + (CMP_LEN - 1)
    valid = blk_end <= pos
    row_ok = (q0 + lax.broadcasted_iota(jnp.int32, (tq, 1), 0)) >= CMP_LEN - 1
    kc = kc_ref[0, :nbp, :].astype(BF16)
    vc2 = vc2_ref[0, :nbp, :].astype(BF16)
    lane_c = lax.broadcasted_iota(jnp.int32, (nbp, LANES), 1)
    c_idx = lax.broadcasted_iota(jnp.int32, (nbp, LANES), 0)
    per = SEL_BLOCK // CMP_STRIDE
    feat_a = (c_idx // per - q0 // SEL_BLOCK).astype(F32)
    feat_b = ((c_idx % per) * CMP_STRIDE + (CMP_LEN - 1)).astype(F32)
    lane_lo = lax.broadcasted_iota(jnp.int32, (tq, LANES), 1) < NSA_HD
    blk = lax.broadcasted_iota(jnp.int32, (ns, tq), 0)
    blk_r = lax.broadcasted_iota(jnp.int32, (sel_rows, tq), 0)
    blk_f = blk_r.astype(F32)
    pos_t = q0 + lax.broadcasted_iota(jnp.int32, (sel_rows, tq), 1)
    cur = pos_t // SEL_BLOCK
    forced = (blk_r == 0) | (blk_r == cur) | (blk_r == cur - 1)
    causal = blk_r * SEL_BLOCK <= pos_t
    sel_t, picks = [], []
    taken = -jnp.inf
    for g in range(NSA_KV_GROUPS):
        f0 = NSA_HD * (1 - g)
        feat = jnp.where(lane_c == f0, feat_a, jnp.where(lane_c == f0 + 1, feat_b, 0.0))
        kc_g = jnp.where((lane_c // NSA_HD) == g, kc, feat.astype(BF16))
        imp = jnp.zeros((tq, nbp), F32)
        acc = []
        for hh in range(HEADS_PER_GROUP):
            h = g * HEADS_PER_GROUP + hh
            s = _nt(q_ref[0, :, h * LANES:(h + 1) * LANES], kc_g)
            s = jnp.where(valid, s, NEG)
            e = jnp.exp(s - jnp.max(s, axis=-1, keepdims=True))
            inv = jnp.where(row_ok, 1.0 / jnp.sum(e, axis=-1, keepdims=True), 0.0)
            p = e * inv
            imp = imp + p
            acc.append(_nn(p.astype(BF16), vc2))
        for jj in range(HEADS_PER_GROUP // 2):
            lower, upper = _pair_blocks(g, jj)
            blk_out = jnp.where(lane_lo, acc[2 * jj][:, lower], acc[2 * jj + 1][:, upper])
            c0 = (g * HEADS_PER_GROUP + 2 * jj) * NSA_HD
            ocmp_ref[0, :, c0:c0 + LANES] = blk_out
        p_slc = sum(_nt(mt_ref[:sel_rows, :nbp], part) for part in _split3(imp))
        if rounds == 0:
            score = jnp.where(forced | causal, taken, NEG)
        else:
            score = jnp.where(forced, taken, jnp.where(causal, p_slc, NEG))
        for _ in range(rounds):
            top = jnp.max(score, axis=0, keepdims=True)
            first = jnp.min(jnp.where(score == top, blk_f, float(ns)), axis=0, keepdims=True)
            score = jnp.where(blk_f == first, taken, score)
        if sel_rows < ns:
            score = jnp.concatenate([score, jnp.full((ns - sel_rows, tq), NEG, F32)], axis=0)
        picked = jnp.where(score == taken, 1.0, 0.0).astype(BF16)
        picks.append([_nt(jnp.ones((SUBLANES, ATT_TQ), BF16), picked[:, t * ATT_TQ:(t + 1) * ATT_TQ])
                      for t in range(tq // ATT_TQ)])
        pen = jnp.where(score == taken, 0.0, SEL_PENALTY)
        pen = jnp.where((blk >= 2) & (blk <= ns - 1), pltpu.roll(pen, 1, axis=0), 0.0)
        if ns < NSA_HD:
            pen = jnp.concatenate([pen, jnp.zeros((NSA_HD - ns, tq), F32)], axis=0)
        zero = jnp.zeros((NSA_HD, tq), F32)
        sel_t += [zero, pen] if g == 0 else [pen, zero]
    sel_ref[0] = jnp.concatenate(sel_t, axis=0).T.astype(BF16)
    for t in range(tq // ATT_TQ):
        picks_ref[0, t] = jnp.concatenate([group[t] for group in picks], axis=1)


def _importance_matrix(nbp, ns):
    per = SEL_BLOCK // CMP_STRIDE
    m = np.zeros((ns, nbp), np.float32)
    for n in range(ns):
        for c in range(per * n, per * (n + 1)):
            for cc in (c - 1, c):
                if 0 <= cc < nbp - 1:
                    m[n, cc] += 1.0
    return jnp.asarray(m, BF16)


def _cmp_attention(q, kc, vc2, b, s):
    nbp = s // CMP_STRIDE
    ns = s // SEL_BLOCK
    tq = min(CMP_TQ, s)
    assert ns <= NSA_HD
    assert min(SEL_TOPK, ns) >= 3 and tq % ATT_TQ == 0 and tq % (SUBLANES * SEL_BLOCK) == 0
    q3 = q.reshape(b, s, NSA_HEADS * LANES)
    kern = functools.partial(_cmp_kernel, tq=tq, ns=ns, n_sel=min(SEL_TOPK, ns))
    return pl.pallas_call(
        kern,
        grid=(b, s // tq),
        in_specs=[pl.BlockSpec((1, tq, NSA_HEADS * LANES), lambda i, j: (i, j, 0)),
                  pl.BlockSpec((1, nbp, KV_W), lambda i, j: (i, 0, 0)),
                  pl.BlockSpec((1, nbp, 2 * KV_W), lambda i, j: (i, 0, 0)),
                  _resident((ns, nbp))],
        out_specs=[pl.BlockSpec((1, tq, NSA_WIDTH), lambda i, j: (i, j, 0)),
                   pl.BlockSpec((1, tq, NSA_KV_GROUPS * LANES), lambda i, j: (i, j, 0)),
                   pl.BlockSpec((1, tq // ATT_TQ, SUBLANES, NSA_KV_GROUPS * ns), lambda i, j: (i, j, 0, 0))],
        out_shape=[jax.ShapeDtypeStruct((b, s, NSA_WIDTH), F32),
                   jax.ShapeDtypeStruct((b, s, NSA_KV_GROUPS * LANES), BF16),
                   jax.ShapeDtypeStruct((b, s // ATT_TQ, SUBLANES, NSA_KV_GROUPS * ns), F32)],
        compiler_params=_params(("parallel", "parallel")),
        name="cmp_topk",
    )(q3, kc, vc2, _importance_matrix(nbp, ns))


def _attn_kernel(tiles_ref, ntiles_ref, q_ref, ks_ref, vs_ref, kw_ref, vw_ref, penq_ref, gl_ref, gexp_ref,
                 ocmp_ref, o_ref, qa_scr, s_scr, p_scr, al_scr, m_scr, acc_scr, *, tq, tk):
    i = pl.program_id(2)
    rows_of = lambda hh: slice(hh * tq, (hh + 1) * tq)
    lane_lo = lax.broadcasted_iota(jnp.int32, (tq, LANES), 1) < NSA_HD
    rel = (lax.broadcasted_iota(jnp.int32, (tq, tk), 1)
           - lax.broadcasted_iota(jnp.int32, (tq, tk), 0)).astype(F32)

    for hh in range(HEADS_PER_GROUP):
        qa_scr[rows_of(hh)] = q_ref[0, :, hh * LANES:(hh + 1) * LANES] + penq_ref[0]

    def scores(k_ref, j, slot):
        s_scr[slot] = _nt(qa_scr[...], k_ref[0, pl.ds(pl.multiple_of(j * tk, tk), tk), :])

    def probs(br, j, mode, slot, first, exists=None):
        for hh in range(HEADS_PER_GROUP):
            r = rows_of(hh)
            s = s_scr[slot, r]
            if mode == "causal":
                s = jnp.where(rel <= 0.0, s, NEG)
            elif mode == "lower":
                s = jnp.where(rel + (j * tk - i * tq).astype(F32) > -float(WINDOW), s, NEG)
            if exists is not None:
                s = jnp.where(exists, s, NEG)
            m_cur = jnp.max(s, axis=-1, keepdims=True)
            if first:
                m_new = jnp.broadcast_to(m_cur, (tq, LANES))
            else:
                m_prev = m_scr[br, r]
                m_new = jnp.maximum(m_prev, m_cur)
                al_scr[slot, r] = jnp.exp(m_prev - m_new)
            m_scr[br, r] = m_new
            p_scr[slot, r] = jnp.exp(s - jnp.concatenate([m_new] * (tk // LANES), axis=1)).astype(BF16)

    def accumulate(br, v_ref, j, slot, first):
        pv = _nn(p_scr[slot], v_ref[0, pl.ds(pl.multiple_of(j * tk, tk), tk), :])
        for hh in range(HEADS_PER_GROUP):
            r = rows_of(hh)
            if first:
                acc_scr[br, r] = pv[r]
            else:
                acc_scr[br, r] = al_scr[slot, r] * acc_scr[br, r] + pv[r]

    n_q = pl.num_programs(2)
    step = (pl.program_id(0) * n_q + i) * pl.num_programs(1) + pl.program_id(1)
    n_tiles = ntiles_ref[step]
    item = lambda br, j, mode, first, exists=None: (br, j, mode, first, exists)
    sel_item = lambda n: item(0, i, "causal", True) if isinstance(n, int) and n == 0 else item(
        0, tiles_ref[step * n_q + n - 1], None, False)
    n_back = WINDOW // tk
    win_items = [item(1, i, "causal", True)] + [
        item(1, jnp.maximum(i - d, 0), "lower" if d == n_back else None, False, i >= d)
        for d in range(1, n_back + 1)]
    k_refs, v_refs = (ks_ref, kw_ref), (vs_ref, vw_ref)

    def run(items, slot0, done=(0, 0, 0), upto=None):
        n_items = len(items)
        slot = lambda n: (slot0 + n) % 2
        n_s, n_p, n_a = done

        def do_scores(n):
            br, j, _, _, _ = items[n]
            scores(k_refs[br], j, slot(n))

        def do_probs(n):
            br, j, mode, first, exists = items[n]
            probs(br, j, mode, slot(n), first, exists)

        def do_acc(n):
            br, j, _, first, _ = items[n]
            accumulate(br, v_refs[br], j, slot(n), first)

        for n in range(n_s, min(n_a + 2, n_items)):
            do_scores(n)
        n_s = max(n_s, min(n_a + 2, n_items))
        for n in range(n_p, min(n_a + 1, n_items)):
            do_probs(n)
        n_p = max(n_p, min(n_a + 1, n_items))
        for k in range(n_a, n_items if upto is None else upto):
            if n_s <= k + 2 < n_items:
                do_scores(k + 2)
                n_s = k + 3
            if n_p <= k + 1 < n_items:
                do_probs(k + 1)
                n_p = k + 2
            do_acc(k)

    for n_static in range(2):
        @pl.when(n_tiles == n_static)
        def _(n_static=n_static):
            run([sel_item(n) for n in range(n_static + 1)] + win_items, 0)

    @pl.when(n_tiles >= 2)
    def _():
        run([sel_item(n) for n in range(3)], 0, upto=1)

        def sel_step(n, parity):
            scores(ks_ref, sel_item(n + 2)[1], parity)
            probs(0, None, None, 1 - parity, False)
            accumulate(0, vs_ref, sel_item(n)[1], parity, False)

        def sel_two_steps(t, carry):
            sel_step(2 * t + 1, 1)
            sel_step(2 * t + 2, 0)
            return carry

        lax.fori_loop(0, (n_tiles - 2) // 2, sel_two_steps, 0)

        def drain(parity):
            tail = [sel_item(n_tiles - 1), sel_item(n_tiles)] + win_items
            run(tail, 1 - parity, done=(2, 1, 0))

        @pl.when(n_tiles % 2 == 0)
        def _():
            drain(0)

        @pl.when(n_tiles % 2 == 1)
        def _():
            sel_step(n_tiles - 2, 1)
            drain(1)

    gexp = sum(_nn(part, gexp_ref[0]) for part in _split2(_sigmoid(gl_ref[0])))
    n_pairs = HEADS_PER_GROUP // 2
    gate_blk = lambda br, jj: gexp[:, (br * n_pairs + jj) * LANES:(br * n_pairs + jj + 1) * LANES]
    for jj in range(n_pairs):
        cols = slice(jj * LANES, (jj + 1) * LANES)
        blk = gate_blk(0, jj) * ocmp_ref[0, :, cols]
        for br in range(2):
            even = acc_scr[br, rows_of(2 * jj)]
            odd = acc_scr[br, rows_of(2 * jj + 1)]
            low = even / pltpu.roll(even, NSA_HD, axis=1)
            up = pltpu.roll(odd, NSA_HD, axis=1) / odd
            blk = blk + gate_blk(br + 1, jj) * jnp.where(lane_lo, low, up)
        o_ref[0, :, cols] = blk.astype(o_ref.dtype)


def _gate_expansion():
    n_pairs = HEADS_PER_GROUP // 2
    r = np.zeros((NSA_KV_GROUPS, LANES, 3 * n_pairs * LANES), np.float32)
    for g in range(NSA_KV_GROUPS):
        for br in range(3):
            for jj in range(n_pairs):
                for odd in range(2):
                    src = br * NSA_HEADS + g * HEADS_PER_GROUP + 2 * jj + odd
                    dst = (br * n_pairs + jj) * LANES + odd * NSA_HD
                    r[g, src, dst:dst + NSA_HD] = 1.0
    return jnp.asarray(r, BF16)


def _picked_tiles(picks, b, s):
    ns, nq = s // SEL_BLOCK, s // ATT_TQ
    per_tile = ATT_TK // SEL_BLOCK
    per_block = picks[:, :, 0, :].reshape(b, nq, NSA_KV_GROUPS, ns // per_tile, per_tile)
    j = jnp.arange(ns // per_tile, dtype=jnp.int32)
    i = jnp.arange(nq, dtype=jnp.int32)[None, :, None, None]
    active = ((per_block.sum(-1) > 0) | (j == 0)) & (j < i)
    slot = jnp.cumsum(active, axis=-1) - 1
    hit = active[..., None, :] & (slot[..., None, :] == j[:, None])
    tiles = jnp.sum(jnp.where(hit, j, 0), axis=-1).astype(jnp.int32)
    return tiles.reshape(-1), active.sum(-1).astype(jnp.int32).reshape(-1)


def _attention(q, ks, vs, kw, vw, pen, picks, gl, ocmp, b, s):
    tq, tk = ATT_TQ, ATT_TK
    assert tq == tk and WINDOW % tk == 0 and WINDOW // tk <= 2 and tq % SEL_BLOCK == 0
    gw = HEADS_PER_GROUP * LANES
    ow = HEADS_PER_GROUP * NSA_HD
    r3 = lambda a: a.reshape(b, s, a.shape[-1])
    tile = lambda w: pl.BlockSpec((1, tq, w), lambda i, g, j, *_: (i, j, 0))
    gtile = lambda w: pl.BlockSpec((1, tq, w), lambda i, g, j, *_: (i, j, g))
    gfull = pl.BlockSpec((1, s, LANES), lambda i, g, j, *_: (i, 0, g))
    gexp = _gate_expansion()
    rows = HEADS_PER_GROUP * tq
    kern = functools.partial(_attn_kernel, tq=tq, tk=tk)
    tiles, n_tiles = _picked_tiles(picks, b, s)
    grid_spec = pltpu.PrefetchScalarGridSpec(
        num_scalar_prefetch=2,
        grid=(b, NSA_KV_GROUPS, s // tq),
        in_specs=[gtile(gw), gfull, gfull, gfull, gfull, gtile(LANES), tile(LANES),
                  pl.BlockSpec((1,) + gexp.shape[1:], lambda i, g, j, *_: (g, 0, 0)), gtile(ow)],
        out_specs=gtile(ow),
        scratch_shapes=[pltpu.VMEM((rows, LANES), BF16), pltpu.VMEM((2, rows, tk), F32),
                        pltpu.VMEM((2, rows, tk), BF16), pltpu.VMEM((2, rows, LANES), F32),
                        pltpu.VMEM((2, rows, LANES), F32), pltpu.VMEM((2, rows, LANES), F32)])
    return pl.pallas_call(
        kern,
        grid_spec=grid_spec,
        out_shape=jax.ShapeDtypeStruct((b, s, NSA_WIDTH), BF16),
        compiler_params=_params(("parallel", "parallel", "parallel")),
        name="sel_win_attention",
    )(tiles, n_tiles, r3(q), r3(ks), r3(vs), r3(kw), r3(vw), pen, r3(gl), gexp, ocmp)


def _hgrn_rows(y, raw, onorm, st_scr, layer):
    c = HG_CHUNK
    ex = jnp.exp(raw - jnp.max(raw, axis=0, keepdims=True))
    sm = ex / jnp.sum(ex, axis=0, keepdims=True)
    lb_all = jnp.zeros((1, raw.shape[1]), F32)
    for l in range(1, layer + 1):
        lb_all = lb_all + sm[l:l + 1, :]

    t_idx = lax.broadcasted_iota(jnp.int32, (c, HG_DK), 0)
    sub = lax.broadcasted_iota(jnp.int32, (SUBLANES, HG_DK), 0)
    ti = lax.broadcasted_iota(jnp.int32, (c, c), 0)
    si = lax.broadcasted_iota(jnp.int32, (c, c), 1)
    tril = jnp.where(si <= ti, 1.0, 0.0).astype(BF16)
    levels = (32, 16, 8, 4, 2, 1)
    hc = HG_HEADS * c
    tb = lax.broadcasted_iota(jnp.int32, (hc, hc), 0)
    sb = lax.broadcasted_iota(jnp.int32, (hc, hc), 1)
    same_head = (tb // c) == (sb // c)
    split_bit = tb ^ sb
    pair_mask = {m: jnp.where(same_head & (tb > sb) & (split_bit >= m) & (split_bit < 2 * m), 1.0, 0.0)
                 for m in levels}
    diagonal = tb == sb
    second_half = {m: (t_idx & m) != 0 for m in levels}
    sign = {m: jnp.where(second_half[m], 1.0, -1.0) for m in levels}

    def ref_rows(b, m):
        row = lambda r, n: jnp.broadcast_to(b[r:r + 1, :], (n, HG_DK))
        if m >= 4:
            return jnp.concatenate([row(s0 + m - 1, 2 * m) for s0 in range(0, c, 2 * m)], axis=0)
        return jnp.concatenate([jnp.where(sub < 4, row(s0 + 1, SUBLANES), row(s0 + 5, SUBLANES))
                                for s0 in range(0, c, SUBLANES)], axis=0)

    part = lambda which, rows, h: y[rows, which * HG_WIDTH + h * HG_DK:which * HG_WIDTH + (h + 1) * HG_DK]
    out = []
    for ci in range(y.shape[0] // c):
        rows = slice(ci * c, (ci + 1) * c)
        q, k, f, v, logf = [], [], [], [], []
        for h in range(HG_HEADS):
            lb = lb_all[:, h * HG_DK:(h + 1) * HG_DK]
            z = part(1, rows, h)
            hq = part(0, rows, h)
            q.append(hq * _sigmoid(hq))
            ez = jnp.exp(-jnp.abs(z))
            big = 1.0 / (1.0 + ez)
            small = ez * big
            f.append(jnp.maximum(lb + (1.0 - lb) * jnp.where(z >= 0.0, big, small), F_FLOOR))
            logf.append(jnp.log2(f[h]))
            k.append((1.0 - lb) * jnp.where(z >= 0.0, small, big))
            v.append(part(2, rows, h).astype(BF16))
        parts = jnp.concatenate([p for h in range(HG_HEADS) for p in _split3(logf[h])], axis=1)
        csum = _nn(tril, parts)
        bcum = [sum(csum[:, (3 * h + i) * HG_DK:(3 * h + i + 1) * HG_DK] for i in range(3))
                for h in range(HG_HEADS)]

        rowdot = jnp.concatenate([jnp.sum(q[h] * k[h], axis=-1, keepdims=True) for h in range(HG_HEADS)], axis=0)
        a = jnp.where(diagonal, rowdot, 0.0)
        for m in levels:
            r = []
            for h in range(HG_HEADS):
                if m == 1:
                    r.append(jnp.where(second_half[m], q[h] * f[h], k[h]))
                else:
                    w = jnp.exp2((bcum[h] - ref_rows(bcum[h], m)) * sign[m])
                    r.append(jnp.where(second_half[m], q[h], k[h]) * w)
            r = jnp.concatenate(r, axis=0).astype(BF16)
            a = a + _nt(r, r) * pair_mask[m]
        o_intra = _nn(a.astype(BF16), jnp.concatenate(v, axis=0))

        heads = []
        for h in range(HG_HEADS):
            st = st_scr[h]
            o = o_intra[h * c:(h + 1) * c] + _nt((q[h] * jnp.exp2(bcum[h])).astype(BF16), st.astype(BF16))
            b_last = bcum[h][c - 1:c, :]
            kd = (k[h] * jnp.exp2(b_last - bcum[h])).astype(BF16)
            st_scr[h] = jnp.exp2(b_last) * st + _tn(v[h], kd)

            o = o * lax.rsqrt(jnp.mean(o * o, axis=-1, keepdims=True) + EPS) * onorm
            gate = part(3, rows, h)
            heads.append((o * (gate * _sigmoid(gate))).astype(BF16))
        out.append(jnp.concatenate(heads, axis=1))
    return jnp.concatenate(out, axis=0)


def _mixer(h, l, b, s, mix_norm, w_ext, cmp_k, cmp_v, hgrn_lower_bound, hgrn_out_norm):
    q, ks, vs, kw, vw, kc_in, vc_in, gl, o_hg = _inproj(h, mix_norm, w_ext, hgrn_lower_bound, hgrn_out_norm, s, l)
    kc = _compress(kc_in.reshape(b, s, KV_W), l, *cmp_k)
    vc2 = _compress(vc_in.reshape(b, s, KV_W), l, *cmp_v)
    ocmp, sel, picks = _cmp_attention(q, kc, vc2, b, s)
    o_nsa = _attention(q, ks, vs, kw, vw, sel, picks, gl, ocmp, b, s)
    return o_nsa.reshape(b * s, NSA_WIDTH), o_hg


def kernel(x, ffn1_norm, ffn1_w_gu, ffn1_w_down, mix_norm, w_in, cmp_pos_k, cmp_pos_v, cmp_k_w1, cmp_k_w2, cmp_v_w1, cmp_v_w2, hgrn_lower_bound, hgrn_out_norm, w_out, ffn2_norm, ffn2_w_gu, ffn2_w_down, final_norm):
    b, s, d = x.shape
    depth = ffn1_norm.shape[0]
    ffn1_w = _ffn_weights(ffn1_w_gu, ffn1_w_down)
    ffn2_w = _ffn_weights(ffn2_w_gu, ffn2_w_down)
    w_out = w_out.astype(BF16)
    w_ext = _build_w_in(w_in)
    cmp_k = _compress_weights(cmp_pos_k, cmp_k_w1, cmp_k_w2, (0, 1))
    cmp_v = _compress_weights(cmp_pos_v, cmp_v_w1, cmp_v_w2, (0, 1, 1, 0))
    h = x.reshape(b * s, d)
    for l in range(depth):
        h = _ffn(h, l, ffn1_norm, *ffn1_w, final_norm, False)
        o_nsa, o_hg = _mixer(h, l, b, s, mix_norm, w_ext, cmp_k, cmp_v, hgrn_lower_bound, hgrn_out_norm)
        h = _ffn(h, l, ffn2_norm, *ffn2_w, final_norm, l == depth - 1, (o_nsa, o_hg, w_out))
    return h.reshape(b, s, d)
```

```python
import functools

import jax
import jax.numpy as jnp
import numpy as np
from jax import lax
from jax.experimental import pallas as pl
from jax.experimental.pallas import tpu as pltpu

F32 = jnp.float32
BF16 = jnp.bfloat16

D_MODEL = 1024
EPS = 1e-6
NEG = -1e30
F_FLOOR = 1e-30
NSA_HEADS = 8
NSA_KV_GROUPS = 2
HEADS_PER_GROUP = NSA_HEADS // NSA_KV_GROUPS
NSA_HD = 64
CMP_LEN = 32
CMP_STRIDE = 16
CMP_HID = 256
SEL_BLOCK = 64
SEL_TOPK = 16
WINDOW = 512
HG_HEADS = 4
HG_DK = 128
HG_DV = 128
HG_CHUNK = 64
D_FF = 2752
NSA_WIDTH = NSA_HEADS * NSA_HD
HG_WIDTH = HG_HEADS * HG_DV
KV_W = NSA_KV_GROUPS * NSA_HD

LANES = 128
SUBLANES = 8
MXU_COLS = 256
FF_CHUNK = 256
D_FF_PAD = -(-D_FF // FF_CHUNK) * FF_CHUNK
ROW_TILE = 512
FFN_ROWS = 1024
ATT_TQ = 256
ATT_TK = 256
CMP_TQ = 1024
VMEM_LIMIT = 56 * 1024 * 1024

SEL_PENALTY = 2.0 ** 50
ALIBI_SLOPES = tuple(2.0 ** (-8.0 * (i + 1) / NSA_HEADS) for i in range(NSA_HEADS))

SEGMENTS = (
    ("q", NSA_HEADS * LANES, BF16, NSA_WIDTH),
    ("ks", NSA_KV_GROUPS * LANES, BF16, KV_W),
    ("vs", NSA_KV_GROUPS * LANES, BF16, KV_W),
    ("kw", NSA_KV_GROUPS * LANES, BF16, KV_W),
    ("vw", NSA_KV_GROUPS * LANES, BF16, KV_W),
    ("kc", KV_W, F32, KV_W),
    ("vc", KV_W, F32, KV_W),
    ("gl", LANES, F32, LANES),
    ("hg", HG_WIDTH, BF16, 4 * HG_WIDTH),
)
SEG_OFFSETS = tuple(int(v) for v in np.cumsum([0] + [s[3] for s in SEGMENTS]))


def _nn(a, b):
    return jnp.dot(a, b, preferred_element_type=F32)


def _nt(a, b):
    return lax.dot_general(a, b, (((1,), (1,)), ((), ())), preferred_element_type=F32)


def _tn(a, b):
    return lax.dot_general(a, b, (((0,), (0,)), ((), ())), preferred_element_type=F32)


def _split2(x):
    hi = x.astype(BF16)
    lo = (x - hi.astype(F32)).astype(BF16)
    return hi, lo


def _split3(x):
    hi = x.astype(BF16)
    r = x - hi.astype(F32)
    mid = r.astype(BF16)
    lo = (r - mid.astype(F32)).astype(BF16)
    return hi, mid, lo


def _sigmoid(x):
    return 1.0 / (1.0 + jnp.exp(-x))


def _rms(x, g):
    return x * lax.rsqrt(jnp.mean(x * x, axis=-1, keepdims=True) + EPS) * g


def _resident(shape):
    nd = len(shape)
    return pl.BlockSpec(shape, lambda *_: (0,) * nd, pipeline_mode=pl.Buffered(1))


def _resident_layer(stacked_shape, layer):
    nd = len(stacked_shape)
    return pl.BlockSpec((None,) + tuple(stacked_shape[1:]), lambda *_: (layer,) + (0,) * (nd - 1),
                        pipeline_mode=pl.Buffered(1))


def _params(sem):
    return pltpu.CompilerParams(dimension_semantics=sem, vmem_limit_bytes=VMEM_LIMIT)


def _ffn_kernel(*refs, final, proj):
    if proj:
        x_ref, a_ref, b_ref, wo_ref, g_ref, wg_ref, wu_ref, wd_ref, gf_ref, o_ref = refs
        x = x_ref[...] + _nn(a_ref[...], wo_ref[:NSA_WIDTH, :]) + _nn(b_ref[...], wo_ref[NSA_WIDTH:, :])
    else:
        x_ref, g_ref, wg_ref, wu_ref, wd_ref, gf_ref, o_ref = refs
        x = x_ref[...]
    xn = _rms(x, g_ref[...]).astype(BF16)
    acc = jnp.zeros(x.shape, F32)
    for k in range(D_FF_PAD // FF_CHUNK):
        sl = slice(k * FF_CHUNK, (k + 1) * FF_CHUNK)
        gate = _nn(xn, wg_ref[:, sl])
        up = _nn(xn, wu_ref[:, sl])
        h = (gate * _sigmoid(gate) * up).astype(BF16)
        acc = acc + _nn(h, wd_ref[sl, :])
    y = x + 0.5 * acc
    if final:
        y = _rms(y, gf_ref[...])
    o_ref[...] = y


def _ffn_weights(w_gu, w_down):
    pad = D_FF_PAD - D_FF
    wg = jnp.pad(w_gu[..., :D_FF], ((0, 0), (0, 0), (0, pad))).astype(BF16)
    wu = jnp.pad(w_gu[..., D_FF:], ((0, 0), (0, 0), (0, pad))).astype(BF16)
    wd = jnp.pad(w_down, ((0, 0), (0, pad), (0, 0))).astype(BF16)
    return wg, wu, wd


def _ffn(h, layer, norm_g, wg, wu, wd, final_g, final, proj=None):
    t = h.shape[0]
    row = lambda w: pl.BlockSpec((FFN_ROWS, w), lambda i: (i, 0))
    norm_g = norm_g[:, None, :]
    weights = [_resident_layer(a.shape, layer) for a in (norm_g, wg, wu, wd)] + [_resident((1, D_MODEL))]
    operands = [norm_g, wg, wu, wd, final_g.reshape(1, -1)]
    if proj is None:
        in_specs, args = [row(D_MODEL)] + weights, [h] + operands
    else:
        o_nsa, o_hg, w_out = proj
        in_specs = [row(D_MODEL), row(NSA_WIDTH), row(HG_WIDTH), _resident_layer(w_out.shape, layer)] + weights
        args = [h, o_nsa, o_hg, w_out] + operands
    return pl.pallas_call(
        functools.partial(_ffn_kernel, final=final, proj=proj is not None),
        grid=(t // FFN_ROWS,),
        in_specs=in_specs,
        out_specs=row(D_MODEL),
        out_shape=jax.ShapeDtypeStruct((t, D_MODEL), F32),
        compiler_params=_params(("parallel",)),
        name="ffn",
    )(*args)


def _inproj_kernel(x_ref, g_ref, w_ref, qfeat_ref, lbraw_ref, onorm_ref, *refs, seq, layer):
    *o_refs, xn_scr, y_scr, st_scr = refs
    out = {seg[0]: o_ref for seg, o_ref in zip(SEGMENTS, o_refs, strict=True)}
    col0 = {seg[0]: off for seg, off in zip(SEGMENTS, SEG_OFFSETS[:-1], strict=True)}
    rows = x_ref.shape[0]
    half = rows // 2
    tile = pl.program_id(0) % (seq // rows)

    @pl.when(tile == 0)
    def _():
        st_scr[...] = jnp.zeros(st_scr.shape, F32)

    def norm_piece(r):
        rs = slice(r * half, (r + 1) * half)
        xn_scr[rs, :] = _rms(x_ref[rs, :], g_ref[...]).astype(BF16)

    lane = lax.broadcasted_iota(jnp.int32, (half, LANES), 1)
    lower = lane < NSA_HD
    swap = lambda a: pltpu.roll(a, NSA_HD, axis=1)
    halves = [NSA_HD * (1 - g) for g in range(NSA_KV_GROUPS)]
    project = lambda rs, c0, cols: _nn(xn_scr[rs, :], w_ref[:, c0:c0 + cols])

    def q_piece(r, cb):
        def emit():
            rs = slice(r * half, (r + 1) * half)
            y = project(rs, col0["q"] + cb * MXU_COLS, MXU_COLS)
            per = MXU_COLS // NSA_HD
            blocks = []
            for h in range(cb * per, (cb + 1) * per):
                pair = y[:, (h % per // 2) * LANES:(h % per // 2 + 1) * LANES]
                g = h // HEADS_PER_GROUP
                data = pair if h % 2 == g else swap(pair)
                feat = qfeat_ref[:, h * LANES:(h + 1) * LANES]
                blocks.append(jnp.where(lower == (g == 0), data, feat))
            out["q"][rs, cb * per * LANES:(cb + 1) * per * LANES] = (
                jnp.concatenate(blocks, axis=1).astype(out["q"].dtype))
        return emit

    small = [seg[0] for seg in SEGMENTS if seg[3] == LANES]
    assert all(col0[b] - col0[a] == LANES for a, b in zip(small, small[1:]))

    def small_piece(r, names):
        def emit():
            rs = slice(r * half, (r + 1) * half)
            y_small = project(rs, col0[names[0]], len(names) * LANES)
            kpos = tile * rows + r * half + lax.broadcasted_iota(jnp.int32, (half, LANES), 0)
            blk = kpos // SEL_BLOCK
            digit = lambda f0: jnp.where(lane == f0, blk.astype(F32),
                                         jnp.where(lane == f0 + 1, (kpos % SEL_BLOCK).astype(F32), 0.0))
            is_pen = lambda f0: (lane == f0 + 1 + blk) & (blk >= 1) & (blk <= seq // SEL_BLOCK - 2)
            for n, name in enumerate(names):
                y = y_small[:, n * LANES:(n + 1) * LANES]
                if name in ("ks", "kw"):
                    feat = [digit(f0) + jnp.where(is_pen(f0), -SEL_PENALTY, 0.0) if name == "ks" else digit(f0)
                            for f0 in halves]
                    y = jnp.concatenate([jnp.where(lower == (g == 0), y, feat[g])
                                         for g in range(NSA_KV_GROUPS)], axis=1)
                elif name in ("vs", "vw"):
                    y = jnp.concatenate([jnp.where(lower, y if g == 0 else swap(y), 1.0)
                                         for g in range(NSA_KV_GROUPS)], axis=1)
                out[name][rs, :] = y.astype(out[name].dtype)
        return emit

    def hg_piece(r, cb):
        def emit():
            rs = slice(r * half, (r + 1) * half)
            y_scr[rs, cb * MXU_COLS:(cb + 1) * MXU_COLS] = project(rs, col0["hg"] + cb * MXU_COLS, MXU_COLS)
        return emit

    hgrn_chunk = _hgrn_chunks(lbraw_ref[...], onorm_ref[...], layer)
    n_chunks = rows // HG_CHUNK
    hg_tiles = 4 * HG_WIDTH // MXU_COLS
    norm_piece(0)
    hg_piece(0, 0)()
    norm_piece(1)
    for cb in range(1, hg_tiles):
        hg_piece(0, cb)()
    later = [hg_piece(1, cb) for cb in range(hg_tiles)]
    rest = [p for r in range(2) for p in (q_piece(r, 0), q_piece(r, 1), small_piece(r, small[0:2]),
                                          small_piece(r, small[2:4]), small_piece(r, small[4:]))]
    spread = lambda items, n: [items[i * len(items) // n:(i + 1) * len(items) // n] for i in range(n)]
    fills = spread(later, n_chunks // 2) + spread(rest, n_chunks - n_chunks // 2)
    last = fills[-1].pop()
    for ci, fill in enumerate(fills):
        cs = slice(ci * HG_CHUNK, (ci + 1) * HG_CHUNK)
        out["hg"][cs, :] = hgrn_chunk(y_scr, cs, st_scr, fill)
    last()


def _q_features():
    feat = np.zeros((1, NSA_HEADS * LANES), np.float32)
    for h in range(NSA_HEADS):
        f0 = h * LANES + NSA_HD * (1 - h // HEADS_PER_GROUP)
        feat[0, f0] = SEL_BLOCK * ALIBI_SLOPES[h]
        feat[0, f0 + 1] = ALIBI_SLOPES[h]
    return jnp.asarray(feat)


def _build_w_in(w_in):
    sizes = (NSA_WIDTH, KV_W, KV_W, KV_W, KV_W, KV_W, KV_W, NSA_HEADS * 3,
             HG_WIDTH, HG_WIDTH, HG_WIDTH, HG_WIDTH)
    splits = [int(v) for v in np.cumsum(sizes)[:-1]]
    w_in = w_in.astype(BF16)
    wq, wkc, wvc, wks, wvs, wkw, wvw, wgl, whq, whf, whi, whg = jnp.split(w_in, splits, axis=-1)
    lead = w_in.shape[:-1]
    gl = jnp.swapaxes(wgl.reshape(lead + (NSA_HEADS, 3)), -1, -2).reshape(lead + (3 * NSA_HEADS,))
    gl = jnp.pad(gl, [(0, 0)] * len(lead) + [(0, LANES - 3 * NSA_HEADS)])
    cols = [wq * NSA_HD ** -0.5, wks, wvs, wkw, wvw, wkc, wvc, gl, whq, whf, whi, whg]
    return jnp.concatenate(cols, axis=-1)


def _inproj(h, norm_g, w_ext, lb_raw, out_norm, seq, layer):
    t = h.shape[0]
    assert seq % ROW_TILE == 0 and ROW_TILE % HG_CHUNK == 0
    row = lambda w: pl.BlockSpec((ROW_TILE, w), lambda i: (i, 0))
    norm_g, out_norm = norm_g[:, None, :], out_norm[:, None, :]
    return pl.pallas_call(
        functools.partial(_inproj_kernel, seq=seq, layer=layer),
        grid=(t // ROW_TILE,),
        in_specs=[row(D_MODEL), _resident_layer(norm_g.shape, layer), _resident_layer(w_ext.shape, layer),
                  _resident((1, NSA_HEADS * LANES)), _resident(lb_raw.shape),
                  _resident_layer(out_norm.shape, layer)],
        out_specs=[row(w) for _, w, _, _ in SEGMENTS],
        out_shape=[jax.ShapeDtypeStruct((t, w), dt) for _, w, dt, _ in SEGMENTS],
        scratch_shapes=[pltpu.VMEM((ROW_TILE, D_MODEL), BF16), pltpu.VMEM((ROW_TILE, 4 * HG_WIDTH), F32),
                        pltpu.VMEM((HG_HEADS, HG_DV, HG_DK), F32)],
        compiler_params=_params(("arbitrary",)),
        name="inproj_hgrn",
    )(h, norm_g, w_ext, _q_features(), lb_raw, out_norm)


def _gelu_tanh(x):
    return 0.5 * x * (1.0 + jnp.tanh(0.7978845608028654 * (x + 0.044715 * (x * x * x))))


def _compress_kernel(kv_ref, pos_ref, w1_ref, w2_ref, o_ref):
    nbp = o_ref.shape[1]
    tokens = [kv_ref[0, pl.ds(l, nbp, stride=CMP_STRIDE), :] for l in range(CMP_STRIDE)]
    chunk = lambda first: jnp.concatenate(
        [(tokens[l] + pos_ref[first + l:first + l + 1, :]).astype(BF16) for l in range(CMP_STRIDE)], axis=1)
    slab = lambda first: w1_ref[first:first + CMP_STRIDE].reshape(CMP_STRIDE * KV_W, NSA_KV_GROUPS * CMP_HID)
    ha = _nn(chunk(0), slab(0))
    hb = _nn(chunk(CMP_STRIDE), slab(CMP_STRIDE))
    act = _gelu_tanh(ha + pltpu.roll(hb, nbp - 1, axis=0)).astype(BF16)
    out = jnp.zeros(o_ref.shape[1:], F32)
    for g in range(NSA_KV_GROUPS):
        out = out + _nn(act[:, g * CMP_HID:(g + 1) * CMP_HID], w2_ref[g])
    o_ref[0] = out


def _compress_weights(pos, w1, w2, reps):
    layers = w1.shape[0]
    w1l = w1.astype(BF16).reshape(layers, CMP_LEN, NSA_HD, CMP_HID)
    z1 = jnp.zeros_like(w1l)
    w1p = jnp.concatenate([jnp.concatenate([w1l, z1], axis=3), jnp.concatenate([z1, w1l], axis=3)], axis=2)
    pos2 = jnp.concatenate([pos] * NSA_KV_GROUPS, axis=2)
    w2 = w2.astype(BF16)
    zero = jnp.zeros_like(w2)
    w2p = jnp.stack([jnp.concatenate([w2 if r == g else zero for r in reps], axis=2)
                     for g in range(NSA_KV_GROUPS)], axis=1)
    return pos2, w1p, w2p


def _compress(kv, layer, pos2, w1p, w2p):
    b, s, _ = kv.shape
    nbp = s // CMP_STRIDE
    width = w2p.shape[-1]
    return pl.pallas_call(
        _compress_kernel,
        grid=(b,),
        in_specs=[pl.BlockSpec((1, s, KV_W), lambda i: (i, 0, 0))]
        + [_resident_layer(a.shape, layer) for a in (pos2, w1p, w2p)],
        out_specs=pl.BlockSpec((1, nbp, width), lambda i: (i, 0, 0)),
        out_shape=jax.ShapeDtypeStruct((b, nbp, width), F32),
        compiler_params=_params(("parallel",)),
        name="compress",
    )(kv, pos2, w1p, w2p)


def _pair_blocks(g, jj):
    lower = slice(0, LANES) if g == 0 else slice(LANES, 2 * LANES)
    upper = slice(LANES, 2 * LANES) if g == 0 else slice(0, LANES)
    return lower, upper


def _cmp_kernel(*refs, tq, ns, n_sel):
    total = refs[1].shape[1]
    per_tile = tq // CMP_STRIDE
    for t in range(total // per_tile):
        nbp = min(total, ((t + 1) * per_tile + LANES - 1) // LANES * LANES)
        sel_rows = min(ns, (t + 1) * tq // SEL_BLOCK)
        rounds = 0 if sel_rows <= n_sel else n_sel - 3

        @pl.when(pl.program_id(1) == t)
        def _(nbp=nbp, sel_rows=sel_rows, rounds=rounds):
            _cmp_tile(*refs, tq=tq, ns=ns, nbp=nbp, sel_rows=sel_rows, rounds=rounds)


def _cmp_tile(q_ref, kc_ref, vc2_ref, mt_ref, ocmp_ref, sel_ref, picks_ref, *, tq, ns, nbp, sel_rows, rounds):
    q0 = pl.program_id(1) * tq
    pos = q0 + lax.broadcasted_iota(jnp.int32, (tq, nbp), 0)
    blk_end = lax.broadcasted_iota(jnp.int32, (tq, nbp), 1) * CMP_STRIDE + (CMP_LEN - 1)
    valid = blk_end <= pos
    row_ok = (q0 + lax.broadcasted_iota(jnp.int32, (tq, 1), 0)) >= CMP_LEN - 1
    kc = kc_ref[0, :nbp, :].astype(BF16)
    vc2 = vc2_ref[0, :nbp, :].astype(BF16)
    lane_c = lax.broadcasted_iota(jnp.int32, (nbp, LANES), 1)
    c_idx = lax.broadcasted_iota(jnp.int32, (nbp, LANES), 0)
    per = SEL_BLOCK // CMP_STRIDE
    feat_a = (c_idx // per - q0 // SEL_BLOCK).astype(F32)
    feat_b = ((c_idx % per) * CMP_STRIDE + (CMP_LEN - 1)).astype(F32)
    lane_lo = lax.broadcasted_iota(jnp.int32, (tq, LANES), 1) < NSA_HD
    blk = lax.broadcasted_iota(jnp.int32, (ns, tq), 0)
    blk_r = lax.broadcasted_iota(jnp.int32, (sel_rows, tq), 0)
    blk_f = blk_r.astype(F32)
    pos_t = q0 + lax.broadcasted_iota(jnp.int32, (sel_rows, tq), 1)
    cur = pos_t // SEL_BLOCK
    forced = (blk_r == 0) | (blk_r == cur) | (blk_r == cur - 1)
    causal = blk_r * SEL_BLOCK <= pos_t
    sel_t, picks = [], []
    taken = -jnp.inf
    for g in range(NSA_KV_GROUPS):
        f0 = NSA_HD * (1 - g)
        feat = jnp.where(lane_c == f0, feat_a, jnp.where(lane_c == f0 + 1, feat_b, 0.0))
        kc_g = jnp.where((lane_c // NSA_HD) == g, kc, feat.astype(BF16))
        imp = jnp.zeros((tq, nbp), F32)
        acc = []
        for hh in range(HEADS_PER_GROUP):
            h = g * HEADS_PER_GROUP + hh
            s = _nt(q_ref[0, :, h * LANES:(h + 1) * LANES], kc_g)
            s = jnp.where(valid, s, NEG)
            e = jnp.exp(s - jnp.max(s, axis=-1, keepdims=True))
            inv = jnp.where(row_ok, 1.0 / jnp.sum(e, axis=-1, keepdims=True), 0.0)
            p = e * inv
            imp = imp + p
            acc.append(_nn(p.astype(BF16), vc2))
        for jj in range(HEADS_PER_GROUP // 2):
            lower, upper = _pair_blocks(g, jj)
            blk_out = jnp.where(lane_lo, acc[2 * jj][:, lower], acc[2 * jj + 1][:, upper])
            c0 = (g * HEADS_PER_GROUP + 2 * jj) * NSA_HD
            ocmp_ref[0, :, c0:c0 + LANES] = blk_out
        p_slc = sum(_nt(mt_ref[:sel_rows, :nbp], part) for part in _split3(imp))
        if rounds == 0:
            score = jnp.where(forced | causal, taken, NEG)
        else:
            score = jnp.where(forced, taken, jnp.where(causal, p_slc, NEG))
        for _ in range(rounds):
            top = jnp.max(score, axis=0, keepdims=True)
            first = jnp.min(jnp.where(score == top, blk_f, float(ns)), axis=0, keepdims=True)
            score = jnp.where(blk_f == first, taken, score)
        if sel_rows < ns:
            score = jnp.concatenate([score, jnp.full((ns - sel_rows, tq), NEG, F32)], axis=0)
        picked = jnp.where(score == taken, 1.0, 0.0).astype(BF16)
        picks.append([_nt(jnp.ones((SUBLANES, ATT_TQ), BF16), picked[:, t * ATT_TQ:(t + 1) * ATT_TQ])
                      for t in range(tq // ATT_TQ)])
        pen = jnp.where(score == taken, 0.0, SEL_PENALTY)
        pen = jnp.where((blk >= 2) & (blk <= ns - 1), pltpu.roll(pen, 1, axis=0), 0.0)
        if ns < NSA_HD:
            pen = jnp.concatenate([pen, jnp.zeros((NSA_HD - ns, tq), F32)], axis=0)
        zero = jnp.zeros((NSA_HD, tq), F32)
        sel_t += [zero, pen] if g == 0 else [pen, zero]
    sel_ref[0] = jnp.concatenate(sel_t, axis=0).T.astype(BF16)
    for t in range(tq // ATT_TQ):
        picks_ref[0, t] = jnp.concatenate([group[t] for group in picks], axis=1)


def _importance_matrix(nbp, ns):
    per = SEL_BLOCK // CMP_STRIDE
    m = np.zeros((ns, nbp), np.float32)
    for n in range(ns):
        for c in range(per * n, per * (n + 1)):
            for cc in (c - 1, c):
                if 0 <= cc < nbp - 1:
                    m[n, cc] += 1.0
    return jnp.asarray(m, BF16)


def _cmp_attention(q, kc, vc2, b, s):
    nbp = s // CMP_STRIDE
    ns = s // SEL_BLOCK
    tq = min(CMP_TQ, s)
    assert ns <= NSA_HD
    assert min(SEL_TOPK, ns) >= 3 and tq % ATT_TQ == 0 and tq % (SUBLANES * SEL_BLOCK) == 0
    q3 = q.reshape(b, s, NSA_HEADS * LANES)
    kern = functools.partial(_cmp_kernel, tq=tq, ns=ns, n_sel=min(SEL_TOPK, ns))
    return pl.pallas_call(
        kern,
        grid=(b, s // tq),
        in_specs=[pl.BlockSpec((1, tq, NSA_HEADS * LANES), lambda i, j: (i, j, 0)),
                  pl.BlockSpec((1, nbp, KV_W), lambda i, j: (i, 0, 0)),
                  pl.BlockSpec((1, nbp, 2 * KV_W), lambda i, j: (i, 0, 0)),
                  _resident((ns, nbp))],
        out_specs=[pl.BlockSpec((1, tq, NSA_WIDTH), lambda i, j: (i, j, 0)),
                   pl.BlockSpec((1, tq, NSA_KV_GROUPS * LANES), lambda i, j: (i, j, 0)),
                   pl.BlockSpec((1, tq // ATT_TQ, SUBLANES, NSA_KV_GROUPS * ns), lambda i, j: (i, j, 0, 0))],
        out_shape=[jax.ShapeDtypeStruct((b, s, NSA_WIDTH), F32),
                   jax.ShapeDtypeStruct((b, s, NSA_KV_GROUPS * LANES), BF16),
                   jax.ShapeDtypeStruct((b, s // ATT_TQ, SUBLANES, NSA_KV_GROUPS * ns), F32)],
        compiler_params=_params(("parallel", "parallel")),
        name="cmp_topk",
    )(q3, kc, vc2, _importance_matrix(nbp, ns))


def _attn_kernel(tiles_ref, ntiles_ref, q_ref, ks_ref, vs_ref, kw_ref, vw_ref, penq_ref, gl_ref, gexp_ref,
                 ocmp_ref, o_ref, qa_scr, s_scr, p_scr, al_scr, m_scr, acc_scr, *, tq, tk):
    i = pl.program_id(2)
    rows_of = lambda hh: slice(hh * tq, (hh + 1) * tq)
    lane_lo = lax.broadcasted_iota(jnp.int32, (tq, LANES), 1) < NSA_HD
    rel = (lax.broadcasted_iota(jnp.int32, (tq, tk), 1)
           - lax.broadcasted_iota(jnp.int32, (tq, tk), 0)).astype(F32)

    for hh in range(HEADS_PER_GROUP):
        qa_scr[rows_of(hh)] = q_ref[0, :, hh * LANES:(hh + 1) * LANES] + penq_ref[0]

    def scores(k_ref, j, slot):
        s_scr[slot] = _nt(qa_scr[...], k_ref[0, pl.ds(pl.multiple_of(j * tk, tk), tk), :])

    def probs(br, j, mode, slot, first, exists=None):
        for hh in range(HEADS_PER_GROUP):
            r = rows_of(hh)
            s = s_scr[slot, r]
            if mode == "causal":
                s = jnp.where(rel <= 0.0, s, NEG)
            elif mode == "lower":
                s = jnp.where(rel + (j * tk - i * tq).astype(F32) > -float(WINDOW), s, NEG)
            if exists is not None:
                s = jnp.where(exists, s, NEG)
            m_cur = jnp.max(s, axis=-1, keepdims=True)
            if first:
                m_new = jnp.broadcast_to(m_cur, (tq, LANES))
            else:
                m_prev = m_scr[br, r]
                m_new = jnp.maximum(m_prev, m_cur)
                al_scr[slot, r] = jnp.exp(m_prev - m_new)
            m_scr[br, r] = m_new
            p_scr[slot, r] = jnp.exp(s - jnp.concatenate([m_new] * (tk // LANES), axis=1)).astype(BF16)

    def accumulate(br, v_ref, j, slot, first):
        pv = _nn(p_scr[slot], v_ref[0, pl.ds(pl.multiple_of(j * tk, tk), tk), :])
        for hh in range(HEADS_PER_GROUP):
            r = rows_of(hh)
            if first:
                acc_scr[br, r] = pv[r]
            else:
                acc_scr[br, r] = al_scr[slot, r] * acc_scr[br, r] + pv[r]

    n_q = pl.num_programs(2)
    step = (pl.program_id(0) * n_q + i) * pl.num_programs(1) + pl.program_id(1)
    n_tiles = ntiles_ref[step]
    item = lambda br, j, mode, first, exists=None: (br, j, mode, first, exists)
    sel_item = lambda n: item(0, i, "causal", True) if isinstance(n, int) and n == 0 else item(
        0, tiles_ref[step * n_q + n - 1], None, False)
    n_back = WINDOW // tk
    win_items = [item(1, i, "causal", True)] + [
        item(1, jnp.maximum(i - d, 0), "lower" if d == n_back else None, False, i >= d)
        for d in range(1, n_back + 1)]
    k_refs, v_refs = (ks_ref, kw_ref), (vs_ref, vw_ref)

    def run(items, slot0, done=(0, 0, 0), upto=None):
        n_items = len(items)
        slot = lambda n: (slot0 + n) % 2
        n_s, n_p, n_a = done

        def do_scores(n):
            br, j, _, _, _ = items[n]
            scores(k_refs[br], j, slot(n))

        def do_probs(n):
            br, j, mode, first, exists = items[n]
            probs(br, j, mode, slot(n), first, exists)

        def do_acc(n):
            br, j, _, first, _ = items[n]
            accumulate(br, v_refs[br], j, slot(n), first)

        for n in range(n_s, min(n_a + 2, n_items)):
            do_scores(n)
        n_s = max(n_s, min(n_a + 2, n_items))
        for n in range(n_p, min(n_a + 1, n_items)):
            do_probs(n)
        n_p = max(n_p, min(n_a + 1, n_items))
        for k in range(n_a, n_items if upto is None else upto):
            if n_s <= k + 2 < n_items:
                do_scores(k + 2)
                n_s = k + 3
            if n_p <= k + 1 < n_items:
                do_probs(k + 1)
                n_p = k + 2
            do_acc(k)

    for n_static in range(2):
        @pl.when(n_tiles == n_static)
        def _(n_static=n_static):
            run([sel_item(n) for n in range(n_static + 1)] + win_items, 0)

    @pl.when(n_tiles >= 2)
    def _():
        run([sel_item(n) for n in range(3)], 0, upto=1)

        def sel_step(n, parity):
            scores(ks_ref, sel_item(n + 2)[1], parity)
            probs(0, None, None, 1 - parity, False)
            accumulate(0, vs_ref, sel_item(n)[1], parity, False)

        def sel_two_steps(t, carry):
            sel_step(2 * t + 1, 1)
            sel_step(2 * t + 2, 0)
            return carry

        lax.fori_loop(0, (n_tiles - 2) // 2, sel_two_steps, 0)

        def drain(parity):
            tail = [sel_item(n_tiles - 1), sel_item(n_tiles)] + win_items
            run(tail, 1 - parity, done=(2, 1, 0))

        @pl.when(n_tiles % 2 == 0)
        def _():
            drain(0)

        @pl.when(n_tiles % 2 == 1)
        def _():
            sel_step(n_tiles - 2, 1)
            drain(1)

    gexp = sum(_nn(part, gexp_ref[0]) for part in _split2(_sigmoid(gl_ref[0])))
    n_pairs = HEADS_PER_GROUP // 2
    gate_blk = lambda br, jj: gexp[:, (br * n_pairs + jj) * LANES:(br * n_pairs + jj + 1) * LANES]
    for jj in range(n_pairs):
        cols = slice(jj * LANES, (jj + 1) * LANES)
        blk = gate_blk(0, jj) * ocmp_ref[0, :, cols]
        for br in range(2):
            even = acc_scr[br, rows_of(2 * jj)]
            odd = acc_scr[br, rows_of(2 * jj + 1)]
            low = even / pltpu.roll(even, NSA_HD, axis=1)
            up = pltpu.roll(odd, NSA_HD, axis=1) / odd
            blk = blk + gate_blk(br + 1, jj) * jnp.where(lane_lo, low, up)
        o_ref[0, :, cols] = blk.astype(o_ref.dtype)


def _gate_expansion():
    n_pairs = HEADS_PER_GROUP // 2
    r = np.zeros((NSA_KV_GROUPS, LANES, 3 * n_pairs * LANES), np.float32)
    for g in range(NSA_KV_GROUPS):
        for br in range(3):
            for jj in range(n_pairs):
                for odd in range(2):
                    src = br * NSA_HEADS + g * HEADS_PER_GROUP + 2 * jj + odd
                    dst = (br * n_pairs + jj) * LANES + odd * NSA_HD
                    r[g, src, dst:dst + NSA_HD] = 1.0
    return jnp.asarray(r, BF16)


def _picked_tiles(picks, b, s):
    ns, nq = s // SEL_BLOCK, s // ATT_TQ
    per_tile = ATT_TK // SEL_BLOCK
    per_block = picks[:, :, 0, :].reshape(b, nq, NSA_KV_GROUPS, ns // per_tile, per_tile)
    j = jnp.arange(ns // per_tile, dtype=jnp.int32)
    i = jnp.arange(nq, dtype=jnp.int32)[None, :, None, None]
    active = ((per_block.sum(-1) > 0) | (j == 0)) & (j < i)
    slot = jnp.cumsum(active, axis=-1) - 1
    hit = active[..., None, :] & (slot[..., None, :] == j[:, None])
    tiles = jnp.sum(jnp.where(hit, j, 0), axis=-1).astype(jnp.int32)
    return tiles.reshape(-1), active.sum(-1).astype(jnp.int32).reshape(-1)


def _attention(q, ks, vs, kw, vw, pen, picks, gl, ocmp, b, s):
    tq, tk = ATT_TQ, ATT_TK
    assert tq == tk and WINDOW % tk == 0 and WINDOW // tk <= 2 and tq % SEL_BLOCK == 0
    gw = HEADS_PER_GROUP * LANES
    ow = HEADS_PER_GROUP * NSA_HD
    r3 = lambda a: a.reshape(b, s, a.shape[-1])
    tile = lambda w: pl.BlockSpec((1, tq, w), lambda i, g, j, *_: (i, j, 0))
    gtile = lambda w: pl.BlockSpec((1, tq, w), lambda i, g, j, *_: (i, j, g))
    gfull = pl.BlockSpec((1, s, LANES), lambda i, g, j, *_: (i, 0, g))
    gexp = _gate_expansion()
    rows = HEADS_PER_GROUP * tq
    kern = functools.partial(_attn_kernel, tq=tq, tk=tk)
    tiles, n_tiles = _picked_tiles(picks, b, s)
    grid_spec = pltpu.PrefetchScalarGridSpec(
        num_scalar_prefetch=2,
        grid=(b, NSA_KV_GROUPS, s // tq),
        in_specs=[gtile(gw), gfull, gfull, gfull, gfull, gtile(LANES), tile(LANES),
                  pl.BlockSpec((1,) + gexp.shape[1:], lambda i, g, j, *_: (g, 0, 0)), gtile(ow)],
        out_specs=gtile(ow),
        scratch_shapes=[pltpu.VMEM((rows, LANES), BF16), pltpu.VMEM((2, rows, tk), F32),
                        pltpu.VMEM((2, rows, tk), BF16), pltpu.VMEM((2, rows, LANES), F32),
                        pltpu.VMEM((2, rows, LANES), F32), pltpu.VMEM((2, rows, LANES), F32)])
    return pl.pallas_call(
        kern,
        grid_spec=grid_spec,
        out_shape=jax.ShapeDtypeStruct((b, s, NSA_WIDTH), BF16),
        compiler_params=_params(("parallel", "parallel", "parallel")),
        name="sel_win_attention",
    )(tiles, n_tiles, r3(q), r3(ks), r3(vs), r3(kw), r3(vw), pen, r3(gl), gexp, ocmp)


def _hgrn_chunks(raw, onorm, layer):
    c = HG_CHUNK
    ex = jnp.exp(raw - jnp.max(raw, axis=0, keepdims=True))
    sm = ex / jnp.sum(ex, axis=0, keepdims=True)
    lb_all = jnp.zeros((1, raw.shape[1]), F32)
    for l in range(1, layer + 1):
        lb_all = lb_all + sm[l:l + 1, :]

    t_idx = lax.broadcasted_iota(jnp.int32, (c, HG_DK), 0)
    sub = lax.broadcasted_iota(jnp.int32, (SUBLANES, HG_DK), 0)
    ti = lax.broadcasted_iota(jnp.int32, (c, c), 0)
    si = lax.broadcasted_iota(jnp.int32, (c, c), 1)
    tril = jnp.where(si <= ti, 1.0, 0.0).astype(BF16)
    levels = (32, 16, 8, 4, 2, 1)
    hc = HG_HEADS * c
    tb = lax.broadcasted_iota(jnp.int32, (hc, hc), 0)
    sb = lax.broadcasted_iota(jnp.int32, (hc, hc), 1)
    same_head = (tb // c) == (sb // c)
    split_bit = tb ^ sb
    pair_mask = {m: jnp.where(same_head & (tb > sb) & (split_bit >= m) & (split_bit < 2 * m), 1.0, 0.0)
                 for m in levels}
    diagonal = tb == sb
    second_half = {m: (t_idx & m) != 0 for m in levels}
    sign = {m: jnp.where(second_half[m], 1.0, -1.0) for m in levels}

    def ref_rows(b, m):
        row = lambda r, n: jnp.broadcast_to(b[r:r + 1, :], (n, HG_DK))
        if m >= 4:
            return jnp.concatenate([row(s0 + m - 1, 2 * m) for s0 in range(0, c, 2 * m)], axis=0)
        return jnp.concatenate([jnp.where(sub < 4, row(s0 + 1, SUBLANES), row(s0 + 5, SUBLANES))
                                for s0 in range(0, c, SUBLANES)], axis=0)

    def chunk(y_ref, rows, st_scr, fill=()):
        fill_at = {j * len(levels) // len(fill): emit for j, emit in enumerate(fill)}
        part = lambda which, h: y_ref[rows, which * HG_WIDTH + h * HG_DK:which * HG_WIDTH + (h + 1) * HG_DK]
        q, k, f, v, logf = [], [], [], [], []
        for h in range(HG_HEADS):
            lb = lb_all[:, h * HG_DK:(h + 1) * HG_DK]
            z = part(1, h)
            hq = part(0, h)
            q.append(hq * _sigmoid(hq))
            ez = jnp.exp(-jnp.abs(z))
            big = 1.0 / (1.0 + ez)
            small = ez * big
            f.append(jnp.maximum(lb + (1.0 - lb) * jnp.where(z >= 0.0, big, small), F_FLOOR))
            logf.append(jnp.log2(f[h]))
            k.append((1.0 - lb) * jnp.where(z >= 0.0, small, big))
            v.append(part(2, h).astype(BF16))
        parts = jnp.concatenate([p for h in range(HG_HEADS) for p in _split3(logf[h])], axis=1)
        csum = _nn(tril, parts)
        bcum = [sum(csum[:, (3 * h + i) * HG_DK:(3 * h + i + 1) * HG_DK] for i in range(3))
                for h in range(HG_HEADS)]

        rowdot = jnp.concatenate([jnp.sum(q[h] * k[h], axis=-1, keepdims=True) for h in range(HG_HEADS)], axis=0)
        a = jnp.where(diagonal, rowdot, 0.0)
        for n, m in enumerate(levels):
            if n in fill_at:
                fill_at[n]()
            r = []
            for h in range(HG_HEADS):
                if m == 1:
                    r.append(jnp.where(second_half[m], q[h] * f[h], k[h]))
                else:
                    w = jnp.exp2((bcum[h] - ref_rows(bcum[h], m)) * sign[m])
                    r.append(jnp.where(second_half[m], q[h], k[h]) * w)
            r = jnp.concatenate(r, axis=0).astype(BF16)
            a = a + _nt(r, r) * pair_mask[m]
        o_intra = _nn(a.astype(BF16), jnp.concatenate(v, axis=0))

        heads = []
        for h in range(HG_HEADS):
            st = st_scr[h]
            o = o_intra[h * c:(h + 1) * c] + _nt((q[h] * jnp.exp2(bcum[h])).astype(BF16), st.astype(BF16))
            b_last = bcum[h][c - 1:c, :]
            kd = (k[h] * jnp.exp2(b_last - bcum[h])).astype(BF16)
            st_scr[h] = jnp.exp2(b_last) * st + _tn(v[h], kd)

            o = o * lax.rsqrt(jnp.mean(o * o, axis=-1, keepdims=True) + EPS) * onorm
            gate = part(3, h)
            heads.append((o * (gate * _sigmoid(gate))).astype(BF16))
        return jnp.concatenate(heads, axis=1)

    return chunk


def _mixer(h, l, b, s, mix_norm, w_ext, cmp_k, cmp_v, hgrn_lower_bound, hgrn_out_norm):
    q, ks, vs, kw, vw, kc_in, vc_in, gl, o_hg = _inproj(h, mix_norm, w_ext, hgrn_lower_bound, hgrn_out_norm, s, l)
    kc = _compress(kc_in.reshape(b, s, KV_W), l, *cmp_k)
    vc2 = _compress(vc_in.reshape(b, s, KV_W), l, *cmp_v)
    ocmp, sel, picks = _cmp_attention(q, kc, vc2, b, s)
    o_nsa = _attention(q, ks, vs, kw, vw, sel, picks, gl, ocmp, b, s)
    return o_nsa.reshape(b * s, NSA_WIDTH), o_hg


def kernel(x, ffn1_norm, ffn1_w_gu, ffn1_w_down, mix_norm, w_in, cmp_pos_k, cmp_pos_v, cmp_k_w1, cmp_k_w2, cmp_v_w1, cmp_v_w2, hgrn_lower_bound, hgrn_out_norm, w_out, ffn2_norm, ffn2_w_gu, ffn2_w_down, final_norm):
    b, s, d = x.shape
    depth = ffn1_norm.shape[0]
    ffn1_w = _ffn_weights(ffn1_w_gu, ffn1_w_down)
    ffn2_w = _ffn_weights(ffn2_w_gu, ffn2_w_down)
    w_out = w_out.astype(BF16)
    w_ext = _build_w_in(w_in)
    cmp_k = _compress_weights(cmp_pos_k, cmp_k_w1, cmp_k_w2, (0, 1))
    cmp_v = _compress_weights(cmp_pos_v, cmp_v_w1, cmp_v_w2, (0, 1, 1, 0))
    h = x.reshape(b * s, d)
    for l in range(depth):
        h = _ffn(h, l, ffn1_norm, *ffn1_w, final_norm, False)
        o_nsa, o_hg = _mixer(h, l, b, s, mix_norm, w_ext, cmp_k, cmp_v, hgrn_lower_bound, hgrn_out_norm)
        h = _ffn(h, l, ffn2_norm, *ffn2_w, final_norm, l == depth - 1, (o_nsa, o_hg, w_out))
    return h.reshape(b, s, d)
```

```python
import functools

import jax
import jax.numpy as jnp
import numpy as np
from jax import lax
from jax.experimental import pallas as pl
from jax.experimental.pallas import tpu as pltpu

F32 = jnp.float32
BF16 = jnp.bfloat16

D_MODEL = 1024
EPS = 1e-6
NEG = -1e30
F_FLOOR = 1e-30
NSA_HEADS = 8
NSA_KV_GROUPS = 2
HEADS_PER_GROUP = NSA_HEADS // NSA_KV_GROUPS
NSA_HD = 64
CMP_LEN = 32
CMP_STRIDE = 16
CMP_HID = 256
SEL_BLOCK = 64
SEL_TOPK = 16
WINDOW = 512
HG_HEADS = 4
HG_DK = 128
HG_DV = 128
HG_CHUNK = 64
D_FF = 2752
NSA_WIDTH = NSA_HEADS * NSA_HD
HG_WIDTH = HG_HEADS * HG_DV
KV_W = NSA_KV_GROUPS * NSA_HD

LANES = 128
SUBLANES = 8
MXU_COLS = 256
FF_CHUNK = 256
D_FF_PAD = -(-D_FF // FF_CHUNK) * FF_CHUNK
ROW_TILE = 512
FFN_ROWS = 1024
ATT_TQ = 256
ATT_TK = 256
CMP_TQ = 1024
VMEM_LIMIT = 56 * 1024 * 1024

SEL_PENALTY = 2.0 ** 50
ALIBI_SLOPES = tuple(2.0 ** (-8.0 * (i + 1) / NSA_HEADS) for i in range(NSA_HEADS))

SEGMENTS = (
    ("q", NSA_HEADS * LANES, BF16, NSA_WIDTH),
    ("ks", NSA_KV_GROUPS * LANES, BF16, KV_W),
    ("vs", NSA_KV_GROUPS * LANES, BF16, KV_W),
    ("kw", NSA_KV_GROUPS * LANES, BF16, KV_W),
    ("vw", NSA_KV_GROUPS * LANES, BF16, KV_W),
    ("kc", KV_W, F32, KV_W),
    ("vc", KV_W, F32, KV_W),
    ("gl", LANES, F32, LANES),
    ("hg", HG_WIDTH, BF16, 4 * HG_WIDTH),
)
SEG_OFFSETS = tuple(int(v) for v in np.cumsum([0] + [s[3] for s in SEGMENTS]))


def _nn(a, b):
    return jnp.dot(a, b, preferred_element_type=F32)


def _nt(a, b):
    return lax.dot_general(a, b, (((1,), (1,)), ((), ())), preferred_element_type=F32)


def _tn(a, b):
    return lax.dot_general(a, b, (((0,), (0,)), ((), ())), preferred_element_type=F32)


def _split2(x):
    hi = x.astype(BF16)
    lo = (x - hi.astype(F32)).astype(BF16)
    return hi, lo


def _split3(x):
    hi = x.astype(BF16)
    r = x - hi.astype(F32)
    mid = r.astype(BF16)
    lo = (r - mid.astype(F32)).astype(BF16)
    return hi, mid, lo


def _sigmoid(x):
    return 1.0 / (1.0 + jnp.exp(-x))


def _rms(x, g):
    return x * lax.rsqrt(jnp.mean(x * x, axis=-1, keepdims=True) + EPS) * g


def _resident(shape):
    nd = len(shape)
    return pl.BlockSpec(shape, lambda *_: (0,) * nd, pipeline_mode=pl.Buffered(1))


def _resident_layer(stacked_shape, layer):
    nd = len(stacked_shape)
    return pl.BlockSpec((None,) + tuple(stacked_shape[1:]), lambda *_: (layer,) + (0,) * (nd - 1),
                        pipeline_mode=pl.Buffered(1))


def _params(sem):
    return pltpu.CompilerParams(dimension_semantics=sem, vmem_limit_bytes=VMEM_LIMIT)


def _ffn_kernel(*refs, final, proj):
    if proj:
        x_ref, a_ref, b_ref, wo_ref, g_ref, wg_ref, wu_ref, wd_ref, gf_ref, o_ref = refs
        x = x_ref[...] + _nn(a_ref[...], wo_ref[:NSA_WIDTH, :]) + _nn(b_ref[...], wo_ref[NSA_WIDTH:, :])
    else:
        x_ref, g_ref, wg_ref, wu_ref, wd_ref, gf_ref, o_ref = refs
        x = x_ref[...]
    xn = _rms(x, g_ref[...]).astype(BF16)
    acc = jnp.zeros(x.shape, F32)
    for k in range(D_FF_PAD // FF_CHUNK):
        sl = slice(k * FF_CHUNK, (k + 1) * FF_CHUNK)
        gate = _nn(xn, wg_ref[:, sl])
        up = _nn(xn, wu_ref[:, sl])
        h = (gate * _sigmoid(gate) * up).astype(BF16)
        acc = acc + _nn(h, wd_ref[sl, :])
    y = x + 0.5 * acc
    if final:
        y = _rms(y, gf_ref[...])
    o_ref[...] = y


def _ffn_weights(w_gu, w_down):
    pad = D_FF_PAD - D_FF
    wg = jnp.pad(w_gu[..., :D_FF], ((0, 0), (0, 0), (0, pad))).astype(BF16)
    wu = jnp.pad(w_gu[..., D_FF:], ((0, 0), (0, 0), (0, pad))).astype(BF16)
    wd = jnp.pad(w_down, ((0, 0), (0, pad), (0, 0))).astype(BF16)
    return wg, wu, wd


def _ffn(h, layer, norm_g, wg, wu, wd, final_g, final, proj=None):
    t = h.shape[0]
    row = lambda w: pl.BlockSpec((FFN_ROWS, w), lambda i: (i, 0))
    norm_g = norm_g[:, None, :]
    weights = [_resident_layer(a.shape, layer) for a in (norm_g, wg, wu, wd)] + [_resident((1, D_MODEL))]
    operands = [norm_g, wg, wu, wd, final_g.reshape(1, -1)]
    if proj is None:
        in_specs, args = [row(D_MODEL)] + weights, [h] + operands
    else:
        o_nsa, o_hg, w_out = proj
        in_specs = [row(D_MODEL), row(NSA_WIDTH), row(HG_WIDTH), _resident_layer(w_out.shape, layer)] + weights
        args = [h, o_nsa, o_hg, w_out] + operands
    return pl.pallas_call(
        functools.partial(_ffn_kernel, final=final, proj=proj is not None),
        grid=(t // FFN_ROWS,),
        in_specs=in_specs,
        out_specs=row(D_MODEL),
        out_shape=jax.ShapeDtypeStruct((t, D_MODEL), F32),
        compiler_params=_params(("parallel",)),
        name="ffn",
    )(*args)


def _inproj_kernel(x_ref, g_ref, w_ref, qfeat_ref, lbraw_ref, onorm_ref, *refs, seq, layer):
    *o_refs, xn_scr, y_scr, st_scr = refs
    out = {seg[0]: o_ref for seg, o_ref in zip(SEGMENTS, o_refs, strict=True)}
    col0 = {seg[0]: off for seg, off in zip(SEGMENTS, SEG_OFFSETS[:-1], strict=True)}
    rows = x_ref.shape[0]
    half = rows // 2
    tile = pl.program_id(0) % (seq // rows)

    @pl.when(tile == 0)
    def _():
        st_scr[...] = jnp.zeros(st_scr.shape, F32)

    def norm_piece(r):
        rs = slice(r * half, (r + 1) * half)
        xn_scr[rs, :] = _rms(x_ref[rs, :], g_ref[...]).astype(BF16)

    lane = lax.broadcasted_iota(jnp.int32, (half, LANES), 1)
    lower = lane < NSA_HD
    swap = lambda a: pltpu.roll(a, NSA_HD, axis=1)
    halves = [NSA_HD * (1 - g) for g in range(NSA_KV_GROUPS)]
    project = lambda rs, c0, cols: _nn(xn_scr[rs, :], w_ref[:, c0:c0 + cols])

    def q_piece(r, cb):
        def emit():
            rs = slice(r * half, (r + 1) * half)
            y = project(rs, col0["q"] + cb * MXU_COLS, MXU_COLS)
            per = MXU_COLS // NSA_HD
            blocks = []
            for h in range(cb * per, (cb + 1) * per):
                pair = y[:, (h % per // 2) * LANES:(h % per // 2 + 1) * LANES]
                g = h // HEADS_PER_GROUP
                data = pair if h % 2 == g else swap(pair)
                feat = qfeat_ref[:, h * LANES:(h + 1) * LANES]
                blocks.append(jnp.where(lower == (g == 0), data, feat))
            out["q"][rs, cb * per * LANES:(cb + 1) * per * LANES] = (
                jnp.concatenate(blocks, axis=1).astype(out["q"].dtype))
        return emit

    small = [seg[0] for seg in SEGMENTS if seg[3] == LANES]
    assert all(col0[b] - col0[a] == LANES for a, b in zip(small, small[1:]))

    def small_piece(r, names):
        def emit():
            rs = slice(r * half, (r + 1) * half)
            y_small = project(rs, col0[names[0]], len(names) * LANES)
            kpos = tile * rows + r * half + lax.broadcasted_iota(jnp.int32, (half, LANES), 0)
            blk = kpos // SEL_BLOCK
            digit = lambda f0: jnp.where(lane == f0, blk.astype(F32),
                                         jnp.where(lane == f0 + 1, (kpos % SEL_BLOCK).astype(F32), 0.0))
            is_pen = lambda f0: (lane == f0 + 1 + blk) & (blk >= 1) & (blk <= seq // SEL_BLOCK - 2)
            for n, name in enumerate(names):
                y = y_small[:, n * LANES:(n + 1) * LANES]
                if name in ("ks", "kw"):
                    feat = [digit(f0) + jnp.where(is_pen(f0), -SEL_PENALTY, 0.0) if name == "ks" else digit(f0)
                            for f0 in halves]
                    y = jnp.concatenate([jnp.where(lower == (g == 0), y, feat[g])
                                         for g in range(NSA_KV_GROUPS)], axis=1)
                elif name in ("vs", "vw"):
                    y = jnp.concatenate([jnp.where(lower, y if g == 0 else swap(y), 1.0)
                                         for g in range(NSA_KV_GROUPS)], axis=1)
                out[name][rs, :] = y.astype(out[name].dtype)
        return emit

    def hg_piece(r, cb):
        def emit():
            rs = slice(r * half, (r + 1) * half)
            y_scr[rs, cb * MXU_COLS:(cb + 1) * MXU_COLS] = project(rs, col0["hg"] + cb * MXU_COLS, MXU_COLS)
        return emit

    prepare, finish, n_slots = _hgrn_chunks(lbraw_ref[...], onorm_ref[...], layer)
    n_chunks = rows // HG_CHUNK
    chunk_rows = lambda ci: slice(ci * HG_CHUNK, (ci + 1) * HG_CHUNK)
    hg_tiles = 4 * HG_WIDTH // MXU_COLS
    norm_piece(0)
    hg_piece(0, 0)()
    norm_piece(1)
    for cb in range(1, hg_tiles):
        hg_piece(0, cb)()
    later = [hg_piece(1, cb) for cb in range(hg_tiles)]
    n_early = n_chunks // 2 - 1
    rest = [p for r in range(2) for p in (q_piece(r, 0), q_piece(r, 1), small_piece(r, small[0:2]),
                                          small_piece(r, small[2:4]), small_piece(r, small[4:]))]
    spread = lambda items, n: [items[i * len(items) // n:(i + 1) * len(items) // n] for i in range(n)]
    tiles = spread(later, n_early) + spread(rest, n_chunks - n_early)
    last = tiles[-1].pop()
    prepared = prepare(y_scr, chunk_rows(0))
    for ci in range(n_chunks):
        slots = [[] for _ in range(n_slots)]
        for j, emit in enumerate(tiles[ci]):
            slots[j * n_slots // len(tiles[ci])].append(emit)
        following = []
        if ci + 1 < n_chunks:
            slots[1].append(lambda ci=ci: following.append(prepare(y_scr, chunk_rows(ci + 1))))
        out["hg"][chunk_rows(ci), :] = finish(prepared, y_scr, chunk_rows(ci), st_scr, slots)
        prepared = following[0] if following else None
    last()


def _q_features():
    feat = np.zeros((1, NSA_HEADS * LANES), np.float32)
    for h in range(NSA_HEADS):
        f0 = h * LANES + NSA_HD * (1 - h // HEADS_PER_GROUP)
        feat[0, f0] = SEL_BLOCK * ALIBI_SLOPES[h]
        feat[0, f0 + 1] = ALIBI_SLOPES[h]
    return jnp.asarray(feat)


def _build_w_in(w_in):
    sizes = (NSA_WIDTH, KV_W, KV_W, KV_W, KV_W, KV_W, KV_W, NSA_HEADS * 3,
             HG_WIDTH, HG_WIDTH, HG_WIDTH, HG_WIDTH)
    splits = [int(v) for v in np.cumsum(sizes)[:-1]]
    w_in = w_in.astype(BF16)
    wq, wkc, wvc, wks, wvs, wkw, wvw, wgl, whq, whf, whi, whg = jnp.split(w_in, splits, axis=-1)
    lead = w_in.shape[:-1]
    gl = jnp.swapaxes(wgl.reshape(lead + (NSA_HEADS, 3)), -1, -2).reshape(lead + (3 * NSA_HEADS,))
    gl = jnp.pad(gl, [(0, 0)] * len(lead) + [(0, LANES - 3 * NSA_HEADS)])
    cols = [wq * NSA_HD ** -0.5, wks, wvs, wkw, wvw, wkc, wvc, gl, whq, whf, whi, whg]
    return jnp.concatenate(cols, axis=-1)


def _inproj(h, norm_g, w_ext, lb_raw, out_norm, seq, layer):
    t = h.shape[0]
    assert seq % ROW_TILE == 0 and ROW_TILE % HG_CHUNK == 0
    row = lambda w: pl.BlockSpec((ROW_TILE, w), lambda i: (i, 0))
    norm_g, out_norm = norm_g[:, None, :], out_norm[:, None, :]
    return pl.pallas_call(
        functools.partial(_inproj_kernel, seq=seq, layer=layer),
        grid=(t // ROW_TILE,),
        in_specs=[row(D_MODEL), _resident_layer(norm_g.shape, layer), _resident_layer(w_ext.shape, layer),
                  _resident((1, NSA_HEADS * LANES)), _resident(lb_raw.shape),
                  _resident_layer(out_norm.shape, layer)],
        out_specs=[row(w) for _, w, _, _ in SEGMENTS],
        out_shape=[jax.ShapeDtypeStruct((t, w), dt) for _, w, dt, _ in SEGMENTS],
        scratch_shapes=[pltpu.VMEM((ROW_TILE, D_MODEL), BF16), pltpu.VMEM((ROW_TILE, 4 * HG_WIDTH), F32),
                        pltpu.VMEM((HG_HEADS, HG_DV, HG_DK), F32)],
        compiler_params=_params(("arbitrary",)),
        name="inproj_hgrn",
    )(h, norm_g, w_ext, _q_features(), lb_raw, out_norm)


def _gelu_tanh(x):
    return 0.5 * x * (1.0 + jnp.tanh(0.7978845608028654 * (x + 0.044715 * (x * x * x))))


def _compress_kernel(kv_ref, pos_ref, w1_ref, w2_ref, o_ref):
    nbp = o_ref.shape[1]
    tokens = [kv_ref[0, pl.ds(l, nbp, stride=CMP_STRIDE), :] for l in range(CMP_STRIDE)]
    chunk = lambda first: jnp.concatenate(
        [(tokens[l] + pos_ref[first + l:first + l + 1, :]).astype(BF16) for l in range(CMP_STRIDE)], axis=1)
    slab = lambda first: w1_ref[first:first + CMP_STRIDE].reshape(CMP_STRIDE * KV_W, NSA_KV_GROUPS * CMP_HID)
    ha = _nn(chunk(0), slab(0))
    hb = _nn(chunk(CMP_STRIDE), slab(CMP_STRIDE))
    act = _gelu_tanh(ha + pltpu.roll(hb, nbp - 1, axis=0)).astype(BF16)
    out = jnp.zeros(o_ref.shape[1:], F32)
    for g in range(NSA_KV_GROUPS):
        out = out + _nn(act[:, g * CMP_HID:(g + 1) * CMP_HID], w2_ref[g])
    o_ref[0] = out


def _compress_weights(pos, w1, w2, reps):
    layers = w1.shape[0]
    w1l = w1.astype(BF16).reshape(layers, CMP_LEN, NSA_HD, CMP_HID)
    z1 = jnp.zeros_like(w1l)
    w1p = jnp.concatenate([jnp.concatenate([w1l, z1], axis=3), jnp.concatenate([z1, w1l], axis=3)], axis=2)
    pos2 = jnp.concatenate([pos] * NSA_KV_GROUPS, axis=2)
    w2 = w2.astype(BF16)
    zero = jnp.zeros_like(w2)
    w2p = jnp.stack([jnp.concatenate([w2 if r == g else zero for r in reps], axis=2)
                     for g in range(NSA_KV_GROUPS)], axis=1)
    return pos2, w1p, w2p


def _compress(kv, layer, pos2, w1p, w2p):
    b, s, _ = kv.shape
    nbp = s // CMP_STRIDE
    width = w2p.shape[-1]
    return pl.pallas_call(
        _compress_kernel,
        grid=(b,),
        in_specs=[pl.BlockSpec((1, s, KV_W), lambda i: (i, 0, 0))]
        + [_resident_layer(a.shape, layer) for a in (pos2, w1p, w2p)],
        out_specs=pl.BlockSpec((1, nbp, width), lambda i: (i, 0, 0)),
        out_shape=jax.ShapeDtypeStruct((b, nbp, width), F32),
        compiler_params=_params(("parallel",)),
        name="compress",
    )(kv, pos2, w1p, w2p)


def _pair_blocks(g, jj):
    lower = slice(0, LANES) if g == 0 else slice(LANES, 2 * LANES)
    upper = slice(LANES, 2 * LANES) if g == 0 else slice(0, LANES)
    return lower, upper


def _cmp_kernel(*refs, tq, ns, n_sel):
    total = refs[1].shape[1]
    per_tile = tq // CMP_STRIDE
    for t in range(total // per_tile):
        nbp = min(total, ((t + 1) * per_tile + LANES - 1) // LANES * LANES)
        sel_rows = min(ns, (t + 1) * tq // SEL_BLOCK)
        rounds = 0 if sel_rows <= n_sel else n_sel - 3

        @pl.when(pl.program_id(1) == t)
        def _(nbp=nbp, sel_rows=sel_rows, rounds=rounds):
            _cmp_tile(*refs, tq=tq, ns=ns, nbp=nbp, sel_rows=sel_rows, rounds=rounds)


def _cmp_tile(q_ref, kc_ref, vc2_ref, mt_ref, ocmp_ref, sel_ref, picks_ref, *, tq, ns, nbp, sel_rows, rounds):
    q0 = pl.program_id(1) * tq
    pos = q0 + lax.broadcasted_iota(jnp.int32, (tq, nbp), 0)
    blk_end = lax.broadcasted_iota(jnp.int32, (tq, nbp), 1) * CMP_STRIDE + (CMP_LEN - 1)
    valid = blk_end <= pos
    row_ok = (q0 + lax.broadcasted_iota(jnp.int32, (tq, 1), 0)) >= CMP_LEN - 1
    kc = kc_ref[0, :nbp, :].astype(BF16)
    vc2 = vc2_ref[0, :nbp, :].astype(BF16)
    lane_c = lax.broadcasted_iota(jnp.int32, (nbp, LANES), 1)
    c_idx = lax.broadcasted_iota(jnp.int32, (nbp, LANES), 0)
    per = SEL_BLOCK // CMP_STRIDE
    feat_a = (c_idx // per - q0 // SEL_BLOCK).astype(F32)
    feat_b = ((c_idx % per) * CMP_STRIDE + (CMP_LEN - 1)).astype(F32)
    lane_lo = lax.broadcasted_iota(jnp.int32, (tq, LANES), 1) < NSA_HD
    blk = lax.broadcasted_iota(jnp.int32, (ns, tq), 0)
    blk_r = lax.broadcasted_iota(jnp.int32, (sel_rows, tq), 0)
    blk_f = blk_r.astype(F32)
    pos_t = q0 + lax.broadcasted_iota(jnp.int32, (sel_rows, tq), 1)
    cur = pos_t // SEL_BLOCK
    forced = (blk_r == 0) | (blk_r == cur) | (blk_r == cur - 1)
    causal = blk_r * SEL_BLOCK <= pos_t
    sel_t, picks = [], []
    taken = -jnp.inf
    for g in range(NSA_KV_GROUPS):
        f0 = NSA_HD * (1 - g)
        feat = jnp.where(lane_c == f0, feat_a, jnp.where(lane_c == f0 + 1, feat_b, 0.0))
        kc_g = jnp.where((lane_c // NSA_HD) == g, kc, feat.astype(BF16))
        imp = jnp.zeros((tq, nbp), F32)
        acc = []
        for hh in range(HEADS_PER_GROUP):
            h = g * HEADS_PER_GROUP + hh
            s = _nt(q_ref[0, :, h * LANES:(h + 1) * LANES], kc_g)
            s = jnp.where(valid, s, NEG)
            e = jnp.exp(s - jnp.max(s, axis=-1, keepdims=True))
            inv = jnp.where(row_ok, 1.0 / jnp.sum(e, axis=-1, keepdims=True), 0.0)
            p = e * inv
            imp = imp + p
            acc.append(_nn(p.astype(BF16), vc2))
        for jj in range(HEADS_PER_GROUP // 2):
            lower, upper = _pair_blocks(g, jj)
            blk_out = jnp.where(lane_lo, acc[2 * jj][:, lower], acc[2 * jj + 1][:, upper])
            c0 = (g * HEADS_PER_GROUP + 2 * jj) * NSA_HD
            ocmp_ref[0, :, c0:c0 + LANES] = blk_out
        p_slc = sum(_nt(mt_ref[:sel_rows, :nbp], part) for part in _split3(imp))
        if rounds == 0:
            score = jnp.where(forced | causal, taken, NEG)
        else:
            score = jnp.where(forced, taken, jnp.where(causal, p_slc, NEG))
        for _ in range(rounds):
            top = jnp.max(score, axis=0, keepdims=True)
            first = jnp.min(jnp.where(score == top, blk_f, float(ns)), axis=0, keepdims=True)
            score = jnp.where(blk_f == first, taken, score)
        if sel_rows < ns:
            score = jnp.concatenate([score, jnp.full((ns - sel_rows, tq), NEG, F32)], axis=0)
        picked = jnp.where(score == taken, 1.0, 0.0).astype(BF16)
        picks.append([_nt(jnp.ones((SUBLANES, ATT_TQ), BF16), picked[:, t * ATT_TQ:(t + 1) * ATT_TQ])
                      for t in range(tq // ATT_TQ)])
        pen = jnp.where(score == taken, 0.0, SEL_PENALTY)
        pen = jnp.where((blk >= 2) & (blk <= ns - 1), pltpu.roll(pen, 1, axis=0), 0.0)
        if ns < NSA_HD:
            pen = jnp.concatenate([pen, jnp.zeros((NSA_HD - ns, tq), F32)], axis=0)
        zero = jnp.zeros((NSA_HD, tq), F32)
        sel_t += [zero, pen] if g == 0 else [pen, zero]
    sel_ref[0] = jnp.concatenate(sel_t, axis=0).T.astype(BF16)
    for t in range(tq // ATT_TQ):
        picks_ref[0, t] = jnp.concatenate([group[t] for group in picks], axis=1)


def _importance_matrix(nbp, ns):
    per = SEL_BLOCK // CMP_STRIDE
    m = np.zeros((ns, nbp), np.float32)
    for n in range(ns):
        for c in range(per * n, per * (n + 1)):
            for cc in (c - 1, c):
                if 0 <= cc < nbp - 1:
                    m[n, cc] += 1.0
    return jnp.asarray(m, BF16)


def _cmp_attention(q, kc, vc2, b, s):
    nbp = s // CMP_STRIDE
    ns = s // SEL_BLOCK
    tq = min(CMP_TQ, s)
    assert ns <= NSA_HD
    assert min(SEL_TOPK, ns) >= 3 and tq % ATT_TQ == 0 and tq % (SUBLANES * SEL_BLOCK) == 0
    q3 = q.reshape(b, s, NSA_HEADS * LANES)
    kern = functools.partial(_cmp_kernel, tq=tq, ns=ns, n_sel=min(SEL_TOPK, ns))
    return pl.pallas_call(
        kern,
        grid=(b, s // tq),
        in_specs=[pl.BlockSpec((1, tq, NSA_HEADS * LANES), lambda i, j: (i, j, 0)),
                  pl.BlockSpec((1, nbp, KV_W), lambda i, j: (i, 0, 0)),
                  pl.BlockSpec((1, nbp, 2 * KV_W), lambda i, j: (i, 0, 0)),
                  _resident((ns, nbp))],
        out_specs=[pl.BlockSpec((1, tq, NSA_WIDTH), lambda i, j: (i, j, 0)),
                   pl.BlockSpec((1, tq, NSA_KV_GROUPS * LANES), lambda i, j: (i, j, 0)),
                   pl.BlockSpec((1, tq // ATT_TQ, SUBLANES, NSA_KV_GROUPS * ns), lambda i, j: (i, j, 0, 0))],
        out_shape=[jax.ShapeDtypeStruct((b, s, NSA_WIDTH), F32),
                   jax.ShapeDtypeStruct((b, s, NSA_KV_GROUPS * LANES), BF16),
                   jax.ShapeDtypeStruct((b, s // ATT_TQ, SUBLANES, NSA_KV_GROUPS * ns), F32)],
        compiler_params=_params(("parallel", "parallel")),
        name="cmp_topk",
    )(q3, kc, vc2, _importance_matrix(nbp, ns))


def _attn_kernel(tiles_ref, ntiles_ref, q_ref, ks_ref, vs_ref, kw_ref, vw_ref, penq_ref, gl_ref, gexp_ref,
                 ocmp_ref, o_ref, qa_scr, s_scr, p_scr, al_scr, m_scr, acc_scr, *, tq, tk):
    i = pl.program_id(2)
    rows_of = lambda hh: slice(hh * tq, (hh + 1) * tq)
    lane_lo = lax.broadcasted_iota(jnp.int32, (tq, LANES), 1) < NSA_HD
    rel = (lax.broadcasted_iota(jnp.int32, (tq, tk), 1)
           - lax.broadcasted_iota(jnp.int32, (tq, tk), 0)).astype(F32)

    for hh in range(HEADS_PER_GROUP):
        qa_scr[rows_of(hh)] = q_ref[0, :, hh * LANES:(hh + 1) * LANES] + penq_ref[0]

    def scores(k_ref, j, slot):
        s_scr[slot] = _nt(qa_scr[...], k_ref[0, pl.ds(pl.multiple_of(j * tk, tk), tk), :])

    def probs(br, j, mode, slot, first, exists=None):
        for hh in range(HEADS_PER_GROUP):
            r = rows_of(hh)
            s = s_scr[slot, r]
            if mode == "causal":
                s = jnp.where(rel <= 0.0, s, NEG)
            elif mode == "lower":
                s = jnp.where(rel + (j * tk - i * tq).astype(F32) > -float(WINDOW), s, NEG)
            if exists is not None:
                s = jnp.where(exists, s, NEG)
            m_cur = jnp.max(s, axis=-1, keepdims=True)
            if first:
                m_new = jnp.broadcast_to(m_cur, (tq, LANES))
            else:
                m_prev = m_scr[br, r]
                m_new = jnp.maximum(m_prev, m_cur)
                al_scr[slot, r] = jnp.exp(m_prev - m_new)
            m_scr[br, r] = m_new
            p_scr[slot, r] = jnp.exp(s - jnp.concatenate([m_new] * (tk // LANES), axis=1)).astype(BF16)

    def accumulate(br, v_ref, j, slot, first):
        pv = _nn(p_scr[slot], v_ref[0, pl.ds(pl.multiple_of(j * tk, tk), tk), :])
        for hh in range(HEADS_PER_GROUP):
            r = rows_of(hh)
            if first:
                acc_scr[br, r] = pv[r]
            else:
                acc_scr[br, r] = al_scr[slot, r] * acc_scr[br, r] + pv[r]

    n_q = pl.num_programs(2)
    step = (pl.program_id(0) * n_q + i) * pl.num_programs(1) + pl.program_id(1)
    n_tiles = ntiles_ref[step]
    item = lambda br, j, mode, first, exists=None: (br, j, mode, first, exists)
    sel_item = lambda n: item(0, i, "causal", True) if isinstance(n, int) and n == 0 else item(
        0, tiles_ref[step * n_q + n - 1], None, False)
    n_back = WINDOW // tk
    win_items = [item(1, i, "causal", True)] + [
        item(1, jnp.maximum(i - d, 0), "lower" if d == n_back else None, False, i >= d)
        for d in range(1, n_back + 1)]
    k_refs, v_refs = (ks_ref, kw_ref), (vs_ref, vw_ref)

    def run(items, slot0, done=(0, 0, 0), upto=None):
        n_items = len(items)
        slot = lambda n: (slot0 + n) % 2
        n_s, n_p, n_a = done

        def do_scores(n):
            br, j, _, _, _ = items[n]
            scores(k_refs[br], j, slot(n))

        def do_probs(n):
            br, j, mode, first, exists = items[n]
            probs(br, j, mode, slot(n), first, exists)

        def do_acc(n):
            br, j, _, first, _ = items[n]
            accumulate(br, v_refs[br], j, slot(n), first)

        for n in range(n_s, min(n_a + 2, n_items)):
            do_scores(n)
        n_s = max(n_s, min(n_a + 2, n_items))
        for n in range(n_p, min(n_a + 1, n_items)):
            do_probs(n)
        n_p = max(n_p, min(n_a + 1, n_items))
        for k in range(n_a, n_items if upto is None else upto):
            if n_s <= k + 2 < n_items:
                do_scores(k + 2)
                n_s = k + 3
            if n_p <= k + 1 < n_items:
                do_probs(k + 1)
                n_p = k + 2
            do_acc(k)

    for n_static in range(2):
        @pl.when(n_tiles == n_static)
        def _(n_static=n_static):
            run([sel_item(n) for n in range(n_static + 1)] + win_items, 0)

    @pl.when(n_tiles >= 2)
    def _():
        run([sel_item(n) for n in range(3)], 0, upto=1)

        def sel_step(n, parity):
            scores(ks_ref, sel_item(n + 2)[1], parity)
            probs(0, None, None, 1 - parity, False)
            accumulate(0, vs_ref, sel_item(n)[1], parity, False)

        def sel_two_steps(t, carry):
            sel_step(2 * t + 1, 1)
            sel_step(2 * t + 2, 0)
            return carry

        lax.fori_loop(0, (n_tiles - 2) // 2, sel_two_steps, 0)

        def drain(parity):
            tail = [sel_item(n_tiles - 1), sel_item(n_tiles)] + win_items
            run(tail, 1 - parity, done=(2, 1, 0))

        @pl.when(n_tiles % 2 == 0)
        def _():
            drain(0)

        @pl.when(n_tiles % 2 == 1)
        def _():
            sel_step(n_tiles - 2, 1)
            drain(1)

    gexp = sum(_nn(part, gexp_ref[0]) for part in _split2(_sigmoid(gl_ref[0])))
    n_pairs = HEADS_PER_GROUP // 2
    gate_blk = lambda br, jj: gexp[:, (br * n_pairs + jj) * LANES:(br * n_pairs + jj + 1) * LANES]
    for jj in range(n_pairs):
        cols = slice(jj * LANES, (jj + 1) * LANES)
        blk = gate_blk(0, jj) * ocmp_ref[0, :, cols]
        for br in range(2):
            even = acc_scr[br, rows_of(2 * jj)]
            odd = acc_scr[br, rows_of(2 * jj + 1)]
            low = even / pltpu.roll(even, NSA_HD, axis=1)
            up = pltpu.roll(odd, NSA_HD, axis=1) / odd
            blk = blk + gate_blk(br + 1, jj) * jnp.where(lane_lo, low, up)
        o_ref[0, :, cols] = blk.astype(o_ref.dtype)


def _gate_expansion():
    n_pairs = HEADS_PER_GROUP // 2
    r = np.zeros((NSA_KV_GROUPS, LANES, 3 * n_pairs * LANES), np.float32)
    for g in range(NSA_KV_GROUPS):
        for br in range(3):
            for jj in range(n_pairs):
                for odd in range(2):
                    src = br * NSA_HEADS + g * HEADS_PER_GROUP + 2 * jj + odd
                    dst = (br * n_pairs + jj) * LANES + odd * NSA_HD
                    r[g, src, dst:dst + NSA_HD] = 1.0
    return jnp.asarray(r, BF16)


def _picked_tiles(picks, b, s):
    ns, nq = s // SEL_BLOCK, s // ATT_TQ
    per_tile = ATT_TK // SEL_BLOCK
    per_block = picks[:, :, 0, :].reshape(b, nq, NSA_KV_GROUPS, ns // per_tile, per_tile)
    j = jnp.arange(ns // per_tile, dtype=jnp.int32)
    i = jnp.arange(nq, dtype=jnp.int32)[None, :, None, None]
    active = ((per_block.sum(-1) > 0) | (j == 0)) & (j < i)
    slot = jnp.cumsum(active, axis=-1) - 1
    hit = active[..., None, :] & (slot[..., None, :] == j[:, None])
    tiles = jnp.sum(jnp.where(hit, j, 0), axis=-1).astype(jnp.int32)
    return tiles.reshape(-1), active.sum(-1).astype(jnp.int32).reshape(-1)


def _attention(q, ks, vs, kw, vw, pen, picks, gl, ocmp, b, s):
    tq, tk = ATT_TQ, ATT_TK
    assert tq == tk and WINDOW % tk == 0 and WINDOW // tk <= 2 and tq % SEL_BLOCK == 0
    gw = HEADS_PER_GROUP * LANES
    ow = HEADS_PER_GROUP * NSA_HD
    r3 = lambda a: a.reshape(b, s, a.shape[-1])
    tile = lambda w: pl.BlockSpec((1, tq, w), lambda i, g, j, *_: (i, j, 0))
    gtile = lambda w: pl.BlockSpec((1, tq, w), lambda i, g, j, *_: (i, j, g))
    gfull = pl.BlockSpec((1, s, LANES), lambda i, g, j, *_: (i, 0, g))
    gexp = _gate_expansion()
    rows = HEADS_PER_GROUP * tq
    kern = functools.partial(_attn_kernel, tq=tq, tk=tk)
    tiles, n_tiles = _picked_tiles(picks, b, s)
    grid_spec = pltpu.PrefetchScalarGridSpec(
        num_scalar_prefetch=2,
        grid=(b, NSA_KV_GROUPS, s // tq),
        in_specs=[gtile(gw), gfull, gfull, gfull, gfull, gtile(LANES), tile(LANES),
                  pl.BlockSpec((1,) + gexp.shape[1:], lambda i, g, j, *_: (g, 0, 0)), gtile(ow)],
        out_specs=gtile(ow),
        scratch_shapes=[pltpu.VMEM((rows, LANES), BF16), pltpu.VMEM((2, rows, tk), F32),
                        pltpu.VMEM((2, rows, tk), BF16), pltpu.VMEM((2, rows, LANES), F32),
                        pltpu.VMEM((2, rows, LANES), F32), pltpu.VMEM((2, rows, LANES), F32)])
    return pl.pallas_call(
        kern,
        grid_spec=grid_spec,
        out_shape=jax.ShapeDtypeStruct((b, s, NSA_WIDTH), BF16),
        compiler_params=_params(("parallel", "parallel", "parallel")),
        name="sel_win_attention",
    )(tiles, n_tiles, r3(q), r3(ks), r3(vs), r3(kw), r3(vw), pen, r3(gl), gexp, ocmp)


def _hgrn_chunks(raw, onorm, layer):
    c = HG_CHUNK
    ex = jnp.exp(raw - jnp.max(raw, axis=0, keepdims=True))
    sm = ex / jnp.sum(ex, axis=0, keepdims=True)
    lb_all = jnp.zeros((1, raw.shape[1]), F32)
    for l in range(1, layer + 1):
        lb_all = lb_all + sm[l:l + 1, :]

    t_idx = lax.broadcasted_iota(jnp.int32, (c, HG_DK), 0)
    sub = lax.broadcasted_iota(jnp.int32, (SUBLANES, HG_DK), 0)
    ti = lax.broadcasted_iota(jnp.int32, (c, c), 0)
    si = lax.broadcasted_iota(jnp.int32, (c, c), 1)
    tril = jnp.where(si <= ti, 1.0, 0.0).astype(BF16)
    levels = (32, 16, 8, 4, 2, 1)
    hc = HG_HEADS * c
    tb = lax.broadcasted_iota(jnp.int32, (hc, hc), 0)
    sb = lax.broadcasted_iota(jnp.int32, (hc, hc), 1)
    same_head = (tb // c) == (sb // c)
    split_bit = tb ^ sb
    pair_mask = {m: jnp.where(same_head & (tb > sb) & (split_bit >= m) & (split_bit < 2 * m), 1.0, 0.0)
                 for m in levels}
    diagonal = tb == sb
    second_half = {m: (t_idx & m) != 0 for m in levels}
    sign = {m: jnp.where(second_half[m], 1.0, -1.0) for m in levels}

    def ref_rows(b, m):
        row = lambda r, n: jnp.broadcast_to(b[r:r + 1, :], (n, HG_DK))
        if m >= 4:
            return jnp.concatenate([row(s0 + m - 1, 2 * m) for s0 in range(0, c, 2 * m)], axis=0)
        return jnp.concatenate([jnp.where(sub < 4, row(s0 + 1, SUBLANES), row(s0 + 5, SUBLANES))
                                for s0 in range(0, c, SUBLANES)], axis=0)

    part = lambda y_ref, rows, which, h: y_ref[rows, which * HG_WIDTH + h * HG_DK:which * HG_WIDTH + (h + 1) * HG_DK]

    def prepare(y_ref, rows):
        q, k, f, v, logf = [], [], [], [], []
        for h in range(HG_HEADS):
            lb = lb_all[:, h * HG_DK:(h + 1) * HG_DK]
            z = part(y_ref, rows, 1, h)
            hq = part(y_ref, rows, 0, h)
            q.append(hq * _sigmoid(hq))
            ez = jnp.exp(-jnp.abs(z))
            big = 1.0 / (1.0 + ez)
            small = ez * big
            f.append(jnp.maximum(lb + (1.0 - lb) * jnp.where(z >= 0.0, big, small), F_FLOOR))
            logf.append(jnp.log2(f[h]))
            k.append((1.0 - lb) * jnp.where(z >= 0.0, small, big))
            v.append(part(y_ref, rows, 2, h).astype(BF16))
        parts = jnp.concatenate([p for h in range(HG_HEADS) for p in _split3(logf[h])], axis=1)
        csum = _nn(tril, parts)
        bcum = [sum(csum[:, (3 * h + i) * HG_DK:(3 * h + i + 1) * HG_DK] for i in range(3))
                for h in range(HG_HEADS)]
        return q, k, f, v, bcum

    def finish(prepared, y_ref, rows, st_scr, fill):
        q, k, f, v, bcum = prepared
        rowdot = jnp.concatenate([jnp.sum(q[h] * k[h], axis=-1, keepdims=True) for h in range(HG_HEADS)], axis=0)
        a = jnp.where(diagonal, rowdot, 0.0)
        for m, emits in zip(levels, fill, strict=True):
            for emit in emits:
                emit()
            r = []
            for h in range(HG_HEADS):
                if m == 1:
                    r.append(jnp.where(second_half[m], q[h] * f[h], k[h]))
                else:
                    w = jnp.exp2((bcum[h] - ref_rows(bcum[h], m)) * sign[m])
                    r.append(jnp.where(second_half[m], q[h], k[h]) * w)
            r = jnp.concatenate(r, axis=0).astype(BF16)
            a = a + _nt(r, r) * pair_mask[m]
        o_intra = _nn(a.astype(BF16), jnp.concatenate(v, axis=0))

        heads = []
        for h in range(HG_HEADS):
            st = st_scr[h]
            o = o_intra[h * c:(h + 1) * c] + _nt((q[h] * jnp.exp2(bcum[h])).astype(BF16), st.astype(BF16))
            b_last = bcum[h][c - 1:c, :]
            kd = (k[h] * jnp.exp2(b_last - bcum[h])).astype(BF16)
            st_scr[h] = jnp.exp2(b_last) * st + _tn(v[h], kd)

            o = o * lax.rsqrt(jnp.mean(o * o, axis=-1, keepdims=True) + EPS) * onorm
            gate = part(y_ref, rows, 3, h)
            heads.append((o * (gate * _sigmoid(gate))).astype(BF16))
        return jnp.concatenate(heads, axis=1)

    return prepare, finish, len(levels)


def _mixer(h, l, b, s, mix_norm, w_ext, cmp_k, cmp_v, hgrn_lower_bound, hgrn_out_norm):
    q, ks, vs, kw, vw, kc_in, vc_in, gl, o_hg = _inproj(h, mix_norm, w_ext, hgrn_lower_bound, hgrn_out_norm, s, l)
    kc = _compress(kc_in.reshape(b, s, KV_W), l, *cmp_k)
    vc2 = _compress(vc_in.reshape(b, s, KV_W), l, *cmp_v)
    ocmp, sel, picks = _cmp_attention(q, kc, vc2, b, s)
    o_nsa = _attention(q, ks, vs, kw, vw, sel, picks, gl, ocmp, b, s)
    return o_nsa.reshape(b * s, NSA_WIDTH), o_hg


def kernel(x, ffn1_norm, ffn1_w_gu, ffn1_w_down, mix_norm, w_in, cmp_pos_k, cmp_pos_v, cmp_k_w1, cmp_k_w2, cmp_v_w1, cmp_v_w2, hgrn_lower_bound, hgrn_out_norm, w_out, ffn2_norm, ffn2_w_gu, ffn2_w_down, final_norm):
    b, s, d = x.shape
    depth = ffn1_norm.shape[0]
    ffn1_w = _ffn_weights(ffn1_w_gu, ffn1_w_down)
    ffn2_w = _ffn_weights(ffn2_w_gu, ffn2_w_down)
    w_out = w_out.astype(BF16)
    w_ext = _build_w_in(w_in)
    cmp_k = _compress_weights(cmp_pos_k, cmp_k_w1, cmp_k_w2, (0, 1))
    cmp_v = _compress_weights(cmp_pos_v, cmp_v_w1, cmp_v_w2, (0, 1, 1, 0))
    h = x.reshape(b * s, d)
    for l in range(depth):
        h = _ffn(h, l, ffn1_norm, *ffn1_w, final_norm, False)
        o_nsa, o_hg = _mixer(h, l, b, s, mix_norm, w_ext, cmp_k, cmp_v, hgrn_lower_bound, hgrn_out_norm)
        h = _ffn(h, l, ffn2_norm, *ffn2_w, final_norm, l == depth - 1, (o_nsa, o_hg, w_out))
    return h.reshape(b, s, d)
```

```python
import functools

import jax
import jax.numpy as jnp
import numpy as np
from jax import lax
from jax.experimental import pallas as pl
from jax.experimental.pallas import tpu as pltpu

F32 = jnp.float32
BF16 = jnp.bfloat16

D_MODEL = 1024
EPS = 1e-6
NEG = -1e30
F_FLOOR = 1e-30
NSA_HEADS = 8
NSA_KV_GROUPS = 2
HEADS_PER_GROUP = NSA_HEADS // NSA_KV_GROUPS
NSA_HD = 64
CMP_LEN = 32
CMP_STRIDE = 16
CMP_HID = 256
SEL_BLOCK = 64
SEL_TOPK = 16
WINDOW = 512
HG_HEADS = 4
HG_DK = 128
HG_DV = 128
HG_CHUNK = 64
D_FF = 2752
NSA_WIDTH = NSA_HEADS * NSA_HD
HG_WIDTH = HG_HEADS * HG_DV
KV_W = NSA_KV_GROUPS * NSA_HD

LANES = 128
SUBLANES = 8
MXU_COLS = 256
FF_CHUNK = 256
D_FF_PAD = -(-D_FF // FF_CHUNK) * FF_CHUNK
ROW_TILE = 512
FFN_ROWS = 1024
ATT_TQ = 256
ATT_TK = 256
CMP_TQ = 1024
VMEM_LIMIT = 56 * 1024 * 1024

SEL_PENALTY = 2.0 ** 50
ALIBI_SLOPES = tuple(2.0 ** (-8.0 * (i + 1) / NSA_HEADS) for i in range(NSA_HEADS))

SEGMENTS = (
    ("q", NSA_HEADS * LANES, BF16, NSA_WIDTH),
    ("ks", NSA_KV_GROUPS * LANES, BF16, KV_W),
    ("vs", NSA_KV_GROUPS * LANES, BF16, KV_W),
    ("kw", NSA_KV_GROUPS * LANES, BF16, KV_W),
    ("vw", NSA_KV_GROUPS * LANES, BF16, KV_W),
    ("kc", KV_W, F32, KV_W),
    ("vc", KV_W, F32, KV_W),
    ("gl", LANES, F32, LANES),
    ("hg", HG_WIDTH, BF16, 4 * HG_WIDTH),
)
SEG_OFFSETS = tuple(int(v) for v in np.cumsum([0] + [s[3] for s in SEGMENTS]))


def _nn(a, b):
    return jnp.dot(a, b, preferred_element_type=F32)


def _nt(a, b):
    return lax.dot_general(a, b, (((1,), (1,)), ((), ())), preferred_element_type=F32)


def _tn(a, b):
    return lax.dot_general(a, b, (((0,), (0,)), ((), ())), preferred_element_type=F32)


def _split2(x):
    hi = x.astype(BF16)
    lo = (x - hi.astype(F32)).astype(BF16)
    return hi, lo


def _split3(x):
    hi = x.astype(BF16)
    r = x - hi.astype(F32)
    mid = r.astype(BF16)
    lo = (r - mid.astype(F32)).astype(BF16)
    return hi, mid, lo


def _sigmoid(x):
    return 1.0 / (1.0 + jnp.exp(-x))


def _rms(x, g):
    return x * lax.rsqrt(jnp.mean(x * x, axis=-1, keepdims=True) + EPS) * g


def _resident(shape):
    nd = len(shape)
    return pl.BlockSpec(shape, lambda *_: (0,) * nd, pipeline_mode=pl.Buffered(1))


def _resident_layer(stacked_shape, layer):
    nd = len(stacked_shape)
    return pl.BlockSpec((None,) + tuple(stacked_shape[1:]), lambda *_: (layer,) + (0,) * (nd - 1),
                        pipeline_mode=pl.Buffered(1))


def _params(sem):
    return pltpu.CompilerParams(dimension_semantics=sem, vmem_limit_bytes=VMEM_LIMIT)


def _ffn_kernel(*refs, final, proj):
    if proj:
        x_ref, a_ref, b_ref, wo_ref, g_ref, wg_ref, wu_ref, wd_ref, gf_ref, o_ref = refs
        x = x_ref[...] + _nn(a_ref[...], wo_ref[:NSA_WIDTH, :]) + _nn(b_ref[...], wo_ref[NSA_WIDTH:, :])
    else:
        x_ref, g_ref, wg_ref, wu_ref, wd_ref, gf_ref, o_ref = refs
        x = x_ref[...]
    xn = _rms(x, g_ref[...]).astype(BF16)
    acc = jnp.zeros(x.shape, F32)
    for k in range(D_FF_PAD // FF_CHUNK):
        sl = slice(k * FF_CHUNK, (k + 1) * FF_CHUNK)
        gate = _nn(xn, wg_ref[:, sl])
        up = _nn(xn, wu_ref[:, sl])
        h = (gate * _sigmoid(gate) * up).astype(BF16)
        acc = acc + _nn(h, wd_ref[sl, :])
    y = x + 0.5 * acc
    if final:
        y = _rms(y, gf_ref[...])
    o_ref[...] = y


def _ffn_weights(w_gu, w_down):
    pad = D_FF_PAD - D_FF
    wg = jnp.pad(w_gu[..., :D_FF], ((0, 0), (0, 0), (0, pad))).astype(BF16)
    wu = jnp.pad(w_gu[..., D_FF:], ((0, 0), (0, 0), (0, pad))).astype(BF16)
    wd = jnp.pad(w_down, ((0, 0), (0, pad), (0, 0))).astype(BF16)
    return wg, wu, wd


def _ffn(h, layer, norm_g, wg, wu, wd, final_g, final, proj=None):
    t = h.shape[0]
    row = lambda w: pl.BlockSpec((FFN_ROWS, w), lambda i: (i, 0))
    norm_g = norm_g[:, None, :]
    weights = [_resident_layer(a.shape, layer) for a in (norm_g, wg, wu, wd)] + [_resident((1, D_MODEL))]
    operands = [norm_g, wg, wu, wd, final_g.reshape(1, -1)]
    if proj is None:
        in_specs, args = [row(D_MODEL)] + weights, [h] + operands
    else:
        o_nsa, o_hg, w_out = proj
        in_specs = [row(D_MODEL), row(NSA_WIDTH), row(HG_WIDTH), _resident_layer(w_out.shape, layer)] + weights
        args = [h, o_nsa, o_hg, w_out] + operands
    return pl.pallas_call(
        functools.partial(_ffn_kernel, final=final, proj=proj is not None),
        grid=(t // FFN_ROWS,),
        in_specs=in_specs,
        out_specs=row(D_MODEL),
        out_shape=jax.ShapeDtypeStruct((t, D_MODEL), F32),
        compiler_params=_params(("parallel",)),
        name="ffn",
    )(*args)


def _inproj_kernel(x_ref, g_ref, w_ref, qfeat_ref, lbraw_ref, onorm_ref, *refs, seq, layer):
    *o_refs, xn_scr, y_scr, st_scr = refs
    out = {seg[0]: o_ref for seg, o_ref in zip(SEGMENTS, o_refs, strict=True)}
    col0 = {seg[0]: off for seg, off in zip(SEGMENTS, SEG_OFFSETS[:-1], strict=True)}
    rows = x_ref.shape[0]
    half = rows // 2
    tile = pl.program_id(0) % (seq // rows)

    @pl.when(tile == 0)
    def _():
        st_scr[...] = jnp.zeros(st_scr.shape, F32)

    def norm_piece(r):
        rs = slice(r * half, (r + 1) * half)
        xn_scr[rs, :] = _rms(x_ref[rs, :], g_ref[...]).astype(BF16)

    lane = lax.broadcasted_iota(jnp.int32, (half, LANES), 1)
    lower = lane < NSA_HD
    swap = lambda a: pltpu.roll(a, NSA_HD, axis=1)
    halves = [NSA_HD * (1 - g) for g in range(NSA_KV_GROUPS)]
    project = lambda rs, c0, cols: _nn(xn_scr[rs, :], w_ref[:, c0:c0 + cols])

    def q_piece(r, cb):
        def emit():
            rs = slice(r * half, (r + 1) * half)
            y = project(rs, col0["q"] + cb * MXU_COLS, MXU_COLS)
            per = MXU_COLS // NSA_HD
            blocks = []
            for h in range(cb * per, (cb + 1) * per):
                pair = y[:, (h % per // 2) * LANES:(h % per // 2 + 1) * LANES]
                g = h // HEADS_PER_GROUP
                data = pair if h % 2 == g else swap(pair)
                feat = qfeat_ref[:, h * LANES:(h + 1) * LANES]
                blocks.append(jnp.where(lower == (g == 0), data, feat))
            out["q"][rs, cb * per * LANES:(cb + 1) * per * LANES] = (
                jnp.concatenate(blocks, axis=1).astype(out["q"].dtype))
        return emit

    small = [seg[0] for seg in SEGMENTS if seg[3] == LANES]
    assert all(col0[b] - col0[a] == LANES for a, b in zip(small, small[1:]))

    def small_piece(r, names):
        def emit():
            rs = slice(r * half, (r + 1) * half)
            y_small = project(rs, col0[names[0]], len(names) * LANES)
            kpos = tile * rows + r * half + lax.broadcasted_iota(jnp.int32, (half, LANES), 0)
            blk = kpos // SEL_BLOCK
            digit = lambda f0: jnp.where(lane == f0, blk.astype(F32),
                                         jnp.where(lane == f0 + 1, (kpos % SEL_BLOCK).astype(F32), 0.0))
            is_pen = lambda f0: (lane == f0 + 1 + blk) & (blk >= 1) & (blk <= seq // SEL_BLOCK - 2)
            for n, name in enumerate(names):
                y = y_small[:, n * LANES:(n + 1) * LANES]
                if name in ("ks", "kw"):
                    feat = [digit(f0) + jnp.where(is_pen(f0), -SEL_PENALTY, 0.0) if name == "ks" else digit(f0)
                            for f0 in halves]
                    y = jnp.concatenate([jnp.where(lower == (g == 0), y, feat[g])
                                         for g in range(NSA_KV_GROUPS)], axis=1)
                elif name in ("vs", "vw"):
                    y = jnp.concatenate([jnp.where(lower, y if g == 0 else swap(y), 1.0)
                                         for g in range(NSA_KV_GROUPS)], axis=1)
                out[name][rs, :] = y.astype(out[name].dtype)
        return emit

    def hg_piece(r, cb):
        def emit():
            rs = slice(r * half, (r + 1) * half)
            y_scr[rs, cb * MXU_COLS:(cb + 1) * MXU_COLS] = project(rs, col0["hg"] + cb * MXU_COLS, MXU_COLS)
        return emit

    prepare, finish, n_slots = _hgrn_chunks(lbraw_ref[...], onorm_ref[...], layer)
    n_chunks = rows // HG_CHUNK
    chunk_rows = lambda ci: slice(ci * HG_CHUNK, (ci + 1) * HG_CHUNK)
    hg_tiles = 4 * HG_WIDTH // MXU_COLS
    norm_piece(0)
    hg_piece(0, 0)()
    norm_piece(1)
    for cb in range(1, hg_tiles):
        hg_piece(0, cb)()
    later = [hg_piece(1, cb) for cb in range(hg_tiles)]
    n_early = n_chunks // 2 - 1
    rest = [p for r in range(2) for p in (q_piece(r, 0), q_piece(r, 1), small_piece(r, small[0:2]),
                                          small_piece(r, small[2:4]), small_piece(r, small[4:]))]
    spread = lambda items, n: [items[i * len(items) // n:(i + 1) * len(items) // n] for i in range(n)]
    tiles = spread(later, n_early) + spread(rest, n_chunks - n_early)
    last = tiles[-1].pop()
    prepared = prepare(y_scr, chunk_rows(0))
    for ci in range(n_chunks):
        slots = [[] for _ in range(n_slots)]
        for j, emit in enumerate(tiles[ci]):
            slots[j * n_slots // len(tiles[ci])].append(emit)
        following = []
        if ci + 1 < n_chunks:
            slots[1].append(lambda ci=ci: following.append(prepare(y_scr, chunk_rows(ci + 1))))
        out["hg"][chunk_rows(ci), :] = finish(prepared, y_scr, chunk_rows(ci), st_scr, slots)
        prepared = following[0] if following else None
    last()


def _q_features():
    feat = np.zeros((1, NSA_HEADS * LANES), np.float32)
    for h in range(NSA_HEADS):
        f0 = h * LANES + NSA_HD * (1 - h // HEADS_PER_GROUP)
        feat[0, f0] = SEL_BLOCK * ALIBI_SLOPES[h]
        feat[0, f0 + 1] = ALIBI_SLOPES[h]
    return jnp.asarray(feat)


def _build_w_in(w_in):
    sizes = (NSA_WIDTH, KV_W, KV_W, KV_W, KV_W, KV_W, KV_W, NSA_HEADS * 3,
             HG_WIDTH, HG_WIDTH, HG_WIDTH, HG_WIDTH)
    splits = [int(v) for v in np.cumsum(sizes)[:-1]]
    w_in = w_in.astype(BF16)
    wq, wkc, wvc, wks, wvs, wkw, wvw, wgl, whq, whf, whi, whg = jnp.split(w_in, splits, axis=-1)
    lead = w_in.shape[:-1]
    gl = jnp.swapaxes(wgl.reshape(lead + (NSA_HEADS, 3)), -1, -2).reshape(lead + (3 * NSA_HEADS,))
    gl = jnp.pad(gl, [(0, 0)] * len(lead) + [(0, LANES - 3 * NSA_HEADS)])
    cols = [wq * NSA_HD ** -0.5, wks, wvs, wkw, wvw, wkc, wvc, gl, whq, whf, whi, whg]
    return jnp.concatenate(cols, axis=-1)


def _inproj(h, norm_g, w_ext, lb_raw, out_norm, seq, layer):
    t = h.shape[0]
    assert seq % ROW_TILE == 0 and ROW_TILE % HG_CHUNK == 0
    row = lambda w: pl.BlockSpec((ROW_TILE, w), lambda i: (i, 0))
    norm_g, out_norm = norm_g[:, None, :], out_norm[:, None, :]
    return pl.pallas_call(
        functools.partial(_inproj_kernel, seq=seq, layer=layer),
        grid=(t // ROW_TILE,),
        in_specs=[row(D_MODEL), _resident_layer(norm_g.shape, layer), _resident_layer(w_ext.shape, layer),
                  _resident((1, NSA_HEADS * LANES)), _resident(lb_raw.shape),
                  _resident_layer(out_norm.shape, layer)],
        out_specs=[row(w) for _, w, _, _ in SEGMENTS],
        out_shape=[jax.ShapeDtypeStruct((t, w), dt) for _, w, dt, _ in SEGMENTS],
        scratch_shapes=[pltpu.VMEM((ROW_TILE, D_MODEL), BF16), pltpu.VMEM((ROW_TILE, 4 * HG_WIDTH), F32),
                        pltpu.VMEM((HG_HEADS, HG_DV, HG_DK), F32)],
        compiler_params=_params(("arbitrary",)),
        name="inproj_hgrn",
    )(h, norm_g, w_ext, _q_features(), lb_raw, out_norm)


def _gelu_tanh(x):
    return 0.5 * x * (1.0 + jnp.tanh(0.7978845608028654 * (x + 0.044715 * (x * x * x))))


def _compress_kernel(kv_ref, pos_ref, w1_ref, w2_ref, o_ref):
    nbp = o_ref.shape[1]
    tokens = [kv_ref[0, pl.ds(l, nbp, stride=CMP_STRIDE), :] for l in range(CMP_STRIDE)]
    chunk = lambda first: jnp.concatenate(
        [(tokens[l] + pos_ref[first + l:first + l + 1, :]).astype(BF16) for l in range(CMP_STRIDE)], axis=1)
    slab = lambda first: w1_ref[first:first + CMP_STRIDE].reshape(CMP_STRIDE * KV_W, NSA_KV_GROUPS * CMP_HID)
    ha = _nn(chunk(0), slab(0))
    hb = _nn(chunk(CMP_STRIDE), slab(CMP_STRIDE))
    act = _gelu_tanh(ha + pltpu.roll(hb, nbp - 1, axis=0)).astype(BF16)
    out = jnp.zeros(o_ref.shape[1:], F32)
    for g in range(NSA_KV_GROUPS):
        out = out + _nn(act[:, g * CMP_HID:(g + 1) * CMP_HID], w2_ref[g])
    o_ref[0] = out


def _compress_weights(pos, w1, w2, reps):
    layers = w1.shape[0]
    w1l = w1.astype(BF16).reshape(layers, CMP_LEN, NSA_HD, CMP_HID)
    z1 = jnp.zeros_like(w1l)
    w1p = jnp.concatenate([jnp.concatenate([w1l, z1], axis=3), jnp.concatenate([z1, w1l], axis=3)], axis=2)
    pos2 = jnp.concatenate([pos] * NSA_KV_GROUPS, axis=2)
    w2 = w2.astype(BF16)
    zero = jnp.zeros_like(w2)
    w2p = jnp.stack([jnp.concatenate([w2 if r == g else zero for r in reps], axis=2)
                     for g in range(NSA_KV_GROUPS)], axis=1)
    return pos2, w1p, w2p


def _compress(kv, layer, pos2, w1p, w2p):
    b, s, _ = kv.shape
    nbp = s // CMP_STRIDE
    width = w2p.shape[-1]
    return pl.pallas_call(
        _compress_kernel,
        grid=(b,),
        in_specs=[pl.BlockSpec((1, s, KV_W), lambda i: (i, 0, 0))]
        + [_resident_layer(a.shape, layer) for a in (pos2, w1p, w2p)],
        out_specs=pl.BlockSpec((1, nbp, width), lambda i: (i, 0, 0)),
        out_shape=jax.ShapeDtypeStruct((b, nbp, width), F32),
        compiler_params=_params(("parallel",)),
        name="compress",
    )(kv, pos2, w1p, w2p)


def _pair_blocks(g, jj):
    lower = slice(0, LANES) if g == 0 else slice(LANES, 2 * LANES)
    upper = slice(LANES, 2 * LANES) if g == 0 else slice(0, LANES)
    return lower, upper


def _cmp_kernel(*refs, tq, ns, n_sel):
    total = refs[1].shape[1]
    per_tile = tq // CMP_STRIDE
    for t in range(total // per_tile):
        nbp = min(total, ((t + 1) * per_tile + LANES - 1) // LANES * LANES)
        sel_rows = min(ns, (t + 1) * tq // SEL_BLOCK)
        rounds = 0 if sel_rows <= n_sel else n_sel - 3

        @pl.when(pl.program_id(1) == t)
        def _(nbp=nbp, sel_rows=sel_rows, rounds=rounds):
            _cmp_tile(*refs, tq=tq, ns=ns, nbp=nbp, sel_rows=sel_rows, rounds=rounds)


def _cmp_tile(q_ref, kc_ref, vc2_ref, mt_ref, ocmp_ref, sel_ref, picks_ref, *, tq, ns, nbp, sel_rows, rounds):
    q0 = pl.program_id(1) * tq
    pos = q0 + lax.broadcasted_iota(jnp.int32, (tq, nbp), 0)
    blk_end = lax.broadcasted_iota(jnp.int32, (tq, nbp), 1) * CMP_STRIDE + (CMP_LEN - 1)
    valid = blk_end <= pos
    row_ok = (q0 + lax.broadcasted_iota(jnp.int32, (tq, 1), 0)) >= CMP_LEN - 1
    kc = kc_ref[0, :nbp, :].astype(BF16)
    vc2 = vc2_ref[0, :nbp, :].astype(BF16)
    lane_c = lax.broadcasted_iota(jnp.int32, (nbp, LANES), 1)
    c_idx = lax.broadcasted_iota(jnp.int32, (nbp, LANES), 0)
    per = SEL_BLOCK // CMP_STRIDE
    feat_a = (c_idx // per - q0 // SEL_BLOCK).astype(F32)
    feat_b = ((c_idx % per) * CMP_STRIDE + (CMP_LEN - 1)).astype(F32)
    lane_lo = lax.broadcasted_iota(jnp.int32, (tq, LANES), 1) < NSA_HD
    blk = lax.broadcasted_iota(jnp.int32, (ns, tq), 0)
    blk_r = lax.broadcasted_iota(jnp.int32, (sel_rows, tq), 0)
    blk_f = blk_r.astype(F32)
    pos_t = q0 + lax.broadcasted_iota(jnp.int32, (sel_rows, tq), 1)
    cur = pos_t // SEL_BLOCK
    forced = (blk_r == 0) | (blk_r == cur) | (blk_r == cur - 1)
    causal = blk_r * SEL_BLOCK <= pos_t
    sel_t, picks = [], []
    taken = -jnp.inf
    for g in range(NSA_KV_GROUPS):
        f0 = NSA_HD * (1 - g)
        feat = jnp.where(lane_c == f0, feat_a, jnp.where(lane_c == f0 + 1, feat_b, 0.0))
        kc_g = jnp.where((lane_c // NSA_HD) == g, kc, feat.astype(BF16))
        imp = jnp.zeros((tq, nbp), F32)
        acc = []
        for hh in range(HEADS_PER_GROUP):
            h = g * HEADS_PER_GROUP + hh
            s = _nt(q_ref[0, :, h * LANES:(h + 1) * LANES], kc_g)
            s = jnp.where(valid, s, NEG)
            e = jnp.exp(s - jnp.max(s, axis=-1, keepdims=True))
            inv = jnp.where(row_ok, 1.0 / jnp.sum(e, axis=-1, keepdims=True), 0.0)
            p = e * inv
            imp = imp + p
            acc.append(_nn(p.astype(BF16), vc2))
        for jj in range(HEADS_PER_GROUP // 2):
            lower, upper = _pair_blocks(g, jj)
            blk_out = jnp.where(lane_lo, acc[2 * jj][:, lower], acc[2 * jj + 1][:, upper])
            c0 = (g * HEADS_PER_GROUP + 2 * jj) * NSA_HD
            ocmp_ref[0, :, c0:c0 + LANES] = blk_out
        p_slc = sum(_nt(mt_ref[:sel_rows, :nbp], part) for part in _split3(imp))
        if rounds == 0:
            score = jnp.where(forced | causal, taken, NEG)
        else:
            score = jnp.where(forced, taken, jnp.where(causal, p_slc, NEG))
        for _ in range(rounds):
            top = jnp.max(score, axis=0, keepdims=True)
            first = jnp.min(jnp.where(score == top, blk_f, float(ns)), axis=0, keepdims=True)
            score = jnp.where(blk_f == first, taken, score)
        if sel_rows < ns:
            score = jnp.concatenate([score, jnp.full((ns - sel_rows, tq), NEG, F32)], axis=0)
        picked = jnp.where(score == taken, 1.0, 0.0).astype(BF16)
        picks.append([_nt(jnp.ones((SUBLANES, ATT_TQ), BF16), picked[:, t * ATT_TQ:(t + 1) * ATT_TQ])
                      for t in range(tq // ATT_TQ)])
        pen = jnp.where(score == taken, 0.0, SEL_PENALTY)
        pen = jnp.where((blk >= 2) & (blk <= ns - 1), pltpu.roll(pen, 1, axis=0), 0.0)
        if ns < NSA_HD:
            pen = jnp.concatenate([pen, jnp.zeros((NSA_HD - ns, tq), F32)], axis=0)
        zero = jnp.zeros((NSA_HD, tq), F32)
        sel_t += [zero, pen] if g == 0 else [pen, zero]
    sel_ref[0] = jnp.concatenate(sel_t, axis=0).T.astype(BF16)
    for t in range(tq // ATT_TQ):
        picks_ref[0, t] = jnp.concatenate([group[t] for group in picks], axis=1)


def _importance_matrix(nbp, ns):
    per = SEL_BLOCK // CMP_STRIDE
    m = np.zeros((ns, nbp), np.float32)
    for n in range(ns):
        for c in range(per * n, per * (n + 1)):
            for cc in (c - 1, c):
                if 0 <= cc < nbp - 1:
                    m[n, cc] += 1.0
    return jnp.asarray(m, BF16)


def _cmp_attention(q, kc, vc2, b, s):
    nbp = s // CMP_STRIDE
    ns = s // SEL_BLOCK
    tq = min(CMP_TQ, s)
    assert ns <= NSA_HD
    assert min(SEL_TOPK, ns) >= 3 and tq % ATT_TQ == 0 and tq % (SUBLANES * SEL_BLOCK) == 0
    q3 = q.reshape(b, s, NSA_HEADS * LANES)
    kern = functools.partial(_cmp_kernel, tq=tq, ns=ns, n_sel=min(SEL_TOPK, ns))
    return pl.pallas_call(
        kern,
        grid=(b, s // tq),
        in_specs=[pl.BlockSpec((1, tq, NSA_HEADS * LANES), lambda i, j: (i, j, 0)),
                  pl.BlockSpec((1, nbp, KV_W), lambda i, j: (i, 0, 0)),
                  pl.BlockSpec((1, nbp, 2 * KV_W), lambda i, j: (i, 0, 0)),
                  _resident((ns, nbp))],
        out_specs=[pl.BlockSpec((1, tq, NSA_WIDTH), lambda i, j: (i, j, 0)),
                   pl.BlockSpec((1, tq, NSA_KV_GROUPS * LANES), lambda i, j: (i, j, 0)),
                   pl.BlockSpec((1, tq // ATT_TQ, SUBLANES, NSA_KV_GROUPS * ns), lambda i, j: (i, j, 0, 0))],
        out_shape=[jax.ShapeDtypeStruct((b, s, NSA_WIDTH), F32),
                   jax.ShapeDtypeStruct((b, s, NSA_KV_GROUPS * LANES), BF16),
                   jax.ShapeDtypeStruct((b, s // ATT_TQ, SUBLANES, NSA_KV_GROUPS * ns), F32)],
        compiler_params=_params(("parallel", "parallel")),
        name="cmp_topk",
    )(q3, kc, vc2, _importance_matrix(nbp, ns))


def _attn_kernel(tiles_ref, ntiles_ref, q_ref, ks_ref, vs_ref, kw_ref, vw_ref, penq_ref, gl_ref, gexp_ref,
                 ocmp_ref, o_ref, qa_scr, s_scr, p_scr, al_scr, m_scr, acc_scr, *, tq, tk):
    i = pl.program_id(2)
    rows_of = lambda hh: slice(hh * tq, (hh + 1) * tq)
    lane_lo = lax.broadcasted_iota(jnp.int32, (tq, LANES), 1) < NSA_HD
    rel = (lax.broadcasted_iota(jnp.int32, (tq, tk), 1)
           - lax.broadcasted_iota(jnp.int32, (tq, tk), 0)).astype(F32)

    for hh in range(HEADS_PER_GROUP):
        qa_scr[rows_of(hh)] = q_ref[0, :, hh * LANES:(hh + 1) * LANES] + penq_ref[0]

    def scores(k_ref, j, slot):
        s_scr[slot] = _nt(qa_scr[...], k_ref[0, pl.ds(pl.multiple_of(j * tk, tk), tk), :])

    def probs(br, j, mode, slot, first, exists=None):
        for hh in range(HEADS_PER_GROUP):
            r = rows_of(hh)
            s = s_scr[slot, r]
            if mode == "causal":
                s = jnp.where(rel <= 0.0, s, NEG)
            elif mode == "lower":
                s = jnp.where(rel + (j * tk - i * tq).astype(F32) > -float(WINDOW), s, NEG)
            if exists is not None:
                s = jnp.where(exists, s, NEG)
            m_cur = jnp.max(s, axis=-1, keepdims=True)
            if first:
                m_new = jnp.broadcast_to(m_cur, (tq, LANES))
            else:
                m_prev = m_scr[br, r]
                m_new = jnp.maximum(m_prev, m_cur)
                al_scr[slot, r] = jnp.exp(m_prev - m_new)
            m_scr[br, r] = m_new
            p_scr[slot, r] = jnp.exp(s - jnp.concatenate([m_new] * (tk // LANES), axis=1)).astype(BF16)

    def accumulate(br, v_ref, j, slot, first):
        pv = _nn(p_scr[slot], v_ref[0, pl.ds(pl.multiple_of(j * tk, tk), tk), :])
        for hh in range(HEADS_PER_GROUP):
            r = rows_of(hh)
            if first:
                acc_scr[br, r] = pv[r]
            else:
                acc_scr[br, r] = al_scr[slot, r] * acc_scr[br, r] + pv[r]

    n_q = pl.num_programs(2)
    step = (pl.program_id(0) * n_q + i) * pl.num_programs(1) + pl.program_id(1)
    n_tiles = ntiles_ref[step]
    item = lambda br, j, mode, first, exists=None: (br, j, mode, first, exists)
    sel_item = lambda n: item(0, i, "causal", True) if isinstance(n, int) and n == 0 else item(
        0, tiles_ref[step * n_q + n - 1], None, False)
    n_back = WINDOW // tk
    win_items = [item(1, i, "causal", True)] + [
        item(1, jnp.maximum(i - d, 0), "lower" if d == n_back else None, False, i >= d)
        for d in range(1, n_back + 1)]
    k_refs, v_refs = (ks_ref, kw_ref), (vs_ref, vw_ref)

    def run(items, slot0, done=(0, 0, 0), upto=None):
        n_items = len(items)
        slot = lambda n: (slot0 + n) % 2
        n_s, n_p, n_a = done

        def do_scores(n):
            br, j, _, _, _ = items[n]
            scores(k_refs[br], j, slot(n))

        def do_probs(n):
            br, j, mode, first, exists = items[n]
            probs(br, j, mode, slot(n), first, exists)

        def do_acc(n):
            br, j, _, first, _ = items[n]
            accumulate(br, v_refs[br], j, slot(n), first)

        for n in range(n_s, min(n_a + 2, n_items)):
            do_scores(n)
        n_s = max(n_s, min(n_a + 2, n_items))
        for n in range(n_p, min(n_a + 1, n_items)):
            do_probs(n)
        n_p = max(n_p, min(n_a + 1, n_items))
        for k in range(n_a, n_items if upto is None else upto):
            if n_s <= k + 2 < n_items:
                do_scores(k + 2)
                n_s = k + 3
            if n_p <= k + 1 < n_items:
                do_probs(k + 1)
                n_p = k + 2
            do_acc(k)

    for n_static in range(2):
        @pl.when(n_tiles == n_static)
        def _(n_static=n_static):
            run([sel_item(n) for n in range(n_static + 1)] + win_items, 0)

    @pl.when(n_tiles >= 2)
    def _():
        run([sel_item(n) for n in range(3)], 0, upto=1)

        def sel_step(n, parity):
            scores(ks_ref, sel_item(n + 2)[1], parity)
            probs(0, None, None, 1 - parity, False)
            accumulate(0, vs_ref, sel_item(n)[1], parity, False)

        def sel_steps(first, count):
            for d in range(count):
                sel_step(first + d, (1 + d) % 2)

        n_pairs_of_steps = (n_tiles - 2) // 2
        n_quads = n_pairs_of_steps // 2

        def sel_four_steps(t, carry):
            sel_steps(4 * t + 1, 4)
            return carry

        lax.fori_loop(0, n_quads, sel_four_steps, 0)

        @pl.when(n_pairs_of_steps % 2 == 1)
        def _():
            sel_steps(4 * n_quads + 1, 2)

        def drain(parity):
            tail = [sel_item(n_tiles - 1), sel_item(n_tiles)] + win_items
            run(tail, 1 - parity, done=(2, 1, 0))

        @pl.when(n_tiles % 2 == 0)
        def _():
            drain(0)

        @pl.when(n_tiles % 2 == 1)
        def _():
            sel_step(n_tiles - 2, 1)
            drain(1)

    gexp = sum(_nn(part, gexp_ref[0]) for part in _split2(_sigmoid(gl_ref[0])))
    n_pairs = HEADS_PER_GROUP // 2
    gate_blk = lambda br, jj: gexp[:, (br * n_pairs + jj) * LANES:(br * n_pairs + jj + 1) * LANES]
    for jj in range(n_pairs):
        cols = slice(jj * LANES, (jj + 1) * LANES)
        blk = gate_blk(0, jj) * ocmp_ref[0, :, cols]
        for br in range(2):
            even = acc_scr[br, rows_of(2 * jj)]
            odd = acc_scr[br, rows_of(2 * jj + 1)]
            low = even / pltpu.roll(even, NSA_HD, axis=1)
            up = pltpu.roll(odd, NSA_HD, axis=1) / odd
            blk = blk + gate_blk(br + 1, jj) * jnp.where(lane_lo, low, up)
        o_ref[0, :, cols] = blk.astype(o_ref.dtype)


def _gate_expansion():
    n_pairs = HEADS_PER_GROUP // 2
    r = np.zeros((NSA_KV_GROUPS, LANES, 3 * n_pairs * LANES), np.float32)
    for g in range(NSA_KV_GROUPS):
        for br in range(3):
            for jj in range(n_pairs):
                for odd in range(2):
                    src = br * NSA_HEADS + g * HEADS_PER_GROUP + 2 * jj + odd
                    dst = (br * n_pairs + jj) * LANES + odd * NSA_HD
                    r[g, src, dst:dst + NSA_HD] = 1.0
    return jnp.asarray(r, BF16)


def _picked_tiles(picks, b, s):
    ns, nq = s // SEL_BLOCK, s // ATT_TQ
    per_tile = ATT_TK // SEL_BLOCK
    per_block = picks[:, :, 0, :].reshape(b, nq, NSA_KV_GROUPS, ns // per_tile, per_tile)
    j = jnp.arange(ns // per_tile, dtype=jnp.int32)
    i = jnp.arange(nq, dtype=jnp.int32)[None, :, None, None]
    active = ((per_block.sum(-1) > 0) | (j == 0)) & (j < i)
    slot = jnp.cumsum(active, axis=-1) - 1
    hit = active[..., None, :] & (slot[..., None, :] == j[:, None])
    tiles = jnp.sum(jnp.where(hit, j, 0), axis=-1).astype(jnp.int32)
    return tiles.reshape(-1), active.sum(-1).astype(jnp.int32).reshape(-1)


def _attention(q, ks, vs, kw, vw, pen, picks, gl, ocmp, b, s):
    tq, tk = ATT_TQ, ATT_TK
    assert tq == tk and WINDOW % tk == 0 and WINDOW // tk <= 2 and tq % SEL_BLOCK == 0
    gw = HEADS_PER_GROUP * LANES
    ow = HEADS_PER_GROUP * NSA_HD
    r3 = lambda a: a.reshape(b, s, a.shape[-1])
    tile = lambda w: pl.BlockSpec((1, tq, w), lambda i, g, j, *_: (i, j, 0))
    gtile = lambda w: pl.BlockSpec((1, tq, w), lambda i, g, j, *_: (i, j, g))
    gfull = pl.BlockSpec((1, s, LANES), lambda i, g, j, *_: (i, 0, g))
    gexp = _gate_expansion()
    rows = HEADS_PER_GROUP * tq
    kern = functools.partial(_attn_kernel, tq=tq, tk=tk)
    tiles, n_tiles = _picked_tiles(picks, b, s)
    grid_spec = pltpu.PrefetchScalarGridSpec(
        num_scalar_prefetch=2,
        grid=(b, NSA_KV_GROUPS, s // tq),
        in_specs=[gtile(gw), gfull, gfull, gfull, gfull, gtile(LANES), tile(LANES),
                  pl.BlockSpec((1,) + gexp.shape[1:], lambda i, g, j, *_: (g, 0, 0)), gtile(ow)],
        out_specs=gtile(ow),
        scratch_shapes=[pltpu.VMEM((rows, LANES), BF16), pltpu.VMEM((2, rows, tk), F32),
                        pltpu.VMEM((2, rows, tk), BF16), pltpu.VMEM((2, rows, LANES), F32),
                        pltpu.VMEM((2, rows, LANES), F32), pltpu.VMEM((2, rows, LANES), F32)])
    return pl.pallas_call(
        kern,
        grid_spec=grid_spec,
        out_shape=jax.ShapeDtypeStruct((b, s, NSA_WIDTH), BF16),
        compiler_params=_params(("parallel", "parallel", "parallel")),
        name="sel_win_attention",
    )(tiles, n_tiles, r3(q), r3(ks), r3(vs), r3(kw), r3(vw), pen, r3(gl), gexp, ocmp)


def _hgrn_chunks(raw, onorm, layer):
    c = HG_CHUNK
    ex = jnp.exp(raw - jnp.max(raw, axis=0, keepdims=True))
    sm = ex / jnp.sum(ex, axis=0, keepdims=True)
    lb_all = jnp.zeros((1, raw.shape[1]), F32)
    for l in range(1, layer + 1):
        lb_all = lb_all + sm[l:l + 1, :]

    t_idx = lax.broadcasted_iota(jnp.int32, (c, HG_DK), 0)
    sub = lax.broadcasted_iota(jnp.int32, (SUBLANES, HG_DK), 0)
    ti = lax.broadcasted_iota(jnp.int32, (c, c), 0)
    si = lax.broadcasted_iota(jnp.int32, (c, c), 1)
    tril = jnp.where(si <= ti, 1.0, 0.0).astype(BF16)
    levels = (32, 16, 8, 4, 2, 1)
    hc = HG_HEADS * c
    tb = lax.broadcasted_iota(jnp.int32, (hc, hc), 0)
    sb = lax.broadcasted_iota(jnp.int32, (hc, hc), 1)
    same_head = (tb // c) == (sb // c)
    split_bit = tb ^ sb
    pair_mask = {m: jnp.where(same_head & (tb > sb) & (split_bit >= m) & (split_bit < 2 * m), 1.0, 0.0)
                 for m in levels}
    diagonal = tb == sb
    second_half = {m: (t_idx & m) != 0 for m in levels}
    sign = {m: jnp.where(second_half[m], 1.0, -1.0) for m in levels}

    def ref_rows(b, m):
        row = lambda r, n: jnp.broadcast_to(b[r:r + 1, :], (n, HG_DK))
        if m >= 4:
            return jnp.concatenate([row(s0 + m - 1, 2 * m) for s0 in range(0, c, 2 * m)], axis=0)
        return jnp.concatenate([jnp.where(sub < 4, row(s0 + 1, SUBLANES), row(s0 + 5, SUBLANES))
                                for s0 in range(0, c, SUBLANES)], axis=0)

    part = lambda y_ref, rows, which, h: y_ref[rows, which * HG_WIDTH + h * HG_DK:which * HG_WIDTH + (h + 1) * HG_DK]

    def prepare(y_ref, rows):
        q, k, f, v, logf = [], [], [], [], []
        for h in range(HG_HEADS):
            lb = lb_all[:, h * HG_DK:(h + 1) * HG_DK]
            z = part(y_ref, rows, 1, h)
            hq = part(y_ref, rows, 0, h)
            q.append(hq * _sigmoid(hq))
            ez = jnp.exp(-jnp.abs(z))
            big = 1.0 / (1.0 + ez)
            small = ez * big
            f.append(jnp.maximum(lb + (1.0 - lb) * jnp.where(z >= 0.0, big, small), F_FLOOR))
            logf.append(jnp.log2(f[h]))
            k.append((1.0 - lb) * jnp.where(z >= 0.0, small, big))
            v.append(part(y_ref, rows, 2, h).astype(BF16))
        parts = jnp.concatenate([p for h in range(HG_HEADS) for p in _split3(logf[h])], axis=1)
        csum = _nn(tril, parts)
        bcum = [sum(csum[:, (3 * h + i) * HG_DK:(3 * h + i + 1) * HG_DK] for i in range(3))
                for h in range(HG_HEADS)]
        return q, k, f, v, bcum

    def finish(prepared, y_ref, rows, st_scr, fill):
        q, k, f, v, bcum = prepared
        rowdot = jnp.concatenate([jnp.sum(q[h] * k[h], axis=-1, keepdims=True) for h in range(HG_HEADS)], axis=0)
        a = jnp.where(diagonal, rowdot, 0.0)
        for m, emits in zip(levels, fill, strict=True):
            for emit in emits:
                emit()
            r = []
            for h in range(HG_HEADS):
                if m == 1:
                    r.append(jnp.where(second_half[m], q[h] * f[h], k[h]))
                else:
                    w = jnp.exp2((bcum[h] - ref_rows(bcum[h], m)) * sign[m])
                    r.append(jnp.where(second_half[m], q[h], k[h]) * w)
            r = jnp.concatenate(r, axis=0).astype(BF16)
            a = a + _nt(r, r) * pair_mask[m]
        o_intra = _nn(a.astype(BF16), jnp.concatenate(v, axis=0))

        heads = []
        for h in range(HG_HEADS):
            st = st_scr[h]
            o = o_intra[h * c:(h + 1) * c] + _nt((q[h] * jnp.exp2(bcum[h])).astype(BF16), st.astype(BF16))
            b_last = bcum[h][c - 1:c, :]
            kd = (k[h] * jnp.exp2(b_last - bcum[h])).astype(BF16)
            st_scr[h] = jnp.exp2(b_last) * st + _tn(v[h], kd)

            o = o * lax.rsqrt(jnp.mean(o * o, axis=-1, keepdims=True) + EPS) * onorm
            gate = part(y_ref, rows, 3, h)
            heads.append((o * (gate * _sigmoid(gate))).astype(BF16))
        return jnp.concatenate(heads, axis=1)

    return prepare, finish, len(levels)


def _mixer(h, l, b, s, mix_norm, w_ext, cmp_k, cmp_v, hgrn_lower_bound, hgrn_out_norm):
    q, ks, vs, kw, vw, kc_in, vc_in, gl, o_hg = _inproj(h, mix_norm, w_ext, hgrn_lower_bound, hgrn_out_norm, s, l)
    kc = _compress(kc_in.reshape(b, s, KV_W), l, *cmp_k)
    vc2 = _compress(vc_in.reshape(b, s, KV_W), l, *cmp_v)
    ocmp, sel, picks = _cmp_attention(q, kc, vc2, b, s)
    o_nsa = _attention(q, ks, vs, kw, vw, sel, picks, gl, ocmp, b, s)
    return o_nsa.reshape(b * s, NSA_WIDTH), o_hg


def kernel(x, ffn1_norm, ffn1_w_gu, ffn1_w_down, mix_norm, w_in, cmp_pos_k, cmp_pos_v, cmp_k_w1, cmp_k_w2, cmp_v_w1, cmp_v_w2, hgrn_lower_bound, hgrn_out_norm, w_out, ffn2_norm, ffn2_w_gu, ffn2_w_down, final_norm):
    b, s, d = x.shape
    depth = ffn1_norm.shape[0]
    ffn1_w = _ffn_weights(ffn1_w_gu, ffn1_w_down)
    ffn2_w = _ffn_weights(ffn2_w_gu, ffn2_w_down)
    w_out = w_out.astype(BF16)
    w_ext = _build_w_in(w_in)
    cmp_k = _compress_weights(cmp_pos_k, cmp_k_w1, cmp_k_w2, (0, 1))
    cmp_v = _compress_weights(cmp_pos_v, cmp_v_w1, cmp_v_w2, (0, 1, 1, 0))
    h = x.reshape(b * s, d)
    for l in range(depth):
        h = _ffn(h, l, ffn1_norm, *ffn1_w, final_norm, False)
        o_nsa, o_hg = _mixer(h, l, b, s, mix_norm, w_ext, cmp_k, cmp_v, hgrn_lower_bound, hgrn_out_norm)
        h = _ffn(h, l, ffn2_norm, *ffn2_w, final_norm, l == depth - 1, (o_nsa, o_hg, w_out))
    return h.reshape(b, s, d)
```

```python
import functools

import jax
import jax.numpy as jnp
import numpy as np
from jax import lax
from jax.experimental import pallas as pl
from jax.experimental.pallas import tpu as pltpu

F32 = jnp.float32
BF16 = jnp.bfloat16

D_MODEL = 1024
EPS = 1e-6
NEG = -1e30
F_FLOOR = 1e-30
NSA_HEADS = 8
NSA_KV_GROUPS = 2
HEADS_PER_GROUP = NSA_HEADS // NSA_KV_GROUPS
NSA_HD = 64
CMP_LEN = 32
CMP_STRIDE = 16
CMP_HID = 256
SEL_BLOCK = 64
SEL_TOPK = 16
WINDOW = 512
HG_HEADS = 4
HG_DK = 128
HG_DV = 128
HG_CHUNK = 64
D_FF = 2752
NSA_WIDTH = NSA_HEADS * NSA_HD
HG_WIDTH = HG_HEADS * HG_DV
KV_W = NSA_KV_GROUPS * NSA_HD

LANES = 128
SUBLANES = 8
MXU_COLS = 256
FF_CHUNK = 256
D_FF_PAD = -(-D_FF // FF_CHUNK) * FF_CHUNK
ROW_TILE = 512
FFN_ROWS = 1024
ATT_TQ = 256
ATT_TK = 256
ATT_STATIC_TILES = 7
CMP_TQ = 1024
VMEM_LIMIT = 56 * 1024 * 1024

SEL_PENALTY = 2.0 ** 50
ALIBI_SLOPES = tuple(2.0 ** (-8.0 * (i + 1) / NSA_HEADS) for i in range(NSA_HEADS))

SEGMENTS = (
    ("q", NSA_HEADS * LANES, BF16, NSA_WIDTH),
    ("ks", NSA_KV_GROUPS * LANES, BF16, KV_W),
    ("vs", NSA_KV_GROUPS * LANES, BF16, KV_W),
    ("kw", NSA_KV_GROUPS * LANES, BF16, KV_W),
    ("vw", NSA_KV_GROUPS * LANES, BF16, KV_W),
    ("kc", KV_W, F32, KV_W),
    ("vc", KV_W, F32, KV_W),
    ("gl", LANES, F32, LANES),
    ("hg", HG_WIDTH, BF16, 4 * HG_WIDTH),
)
SEG_OFFSETS = tuple(int(v) for v in np.cumsum([0] + [s[3] for s in SEGMENTS]))


def _nn(a, b):
    return jnp.dot(a, b, preferred_element_type=F32)


def _nt(a, b):
    return lax.dot_general(a, b, (((1,), (1,)), ((), ())), preferred_element_type=F32)


def _tn(a, b):
    return lax.dot_general(a, b, (((0,), (0,)), ((), ())), preferred_element_type=F32)


def _split2(x):
    hi = x.astype(BF16)
    lo = (x - hi.astype(F32)).astype(BF16)
    return hi, lo


def _split3(x):
    hi = x.astype(BF16)
    r = x - hi.astype(F32)
    mid = r.astype(BF16)
    lo = (r - mid.astype(F32)).astype(BF16)
    return hi, mid, lo


def _sigmoid(x):
    return 1.0 / (1.0 + jnp.exp(-x))


def _rms(x, g):
    return x * lax.rsqrt(jnp.mean(x * x, axis=-1, keepdims=True) + EPS) * g


def _resident(shape):
    nd = len(shape)
    return pl.BlockSpec(shape, lambda *_: (0,) * nd, pipeline_mode=pl.Buffered(1))


def _resident_layer(stacked_shape, layer):
    nd = len(stacked_shape)
    return pl.BlockSpec((None,) + tuple(stacked_shape[1:]), lambda *_: (layer,) + (0,) * (nd - 1),
                        pipeline_mode=pl.Buffered(1))


def _params(sem):
    return pltpu.CompilerParams(dimension_semantics=sem, vmem_limit_bytes=VMEM_LIMIT)


def _ffn_kernel(*refs, final, proj):
    if proj:
        x_ref, a_ref, b_ref, wo_ref, g_ref, wg_ref, wu_ref, wd_ref, gf_ref, o_ref = refs
        x = x_ref[...] + _nn(a_ref[...], wo_ref[:NSA_WIDTH, :]) + _nn(b_ref[...], wo_ref[NSA_WIDTH:, :])
    else:
        x_ref, g_ref, wg_ref, wu_ref, wd_ref, gf_ref, o_ref = refs
        x = x_ref[...]
    xn = _rms(x, g_ref[...]).astype(BF16)
    acc = jnp.zeros(x.shape, F32)
    for k in range(D_FF_PAD // FF_CHUNK):
        sl = slice(k * FF_CHUNK, (k + 1) * FF_CHUNK)
        gate = _nn(xn, wg_ref[:, sl])
        up = _nn(xn, wu_ref[:, sl])
        h = (gate * _sigmoid(gate) * up).astype(BF16)
        acc = acc + _nn(h, wd_ref[sl, :])
    y = x + 0.5 * acc
    if final:
        y = _rms(y, gf_ref[...])
    o_ref[...] = y


def _ffn_weights(w_gu, w_down):
    pad = D_FF_PAD - D_FF
    wg = jnp.pad(w_gu[..., :D_FF], ((0, 0), (0, 0), (0, pad))).astype(BF16)
    wu = jnp.pad(w_gu[..., D_FF:], ((0, 0), (0, 0), (0, pad))).astype(BF16)
    wd = jnp.pad(w_down, ((0, 0), (0, pad), (0, 0))).astype(BF16)
    return wg, wu, wd


def _ffn(h, layer, norm_g, wg, wu, wd, final_g, final, proj=None):
    t = h.shape[0]
    row = lambda w: pl.BlockSpec((FFN_ROWS, w), lambda i: (i, 0))
    norm_g = norm_g[:, None, :]
    weights = [_resident_layer(a.shape, layer) for a in (norm_g, wg, wu, wd)] + [_resident((1, D_MODEL))]
    operands = [norm_g, wg, wu, wd, final_g.reshape(1, -1)]
    if proj is None:
        in_specs, args = [row(D_MODEL)] + weights, [h] + operands
    else:
        o_nsa, o_hg, w_out = proj
        in_specs = [row(D_MODEL), row(NSA_WIDTH), row(HG_WIDTH), _resident_layer(w_out.shape, layer)] + weights
        args = [h, o_nsa, o_hg, w_out] + operands
    return pl.pallas_call(
        functools.partial(_ffn_kernel, final=final, proj=proj is not None),
        grid=(t // FFN_ROWS,),
        in_specs=in_specs,
        out_specs=row(D_MODEL),
        out_shape=jax.ShapeDtypeStruct((t, D_MODEL), F32),
        compiler_params=_params(("parallel",)),
        name="ffn",
    )(*args)


def _inproj_kernel(x_ref, g_ref, w_ref, qfeat_ref, lbraw_ref, onorm_ref, *refs, seq, layer):
    *o_refs, xn_scr, y_scr, st_scr = refs
    out = {seg[0]: o_ref for seg, o_ref in zip(SEGMENTS, o_refs, strict=True)}
    col0 = {seg[0]: off for seg, off in zip(SEGMENTS, SEG_OFFSETS[:-1], strict=True)}
    rows = x_ref.shape[0]
    half = rows // 2
    tile = pl.program_id(0) % (seq // rows)

    @pl.when(tile == 0)
    def _():
        st_scr[...] = jnp.zeros(st_scr.shape, F32)

    def norm_piece(r):
        rs = slice(r * half, (r + 1) * half)
        xn_scr[rs, :] = _rms(x_ref[rs, :], g_ref[...]).astype(BF16)

    lane = lax.broadcasted_iota(jnp.int32, (half, LANES), 1)
    lower = lane < NSA_HD
    swap = lambda a: pltpu.roll(a, NSA_HD, axis=1)
    halves = [NSA_HD * (1 - g) for g in range(NSA_KV_GROUPS)]
    project = lambda rs, c0, cols: _nn(xn_scr[rs, :], w_ref[:, c0:c0 + cols])

    def q_piece(r, cb):
        def emit():
            rs = slice(r * half, (r + 1) * half)
            y = project(rs, col0["q"] + cb * MXU_COLS, MXU_COLS)
            per = MXU_COLS // NSA_HD
            blocks = []
            for h in range(cb * per, (cb + 1) * per):
                pair = y[:, (h % per // 2) * LANES:(h % per // 2 + 1) * LANES]
                g = h // HEADS_PER_GROUP
                data = pair if h % 2 == g else swap(pair)
                feat = qfeat_ref[:, h * LANES:(h + 1) * LANES]
                blocks.append(jnp.where(lower == (g == 0), data, feat))
            out["q"][rs, cb * per * LANES:(cb + 1) * per * LANES] = (
                jnp.concatenate(blocks, axis=1).astype(out["q"].dtype))
        return emit

    small = [seg[0] for seg in SEGMENTS if seg[3] == LANES]
    assert all(col0[b] - col0[a] == LANES for a, b in zip(small, small[1:]))

    def small_piece(r, names):
        def emit():
            rs = slice(r * half, (r + 1) * half)
            y_small = project(rs, col0[names[0]], len(names) * LANES)
            kpos = tile * rows + r * half + lax.broadcasted_iota(jnp.int32, (half, LANES), 0)
            blk = kpos // SEL_BLOCK
            digit = lambda f0: jnp.where(lane == f0, blk.astype(F32),
                                         jnp.where(lane == f0 + 1, (kpos % SEL_BLOCK).astype(F32), 0.0))
            is_pen = lambda f0: (lane == f0 + 1 + blk) & (blk >= 1) & (blk <= seq // SEL_BLOCK - 2)
            for n, name in enumerate(names):
                y = y_small[:, n * LANES:(n + 1) * LANES]
                if name in ("ks", "kw"):
                    feat = [digit(f0) + jnp.where(is_pen(f0), -SEL_PENALTY, 0.0) if name == "ks" else digit(f0)
                            for f0 in halves]
                    y = jnp.concatenate([jnp.where(lower == (g == 0), y, feat[g])
                                         for g in range(NSA_KV_GROUPS)], axis=1)
                elif name in ("vs", "vw"):
                    y = jnp.concatenate([jnp.where(lower, y if g == 0 else swap(y), 1.0)
                                         for g in range(NSA_KV_GROUPS)], axis=1)
                out[name][rs, :] = y.astype(out[name].dtype)
        return emit

    def hg_piece(r, cb):
        def emit():
            rs = slice(r * half, (r + 1) * half)
            y_scr[rs, cb * MXU_COLS:(cb + 1) * MXU_COLS] = project(rs, col0["hg"] + cb * MXU_COLS, MXU_COLS)
        return emit

    prepare, finish, n_slots = _hgrn_chunks(lbraw_ref[...], onorm_ref[...], layer)
    n_chunks = rows // HG_CHUNK
    chunk_rows = lambda ci: slice(ci * HG_CHUNK, (ci + 1) * HG_CHUNK)
    hg_tiles = 4 * HG_WIDTH // MXU_COLS
    norm_piece(0)
    hg_piece(0, 0)()
    norm_piece(1)
    for cb in range(1, hg_tiles):
        hg_piece(0, cb)()
    later = [hg_piece(1, cb) for cb in range(hg_tiles)]
    n_early = n_chunks // 2 - 1
    rest = [p for r in range(2) for p in (q_piece(r, 0), q_piece(r, 1), small_piece(r, small[0:2]),
                                          small_piece(r, small[2:4]), small_piece(r, small[4:]))]
    spread = lambda items, n: [items[i * len(items) // n:(i + 1) * len(items) // n] for i in range(n)]
    tiles = spread(later, n_early) + spread(rest, n_chunks - n_early)
    last = tiles[-1].pop()
    prepared = prepare(y_scr, chunk_rows(0))
    for ci in range(n_chunks):
        slots = [[] for _ in range(n_slots)]
        for j, emit in enumerate(tiles[ci]):
            slots[j * n_slots // len(tiles[ci])].append(emit)
        following = []
        if ci + 1 < n_chunks:
            slots[1].append(lambda ci=ci: following.append(prepare(y_scr, chunk_rows(ci + 1))))
        out["hg"][chunk_rows(ci), :] = finish(prepared, y_scr, chunk_rows(ci), st_scr, slots)
        prepared = following[0] if following else None
    last()


def _q_features():
    feat = np.zeros((1, NSA_HEADS * LANES), np.float32)
    for h in range(NSA_HEADS):
        f0 = h * LANES + NSA_HD * (1 - h // HEADS_PER_GROUP)
        feat[0, f0] = SEL_BLOCK * ALIBI_SLOPES[h]
        feat[0, f0 + 1] = ALIBI_SLOPES[h]
    return jnp.asarray(feat)


def _build_w_in(w_in):
    sizes = (NSA_WIDTH, KV_W, KV_W, KV_W, KV_W, KV_W, KV_W, NSA_HEADS * 3,
             HG_WIDTH, HG_WIDTH, HG_WIDTH, HG_WIDTH)
    splits = [int(v) for v in np.cumsum(sizes)[:-1]]
    w_in = w_in.astype(BF16)
    wq, wkc, wvc, wks, wvs, wkw, wvw, wgl, whq, whf, whi, whg = jnp.split(w_in, splits, axis=-1)
    lead = w_in.shape[:-1]
    gl = jnp.swapaxes(wgl.reshape(lead + (NSA_HEADS, 3)), -1, -2).reshape(lead + (3 * NSA_HEADS,))
    gl = jnp.pad(gl, [(0, 0)] * len(lead) + [(0, LANES - 3 * NSA_HEADS)])
    cols = [wq * NSA_HD ** -0.5, wks, wvs, wkw, wvw, wkc, wvc, gl, whq, whf, whi, whg]
    return jnp.concatenate(cols, axis=-1)


def _inproj(h, norm_g, w_ext, lb_raw, out_norm, seq, layer):
    t = h.shape[0]
    assert seq % ROW_TILE == 0 and ROW_TILE % HG_CHUNK == 0
    row = lambda w: pl.BlockSpec((ROW_TILE, w), lambda i: (i, 0))
    norm_g, out_norm = norm_g[:, None, :], out_norm[:, None, :]
    return pl.pallas_call(
        functools.partial(_inproj_kernel, seq=seq, layer=layer),
        grid=(t // ROW_TILE,),
        in_specs=[row(D_MODEL), _resident_layer(norm_g.shape, layer), _resident_layer(w_ext.shape, layer),
                  _resident((1, NSA_HEADS * LANES)), _resident(lb_raw.shape),
                  _resident_layer(out_norm.shape, layer)],
        out_specs=[row(w) for _, w, _, _ in SEGMENTS],
        out_shape=[jax.ShapeDtypeStruct((t, w), dt) for _, w, dt, _ in SEGMENTS],
        scratch_shapes=[pltpu.VMEM((ROW_TILE, D_MODEL), BF16), pltpu.VMEM((ROW_TILE, 4 * HG_WIDTH), F32),
                        pltpu.VMEM((HG_HEADS, HG_DV, HG_DK), F32)],
        compiler_params=_params(("arbitrary",)),
        name="inproj_hgrn",
    )(h, norm_g, w_ext, _q_features(), lb_raw, out_norm)


def _gelu_tanh(x):
    return 0.5 * x * (1.0 + jnp.tanh(0.7978845608028654 * (x + 0.044715 * (x * x * x))))


def _compress_kernel(kv_ref, pos_ref, w1_ref, w2_ref, o_ref):
    nbp = o_ref.shape[1]
    tokens = [kv_ref[0, pl.ds(l, nbp, stride=CMP_STRIDE), :] for l in range(CMP_STRIDE)]
    chunk = lambda first: jnp.concatenate(
        [(tokens[l] + pos_ref[first + l:first + l + 1, :]).astype(BF16) for l in range(CMP_STRIDE)], axis=1)
    slab = lambda first: w1_ref[first:first + CMP_STRIDE].reshape(CMP_STRIDE * KV_W, NSA_KV_GROUPS * CMP_HID)
    ha = _nn(chunk(0), slab(0))
    hb = _nn(chunk(CMP_STRIDE), slab(CMP_STRIDE))
    act = _gelu_tanh(ha + pltpu.roll(hb, nbp - 1, axis=0)).astype(BF16)
    out = jnp.zeros(o_ref.shape[1:], F32)
    for g in range(NSA_KV_GROUPS):
        out = out + _nn(act[:, g * CMP_HID:(g + 1) * CMP_HID], w2_ref[g])
    o_ref[0] = out


def _compress_weights(pos, w1, w2, reps):
    layers = w1.shape[0]
    w1l = w1.astype(BF16).reshape(layers, CMP_LEN, NSA_HD, CMP_HID)
    z1 = jnp.zeros_like(w1l)
    w1p = jnp.concatenate([jnp.concatenate([w1l, z1], axis=3), jnp.concatenate([z1, w1l], axis=3)], axis=2)
    pos2 = jnp.concatenate([pos] * NSA_KV_GROUPS, axis=2)
    w2 = w2.astype(BF16)
    zero = jnp.zeros_like(w2)
    w2p = jnp.stack([jnp.concatenate([w2 if r == g else zero for r in reps], axis=2)
                     for g in range(NSA_KV_GROUPS)], axis=1)
    return pos2, w1p, w2p


def _compress(kv, layer, pos2, w1p, w2p):
    b, s, _ = kv.shape
    nbp = s // CMP_STRIDE
    width = w2p.shape[-1]
    return pl.pallas_call(
        _compress_kernel,
        grid=(b,),
        in_specs=[pl.BlockSpec((1, s, KV_W), lambda i: (i, 0, 0))]
        + [_resident_layer(a.shape, layer) for a in (pos2, w1p, w2p)],
        out_specs=pl.BlockSpec((1, nbp, width), lambda i: (i, 0, 0)),
        out_shape=jax.ShapeDtypeStruct((b, nbp, width), F32),
        compiler_params=_params(("parallel",)),
        name="compress",
    )(kv, pos2, w1p, w2p)


def _pair_blocks(g, jj):
    lower = slice(0, LANES) if g == 0 else slice(LANES, 2 * LANES)
    upper = slice(LANES, 2 * LANES) if g == 0 else slice(0, LANES)
    return lower, upper


def _cmp_kernel(*refs, tq, ns, n_sel):
    total = refs[1].shape[1]
    per_tile = tq // CMP_STRIDE
    for t in range(total // per_tile):
        nbp = min(total, ((t + 1) * per_tile + LANES - 1) // LANES * LANES)
        sel_rows = min(ns, (t + 1) * tq // SEL_BLOCK)
        rounds = 0 if sel_rows <= n_sel else n_sel - 3

        @pl.when(pl.program_id(1) == t)
        def _(nbp=nbp, sel_rows=sel_rows, rounds=rounds):
            _cmp_tile(*refs, tq=tq, ns=ns, nbp=nbp, sel_rows=sel_rows, rounds=rounds)


def _cmp_tile(q_ref, kc_ref, vc2_ref, mt_ref, ocmp_ref, sel_ref, picks_ref, *, tq, ns, nbp, sel_rows, rounds):
    q0 = pl.program_id(1) * tq
    pos = q0 + lax.broadcasted_iota(jnp.int32, (tq, nbp), 0)
    blk_end = lax.broadcasted_iota(jnp.int32, (tq, nbp), 1) * CMP_STRIDE + (CMP_LEN - 1)
    valid = blk_end <= pos
    row_ok = (q0 + lax.broadcasted_iota(jnp.int32, (tq, 1), 0)) >= CMP_LEN - 1
    kc = kc_ref[0, :nbp, :].astype(BF16)
    vc2 = vc2_ref[0, :nbp, :].astype(BF16)
    lane_c = lax.broadcasted_iota(jnp.int32, (nbp, LANES), 1)
    c_idx = lax.broadcasted_iota(jnp.int32, (nbp, LANES), 0)
    per = SEL_BLOCK // CMP_STRIDE
    feat_a = (c_idx // per - q0 // SEL_BLOCK).astype(F32)
    feat_b = ((c_idx % per) * CMP_STRIDE + (CMP_LEN - 1)).astype(F32)
    lane_lo = lax.broadcasted_iota(jnp.int32, (tq, LANES), 1) < NSA_HD
    blk = lax.broadcasted_iota(jnp.int32, (ns, tq), 0)
    blk_r = lax.broadcasted_iota(jnp.int32, (sel_rows, tq), 0)
    blk_f = blk_r.astype(F32)
    pos_t = q0 + lax.broadcasted_iota(jnp.int32, (sel_rows, tq), 1)
    cur = pos_t // SEL_BLOCK
    forced = (blk_r == 0) | (blk_r == cur) | (blk_r == cur - 1)
    causal = blk_r * SEL_BLOCK <= pos_t
    sel_t, picks = [], []
    taken = -jnp.inf
    for g in range(NSA_KV_GROUPS):
        f0 = NSA_HD * (1 - g)
        feat = jnp.where(lane_c == f0, feat_a, jnp.where(lane_c == f0 + 1, feat_b, 0.0))
        kc_g = jnp.where((lane_c // NSA_HD) == g, kc, feat.astype(BF16))
        imp = jnp.zeros((tq, nbp), F32)
        acc = []
        for hh in range(HEADS_PER_GROUP):
            h = g * HEADS_PER_GROUP + hh
            s = _nt(q_ref[0, :, h * LANES:(h + 1) * LANES], kc_g)
            s = jnp.where(valid, s, NEG)
            e = jnp.exp(s - jnp.max(s, axis=-1, keepdims=True))
            inv = jnp.where(row_ok, 1.0 / jnp.sum(e, axis=-1, keepdims=True), 0.0)
            p = e * inv
            imp = imp + p
            acc.append(_nn(p.astype(BF16), vc2))
        for jj in range(HEADS_PER_GROUP // 2):
            lower, upper = _pair_blocks(g, jj)
            blk_out = jnp.where(lane_lo, acc[2 * jj][:, lower], acc[2 * jj + 1][:, upper])
            c0 = (g * HEADS_PER_GROUP + 2 * jj) * NSA_HD
            ocmp_ref[0, :, c0:c0 + LANES] = blk_out
        p_slc = sum(_nt(mt_ref[:sel_rows, :nbp], part) for part in _split3(imp))
        if rounds == 0:
            score = jnp.where(forced | causal, taken, NEG)
        else:
            score = jnp.where(forced, taken, jnp.where(causal, p_slc, NEG))
        for _ in range(rounds):
            top = jnp.max(score, axis=0, keepdims=True)
            first = jnp.min(jnp.where(score == top, blk_f, float(ns)), axis=0, keepdims=True)
            score = jnp.where(blk_f == first, taken, score)
        if sel_rows < ns:
            score = jnp.concatenate([score, jnp.full((ns - sel_rows, tq), NEG, F32)], axis=0)
        picked = jnp.where(score == taken, 1.0, 0.0).astype(BF16)
        picks.append([_nt(jnp.ones((SUBLANES, ATT_TQ), BF16), picked[:, t * ATT_TQ:(t + 1) * ATT_TQ])
                      for t in range(tq // ATT_TQ)])
        pen = jnp.where(score == taken, 0.0, SEL_PENALTY)
        pen = jnp.where((blk >= 2) & (blk <= ns - 1), pltpu.roll(pen, 1, axis=0), 0.0)
        if ns < NSA_HD:
            pen = jnp.concatenate([pen, jnp.zeros((NSA_HD - ns, tq), F32)], axis=0)
        zero = jnp.zeros((NSA_HD, tq), F32)
        sel_t += [zero, pen] if g == 0 else [pen, zero]
    sel_ref[0] = jnp.concatenate(sel_t, axis=0).T.astype(BF16)
    for t in range(tq // ATT_TQ):
        picks_ref[0, t] = jnp.concatenate([group[t] for group in picks], axis=1)


def _importance_matrix(nbp, ns):
    per = SEL_BLOCK // CMP_STRIDE
    m = np.zeros((ns, nbp), np.float32)
    for n in range(ns):
        for c in range(per * n, per * (n + 1)):
            for cc in (c - 1, c):
                if 0 <= cc < nbp - 1:
                    m[n, cc] += 1.0
    return jnp.asarray(m, BF16)


def _cmp_attention(q, kc, vc2, b, s):
    nbp = s // CMP_STRIDE
    ns = s // SEL_BLOCK
    tq = min(CMP_TQ, s)
    assert ns <= NSA_HD
    assert min(SEL_TOPK, ns) >= 3 and tq % ATT_TQ == 0 and tq % (SUBLANES * SEL_BLOCK) == 0
    q3 = q.reshape(b, s, NSA_HEADS * LANES)
    kern = functools.partial(_cmp_kernel, tq=tq, ns=ns, n_sel=min(SEL_TOPK, ns))
    return pl.pallas_call(
        kern,
        grid=(b, s // tq),
        in_specs=[pl.BlockSpec((1, tq, NSA_HEADS * LANES), lambda i, j: (i, j, 0)),
                  pl.BlockSpec((1, nbp, KV_W), lambda i, j: (i, 0, 0)),
                  pl.BlockSpec((1, nbp, 2 * KV_W), lambda i, j: (i, 0, 0)),
                  _resident((ns, nbp))],
        out_specs=[pl.BlockSpec((1, tq, NSA_WIDTH), lambda i, j: (i, j, 0)),
                   pl.BlockSpec((1, tq, NSA_KV_GROUPS * LANES), lambda i, j: (i, j, 0)),
                   pl.BlockSpec((1, tq // ATT_TQ, SUBLANES, NSA_KV_GROUPS * ns), lambda i, j: (i, j, 0, 0))],
        out_shape=[jax.ShapeDtypeStruct((b, s, NSA_WIDTH), F32),
                   jax.ShapeDtypeStruct((b, s, NSA_KV_GROUPS * LANES), BF16),
                   jax.ShapeDtypeStruct((b, s // ATT_TQ, SUBLANES, NSA_KV_GROUPS * ns), F32)],
        compiler_params=_params(("parallel", "parallel")),
        name="cmp_topk",
    )(q3, kc, vc2, _importance_matrix(nbp, ns))


def _attn_kernel(tiles_ref, ntiles_ref, q_ref, ks_ref, vs_ref, kw_ref, vw_ref, penq_ref, gl_ref, gexp_ref,
                 ocmp_ref, o_ref, qa_scr, s_scr, p_scr, al_scr, m_scr, acc_scr, *, tq, tk):
    i = pl.program_id(2)
    rows_of = lambda hh: slice(hh * tq, (hh + 1) * tq)
    lane_lo = lax.broadcasted_iota(jnp.int32, (tq, LANES), 1) < NSA_HD
    rel = (lax.broadcasted_iota(jnp.int32, (tq, tk), 1)
           - lax.broadcasted_iota(jnp.int32, (tq, tk), 0)).astype(F32)

    for hh in range(HEADS_PER_GROUP):
        qa_scr[rows_of(hh)] = q_ref[0, :, hh * LANES:(hh + 1) * LANES] + penq_ref[0]

    def scores(k_ref, j, slot):
        s_scr[slot] = _nt(qa_scr[...], k_ref[0, pl.ds(pl.multiple_of(j * tk, tk), tk), :])

    def probs(br, j, mode, slot, first, exists=None):
        for hh in range(HEADS_PER_GROUP):
            r = rows_of(hh)
            s = s_scr[slot, r]
            if mode == "causal":
                s = jnp.where(rel <= 0.0, s, NEG)
            elif mode == "lower":
                s = jnp.where(rel + (j * tk - i * tq).astype(F32) > -float(WINDOW), s, NEG)
            if exists is not None:
                s = jnp.where(exists, s, NEG)
            m_cur = jnp.max(s, axis=-1, keepdims=True)
            if first:
                m_new = jnp.broadcast_to(m_cur, (tq, LANES))
            else:
                m_prev = m_scr[br, r]
                m_new = jnp.maximum(m_prev, m_cur)
                al_scr[slot, r] = jnp.exp(m_prev - m_new)
            m_scr[br, r] = m_new
            p_scr[slot, r] = jnp.exp(s - jnp.concatenate([m_new] * (tk // LANES), axis=1)).astype(BF16)

    def accumulate(br, v_ref, j, slot, first):
        pv = _nn(p_scr[slot], v_ref[0, pl.ds(pl.multiple_of(j * tk, tk), tk), :])
        for hh in range(HEADS_PER_GROUP):
            r = rows_of(hh)
            if first:
                acc_scr[br, r] = pv[r]
            else:
                acc_scr[br, r] = al_scr[slot, r] * acc_scr[br, r] + pv[r]

    n_q = pl.num_programs(2)
    step = (pl.program_id(0) * n_q + i) * pl.num_programs(1) + pl.program_id(1)
    n_tiles = ntiles_ref[step]
    item = lambda br, j, mode, first, exists=None: (br, j, mode, first, exists)
    sel_item = lambda n: item(0, i, "causal", True) if isinstance(n, int) and n == 0 else item(
        0, tiles_ref[step * n_q + n - 1], None, False)
    n_back = WINDOW // tk
    win_items = [item(1, i, "causal", True)] + [
        item(1, jnp.maximum(i - d, 0), "lower" if d == n_back else None, False, i >= d)
        for d in range(1, n_back + 1)]
    k_refs, v_refs = (ks_ref, kw_ref), (vs_ref, vw_ref)

    def run(items, slot0, done=(0, 0, 0), upto=None):
        n_items = len(items)
        slot = lambda n: (slot0 + n) % 2
        n_s, n_p, n_a = done

        def do_scores(n):
            br, j, _, _, _ = items[n]
            scores(k_refs[br], j, slot(n))

        def do_probs(n):
            br, j, mode, first, exists = items[n]
            probs(br, j, mode, slot(n), first, exists)

        def do_acc(n):
            br, j, _, first, _ = items[n]
            accumulate(br, v_refs[br], j, slot(n), first)

        for n in range(n_s, min(n_a + 2, n_items)):
            do_scores(n)
        n_s = max(n_s, min(n_a + 2, n_items))
        for n in range(n_p, min(n_a + 1, n_items)):
            do_probs(n)
        n_p = max(n_p, min(n_a + 1, n_items))
        for k in range(n_a, n_items if upto is None else upto):
            if n_s <= k + 2 < n_items:
                do_scores(k + 2)
                n_s = k + 3
            if n_p <= k + 1 < n_items:
                do_probs(k + 1)
                n_p = k + 2
            do_acc(k)

    for n_static in range(ATT_STATIC_TILES):
        @pl.when(n_tiles == n_static)
        def _(n_static=n_static):
            run([sel_item(n) for n in range(n_static + 1)] + win_items, 0)

    assert ATT_STATIC_TILES >= 2
    @pl.when(n_tiles >= ATT_STATIC_TILES)
    def _():
        run([sel_item(n) for n in range(3)], 0, upto=1)

        def sel_step(n, parity):
            scores(ks_ref, sel_item(n + 2)[1], parity)
            probs(0, None, None, 1 - parity, False)
            accumulate(0, vs_ref, sel_item(n)[1], parity, False)

        def sel_steps(first, count):
            for d in range(count):
                sel_step(first + d, (1 + d) % 2)

        n_pairs_of_steps = (n_tiles - 2) // 2
        n_quads = n_pairs_of_steps // 2

        def sel_four_steps(t, carry):
            sel_steps(4 * t + 1, 4)
            return carry

        lax.fori_loop(0, n_quads, sel_four_steps, 0)

        @pl.when(n_pairs_of_steps % 2 == 1)
        def _():
            sel_steps(4 * n_quads + 1, 2)

        def drain(parity):
            tail = [sel_item(n_tiles - 1), sel_item(n_tiles)] + win_items
            run(tail, 1 - parity, done=(2, 1, 0))

        @pl.when(n_tiles % 2 == 0)
        def _():
            drain(0)

        @pl.when(n_tiles % 2 == 1)
        def _():
            sel_step(n_tiles - 2, 1)
            drain(1)

    gexp = sum(_nn(part, gexp_ref[0]) for part in _split2(_sigmoid(gl_ref[0])))
    n_pairs = HEADS_PER_GROUP // 2
    gate_blk = lambda br, jj: gexp[:, (br * n_pairs + jj) * LANES:(br * n_pairs + jj + 1) * LANES]
    for jj in range(n_pairs):
        cols = slice(jj * LANES, (jj + 1) * LANES)
        blk = gate_blk(0, jj) * ocmp_ref[0, :, cols]
        for br in range(2):
            even = acc_scr[br, rows_of(2 * jj)]
            odd = acc_scr[br, rows_of(2 * jj + 1)]
            low = even / pltpu.roll(even, NSA_HD, axis=1)
            up = pltpu.roll(odd, NSA_HD, axis=1) / odd
            blk = blk + gate_blk(br + 1, jj) * jnp.where(lane_lo, low, up)
        o_ref[0, :, cols] = blk.astype(o_ref.dtype)


def _gate_expansion():
    n_pairs = HEADS_PER_GROUP // 2
    r = np.zeros((NSA_KV_GROUPS, LANES, 3 * n_pairs * LANES), np.float32)
    for g in range(NSA_KV_GROUPS):
        for br in range(3):
            for jj in range(n_pairs):
                for odd in range(2):
                    src = br * NSA_HEADS + g * HEADS_PER_GROUP + 2 * jj + odd
                    dst = (br * n_pairs + jj) * LANES + odd * NSA_HD
                    r[g, src, dst:dst + NSA_HD] = 1.0
    return jnp.asarray(r, BF16)


def _picked_tiles(picks, b, s):
    ns, nq = s // SEL_BLOCK, s // ATT_TQ
    per_tile = ATT_TK // SEL_BLOCK
    per_block = picks[:, :, 0, :].reshape(b, nq, NSA_KV_GROUPS, ns // per_tile, per_tile)
    j = jnp.arange(ns // per_tile, dtype=jnp.int32)
    i = jnp.arange(nq, dtype=jnp.int32)[None, :, None, None]
    active = ((per_block.sum(-1) > 0) | (j == 0)) & (j < i)
    slot = jnp.cumsum(active, axis=-1) - 1
    hit = active[..., None, :] & (slot[..., None, :] == j[:, None])
    tiles = jnp.sum(jnp.where(hit, j, 0), axis=-1).astype(jnp.int32)
    return tiles.reshape(-1), active.sum(-1).astype(jnp.int32).reshape(-1)


def _attention(q, ks, vs, kw, vw, pen, picks, gl, ocmp, b, s):
    tq, tk = ATT_TQ, ATT_TK
    assert tq == tk and WINDOW % tk == 0 and WINDOW // tk <= 2 and tq % SEL_BLOCK == 0
    gw = HEADS_PER_GROUP * LANES
    ow = HEADS_PER_GROUP * NSA_HD
    r3 = lambda a: a.reshape(b, s, a.shape[-1])
    tile = lambda w: pl.BlockSpec((1, tq, w), lambda i, g, j, *_: (i, j, 0))
    gtile = lambda w: pl.BlockSpec((1, tq, w), lambda i, g, j, *_: (i, j, g))
    gfull = pl.BlockSpec((1, s, LANES), lambda i, g, j, *_: (i, 0, g))
    gexp = _gate_expansion()
    rows = HEADS_PER_GROUP * tq
    kern = functools.partial(_attn_kernel, tq=tq, tk=tk)
    tiles, n_tiles = _picked_tiles(picks, b, s)
    grid_spec = pltpu.PrefetchScalarGridSpec(
        num_scalar_prefetch=2,
        grid=(b, NSA_KV_GROUPS, s // tq),
        in_specs=[gtile(gw), gfull, gfull, gfull, gfull, gtile(LANES), tile(LANES),
                  pl.BlockSpec((1,) + gexp.shape[1:], lambda i, g, j, *_: (g, 0, 0)), gtile(ow)],
        out_specs=gtile(ow),
        scratch_shapes=[pltpu.VMEM((rows, LANES), BF16), pltpu.VMEM((2, rows, tk), F32),
                        pltpu.VMEM((2, rows, tk), BF16), pltpu.VMEM((2, rows, LANES), F32),
                        pltpu.VMEM((2, rows, LANES), F32), pltpu.VMEM((2, rows, LANES), F32)])
    return pl.pallas_call(
        kern,
        grid_spec=grid_spec,
        out_shape=jax.ShapeDtypeStruct((b, s, NSA_WIDTH), BF16),
        compiler_params=_params(("parallel", "parallel", "parallel")),
        name="sel_win_attention",
    )(tiles, n_tiles, r3(q), r3(ks), r3(vs), r3(kw), r3(vw), pen, r3(gl), gexp, ocmp)


def _hgrn_chunks(raw, onorm, layer):
    c = HG_CHUNK
    ex = jnp.exp(raw - jnp.max(raw, axis=0, keepdims=True))
    sm = ex / jnp.sum(ex, axis=0, keepdims=True)
    lb_all = jnp.zeros((1, raw.shape[1]), F32)
    for l in range(1, layer + 1):
        lb_all = lb_all + sm[l:l + 1, :]

    t_idx = lax.broadcasted_iota(jnp.int32, (c, HG_DK), 0)
    sub = lax.broadcasted_iota(jnp.int32, (SUBLANES, HG_DK), 0)
    ti = lax.broadcasted_iota(jnp.int32, (c, c), 0)
    si = lax.broadcasted_iota(jnp.int32, (c, c), 1)
    tril = jnp.where(si <= ti, 1.0, 0.0).astype(BF16)
    levels = (32, 16, 8, 4, 2, 1)
    hc = HG_HEADS * c
    tb = lax.broadcasted_iota(jnp.int32, (hc, hc), 0)
    sb = lax.broadcasted_iota(jnp.int32, (hc, hc), 1)
    same_head = (tb // c) == (sb // c)
    split_bit = tb ^ sb
    pair_mask = {m: jnp.where(same_head & (tb > sb) & (split_bit >= m) & (split_bit < 2 * m), 1.0, 0.0)
                 for m in levels}
    diagonal = tb == sb
    second_half = {m: (t_idx & m) != 0 for m in levels}
    sign = {m: jnp.where(second_half[m], 1.0, -1.0) for m in levels}

    def ref_rows(b, m):
        row = lambda r, n: jnp.broadcast_to(b[r:r + 1, :], (n, HG_DK))
        if m >= 4:
            return jnp.concatenate([row(s0 + m - 1, 2 * m) for s0 in range(0, c, 2 * m)], axis=0)
        return jnp.concatenate([jnp.where(sub < 4, row(s0 + 1, SUBLANES), row(s0 + 5, SUBLANES))
                                for s0 in range(0, c, SUBLANES)], axis=0)

    part = lambda y_ref, rows, which, h: y_ref[rows, which * HG_WIDTH + h * HG_DK:which * HG_WIDTH + (h + 1) * HG_DK]

    def prepare(y_ref, rows):
        q, k, f, v, logf = [], [], [], [], []
        for h in range(HG_HEADS):
            lb = lb_all[:, h * HG_DK:(h + 1) * HG_DK]
            z = part(y_ref, rows, 1, h)
            hq = part(y_ref, rows, 0, h)
            q.append(hq * _sigmoid(hq))
            ez = jnp.exp(-jnp.abs(z))
            big = 1.0 / (1.0 + ez)
            small = ez * big
            f.append(jnp.maximum(lb + (1.0 - lb) * jnp.where(z >= 0.0, big, small), F_FLOOR))
            logf.append(jnp.log2(f[h]))
            k.append((1.0 - lb) * jnp.where(z >= 0.0, small, big))
            v.append(part(y_ref, rows, 2, h).astype(BF16))
        parts = jnp.concatenate([p for h in range(HG_HEADS) for p in _split3(logf[h])], axis=1)
        csum = _nn(tril, parts)
        bcum = [sum(csum[:, (3 * h + i) * HG_DK:(3 * h + i + 1) * HG_DK] for i in range(3))
                for h in range(HG_HEADS)]
        return q, k, f, v, bcum

    def finish(prepared, y_ref, rows, st_scr, fill):
        q, k, f, v, bcum = prepared
        rowdot = jnp.concatenate([jnp.sum(q[h] * k[h], axis=-1, keepdims=True) for h in range(HG_HEADS)], axis=0)
        a = jnp.where(diagonal, rowdot, 0.0)
        for m, emits in zip(levels, fill, strict=True):
            for emit in emits:
                emit()
            r = []
            for h in range(HG_HEADS):
                if m == 1:
                    r.append(jnp.where(second_half[m], q[h] * f[h], k[h]))
                else:
                    w = jnp.exp2((bcum[h] - ref_rows(bcum[h], m)) * sign[m])
                    r.append(jnp.where(second_half[m], q[h], k[h]) * w)
            r = jnp.concatenate(r, axis=0).astype(BF16)
            a = a + _nt(r, r) * pair_mask[m]
        o_intra = _nn(a.astype(BF16), jnp.concatenate(v, axis=0))

        heads = []
        for h in range(HG_HEADS):
            st = st_scr[h]
            o = o_intra[h * c:(h + 1) * c] + _nt((q[h] * jnp.exp2(bcum[h])).astype(BF16), st.astype(BF16))
            b_last = bcum[h][c - 1:c, :]
            kd = (k[h] * jnp.exp2(b_last - bcum[h])).astype(BF16)
            st_scr[h] = jnp.exp2(b_last) * st + _tn(v[h], kd)

            o = o * lax.rsqrt(jnp.mean(o * o, axis=-1, keepdims=True) + EPS) * onorm
            gate = part(y_ref, rows, 3, h)
            heads.append((o * (gate * _sigmoid(gate))).astype(BF16))
        return jnp.concatenate(heads, axis=1)

    return prepare, finish, len(levels)


def _mixer(h, l, b, s, mix_norm, w_ext, cmp_k, cmp_v, hgrn_lower_bound, hgrn_out_norm):
    q, ks, vs, kw, vw, kc_in, vc_in, gl, o_hg = _inproj(h, mix_norm, w_ext, hgrn_lower_bound, hgrn_out_norm, s, l)
    kc = _compress(kc_in.reshape(b, s, KV_W), l, *cmp_k)
    vc2 = _compress(vc_in.reshape(b, s, KV_W), l, *cmp_v)
    ocmp, sel, picks = _cmp_attention(q, kc, vc2, b, s)
    o_nsa = _attention(q, ks, vs, kw, vw, sel, picks, gl, ocmp, b, s)
    return o_nsa.reshape(b * s, NSA_WIDTH), o_hg


def kernel(x, ffn1_norm, ffn1_w_gu, ffn1_w_down, mix_norm, w_in, cmp_pos_k, cmp_pos_v, cmp_k_w1, cmp_k_w2, cmp_v_w1, cmp_v_w2, hgrn_lower_bound, hgrn_out_norm, w_out, ffn2_norm, ffn2_w_gu, ffn2_w_down, final_norm):
    b, s, d = x.shape
    depth = ffn1_norm.shape[0]
    ffn1_w = _ffn_weights(ffn1_w_gu, ffn1_w_down)
    ffn2_w = _ffn_weights(ffn2_w_gu, ffn2_w_down)
    w_out = w_out.astype(BF16)
    w_ext = _build_w_in(w_in)
    cmp_k = _compress_weights(cmp_pos_k, cmp_k_w1, cmp_k_w2, (0, 1))
    cmp_v = _compress_weights(cmp_pos_v, cmp_v_w1, cmp_v_w2, (0, 1, 1, 0))
    h = x.reshape(b * s, d)
    for l in range(depth):
        h = _ffn(h, l, ffn1_norm, *ffn1_w, final_norm, False)
        o_nsa, o_hg = _mixer(h, l, b, s, mix_norm, w_ext, cmp_k, cmp_v, hgrn_lower_bound, hgrn_out_norm)
        h = _ffn(h, l, ffn2_norm, *ffn2_w, final_norm, l == depth - 1, (o_nsa, o_hg, w_out))
    return h.reshape(b, s, d)
```

```python
import functools

import jax
import jax.numpy as jnp
import numpy as np
from jax import lax
from jax.experimental import pallas as pl
from jax.experimental.pallas import tpu as pltpu

F32 = jnp.float32
BF16 = jnp.bfloat16

D_MODEL = 1024
EPS = 1e-6
NEG = -1e30
F_FLOOR = 1e-30
NSA_HEADS = 8
NSA_KV_GROUPS = 2
HEADS_PER_GROUP = NSA_HEADS // NSA_KV_GROUPS
NSA_HD = 64
CMP_LEN = 32
CMP_STRIDE = 16
CMP_HID = 256
SEL_BLOCK = 64
SEL_TOPK = 16
WINDOW = 512
HG_HEADS = 4
HG_DK = 128
HG_DV = 128
HG_CHUNK = 64
D_FF = 2752
NSA_WIDTH = NSA_HEADS * NSA_HD
HG_WIDTH = HG_HEADS * HG_DV
KV_W = NSA_KV_GROUPS * NSA_HD

LANES = 128
SUBLANES = 8
MXU_COLS = 256
FF_CHUNK = 256
D_FF_PAD = -(-D_FF // FF_CHUNK) * FF_CHUNK
ROW_TILE = 512
FFN_ROWS = 1024
ATT_TQ = 256
ATT_TK = 256
ATT_STATIC_TILES = 7
CMP_TQ = 1024
VMEM_LIMIT = 56 * 1024 * 1024

SEL_PENALTY = 2.0 ** 50
ALIBI_SLOPES = tuple(2.0 ** (-8.0 * (i + 1) / NSA_HEADS) for i in range(NSA_HEADS))

SEGMENTS = (
    ("q", NSA_HEADS * LANES, BF16, NSA_WIDTH),
    ("ks", NSA_KV_GROUPS * LANES, BF16, KV_W),
    ("vs", NSA_KV_GROUPS * LANES, BF16, KV_W),
    ("kw", NSA_KV_GROUPS * LANES, BF16, KV_W),
    ("vw", NSA_KV_GROUPS * LANES, BF16, KV_W),
    ("kc", KV_W, F32, KV_W),
    ("vc", KV_W, F32, KV_W),
    ("gl", LANES, F32, LANES),
    ("hg", HG_WIDTH, BF16, 4 * HG_WIDTH),
)
SEG_OFFSETS = tuple(int(v) for v in np.cumsum([0] + [s[3] for s in SEGMENTS]))


def _nn(a, b):
    return jnp.dot(a, b, preferred_element_type=F32)


def _nt(a, b):
    return lax.dot_general(a, b, (((1,), (1,)), ((), ())), preferred_element_type=F32)


def _tn(a, b):
    return lax.dot_general(a, b, (((0,), (0,)), ((), ())), preferred_element_type=F32)


def _split2(x):
    hi = x.astype(BF16)
    lo = (x - hi.astype(F32)).astype(BF16)
    return hi, lo


def _split3(x):
    hi = x.astype(BF16)
    r = x - hi.astype(F32)
    mid = r.astype(BF16)
    lo = (r - mid.astype(F32)).astype(BF16)
    return hi, mid, lo


def _sigmoid(x):
    return 1.0 / (1.0 + jnp.exp(-x))


def _rms(x, g):
    return x * lax.rsqrt(jnp.mean(x * x, axis=-1, keepdims=True) + EPS) * g


def _resident(shape):
    nd = len(shape)
    return pl.BlockSpec(shape, lambda *_: (0,) * nd, pipeline_mode=pl.Buffered(1))


def _resident_layer(stacked_shape, layer):
    nd = len(stacked_shape)
    return pl.BlockSpec((None,) + tuple(stacked_shape[1:]), lambda *_: (layer,) + (0,) * (nd - 1),
                        pipeline_mode=pl.Buffered(1))


def _params(sem):
    return pltpu.CompilerParams(dimension_semantics=sem, vmem_limit_bytes=VMEM_LIMIT)


def _ffn_kernel(*refs, final, proj):
    if proj:
        x_ref, a_ref, b_ref, wo_ref, g_ref, wg_ref, wu_ref, wd_ref, gf_ref, o_ref = refs
        x = x_ref[...] + _nn(a_ref[...], wo_ref[:NSA_WIDTH, :]) + _nn(b_ref[...], wo_ref[NSA_WIDTH:, :])
    else:
        x_ref, g_ref, wg_ref, wu_ref, wd_ref, gf_ref, o_ref = refs
        x = x_ref[...]
    xn = _rms(x, g_ref[...]).astype(BF16)
    acc = jnp.zeros(x.shape, F32)
    for k in range(D_FF_PAD // FF_CHUNK):
        sl = slice(k * FF_CHUNK, (k + 1) * FF_CHUNK)
        gate = _nn(xn, wg_ref[:, sl])
        up = _nn(xn, wu_ref[:, sl])
        h = (gate * _sigmoid(gate) * up).astype(BF16)
        acc = acc + _nn(h, wd_ref[sl, :])
    y = x + 0.5 * acc
    if final:
        y = _rms(y, gf_ref[...])
    o_ref[...] = y


def _ffn_weights(w_gu, w_down):
    pad = D_FF_PAD - D_FF
    wg = jnp.pad(w_gu[..., :D_FF], ((0, 0), (0, 0), (0, pad))).astype(BF16)
    wu = jnp.pad(w_gu[..., D_FF:], ((0, 0), (0, 0), (0, pad))).astype(BF16)
    wd = jnp.pad(w_down, ((0, 0), (0, pad), (0, 0))).astype(BF16)
    return wg, wu, wd


def _ffn(h, layer, norm_g, wg, wu, wd, final_g, final, proj=None):
    t = h.shape[0]
    row = lambda w: pl.BlockSpec((FFN_ROWS, w), lambda i: (i, 0))
    norm_g = norm_g[:, None, :]
    weights = [_resident_layer(a.shape, layer) for a in (norm_g, wg, wu, wd)] + [_resident((1, D_MODEL))]
    operands = [norm_g, wg, wu, wd, final_g.reshape(1, -1)]
    if proj is None:
        in_specs, args = [row(D_MODEL)] + weights, [h] + operands
    else:
        o_nsa, o_hg, w_out = proj
        in_specs = [row(D_MODEL), row(NSA_WIDTH), row(HG_WIDTH), _resident_layer(w_out.shape, layer)] + weights
        args = [h, o_nsa, o_hg, w_out] + operands
    return pl.pallas_call(
        functools.partial(_ffn_kernel, final=final, proj=proj is not None),
        grid=(t // FFN_ROWS,),
        in_specs=in_specs,
        out_specs=row(D_MODEL),
        out_shape=jax.ShapeDtypeStruct((t, D_MODEL), F32),
        compiler_params=_params(("parallel",)),
        name="ffn",
    )(*args)


def _inproj_kernel(x_ref, g_ref, w_ref, qfeat_ref, lbraw_ref, onorm_ref, *refs, seq, layer):
    *o_refs, xn_scr, y_scr, st_scr = refs
    out = {seg[0]: o_ref for seg, o_ref in zip(SEGMENTS, o_refs, strict=True)}
    col0 = {seg[0]: off for seg, off in zip(SEGMENTS, SEG_OFFSETS[:-1], strict=True)}
    rows = x_ref.shape[0]
    half = rows // 2
    tile = pl.program_id(0) % (seq // rows)

    @pl.when(tile == 0)
    def _():
        st_scr[...] = jnp.zeros(st_scr.shape, F32)

    def norm_piece(r):
        rs = slice(r * half, (r + 1) * half)
        xn_scr[rs, :] = _rms(x_ref[rs, :], g_ref[...]).astype(BF16)

    lane = lax.broadcasted_iota(jnp.int32, (half, LANES), 1)
    lower = lane < NSA_HD
    swap = lambda a: pltpu.roll(a, NSA_HD, axis=1)
    halves = [NSA_HD * (1 - g) for g in range(NSA_KV_GROUPS)]
    project = lambda rs, c0, cols: _nn(xn_scr[rs, :], w_ref[:, c0:c0 + cols])

    def q_piece(r, cb):
        def emit():
            rs = slice(r * half, (r + 1) * half)
            y = project(rs, col0["q"] + cb * MXU_COLS, MXU_COLS)
            per = MXU_COLS // NSA_HD
            blocks = []
            for h in range(cb * per, (cb + 1) * per):
                pair = y[:, (h % per // 2) * LANES:(h % per // 2 + 1) * LANES]
                g = h // HEADS_PER_GROUP
                data = pair if h % 2 == g else swap(pair)
                feat = qfeat_ref[:, h * LANES:(h + 1) * LANES]
                blocks.append(jnp.where(lower == (g == 0), data, feat))
            out["q"][rs, cb * per * LANES:(cb + 1) * per * LANES] = (
                jnp.concatenate(blocks, axis=1).astype(out["q"].dtype))
        return emit

    small = [seg[0] for seg in SEGMENTS if seg[3] == LANES]
    assert all(col0[b] - col0[a] == LANES for a, b in zip(small, small[1:]))

    def small_piece(r, names):
        def emit():
            rs = slice(r * half, (r + 1) * half)
            y_small = project(rs, col0[names[0]], len(names) * LANES)
            kpos = tile * rows + r * half + lax.broadcasted_iota(jnp.int32, (half, LANES), 0)
            blk = kpos // SEL_BLOCK
            digit = lambda f0: jnp.where(lane == f0, blk.astype(F32),
                                         jnp.where(lane == f0 + 1, (kpos % SEL_BLOCK).astype(F32), 0.0))
            is_pen = lambda f0: (lane == f0 + 1 + blk) & (blk >= 1) & (blk <= seq // SEL_BLOCK - 2)
            for n, name in enumerate(names):
                y = y_small[:, n * LANES:(n + 1) * LANES]
                if name in ("ks", "kw"):
                    feat = [digit(f0) + jnp.where(is_pen(f0), -SEL_PENALTY, 0.0) if name == "ks" else digit(f0)
                            for f0 in halves]
                    y = jnp.concatenate([jnp.where(lower == (g == 0), y, feat[g])
                                         for g in range(NSA_KV_GROUPS)], axis=1)
                elif name in ("vs", "vw"):
                    y = jnp.concatenate([jnp.where(lower, y if g == 0 else swap(y), 1.0)
                                         for g in range(NSA_KV_GROUPS)], axis=1)
                out[name][rs, :] = y.astype(out[name].dtype)
        return emit

    def hg_piece(r, cb):
        def emit():
            rs = slice(r * half, (r + 1) * half)
            y_scr[rs, cb * MXU_COLS:(cb + 1) * MXU_COLS] = project(rs, col0["hg"] + cb * MXU_COLS, MXU_COLS)
        return emit

    prepare, finish, n_slots = _hgrn_chunks(lbraw_ref[...], onorm_ref[...], layer)
    n_chunks = rows // HG_CHUNK
    chunk_rows = lambda ci: slice(ci * HG_CHUNK, (ci + 1) * HG_CHUNK)
    hg_tiles = 4 * HG_WIDTH // MXU_COLS
    norm_piece(0)
    hg_piece(0, 0)()
    norm_piece(1)
    for cb in range(1, hg_tiles):
        hg_piece(0, cb)()
    later = [hg_piece(1, cb) for cb in range(hg_tiles)]
    n_early = n_chunks // 2 - 1
    rest = [p for r in range(2) for p in (q_piece(r, 0), q_piece(r, 1), small_piece(r, small[0:2]),
                                          small_piece(r, small[2:4]), small_piece(r, small[4:]))]
    spread = lambda items, n: [items[i * len(items) // n:(i + 1) * len(items) // n] for i in range(n)]
    tiles = spread(later, n_early) + spread(rest, n_chunks - n_early)
    last = tiles[-1].pop()
    prepared = prepare(y_scr, chunk_rows(0))
    for ci in range(n_chunks):
        slots = [[] for _ in range(n_slots)]
        for j, emit in enumerate(tiles[ci]):
            slots[j * n_slots // len(tiles[ci])].append(emit)
        following = []
        if ci + 1 < n_chunks:
            slots[1].append(lambda ci=ci: following.append(prepare(y_scr, chunk_rows(ci + 1))))
        out["hg"][chunk_rows(ci), :] = finish(prepared, y_scr, chunk_rows(ci), st_scr, slots)
        prepared = following[0] if following else None
    last()


def _q_features():
    feat = np.zeros((1, NSA_HEADS * LANES), np.float32)
    for h in range(NSA_HEADS):
        f0 = h * LANES + NSA_HD * (1 - h // HEADS_PER_GROUP)
        feat[0, f0] = SEL_BLOCK * ALIBI_SLOPES[h]
        feat[0, f0 + 1] = ALIBI_SLOPES[h]
    return jnp.asarray(feat)


def _build_w_in(w_in):
    sizes = (NSA_WIDTH, KV_W, KV_W, KV_W, KV_W, KV_W, KV_W, NSA_HEADS * 3,
             HG_WIDTH, HG_WIDTH, HG_WIDTH, HG_WIDTH)
    splits = [int(v) for v in np.cumsum(sizes)[:-1]]
    w_in = w_in.astype(BF16)
    wq, wkc, wvc, wks, wvs, wkw, wvw, wgl, whq, whf, whi, whg = jnp.split(w_in, splits, axis=-1)
    lead = w_in.shape[:-1]
    gl = jnp.swapaxes(wgl.reshape(lead + (NSA_HEADS, 3)), -1, -2).reshape(lead + (3 * NSA_HEADS,))
    gl = jnp.pad(gl, [(0, 0)] * len(lead) + [(0, LANES - 3 * NSA_HEADS)])
    cols = [wq * NSA_HD ** -0.5, wks, wvs, wkw, wvw, wkc, wvc, gl, whq, whf, whi, whg]
    return jnp.concatenate(cols, axis=-1)


def _inproj(h, norm_g, w_ext, lb_raw, out_norm, seq, layer):
    t = h.shape[0]
    assert seq % ROW_TILE == 0 and ROW_TILE % HG_CHUNK == 0
    row = lambda w: pl.BlockSpec((ROW_TILE, w), lambda i: (i, 0))
    norm_g, out_norm = norm_g[:, None, :], out_norm[:, None, :]
    return pl.pallas_call(
        functools.partial(_inproj_kernel, seq=seq, layer=layer),
        grid=(t // ROW_TILE,),
        in_specs=[row(D_MODEL), _resident_layer(norm_g.shape, layer), _resident_layer(w_ext.shape, layer),
                  _resident((1, NSA_HEADS * LANES)), _resident(lb_raw.shape),
                  _resident_layer(out_norm.shape, layer)],
        out_specs=[row(w) for _, w, _, _ in SEGMENTS],
        out_shape=[jax.ShapeDtypeStruct((t, w), dt) for _, w, dt, _ in SEGMENTS],
        scratch_shapes=[pltpu.VMEM((ROW_TILE, D_MODEL), BF16), pltpu.VMEM((ROW_TILE, 4 * HG_WIDTH), F32),
                        pltpu.VMEM((HG_HEADS, HG_DV, HG_DK), F32)],
        compiler_params=_params(("arbitrary",)),
        name="inproj_hgrn",
    )(h, norm_g, w_ext, _q_features(), lb_raw, out_norm)


def _gelu_tanh(x):
    return 0.5 * x * (1.0 + jnp.tanh(0.7978845608028654 * (x + 0.044715 * (x * x * x))))


def _compress_kernel(kv_ref, pos_ref, w1_ref, w2_ref, o_ref):
    nbp = o_ref.shape[1]
    tokens = [kv_ref[0, pl.ds(l, nbp, stride=CMP_STRIDE), :] for l in range(CMP_STRIDE)]
    chunk = lambda first: jnp.concatenate(
        [(tokens[l] + pos_ref[first + l:first + l + 1, :]).astype(BF16) for l in range(CMP_STRIDE)], axis=1)
    slab = lambda first: w1_ref[first:first + CMP_STRIDE].reshape(CMP_STRIDE * KV_W, NSA_KV_GROUPS * CMP_HID)
    ha = _nn(chunk(0), slab(0))
    hb = _nn(chunk(CMP_STRIDE), slab(CMP_STRIDE))
    act = _gelu_tanh(ha + pltpu.roll(hb, nbp - 1, axis=0)).astype(BF16)
    out = jnp.zeros(o_ref.shape[1:], F32)
    for g in range(NSA_KV_GROUPS):
        out = out + _nn(act[:, g * CMP_HID:(g + 1) * CMP_HID], w2_ref[g])
    o_ref[0] = out


def _compress_weights(pos, w1, w2, reps):
    layers = w1.shape[0]
    w1l = w1.astype(BF16).reshape(layers, CMP_LEN, NSA_HD, CMP_HID)
    z1 = jnp.zeros_like(w1l)
    w1p = jnp.concatenate([jnp.concatenate([w1l, z1], axis=3), jnp.concatenate([z1, w1l], axis=3)], axis=2)
    pos2 = jnp.concatenate([pos] * NSA_KV_GROUPS, axis=2)
    w2 = w2.astype(BF16)
    zero = jnp.zeros_like(w2)
    w2p = jnp.stack([jnp.concatenate([w2 if r == g else zero for r in reps], axis=2)
                     for g in range(NSA_KV_GROUPS)], axis=1)
    return pos2, w1p, w2p


def _compress(kv, layer, pos2, w1p, w2p):
    b, s, _ = kv.shape
    nbp = s // CMP_STRIDE
    width = w2p.shape[-1]
    return pl.pallas_call(
        _compress_kernel,
        grid=(b,),
        in_specs=[pl.BlockSpec((1, s, KV_W), lambda i: (i, 0, 0))]
        + [_resident_layer(a.shape, layer) for a in (pos2, w1p, w2p)],
        out_specs=pl.BlockSpec((1, nbp, width), lambda i: (i, 0, 0)),
        out_shape=jax.ShapeDtypeStruct((b, nbp, width), F32),
        compiler_params=_params(("parallel",)),
        name="compress",
    )(kv, pos2, w1p, w2p)


def _pair_blocks(g, jj):
    lower = slice(0, LANES) if g == 0 else slice(LANES, 2 * LANES)
    upper = slice(LANES, 2 * LANES) if g == 0 else slice(0, LANES)
    return lower, upper


def _cmp_kernel(*refs, tq, ns, n_sel):
    total = refs[1].shape[1]
    per_tile = tq // CMP_STRIDE
    for t in range(total // per_tile):
        nbp = min(total, ((t + 1) * per_tile + LANES - 1) // LANES * LANES)
        sel_rows = min(ns, (t + 1) * tq // SEL_BLOCK)
        rounds = 0 if sel_rows <= n_sel else n_sel - 3

        @pl.when(pl.program_id(1) == t)
        def _(nbp=nbp, sel_rows=sel_rows, rounds=rounds):
            _cmp_tile(*refs, tq=tq, ns=ns, nbp=nbp, sel_rows=sel_rows, rounds=rounds)


def _cmp_tile(q_ref, kc_ref, vc2_ref, mt_ref, ocmp_ref, sel_ref, picks_ref, *, tq, ns, nbp, sel_rows, rounds):
    q0 = pl.program_id(1) * tq
    pos = q0 + lax.broadcasted_iota(jnp.int32, (tq, nbp), 0)
    blk_end = lax.broadcasted_iota(jnp.int32, (tq, nbp), 1) * CMP_STRIDE + (CMP_LEN - 1)
    valid = blk_end <= pos
    row_ok = (q0 + lax.broadcasted_iota(jnp.int32, (tq, 1), 0)) >= CMP_LEN - 1
    kc = kc_ref[0, :nbp, :].astype(BF16)
    vc2 = vc2_ref[0, :nbp, :].astype(BF16)
    lane_c = lax.broadcasted_iota(jnp.int32, (nbp, LANES), 1)
    c_idx = lax.broadcasted_iota(jnp.int32, (nbp, LANES), 0)
    per = SEL_BLOCK // CMP_STRIDE
    feat_a = (c_idx // per - q0 // SEL_BLOCK).astype(F32)
    feat_b = ((c_idx % per) * CMP_STRIDE + (CMP_LEN - 1)).astype(F32)
    lane_lo = lax.broadcasted_iota(jnp.int32, (tq, LANES), 1) < NSA_HD
    blk = lax.broadcasted_iota(jnp.int32, (ns, tq), 0)
    blk_r = lax.broadcasted_iota(jnp.int32, (sel_rows, tq), 0)
    blk_f = blk_r.astype(F32)
    pos_t = q0 + lax.broadcasted_iota(jnp.int32, (sel_rows, tq), 1)
    cur = pos_t // SEL_BLOCK
    forced = (blk_r == 0) | (blk_r == cur) | (blk_r == cur - 1)
    causal = blk_r * SEL_BLOCK <= pos_t
    sel_t, picks = [], []
    taken = -jnp.inf
    for g in range(NSA_KV_GROUPS):
        f0 = NSA_HD * (1 - g)
        feat = jnp.where(lane_c == f0, feat_a, jnp.where(lane_c == f0 + 1, feat_b, 0.0))
        kc_g = jnp.where((lane_c // NSA_HD) == g, kc, feat.astype(BF16))
        imp = jnp.zeros((tq, nbp), F32)
        acc = []
        for hh in range(HEADS_PER_GROUP):
            h = g * HEADS_PER_GROUP + hh
            s = _nt(q_ref[0, :, h * LANES:(h + 1) * LANES], kc_g)
            s = jnp.where(valid, s, NEG)
            e = jnp.exp(s - jnp.max(s, axis=-1, keepdims=True))
            inv = jnp.where(row_ok, 1.0 / jnp.sum(e, axis=-1, keepdims=True), 0.0)
            p = e * inv
            imp = imp + p
            acc.append(_nn(p.astype(BF16), vc2))
        for jj in range(HEADS_PER_GROUP // 2):
            lower, upper = _pair_blocks(g, jj)
            blk_out = jnp.where(lane_lo, acc[2 * jj][:, lower], acc[2 * jj + 1][:, upper])
            c0 = (g * HEADS_PER_GROUP + 2 * jj) * NSA_HD
            ocmp_ref[0, :, c0:c0 + LANES] = blk_out
        p_slc = sum(_nt(mt_ref[:sel_rows, :nbp], part) for part in _split3(imp))
        if rounds == 0:
            score = jnp.where(forced | causal, taken, NEG)
        else:
            score = jnp.where(forced, taken, jnp.where(causal, p_slc, NEG))
        for _ in range(rounds):
            top = jnp.max(score, axis=0, keepdims=True)
            first = jnp.min(jnp.where(score == top, blk_f, float(ns)), axis=0, keepdims=True)
            score = jnp.where(blk_f == first, taken, score)
        if sel_rows < ns:
            score = jnp.concatenate([score, jnp.full((ns - sel_rows, tq), NEG, F32)], axis=0)
        picked = jnp.where(score == taken, 1.0, 0.0).astype(BF16)
        picks.append([_nt(jnp.ones((SUBLANES, ATT_TQ), BF16), picked[:, t * ATT_TQ:(t + 1) * ATT_TQ])
                      for t in range(tq // ATT_TQ)])
        pen = jnp.where(score == taken, 0.0, SEL_PENALTY)
        pen = jnp.where((blk >= 2) & (blk <= ns - 1), pltpu.roll(pen, 1, axis=0), 0.0)
        if ns < NSA_HD:
            pen = jnp.concatenate([pen, jnp.zeros((NSA_HD - ns, tq), F32)], axis=0)
        zero = jnp.zeros((NSA_HD, tq), F32)
        sel_t += [zero, pen] if g == 0 else [pen, zero]
    sel_ref[0] = jnp.concatenate(sel_t, axis=0).T.astype(BF16)
    for t in range(tq // ATT_TQ):
        picks_ref[0, t] = jnp.concatenate([group[t] for group in picks], axis=1)


def _importance_matrix(nbp, ns):
    per = SEL_BLOCK // CMP_STRIDE
    m = np.zeros((ns, nbp), np.float32)
    for n in range(ns):
        for c in range(per * n, per * (n + 1)):
            for cc in (c - 1, c):
                if 0 <= cc < nbp - 1:
                    m[n, cc] += 1.0
    return jnp.asarray(m, BF16)


def _cmp_attention(q, kc, vc2, b, s):
    nbp = s // CMP_STRIDE
    ns = s // SEL_BLOCK
    tq = min(CMP_TQ, s)
    assert ns <= NSA_HD
    assert min(SEL_TOPK, ns) >= 3 and tq % ATT_TQ == 0 and tq % (SUBLANES * SEL_BLOCK) == 0
    q3 = q.reshape(b, s, NSA_HEADS * LANES)
    kern = functools.partial(_cmp_kernel, tq=tq, ns=ns, n_sel=min(SEL_TOPK, ns))
    return pl.pallas_call(
        kern,
        grid=(b, s // tq),
        in_specs=[pl.BlockSpec((1, tq, NSA_HEADS * LANES), lambda i, j: (i, j, 0)),
                  pl.BlockSpec((1, nbp, KV_W), lambda i, j: (i, 0, 0)),
                  pl.BlockSpec((1, nbp, 2 * KV_W), lambda i, j: (i, 0, 0)),
                  _resident((ns, nbp))],
        out_specs=[pl.BlockSpec((1, tq, NSA_WIDTH), lambda i, j: (i, j, 0)),
                   pl.BlockSpec((1, tq, NSA_KV_GROUPS * LANES), lambda i, j: (i, j, 0)),
                   pl.BlockSpec((1, tq // ATT_TQ, SUBLANES, NSA_KV_GROUPS * ns), lambda i, j: (i, j, 0, 0))],
        out_shape=[jax.ShapeDtypeStruct((b, s, NSA_WIDTH), F32),
                   jax.ShapeDtypeStruct((b, s, NSA_KV_GROUPS * LANES), BF16),
                   jax.ShapeDtypeStruct((b, s // ATT_TQ, SUBLANES, NSA_KV_GROUPS * ns), F32)],
        compiler_params=_params(("parallel", "parallel")),
        name="cmp_topk",
    )(q3, kc, vc2, _importance_matrix(nbp, ns))


def _attn_kernel(tiles_ref, ntiles_ref, q_ref, ks_ref, vs_ref, kw_ref, vw_ref, penq_ref, gl_ref, gexp_ref,
                 ocmp_ref, o_ref, qa_scr, s_scr, p_scr, al_scr, m_scr, acc_scr, *, tq, tk):
    i = pl.program_id(2)
    rows_of = lambda hh: slice(hh * tq, (hh + 1) * tq)
    lane_lo = lax.broadcasted_iota(jnp.int32, (tq, LANES), 1) < NSA_HD
    rel = (lax.broadcasted_iota(jnp.int32, (tq, tk), 1)
           - lax.broadcasted_iota(jnp.int32, (tq, tk), 0)).astype(F32)

    for hh in range(HEADS_PER_GROUP):
        qa_scr[rows_of(hh)] = q_ref[0, :, hh * LANES:(hh + 1) * LANES] + penq_ref[0]

    def scores(k_ref, j, slot):
        s_scr[slot] = _nt(qa_scr[...], k_ref[0, pl.ds(pl.multiple_of(j * tk, tk), tk), :])

    def probs(br, j, mode, slot, first, exists=None):
        for hh in range(HEADS_PER_GROUP):
            r = rows_of(hh)
            s = s_scr[slot, r]
            if mode == "causal":
                s = jnp.where(rel <= 0.0, s, NEG)
            elif mode == "lower":
                s = jnp.where(rel + (j * tk - i * tq).astype(F32) > -float(WINDOW), s, NEG)
            if exists is not None:
                s = jnp.where(exists, s, NEG)
            m_cur = jnp.max(s, axis=-1, keepdims=True)
            if first:
                m_new = jnp.broadcast_to(m_cur, (tq, LANES))
            else:
                m_prev = m_scr[br, r]
                m_new = jnp.maximum(m_prev, m_cur)
                al_scr[slot, r] = jnp.exp(m_prev - m_new)
            m_scr[br, r] = m_new
            p_scr[slot, r] = jnp.exp(s - jnp.concatenate([m_new] * (tk // LANES), axis=1)).astype(BF16)

    def accumulate(br, v_ref, j, slot, first):
        pv = _nn(p_scr[slot], v_ref[0, pl.ds(pl.multiple_of(j * tk, tk), tk), :])
        for hh in range(HEADS_PER_GROUP):
            r = rows_of(hh)
            if first:
                acc_scr[br, r] = pv[r]
            else:
                acc_scr[br, r] = al_scr[slot, r] * acc_scr[br, r] + pv[r]

    n_q = pl.num_programs(2)
    step = (pl.program_id(0) * n_q + i) * pl.num_programs(1) + pl.program_id(1)
    n_tiles = ntiles_ref[step]
    item = lambda br, j, mode, first, exists=None: (br, j, mode, first, exists)
    sel_item = lambda n: item(0, i, "causal", True) if isinstance(n, int) and n == 0 else item(
        0, tiles_ref[step * n_q + n - 1], None, False)
    n_back = WINDOW // tk
    win_items = [item(1, i, "causal", True)] + [
        item(1, jnp.maximum(i - d, 0), "lower" if d == n_back else None, False, i >= d)
        for d in range(1, n_back + 1)]
    k_refs, v_refs = (ks_ref, kw_ref), (vs_ref, vw_ref)

    def run(items, slot0, done=(0, 0, 0), upto=None):
        n_items = len(items)
        slot = lambda n: (slot0 + n) % 2
        n_s, n_p, n_a = done

        def do_scores(n):
            br, j, _, _, _ = items[n]
            scores(k_refs[br], j, slot(n))

        def do_probs(n):
            br, j, mode, first, exists = items[n]
            probs(br, j, mode, slot(n), first, exists)

        def do_acc(n):
            br, j, _, first, _ = items[n]
            accumulate(br, v_refs[br], j, slot(n), first)

        for n in range(n_s, min(n_a + 2, n_items)):
            do_scores(n)
        n_s = max(n_s, min(n_a + 2, n_items))
        for n in range(n_p, min(n_a + 1, n_items)):
            do_probs(n)
        n_p = max(n_p, min(n_a + 1, n_items))
        for k in range(n_a, n_items if upto is None else upto):
            if n_s <= k + 2 < n_items:
                do_scores(k + 2)
                n_s = k + 3
            if n_p <= k + 1 < n_items:
                do_probs(k + 1)
                n_p = k + 2
            do_acc(k)

    for n_static in range(ATT_STATIC_TILES):
        @pl.when(n_tiles == n_static)
        def _(n_static=n_static):
            run([sel_item(n) for n in range(n_static + 1)] + win_items, 0)

    assert ATT_STATIC_TILES >= 2
    @pl.when(n_tiles >= ATT_STATIC_TILES)
    def _():
        run([sel_item(n) for n in range(3)], 0, upto=1)

        def sel_step(n, parity):
            scores(ks_ref, sel_item(n + 2)[1], parity)
            probs(0, None, None, 1 - parity, False)
            accumulate(0, vs_ref, sel_item(n)[1], parity, False)

        def sel_steps(first, count):
            for d in range(count):
                sel_step(first + d, (1 + d) % 2)

        n_pairs_of_steps = (n_tiles - 2) // 2
        n_quads = n_pairs_of_steps // 2

        def sel_four_steps(t, carry):
            sel_steps(4 * t + 1, 4)
            return carry

        lax.fori_loop(0, n_quads, sel_four_steps, 0)

        @pl.when(n_pairs_of_steps % 2 == 1)
        def _():
            sel_steps(4 * n_quads + 1, 2)

        def drain(parity):
            tail = [sel_item(n_tiles - 1), sel_item(n_tiles)] + win_items
            run(tail, 1 - parity, done=(2, 1, 0))

        @pl.when(n_tiles % 2 == 0)
        def _():
            drain(0)

        @pl.when(n_tiles % 2 == 1)
        def _():
            sel_step(n_tiles - 2, 1)
            drain(1)

    gexp = sum(_nn(part, gexp_ref[0]) for part in _split2(_sigmoid(gl_ref[0])))
    n_pairs = HEADS_PER_GROUP // 2
    gate_blk = lambda br, jj: gexp[:, (br * n_pairs + jj) * LANES:(br * n_pairs + jj + 1) * LANES]
    for jj in range(n_pairs):
        cols = slice(jj * LANES, (jj + 1) * LANES)
        blk = gate_blk(0, jj) * ocmp_ref[0, :, cols]
        for br in range(2):
            even = acc_scr[br, rows_of(2 * jj)]
            odd = acc_scr[br, rows_of(2 * jj + 1)]
            low = even / pltpu.roll(even, NSA_HD, axis=1)
            up = pltpu.roll(odd, NSA_HD, axis=1) / odd
            blk = blk + gate_blk(br + 1, jj) * jnp.where(lane_lo, low, up)
        o_ref[0, :, cols] = blk.astype(o_ref.dtype)


def _gate_expansion():
    n_pairs = HEADS_PER_GROUP // 2
    r = np.zeros((NSA_KV_GROUPS, LANES, 3 * n_pairs * LANES), np.float32)
    for g in range(NSA_KV_GROUPS):
        for br in range(3):
            for jj in range(n_pairs):
                for odd in range(2):
                    src = br * NSA_HEADS + g * HEADS_PER_GROUP + 2 * jj + odd
                    dst = (br * n_pairs + jj) * LANES + odd * NSA_HD
                    r[g, src, dst:dst + NSA_HD] = 1.0
    return jnp.asarray(r, BF16)


def _picked_tiles(picks, b, s):
    ns, nq = s // SEL_BLOCK, s // ATT_TQ
    per_tile = ATT_TK // SEL_BLOCK
    per_block = picks[:, :, 0, :].reshape(b, nq, NSA_KV_GROUPS, ns // per_tile, per_tile)
    j = jnp.arange(ns // per_tile, dtype=jnp.int32)
    i = jnp.arange(nq, dtype=jnp.int32)[None, :, None, None]
    active = ((per_block.sum(-1) > 0) | (j == 0)) & (j < i)
    slot = jnp.cumsum(active, axis=-1) - 1
    hit = active[..., None, :] & (slot[..., None, :] == j[:, None])
    tiles = jnp.sum(jnp.where(hit, j, 0), axis=-1).astype(jnp.int32)
    return tiles.reshape(-1), active.sum(-1).astype(jnp.int32).reshape(-1)


def _attention(q, ks, vs, kw, vw, pen, picks, gl, ocmp, b, s):
    tq, tk = ATT_TQ, ATT_TK
    assert tq == tk and WINDOW % tk == 0 and WINDOW // tk <= 2 and tq % SEL_BLOCK == 0
    gw = HEADS_PER_GROUP * LANES
    ow = HEADS_PER_GROUP * NSA_HD
    r3 = lambda a: a.reshape(b, s, a.shape[-1])
    tile = lambda w: pl.BlockSpec((1, tq, w), lambda i, g, j, *_: (i, j, 0))
    gtile = lambda w: pl.BlockSpec((1, tq, w), lambda i, g, j, *_: (i, j, g))
    gfull = pl.BlockSpec((1, s, LANES), lambda i, g, j, *_: (i, 0, g))
    gexp = _gate_expansion()
    rows = HEADS_PER_GROUP * tq
    kern = functools.partial(_attn_kernel, tq=tq, tk=tk)
    tiles, n_tiles = _picked_tiles(picks, b, s)
    grid_spec = pltpu.PrefetchScalarGridSpec(
        num_scalar_prefetch=2,
        grid=(b, NSA_KV_GROUPS, s // tq),
        in_specs=[gtile(gw), gfull, gfull, gfull, gfull, gtile(LANES), tile(LANES),
                  pl.BlockSpec((1,) + gexp.shape[1:], lambda i, g, j, *_: (g, 0, 0)), gtile(ow)],
        out_specs=gtile(ow),
        scratch_shapes=[pltpu.VMEM((rows, LANES), BF16), pltpu.VMEM((2, rows, tk), F32),
                        pltpu.VMEM((2, rows, tk), BF16), pltpu.VMEM((2, rows, LANES), F32),
                        pltpu.VMEM((2, rows, LANES), F32), pltpu.VMEM((2, rows, LANES), F32)])
    return pl.pallas_call(
        kern,
        grid_spec=grid_spec,
        out_shape=jax.ShapeDtypeStruct((b, s, NSA_WIDTH), BF16),
        compiler_params=_params(("parallel", "parallel", "parallel")),
        name="sel_win_attention",
    )(tiles, n_tiles, r3(q), r3(ks), r3(vs), r3(kw), r3(vw), pen, r3(gl), gexp, ocmp)


def _hgrn_chunks(raw, onorm, layer):
    c = HG_CHUNK
    ex = jnp.exp(raw - jnp.max(raw, axis=0, keepdims=True))
    sm = ex / jnp.sum(ex, axis=0, keepdims=True)
    lb_all = jnp.zeros((1, raw.shape[1]), F32)
    for l in range(1, layer + 1):
        lb_all = lb_all + sm[l:l + 1, :]

    t_idx = lax.broadcasted_iota(jnp.int32, (c, HG_DK), 0)
    sub = lax.broadcasted_iota(jnp.int32, (SUBLANES, HG_DK), 0)
    ti = lax.broadcasted_iota(jnp.int32, (c, c), 0)
    si = lax.broadcasted_iota(jnp.int32, (c, c), 1)
    tril = jnp.where(si <= ti, 1.0, 0.0).astype(BF16)
    levels = (32, 16, 8, 4, 2, 1)
    hc = HG_HEADS * c
    tb = lax.broadcasted_iota(jnp.int32, (hc, hc), 0)
    sb = lax.broadcasted_iota(jnp.int32, (hc, hc), 1)
    same_head = (tb // c) == (sb // c)
    split_bit = tb ^ sb
    pair_mask = {m: jnp.where(same_head & (tb > sb) & (split_bit >= m) & (split_bit < 2 * m), 1.0, 0.0)
                 for m in levels}
    upper_groups = {m: [h * c + s0 + m + d for h in range(HG_HEADS) for s0 in range(0, c, 2 * m)
                        for d in range(0, m, SUBLANES)] for m in levels if m % SUBLANES == 0}
    upper_mask = {m: jnp.concatenate([pair_mask[m][g:g + SUBLANES] for g in groups], axis=0)
                  for m, groups in upper_groups.items()}
    diagonal = tb == sb
    second_half ={m: (t_idx & m) != 0 for m in levels}
    sign = {m: jnp.where(second_half[m], 1.0, -1.0) for m in levels}

    def ref_rows(b, m):
        row = lambda r, n: jnp.broadcast_to(b[r:r + 1, :], (n, HG_DK))
        if m >= 4:
            return jnp.concatenate([row(s0 + m - 1, 2 * m) for s0 in range(0, c, 2 * m)], axis=0)
        return jnp.concatenate([jnp.where(sub < 4, row(s0 + 1, SUBLANES), row(s0 + 5, SUBLANES))
                                for s0 in range(0, c, SUBLANES)], axis=0)

    part = lambda y_ref, rows, which, h: y_ref[rows, which * HG_WIDTH + h * HG_DK:which * HG_WIDTH + (h + 1) * HG_DK]

    def prepare(y_ref, rows):
        q, k, f, v, logf = [], [], [], [], []
        for h in range(HG_HEADS):
            lb = lb_all[:, h * HG_DK:(h + 1) * HG_DK]
            z = part(y_ref, rows, 1, h)
            hq = part(y_ref, rows, 0, h)
            q.append(hq * _sigmoid(hq))
            ez = jnp.exp(-jnp.abs(z))
            big = 1.0 / (1.0 + ez)
            small = ez * big
            f.append(jnp.maximum(lb + (1.0 - lb) * jnp.where(z >= 0.0, big, small), F_FLOOR))
            logf.append(jnp.log2(f[h]))
            k.append((1.0 - lb) * jnp.where(z >= 0.0, small, big))
            v.append(part(y_ref, rows, 2, h).astype(BF16))
        parts = jnp.concatenate([p for h in range(HG_HEADS) for p in _split3(logf[h])], axis=1)
        csum = _nn(tril, parts)
        bcum = [sum(csum[:, (3 * h + i) * HG_DK:(3 * h + i + 1) * HG_DK] for i in range(3))
                for h in range(HG_HEADS)]
        return q, k, f, v, bcum

    def finish(prepared, y_ref, rows, st_scr, fill):
        q, k, f, v, bcum = prepared
        rowdot = jnp.concatenate([jnp.sum(q[h] * k[h], axis=-1, keepdims=True) for h in range(HG_HEADS)], axis=0)
        a = jnp.where(diagonal, rowdot, 0.0)
        for m, emits in zip(levels, fill, strict=True):
            for emit in emits:
                emit()
            r = []
            for h in range(HG_HEADS):
                if m == 1:
                    r.append(jnp.where(second_half[m], q[h] * f[h], k[h]))
                else:
                    w = jnp.exp2((bcum[h] - ref_rows(bcum[h], m)) * sign[m])
                    r.append(jnp.where(second_half[m], q[h], k[h]) * w)
            r = jnp.concatenate(r, axis=0)
            if m in upper_groups:
                lhs = jnp.concatenate([r[g:g + SUBLANES] for g in upper_groups[m]], axis=0).astype(BF16)
                p = _nt(lhs, r.astype(BF16)) * upper_mask[m]
                parts = {g: p[n * SUBLANES:(n + 1) * SUBLANES] for n, g in enumerate(upper_groups[m])}
                a = jnp.concatenate([a[g:g + SUBLANES] + parts[g] if g in parts else a[g:g + SUBLANES]
                                     for g in range(0, hc, SUBLANES)], axis=0)
            else:
                r = r.astype(BF16)
                a = a + _nt(r, r) * pair_mask[m]
        o_intra = _nn(a.astype(BF16), jnp.concatenate(v, axis=0))

        heads = []
        for h in range(HG_HEADS):
            st = st_scr[h]
            o = o_intra[h * c:(h + 1) * c] + _nt((q[h] * jnp.exp2(bcum[h])).astype(BF16), st.astype(BF16))
            b_last = bcum[h][c - 1:c, :]
            kd = (k[h] * jnp.exp2(b_last - bcum[h])).astype(BF16)
            st_scr[h] = jnp.exp2(b_last) * st + _tn(v[h], kd)

            o = o * lax.rsqrt(jnp.mean(o * o, axis=-1, keepdims=True) + EPS) * onorm
            gate = part(y_ref, rows, 3, h)
            heads.append((o * (gate * _sigmoid(gate))).astype(BF16))
        return jnp.concatenate(heads, axis=1)

    return prepare, finish, len(levels)


def _mixer(h, l, b, s, mix_norm, w_ext, cmp_k, cmp_v, hgrn_lower_bound, hgrn_out_norm):
    q, ks, vs, kw, vw, kc_in, vc_in, gl, o_hg = _inproj(h, mix_norm, w_ext, hgrn_lower_bound, hgrn_out_norm, s, l)
    kc = _compress(kc_in.reshape(b, s, KV_W), l, *cmp_k)
    vc2 = _compress(vc_in.reshape(b, s, KV_W), l, *cmp_v)
    ocmp, sel, picks = _cmp_attention(q, kc, vc2, b, s)
    o_nsa = _attention(q, ks, vs, kw, vw, sel, picks, gl, ocmp, b, s)
    return o_nsa.reshape(b * s, NSA_WIDTH), o_hg


def kernel(x, ffn1_norm, ffn1_w_gu, ffn1_w_down, mix_norm, w_in, cmp_pos_k, cmp_pos_v, cmp_k_w1, cmp_k_w2, cmp_v_w1, cmp_v_w2, hgrn_lower_bound, hgrn_out_norm, w_out, ffn2_norm, ffn2_w_gu, ffn2_w_down, final_norm):
    b, s, d = x.shape
    depth = ffn1_norm.shape[0]
    ffn1_w = _ffn_weights(ffn1_w_gu, ffn1_w_down)
    ffn2_w = _ffn_weights(ffn2_w_gu, ffn2_w_down)
    w_out = w_out.astype(BF16)
    w_ext = _build_w_in(w_in)
    cmp_k = _compress_weights(cmp_pos_k, cmp_k_w1, cmp_k_w2, (0, 1))
    cmp_v = _compress_weights(cmp_pos_v, cmp_v_w1, cmp_v_w2, (0, 1, 1, 0))
    h = x.reshape(b * s, d)
    for l in range(depth):
        h = _ffn(h, l, ffn1_norm, *ffn1_w, final_norm, False)
        o_nsa, o_hg = _mixer(h, l, b, s, mix_norm, w_ext, cmp_k, cmp_v, hgrn_lower_bound, hgrn_out_norm)
        h = _ffn(h, l, ffn2_norm, *ffn2_w, final_norm, l == depth - 1, (o_nsa, o_hg, w_out))
    return h.reshape(b, s, d)
```

```python
import functools

import jax
import jax.numpy as jnp
import numpy as np
from jax import lax
from jax.experimental import pallas as pl
from jax.experimental.pallas import tpu as pltpu

F32 = jnp.float32
BF16 = jnp.bfloat16

D_MODEL = 1024
EPS = 1e-6
NEG = -1e30
F_FLOOR = 1e-30
NSA_HEADS = 8
NSA_KV_GROUPS = 2
HEADS_PER_GROUP = NSA_HEADS // NSA_KV_GROUPS
NSA_HD = 64
CMP_LEN = 32
CMP_STRIDE = 16
CMP_HID = 256
SEL_BLOCK = 64
SEL_TOPK = 16
WINDOW = 512
HG_HEADS = 4
HG_DK = 128
HG_DV = 128
HG_CHUNK = 64
D_FF = 2752
NSA_WIDTH = NSA_HEADS * NSA_HD
HG_WIDTH = HG_HEADS * HG_DV
KV_W = NSA_KV_GROUPS * NSA_HD

LANES = 128
SUBLANES = 8
MXU_COLS = 256
FF_CHUNK = 256
D_FF_PAD = -(-D_FF // FF_CHUNK) * FF_CHUNK
ROW_TILE = 512
FFN_ROWS = 1024
ATT_TQ = 256
ATT_TK = 256
ATT_STATIC_TILES = 7
CMP_TQ = 1024
VMEM_LIMIT = 56 * 1024 * 1024

SEL_PENALTY = 2.0 ** 50
ALIBI_SLOPES = tuple(2.0 ** (-8.0 * (i + 1) / NSA_HEADS) for i in range(NSA_HEADS))

SEGMENTS = (
    ("q", NSA_HEADS * LANES, BF16, NSA_WIDTH),
    ("ks", NSA_KV_GROUPS * LANES, BF16, KV_W),
    ("vs", NSA_KV_GROUPS * LANES, BF16, KV_W),
    ("kw", NSA_KV_GROUPS * LANES, BF16, KV_W),
    ("vw", NSA_KV_GROUPS * LANES, BF16, KV_W),
    ("kc", KV_W, F32, KV_W),
    ("vc", KV_W, F32, KV_W),
    ("gl", LANES, F32, LANES),
    ("hg", HG_WIDTH, BF16, 4 * HG_WIDTH),
)
SEG_OFFSETS = tuple(int(v) for v in np.cumsum([0] + [s[3] for s in SEGMENTS]))


def _nn(a, b):
    return jnp.dot(a, b, preferred_element_type=F32)


def _nt(a, b):
    return lax.dot_general(a, b, (((1,), (1,)), ((), ())), preferred_element_type=F32)


def _tn(a, b):
    return lax.dot_general(a, b, (((0,), (0,)), ((), ())), preferred_element_type=F32)


def _split2(x):
    hi = x.astype(BF16)
    lo = (x - hi.astype(F32)).astype(BF16)
    return hi, lo


def _split3(x):
    hi = x.astype(BF16)
    r = x - hi.astype(F32)
    mid = r.astype(BF16)
    lo = (r - mid.astype(F32)).astype(BF16)
    return hi, mid, lo


def _sigmoid(x):
    return 1.0 / (1.0 + jnp.exp(-x))


def _rms(x, g):
    return x * lax.rsqrt(jnp.mean(x * x, axis=-1, keepdims=True) + EPS) * g


def _resident(shape):
    nd = len(shape)
    return pl.BlockSpec(shape, lambda *_: (0,) * nd, pipeline_mode=pl.Buffered(1))


def _resident_layer(stacked_shape, layer):
    nd = len(stacked_shape)
    return pl.BlockSpec((None,) + tuple(stacked_shape[1:]), lambda *_: (layer,) + (0,) * (nd - 1),
                        pipeline_mode=pl.Buffered(1))


def _params(sem):
    return pltpu.CompilerParams(dimension_semantics=sem, vmem_limit_bytes=VMEM_LIMIT)


def _ffn_kernel(*refs, final, proj):
    if proj:
        x_ref, a_ref, b_ref, wo_ref, g_ref, wg_ref, wu_ref, wd_ref, gf_ref, o_ref = refs
        x = x_ref[...] + _nn(a_ref[...], wo_ref[:NSA_WIDTH, :]) + _nn(b_ref[...], wo_ref[NSA_WIDTH:, :])
    else:
        x_ref, g_ref, wg_ref, wu_ref, wd_ref, gf_ref, o_ref = refs
        x = x_ref[...]
    xn = _rms(x, g_ref[...]).astype(BF16)
    acc = jnp.zeros(x.shape, F32)
    for k in range(D_FF_PAD // FF_CHUNK):
        sl = slice(k * FF_CHUNK, (k + 1) * FF_CHUNK)
        gate = _nn(xn, wg_ref[:, sl])
        up = _nn(xn, wu_ref[:, sl])
        h = (gate * _sigmoid(gate) * up).astype(BF16)
        acc = acc + _nn(h, wd_ref[sl, :])
    y = x + 0.5 * acc
    if final:
        y = _rms(y, gf_ref[...])
    o_ref[...] = y


def _ffn_weights(w_gu, w_down):
    pad = D_FF_PAD - D_FF
    wg = jnp.pad(w_gu[..., :D_FF], ((0, 0), (0, 0), (0, pad))).astype(BF16)
    wu = jnp.pad(w_gu[..., D_FF:], ((0, 0), (0, 0), (0, pad))).astype(BF16)
    wd = jnp.pad(w_down, ((0, 0), (0, pad), (0, 0))).astype(BF16)
    return wg, wu, wd


def _ffn(h, layer, norm_g, wg, wu, wd, final_g, final, proj=None):
    t = h.shape[0]
    row = lambda w: pl.BlockSpec((FFN_ROWS, w), lambda i: (i, 0))
    norm_g = norm_g[:, None, :]
    weights = [_resident_layer(a.shape, layer) for a in (norm_g, wg, wu, wd)] + [_resident((1, D_MODEL))]
    operands = [norm_g, wg, wu, wd, final_g.reshape(1, -1)]
    if proj is None:
        in_specs, args = [row(D_MODEL)] + weights, [h] + operands
    else:
        o_nsa, o_hg, w_out = proj
        in_specs = [row(D_MODEL), row(NSA_WIDTH), row(HG_WIDTH), _resident_layer(w_out.shape, layer)] + weights
        args = [h, o_nsa, o_hg, w_out] + operands
    return pl.pallas_call(
        functools.partial(_ffn_kernel, final=final, proj=proj is not None),
        grid=(t // FFN_ROWS,),
        in_specs=in_specs,
        out_specs=row(D_MODEL),
        out_shape=jax.ShapeDtypeStruct((t, D_MODEL), F32),
        compiler_params=_params(("parallel",)),
        name="ffn",
    )(*args)


def _inproj_kernel(x_ref, g_ref, w_ref, qfeat_ref, lbraw_ref, onorm_ref, *refs, seq, layer):
    *o_refs, xn_scr, y_scr, st_scr = refs
    out = {seg[0]: o_ref for seg, o_ref in zip(SEGMENTS, o_refs, strict=True)}
    col0 = {seg[0]: off for seg, off in zip(SEGMENTS, SEG_OFFSETS[:-1], strict=True)}
    rows = x_ref.shape[0]
    half = rows // 2
    tile = pl.program_id(0) % (seq // rows)

    @pl.when(tile == 0)
    def _():
        st_scr[...] = jnp.zeros(st_scr.shape, F32)

    def norm_piece(r):
        rs = slice(r * half, (r + 1) * half)
        xn_scr[rs, :] = _rms(x_ref[rs, :], g_ref[...]).astype(BF16)

    lane = lax.broadcasted_iota(jnp.int32, (half, LANES), 1)
    lower = lane < NSA_HD
    swap = lambda a: pltpu.roll(a, NSA_HD, axis=1)
    halves = [NSA_HD * (1 - g) for g in range(NSA_KV_GROUPS)]
    project = lambda rs, c0, cols: _nn(xn_scr[rs, :], w_ref[:, c0:c0 + cols])

    def q_piece(r, cb):
        def emit():
            rs = slice(r * half, (r + 1) * half)
            y = project(rs, col0["q"] + cb * MXU_COLS, MXU_COLS)
            per = MXU_COLS // NSA_HD
            blocks = []
            for h in range(cb * per, (cb + 1) * per):
                pair = y[:, (h % per // 2) * LANES:(h % per // 2 + 1) * LANES]
                g = h // HEADS_PER_GROUP
                data = pair if h % 2 == g else swap(pair)
                feat = qfeat_ref[:, h * LANES:(h + 1) * LANES]
                blocks.append(jnp.where(lower == (g == 0), data, feat))
            out["q"][rs, cb * per * LANES:(cb + 1) * per * LANES] = (
                jnp.concatenate(blocks, axis=1).astype(out["q"].dtype))
        return emit

    small = [seg[0] for seg in SEGMENTS if seg[3] == LANES]
    assert all(col0[b] - col0[a] == LANES for a, b in zip(small, small[1:]))

    def small_piece(r, names):
        def emit():
            rs = slice(r * half, (r + 1) * half)
            y_small = project(rs, col0[names[0]], len(names) * LANES)
            kpos = tile * rows + r * half + lax.broadcasted_iota(jnp.int32, (half, LANES), 0)
            blk = kpos // SEL_BLOCK
            digit = lambda f0: jnp.where(lane == f0, blk.astype(F32),
                                         jnp.where(lane == f0 + 1, (kpos % SEL_BLOCK).astype(F32), 0.0))
            is_pen = lambda f0: (lane == f0 + 1 + blk) & (blk >= 1) & (blk <= seq // SEL_BLOCK - 2)
            for n, name in enumerate(names):
                y = y_small[:, n * LANES:(n + 1) * LANES]
                if name in ("ks", "kw"):
                    feat = [digit(f0) + jnp.where(is_pen(f0), -SEL_PENALTY, 0.0) if name == "ks" else digit(f0)
                            for f0 in halves]
                    y = jnp.concatenate([jnp.where(lower == (g == 0), y, feat[g])
                                         for g in range(NSA_KV_GROUPS)], axis=1)
                elif name in ("vs", "vw"):
                    y = jnp.concatenate([jnp.where(lower, y if g == 0 else swap(y), 1.0)
                                         for g in range(NSA_KV_GROUPS)], axis=1)
                out[name][rs, :] = y.astype(out[name].dtype)
        return emit

    def hg_piece(r, cb):
        def emit():
            rs = slice(r * half, (r + 1) * half)
            y_scr[rs, cb * MXU_COLS:(cb + 1) * MXU_COLS] = project(rs, col0["hg"] + cb * MXU_COLS, MXU_COLS)
        return emit

    prepare, finish, n_slots = _hgrn_chunks(lbraw_ref[...], onorm_ref[...], layer)
    n_chunks = rows // HG_CHUNK
    chunk_rows = lambda ci: slice(ci * HG_CHUNK, (ci + 1) * HG_CHUNK)
    hg_tiles = 4 * HG_WIDTH // MXU_COLS
    norm_piece(0)
    hg_piece(0, 0)()
    norm_piece(1)
    for cb in range(1, hg_tiles):
        hg_piece(0, cb)()
    later = [hg_piece(1, cb) for cb in range(hg_tiles)]
    n_early = n_chunks // 2 - 1
    rest = [p for r in range(2) for p in (q_piece(r, 0), q_piece(r, 1), small_piece(r, small[0:2]),
                                          small_piece(r, small[2:4]), small_piece(r, small[4:]))]
    spread = lambda items, n: [items[i * len(items) // n:(i + 1) * len(items) // n] for i in range(n)]
    tiles = spread(later, n_early) + spread(rest, n_chunks - n_early)
    last = tiles[-1].pop()
    prepared = prepare(y_scr, chunk_rows(0))
    for ci in range(n_chunks):
        slots = [[] for _ in range(n_slots)]
        for j, emit in enumerate(tiles[ci]):
            slots[j * n_slots // len(tiles[ci])].append(emit)
        following = []
        if ci + 1 < n_chunks:
            slots[0].append(lambda ci=ci: following.append(prepare(y_scr, chunk_rows(ci + 1))))
        out["hg"][chunk_rows(ci), :] = finish(prepared, y_scr, chunk_rows(ci), st_scr, slots)
        prepared = following[0] if following else None
    last()


def _q_features():
    feat = np.zeros((1, NSA_HEADS * LANES), np.float32)
    for h in range(NSA_HEADS):
        f0 = h * LANES + NSA_HD * (1 - h // HEADS_PER_GROUP)
        feat[0, f0] = SEL_BLOCK * ALIBI_SLOPES[h]
        feat[0, f0 + 1] = ALIBI_SLOPES[h]
    return jnp.asarray(feat)


def _build_w_in(w_in):
    sizes = (NSA_WIDTH, KV_W, KV_W, KV_W, KV_W, KV_W, KV_W, NSA_HEADS * 3,
             HG_WIDTH, HG_WIDTH, HG_WIDTH, HG_WIDTH)
    splits = [int(v) for v in np.cumsum(sizes)[:-1]]
    w_in = w_in.astype(BF16)
    wq, wkc, wvc, wks, wvs, wkw, wvw, wgl, whq, whf, whi, whg = jnp.split(w_in, splits, axis=-1)
    lead = w_in.shape[:-1]
    gl = jnp.swapaxes(wgl.reshape(lead + (NSA_HEADS, 3)), -1, -2).reshape(lead + (3 * NSA_HEADS,))
    gl = jnp.pad(gl, [(0, 0)] * len(lead) + [(0, LANES - 3 * NSA_HEADS)])
    cols = [wq * NSA_HD ** -0.5, wks, wvs, wkw, wvw, wkc, wvc, gl, whq, whf, whi, whg]
    return jnp.concatenate(cols, axis=-1)


def _inproj(h, norm_g, w_ext, lb_raw, out_norm, seq, layer):
    t = h.shape[0]
    assert seq % ROW_TILE == 0 and ROW_TILE % HG_CHUNK == 0
    row = lambda w: pl.BlockSpec((ROW_TILE, w), lambda i: (i, 0))
    norm_g, out_norm = norm_g[:, None, :], out_norm[:, None, :]
    return pl.pallas_call(
        functools.partial(_inproj_kernel, seq=seq, layer=layer),
        grid=(t // ROW_TILE,),
        in_specs=[row(D_MODEL), _resident_layer(norm_g.shape, layer), _resident_layer(w_ext.shape, layer),
                  _resident((1, NSA_HEADS * LANES)), _resident(lb_raw.shape),
                  _resident_layer(out_norm.shape, layer)],
        out_specs=[row(w) for _, w, _, _ in SEGMENTS],
        out_shape=[jax.ShapeDtypeStruct((t, w), dt) for _, w, dt, _ in SEGMENTS],
        scratch_shapes=[pltpu.VMEM((ROW_TILE, D_MODEL), BF16), pltpu.VMEM((ROW_TILE, 4 * HG_WIDTH), F32),
                        pltpu.VMEM((HG_HEADS, HG_DV, HG_DK), F32)],
        compiler_params=_params(("arbitrary",)),
        name="inproj_hgrn",
    )(h, norm_g, w_ext, _q_features(), lb_raw, out_norm)


def _gelu_tanh(x):
    return 0.5 * x * (1.0 + jnp.tanh(0.7978845608028654 * (x + 0.044715 * (x * x * x))))


def _compress_kernel(kv_ref, pos_ref, w1_ref, w2_ref, o_ref):
    nbp = o_ref.shape[1]
    tokens = [kv_ref[0, pl.ds(l, nbp, stride=CMP_STRIDE), :] for l in range(CMP_STRIDE)]
    chunk = lambda first: jnp.concatenate(
        [(tokens[l] + pos_ref[first + l:first + l + 1, :]).astype(BF16) for l in range(CMP_STRIDE)], axis=1)
    slab = lambda first: w1_ref[first:first + CMP_STRIDE].reshape(CMP_STRIDE * KV_W, NSA_KV_GROUPS * CMP_HID)
    ha = _nn(chunk(0), slab(0))
    hb = _nn(chunk(CMP_STRIDE), slab(CMP_STRIDE))
    act = _gelu_tanh(ha + pltpu.roll(hb, nbp - 1, axis=0)).astype(BF16)
    out = jnp.zeros(o_ref.shape[1:], F32)
    for g in range(NSA_KV_GROUPS):
        out = out + _nn(act[:, g * CMP_HID:(g + 1) * CMP_HID], w2_ref[g])
    o_ref[0] = out


def _compress_weights(pos, w1, w2, reps):
    layers = w1.shape[0]
    w1l = w1.astype(BF16).reshape(layers, CMP_LEN, NSA_HD, CMP_HID)
    z1 = jnp.zeros_like(w1l)
    w1p = jnp.concatenate([jnp.concatenate([w1l, z1], axis=3), jnp.concatenate([z1, w1l], axis=3)], axis=2)
    pos2 = jnp.concatenate([pos] * NSA_KV_GROUPS, axis=2)
    w2 = w2.astype(BF16)
    zero = jnp.zeros_like(w2)
    w2p = jnp.stack([jnp.concatenate([w2 if r == g else zero for r in reps], axis=2)
                     for g in range(NSA_KV_GROUPS)], axis=1)
    return pos2, w1p, w2p


def _compress(kv, layer, pos2, w1p, w2p):
    b, s, _ = kv.shape
    nbp = s // CMP_STRIDE
    width = w2p.shape[-1]
    return pl.pallas_call(
        _compress_kernel,
        grid=(b,),
        in_specs=[pl.BlockSpec((1, s, KV_W), lambda i: (i, 0, 0))]
        + [_resident_layer(a.shape, layer) for a in (pos2, w1p, w2p)],
        out_specs=pl.BlockSpec((1, nbp, width), lambda i: (i, 0, 0)),
        out_shape=jax.ShapeDtypeStruct((b, nbp, width), F32),
        compiler_params=_params(("parallel",)),
        name="compress",
    )(kv, pos2, w1p, w2p)


def _pair_blocks(g, jj):
    lower = slice(0, LANES) if g == 0 else slice(LANES, 2 * LANES)
    upper = slice(LANES, 2 * LANES) if g == 0 else slice(0, LANES)
    return lower, upper


def _cmp_kernel(*refs, tq, ns, n_sel):
    total = refs[1].shape[1]
    per_tile = tq // CMP_STRIDE
    for t in range(total // per_tile):
        nbp = min(total, ((t + 1) * per_tile + LANES - 1) // LANES * LANES)
        sel_rows = min(ns, (t + 1) * tq // SEL_BLOCK)
        rounds = 0 if sel_rows <= n_sel else n_sel - 3

        @pl.when(pl.program_id(1) == t)
        def _(nbp=nbp, sel_rows=sel_rows, rounds=rounds):
            _cmp_tile(*refs, tq=tq, ns=ns, nbp=nbp, sel_rows=sel_rows, rounds=rounds)


def _cmp_tile(q_ref, kc_ref, vc2_ref, mt_ref, ocmp_ref, sel_ref, picks_ref, *, tq, ns, nbp, sel_rows, rounds):
    q0 = pl.program_id(1) * tq
    pos = q0 + lax.broadcasted_iota(jnp.int32, (tq, nbp), 0)
    blk_end = lax.broadcasted_iota(jnp.int32, (tq, nbp), 1) * CMP_STRIDE + (CMP_LEN - 1)
    valid = blk_end <= pos
    row_ok = (q0 + lax.broadcasted_iota(jnp.int32, (tq, 1), 0)) >= CMP_LEN - 1
    kc = kc_ref[0, :nbp, :].astype(BF16)
    vc2 = vc2_ref[0, :nbp, :].astype(BF16)
    lane_c = lax.broadcasted_iota(jnp.int32, (nbp, LANES), 1)
    c_idx = lax.broadcasted_iota(jnp.int32, (nbp, LANES), 0)
    per = SEL_BLOCK // CMP_STRIDE
    feat_a = (c_idx // per - q0 // SEL_BLOCK).astype(F32)
    feat_b = ((c_idx % per) * CMP_STRIDE + (CMP_LEN - 1)).astype(F32)
    lane_lo = lax.broadcasted_iota(jnp.int32, (tq, LANES), 1) < NSA_HD
    blk = lax.broadcasted_iota(jnp.int32, (ns, tq), 0)
    blk_r = lax.broadcasted_iota(jnp.int32, (sel_rows, tq), 0)
    blk_f = blk_r.astype(F32)
    pos_t = q0 + lax.broadcasted_iota(jnp.int32, (sel_rows, tq), 1)
    cur = pos_t // SEL_BLOCK
    forced = (blk_r == 0) | (blk_r == cur) | (blk_r == cur - 1)
    causal = blk_r * SEL_BLOCK <= pos_t
    sel_t, picks = [], []
    taken = -jnp.inf
    for g in range(NSA_KV_GROUPS):
        f0 = NSA_HD * (1 - g)
        feat = jnp.where(lane_c == f0, feat_a, jnp.where(lane_c == f0 + 1, feat_b, 0.0))
        kc_g = jnp.where((lane_c // NSA_HD) == g, kc, feat.astype(BF16))
        imp = jnp.zeros((tq, nbp), F32)
        acc = []
        for hh in range(HEADS_PER_GROUP):
            h = g * HEADS_PER_GROUP + hh
            s = _nt(q_ref[0, :, h * LANES:(h + 1) * LANES], kc_g)
            s = jnp.where(valid, s, NEG)
            e = jnp.exp(s - jnp.max(s, axis=-1, keepdims=True))
            inv = jnp.where(row_ok, 1.0 / jnp.sum(e, axis=-1, keepdims=True), 0.0)
            p = e * inv
            imp = imp + p
            acc.append(_nn(p.astype(BF16), vc2))
        for jj in range(HEADS_PER_GROUP // 2):
            lower, upper = _pair_blocks(g, jj)
            blk_out = jnp.where(lane_lo, acc[2 * jj][:, lower], acc[2 * jj + 1][:, upper])
            c0 = (g * HEADS_PER_GROUP + 2 * jj) * NSA_HD
            ocmp_ref[0, :, c0:c0 + LANES] = blk_out
        p_slc = sum(_nt(mt_ref[:sel_rows, :nbp], part) for part in _split3(imp))
        if rounds == 0:
            score = jnp.where(forced | causal, taken, NEG)
        else:
            score = jnp.where(forced, taken, jnp.where(causal, p_slc, NEG))
        for _ in range(rounds):
            top = jnp.max(score, axis=0, keepdims=True)
            first = jnp.min(jnp.where(score == top, blk_f, float(ns)), axis=0, keepdims=True)
            score = jnp.where(blk_f == first, taken, score)
        if sel_rows < ns:
            score = jnp.concatenate([score, jnp.full((ns - sel_rows, tq), NEG, F32)], axis=0)
        picked = jnp.where(score == taken, 1.0, 0.0).astype(BF16)
        picks.append([_nt(jnp.ones((SUBLANES, ATT_TQ), BF16), picked[:, t * ATT_TQ:(t + 1) * ATT_TQ])
                      for t in range(tq // ATT_TQ)])
        pen = jnp.where(score == taken, 0.0, SEL_PENALTY)
        pen = jnp.where((blk >= 2) & (blk <= ns - 1), pltpu.roll(pen, 1, axis=0), 0.0)
        if ns < NSA_HD:
            pen = jnp.concatenate([pen, jnp.zeros((NSA_HD - ns, tq), F32)], axis=0)
        zero = jnp.zeros((NSA_HD, tq), F32)
        sel_t += [zero, pen] if g == 0 else [pen, zero]
    sel_ref[0] = jnp.concatenate(sel_t, axis=0).T.astype(BF16)
    for t in range(tq // ATT_TQ):
        picks_ref[0, t] = jnp.concatenate([group[t] for group in picks], axis=1)


def _importance_matrix(nbp, ns):
    per = SEL_BLOCK // CMP_STRIDE
    m = np.zeros((ns, nbp), np.float32)
    for n in range(ns):
        for c in range(per * n, per * (n + 1)):
            for cc in (c - 1, c):
                if 0 <= cc < nbp - 1:
                    m[n, cc] += 1.0
    return jnp.asarray(m, BF16)


def _cmp_attention(q, kc, vc2, b, s):
    nbp = s // CMP_STRIDE
    ns = s // SEL_BLOCK
    tq = min(CMP_TQ, s)
    assert ns <= NSA_HD
    assert min(SEL_TOPK, ns) >= 3 and tq % ATT_TQ == 0 and tq % (SUBLANES * SEL_BLOCK) == 0
    q3 = q.reshape(b, s, NSA_HEADS * LANES)
    kern = functools.partial(_cmp_kernel, tq=tq, ns=ns, n_sel=min(SEL_TOPK, ns))
    return pl.pallas_call(
        kern,
        grid=(b, s // tq),
        in_specs=[pl.BlockSpec((1, tq, NSA_HEADS * LANES), lambda i, j: (i, j, 0)),
                  pl.BlockSpec((1, nbp, KV_W), lambda i, j: (i, 0, 0)),
                  pl.BlockSpec((1, nbp, 2 * KV_W), lambda i, j: (i, 0, 0)),
                  _resident((ns, nbp))],
        out_specs=[pl.BlockSpec((1, tq, NSA_WIDTH), lambda i, j: (i, j, 0)),
                   pl.BlockSpec((1, tq, NSA_KV_GROUPS * LANES), lambda i, j: (i, j, 0)),
                   pl.BlockSpec((1, tq // ATT_TQ, SUBLANES, NSA_KV_GROUPS * ns), lambda i, j: (i, j, 0, 0))],
        out_shape=[jax.ShapeDtypeStruct((b, s, NSA_WIDTH), F32),
                   jax.ShapeDtypeStruct((b, s, NSA_KV_GROUPS * LANES), BF16),
                   jax.ShapeDtypeStruct((b, s // ATT_TQ, SUBLANES, NSA_KV_GROUPS * ns), F32)],
        compiler_params=_params(("parallel", "parallel")),
        name="cmp_topk",
    )(q3, kc, vc2, _importance_matrix(nbp, ns))


def _attn_kernel(tiles_ref, ntiles_ref, q_ref, ks_ref, vs_ref, kw_ref, vw_ref, penq_ref, gl_ref, gexp_ref,
                 ocmp_ref, o_ref, qa_scr, s_scr, p_scr, al_scr, m_scr, acc_scr, *, tq, tk):
    i = pl.program_id(2)
    rows_of = lambda hh: slice(hh * tq, (hh + 1) * tq)
    lane_lo = lax.broadcasted_iota(jnp.int32, (tq, LANES), 1) < NSA_HD
    rel = (lax.broadcasted_iota(jnp.int32, (tq, tk), 1)
           - lax.broadcasted_iota(jnp.int32, (tq, tk), 0)).astype(F32)

    for hh in range(HEADS_PER_GROUP):
        qa_scr[rows_of(hh)] = q_ref[0, :, hh * LANES:(hh + 1) * LANES] + penq_ref[0]

    def scores(k_ref, j, slot):
        s_scr[slot] = _nt(qa_scr[...], k_ref[0, pl.ds(pl.multiple_of(j * tk, tk), tk), :])

    def probs(br, j, mode, slot, first, exists=None):
        for hh in range(HEADS_PER_GROUP):
            r = rows_of(hh)
            s = s_scr[slot, r]
            if mode == "causal":
                s = jnp.where(rel <= 0.0, s, NEG)
            elif mode == "lower":
                s = jnp.where(rel + (j * tk - i * tq).astype(F32) > -float(WINDOW), s, NEG)
            if exists is not None:
                s = jnp.where(exists, s, NEG)
            m_cur = jnp.max(s, axis=-1, keepdims=True)
            if first:
                m_new = jnp.broadcast_to(m_cur, (tq, LANES))
            else:
                m_prev = m_scr[br, r]
                m_new = jnp.maximum(m_prev, m_cur)
                al_scr[slot, r] = jnp.exp(m_prev - m_new)
            m_scr[br, r] = m_new
            p_scr[slot, r] = jnp.exp(s - jnp.concatenate([m_new] * (tk // LANES), axis=1)).astype(BF16)

    def accumulate(br, v_ref, j, slot, first):
        pv = _nn(p_scr[slot], v_ref[0, pl.ds(pl.multiple_of(j * tk, tk), tk), :])
        for hh in range(HEADS_PER_GROUP):
            r = rows_of(hh)
            if first:
                acc_scr[br, r] = pv[r]
            else:
                acc_scr[br, r] = al_scr[slot, r] * acc_scr[br, r] + pv[r]

    n_q = pl.num_programs(2)
    step = (pl.program_id(0) * n_q + i) * pl.num_programs(1) + pl.program_id(1)
    n_tiles = ntiles_ref[step]
    item = lambda br, j, mode, first, exists=None: (br, j, mode, first, exists)
    sel_item = lambda n: item(0, i, "causal", True) if isinstance(n, int) and n == 0 else item(
        0, tiles_ref[step * n_q + n - 1], None, False)
    n_back = WINDOW // tk
    win_items = [item(1, i, "causal", True)] + [
        item(1, jnp.maximum(i - d, 0), "lower" if d == n_back else None, False, i >= d)
        for d in range(1, n_back + 1)]
    k_refs, v_refs = (ks_ref, kw_ref), (vs_ref, vw_ref)

    def run(items, slot0, done=(0, 0, 0), upto=None):
        n_items = len(items)
        slot = lambda n: (slot0 + n) % 2
        n_s, n_p, n_a = done

        def do_scores(n):
            br, j, _, _, _ = items[n]
            scores(k_refs[br], j, slot(n))

        def do_probs(n):
            br, j, mode, first, exists = items[n]
            probs(br, j, mode, slot(n), first, exists)

        def do_acc(n):
            br, j, _, first, _ = items[n]
            accumulate(br, v_refs[br], j, slot(n), first)

        for n in range(n_s, min(n_a + 2, n_items)):
            do_scores(n)
        n_s = max(n_s, min(n_a + 2, n_items))
        for n in range(n_p, min(n_a + 1, n_items)):
            do_probs(n)
        n_p = max(n_p, min(n_a + 1, n_items))
        for k in range(n_a, n_items if upto is None else upto):
            if n_s <= k + 2 < n_items:
                do_scores(k + 2)
                n_s = k + 3
            if n_p <= k + 1 < n_items:
                do_probs(k + 1)
                n_p = k + 2
            do_acc(k)

    for n_static in range(ATT_STATIC_TILES):
        @pl.when(n_tiles == n_static)
        def _(n_static=n_static):
            run([sel_item(n) for n in range(n_static + 1)] + win_items, 0)

    assert ATT_STATIC_TILES >= 2
    @pl.when(n_tiles >= ATT_STATIC_TILES)
    def _():
        run([sel_item(n) for n in range(3)], 0, upto=1)

        def sel_step(n, parity):
            scores(ks_ref, sel_item(n + 2)[1], parity)
            probs(0, None, None, 1 - parity, False)
            accumulate(0, vs_ref, sel_item(n)[1], parity, False)

        def sel_steps(first, count):
            for d in range(count):
                sel_step(first + d, (1 + d) % 2)

        n_pairs_of_steps = (n_tiles - 2) // 2
        n_quads = n_pairs_of_steps // 2

        def sel_four_steps(t, carry):
            sel_steps(4 * t + 1, 4)
            return carry

        lax.fori_loop(0, n_quads, sel_four_steps, 0)

        @pl.when(n_pairs_of_steps % 2 == 1)
        def _():
            sel_steps(4 * n_quads + 1, 2)

        def drain(parity):
            tail = [sel_item(n_tiles - 1), sel_item(n_tiles)] + win_items
            run(tail, 1 - parity, done=(2, 1, 0))

        @pl.when(n_tiles % 2 == 0)
        def _():
            drain(0)

        @pl.when(n_tiles % 2 == 1)
        def _():
            sel_step(n_tiles - 2, 1)
            drain(1)

    gexp = sum(_nn(part, gexp_ref[0]) for part in _split2(_sigmoid(gl_ref[0])))
    n_pairs = HEADS_PER_GROUP // 2
    gate_blk = lambda br, jj: gexp[:, (br * n_pairs + jj) * LANES:(br * n_pairs + jj + 1) * LANES]
    for jj in range(n_pairs):
        cols = slice(jj * LANES, (jj + 1) * LANES)
        blk = gate_blk(0, jj) * ocmp_ref[0, :, cols]
        for br in range(2):
            even = acc_scr[br, rows_of(2 * jj)]
            odd = acc_scr[br, rows_of(2 * jj + 1)]
            low = even / pltpu.roll(even, NSA_HD, axis=1)
            up = pltpu.roll(odd, NSA_HD, axis=1) / odd
            blk = blk + gate_blk(br + 1, jj) * jnp.where(lane_lo, low, up)
        o_ref[0, :, cols] = blk.astype(o_ref.dtype)


def _gate_expansion():
    n_pairs = HEADS_PER_GROUP // 2
    r = np.zeros((NSA_KV_GROUPS, LANES, 3 * n_pairs * LANES), np.float32)
    for g in range(NSA_KV_GROUPS):
        for br in range(3):
            for jj in range(n_pairs):
                for odd in range(2):
                    src = br * NSA_HEADS + g * HEADS_PER_GROUP + 2 * jj + odd
                    dst = (br * n_pairs + jj) * LANES + odd * NSA_HD
                    r[g, src, dst:dst + NSA_HD] = 1.0
    return jnp.asarray(r, BF16)


def _picked_tiles(picks, b, s):
    ns, nq = s // SEL_BLOCK, s // ATT_TQ
    per_tile = ATT_TK // SEL_BLOCK
    per_block = picks[:, :, 0, :].reshape(b, nq, NSA_KV_GROUPS, ns // per_tile, per_tile)
    j = jnp.arange(ns // per_tile, dtype=jnp.int32)
    i = jnp.arange(nq, dtype=jnp.int32)[None, :, None, None]
    active = ((per_block.sum(-1) > 0) | (j == 0)) & (j < i)
    slot = jnp.cumsum(active, axis=-1) - 1
    hit = active[..., None, :] & (slot[..., None, :] == j[:, None])
    tiles = jnp.sum(jnp.where(hit, j, 0), axis=-1).astype(jnp.int32)
    return tiles.reshape(-1), active.sum(-1).astype(jnp.int32).reshape(-1)


def _attention(q, ks, vs, kw, vw, pen, picks, gl, ocmp, b, s):
    tq, tk = ATT_TQ, ATT_TK
    assert tq == tk and WINDOW % tk == 0 and WINDOW // tk <= 2 and tq % SEL_BLOCK == 0
    gw = HEADS_PER_GROUP * LANES
    ow = HEADS_PER_GROUP * NSA_HD
    r3 = lambda a: a.reshape(b, s, a.shape[-1])
    tile = lambda w: pl.BlockSpec((1, tq, w), lambda i, g, j, *_: (i, j, 0))
    gtile = lambda w: pl.BlockSpec((1, tq, w), lambda i, g, j, *_: (i, j, g))
    gfull = pl.BlockSpec((1, s, LANES), lambda i, g, j, *_: (i, 0, g))
    gexp = _gate_expansion()
    rows = HEADS_PER_GROUP * tq
    kern = functools.partial(_attn_kernel, tq=tq, tk=tk)
    tiles, n_tiles = _picked_tiles(picks, b, s)
    grid_spec = pltpu.PrefetchScalarGridSpec(
        num_scalar_prefetch=2,
        grid=(b, NSA_KV_GROUPS, s // tq),
        in_specs=[gtile(gw), gfull, gfull, gfull, gfull, gtile(LANES), tile(LANES),
                  pl.BlockSpec((1,) + gexp.shape[1:], lambda i, g, j, *_: (g, 0, 0)), gtile(ow)],
        out_specs=gtile(ow),
        scratch_shapes=[pltpu.VMEM((rows, LANES), BF16), pltpu.VMEM((2, rows, tk), F32),
                        pltpu.VMEM((2, rows, tk), BF16), pltpu.VMEM((2, rows, LANES), F32),
                        pltpu.VMEM((2, rows, LANES), F32), pltpu.VMEM((2, rows, LANES), F32)])
    return pl.pallas_call(
        kern,
        grid_spec=grid_spec,
        out_shape=jax.ShapeDtypeStruct((b, s, NSA_WIDTH), BF16),
        compiler_params=_params(("parallel", "parallel", "parallel")),
        name="sel_win_attention",
    )(tiles, n_tiles, r3(q), r3(ks), r3(vs), r3(kw), r3(vw), pen, r3(gl), gexp, ocmp)


def _hgrn_chunks(raw, onorm, layer):
    c = HG_CHUNK
    ex = jnp.exp(raw - jnp.max(raw, axis=0, keepdims=True))
    sm = ex / jnp.sum(ex, axis=0, keepdims=True)
    lb_all = jnp.zeros((1, raw.shape[1]), F32)
    for l in range(1, layer + 1):
        lb_all = lb_all + sm[l:l + 1, :]

    t_idx = lax.broadcasted_iota(jnp.int32, (c, HG_DK), 0)
    sub = lax.broadcasted_iota(jnp.int32, (SUBLANES, HG_DK), 0)
    ti = lax.broadcasted_iota(jnp.int32, (c, c), 0)
    si = lax.broadcasted_iota(jnp.int32, (c, c), 1)
    tril = jnp.where(si <= ti, 1.0, 0.0).astype(BF16)
    levels = (32, 16, 8, 4, 2, 1)
    hc = HG_HEADS * c
    tb = lax.broadcasted_iota(jnp.int32, (hc, hc), 0)
    sb = lax.broadcasted_iota(jnp.int32, (hc, hc), 1)
    same_head = (tb // c) == (sb // c)
    split_bit = tb ^ sb
    pair_mask = {m: jnp.where(same_head & (tb > sb) & (split_bit >= m) & (split_bit < 2 * m), 1.0, 0.0)
                 for m in levels}
    upper_groups = {m: [h * c + s0 + m + d for h in range(HG_HEADS) for s0 in range(0, c, 2 * m)
                        for d in range(0, m, SUBLANES)] for m in levels if m % SUBLANES == 0}
    upper_mask = {m: jnp.concatenate([pair_mask[m][g:g + SUBLANES] for g in groups], axis=0)
                  for m, groups in upper_groups.items()}
    diagonal = tb == sb
    second_half ={m: (t_idx & m) != 0 for m in levels}
    sign = {m: jnp.where(second_half[m], 1.0, -1.0) for m in levels}

    def ref_rows(b, m):
        row = lambda r, n: jnp.broadcast_to(b[r:r + 1, :], (n, HG_DK))
        if m >= 4:
            return jnp.concatenate([row(s0 + m - 1, 2 * m) for s0 in range(0, c, 2 * m)], axis=0)
        return jnp.concatenate([jnp.where(sub < 4, row(s0 + 1, SUBLANES), row(s0 + 5, SUBLANES))
                                for s0 in range(0, c, SUBLANES)], axis=0)

    part = lambda y_ref, rows, which, h: y_ref[rows, which * HG_WIDTH + h * HG_DK:which * HG_WIDTH + (h + 1) * HG_DK]

    def prepare(y_ref, rows):
        q, k, f, v, logf = [], [], [], [], []
        for h in range(HG_HEADS):
            lb = lb_all[:, h * HG_DK:(h + 1) * HG_DK]
            z = part(y_ref, rows, 1, h)
            hq = part(y_ref, rows, 0, h)
            q.append(hq * _sigmoid(hq))
            ez = jnp.exp(-jnp.abs(z))
            big = 1.0 / (1.0 + ez)
            small = ez * big
            f.append(jnp.maximum(lb + (1.0 - lb) * jnp.where(z >= 0.0, big, small), F_FLOOR))
            logf.append(jnp.log2(f[h]))
            k.append((1.0 - lb) * jnp.where(z >= 0.0, small, big))
            v.append(part(y_ref, rows, 2, h).astype(BF16))
        parts = jnp.concatenate([p for h in range(HG_HEADS) for p in _split3(logf[h])], axis=1)
        csum = _nn(tril, parts)
        bcum = [sum(csum[:, (3 * h + i) * HG_DK:(3 * h + i + 1) * HG_DK] for i in range(3))
                for h in range(HG_HEADS)]
        return q, k, f, v, bcum

    def finish(prepared, y_ref, rows, st_scr, fill):
        q, k, f, v, bcum = prepared
        rowdot = jnp.concatenate([jnp.sum(q[h] * k[h], axis=-1, keepdims=True) for h in range(HG_HEADS)], axis=0)
        a = jnp.where(diagonal, rowdot, 0.0)
        for m, emits in zip(levels, fill, strict=True):
            for emit in emits:
                emit()
            r = []
            for h in range(HG_HEADS):
                if m == 1:
                    r.append(jnp.where(second_half[m], q[h] * f[h], k[h]))
                else:
                    w = jnp.exp2((bcum[h] - ref_rows(bcum[h], m)) * sign[m])
                    r.append(jnp.where(second_half[m], q[h], k[h]) * w)
            r = jnp.concatenate(r, axis=0)
            if m in upper_groups:
                lhs = jnp.concatenate([r[g:g + SUBLANES] for g in upper_groups[m]], axis=0).astype(BF16)
                p = _nt(lhs, r.astype(BF16)) * upper_mask[m]
                parts = {g: p[n * SUBLANES:(n + 1) * SUBLANES] for n, g in enumerate(upper_groups[m])}
                a = jnp.concatenate([a[g:g + SUBLANES] + parts[g] if g in parts else a[g:g + SUBLANES]
                                     for g in range(0, hc, SUBLANES)], axis=0)
            else:
                r = r.astype(BF16)
                a = a + _nt(r, r) * pair_mask[m]
        o_intra = _nn(a.astype(BF16), jnp.concatenate(v, axis=0))

        heads = []
        for h in range(HG_HEADS):
            st = st_scr[h]
            o = o_intra[h * c:(h + 1) * c] + _nt((q[h] * jnp.exp2(bcum[h])).astype(BF16), st.astype(BF16))
            b_last = bcum[h][c - 1:c, :]
            kd = (k[h] * jnp.exp2(b_last - bcum[h])).astype(BF16)
            st_scr[h] = jnp.exp2(b_last) * st + _tn(v[h], kd)

            o = o * lax.rsqrt(jnp.mean(o * o, axis=-1, keepdims=True) + EPS) * onorm
            gate = part(y_ref, rows, 3, h)
            heads.append((o * (gate * _sigmoid(gate))).astype(BF16))
        return jnp.concatenate(heads, axis=1)

    return prepare, finish, len(levels)


def _mixer(h, l, b, s, mix_norm, w_ext, cmp_k, cmp_v, hgrn_lower_bound, hgrn_out_norm):
    q, ks, vs, kw, vw, kc_in, vc_in, gl, o_hg = _inproj(h, mix_norm, w_ext, hgrn_lower_bound, hgrn_out_norm, s, l)
    kc = _compress(kc_in.reshape(b, s, KV_W), l, *cmp_k)
    vc2 = _compress(vc_in.reshape(b, s, KV_W), l, *cmp_v)
    ocmp, sel, picks = _cmp_attention(q, kc, vc2, b, s)
    o_nsa = _attention(q, ks, vs, kw, vw, sel, picks, gl, ocmp, b, s)
    return o_nsa.reshape(b * s, NSA_WIDTH), o_hg


def kernel(x, ffn1_norm, ffn1_w_gu, ffn1_w_down, mix_norm, w_in, cmp_pos_k, cmp_pos_v, cmp_k_w1, cmp_k_w2, cmp_v_w1, cmp_v_w2, hgrn_lower_bound, hgrn_out_norm, w_out, ffn2_norm, ffn2_w_gu, ffn2_w_down, final_norm):
    b, s, d = x.shape
    depth = ffn1_norm.shape[0]
    ffn1_w = _ffn_weights(ffn1_w_gu, ffn1_w_down)
    ffn2_w = _ffn_weights(ffn2_w_gu, ffn2_w_down)
    w_out = w_out.astype(BF16)
    w_ext = _build_w_in(w_in)
    cmp_k = _compress_weights(cmp_pos_k, cmp_k_w1, cmp_k_w2, (0, 1))
    cmp_v = _compress_weights(cmp_pos_v, cmp_v_w1, cmp_v_w2, (0, 1, 1, 0))
    h = x.reshape(b * s, d)
    for l in range(depth):
        h = _ffn(h, l, ffn1_norm, *ffn1_w, final_norm, False)
        o_nsa, o_hg = _mixer(h, l, b, s, mix_norm, w_ext, cmp_k, cmp_v, hgrn_lower_bound, hgrn_out_norm)
        h = _ffn(h, l, ffn2_norm, *ffn2_w, final_norm, l == depth - 1, (o_nsa, o_hg, w_out))
    return h.reshape(b, s, d)
```

```python
import functools

import jax
import jax.numpy as jnp
import numpy as np
from jax import lax
from jax.experimental import pallas as pl
from jax.experimental.pallas import tpu as pltpu

F32 = jnp.float32
BF16 = jnp.bfloat16

D_MODEL = 1024
EPS = 1e-6
NEG = -1e30
F_FLOOR = 1e-30
NSA_HEADS = 8
NSA_KV_GROUPS = 2
HEADS_PER_GROUP = NSA_HEADS // NSA_KV_GROUPS
NSA_HD = 64
CMP_LEN = 32
CMP_STRIDE = 16
CMP_HID = 256
SEL_BLOCK = 64
SEL_TOPK = 16
WINDOW = 512
HG_HEADS = 4
HG_DK = 128
HG_DV = 128
HG_CHUNK = 64
D_FF = 2752
NSA_WIDTH = NSA_HEADS * NSA_HD
HG_WIDTH = HG_HEADS * HG_DV
KV_W = NSA_KV_GROUPS * NSA_HD

LANES = 128
SUBLANES = 8
MXU_COLS = 256
FF_CHUNK = 256
D_FF_PAD = -(-D_FF // FF_CHUNK) * FF_CHUNK
ROW_TILE = 512
FFN_ROWS = 1024
ATT_TQ = 256
ATT_TK = 256
ATT_STATIC_TILES = 7
CMP_TQ = 1024
VMEM_LIMIT = 56 * 1024 * 1024

SEL_PENALTY = 2.0 ** 50
ALIBI_SLOPES = tuple(2.0 ** (-8.0 * (i + 1) / NSA_HEADS) for i in range(NSA_HEADS))

SEGMENTS = (
    ("q", NSA_HEADS * LANES, BF16, NSA_WIDTH),
    ("ks", NSA_KV_GROUPS * LANES, BF16, KV_W),
    ("vs", NSA_KV_GROUPS * LANES, BF16, KV_W),
    ("kw", NSA_KV_GROUPS * LANES, BF16, KV_W),
    ("vw", NSA_KV_GROUPS * LANES, BF16, KV_W),
    ("kc", KV_W, F32, KV_W),
    ("vc", KV_W, F32, KV_W),
    ("gl", LANES, F32, LANES),
    ("hg", HG_WIDTH, BF16, 4 * HG_WIDTH),
)
SEG_OFFSETS = tuple(int(v) for v in np.cumsum([0] + [s[3] for s in SEGMENTS]))


def _nn(a, b):
    return jnp.dot(a, b, preferred_element_type=F32)


def _nt(a, b):
    return lax.dot_general(a, b, (((1,), (1,)), ((), ())), preferred_element_type=F32)


def _tn(a, b):
    return lax.dot_general(a, b, (((0,), (0,)), ((), ())), preferred_element_type=F32)


def _split2(x):
    hi = x.astype(BF16)
    lo = (x - hi.astype(F32)).astype(BF16)
    return hi, lo


def _split3(x):
    hi = x.astype(BF16)
    r = x - hi.astype(F32)
    mid = r.astype(BF16)
    lo = (r - mid.astype(F32)).astype(BF16)
    return hi, mid, lo


def _sigmoid(x):
    return 1.0 / (1.0 + jnp.exp(-x))


def _rms(x, g):
    return x * lax.rsqrt(jnp.mean(x * x, axis=-1, keepdims=True) + EPS) * g


def _resident(shape):
    nd = len(shape)
    return pl.BlockSpec(shape, lambda *_: (0,) * nd, pipeline_mode=pl.Buffered(1))


def _resident_layer(stacked_shape, layer):
    nd = len(stacked_shape)
    return pl.BlockSpec((None,) + tuple(stacked_shape[1:]), lambda *_: (layer,) + (0,) * (nd - 1),
                        pipeline_mode=pl.Buffered(1))


def _params(sem):
    return pltpu.CompilerParams(dimension_semantics=sem, vmem_limit_bytes=VMEM_LIMIT)


def _ffn_kernel(*refs, final, proj):
    if proj:
        x_ref, a_ref, b_ref, wo_ref, g_ref, wg_ref, wu_ref, wd_ref, gf_ref, o_ref = refs
        x = x_ref[...] + _nn(a_ref[...], wo_ref[:NSA_WIDTH, :]) + _nn(b_ref[...], wo_ref[NSA_WIDTH:, :])
    else:
        x_ref, g_ref, wg_ref, wu_ref, wd_ref, gf_ref, o_ref = refs
        x = x_ref[...]
    xn = _rms(x, g_ref[...]).astype(BF16)
    acc = jnp.zeros(x.shape, F32)
    for k in range(D_FF_PAD // FF_CHUNK):
        sl = slice(k * FF_CHUNK, (k + 1) * FF_CHUNK)
        gate = _nn(xn, wg_ref[:, sl])
        up = _nn(xn, wu_ref[:, sl])
        h = (gate * _sigmoid(gate) * up).astype(BF16)
        acc = acc + _nn(h, wd_ref[sl, :])
    y = x + 0.5 * acc
    if final:
        y = _rms(y, gf_ref[...])
    o_ref[...] = y


def _ffn_weights(w_gu, w_down):
    pad = D_FF_PAD - D_FF
    wg = jnp.pad(w_gu[..., :D_FF], ((0, 0), (0, 0), (0, pad))).astype(BF16)
    wu = lax.dynamic_update_slice(jnp.zeros(w_gu.shape[:-1] + (D_FF_PAD,), BF16),
                                  w_gu[..., D_FF:].astype(BF16), (0, 0, 0))
    wd = jnp.pad(w_down, ((0, 0), (0, pad), (0, 0))).astype(BF16)
    return wg, wu, wd


def _ffn(h, layer, norm_g, wg, wu, wd, final_g, final, proj=None):
    t = h.shape[0]
    row = lambda w: pl.BlockSpec((FFN_ROWS, w), lambda i: (i, 0))
    norm_g = norm_g[:, None, :]
    weights = [_resident_layer(a.shape, layer) for a in (norm_g, wg, wu, wd)] + [_resident((1, D_MODEL))]
    operands = [norm_g, wg, wu, wd, final_g.reshape(1, -1)]
    if proj is None:
        in_specs, args = [row(D_MODEL)] + weights, [h] + operands
    else:
        o_nsa, o_hg, w_out = proj
        in_specs = [row(D_MODEL), row(NSA_WIDTH), row(HG_WIDTH), _resident_layer(w_out.shape, layer)] + weights
        args = [h, o_nsa, o_hg, w_out] + operands
    return pl.pallas_call(
        functools.partial(_ffn_kernel, final=final, proj=proj is not None),
        grid=(t // FFN_ROWS,),
        in_specs=in_specs,
        out_specs=row(D_MODEL),
        out_shape=jax.ShapeDtypeStruct((t, D_MODEL), F32),
        compiler_params=_params(("parallel",)),
        name="ffn",
    )(*args)


def _inproj_kernel(x_ref, g_ref, w_ref, qfeat_ref, lbraw_ref, onorm_ref, *refs, seq, layer):
    *o_refs, xn_scr, y_scr, st_scr = refs
    out = {seg[0]: o_ref for seg, o_ref in zip(SEGMENTS, o_refs, strict=True)}
    col0 = {seg[0]: off for seg, off in zip(SEGMENTS, SEG_OFFSETS[:-1], strict=True)}
    rows = x_ref.shape[0]
    half = rows // 2
    tile = pl.program_id(0) % (seq // rows)

    @pl.when(tile == 0)
    def _():
        st_scr[...] = jnp.zeros(st_scr.shape, F32)

    def norm_piece(r):
        rs = slice(r * half, (r + 1) * half)
        xn_scr[rs, :] = _rms(x_ref[rs, :], g_ref[...]).astype(BF16)

    lane = lax.broadcasted_iota(jnp.int32, (half, LANES), 1)
    lower = lane < NSA_HD
    swap = lambda a: pltpu.roll(a, NSA_HD, axis=1)
    halves = [NSA_HD * (1 - g) for g in range(NSA_KV_GROUPS)]
    project = lambda rs, c0, cols: _nn(xn_scr[rs, :], w_ref[:, c0:c0 + cols])

    def q_piece(r, cb):
        def emit():
            rs = slice(r * half, (r + 1) * half)
            y = project(rs, col0["q"] + cb * MXU_COLS, MXU_COLS)
            per = MXU_COLS // NSA_HD
            blocks = []
            for h in range(cb * per, (cb + 1) * per):
                pair = y[:, (h % per // 2) * LANES:(h % per // 2 + 1) * LANES]
                g = h // HEADS_PER_GROUP
                data = pair if h % 2 == g else swap(pair)
                feat = qfeat_ref[:, h * LANES:(h + 1) * LANES]
                blocks.append(jnp.where(lower == (g == 0), data, feat))
            out["q"][rs, cb * per * LANES:(cb + 1) * per * LANES] = (
                jnp.concatenate(blocks, axis=1).astype(out["q"].dtype))
        return emit

    small = [seg[0] for seg in SEGMENTS if seg[3] == LANES]
    assert all(col0[b] - col0[a] == LANES for a, b in zip(small, small[1:]))

    def small_piece(r, names):
        def emit():
            rs = slice(r * half, (r + 1) * half)
            y_small = project(rs, col0[names[0]], len(names) * LANES)
            kpos = tile * rows + r * half + lax.broadcasted_iota(jnp.int32, (half, LANES), 0)
            blk = kpos // SEL_BLOCK
            digit = lambda f0: jnp.where(lane == f0, blk.astype(F32),
                                         jnp.where(lane == f0 + 1, (kpos % SEL_BLOCK).astype(F32), 0.0))
            is_pen = lambda f0: (lane == f0 + 1 + blk) & (blk >= 1) & (blk <= seq // SEL_BLOCK - 2)
            for n, name in enumerate(names):
                y = y_small[:, n * LANES:(n + 1) * LANES]
                if name in ("ks", "kw"):
                    feat = [digit(f0) + jnp.where(is_pen(f0), -SEL_PENALTY, 0.0) if name == "ks" else digit(f0)
                            for f0 in halves]
                    y = jnp.concatenate([jnp.where(lower == (g == 0), y, feat[g])
                                         for g in range(NSA_KV_GROUPS)], axis=1)
                elif name in ("vs", "vw"):
                    y = jnp.concatenate([jnp.where(lower, y if g == 0 else swap(y), 1.0)
                                         for g in range(NSA_KV_GROUPS)], axis=1)
                out[name][rs, :] = y.astype(out[name].dtype)
        return emit

    def hg_piece(r, cb):
        def emit():
            rs = slice(r * half, (r + 1) * half)
            y_scr[rs, cb * MXU_COLS:(cb + 1) * MXU_COLS] = project(rs, col0["hg"] + cb * MXU_COLS, MXU_COLS)
        return emit

    prepare, finish, n_slots = _hgrn_chunks(lbraw_ref[...], onorm_ref[...], layer)
    n_chunks = rows // HG_CHUNK
    chunk_rows = lambda ci: slice(ci * HG_CHUNK, (ci + 1) * HG_CHUNK)
    hg_tiles = 4 * HG_WIDTH // MXU_COLS
    norm_piece(0)
    hg_piece(0, 0)()
    norm_piece(1)
    for cb in range(1, hg_tiles):
        hg_piece(0, cb)()
    later = [hg_piece(1, cb) for cb in range(hg_tiles)]
    n_early = n_chunks // 2 - 1
    rest = [p for r in range(2) for p in (q_piece(r, 0), q_piece(r, 1), small_piece(r, small[0:2]),
                                          small_piece(r, small[2:4]), small_piece(r, small[4:]))]
    spread = lambda items, n: [items[i * len(items) // n:(i + 1) * len(items) // n] for i in range(n)]
    tiles = spread(later, n_early) + spread(rest, n_chunks - n_early)
    last = tiles[-1].pop()
    prepared = prepare(y_scr, chunk_rows(0))
    for ci in range(n_chunks):
        slots = [[] for _ in range(n_slots)]
        for j, emit in enumerate(tiles[ci]):
            slots[j * n_slots // len(tiles[ci])].append(emit)
        following = []
        if ci + 1 < n_chunks:
            slots[0].append(lambda ci=ci: following.append(prepare(y_scr, chunk_rows(ci + 1))))
        out["hg"][chunk_rows(ci), :] = finish(prepared, y_scr, chunk_rows(ci), st_scr, slots)
        prepared = following[0] if following else None
    last()


def _q_features():
    feat = np.zeros((1, NSA_HEADS * LANES), np.float32)
    for h in range(NSA_HEADS):
        f0 = h * LANES + NSA_HD * (1 - h // HEADS_PER_GROUP)
        feat[0, f0] = SEL_BLOCK * ALIBI_SLOPES[h]
        feat[0, f0 + 1] = ALIBI_SLOPES[h]
    return jnp.asarray(feat)


def _build_w_in(w_in):
    sizes = (NSA_WIDTH, KV_W, KV_W, KV_W, KV_W, KV_W, KV_W, NSA_HEADS * 3,
             HG_WIDTH, HG_WIDTH, HG_WIDTH, HG_WIDTH)
    splits = [int(v) for v in np.cumsum(sizes)[:-1]]
    w_in = w_in.astype(BF16)
    wq, wkc, wvc, wks, wvs, wkw, wvw, wgl, whq, whf, whi, whg = jnp.split(w_in, splits, axis=-1)
    lead = w_in.shape[:-1]
    gl = jnp.swapaxes(wgl.reshape(lead + (NSA_HEADS, 3)), -1, -2).reshape(lead + (3 * NSA_HEADS,))
    gl = jnp.pad(gl, [(0, 0)] * len(lead) + [(0, LANES - 3 * NSA_HEADS)])
    cols = [wq * NSA_HD ** -0.5, wks, wvs, wkw, wvw, wkc, wvc, gl, whq, whf, whi, whg]
    return jnp.concatenate(cols, axis=-1)


def _inproj(h, norm_g, w_ext, lb_raw, out_norm, seq, layer):
    t = h.shape[0]
    assert seq % ROW_TILE == 0 and ROW_TILE % HG_CHUNK == 0
    row = lambda w: pl.BlockSpec((ROW_TILE, w), lambda i: (i, 0))
    norm_g, out_norm = norm_g[:, None, :], out_norm[:, None, :]
    return pl.pallas_call(
        functools.partial(_inproj_kernel, seq=seq, layer=layer),
        grid=(t // ROW_TILE,),
        in_specs=[row(D_MODEL), _resident_layer(norm_g.shape, layer), _resident_layer(w_ext.shape, layer),
                  _resident((1, NSA_HEADS * LANES)), _resident(lb_raw.shape),
                  _resident_layer(out_norm.shape, layer)],
        out_specs=[row(w) for _, w, _, _ in SEGMENTS],
        out_shape=[jax.ShapeDtypeStruct((t, w), dt) for _, w, dt, _ in SEGMENTS],
        scratch_shapes=[pltpu.VMEM((ROW_TILE, D_MODEL), BF16), pltpu.VMEM((ROW_TILE, 4 * HG_WIDTH), F32),
                        pltpu.VMEM((HG_HEADS, HG_DV, HG_DK), F32)],
        compiler_params=_params(("arbitrary",)),
        name="inproj_hgrn",
    )(h, norm_g, w_ext, _q_features(), lb_raw, out_norm)


def _gelu_tanh(x):
    return 0.5 * x * (1.0 + jnp.tanh(0.7978845608028654 * (x + 0.044715 * (x * x * x))))


def _compress_kernel(kv_ref, pos_ref, w1_ref, w2_ref, o_ref):
    nbp = o_ref.shape[1]
    tokens = [kv_ref[0, pl.ds(l, nbp, stride=CMP_STRIDE), :] for l in range(CMP_STRIDE)]
    chunk = lambda first: jnp.concatenate(
        [(tokens[l] + pos_ref[first + l:first + l + 1, :]).astype(BF16) for l in range(CMP_STRIDE)], axis=1)
    slab = lambda first: w1_ref[first:first + CMP_STRIDE].reshape(CMP_STRIDE * KV_W, NSA_KV_GROUPS * CMP_HID)
    ha = _nn(chunk(0), slab(0))
    hb = _nn(chunk(CMP_STRIDE), slab(CMP_STRIDE))
    act = _gelu_tanh(ha + pltpu.roll(hb, nbp - 1, axis=0)).astype(BF16)
    out = jnp.zeros(o_ref.shape[1:], F32)
    for g in range(NSA_KV_GROUPS):
        out = out + _nn(act[:, g * CMP_HID:(g + 1) * CMP_HID], w2_ref[g])
    o_ref[0] = out


def _compress_weights(pos, w1, w2, reps):
    layers = w1.shape[0]
    w1l = w1.astype(BF16).reshape(layers, CMP_LEN, NSA_HD, CMP_HID)
    z1 = jnp.zeros_like(w1l)
    w1p = jnp.concatenate([jnp.concatenate([w1l, z1], axis=3), jnp.concatenate([z1, w1l], axis=3)], axis=2)
    pos2 = jnp.concatenate([pos] * NSA_KV_GROUPS, axis=2)
    w2 = w2.astype(BF16)
    zero = jnp.zeros_like(w2)
    w2p = jnp.stack([jnp.concatenate([w2 if r == g else zero for r in reps], axis=2)
                     for g in range(NSA_KV_GROUPS)], axis=1)
    return pos2, w1p, w2p


def _compress(kv, layer, pos2, w1p, w2p):
    b, s, _ = kv.shape
    nbp = s // CMP_STRIDE
    width = w2p.shape[-1]
    return pl.pallas_call(
        _compress_kernel,
        grid=(b,),
        in_specs=[pl.BlockSpec((1, s, KV_W), lambda i: (i, 0, 0))]
        + [_resident_layer(a.shape, layer) for a in (pos2, w1p, w2p)],
        out_specs=pl.BlockSpec((1, nbp, width), lambda i: (i, 0, 0)),
        out_shape=jax.ShapeDtypeStruct((b, nbp, width), F32),
        compiler_params=_params(("parallel",)),
        name="compress",
    )(kv, pos2, w1p, w2p)


def _pair_blocks(g, jj):
    lower = slice(0, LANES) if g == 0 else slice(LANES, 2 * LANES)
    upper = slice(LANES, 2 * LANES) if g == 0 else slice(0, LANES)
    return lower, upper


def _cmp_kernel(*refs, tq, ns, n_sel):
    total = refs[1].shape[1]
    per_tile = tq // CMP_STRIDE
    for t in range(total // per_tile):
        nbp = min(total, ((t + 1) * per_tile + LANES - 1) // LANES * LANES)
        sel_rows = min(ns, (t + 1) * tq // SEL_BLOCK)
        rounds = 0 if sel_rows <= n_sel else n_sel - 3

        @pl.when(pl.program_id(1) == t)
        def _(nbp=nbp, sel_rows=sel_rows, rounds=rounds):
            _cmp_tile(*refs, tq=tq, ns=ns, nbp=nbp, sel_rows=sel_rows, rounds=rounds)


def _cmp_tile(q_ref, kc_ref, vc2_ref, mt_ref, ocmp_ref, sel_ref, picks_ref, *, tq, ns, nbp, sel_rows, rounds):
    q0 = pl.program_id(1) * tq
    pos = q0 + lax.broadcasted_iota(jnp.int32, (tq, nbp), 0)
    blk_end = lax.broadcasted_iota(jnp.int32, (tq, nbp), 1) * CMP_STRIDE + (CMP_LEN - 1)
    valid = blk_end <= pos
    row_ok = (q0 + lax.broadcasted_iota(jnp.int32, (tq, 1), 0)) >= CMP_LEN - 1
    kc = kc_ref[0, :nbp, :].astype(BF16)
    vc2 = vc2_ref[0, :nbp, :].astype(BF16)
    lane_c = lax.broadcasted_iota(jnp.int32, (nbp, LANES), 1)
    c_idx = lax.broadcasted_iota(jnp.int32, (nbp, LANES), 0)
    per = SEL_BLOCK // CMP_STRIDE
    feat_a = (c_idx // per - q0 // SEL_BLOCK).astype(F32)
    feat_b = ((c_idx % per) * CMP_STRIDE + (CMP_LEN - 1)).astype(F32)
    lane_lo = lax.broadcasted_iota(jnp.int32, (tq, LANES), 1) < NSA_HD
    blk = lax.broadcasted_iota(jnp.int32, (ns, tq), 0)
    blk_r = lax.broadcasted_iota(jnp.int32, (sel_rows, tq), 0)
    blk_f = blk_r.astype(F32)
    pos_t = q0 + lax.broadcasted_iota(jnp.int32, (sel_rows, tq), 1)
    cur = pos_t // SEL_BLOCK
    forced = (blk_r == 0) | (blk_r == cur) | (blk_r == cur - 1)
    causal = blk_r * SEL_BLOCK <= pos_t
    sel_t, picks = [], []
    taken = -jnp.inf
    for g in range(NSA_KV_GROUPS):
        f0 = NSA_HD * (1 - g)
        feat = jnp.where(lane_c == f0, feat_a, jnp.where(lane_c == f0 + 1, feat_b, 0.0))
        kc_g = jnp.where((lane_c // NSA_HD) == g, kc, feat.astype(BF16))
        imp = jnp.zeros((tq, nbp), F32)
        acc = []
        for hh in range(HEADS_PER_GROUP):
            h = g * HEADS_PER_GROUP + hh
            s = _nt(q_ref[0, :, h * LANES:(h + 1) * LANES], kc_g)
            s = jnp.where(valid, s, NEG)
            e = jnp.exp(s - jnp.max(s, axis=-1, keepdims=True))
            inv = jnp.where(row_ok, 1.0 / jnp.sum(e, axis=-1, keepdims=True), 0.0)
            p = e * inv
            imp = imp + p
            acc.append(_nn(p.astype(BF16), vc2))
        for jj in range(HEADS_PER_GROUP // 2):
            lower, upper = _pair_blocks(g, jj)
            blk_out = jnp.where(lane_lo, acc[2 * jj][:, lower], acc[2 * jj + 1][:, upper])
            c0 = (g * HEADS_PER_GROUP + 2 * jj) * NSA_HD
            ocmp_ref[0, :, c0:c0 + LANES] = blk_out
        p_slc = sum(_nt(mt_ref[:sel_rows, :nbp], part) for part in _split3(imp))
        if rounds == 0:
            score = jnp.where(forced | causal, taken, NEG)
        else:
            score = jnp.where(forced, taken, jnp.where(causal, p_slc, NEG))
        for _ in range(rounds):
            top = jnp.max(score, axis=0, keepdims=True)
            first = jnp.min(jnp.where(score == top, blk_f, float(ns)), axis=0, keepdims=True)
            score = jnp.where(blk_f == first, taken, score)
        if sel_rows < ns:
            score = jnp.concatenate([score, jnp.full((ns - sel_rows, tq), NEG, F32)], axis=0)
        picked = jnp.where(score == taken, 1.0, 0.0).astype(BF16)
        picks.append([_nt(jnp.ones((SUBLANES, ATT_TQ), BF16), picked[:, t * ATT_TQ:(t + 1) * ATT_TQ])
                      for t in range(tq // ATT_TQ)])
        pen = jnp.where(score == taken, 0.0, SEL_PENALTY)
        pen = jnp.where((blk >= 2) & (blk <= ns - 1), pltpu.roll(pen, 1, axis=0), 0.0)
        if ns < NSA_HD:
            pen = jnp.concatenate([pen, jnp.zeros((NSA_HD - ns, tq), F32)], axis=0)
        zero = jnp.zeros((NSA_HD, tq), F32)
        sel_t += [zero, pen] if g == 0 else [pen, zero]
    sel_ref[0] = jnp.concatenate(sel_t, axis=0).T.astype(BF16)
    for t in range(tq // ATT_TQ):
        picks_ref[0, t] = jnp.concatenate([group[t] for group in picks], axis=1)


def _importance_matrix(nbp, ns):
    per = SEL_BLOCK // CMP_STRIDE
    m = np.zeros((ns, nbp), np.float32)
    for n in range(ns):
        for c in range(per * n, per * (n + 1)):
            for cc in (c - 1, c):
                if 0 <= cc < nbp - 1:
                    m[n, cc] += 1.0
    return jnp.asarray(m, BF16)


def _cmp_attention(q, kc, vc2, b, s):
    nbp = s // CMP_STRIDE
    ns = s // SEL_BLOCK
    tq = min(CMP_TQ, s)
    assert ns <= NSA_HD
    assert min(SEL_TOPK, ns) >= 3 and tq % ATT_TQ == 0 and tq % (SUBLANES * SEL_BLOCK) == 0
    q3 = q.reshape(b, s, NSA_HEADS * LANES)
    kern = functools.partial(_cmp_kernel, tq=tq, ns=ns, n_sel=min(SEL_TOPK, ns))
    return pl.pallas_call(
        kern,
        grid=(b, s // tq),
        in_specs=[pl.BlockSpec((1, tq, NSA_HEADS * LANES), lambda i, j: (i, j, 0)),
                  pl.BlockSpec((1, nbp, KV_W), lambda i, j: (i, 0, 0)),
                  pl.BlockSpec((1, nbp, 2 * KV_W), lambda i, j: (i, 0, 0)),
                  _resident((ns, nbp))],
        out_specs=[pl.BlockSpec((1, tq, NSA_WIDTH), lambda i, j: (i, j, 0)),
                   pl.BlockSpec((1, tq, NSA_KV_GROUPS * LANES), lambda i, j: (i, j, 0)),
                   pl.BlockSpec((1, tq // ATT_TQ, SUBLANES, NSA_KV_GROUPS * ns), lambda i, j: (i, j, 0, 0))],
        out_shape=[jax.ShapeDtypeStruct((b, s, NSA_WIDTH), F32),
                   jax.ShapeDtypeStruct((b, s, NSA_KV_GROUPS * LANES), BF16),
                   jax.ShapeDtypeStruct((b, s // ATT_TQ, SUBLANES, NSA_KV_GROUPS * ns), F32)],
        compiler_params=_params(("parallel", "parallel")),
        name="cmp_topk",
    )(q3, kc, vc2, _importance_matrix(nbp, ns))


def _attn_kernel(tiles_ref, ntiles_ref, q_ref, ks_ref, vs_ref, kw_ref, vw_ref, penq_ref, gl_ref, gexp_ref,
                 ocmp_ref, o_ref, qa_scr, s_scr, p_scr, al_scr, m_scr, acc_scr, *, tq, tk):
    i = pl.program_id(2)
    rows_of = lambda hh: slice(hh * tq, (hh + 1) * tq)
    lane_lo = lax.broadcasted_iota(jnp.int32, (tq, LANES), 1) < NSA_HD
    rel = (lax.broadcasted_iota(jnp.int32, (tq, tk), 1)
           - lax.broadcasted_iota(jnp.int32, (tq, tk), 0)).astype(F32)

    for hh in range(HEADS_PER_GROUP):
        qa_scr[rows_of(hh)] = q_ref[0, :, hh * LANES:(hh + 1) * LANES] + penq_ref[0]

    def scores(k_ref, j, slot):
        s_scr[slot] = _nt(qa_scr[...], k_ref[0, pl.ds(pl.multiple_of(j * tk, tk), tk), :])

    def probs(br, j, mode, slot, first, exists=None):
        for hh in range(HEADS_PER_GROUP):
            r = rows_of(hh)
            s = s_scr[slot, r]
            if mode == "causal":
                s = jnp.where(rel <= 0.0, s, NEG)
            elif mode == "lower":
                s = jnp.where(rel + (j * tk - i * tq).astype(F32) > -float(WINDOW), s, NEG)
            if exists is not None:
                s = jnp.where(exists, s, NEG)
            m_cur = jnp.max(s, axis=-1, keepdims=True)
            if first:
                m_new = jnp.broadcast_to(m_cur, (tq, LANES))
            else:
                m_prev = m_scr[br, r]
                m_new = jnp.maximum(m_prev, m_cur)
                al_scr[slot, r] = jnp.exp(m_prev - m_new)
            m_scr[br, r] = m_new
            p_scr[slot, r] = jnp.exp(s - jnp.concatenate([m_new] * (tk // LANES), axis=1)).astype(BF16)

    def accumulate(br, v_ref, j, slot, first):
        pv = _nn(p_scr[slot], v_ref[0, pl.ds(pl.multiple_of(j * tk, tk), tk), :])
        for hh in range(HEADS_PER_GROUP):
            r = rows_of(hh)
            if first:
                acc_scr[br, r] = pv[r]
            else:
                acc_scr[br, r] = al_scr[slot, r] * acc_scr[br, r] + pv[r]

    n_q = pl.num_programs(2)
    step = (pl.program_id(0) * n_q + i) * pl.num_programs(1) + pl.program_id(1)
    n_tiles = ntiles_ref[step]
    item = lambda br, j, mode, first, exists=None: (br, j, mode, first, exists)
    sel_item = lambda n: item(0, i, "causal", True) if isinstance(n, int) and n == 0 else item(
        0, tiles_ref[step * n_q + n - 1], None, False)
    n_back = WINDOW // tk
    win_items = [item(1, i, "causal", True)] + [
        item(1, jnp.maximum(i - d, 0), "lower" if d == n_back else None, False, i >= d)
        for d in range(1, n_back + 1)]
    k_refs, v_refs = (ks_ref, kw_ref), (vs_ref, vw_ref)

    def run(items, slot0, done=(0, 0, 0), upto=None):
        n_items = len(items)
        slot = lambda n: (slot0 + n) % 2
        n_s, n_p, n_a = done

        def do_scores(n):
            br, j, _, _, _ = items[n]
            scores(k_refs[br], j, slot(n))

        def do_probs(n):
            br, j, mode, first, exists = items[n]
            probs(br, j, mode, slot(n), first, exists)

        def do_acc(n):
            br, j, _, first, _ = items[n]
            accumulate(br, v_refs[br], j, slot(n), first)

        for n in range(n_s, min(n_a + 2, n_items)):
            do_scores(n)
        n_s = max(n_s, min(n_a + 2, n_items))
        for n in range(n_p, min(n_a + 1, n_items)):
            do_probs(n)
        n_p = max(n_p, min(n_a + 1, n_items))
        for k in range(n_a, n_items if upto is None else upto):
            if n_s <= k + 2 < n_items:
                do_scores(k + 2)
                n_s = k + 3
            if n_p <= k + 1 < n_items:
                do_probs(k + 1)
                n_p = k + 2
            do_acc(k)

    for n_static in range(ATT_STATIC_TILES):
        @pl.when(n_tiles == n_static)
        def _(n_static=n_static):
            run([sel_item(n) for n in range(n_static + 1)] + win_items, 0)

    assert ATT_STATIC_TILES >= 2
    @pl.when(n_tiles >= ATT_STATIC_TILES)
    def _():
        run([sel_item(n) for n in range(3)], 0, upto=1)

        def sel_step(n, parity):
            scores(ks_ref, sel_item(n + 2)[1], parity)
            probs(0, None, None, 1 - parity, False)
            accumulate(0, vs_ref, sel_item(n)[1], parity, False)

        def sel_steps(first, count):
            for d in range(count):
                sel_step(first + d, (1 + d) % 2)

        n_pairs_of_steps = (n_tiles - 2) // 2
        n_quads = n_pairs_of_steps // 2

        def sel_four_steps(t, carry):
            sel_steps(4 * t + 1, 4)
            return carry

        lax.fori_loop(0, n_quads, sel_four_steps, 0)

        @pl.when(n_pairs_of_steps % 2 == 1)
        def _():
            sel_steps(4 * n_quads + 1, 2)

        def drain(parity):
            tail = [sel_item(n_tiles - 1), sel_item(n_tiles)] + win_items
            run(tail, 1 - parity, done=(2, 1, 0))

        @pl.when(n_tiles % 2 == 0)
        def _():
            drain(0)

        @pl.when(n_tiles % 2 == 1)
        def _():
            sel_step(n_tiles - 2, 1)
            drain(1)

    gexp = sum(_nn(part, gexp_ref[0]) for part in _split2(_sigmoid(gl_ref[0])))
    n_pairs = HEADS_PER_GROUP // 2
    gate_blk = lambda br, jj: gexp[:, (br * n_pairs + jj) * LANES:(br * n_pairs + jj + 1) * LANES]
    for jj in range(n_pairs):
        cols = slice(jj * LANES, (jj + 1) * LANES)
        blk = gate_blk(0, jj) * ocmp_ref[0, :, cols]
        for br in range(2):
            even = acc_scr[br, rows_of(2 * jj)]
            odd = acc_scr[br, rows_of(2 * jj + 1)]
            low = even / pltpu.roll(even, NSA_HD, axis=1)
            up = pltpu.roll(odd, NSA_HD, axis=1) / odd
            blk = blk + gate_blk(br + 1, jj) * jnp.where(lane_lo, low, up)
        o_ref[0, :, cols] = blk.astype(o_ref.dtype)


def _gate_expansion():
    n_pairs = HEADS_PER_GROUP // 2
    r = np.zeros((NSA_KV_GROUPS, LANES, 3 * n_pairs * LANES), np.float32)
    for g in range(NSA_KV_GROUPS):
        for br in range(3):
            for jj in range(n_pairs):
                for odd in range(2):
                    src = br * NSA_HEADS + g * HEADS_PER_GROUP + 2 * jj + odd
                    dst = (br * n_pairs + jj) * LANES + odd * NSA_HD
                    r[g, src, dst:dst + NSA_HD] = 1.0
    return jnp.asarray(r, BF16)


def _picked_tiles(picks, b, s):
    ns, nq = s // SEL_BLOCK, s // ATT_TQ
    per_tile = ATT_TK // SEL_BLOCK
    per_block = picks[:, :, 0, :].reshape(b, nq, NSA_KV_GROUPS, ns // per_tile, per_tile)
    j = jnp.arange(ns // per_tile, dtype=jnp.int32)
    i = jnp.arange(nq, dtype=jnp.int32)[None, :, None, None]
    active = ((per_block.sum(-1) > 0) | (j == 0)) & (j < i)
    slot = jnp.cumsum(active, axis=-1) - 1
    hit = active[..., None, :] & (slot[..., None, :] == j[:, None])
    tiles = jnp.sum(jnp.where(hit, j, 0), axis=-1).astype(jnp.int32)
    return tiles.reshape(-1), active.sum(-1).astype(jnp.int32).reshape(-1)


def _attention(q, ks, vs, kw, vw, pen, picks, gl, ocmp, b, s):
    tq, tk = ATT_TQ, ATT_TK
    assert tq == tk and WINDOW % tk == 0 and WINDOW // tk <= 2 and tq % SEL_BLOCK == 0
    gw = HEADS_PER_GROUP * LANES
    ow = HEADS_PER_GROUP * NSA_HD
    r3 = lambda a: a.reshape(b, s, a.shape[-1])
    tile = lambda w: pl.BlockSpec((1, tq, w), lambda i, g, j, *_: (i, j, 0))
    gtile = lambda w: pl.BlockSpec((1, tq, w), lambda i, g, j, *_: (i, j, g))
    gfull = pl.BlockSpec((1, s, LANES), lambda i, g, j, *_: (i, 0, g))
    gexp = _gate_expansion()
    rows = HEADS_PER_GROUP * tq
    kern = functools.partial(_attn_kernel, tq=tq, tk=tk)
    tiles, n_tiles = _picked_tiles(picks, b, s)
    grid_spec = pltpu.PrefetchScalarGridSpec(
        num_scalar_prefetch=2,
        grid=(b, NSA_KV_GROUPS, s // tq),
        in_specs=[gtile(gw), gfull, gfull, gfull, gfull, gtile(LANES), tile(LANES),
                  pl.BlockSpec((1,) + gexp.shape[1:], lambda i, g, j, *_: (g, 0, 0)), gtile(ow)],
        out_specs=gtile(ow),
        scratch_shapes=[pltpu.VMEM((rows, LANES), BF16), pltpu.VMEM((2, rows, tk), F32),
                        pltpu.VMEM((2, rows, tk), BF16), pltpu.VMEM((2, rows, LANES), F32),
                        pltpu.VMEM((2, rows, LANES), F32), pltpu.VMEM((2, rows, LANES), F32)])
    return pl.pallas_call(
        kern,
        grid_spec=grid_spec,
        out_shape=jax.ShapeDtypeStruct((b, s, NSA_WIDTH), BF16),
        compiler_params=_params(("parallel", "parallel", "parallel")),
        name="sel_win_attention",
    )(tiles, n_tiles, r3(q), r3(ks), r3(vs), r3(kw), r3(vw), pen, r3(gl), gexp, ocmp)


def _hgrn_chunks(raw, onorm, layer):
    c = HG_CHUNK
    ex = jnp.exp(raw - jnp.max(raw, axis=0, keepdims=True))
    sm = ex / jnp.sum(ex, axis=0, keepdims=True)
    lb_all = jnp.zeros((1, raw.shape[1]), F32)
    for l in range(1, layer + 1):
        lb_all = lb_all + sm[l:l + 1, :]

    t_idx = lax.broadcasted_iota(jnp.int32, (c, HG_DK), 0)
    sub = lax.broadcasted_iota(jnp.int32, (SUBLANES, HG_DK), 0)
    ti = lax.broadcasted_iota(jnp.int32, (c, c), 0)
    si = lax.broadcasted_iota(jnp.int32, (c, c), 1)
    tril = jnp.where(si <= ti, 1.0, 0.0).astype(BF16)
    levels = (32, 16, 8, 4, 2, 1)
    hc = HG_HEADS * c
    tb = lax.broadcasted_iota(jnp.int32, (hc, hc), 0)
    sb = lax.broadcasted_iota(jnp.int32, (hc, hc), 1)
    same_head = (tb // c) == (sb // c)
    split_bit = tb ^ sb
    pair_mask = {m: jnp.where(same_head & (tb > sb) & (split_bit >= m) & (split_bit < 2 * m), 1.0, 0.0)
                 for m in levels}
    upper_groups = {m: [h * c + s0 + m + d for h in range(HG_HEADS) for s0 in range(0, c, 2 * m)
                        for d in range(0, m, SUBLANES)] for m in levels if m % SUBLANES == 0}
    upper_mask = {m: jnp.concatenate([pair_mask[m][g:g + SUBLANES] for g in groups], axis=0)
                  for m, groups in upper_groups.items()}
    diagonal = tb == sb
    second_half ={m: (t_idx & m) != 0 for m in levels}
    sign = {m: jnp.where(second_half[m], 1.0, -1.0) for m in levels}

    def ref_rows(b, m):
        row = lambda r, n: jnp.broadcast_to(b[r:r + 1, :], (n, HG_DK))
        if m >= 4:
            return jnp.concatenate([row(s0 + m - 1, 2 * m) for s0 in range(0, c, 2 * m)], axis=0)
        return jnp.concatenate([jnp.where(sub < 4, row(s0 + 1, SUBLANES), row(s0 + 5, SUBLANES))
                                for s0 in range(0, c, SUBLANES)], axis=0)

    part = lambda y_ref, rows, which, h: y_ref[rows, which * HG_WIDTH + h * HG_DK:which * HG_WIDTH + (h + 1) * HG_DK]

    def prepare(y_ref, rows):
        q, k, f, v, logf = [], [], [], [], []
        for h in range(HG_HEADS):
            lb = lb_all[:, h * HG_DK:(h + 1) * HG_DK]
            z = part(y_ref, rows, 1, h)
            hq = part(y_ref, rows, 0, h)
            q.append(hq * _sigmoid(hq))
            ez = jnp.exp(-jnp.abs(z))
            big = 1.0 / (1.0 + ez)
            small = ez * big
            f.append(jnp.maximum(lb + (1.0 - lb) * jnp.where(z >= 0.0, big, small), F_FLOOR))
            logf.append(jnp.log2(f[h]))
            k.append((1.0 - lb) * jnp.where(z >= 0.0, small, big))
            v.append(part(y_ref, rows, 2, h).astype(BF16))
        parts = jnp.concatenate([p for h in range(HG_HEADS) for p in _split3(logf[h])], axis=1)
        csum = _nn(tril, parts)
        bcum = [sum(csum[:, (3 * h + i) * HG_DK:(3 * h + i + 1) * HG_DK] for i in range(3))
                for h in range(HG_HEADS)]
        return q, k, f, v, bcum

    def finish(prepared, y_ref, rows, st_scr, fill):
        q, k, f, v, bcum = prepared
        rowdot = jnp.concatenate([jnp.sum(q[h] * k[h], axis=-1, keepdims=True) for h in range(HG_HEADS)], axis=0)
        a = jnp.where(diagonal, rowdot, 0.0)
        for m, emits in zip(levels, fill, strict=True):
            for emit in emits:
                emit()
            r = []
            for h in range(HG_HEADS):
                if m == 1:
                    r.append(jnp.where(second_half[m], q[h] * f[h], k[h]))
                else:
                    w = jnp.exp2((bcum[h] - ref_rows(bcum[h], m)) * sign[m])
                    r.append(jnp.where(second_half[m], q[h], k[h]) * w)
            r = jnp.concatenate(r, axis=0)
            if m in upper_groups:
                lhs = jnp.concatenate([r[g:g + SUBLANES] for g in upper_groups[m]], axis=0).astype(BF16)
                p = _nt(lhs, r.astype(BF16)) * upper_mask[m]
                parts = {g: p[n * SUBLANES:(n + 1) * SUBLANES] for n, g in enumerate(upper_groups[m])}
                a = jnp.concatenate([a[g:g + SUBLANES] + parts[g] if g in parts else a[g:g + SUBLANES]
                                     for g in range(0, hc, SUBLANES)], axis=0)
            else:
                r = r.astype(BF16)
                a = a + _nt(r, r) * pair_mask[m]
        o_intra = _nn(a.astype(BF16), jnp.concatenate(v, axis=0))

        heads = []
        for h in range(HG_HEADS):
            st = st_scr[h]
            o = o_intra[h * c:(h + 1) * c] + _nt((q[h] * jnp.exp2(bcum[h])).astype(BF16), st.astype(BF16))
            b_last = bcum[h][c - 1:c, :]
            kd = (k[h] * jnp.exp2(b_last - bcum[h])).astype(BF16)
            st_scr[h] = jnp.exp2(b_last) * st + _tn(v[h], kd)

            o = o * lax.rsqrt(jnp.mean(o * o, axis=-1, keepdims=True) + EPS) * onorm
            gate = part(y_ref, rows, 3, h)
            heads.append((o * (gate * _sigmoid(gate))).astype(BF16))
        return jnp.concatenate(heads, axis=1)

    return prepare, finish, len(levels)


def _mixer(h, l, b, s, mix_norm, w_ext, cmp_k, cmp_v, hgrn_lower_bound, hgrn_out_norm):
    q, ks, vs, kw, vw, kc_in, vc_in, gl, o_hg = _inproj(h, mix_norm, w_ext, hgrn_lower_bound, hgrn_out_norm, s, l)
    kc = _compress(kc_in.reshape(b, s, KV_W), l, *cmp_k)
    vc2 = _compress(vc_in.reshape(b, s, KV_W), l, *cmp_v)
    ocmp, sel, picks = _cmp_attention(q, kc, vc2, b, s)
    o_nsa = _attention(q, ks, vs, kw, vw, sel, picks, gl, ocmp, b, s)
    return o_nsa.reshape(b * s, NSA_WIDTH), o_hg


def kernel(x, ffn1_norm, ffn1_w_gu, ffn1_w_down, mix_norm, w_in, cmp_pos_k, cmp_pos_v, cmp_k_w1, cmp_k_w2, cmp_v_w1, cmp_v_w2, hgrn_lower_bound, hgrn_out_norm, w_out, ffn2_norm, ffn2_w_gu, ffn2_w_down, final_norm):
    b, s, d = x.shape
    depth = ffn1_norm.shape[0]
    ffn1_w = _ffn_weights(ffn1_w_gu, ffn1_w_down)
    ffn2_w = _ffn_weights(ffn2_w_gu, ffn2_w_down)
    w_out = w_out.astype(BF16)
    w_ext = _build_w_in(w_in)
    cmp_k = _compress_weights(cmp_pos_k, cmp_k_w1, cmp_k_w2, (0, 1))
    cmp_v = _compress_weights(cmp_pos_v, cmp_v_w1, cmp_v_w2, (0, 1, 1, 0))
    h = x.reshape(b * s, d)
    for l in range(depth):
        h = _ffn(h, l, ffn1_norm, *ffn1_w, final_norm, False)
        o_nsa, o_hg = _mixer(h, l, b, s, mix_norm, w_ext, cmp_k, cmp_v, hgrn_lower_bound, hgrn_out_norm)
        h = _ffn(h, l, ffn2_norm, *ffn2_w, final_norm, l == depth - 1, (o_nsa, o_hg, w_out))
    return h.reshape(b, s, d)
```

```python
import functools

import jax
import jax.numpy as jnp
import numpy as np
from jax import lax
from jax.experimental import pallas as pl
from jax.experimental.pallas import tpu as pltpu

F32 = jnp.float32
BF16 = jnp.bfloat16

D_MODEL = 1024
EPS = 1e-6
NEG = -1e30
F_FLOOR = 1e-30
NSA_HEADS = 8
NSA_KV_GROUPS = 2
HEADS_PER_GROUP = NSA_HEADS // NSA_KV_GROUPS
NSA_HD = 64
CMP_LEN = 32
CMP_STRIDE = 16
CMP_HID = 256
SEL_BLOCK = 64
SEL_TOPK = 16
WINDOW = 512
HG_HEADS = 4
HG_DK = 128
HG_DV = 128
HG_CHUNK = 64
D_FF = 2752
NSA_WIDTH = NSA_HEADS * NSA_HD
HG_WIDTH = HG_HEADS * HG_DV
KV_W = NSA_KV_GROUPS * NSA_HD

LANES = 128
OCMP_GROUP_W = HEADS_PER_GROUP * NSA_HD + LANES
SUBLANES = 8
MXU_COLS = 256
FF_CHUNK = 256
D_FF_PAD = -(-D_FF // FF_CHUNK) * FF_CHUNK
ROW_TILE = 512
FFN_ROWS = 1024
ATT_TQ = 256
ATT_TK = 256
ATT_STATIC_TILES = 7
CMP_TQ = 1024
VMEM_LIMIT = 56 * 1024 * 1024

SEL_PENALTY = 2.0 ** 50
ALIBI_SLOPES = tuple(2.0 ** (-8.0 * (i + 1) / NSA_HEADS) for i in range(NSA_HEADS))

SEGMENTS = (
    ("q", NSA_HEADS * LANES, BF16, NSA_WIDTH),
    ("ks", NSA_KV_GROUPS * LANES, BF16, KV_W),
    ("vs", NSA_KV_GROUPS * LANES, BF16, KV_W),
    ("kw", NSA_KV_GROUPS * LANES, BF16, KV_W),
    ("vw", NSA_KV_GROUPS * LANES, BF16, KV_W),
    ("kc", KV_W, F32, KV_W),
    ("vc", KV_W, F32, KV_W),
    ("gl", LANES, F32, LANES),
    ("hg", HG_WIDTH, BF16, 4 * HG_WIDTH),
)
SEG_OFFSETS = tuple(int(v) for v in np.cumsum([0] + [s[3] for s in SEGMENTS]))


def _nn(a, b):
    return jnp.dot(a, b, preferred_element_type=F32)


def _nt(a, b):
    return lax.dot_general(a, b, (((1,), (1,)), ((), ())), preferred_element_type=F32)


def _tn(a, b):
    return lax.dot_general(a, b, (((0,), (0,)), ((), ())), preferred_element_type=F32)


def _split2(x):
    hi = x.astype(BF16)
    lo = (x - hi.astype(F32)).astype(BF16)
    return hi, lo


def _split3(x):
    hi = x.astype(BF16)
    r = x - hi.astype(F32)
    mid = r.astype(BF16)
    lo = (r - mid.astype(F32)).astype(BF16)
    return hi, mid, lo


def _sigmoid(x):
    return 1.0 / (1.0 + jnp.exp(-x))


def _rms(x, g):
    return x * lax.rsqrt(jnp.mean(x * x, axis=-1, keepdims=True) + EPS) * g


def _resident(shape):
    nd = len(shape)
    return pl.BlockSpec(shape, lambda *_: (0,) * nd, pipeline_mode=pl.Buffered(1))


def _resident_layer(stacked_shape, layer):
    nd = len(stacked_shape)
    return pl.BlockSpec((None,) + tuple(stacked_shape[1:]), lambda *_: (layer,) + (0,) * (nd - 1),
                        pipeline_mode=pl.Buffered(1))


def _params(sem):
    return pltpu.CompilerParams(dimension_semantics=sem, vmem_limit_bytes=VMEM_LIMIT)


def _ffn_kernel(*refs, final, proj):
    if proj:
        x_ref, a_ref, b_ref, wo_ref, g_ref, wg_ref, wu_ref, wd_ref, gf_ref, o_ref = refs
        x = x_ref[...] + _nn(a_ref[...], wo_ref[:NSA_WIDTH, :]) + _nn(b_ref[...], wo_ref[NSA_WIDTH:, :])
    else:
        x_ref, g_ref, wg_ref, wu_ref, wd_ref, gf_ref, o_ref = refs
        x = x_ref[...]
    xn = _rms(x, g_ref[...]).astype(BF16)
    acc = jnp.zeros(x.shape, F32)
    for k in range(D_FF_PAD // FF_CHUNK):
        sl = slice(k * FF_CHUNK, (k + 1) * FF_CHUNK)
        gate = _nn(xn, wg_ref[:, sl])
        up = _nn(xn, wu_ref[:, sl])
        h = (gate * _sigmoid(gate) * up).astype(BF16)
        acc = acc + _nn(h, wd_ref[sl, :])
    y = x + 0.5 * acc
    if final:
        y = _rms(y, gf_ref[...])
    o_ref[...] = y


def _ffn_weights(w_gu, w_down):
    pad = D_FF_PAD - D_FF
    wg = jnp.pad(w_gu[..., :D_FF], ((0, 0), (0, 0), (0, pad))).astype(BF16)
    wu = jnp.pad(w_gu[..., D_FF:], ((0, 0), (0, 0), (0, pad))).astype(BF16)
    wd = jnp.pad(w_down, ((0, 0), (0, pad), (0, 0))).astype(BF16)
    return wg, wu, wd


def _ffn(h, layer, norm_g, wg, wu, wd, final_g, final, proj=None):
    t = h.shape[0]
    row = lambda w: pl.BlockSpec((FFN_ROWS, w), lambda i: (i, 0))
    norm_g = norm_g[:, None, :]
    weights = [_resident_layer(a.shape, layer) for a in (norm_g, wg, wu, wd)] + [_resident((1, D_MODEL))]
    operands = [norm_g, wg, wu, wd, final_g.reshape(1, -1)]
    if proj is None:
        in_specs, args = [row(D_MODEL)] + weights, [h] + operands
    else:
        o_nsa, o_hg, w_out = proj
        in_specs = [row(D_MODEL), row(NSA_WIDTH), row(HG_WIDTH), _resident_layer(w_out.shape, layer)] + weights
        args = [h, o_nsa, o_hg, w_out] + operands
    return pl.pallas_call(
        functools.partial(_ffn_kernel, final=final, proj=proj is not None),
        grid=(t // FFN_ROWS,),
        in_specs=in_specs,
        out_specs=row(D_MODEL),
        out_shape=jax.ShapeDtypeStruct((t, D_MODEL), F32),
        compiler_params=_params(("parallel",)),
        name="ffn",
    )(*args)


def _inproj_kernel(x_ref, g_ref, w_ref, qfeat_ref, lbraw_ref, onorm_ref, *refs, seq, layer):
    *o_refs, xn_scr, y_scr, st_scr = refs
    out = {seg[0]: o_ref for seg, o_ref in zip(SEGMENTS, o_refs, strict=True)}
    col0 = {seg[0]: off for seg, off in zip(SEGMENTS, SEG_OFFSETS[:-1], strict=True)}
    rows = x_ref.shape[0]
    half = rows // 2
    tile = pl.program_id(0) % (seq // rows)

    @pl.when(tile == 0)
    def _():
        st_scr[...] = jnp.zeros(st_scr.shape, F32)

    def norm_piece(r):
        rs = slice(r * half, (r + 1) * half)
        xn_scr[rs, :] = _rms(x_ref[rs, :], g_ref[...]).astype(BF16)

    lane = lax.broadcasted_iota(jnp.int32, (half, LANES), 1)
    lower = lane < NSA_HD
    swap = lambda a: pltpu.roll(a, NSA_HD, axis=1)
    halves = [NSA_HD * (1 - g) for g in range(NSA_KV_GROUPS)]
    project = lambda rs, c0, cols: _nn(xn_scr[rs, :], w_ref[:, c0:c0 + cols])

    def q_piece(r, cb):
        def emit():
            rs = slice(r * half, (r + 1) * half)
            y = project(rs, col0["q"] + cb * MXU_COLS, MXU_COLS)
            per = MXU_COLS // NSA_HD
            blocks = []
            for h in range(cb * per, (cb + 1) * per):
                pair = y[:, (h % per // 2) * LANES:(h % per // 2 + 1) * LANES]
                g = h // HEADS_PER_GROUP
                data = pair if h % 2 == g else swap(pair)
                feat = qfeat_ref[:, h * LANES:(h + 1) * LANES]
                blocks.append(jnp.where(lower == (g == 0), data, feat))
            out["q"][rs, cb * per * LANES:(cb + 1) * per * LANES] = (
                jnp.concatenate(blocks, axis=1).astype(out["q"].dtype))
        return emit

    small = [seg[0] for seg in SEGMENTS if seg[3] == LANES]
    assert all(col0[b] - col0[a] == LANES for a, b in zip(small, small[1:]))

    def small_piece(r, names):
        def emit():
            rs = slice(r * half, (r + 1) * half)
            y_small = project(rs, col0[names[0]], len(names) * LANES)
            kpos = tile * rows + r * half + lax.broadcasted_iota(jnp.int32, (half, LANES), 0)
            blk = kpos // SEL_BLOCK
            digit = lambda f0: jnp.where(lane == f0, blk.astype(F32),
                                         jnp.where(lane == f0 + 1, (kpos % SEL_BLOCK).astype(F32), 0.0))
            is_pen = lambda f0: (lane == f0 + 1 + blk) & (blk >= 1) & (blk <= seq // SEL_BLOCK - 2)
            for n, name in enumerate(names):
                y = y_small[:, n * LANES:(n + 1) * LANES]
                if name in ("ks", "kw"):
                    feat = [digit(f0) + jnp.where(is_pen(f0), -SEL_PENALTY, 0.0) if name == "ks" else digit(f0)
                            for f0 in halves]
                    y = jnp.concatenate([jnp.where(lower == (g == 0), y, feat[g])
                                         for g in range(NSA_KV_GROUPS)], axis=1)
                elif name in ("vs", "vw"):
                    y = jnp.concatenate([jnp.where(lower, y if g == 0 else swap(y), 1.0)
                                         for g in range(NSA_KV_GROUPS)], axis=1)
                out[name][rs, :] = y.astype(out[name].dtype)
        return emit

    def hg_piece(r, cb):
        def emit():
            rs = slice(r * half, (r + 1) * half)
            y_scr[rs, cb * MXU_COLS:(cb + 1) * MXU_COLS] = project(rs, col0["hg"] + cb * MXU_COLS, MXU_COLS)
        return emit

    prepare, finish, n_slots = _hgrn_chunks(lbraw_ref[...], onorm_ref[...], layer)
    n_chunks = rows // HG_CHUNK
    chunk_rows = lambda ci: slice(ci * HG_CHUNK, (ci + 1) * HG_CHUNK)
    hg_tiles = 4 * HG_WIDTH // MXU_COLS
    norm_piece(0)
    hg_piece(0, 0)()
    norm_piece(1)
    for cb in range(1, hg_tiles):
        hg_piece(0, cb)()
    later = [hg_piece(1, cb) for cb in range(hg_tiles)]
    n_early = n_chunks // 2 - 1
    rest = [p for r in range(2) for p in (q_piece(r, 0), q_piece(r, 1), small_piece(r, small[0:2]),
                                          small_piece(r, small[2:4]), small_piece(r, small[4:]))]
    spread = lambda items, n: [items[i * len(items) // n:(i + 1) * len(items) // n] for i in range(n)]
    tiles = spread(later, n_early) + spread(rest, n_chunks - n_early)
    last = tiles[-1].pop()
    prepared = prepare(y_scr, chunk_rows(0))
    for ci in range(n_chunks):
        slots = [[] for _ in range(n_slots)]
        for j, emit in enumerate(tiles[ci]):
            slots[j * n_slots // len(tiles[ci])].append(emit)
        following = []
        if ci + 1 < n_chunks:
            slots[0].append(lambda ci=ci: following.append(prepare(y_scr, chunk_rows(ci + 1))))
        out["hg"][chunk_rows(ci), :] = finish(prepared, y_scr, chunk_rows(ci), st_scr, slots)
        prepared = following[0] if following else None
    last()


def _q_features():
    feat = np.zeros((1, NSA_HEADS * LANES), np.float32)
    for h in range(NSA_HEADS):
        f0 = h * LANES + NSA_HD * (1 - h // HEADS_PER_GROUP)
        feat[0, f0] = SEL_BLOCK * ALIBI_SLOPES[h]
        feat[0, f0 + 1] = ALIBI_SLOPES[h]
    return jnp.asarray(feat)


def _build_w_in(w_in):
    sizes = (NSA_WIDTH, KV_W, KV_W, KV_W, KV_W, KV_W, KV_W, NSA_HEADS * 3,
             HG_WIDTH, HG_WIDTH, HG_WIDTH, HG_WIDTH)
    splits = [int(v) for v in np.cumsum(sizes)[:-1]]
    w_in = w_in.astype(BF16)
    wq, wkc, wvc, wks, wvs, wkw, wvw, wgl, whq, whf, whi, whg = jnp.split(w_in, splits, axis=-1)
    lead = w_in.shape[:-1]
    gl = jnp.swapaxes(wgl.reshape(lead + (NSA_HEADS, 3)), -1, -2).reshape(lead + (3 * NSA_HEADS,))
    gl = jnp.pad(gl, [(0, 0)] * len(lead) + [(0, LANES - 3 * NSA_HEADS)])
    cols = [wq * NSA_HD ** -0.5, wks, wvs, wkw, wvw, wkc, wvc, gl, whq, whf, whi, whg]
    return jnp.concatenate(cols, axis=-1)


def _inproj(h, norm_g, w_ext, lb_raw, out_norm, seq, layer):
    t = h.shape[0]
    assert seq % ROW_TILE == 0 and ROW_TILE % HG_CHUNK == 0
    row = lambda w: pl.BlockSpec((ROW_TILE, w), lambda i: (i, 0))
    norm_g, out_norm = norm_g[:, None, :], out_norm[:, None, :]
    return pl.pallas_call(
        functools.partial(_inproj_kernel, seq=seq, layer=layer),
        grid=(t // ROW_TILE,),
        in_specs=[row(D_MODEL), _resident_layer(norm_g.shape, layer), _resident_layer(w_ext.shape, layer),
                  _resident((1, NSA_HEADS * LANES)), _resident(lb_raw.shape),
                  _resident_layer(out_norm.shape, layer)],
        out_specs=[row(w) for _, w, _, _ in SEGMENTS],
        out_shape=[jax.ShapeDtypeStruct((t, w), dt) for _, w, dt, _ in SEGMENTS],
        scratch_shapes=[pltpu.VMEM((ROW_TILE, D_MODEL), BF16), pltpu.VMEM((ROW_TILE, 4 * HG_WIDTH), F32),
                        pltpu.VMEM((HG_HEADS, HG_DV, HG_DK), F32)],
        compiler_params=_params(("arbitrary",)),
        name="inproj_hgrn",
    )(h, norm_g, w_ext, _q_features(), lb_raw, out_norm)


def _gelu_tanh(x):
    return 0.5 * x * (1.0 + jnp.tanh(0.7978845608028654 * (x + 0.044715 * (x * x * x))))


def _compress_kernel(kv_ref, pos_ref, w1_ref, w2_ref, o_ref):
    nbp = o_ref.shape[1]
    tokens = [kv_ref[0, pl.ds(l, nbp, stride=CMP_STRIDE), :] for l in range(CMP_STRIDE)]
    chunk = lambda first: jnp.concatenate(
        [(tokens[l] + pos_ref[first + l:first + l + 1, :]).astype(BF16) for l in range(CMP_STRIDE)], axis=1)
    slab = lambda first: w1_ref[first:first + CMP_STRIDE].reshape(CMP_STRIDE * KV_W, NSA_KV_GROUPS * CMP_HID)
    ha = _nn(chunk(0), slab(0))
    hb = _nn(chunk(CMP_STRIDE), slab(CMP_STRIDE))
    act = _gelu_tanh(ha + pltpu.roll(hb, nbp - 1, axis=0)).astype(BF16)
    out = jnp.zeros(o_ref.shape[1:], F32)
    for g in range(NSA_KV_GROUPS):
        out = out + _nn(act[:, g * CMP_HID:(g + 1) * CMP_HID], w2_ref[g])
    o_ref[0] = out


def _compress_weights(pos, w1, w2, reps):
    layers = w1.shape[0]
    w1l = w1.astype(BF16).reshape(layers, CMP_LEN, NSA_HD, CMP_HID)
    z1 = jnp.zeros_like(w1l)
    w1p = jnp.concatenate([jnp.concatenate([w1l, z1], axis=3), jnp.concatenate([z1, w1l], axis=3)], axis=2)
    pos2 = jnp.concatenate([pos] * NSA_KV_GROUPS, axis=2)
    w2 = w2.astype(BF16)
    zero = jnp.zeros_like(w2)
    w2p = jnp.stack([jnp.concatenate([w2 if r == g else zero for r in reps], axis=2)
                     for g in range(NSA_KV_GROUPS)], axis=1)
    return pos2, w1p, w2p


def _compress(kv, layer, pos2, w1p, w2p):
    b, s, _ = kv.shape
    nbp = s // CMP_STRIDE
    width = w2p.shape[-1]
    return pl.pallas_call(
        _compress_kernel,
        grid=(b,),
        in_specs=[pl.BlockSpec((1, s, KV_W), lambda i: (i, 0, 0))]
        + [_resident_layer(a.shape, layer) for a in (pos2, w1p, w2p)],
        out_specs=pl.BlockSpec((1, nbp, width), lambda i: (i, 0, 0)),
        out_shape=jax.ShapeDtypeStruct((b, nbp, width), F32),
        compiler_params=_params(("parallel",)),
        name="compress",
    )(kv, pos2, w1p, w2p)


def _pair_blocks(g, jj):
    lower = slice(0, LANES) if g == 0 else slice(LANES, 2 * LANES)
    upper = slice(LANES, 2 * LANES) if g == 0 else slice(0, LANES)
    return lower, upper


def _cmp_kernel(*refs, tq, ns, n_sel):
    total = refs[1].shape[1]
    per_tile = tq // CMP_STRIDE
    for t in range(total // per_tile):
        nbp = min(total, ((t + 1) * per_tile + LANES - 1) // LANES * LANES)
        sel_rows = min(ns, (t + 1) * tq // SEL_BLOCK)
        rounds = 0 if sel_rows <= n_sel else n_sel - 3

        @pl.when(pl.program_id(1) == t)
        def _(nbp=nbp, sel_rows=sel_rows, rounds=rounds):
            _cmp_tile(*refs, tq=tq, ns=ns, nbp=nbp, sel_rows=sel_rows, rounds=rounds)


def _cmp_tile(q_ref, kc_ref, vc2_ref, mt_ref, ocmp_ref, picks_ref, *, tq, ns, nbp, sel_rows, rounds):
    q0 = pl.program_id(1) * tq
    pos = q0 + lax.broadcasted_iota(jnp.int32, (tq, nbp), 0)
    blk_end = lax.broadcasted_iota(jnp.int32, (tq, nbp), 1) * CMP_STRIDE + (CMP_LEN - 1)
    valid = blk_end <= pos
    row_ok = (q0 + lax.broadcasted_iota(jnp.int32, (tq, 1), 0)) >= CMP_LEN - 1
    kc = kc_ref[0, :nbp, :].astype(BF16)
    vc2 = vc2_ref[0, :nbp, :].astype(BF16)
    lane_c = lax.broadcasted_iota(jnp.int32, (nbp, LANES), 1)
    c_idx = lax.broadcasted_iota(jnp.int32, (nbp, LANES), 0)
    per = SEL_BLOCK // CMP_STRIDE
    feat_a = (c_idx // per - q0 // SEL_BLOCK).astype(F32)
    feat_b = ((c_idx % per) * CMP_STRIDE + (CMP_LEN - 1)).astype(F32)
    lane_lo = lax.broadcasted_iota(jnp.int32, (tq, LANES), 1) < NSA_HD
    blk = lax.broadcasted_iota(jnp.int32, (ns, tq), 0)
    blk_r = lax.broadcasted_iota(jnp.int32, (sel_rows, tq), 0)
    blk_f = blk_r.astype(F32)
    pos_t = q0 + lax.broadcasted_iota(jnp.int32, (sel_rows, tq), 1)
    cur = pos_t // SEL_BLOCK
    forced = (blk_r == 0) | (blk_r == cur) | (blk_r == cur - 1)
    causal = blk_r * SEL_BLOCK <= pos_t
    sel_t, picks = [], []
    taken = -jnp.inf
    for g in range(NSA_KV_GROUPS):
        f0 = NSA_HD * (1 - g)
        feat = jnp.where(lane_c == f0, feat_a, jnp.where(lane_c == f0 + 1, feat_b, 0.0))
        kc_g = jnp.where((lane_c // NSA_HD) == g, kc, feat.astype(BF16))
        imp = jnp.zeros((tq, nbp), F32)
        acc = []
        for hh in range(HEADS_PER_GROUP):
            h = g * HEADS_PER_GROUP + hh
            s = _nt(q_ref[0, :, h * LANES:(h + 1) * LANES], kc_g)
            s = jnp.where(valid, s, NEG)
            e = jnp.exp(s - jnp.max(s, axis=-1, keepdims=True))
            inv = jnp.where(row_ok, 1.0 / jnp.sum(e, axis=-1, keepdims=True), 0.0)
            p = e * inv
            imp = imp + p
            acc.append(_nn(p.astype(BF16), vc2))
        for jj in range(HEADS_PER_GROUP // 2):
            lower, upper = _pair_blocks(g, jj)
            blk_out = jnp.where(lane_lo, acc[2 * jj][:, lower], acc[2 * jj + 1][:, upper])
            c0 = g * OCMP_GROUP_W + 2 * jj * NSA_HD
            ocmp_ref[0, :, c0:c0 + LANES] = blk_out
        p_slc = sum(_nt(mt_ref[:sel_rows, :nbp], part) for part in _split3(imp))
        if rounds == 0:
            score = jnp.where(forced | causal, taken, NEG)
        else:
            score = jnp.where(forced, taken, jnp.where(causal, p_slc, NEG))
        for _ in range(rounds):
            top = jnp.max(score, axis=0, keepdims=True)
            first = jnp.min(jnp.where(score == top, blk_f, float(ns)), axis=0, keepdims=True)
            score = jnp.where(blk_f == first, taken, score)
        if sel_rows < ns:
            score = jnp.concatenate([score, jnp.full((ns - sel_rows, tq), NEG, F32)], axis=0)
        picked = jnp.where(score == taken, 1.0, 0.0).astype(BF16)
        picks.append([_nt(jnp.ones((SUBLANES, ATT_TQ), BF16), picked[:, t * ATT_TQ:(t + 1) * ATT_TQ])
                      for t in range(tq // ATT_TQ)])
        pen = jnp.where(score == taken, 0.0, SEL_PENALTY)
        pen = jnp.where((blk >= 2) & (blk <= ns - 1), pltpu.roll(pen, 1, axis=0), 0.0)
        if ns < NSA_HD:
            pen = jnp.concatenate([pen, jnp.zeros((NSA_HD - ns, tq), F32)], axis=0)
        zero = jnp.zeros((NSA_HD, tq), F32)
        sel_t += [zero, pen] if g == 0 else [pen, zero]
    sel = jnp.concatenate(sel_t, axis=0).T.astype(BF16).astype(F32)
    for g in range(NSA_KV_GROUPS):
        ocmp_ref[0, :, (g + 1) * OCMP_GROUP_W - LANES:(g + 1) * OCMP_GROUP_W] = sel[:, g * LANES:(g + 1) * LANES]
    for t in range(tq // ATT_TQ):
        picks_ref[0, t] = jnp.concatenate([group[t] for group in picks], axis=1)


def _importance_matrix(nbp, ns):
    per = SEL_BLOCK // CMP_STRIDE
    m = np.zeros((ns, nbp), np.float32)
    for n in range(ns):
        for c in range(per * n, per * (n + 1)):
            for cc in (c - 1, c):
                if 0 <= cc < nbp - 1:
                    m[n, cc] += 1.0
    return jnp.asarray(m, BF16)


def _cmp_attention(q, kc, vc2, b, s):
    nbp = s // CMP_STRIDE
    ns = s // SEL_BLOCK
    tq = min(CMP_TQ, s)
    assert ns <= NSA_HD
    assert min(SEL_TOPK, ns) >= 3 and tq % ATT_TQ == 0 and tq % (SUBLANES * SEL_BLOCK) == 0
    q3 = q.reshape(b, s, NSA_HEADS * LANES)
    kern = functools.partial(_cmp_kernel, tq=tq, ns=ns, n_sel=min(SEL_TOPK, ns))
    return pl.pallas_call(
        kern,
        grid=(b, s // tq),
        in_specs=[pl.BlockSpec((1, tq, NSA_HEADS * LANES), lambda i, j: (i, j, 0)),
                  pl.BlockSpec((1, nbp, KV_W), lambda i, j: (i, 0, 0)),
                  pl.BlockSpec((1, nbp, 2 * KV_W), lambda i, j: (i, 0, 0)),
                  _resident((ns, nbp))],
        out_specs=[pl.BlockSpec((1, tq, NSA_KV_GROUPS * OCMP_GROUP_W), lambda i, j: (i, j, 0)),
                   pl.BlockSpec((1, tq // ATT_TQ, SUBLANES, NSA_KV_GROUPS * ns), lambda i, j: (i, j, 0, 0))],
        out_shape=[jax.ShapeDtypeStruct((b, s, NSA_KV_GROUPS * OCMP_GROUP_W), F32),
                   jax.ShapeDtypeStruct((b, s // ATT_TQ, SUBLANES, NSA_KV_GROUPS * ns), F32)],
        compiler_params=_params(("parallel", "parallel")),
        name="cmp_topk",
    )(q3, kc, vc2, _importance_matrix(nbp, ns))


def _attn_kernel(tiles_ref, ntiles_ref, q_ref, ks_ref, vs_ref, kw_ref, vw_ref, gl_ref, gexp_ref,
                 ocmp_ref, o_ref, qa_scr, s_scr, p_scr, al_scr, m_scr, acc_scr, *, tq, tk):
    i = pl.program_id(2)
    rows_of = lambda hh: slice(hh * tq, (hh + 1) * tq)
    lane_lo = lax.broadcasted_iota(jnp.int32, (tq, LANES), 1) < NSA_HD
    rel = (lax.broadcasted_iota(jnp.int32, (tq, tk), 1)
           - lax.broadcasted_iota(jnp.int32, (tq, tk), 0)).astype(F32)

    for hh in range(HEADS_PER_GROUP):
        qa_scr[rows_of(hh)] = (q_ref[0, :, hh * LANES:(hh + 1) * LANES]
                               + ocmp_ref[0, :, OCMP_GROUP_W - LANES:].astype(BF16))

    def scores(k_ref, j, slot):
        s_scr[slot] = _nt(qa_scr[...], k_ref[0, pl.ds(pl.multiple_of(j * tk, tk), tk), :])

    def probs(br, j, mode, slot, first, exists=None):
        for hh in range(HEADS_PER_GROUP):
            r = rows_of(hh)
            s = s_scr[slot, r]
            if mode == "causal":
                s = jnp.where(rel <= 0.0, s, NEG)
            elif mode == "lower":
                s = jnp.where(rel + (j * tk - i * tq).astype(F32) > -float(WINDOW), s, NEG)
            if exists is not None:
                s = jnp.where(exists, s, NEG)
            m_cur = jnp.max(s, axis=-1, keepdims=True)
            if first:
                m_new = jnp.broadcast_to(m_cur, (tq, LANES))
            else:
                m_prev = m_scr[br, r]
                m_new = jnp.maximum(m_prev, m_cur)
                al_scr[slot, r] = jnp.exp(m_prev - m_new)
            m_scr[br, r] = m_new
            p_scr[slot, r] = jnp.exp(s - jnp.concatenate([m_new] * (tk // LANES), axis=1)).astype(BF16)

    def accumulate(br, v_ref, j, slot, first):
        pv = _nn(p_scr[slot], v_ref[0, pl.ds(pl.multiple_of(j * tk, tk), tk), :])
        for hh in range(HEADS_PER_GROUP):
            r = rows_of(hh)
            if first:
                acc_scr[br, r] = pv[r]
            else:
                acc_scr[br, r] = al_scr[slot, r] * acc_scr[br, r] + pv[r]

    n_q = pl.num_programs(2)
    step = (pl.program_id(0) * n_q + i) * pl.num_programs(1) + pl.program_id(1)
    n_tiles = ntiles_ref[step]
    item = lambda br, j, mode, first, exists=None: (br, j, mode, first, exists)
    sel_item = lambda n: item(0, i, "causal", True) if isinstance(n, int) and n == 0 else item(
        0, tiles_ref[step * n_q + n - 1], None, False)
    n_back = WINDOW // tk
    win_items = [item(1, i, "causal", True)] + [
        item(1, jnp.maximum(i - d, 0), "lower" if d == n_back else None, False, i >= d)
        for d in range(1, n_back + 1)]
    k_refs, v_refs = (ks_ref, kw_ref), (vs_ref, vw_ref)

    def run(items, slot0, done=(0, 0, 0), upto=None):
        n_items = len(items)
        slot = lambda n: (slot0 + n) % 2
        n_s, n_p, n_a = done

        def do_scores(n):
            br, j, _, _, _ = items[n]
            scores(k_refs[br], j, slot(n))

        def do_probs(n):
            br, j, mode, first, exists = items[n]
            probs(br, j, mode, slot(n), first, exists)

        def do_acc(n):
            br, j, _, first, _ = items[n]
            accumulate(br, v_refs[br], j, slot(n), first)

        for n in range(n_s, min(n_a + 2, n_items)):
            do_scores(n)
        n_s = max(n_s, min(n_a + 2, n_items))
        for n in range(n_p, min(n_a + 1, n_items)):
            do_probs(n)
        n_p = max(n_p, min(n_a + 1, n_items))
        for k in range(n_a, n_items if upto is None else upto):
            if n_s <= k + 2 < n_items:
                do_scores(k + 2)
                n_s = k + 3
            if n_p <= k + 1 < n_items:
                do_probs(k + 1)
                n_p = k + 2
            do_acc(k)

    for n_static in range(ATT_STATIC_TILES):
        @pl.when(n_tiles == n_static)
        def _(n_static=n_static):
            run([sel_item(n) for n in range(n_static + 1)] + win_items, 0)

    assert ATT_STATIC_TILES >= 2
    @pl.when(n_tiles >= ATT_STATIC_TILES)
    def _():
        run([sel_item(n) for n in range(3)], 0, upto=1)

        def sel_step(n, parity):
            scores(ks_ref, sel_item(n + 2)[1], parity)
            probs(0, None, None, 1 - parity, False)
            accumulate(0, vs_ref, sel_item(n)[1], parity, False)

        def sel_steps(first, count):
            for d in range(count):
                sel_step(first + d, (1 + d) % 2)

        n_pairs_of_steps = (n_tiles - 2) // 2
        n_quads = n_pairs_of_steps // 2

        def sel_four_steps(t, carry):
            sel_steps(4 * t + 1, 4)
            return carry

        lax.fori_loop(0, n_quads, sel_four_steps, 0)

        @pl.when(n_pairs_of_steps % 2 == 1)
        def _():
            sel_steps(4 * n_quads + 1, 2)

        def drain(parity):
            tail = [sel_item(n_tiles - 1), sel_item(n_tiles)] + win_items
            run(tail, 1 - parity, done=(2, 1, 0))

        @pl.when(n_tiles % 2 == 0)
        def _():
            drain(0)

        @pl.when(n_tiles % 2 == 1)
        def _():
            sel_step(n_tiles - 2, 1)
            drain(1)

    gexp = sum(_nn(part, gexp_ref[0]) for part in _split2(_sigmoid(gl_ref[0])))
    n_pairs = HEADS_PER_GROUP // 2
    gate_blk = lambda br, jj: gexp[:, (br * n_pairs + jj) * LANES:(br * n_pairs + jj + 1) * LANES]
    for jj in range(n_pairs):
        cols = slice(jj * LANES, (jj + 1) * LANES)
        blk = gate_blk(0, jj) * ocmp_ref[0, :, cols]
        for br in range(2):
            even = acc_scr[br, rows_of(2 * jj)]
            odd = acc_scr[br, rows_of(2 * jj + 1)]
            low = even / pltpu.roll(even, NSA_HD, axis=1)
            up = pltpu.roll(odd, NSA_HD, axis=1) / odd
            blk = blk + gate_blk(br + 1, jj) * jnp.where(lane_lo, low, up)
        o_ref[0, :, cols] = blk.astype(o_ref.dtype)


def _gate_expansion():
    n_pairs = HEADS_PER_GROUP // 2
    r = np.zeros((NSA_KV_GROUPS, LANES, 3 * n_pairs * LANES), np.float32)
    for g in range(NSA_KV_GROUPS):
        for br in range(3):
            for jj in range(n_pairs):
                for odd in range(2):
                    src = br * NSA_HEADS + g * HEADS_PER_GROUP + 2 * jj + odd
                    dst = (br * n_pairs + jj) * LANES + odd * NSA_HD
                    r[g, src, dst:dst + NSA_HD] = 1.0
    return jnp.asarray(r, BF16)


def _picked_tiles(picks, b, s):
    ns, nq = s // SEL_BLOCK, s // ATT_TQ
    per_tile = ATT_TK // SEL_BLOCK
    per_block = picks[:, :, 0, :].reshape(b, nq, NSA_KV_GROUPS, ns // per_tile, per_tile)
    j = jnp.arange(ns // per_tile, dtype=jnp.int32)
    i = jnp.arange(nq, dtype=jnp.int32)[None, :, None, None]
    active = ((per_block.sum(-1) > 0) | (j == 0)) & (j < i)
    slot = jnp.cumsum(active, axis=-1) - 1
    hit = active[..., None, :] & (slot[..., None, :] == j[:, None])
    tiles = jnp.sum(jnp.where(hit, j, 0), axis=-1).astype(jnp.int32)
    return tiles.reshape(-1), active.sum(-1).astype(jnp.int32).reshape(-1)


def _attention(q, ks, vs, kw, vw, picks, gl, ocmp, b, s):
    tq, tk = ATT_TQ, ATT_TK
    assert tq == tk and WINDOW % tk == 0 and WINDOW // tk <= 2 and tq % SEL_BLOCK == 0
    gw = HEADS_PER_GROUP * LANES
    ow = HEADS_PER_GROUP * NSA_HD
    r3 = lambda a: a.reshape(b, s, a.shape[-1])
    tile = lambda w: pl.BlockSpec((1, tq, w), lambda i, g, j, *_: (i, j, 0))
    gtile = lambda w: pl.BlockSpec((1, tq, w), lambda i, g, j, *_: (i, j, g))
    gfull = pl.BlockSpec((1, s, LANES), lambda i, g, j, *_: (i, 0, g))
    gexp = _gate_expansion()
    rows = HEADS_PER_GROUP * tq
    kern = functools.partial(_attn_kernel, tq=tq, tk=tk)
    tiles, n_tiles = _picked_tiles(picks, b, s)
    grid_spec = pltpu.PrefetchScalarGridSpec(
        num_scalar_prefetch=2,
        grid=(b, NSA_KV_GROUPS, s // tq),
        in_specs=[gtile(gw), gfull, gfull, gfull, gfull, tile(LANES),
                  pl.BlockSpec((1,) + gexp.shape[1:], lambda i, g, j, *_: (g, 0, 0)), gtile(OCMP_GROUP_W)],
        out_specs=gtile(ow),
        scratch_shapes=[pltpu.VMEM((rows, LANES), BF16), pltpu.VMEM((2, rows, tk), F32),
                        pltpu.VMEM((2, rows, tk), BF16), pltpu.VMEM((2, rows, LANES), F32),
                        pltpu.VMEM((2, rows, LANES), F32), pltpu.VMEM((2, rows, LANES), F32)])
    return pl.pallas_call(
        kern,
        grid_spec=grid_spec,
        out_shape=jax.ShapeDtypeStruct((b, s, NSA_WIDTH), BF16),
        compiler_params=_params(("parallel", "parallel", "parallel")),
        name="sel_win_attention",
    )(tiles, n_tiles, r3(q), r3(ks), r3(vs), r3(kw), r3(vw), r3(gl), gexp, ocmp)


def _hgrn_chunks(raw, onorm, layer):
    c = HG_CHUNK
    ex = jnp.exp(raw - jnp.max(raw, axis=0, keepdims=True))
    sm = ex / jnp.sum(ex, axis=0, keepdims=True)
    lb_all = jnp.zeros((1, raw.shape[1]), F32)
    for l in range(1, layer + 1):
        lb_all = lb_all + sm[l:l + 1, :]

    t_idx = lax.broadcasted_iota(jnp.int32, (c, HG_DK), 0)
    sub = lax.broadcasted_iota(jnp.int32, (SUBLANES, HG_DK), 0)
    ti = lax.broadcasted_iota(jnp.int32, (c, c), 0)
    si = lax.broadcasted_iota(jnp.int32, (c, c), 1)
    tril = jnp.where(si <= ti, 1.0, 0.0).astype(BF16)
    levels = (32, 16, 8, 4, 2, 1)
    hc = HG_HEADS * c
    tb = lax.broadcasted_iota(jnp.int32, (hc, hc), 0)
    sb = lax.broadcasted_iota(jnp.int32, (hc, hc), 1)
    same_head = (tb // c) == (sb // c)
    split_bit = tb ^ sb
    pair_mask = {m: jnp.where(same_head & (tb > sb) & (split_bit >= m) & (split_bit < 2 * m), 1.0, 0.0)
                 for m in levels}
    upper_groups = {m: [h * c + s0 + m + d for h in range(HG_HEADS) for s0 in range(0, c, 2 * m)
                        for d in range(0, m, SUBLANES)] for m in levels if m % SUBLANES == 0}
    upper_mask = {m: jnp.concatenate([pair_mask[m][g:g + SUBLANES] for g in groups], axis=0)
                  for m, groups in upper_groups.items()}
    diagonal = tb == sb
    second_half ={m: (t_idx & m) != 0 for m in levels}
    sign = {m: jnp.where(second_half[m], 1.0, -1.0) for m in levels}

    def ref_rows(b, m):
        row = lambda r, n: jnp.broadcast_to(b[r:r + 1, :], (n, HG_DK))
        if m >= 4:
            return jnp.concatenate([row(s0 + m - 1, 2 * m) for s0 in range(0, c, 2 * m)], axis=0)
        return jnp.concatenate([jnp.where(sub < 4, row(s0 + 1, SUBLANES), row(s0 + 5, SUBLANES))
                                for s0 in range(0, c, SUBLANES)], axis=0)

    part = lambda y_ref, rows, which, h: y_ref[rows, which * HG_WIDTH + h * HG_DK:which * HG_WIDTH + (h + 1) * HG_DK]

    def prepare(y_ref, rows):
        q, k, f, v, logf = [], [], [], [], []
        for h in range(HG_HEADS):
            lb = lb_all[:, h * HG_DK:(h + 1) * HG_DK]
            z = part(y_ref, rows, 1, h)
            hq = part(y_ref, rows, 0, h)
            q.append(hq * _sigmoid(hq))
            ez = jnp.exp(-jnp.abs(z))
            big = 1.0 / (1.0 + ez)
            small = ez * big
            f.append(jnp.maximum(lb + (1.0 - lb) * jnp.where(z >= 0.0, big, small), F_FLOOR))
            logf.append(jnp.log2(f[h]))
            k.append((1.0 - lb) * jnp.where(z >= 0.0, small, big))
            v.append(part(y_ref, rows, 2, h).astype(BF16))
        parts = jnp.concatenate([p for h in range(HG_HEADS) for p in _split3(logf[h])], axis=1)
        csum = _nn(tril, parts)
        bcum = [sum(csum[:, (3 * h + i) * HG_DK:(3 * h + i + 1) * HG_DK] for i in range(3))
                for h in range(HG_HEADS)]
        return q, k, f, v, bcum

    def finish(prepared, y_ref, rows, st_scr, fill):
        q, k, f, v, bcum = prepared
        rowdot = jnp.concatenate([jnp.sum(q[h] * k[h], axis=-1, keepdims=True) for h in range(HG_HEADS)], axis=0)
        a = jnp.where(diagonal, rowdot, 0.0)
        for m, emits in zip(levels, fill, strict=True):
            for emit in emits:
                emit()
            r = []
            for h in range(HG_HEADS):
                if m == 1:
                    r.append(jnp.where(second_half[m], q[h] * f[h], k[h]))
                else:
                    w = jnp.exp2((bcum[h] - ref_rows(bcum[h], m)) * sign[m])
                    r.append(jnp.where(second_half[m], q[h], k[h]) * w)
            r = jnp.concatenate(r, axis=0)
            if m in upper_groups:
                lhs = jnp.concatenate([r[g:g + SUBLANES] for g in upper_groups[m]], axis=0).astype(BF16)
                p = _nt(lhs, r.astype(BF16)) * upper_mask[m]
                parts = {g: p[n * SUBLANES:(n + 1) * SUBLANES] for n, g in enumerate(upper_groups[m])}
                a = jnp.concatenate([a[g:g + SUBLANES] + parts[g] if g in parts else a[g:g + SUBLANES]
                                     for g in range(0, hc, SUBLANES)], axis=0)
            else:
                r = r.astype(BF16)
                a = a + _nt(r, r) * pair_mask[m]
        o_intra = _nn(a.astype(BF16), jnp.concatenate(v, axis=0))

        heads = []
        for h in range(HG_HEADS):
            st = st_scr[h]
            o = o_intra[h * c:(h + 1) * c] + _nt((q[h] * jnp.exp2(bcum[h])).astype(BF16), st.astype(BF16))
            b_last = bcum[h][c - 1:c, :]
            kd = (k[h] * jnp.exp2(b_last - bcum[h])).astype(BF16)
            st_scr[h] = jnp.exp2(b_last) * st + _tn(v[h], kd)

            o = o * lax.rsqrt(jnp.mean(o * o, axis=-1, keepdims=True) + EPS) * onorm
            gate = part(y_ref, rows, 3, h)
            heads.append((o * (gate * _sigmoid(gate))).astype(BF16))
        return jnp.concatenate(heads, axis=1)

    return prepare, finish, len(levels)


def _mixer(h, l, b, s, mix_norm, w_ext, cmp_k, cmp_v, hgrn_lower_bound, hgrn_out_norm):
    q, ks, vs, kw, vw, kc_in, vc_in, gl, o_hg = _inproj(h, mix_norm, w_ext, hgrn_lower_bound, hgrn_out_norm, s, l)
    kc = _compress(kc_in.reshape(b, s, KV_W), l, *cmp_k)
    vc2 = _compress(vc_in.reshape(b, s, KV_W), l, *cmp_v)
    ocmp, picks = _cmp_attention(q, kc, vc2, b, s)
    o_nsa = _attention(q, ks, vs, kw, vw, picks, gl, ocmp, b, s)
    return o_nsa.reshape(b * s, NSA_WIDTH), o_hg


def kernel(x, ffn1_norm, ffn1_w_gu, ffn1_w_down, mix_norm, w_in, cmp_pos_k, cmp_pos_v, cmp_k_w1, cmp_k_w2, cmp_v_w1, cmp_v_w2, hgrn_lower_bound, hgrn_out_norm, w_out, ffn2_norm, ffn2_w_gu, ffn2_w_down, final_norm):
    b, s, d = x.shape
    depth = ffn1_norm.shape[0]
    ffn1_w = _ffn_weights(ffn1_w_gu, ffn1_w_down)
    ffn2_w = _ffn_weights(ffn2_w_gu, ffn2_w_down)
    w_out = w_out.astype(BF16)
    w_ext = _build_w_in(w_in)
    cmp_k = _compress_weights(cmp_pos_k, cmp_k_w1, cmp_k_w2, (0, 1))
    cmp_v = _compress_weights(cmp_pos_v, cmp_v_w1, cmp_v_w2, (0, 1, 1, 0))
    h = x.reshape(b * s, d)
    for l in range(depth):
        h = _ffn(h, l, ffn1_norm, *ffn1_w, final_norm, False)
        o_nsa, o_hg = _mixer(h, l, b, s, mix_norm, w_ext, cmp_k, cmp_v, hgrn_lower_bound, hgrn_out_norm)
        h = _ffn(h, l, ffn2_norm, *ffn2_w, final_norm, l == depth - 1, (o_nsa, o_hg, w_out))
    return h.reshape(b, s, d)
```
